```python
import jax, jax.numpy as jnp
from jax import lax
import numpy as np

D_MODEL = 1024
BATCH = 8
SEQ = 8192
DEPTH = 1

CHUNK = 64
D_MIX = D_MODEL
D_GMLP = D_MIX // 2
D_HGRN = D_MIX - D_GMLP
GMLP_HEADS = 4
GMLP_HEAD_DIM = D_GMLP // GMLP_HEADS
GMLP_BLOCK = 128
HGRN_HEADS = 4
HGRN_HEAD_DIM = D_HGRN // HGRN_HEADS
D_FF = -(-(8 * D_MODEL) // (3 * 256)) * 256
N_ADA = 6
D_IN = 2 * D_GMLP + 4 * D_HGRN
EPS = 1e-6

kernel_name = "hybrid_gmlp_hgrn2_adaln_block"


def rmsnorm(x, w):
    xf = x.astype(jnp.float32)
    y = xf * lax.rsqrt(jnp.mean(xf * xf, axis=-1, keepdims=True) + EPS)
    return (y * w.astype(jnp.float32)).astype(x.dtype)


def layernorm(x, w, b):
    xf = x.astype(jnp.float32)
    mu = jnp.mean(xf, axis=-1, keepdims=True)
    var = jnp.mean(jnp.square(xf - mu), axis=-1, keepdims=True)
    y = (xf - mu) * lax.rsqrt(var + EPS)
    return (y * w.astype(jnp.float32) + b.astype(jnp.float32)).astype(x.dtype)


def modulate(h, shift, scale):
    return h * (1 + scale[:, None, :]) + shift[:, None, :]


def gmlp_spatial_gating(u, v, w_s, b_s, ln_w, ln_b):
    bsz, seq, _ = u.shape
    nb = seq // GMLP_BLOCK
    u = jax.nn.gelu(u, approximate=False)
    v = layernorm(jax.nn.gelu(v, approximate=False), ln_w, ln_b)
    vb = v.reshape(bsz, nb, GMLP_BLOCK, GMLP_HEADS, GMLP_HEAD_DIM)
    cid = jnp.arange(GMLP_BLOCK) // CHUNK
    mask = cid[:, None] >= cid[None, :]
    ws = jnp.where(mask[None], w_s, 0).astype(v.dtype)
    mixed = jnp.einsum('hts,bnshc->bnthc', ws, vb) + b_s.T.astype(v.dtype)[None, None, :, :, None]
    return u * mixed.reshape(bsz, seq, D_GMLP)


def hgrn2_recurrence(q, f_logit, inp, g, lb, gn_w):
    dtype = q.dtype
    bsz, seq, _ = q.shape
    nc = seq // CHUNK
    qf = jax.nn.silu(q.astype(jnp.float32))
    f = lb + (1.0 - lb) * jax.nn.sigmoid(f_logit.astype(jnp.float32))
    logf = jnp.log(f)
    k = 1.0 - f
    vf = inp.astype(jnp.float32)

    def to_chunks(t):
        return t.reshape(bsz, nc, CHUNK, HGRN_HEADS, HGRN_HEAD_DIM).transpose(1, 0, 3, 2, 4)

    tri = jnp.arange(CHUNK)[:, None] >= jnp.arange(CHUNK)[None, :]

    def step(state, xs):
        qc, kc, vc, lc = xs
        b = jnp.cumsum(lc, axis=2)
        inter = jnp.einsum('bhtd,bhde->bhte', qc * jnp.exp(b), state)
        diff = b[:, :, :, None, :] - b[:, :, None, :, :]
        decay = jnp.where(tri[:, :, None], jnp.exp(jnp.minimum(diff, 0.0)), 0.0)
        attn = jnp.einsum('bhtd,bhsd,bhtsd->bhts', qc, kc, decay)
        intra = jnp.einsum('bhts,bhse->bhte', attn, vc)
        b_last = b[:, :, -1, :]
        state = state * jnp.exp(b_last)[..., None] + jnp.einsum(
            'bhsd,bhse->bhde', kc * jnp.exp(b_last[:, :, None, :] - b), vc)
        return state, inter + intra

    s0 = jnp.zeros((bsz, HGRN_HEADS, HGRN_HEAD_DIM, HGRN_HEAD_DIM), jnp.float32)
    _, o = lax.scan(step, s0, (to_chunks(qf), to_chunks(k), to_chunks(vf), to_chunks(logf)))
    o = o.transpose(1, 0, 3, 2, 4).reshape(bsz, seq, HGRN_HEADS, HGRN_HEAD_DIM)
    gate = jax.nn.silu(g.astype(jnp.float32)).reshape(bsz, seq, HGRN_HEADS, HGRN_HEAD_DIM)
    o = rmsnorm(o, gn_w) * gate
    return o.reshape(bsz, seq, D_HGRN).astype(dtype)


def _fwd_setup_inputs(seed: int = 0) -> dict:
    key = jax.random.key(seed)
    ks = jax.random.split(key, 20)
    nrm = lambda k, shape, s: jax.random.normal(k, shape, jnp.float32) * s
    return {
        "x": nrm(ks[0], (BATCH, SEQ, D_MODEL), 1.0),
        "c": nrm(ks[1], (BATCH, D_MODEL), 1.0),
        "w_ada": nrm(ks[2], (DEPTH, D_MODEL, N_ADA * D_MODEL), 0.5 * D_MODEL ** -0.5),
        "b_ada": nrm(ks[3], (DEPTH, N_ADA * D_MODEL), 0.02),
        "norm1_w": 1.0 + nrm(ks[4], (DEPTH, D_MODEL), 0.02),
        "w_in": nrm(ks[5], (DEPTH, D_MODEL, D_IN), D_MODEL ** -0.5),
        "w_s": nrm(ks[6], (DEPTH, GMLP_HEADS, GMLP_BLOCK, GMLP_BLOCK), GMLP_BLOCK ** -0.5),
        "b_s": 1.0 + nrm(ks[7], (DEPTH, GMLP_HEADS, GMLP_BLOCK), 0.02),
        "v_ln_w": 1.0 + nrm(ks[8], (DEPTH, D_GMLP), 0.02),
        "v_ln_b": nrm(ks[9], (DEPTH, D_GMLP), 0.02),
        "lower_bounds": nrm(ks[10], (DEPTH + 1, D_HGRN), 0.5),
        "gn_w": 1.0 + nrm(ks[11], (DEPTH, HGRN_HEAD_DIM), 0.02),
        "w_out": nrm(ks[12], (DEPTH, D_MIX, D_MODEL), D_MIX ** -0.5),
        "norm2_w": 1.0 + nrm(ks[13], (DEPTH, D_MODEL), 0.02),
        "w_ffn_in": nrm(ks[14], (DEPTH, D_MODEL, 2 * D_FF), D_MODEL ** -0.5),
        "w_ffn_out": nrm(ks[15], (DEPTH, D_FF, D_MODEL), D_FF ** -0.5),
        "final_norm_w": 1.0 + nrm(ks[16], (D_MODEL,), 0.02),
    }


def _fwd_reference(x, c, w_ada, b_ada, norm1_w, w_in, w_s, b_s, v_ln_w, v_ln_b,
              lower_bounds, gn_w, w_out, norm2_w, w_ffn_in, w_ffn_out, final_norm_w):
    lb_all = jnp.cumsum(jax.nn.softmax(lower_bounds.astype(jnp.float32), axis=0), axis=0)
    c_act = jax.nn.silu(c)
    split_at = [D_GMLP, 2 * D_GMLP, 2 * D_GMLP + D_HGRN,
                2 * D_GMLP + 2 * D_HGRN, 2 * D_GMLP + 3 * D_HGRN]
    for l in range(DEPTH):
        ada = (c_act @ w_ada[l] + b_ada[l]).astype(x.dtype)
        sh1, sc1, g1, sh2, sc2, g2 = jnp.split(ada, N_ADA, axis=-1)

        h = modulate(rmsnorm(x, norm1_w[l]), sh1, sc1)
        proj = h @ w_in[l]
        u, v, q, f_logit, inp, g = jnp.split(proj, split_at, axis=-1)
        y_a = gmlp_spatial_gating(u, v, w_s[l], b_s[l], v_ln_w[l], v_ln_b[l])
        y_b = hgrn2_recurrence(q, f_logit, inp, g, lb_all[l], gn_w[l])
        mix = jnp.concatenate([y_a, y_b], axis=-1) @ w_out[l]
        x = x + g1[:, None, :] * mix

        h = modulate(rmsnorm(x, norm2_w[l]), sh2, sc2)
        gate, up = jnp.split(h @ w_ffn_in[l], 2, axis=-1)
        x = x + g2[:, None, :] * ((jax.nn.silu(gate) * up) @ w_ffn_out[l])
    return rmsnorm(x, final_norm_w)


import jax as _jax
import jax.numpy as _jnp

TWIN_FORMAT = 'train_step'
FWD_PARAMS = ['x', 'c', 'w_ada', 'b_ada', 'norm1_w', 'w_in', 'w_s', 'b_s', 'v_ln_w', 'v_ln_b', 'lower_bounds', 'gn_w', 'w_out', 'norm2_w', 'w_ffn_in', 'w_ffn_out', 'final_norm_w']
TWIN_WEIGHTS = ['w_ada', 'b_ada', 'norm1_w', 'w_in', 'w_s', 'b_s', 'v_ln_w', 'v_ln_b', 'lower_bounds', 'gn_w', 'w_out', 'norm2_w', 'w_ffn_in', 'w_ffn_out', 'final_norm_w']
TWIN_DIFF_INPUT = 'x'
TWIN_INPUTS = ['x', 'c', 'w_ada', 'b_ada', 'norm1_w', 'w_in', 'w_s', 'b_s', 'v_ln_w', 'v_ln_b', 'lower_bounds', 'gn_w', 'w_out', 'norm2_w', 'w_ffn_in', 'w_ffn_out', 'final_norm_w', 'loss_target', 'm_w_ada', 'm_b_ada', 'm_norm1_w', 'm_w_in', 'm_w_s', 'm_b_s', 'm_v_ln_w', 'm_v_ln_b', 'm_lower_bounds', 'm_gn_w', 'm_w_out', 'm_norm2_w', 'm_w_ffn_in', 'm_w_ffn_out', 'm_final_norm_w', 'v_w_ada', 'v_b_ada', 'v_norm1_w', 'v_w_in', 'v_w_s', 'v_b_s', 'v_v_ln_w', 'v_v_ln_b', 'v_lower_bounds', 'v_gn_w', 'v_w_out', 'v_norm2_w', 'v_w_ffn_in', 'v_w_ffn_out', 'v_final_norm_w']
TWIN_OUTPUTS = ['loss', 'grad_x', 'grad_w_ada', 'grad_b_ada', 'grad_norm1_w', 'grad_w_in', 'grad_w_s', 'grad_b_s', 'grad_v_ln_w', 'grad_v_ln_b', 'grad_lower_bounds', 'grad_gn_w', 'grad_w_out', 'grad_norm2_w', 'grad_w_ffn_in', 'grad_w_ffn_out', 'grad_final_norm_w', 'delta_w_ada', 'delta_b_ada', 'delta_norm1_w', 'delta_w_in', 'delta_w_s', 'delta_b_s', 'delta_v_ln_w', 'delta_v_ln_b', 'delta_lower_bounds', 'delta_gn_w', 'delta_w_out', 'delta_norm2_w', 'delta_w_ffn_in', 'delta_w_ffn_out', 'delta_final_norm_w', 'new_m_w_ada', 'new_m_b_ada', 'new_m_norm1_w', 'new_m_w_in', 'new_m_w_s', 'new_m_b_s', 'new_m_v_ln_w', 'new_m_v_ln_b', 'new_m_lower_bounds', 'new_m_gn_w', 'new_m_w_out', 'new_m_norm2_w', 'new_m_w_ffn_in', 'new_m_w_ffn_out', 'new_m_final_norm_w', 'new_v_w_ada', 'new_v_b_ada', 'new_v_norm1_w', 'new_v_w_in', 'new_v_w_s', 'new_v_b_s', 'new_v_v_ln_w', 'new_v_v_ln_b', 'new_v_lower_bounds', 'new_v_gn_w', 'new_v_w_out', 'new_v_norm2_w', 'new_v_w_ffn_in', 'new_v_w_ffn_out', 'new_v_final_norm_w']
TWIN_LEAF_KINDS = {'loss': 'loss', 'grad_x': 'grad_x', 'grad_w_ada': 'grad_w', 'grad_b_ada': 'grad_w', 'grad_norm1_w': 'grad_w', 'grad_w_in': 'grad_w', 'grad_w_s': 'grad_w', 'grad_b_s': 'grad_w', 'grad_v_ln_w': 'grad_w', 'grad_v_ln_b': 'grad_w', 'grad_lower_bounds': 'grad_w', 'grad_gn_w': 'grad_w', 'grad_w_out': 'grad_w', 'grad_norm2_w': 'grad_w', 'grad_w_ffn_in': 'grad_w', 'grad_w_ffn_out': 'grad_w', 'grad_final_norm_w': 'grad_w', 'delta_w_ada': 'delta_w', 'delta_b_ada': 'delta_w', 'delta_norm1_w': 'delta_w', 'delta_w_in': 'delta_w', 'delta_w_s': 'delta_w', 'delta_b_s': 'delta_w', 'delta_v_ln_w': 'delta_w', 'delta_v_ln_b': 'delta_w', 'delta_lower_bounds': 'delta_w', 'delta_gn_w': 'delta_w', 'delta_w_out': 'delta_w', 'delta_norm2_w': 'delta_w', 'delta_w_ffn_in': 'delta_w', 'delta_w_ffn_out': 'delta_w', 'delta_final_norm_w': 'delta_w', 'new_m_w_ada': 'new_m', 'new_m_b_ada': 'new_m', 'new_m_norm1_w': 'new_m', 'new_m_w_in': 'new_m', 'new_m_w_s': 'new_m', 'new_m_b_s': 'new_m', 'new_m_v_ln_w': 'new_m', 'new_m_v_ln_b': 'new_m', 'new_m_lower_bounds': 'new_m', 'new_m_gn_w': 'new_m', 'new_m_w_out': 'new_m', 'new_m_norm2_w': 'new_m', 'new_m_w_ffn_in': 'new_m', 'new_m_w_ffn_out': 'new_m', 'new_m_final_norm_w': 'new_m', 'new_v_w_ada': 'new_v', 'new_v_b_ada': 'new_v', 'new_v_norm1_w': 'new_v', 'new_v_w_in': 'new_v', 'new_v_w_s': 'new_v', 'new_v_b_s': 'new_v', 'new_v_v_ln_w': 'new_v', 'new_v_v_ln_b': 'new_v', 'new_v_lower_bounds': 'new_v', 'new_v_gn_w': 'new_v', 'new_v_w_out': 'new_v', 'new_v_norm2_w': 'new_v', 'new_v_w_ffn_in': 'new_v', 'new_v_w_ffn_out': 'new_v', 'new_v_final_norm_w': 'new_v'}


def _forward(args):
    return _fwd_reference(*[args[k] for k in FWD_PARAMS])


def _output_shape():
    def fwd():
        inp = _fwd_setup_inputs(0)
        return _fwd_reference(*[inp[k] for k in FWD_PARAMS])
    out = _jax.eval_shape(fwd)
    return out.shape, out.dtype

N_MICROBATCH = 1
ADAM_LR = 0.001
ADAM_B1 = 0.9
ADAM_B2 = 0.999
ADAM_EPS = 1e-08
ADAM_WD = 0.01
ADAM_STEP = 10
PER_EXAMPLE_BATCH_AXIS = {'x': 0, 'c': 0, 'loss_target': 0}
SHARED_INPUTS = []
_WEIGHT_DTYPES = {'w_ada': _jnp.float32, 'b_ada': _jnp.float32, 'norm1_w': _jnp.float32, 'w_in': _jnp.float32, 'w_s': _jnp.float32, 'b_s': _jnp.float32, 'v_ln_w': _jnp.float32, 'v_ln_b': _jnp.float32, 'lower_bounds': _jnp.float32, 'gn_w': _jnp.float32, 'w_out': _jnp.float32, 'norm2_w': _jnp.float32, 'w_ffn_in': _jnp.float32, 'w_ffn_out': _jnp.float32, 'final_norm_w': _jnp.float32}
MOMENT_SCALE = {'w_ada': 8.084327e-02, 'b_ada': 1.417339e-01, 'norm1_w': 7.904185e-02, 'w_in': 4.743729e-02, 'w_s': 4.711275e-02, 'b_s': 6.343617e-02, 'v_ln_w': 4.805302e-02, 'v_ln_b': 5.833660e-02, 'lower_bounds': 3.892622e-03, 'gn_w': 1.098437e-01, 'w_out': 6.166740e-02, 'norm2_w': 7.553289e-02, 'w_ffn_in': 3.217091e-02, 'w_ffn_out': 5.249439e-02, 'final_norm_w': 6.410971e+01}


def _to_microbatches(a, axis):
    t = _jnp.moveaxis(a, axis, 0)
    t = t.reshape((N_MICROBATCH, t.shape[0] // N_MICROBATCH) + t.shape[1:])
    return _jnp.moveaxis(t, 1, axis + 1)


def setup_inputs(seed: int = 0) -> dict:
    inp = _fwd_setup_inputs(seed)
    key = _jax.random.fold_in(_jax.random.key(seed), 7919)
    shape, _ = _output_shape()
    out = dict(inp)
    out["loss_target"] = _jax.random.normal(_jax.random.fold_in(key, 0), shape, _jnp.float32)
    for i, name in enumerate(TWIN_WEIGHTS):
        w = inp[name].astype(_jnp.float32)
        if MOMENT_SCALE is None:
            s = _jnp.sqrt(_jnp.mean(_jnp.square(w)) + 1e-30)
        else:
            s = MOMENT_SCALE[name]
        km, kv = _jax.random.split(_jax.random.fold_in(key, i + 1))
        out[name] = w
        out["m_" + name] = s * _jax.random.normal(km, w.shape, _jnp.float32)
        out["v_" + name] = (s * s) * _jax.random.uniform(kv, w.shape, _jnp.float32, 0.5, 1.5)
    if N_MICROBATCH > 1:
        for name, axis in PER_EXAMPLE_BATCH_AXIS.items():
            out[name] = _to_microbatches(out[name], axis)
    return {'x': out['x'], 'c': out['c'], 'w_ada': out['w_ada'], 'b_ada': out['b_ada'], 'norm1_w': out['norm1_w'], 'w_in': out['w_in'], 'w_s': out['w_s'], 'b_s': out['b_s'], 'v_ln_w': out['v_ln_w'], 'v_ln_b': out['v_ln_b'], 'lower_bounds': out['lower_bounds'], 'gn_w': out['gn_w'], 'w_out': out['w_out'], 'norm2_w': out['norm2_w'], 'w_ffn_in': out['w_ffn_in'], 'w_ffn_out': out['w_ffn_out'], 'final_norm_w': out['final_norm_w'], 'loss_target': out['loss_target'], 'm_w_ada': out['m_w_ada'], 'm_b_ada': out['m_b_ada'], 'm_norm1_w': out['m_norm1_w'], 'm_w_in': out['m_w_in'], 'm_w_s': out['m_w_s'], 'm_b_s': out['m_b_s'], 'm_v_ln_w': out['m_v_ln_w'], 'm_v_ln_b': out['m_v_ln_b'], 'm_lower_bounds': out['m_lower_bounds'], 'm_gn_w': out['m_gn_w'], 'm_w_out': out['m_w_out'], 'm_norm2_w': out['m_norm2_w'], 'm_w_ffn_in': out['m_w_ffn_in'], 'm_w_ffn_out': out['m_w_ffn_out'], 'm_final_norm_w': out['m_final_norm_w'], 'v_w_ada': out['v_w_ada'], 'v_b_ada': out['v_b_ada'], 'v_norm1_w': out['v_norm1_w'], 'v_w_in': out['v_w_in'], 'v_w_s': out['v_w_s'], 'v_b_s': out['v_b_s'], 'v_v_ln_w': out['v_v_ln_w'], 'v_v_ln_b': out['v_v_ln_b'], 'v_lower_bounds': out['v_lower_bounds'], 'v_gn_w': out['v_gn_w'], 'v_w_out': out['v_w_out'], 'v_norm2_w': out['v_norm2_w'], 'v_w_ffn_in': out['v_w_ffn_in'], 'v_w_ffn_out': out['v_w_ffn_out'], 'v_final_norm_w': out['v_final_norm_w']}


def _loss(weights, diff, rest, loss_target):
    with _jax.named_scope("forward"):
        args = {**rest, TWIN_DIFF_INPUT: diff, **{k: w.astype(_WEIGHT_DTYPES[k]) for k, w in weights.items()}}
        y = _forward(args)
    with _jax.named_scope("loss_head"):
        err = _jnp.square(y.astype(_jnp.float32) - loss_target)
        return 0.5 * _jnp.sum(_jnp.mean(err, axis=-1)) if err.ndim else 0.5 * err


def _adamw(w, g, m, v):
    m = ADAM_B1 * m + (1.0 - ADAM_B1) * g
    v = ADAM_B2 * v + (1.0 - ADAM_B2) * _jnp.square(g)
    m_hat = m / (1.0 - ADAM_B1 ** ADAM_STEP)
    v_hat = v / (1.0 - ADAM_B2 ** ADAM_STEP)
    delta = -ADAM_LR * (m_hat / (_jnp.sqrt(v_hat) + ADAM_EPS) + ADAM_WD * w)
    return delta, m, v


def reference(x, c, w_ada, b_ada, norm1_w, w_in, w_s, b_s, v_ln_w, v_ln_b, lower_bounds, gn_w, w_out, norm2_w, w_ffn_in, w_ffn_out, final_norm_w, loss_target, m_w_ada, m_b_ada, m_norm1_w, m_w_in, m_w_s, m_b_s, m_v_ln_w, m_v_ln_b, m_lower_bounds, m_gn_w, m_w_out, m_norm2_w, m_w_ffn_in, m_w_ffn_out, m_final_norm_w, v_w_ada, v_b_ada, v_norm1_w, v_w_in, v_w_s, v_b_s, v_v_ln_w, v_v_ln_b, v_lower_bounds, v_gn_w, v_w_out, v_norm2_w, v_w_ffn_in, v_w_ffn_out, v_final_norm_w):
    given = dict(x=x, c=c, w_ada=w_ada, b_ada=b_ada, norm1_w=norm1_w, w_in=w_in, w_s=w_s, b_s=b_s, v_ln_w=v_ln_w, v_ln_b=v_ln_b, lower_bounds=lower_bounds, gn_w=gn_w, w_out=w_out, norm2_w=norm2_w, w_ffn_in=w_ffn_in, w_ffn_out=w_ffn_out, final_norm_w=final_norm_w, loss_target=loss_target, m_w_ada=m_w_ada, m_b_ada=m_b_ada, m_norm1_w=m_norm1_w, m_w_in=m_w_in, m_w_s=m_w_s, m_b_s=m_b_s, m_v_ln_w=m_v_ln_w, m_v_ln_b=m_v_ln_b, m_lower_bounds=m_lower_bounds, m_gn_w=m_gn_w, m_w_out=m_w_out, m_norm2_w=m_norm2_w, m_w_ffn_in=m_w_ffn_in, m_w_ffn_out=m_w_ffn_out, m_final_norm_w=m_final_norm_w, v_w_ada=v_w_ada, v_b_ada=v_b_ada, v_norm1_w=v_norm1_w, v_w_in=v_w_in, v_w_s=v_w_s, v_b_s=v_b_s, v_v_ln_w=v_v_ln_w, v_v_ln_b=v_v_ln_b, v_lower_bounds=v_lower_bounds, v_gn_w=v_gn_w, v_w_out=v_w_out, v_norm2_w=v_norm2_w, v_w_ffn_in=v_w_ffn_in, v_w_ffn_out=v_w_ffn_out, v_final_norm_w=v_final_norm_w)
    weights = {n: given[n] for n in TWIN_WEIGHTS}
    shared = {n: given[n] for n in SHARED_INPUTS}
    per_example = {n: given[n] for n in ['x', 'c']}
    grad_fn = _jax.value_and_grad(_loss, argnums=(0, 1))

    def one_microbatch(ex, loss_target):
        ex = dict(ex)
        diff = ex.pop(TWIN_DIFF_INPUT)
        return grad_fn(weights, diff, {**shared, **ex}, loss_target)

    if N_MICROBATCH == 1:
        loss, (grad_w, grad_x) = one_microbatch(per_example, given["loss_target"])
    else:
        def body(carry, xs):
            loss_sum, grad_sum = carry
            l_k, (gw_k, gx_k) = one_microbatch(xs[0], xs[1])
            with _jax.named_scope("update"):
                return (loss_sum + l_k, _jax.tree.map(_jnp.add, grad_sum, gw_k)), gx_k

        init = (_jnp.zeros((), _jnp.float32), _jax.tree.map(_jnp.zeros_like, weights))
        (loss, grad_w), grad_x = _jax.lax.scan(body, init, (per_example, given["loss_target"]))
    with _jax.named_scope("update"):
        delta_w, new_m, new_v = {}, {}, {}
        for n in TWIN_WEIGHTS:
            delta_w[n], new_m[n], new_v[n] = _adamw(weights[n], grad_w[n], given["m_" + n], given["v_" + n])
    return (loss, grad_x, *[grad_w[n] for n in TWIN_WEIGHTS], *[delta_w[n] for n in TWIN_WEIGHTS],
            *[new_m[n] for n in TWIN_WEIGHTS], *[new_v[n] for n in TWIN_WEIGHTS])
```

```python
import functools

import jax
import jax.numpy as jnp
from jax import lax
from jax.experimental import pallas as pl
from jax.experimental.pallas import tpu as pltpu

F32 = jnp.float32
BF16 = jnp.bfloat16
SDS = jax.ShapeDtypeStruct
MESH = pl.DeviceIdType.MESH
HIGHEST = lax.Precision.HIGHEST

D = 1024
DG = 512
DH = 512
NH = 4
HD = 128
BLK = 128
CH = 64
DFF = 2816
DIN = 3072
FFB = 1408
N_CHIPS = 4
N_DEV = 8
EPS = 1e-6
LR, B1, B2, AEPS, WD, STEP = 0.001, 0.9, 0.999, 1e-08, 0.01, 10
SMALL_ROWS = 80

NT = (((1,), (1,)), ((), ()))
TN = (((0,), (0,)), ((), ()))


def _full(shape):
    nd = len(shape)
    return pl.BlockSpec(shape, lambda *_: (0,) * nd)


def _resident(shape):
    nd = len(shape)
    return pl.BlockSpec(shape, lambda *_: (0,) * nd, pipeline_mode=pl.Buffered(1))


def _arb(n=1):
    return pltpu.CompilerParams(dimension_semantics=("arbitrary",) * n)


def _dot(a, b, dims=None, precision=None):
    if dims is None:
        return jnp.dot(a, b, preferred_element_type=F32, precision=precision)
    return lax.dot_general(a, b, dims, preferred_element_type=F32, precision=precision)


def _sigmoid(x):
    return jax.nn.sigmoid(x)


def _gelu_parts(x):
    cdf = 0.5 * (1.0 + lax.erf(x * 0.7071067811865476))
    pdf = jnp.exp(-0.5 * x * x) * 0.3989422804014327
    return x * cdf, cdf + x * pdf


def _rms(x):
    return lax.rsqrt(jnp.mean(x * x, axis=-1, keepdims=True) + EPS)


def _rms_bwd(xhat, r, gw):
    return r * (gw - xhat * jnp.mean(xhat * gw, axis=-1, keepdims=True))


def _lower_bound(lbp_ref):
    l0, l1 = lbp_ref[0:1, :], lbp_ref[1:2, :]
    m = jnp.maximum(l0, l1)
    e0, e1 = jnp.exp(l0 - m), jnp.exp(l1 - m)
    return e0 / (e0 + e1), e1 / (e0 + e1)


def _proj_in(x, nw, sc, sh, w_in_b, tm):
    T = x.shape[0]

    def body(x_ref, nw_ref, sc_ref, sh_ref, w_ref, h_ref, p_ref):
        xv = x_ref[...]
        h = ((xv * _rms(xv)) * nw_ref[...]) * (1.0 + sc_ref[...]) + sh_ref[...]
        hb = h.astype(BF16)
        h_ref[...] = hb
        p_ref[...] = _dot(hb, w_ref[...])

    row = lambda i: (i, 0)
    return pl.pallas_call(
        body, grid=(T // tm,),
        in_specs=[pl.BlockSpec((tm, D), row), _full((1, D)), _full((1, D)), _full((1, D)), _resident((D, DIN))],
        out_specs=[pl.BlockSpec((tm, D), row), pl.BlockSpec((tm, DIN), row)],
        out_shape=[SDS((T, D), BF16), SDS((T, DIN), F32)],
        compiler_params=_arb(), name="proj_in")(x, nw, sc, sh, w_in_b)


def _gmlp_common(u, v, lnw, lnb, ws_ref, bst_ref):
    ug, dug = _gelu_parts(u)
    vg, dvg = _gelu_parts(v)
    mu = jnp.mean(vg, axis=-1, keepdims=True)
    vc = vg - mu
    rstd = lax.rsqrt(jnp.mean(vc * vc, axis=-1, keepdims=True) + EPS)
    vhat = vc * rstd
    vn = vhat * lnw + lnb
    vnb = vn.astype(BF16)
    mixed = []
    for h in range(NH):
        sl = slice(h * HD, (h + 1) * HD)
        mixed.append(_dot(ws_ref[h], vnb[:, sl]) + bst_ref[:, h:h + 1])
    return ug, dug, dvg, rstd, vhat, vnb, jnp.concatenate(mixed, axis=1)


def _gmlp_fwd(proj, ws_b, bst, lnw, lnb):
    T = proj.shape[0]

    def body(u_ref, v_ref, ws_ref, bst_ref, lnw_ref, lnb_ref, y_ref):
        ug, _, _, _, _, _, mixed = _gmlp_common(u_ref[...], v_ref[...], lnw_ref[...], lnb_ref[...], ws_ref, bst_ref)
        y_ref[...] = (ug * mixed).astype(BF16)

    return pl.pallas_call(
        body, grid=(T // BLK,),
        in_specs=[pl.BlockSpec((BLK, DG), lambda i: (i, 0)), pl.BlockSpec((BLK, DG), lambda i: (i, 1)),
                  _full((NH, BLK, BLK)), _full((BLK, NH)), _full((1, DG)), _full((1, DG))],
        out_specs=pl.BlockSpec((BLK, DG), lambda i: (i, 0)),
        out_shape=SDS((T, D), BF16),
        compiler_params=_arb(), name="gmlp_fwd")(proj, proj, ws_b, bst, lnw, lnb)


def _hgrn_prep(q, fl, lb, omlb):
    sq = _sigmoid(q)
    qf = q * sq
    sig = _sigmoid(fl)
    f = lb + omlb * sig
    lf = jnp.log(f)
    k = 1.0 - f
    r = lax.broadcasted_iota(jnp.int32, (CH, CH), 0)
    c = lax.broadcasted_iota(jnp.int32, (CH, CH), 1)
    tri = (r >= c).astype(F32)
    b = _dot(tri, lf, precision=HIGHEST)
    return sq, qf, sig, f, k, b


def _hgrn_fwd(proj, lower_bounds, gn_w, ycat):
    T = proj.shape[0]
    nc = T // CH

    def body(q_ref, f_ref, i_ref, g_ref, lbp_ref, gn_ref, ycat_any, y_ref, o_ref, at_ref, st_ref, s_scr, b_scr, q_scr):
        @pl.when(pl.program_id(0) == 0)
        def _():
            s_scr[...] = jnp.zeros_like(s_scr)

        lb, omlb = _lower_bound(lbp_ref)
        v = i_ref[...]
        g = g_ref[...]
        _, qf, _, _, k, b = _hgrn_prep(q_ref[...], f_ref[...], lb, omlb)
        eb = jnp.exp(b)
        bl = b[CH - 1:CH, :]
        ebl = jnp.exp(bl)
        kd = k * jnp.exp(bl - b)
        rows = lax.broadcasted_iota(jnp.int32, (CH, 1), 0)
        lanes = lax.broadcasted_iota(jnp.int32, (CH, CH), 1)
        for h in range(NH):
            sl = slice(h * HD, (h + 1) * HD)
            b_scr[h] = b[:, sl]
            q_scr[h] = qf[:, sl]
            st0 = s_scr[h]
            st_ref[0, h] = st0
            qe = (qf[:, sl] * eb[:, sl]).astype(BF16)
            inter = _dot(qe, st0.astype(BF16), NT)
            bh, kh = b[:, sl], k[:, sl]

            def step(t, at):
                bt = b_scr[h, pl.ds(t, 1), :]
                qt = q_scr[h, pl.ds(t, 1), :]
                dm = jnp.exp(jnp.minimum(bt - bh, 0.0))
                col = jnp.sum(dm * kh * qt, axis=-1, keepdims=True)
                col = jnp.where(rows <= t, col, 0.0)
                return jnp.where(lanes == t, col, at)

            at = lax.fori_loop(0, CH, step, jnp.zeros((CH, CH), F32))
            at_ref[0, h] = at
            vb = v[:, sl].astype(BF16)
            o = inter + _dot(at.astype(BF16), vb, TN)
            s_scr[h] = st0 * ebl[:, sl] + _dot(vb, kd[:, sl].astype(BF16), TN)
            o_ref[:, sl] = o
            gh = g[:, sl]
            y_ref[:, sl] = (((o * _rms(o)) * gn_ref[...]) * (gh * _sigmoid(gh))).astype(BF16)

    blk = lambda j: pl.BlockSpec((CH, DH), lambda c: (c, j))
    return pl.pallas_call(
        body, grid=(nc,),
        in_specs=[blk(2), blk(3), blk(4), blk(5), _full((2, DH)), _full((1, HD)),
                  pl.BlockSpec(memory_space=pl.ANY)],
        out_specs=[pl.BlockSpec((CH, DH), lambda c: (c, 1)),
                   pl.BlockSpec((CH, DH), lambda c: (c, 0)),
                   pl.BlockSpec((1, NH, CH, CH), lambda c: (c, 0, 0, 0)),
                   pl.BlockSpec((1, NH, HD, HD), lambda c: (c, 0, 0, 0))],
        out_shape=[SDS((T, D), BF16), SDS((T, DH), F32), SDS((nc, NH, CH, CH), F32), SDS((nc, NH, HD, HD), F32)],
        scratch_shapes=[pltpu.VMEM((NH, HD, HD), F32), pltpu.VMEM((NH, CH, HD), F32), pltpu.VMEM((NH, CH, HD), F32)],
        input_output_aliases={6: 0},
        compiler_params=_arb(), name="hgrn_fwd")(proj, proj, proj, proj, lower_bounds, gn_w, ycat)


def _token_local(x, ycat, tgt, g1, nw2, sc2, sh2, g2, fw, w_out_b, w_fi_b, w_fo_b, tm):
    T = x.shape[0]
    inv_d = 1.0 / D

    def body(x_ref, y_ref, t_ref, g1_ref, nw2_ref, sc2_ref, sh2_ref, g2_ref, fw_ref, wo_ref, wfi_ref, wfo_ref,
             dy_ref, dx1_ref, h2_ref, act_ref, dff_ref, dgu_ref, dmix_ref, acc_ref):
        @pl.when(pl.program_id(0) == 0)
        def _():
            acc_ref[...] = jnp.zeros_like(acc_ref)

        def acc(row, val):
            acc_ref[row:row + 1, :] += jnp.sum(val, axis=0, keepdims=True)

        g1v, g2v = g1_ref[...], g2_ref[...]
        mix = _dot(y_ref[...], wo_ref[...])
        x1 = x_ref[...] + g1v * mix
        r2 = _rms(x1)
        xh2 = x1 * r2
        n2 = xh2 * nw2_ref[...]
        osc2 = 1.0 + sc2_ref[...]
        h2b = (n2 * osc2 + sh2_ref[...]).astype(BF16)
        h2_ref[...] = h2b
        ff = jnp.zeros((tm, D), F32)
        saved = []
        for kb in range(DFF // FFB):
            gate = _dot(h2b, wfi_ref[:, kb * FFB:(kb + 1) * FFB])
            up = _dot(h2b, wfi_ref[:, DFF + kb * FFB:DFF + (kb + 1) * FFB])
            sg = _sigmoid(gate)
            actb = (gate * sg * up).astype(BF16)
            act_ref[:, kb * FFB:(kb + 1) * FFB] = actb
            ff = ff + _dot(actb, wfo_ref[kb * FFB:(kb + 1) * FFB, :])
            saved.append((gate, up, sg))
        x2 = x1 + g2v * ff
        r3 = _rms(x2)
        xh3 = x2 * r3
        err = xh3 * fw_ref[...] - t_ref[...]
        acc(6, (0.5 * inv_d) * err * err)
        dy = err * inv_d
        acc(4, dy * xh3)
        dx2 = _rms_bwd(xh3, r3, dy * fw_ref[...])
        acc(0, dx2 * ff)
        dffb = (dx2 * g2v).astype(BF16)
        dff_ref[...] = dffb
        dh2 = jnp.zeros((tm, D), F32)
        for kb in range(DFF // FFB):
            gate, up, sg = saved[kb]
            da = _dot(dffb, wfo_ref[kb * FFB:(kb + 1) * FFB, :], NT)
            dgate = (da * up * (sg * (1.0 + gate * (1.0 - sg)))).astype(BF16)
            dup = (da * gate * sg).astype(BF16)
            dgu_ref[:, kb * FFB:(kb + 1) * FFB] = dgate
            dgu_ref[:, DFF + kb * FFB:DFF + (kb + 1) * FFB] = dup
            dh2 = dh2 + _dot(dgate, wfi_ref[:, kb * FFB:(kb + 1) * FFB], NT)
            dh2 = dh2 + _dot(dup, wfi_ref[:, DFF + kb * FFB:DFF + (kb + 1) * FFB], NT)
        acc(2, dh2)
        acc(1, dh2 * n2)
        dn2 = dh2 * osc2
        acc(3, dn2 * xh2)
        dx1 = dx2 + _rms_bwd(xh2, r2, dn2 * nw2_ref[...])
        acc(5, dx1 * mix)
        dmixb = (dx1 * g1v).astype(BF16)
        dmix_ref[...] = dmixb
        dy_ref[...] = _dot(dmixb, wo_ref[...], NT)
        dx1_ref[...] = dx1

    row = lambda i: (i, 0)
    vec = _full((1, D))
    return pl.pallas_call(
        body, grid=(T // tm,),
        in_specs=[pl.BlockSpec((tm, D), row), pl.BlockSpec((tm, D), row), pl.BlockSpec((tm, D), row),
                  vec, vec, vec, vec, vec, vec,
                  _resident((D, D)), _resident((D, 2 * DFF)), _resident((DFF, D))],
        out_specs=[pl.BlockSpec((tm, D), row), pl.BlockSpec((tm, D), row), pl.BlockSpec((tm, D), row),
                   pl.BlockSpec((tm, DFF), row), pl.BlockSpec((tm, D), row), pl.BlockSpec((tm, 2 * DFF), row),
                   pl.BlockSpec((tm, D), row), _full((8, D))],
        out_shape=[SDS((T, D), F32), SDS((T, D), F32), SDS((T, D), BF16), SDS((T, DFF), BF16), SDS((T, D), BF16),
                   SDS((T, 2 * DFF), BF16), SDS((T, D), BF16), SDS((8, D), F32)],
        compiler_params=_arb(), name="token_local")(x, ycat, tgt, g1, nw2, sc2, sh2, g2, fw, w_out_b, w_fi_b, w_fo_b)


def _gmlp_bwd(proj, dycat, ws_b, bst, lnw, lnb):
    T = proj.shape[0]
    nb = T // BLK

    def body(u_ref, v_ref, dy_ref, ws_ref, bst_ref, lnw_ref, lnb_ref, dp_ref, dws_ref, dbs_ref, dln_ref, dbs_acc):
        i = pl.program_id(0)

        @pl.when(i == 0)
        def _():
            dws_ref[...] = jnp.zeros_like(dws_ref)
            dln_ref[...] = jnp.zeros_like(dln_ref)
            dbs_acc[...] = jnp.zeros_like(dbs_acc)

        ug, dug, dvg, rstd, vhat, vnb, mixed = _gmlp_common(u_ref[...], v_ref[...], lnw_ref[...], lnb_ref[...], ws_ref, bst_ref)
        dya = dy_ref[...]
        dp_ref[:, 0:DG] = (dya * mixed * dug).astype(BF16)
        dmixed = dya * ug
        dbs_acc[...] += dmixed
        dmb = dmixed.astype(BF16)
        r = lax.broadcasted_iota(jnp.int32, (BLK, BLK), 0) // CH
        c = lax.broadcasted_iota(jnp.int32, (BLK, BLK), 1) // CH
        dvn = []
        for h in range(NH):
            sl = slice(h * HD, (h + 1) * HD)
            dws_ref[h] += jnp.where(r >= c, _dot(dmb[:, sl], vnb[:, sl], NT), 0.0)
            dvn.append(_dot(ws_ref[h], dmb[:, sl], TN))
        dvn = jnp.concatenate(dvn, axis=1)
        dln_ref[0:1, :] += jnp.sum(dvn * vhat, axis=0, keepdims=True)
        dln_ref[1:2, :] += jnp.sum(dvn, axis=0, keepdims=True)
        dvh = dvn * lnw_ref[...]
        dvgel = rstd * (dvh - jnp.mean(dvh, axis=-1, keepdims=True) - vhat * jnp.mean(dvh * vhat, axis=-1, keepdims=True))
        dp_ref[:, DG:2 * DG] = (dvgel * dvg).astype(BF16)

        @pl.when(i == nb - 1)
        def _():
            lane = lax.broadcasted_iota(jnp.int32, (BLK, HD), 1)
            out = jnp.zeros((BLK, HD), F32)
            for h in range(NH):
                out = out + jnp.where(lane == h, jnp.sum(dbs_acc[:, h * HD:(h + 1) * HD], axis=-1, keepdims=True), 0.0)
            dbs_ref[...] = out

    return pl.pallas_call(
        body, grid=(nb,),
        in_specs=[pl.BlockSpec((BLK, DG), lambda i: (i, 0)), pl.BlockSpec((BLK, DG), lambda i: (i, 1)),
                  pl.BlockSpec((BLK, DG), lambda i: (i, 0)),
                  _full((NH, BLK, BLK)), _full((BLK, NH)), _full((1, DG)), _full((1, DG))],
        out_specs=[pl.BlockSpec((BLK, 2 * DG), lambda i: (i, 2)), _full((NH, BLK, BLK)), _full((BLK, HD)), _full((8, DG))],
        out_shape=[SDS((T, DIN), BF16), SDS((NH, BLK, BLK), F32), SDS((BLK, HD), F32), SDS((8, DG), F32)],
        scratch_shapes=[pltpu.VMEM((BLK, DG), F32)],
        compiler_params=_arb(), name="gmlp_bwd")(proj, proj, dycat, ws_b, bst, lnw, lnb)


def _hgrn_bwd(proj, o_pre, at_all, st_all, dycat, lower_bounds, gn_w, dproj):
    T = proj.shape[0]
    nc = T // CH

    def body(q_ref, f_ref, i_ref, g_ref, o_ref, at_ref, st_ref, dy_ref, lbp_ref, gn_ref, dp_any,
             dp_ref, dlb_ref, dgn_ref, ds_scr, b_scr, q_scr, dqi_scr):
        i = pl.program_id(0)

        @pl.when(i == 0)
        def _():
            ds_scr[...] = jnp.zeros_like(ds_scr)
            dlb_ref[...] = jnp.zeros_like(dlb_ref)
            dgn_ref[...] = jnp.zeros_like(dgn_ref)

        lb, omlb = _lower_bound(lbp_ref)
        q = q_ref[...]
        v = i_ref[...]
        g = g_ref[...]
        sq, qf, sig, f, k, b = _hgrn_prep(q, f_ref[...], lb, omlb)
        eb = jnp.exp(b)
        bl = b[CH - 1:CH, :]
        ebl = jnp.exp(bl)
        ekd = jnp.exp(bl - b)
        kd = k * ekd
        rows = lax.broadcasted_iota(jnp.int32, (CH, CH), 0)
        lanes = lax.broadcasted_iota(jnp.int32, (CH, CH), 1)
        row1 = lax.broadcasted_iota(jnp.int32, (CH, 1), 0)
        upper = (lanes >= rows).astype(F32)
        dgn = jnp.zeros((1, HD), F32)
        for h in range(NH):
            sl = slice(h * HD, (h + 1) * HD)
            b_scr[h] = b[:, sl]
            q_scr[h] = qf[:, sl]
            o = o_ref[:, sl]
            ro = _rms(o)
            oh = o * ro
            gh = g[:, sl]
            sg = _sigmoid(gh)
            dyb = dy_ref[:, sl]
            dgate = dyb * (oh * gn_ref[...])
            dg = dgate * (sg * (1.0 + gh * (1.0 - sg)))
            don = dyb * (gh * sg)
            dgn = dgn + jnp.sum(don * oh, axis=0, keepdims=True)
            do = _rms_bwd(oh, ro, don * gn_ref[...])
            dob = do.astype(BF16)
            vb = v[:, sl].astype(BF16)
            st0 = st_ref[0, h]
            dst1 = ds_scr[h]
            dst1b = dst1.astype(BF16)
            qfh, kh, bh, ebh, kdh = qf[:, sl], k[:, sl], b[:, sl], eb[:, sl], kd[:, sl]
            qe = qfh * ebh
            dqe = _dot(dob, st0.astype(BF16))
            ds_scr[h] = dst1 * ebl[:, sl] + _dot(dob, qe.astype(BF16), TN)
            dbl = ebl[:, sl] * jnp.sum(st0 * dst1, axis=0, keepdims=True)
            dkd = _dot(vb, dst1b)
            dv = _dot(at_ref[0, h].astype(BF16), dob) + _dot(kdh.astype(BF16), dst1b, NT)
            dat = jnp.where(rows <= lanes, _dot(vb, dob, NT), 0.0)

            def step(t, dki):
                bt = b_scr[h, pl.ds(t, 1), :]
                qt = q_scr[h, pl.ds(t, 1), :]
                dm = jnp.exp(jnp.minimum(bt - bh, 0.0))
                col = jnp.sum(jnp.where(lanes == t, dat, 0.0), axis=-1, keepdims=True)
                xm = col * dm
                dqi_scr[pl.ds(t, 1), :] = jnp.sum(xm * kh, axis=0, keepdims=True)
                return dki + xm * qt

            dki = lax.fori_loop(0, CH, step, jnp.zeros((CH, HD), F32))
            dqi = dqi_scr[...]
            dqf = dqe * ebh + dqi
            dk = dkd * ekd[:, sl] + dki
            dkk = dkd * kdh
            db = dqe * qe + qfh * dqi - kh * dki - dkk
            db = db + jnp.where(row1 == CH - 1, dbl + jnp.sum(dkk, axis=0, keepdims=True), 0.0)
            dlf = _dot(upper, db, precision=HIGHEST)
            df = dlf / f[:, sl] - dk
            sgf = sig[:, sl]
            dlb_ref[0:1, sl] += jnp.sum(df * (1.0 - sgf), axis=0, keepdims=True)
            dfl = df * omlb[:, sl] * sgf * (1.0 - sgf)
            sqh = sq[:, sl]
            dq = dqf * (sqh * (1.0 + q[:, sl] * (1.0 - sqh)))
            dp_ref[:, h * HD:(h + 1) * HD] = dq.astype(BF16)
            dp_ref[:, DH + h * HD:DH + (h + 1) * HD] = dfl.astype(BF16)
            dp_ref[:, 2 * DH + h * HD:2 * DH + (h + 1) * HD] = dv.astype(BF16)
            dp_ref[:, 3 * DH + h * HD:3 * DH + (h + 1) * HD] = dg.astype(BF16)
        dgn_ref[0:1, :] += dgn

        @pl.when(i == nc - 1)
        def _():
            gl = dlb_ref[0:1, :] * lb * omlb
            dlb_ref[0:1, :] = gl
            dlb_ref[1:2, :] = -gl

    rev = lambda j: pl.BlockSpec((CH, DH), lambda c: (nc - 1 - c, j))
    return pl.pallas_call(
        body, grid=(nc,),
        in_specs=[rev(2), rev(3), rev(4), rev(5), rev(0),
                  pl.BlockSpec((1, NH, CH, CH), lambda c: (nc - 1 - c, 0, 0, 0)),
                  pl.BlockSpec((1, NH, HD, HD), lambda c: (nc - 1 - c, 0, 0, 0)),
                  rev(1), _full((2, DH)), _full((1, HD)), pl.BlockSpec(memory_space=pl.ANY)],
        out_specs=[pl.BlockSpec((CH, 4 * DH), lambda c: (nc - 1 - c, 0)), _full((8, DH)), _full((8, HD))],
        out_shape=[SDS((T, DIN), BF16), SDS((8, DH), F32), SDS((8, HD), F32)],
        scratch_shapes=[pltpu.VMEM((NH, HD, HD), F32), pltpu.VMEM((NH, CH, HD), F32), pltpu.VMEM((NH, CH, HD), F32),
                        pltpu.VMEM((CH, HD), F32)],
        input_output_aliases={10: 0},
        compiler_params=_arb(), name="hgrn_bwd")(proj, proj, proj, proj, o_pre, at_all, st_all, dycat, lower_bounds, gn_w, dproj)


def _proj_in_bwd(dproj, x, dx1, nw, sc, w_in_b, tm):
    T = x.shape[0]

    def body(dp_ref, x_ref, dx1_ref, nw_ref, sc_ref, w_ref, gx_ref, acc_ref):
        @pl.when(pl.program_id(0) == 0)
        def _():
            acc_ref[...] = jnp.zeros_like(acc_ref)

        dh = _dot(dp_ref[:, 0:4 * DH], w_ref[:, 2 * DG:DIN], NT) + _dot(dp_ref[:, 4 * DH:DIN], w_ref[:, 0:2 * DG], NT)
        xv = x_ref[...]
        r = _rms(xv)
        xh = xv * r
        n1 = xh * nw_ref[...]
        acc_ref[0:1, :] += jnp.sum(dh, axis=0, keepdims=True)
        acc_ref[1:2, :] += jnp.sum(dh * n1, axis=0, keepdims=True)
        dn = dh * (1.0 + sc_ref[...])
        acc_ref[2:3, :] += jnp.sum(dn * xh, axis=0, keepdims=True)
        gx_ref[...] = dx1_ref[...] + _rms_bwd(xh, r, dn * nw_ref[...])

    row = lambda i: (i, 0)
    return pl.pallas_call(
        body, grid=(T // tm,),
        in_specs=[pl.BlockSpec((tm, DIN), row), pl.BlockSpec((tm, D), row), pl.BlockSpec((tm, D), row),
                  _full((1, D)), _full((1, D)), _resident((D, DIN))],
        out_specs=[pl.BlockSpec((tm, D), row), _full((8, D))],
        out_shape=[SDS((T, D), F32), SDS((8, D), F32)],
        compiler_params=_arb(), name="proj_in_bwd")(dproj, x, dx1, nw, sc, w_in_b)


def _wgrad(a, b, bk, bn, tt, name, b_col_block=None):
    T, K = a.shape
    N = b.shape[1]
    nn, nk, nt = N // bn, K // bk, T // tt
    bmap = (lambda n, k, t: (t, n)) if b_col_block is None else (lambda n, k, t: (t, b_col_block(n)))

    def body(a_ref, b_ref, o_ref):
        @pl.when(pl.program_id(2) == 0)
        def _():
            o_ref[...] = jnp.zeros_like(o_ref)

        o_ref[0] += _dot(a_ref[...], b_ref[...], TN)

    return pl.pallas_call(
        body, grid=(nn, nk, nt),
        in_specs=[pl.BlockSpec((tt, bk), lambda n, k, t: (t, k)), pl.BlockSpec((tt, bn), bmap)],
        out_specs=pl.BlockSpec((1, bk, bn), lambda n, k, t: (n, k, 0)),
        out_shape=SDS((nn, K, bn), F32),
        compiler_params=_arb(3), name=name)(a, b)


def _adam_math(w, g, m, v):
    m = B1 * m + (1.0 - B1) * g
    v = B2 * v + (1.0 - B2) * (g * g)
    m_hat = m / (1.0 - B1 ** STEP)
    v_hat = v / (1.0 - B2 ** STEP)
    return -LR * (m_hat / (jnp.sqrt(v_hat) + AEPS) + WD * w), m, v


def _adamw(w, g, m, v, rb, name):
    R, C = w.shape

    def body(w_ref, g_ref, m_ref, v_ref, d_out, m_out, v_out):
        d_out[...], m_out[...], v_out[...] = _adam_math(w_ref[...], g_ref[...], m_ref[...], v_ref[...])

    spec = pl.BlockSpec((rb, C), lambda i: (i, 0))
    return pl.pallas_call(
        body, grid=(R // rb,), in_specs=[spec] * 4, out_specs=[spec] * 3,
        out_shape=[SDS((R, C), F32)] * 3, compiler_params=_arb(), name=name)(w, g, m, v)


def _ada_forward(c_all, w_ada):
    n = w_ada.shape[1]

    def body(c_ref, w_ref, ca_ref, p_ref):
        cv = c_ref[...]
        ca = cv * _sigmoid(cv)
        ca_ref[...] = ca
        p_ref[...] = _dot(ca, w_ref[...], precision=HIGHEST)

    return pl.pallas_call(
        body, grid=(n // 512,),
        in_specs=[_full((N_DEV, D)), pl.BlockSpec((D, 512), lambda i: (0, i))],
        out_specs=[_full((N_DEV, D)), pl.BlockSpec((N_DEV, 512), lambda i: (0, i))],
        out_shape=[SDS((N_DEV, D), F32), SDS((N_DEV, n), F32)],
        compiler_params=_arb(), name="ada_forward")(c_all, w_ada)


def _ada_wgrad_adam(cact_t, dada, w, m, v):
    R, C = w.shape
    rb = 256

    def body(c_ref, d_ref, w_ref, m_ref, v_ref, g_out, d_out, m_out, v_out):
        g = _dot(c_ref[...], d_ref[...], precision=HIGHEST)
        g_out[...] = g
        d_out[...], m_out[...], v_out[...] = _adam_math(w_ref[...], g, m_ref[...], v_ref[...])

    spec = pl.BlockSpec((rb, C), lambda i: (i, 0))
    return pl.pallas_call(
        body, grid=(R // rb,),
        in_specs=[pl.BlockSpec((rb, N_DEV), lambda i: (i, 0)), _full((N_DEV, C)), spec, spec, spec],
        out_specs=[spec] * 4, out_shape=[SDS((R, C), F32)] * 4,
        compiler_params=_arb(), name="ada_wgrad_adam")(cact_t, dada, w, m, v)


def _small_finalize(gathered, w, m, v):
    def body(ga_ref, w_ref, m_ref, v_ref, g_out, d_out, m_out, v_out):
        g = ga_ref[0:SMALL_ROWS, :]
        for dev in range(1, N_DEV):
            g = g + ga_ref[dev * SMALL_ROWS:(dev + 1) * SMALL_ROWS, :]
        g_out[...] = g
        d_out[...], m_out[...], v_out[...] = _adam_math(w_ref[...], g, m_ref[...], v_ref[...])

    return pl.pallas_call(
        body, out_shape=[SDS((SMALL_ROWS, D), F32)] * 4, name="small_finalize")(gathered, w, m, v)


def _position():
    x, y, c = lax.axis_index("x"), lax.axis_index("y"), lax.axis_index("c")
    return x, y, c


def _chip_at(x, y, r):
    return (x ^ (r >> 1), y ^ (r & 1))


def _all_gather_rows(block, name):
    m_per, n = block.shape

    def body(x_ref, out_ref, send_sems, recv_sems, local_sem):
        x, y, c = _position()
        me, sibling = (x, y, c), (x, y, 1 - c)
        chips = [_chip_at(x, y, r) for r in (1, 2, 3)]

        def rows(px, py, pc):
            return out_ref.at[pl.ds((4 * px + 2 * py + pc) * m_per, m_per), :]

        def copy(k, blk, to, src=None):
            return pltpu.make_async_remote_copy(
                src_ref=rows(*blk) if src is None else src, dst_ref=rows(*blk),
                send_sem=send_sems.at[k], recv_sem=recv_sems.at[k], device_id=to, device_id_type=MESH)

        mine = pltpu.make_async_copy(x_ref, rows(*me), local_sem)
        mine.start()
        first = [copy(0, me, sibling, src=x_ref)]
        first += [copy(1 + j, me, (*chip, c), src=x_ref) for j, chip in enumerate(chips)]
        for cp in first:
            cp.start()
        passed = [copy(4 + j, (*chip, c), sibling) for j, chip in enumerate(chips)]
        for j, chip in enumerate(chips):
            copy(1 + j, (*chip, c), me).wait_recv()
            passed[j].start()
        copy(0, sibling, me).wait_recv()
        for j, chip in enumerate(chips):
            copy(4 + j, (*chip, 1 - c), me).wait_recv()
        for cp in first + passed:
            cp.wait_send()
        mine.wait()

    return pl.pallas_call(
        body, out_shape=SDS((N_DEV * m_per, n), block.dtype),
        in_specs=[pl.BlockSpec(memory_space=pltpu.VMEM)], out_specs=pl.BlockSpec(memory_space=pltpu.VMEM),
        scratch_shapes=[pltpu.SemaphoreType.DMA((7,)), pltpu.SemaphoreType.DMA((7,)), pltpu.SemaphoreType.DMA],
        name=name)(block)


def _gather_weights(shards):
    nw = len(shards)

    def body(*refs):
        ins, outs = refs[:nw], refs[nw:2 * nw]
        send_sems, recv_sems, local_sems = refs[2 * nw:]
        x, y, c = _position()
        j = 2 * x + y
        started = []
        for w, (arr, axis) in enumerate(shards):
            size = arr.shape[axis]

            def slot(chip_idx, w=w, axis=axis, size=size):
                if axis == 0:
                    return outs[w].at[pl.ds(chip_idx * size, size), :]
                return outs[w].at[:, pl.ds(chip_idx * size, size)]

            local = pltpu.make_async_copy(ins[w], slot(j), local_sems.at[w])
            local.start()
            started.append(local)
            for r in (1, 2, 3):
                cx, cy = _chip_at(x, y, r)
                k = 3 * w + r - 1
                cp = pltpu.make_async_remote_copy(
                    src_ref=ins[w], dst_ref=slot(j), send_sem=send_sems.at[k], recv_sem=recv_sems.at[k],
                    device_id=(cx, cy, c), device_id_type=MESH)
                cp.start()
                started.append((cp, slot(j ^ r), k, w))
        for item in started:
            if isinstance(item, tuple):
                cp, from_slot, k, w = item
                cp.wait_send()
                pltpu.make_async_remote_copy(
                    src_ref=ins[w], dst_ref=from_slot, send_sem=send_sems.at[k], recv_sem=recv_sems.at[k],
                    device_id=(x, y, c), device_id_type=MESH).wait_recv()
            else:
                item.wait()

    out_shape = []
    for arr, axis in shards:
        shp = list(arr.shape)
        shp[axis] *= N_CHIPS
        out_shape.append(SDS(tuple(shp), arr.dtype))
    anyspec = pl.BlockSpec(memory_space=pl.ANY)
    return pl.pallas_call(
        body, out_shape=out_shape, in_specs=[anyspec] * nw, out_specs=[anyspec] * nw,
        scratch_shapes=[pltpu.SemaphoreType.DMA((3 * nw,)), pltpu.SemaphoreType.DMA((3 * nw,)),
                        pltpu.SemaphoreType.DMA((nw,))],
        name="gather_weights")(*[a for a, _ in shards])


def _exchange_core_halves(grads):
    nw = len(grads)

    def body(*refs):
        ins, outs = refs[:nw], refs[nw:2 * nw]
        send_sems, recv_sems = refs[2 * nw:]
        x, y, c = _position()
        cps = []
        for w in range(nw):
            cp = pltpu.make_async_remote_copy(
                src_ref=ins[w].at[:, 1 - c], dst_ref=outs[w], send_sem=send_sems.at[w], recv_sem=recv_sems.at[w],
                device_id=(x, y, 1 - c), device_id_type=MESH)
            cp.start()
            cps.append(cp)
        for cp in cps:
            cp.wait()

    anyspec = pl.BlockSpec(memory_space=pl.ANY)
    return pl.pallas_call(
        body, out_shape=[SDS((g.shape[0], g.shape[2], g.shape[3]), F32) for g in grads],
        in_specs=[anyspec] * nw, out_specs=[anyspec] * nw,
        scratch_shapes=[pltpu.SemaphoreType.DMA((nw,)), pltpu.SemaphoreType.DMA((nw,))],
        name="exchange_core_halves")(*grads)


def _add_core_halves(g4, recv, c_idx, rb, name):
    ns, _, rh, C = g4.shape

    def body(c_ref, g_ref, r_ref, o_ref):
        o_ref[...] = g_ref[0] + r_ref[...]

    return pl.pallas_call(
        body,
        grid_spec=pltpu.PrefetchScalarGridSpec(
            num_scalar_prefetch=1, grid=(ns, rh // rb),
            in_specs=[pl.BlockSpec((1, 1, rb, C), lambda s, i, cr: (s, cr[0], i, 0)),
                      pl.BlockSpec((1, rb, C), lambda s, i, cr: (s, i, 0))],
            out_specs=pl.BlockSpec((1, rb, C), lambda s, i, cr: (s, i, 0))),
        out_shape=SDS((ns, rh, C), F32), compiler_params=_arb(2), name=name)(c_idx, g4, recv)


def _exchange_chips(sums):
    nw = len(sums)

    def body(*refs):
        ins, outs = refs[:nw], refs[nw:2 * nw]
        send_sems, recv_sems, local_sems = refs[2 * nw:]
        x, y, c = _position()
        j = 2 * x + y
        started = []
        for w in range(nw):
            local = pltpu.make_async_copy(ins[w].at[j], outs[w].at[0], local_sems.at[w])
            local.start()
            started.append(local)
            for r in (1, 2, 3):
                cx, cy = _chip_at(x, y, r)
                k = 3 * w + r - 1
                cp = pltpu.make_async_remote_copy(
                    src_ref=ins[w].at[j ^ r], dst_ref=outs[w].at[r], send_sem=send_sems.at[k], recv_sem=recv_sems.at[k],
                    device_id=(cx, cy, c), device_id_type=MESH)
                cp.start()
                started.append(cp)
        for cp in started:
            cp.wait()

    anyspec = pl.BlockSpec(memory_space=pl.ANY)
    return pl.pallas_call(
        body, out_shape=[SDS(s.shape, F32) for s in sums], in_specs=[anyspec] * nw, out_specs=[anyspec] * nw,
        scratch_shapes=[pltpu.SemaphoreType.DMA((3 * nw,)), pltpu.SemaphoreType.DMA((3 * nw,)),
                        pltpu.SemaphoreType.DMA((nw,))],
        name="exchange_chips")(*sums)


def _add_chips(slots, order, rb, name):
    _, rh, C = slots.shape

    def body(o_ref, a_ref, b_ref, c_ref, d_ref, out_ref):
        out_ref[...] = ((a_ref[0] + b_ref[0]) + c_ref[0]) + d_ref[0]

    def spec(i):
        return pl.BlockSpec((1, rb, C), lambda t, o: (o[i], t, 0))

    return pl.pallas_call(
        body,
        grid_spec=pltpu.PrefetchScalarGridSpec(
            num_scalar_prefetch=1, grid=(rh // rb,),
            in_specs=[spec(0), spec(1), spec(2), spec(3)],
            out_specs=pl.BlockSpec((rb, C), lambda t, o: (t, 0))),
        out_shape=SDS((rh, C), F32), compiler_params=_arb(), name=name)(order, slots, slots, slots, slots)


def _share_halves(halves):
    nw = len(halves)

    def body(*refs):
        ins, outs = refs[:nw], refs[nw:2 * nw]
        send_sems, recv_sems, local_sems = refs[2 * nw:]
        x, y, c = _position()
        started = []
        for w in range(nw):
            local = pltpu.make_async_copy(ins[w], outs[w].at[c], local_sems.at[w])
            local.start()
            started.append(local)
            cp = pltpu.make_async_remote_copy(
                src_ref=ins[w], dst_ref=outs[w].at[c], send_sem=send_sems.at[w], recv_sem=recv_sems.at[w],
                device_id=(x, y, 1 - c), device_id_type=MESH)
            cp.start()
            started.append(cp)
        for cp in started:
            cp.wait()

    anyspec = pl.BlockSpec(memory_space=pl.ANY)
    return pl.pallas_call(
        body, out_shape=[SDS((2,) + h.shape, F32) for h in halves], in_specs=[anyspec] * nw, out_specs=[anyspec] * nw,
        scratch_shapes=[pltpu.SemaphoreType.DMA((nw,)), pltpu.SemaphoreType.DMA((nw,)), pltpu.SemaphoreType.DMA((nw,))],
        name="share_halves")(*halves)


def _pack_small(b_ada, norm1_w, norm2_w, final_norm_w, v_ln_w, v_ln_b, lower_bounds, b_s, gn_w, w_s):
    pad = jnp.zeros((D - NH * BLK - HD,), F32)
    rows = [b_ada.reshape(6, D), norm1_w.reshape(1, D), norm2_w.reshape(1, D), final_norm_w.reshape(1, D),
            jnp.concatenate([v_ln_w.reshape(-1), v_ln_b.reshape(-1)]).reshape(1, D),
            lower_bounds.reshape(1, D),
            jnp.concatenate([b_s.reshape(-1), gn_w.reshape(-1), pad]).reshape(1, D),
            w_s.reshape(NH * BLK * BLK // D, D),
            jnp.zeros((SMALL_ROWS - 76, D), F32)]
    return jnp.concatenate(rows, axis=0)


def _unpack_small(p):
    return dict(
        b_ada=p[0:6].reshape(1, 6 * D), norm1_w=p[6:7], norm2_w=p[7:8], final_norm_w=p[8],
        v_ln_w=p[9:10, 0:DG], v_ln_b=p[9:10, DG:D], lower_bounds=p[10].reshape(2, DH),
        b_s=p[11, 0:NH * BLK].reshape(1, NH, BLK), gn_w=p[11:12, NH * BLK:NH * BLK + HD],
        w_s=p[12:76].reshape(1, NH, BLK, BLK))


def _row_block(r):
    for cand in (256, 176, 128, 64, 32, 16, 8):
        if r % cand == 0:
            return cand
    return r


def kernel(x, c, w_ada, b_ada, norm1_w, w_in, w_s, b_s, v_ln_w, v_ln_b, lower_bounds, gn_w, w_out, norm2_w, w_ffn_in, w_ffn_out, final_norm_w, loss_target, m_w_ada, m_b_ada, m_norm1_w, m_w_in, m_w_s, m_b_s, m_v_ln_w, m_v_ln_b, m_lower_bounds, m_gn_w, m_w_out, m_norm2_w, m_w_ffn_in, m_w_ffn_out, m_final_norm_w, v_w_ada, v_b_ada, v_norm1_w, v_w_in, v_w_s, v_b_s, v_v_ln_w, v_v_ln_b, v_lower_bounds, v_gn_w, v_w_out, v_norm2_w, v_w_ffn_in, v_w_ffn_out, v_final_norm_w):
    T = x.shape[1]
    tm = min(256, T)
    px, py, pc = _position()
    chip = 2 * px + py
    me = 4 * px + 2 * py + pc
    x2d = x.reshape(T, D)
    tgt = loss_target.reshape(T, D)

    c_all = _all_gather_rows(jnp.broadcast_to(c, (8, D)), "gather_c").reshape(N_DEV, 8, D)[:, 0, :]
    cact, ada_part = _ada_forward(c_all, w_ada[0])
    n_ada = ada_part.shape[1]
    ada_all = _all_gather_rows(ada_part, "gather_ada").reshape(N_CHIPS, 2, N_DEV, n_ada)[:, 0]
    ada = lax.dynamic_index_in_dim(ada_all, me, axis=1, keepdims=False).reshape(1, 6 * D) + b_ada
    sh1, sc1, g1, sh2, sc2, g2 = [ada[:, i * D:(i + 1) * D] for i in range(6)]

    w_in_b, w_out_b, w_fi_b, w_fo_b = _gather_weights(
        [(w_in[0].astype(BF16), 1), (w_out[0].astype(BF16), 0), (w_ffn_in[0].astype(BF16), 1), (w_ffn_out[0].astype(BF16), 0)])

    rr = lax.broadcasted_iota(jnp.int32, (BLK, BLK), 0) // CH
    cc = lax.broadcasted_iota(jnp.int32, (BLK, BLK), 1) // CH
    ws_b = jnp.where((rr >= cc)[None], w_s[0], 0.0).astype(BF16)
    bst = b_s[0].T
    lnw, lnb = v_ln_w, v_ln_b
    nw1, nw2, fw = norm1_w, norm2_w, final_norm_w.reshape(1, D)

    h1, proj = _proj_in(x2d, nw1, sc1, sh1, w_in_b, tm)
    ycat = _gmlp_fwd(proj, ws_b, bst, lnw, lnb)
    ycat, o_pre, at_all, st_all = _hgrn_fwd(proj, lower_bounds, gn_w, ycat)

    dycat, dx1, h2, act, dff, dgu, dmix, acc2 = _token_local(
        x2d, ycat, tgt, g1, nw2, sc2, sh2, g2, fw, w_out_b, w_fi_b, w_fo_b, tm)

    dproj, dws, dbs, dln = _gmlp_bwd(proj, dycat, ws_b, bst, lnw, lnb)
    dproj, dlb, dgn = _hgrn_bwd(proj, o_pre, at_all, st_all, dycat, lower_bounds, gn_w, dproj)
    grad_x, acc1 = _proj_in_bwd(dproj, x2d, dx1, nw1, sc1, w_in_b, tm)

    tt = min(512, T)
    g_in = _wgrad(h1, dproj, D, 256, tt, "wgrad_in", b_col_block=lambda n: (n + 8) % 12)
    g_in = g_in.reshape(N_CHIPS, 3, D, 256).transpose(0, 2, 1, 3).reshape(N_CHIPS, D, 768)
    g_out = _wgrad(ycat, dmix, D, D, tt, "wgrad_out").reshape(N_CHIPS, D // N_CHIPS, D)
    g_fi = _wgrad(h2, dgu, D, FFB, tt, "wgrad_ffn_in")
    g_fo = _wgrad(act, dff, FFB, D, tt, "wgrad_ffn_out").reshape(N_CHIPS, DFF // N_CHIPS, D)

    big = [g_in, g_out, g_fi, g_fo]
    g4 = [g.reshape(N_CHIPS, 2, g.shape[1] // 2, g.shape[2]) for g in big]
    recv = _exchange_core_halves(g4)
    c_idx = jnp.reshape(pc, (1,)).astype(jnp.int32)
    names = ["in", "out", "ffn_in", "ffn_out"]
    sums = [_add_core_halves(a, b, c_idx, _row_block(a.shape[2]), "add_core_" + n) for a, b, n in zip(g4, recv, names)]
    slots = _exchange_chips(sums)
    order = (chip ^ jnp.arange(N_CHIPS, dtype=jnp.int32)).astype(jnp.int32)
    halves = [_add_chips(s, order, _row_block(s.shape[1]), "add_chips_" + n) for s, n in zip(slots, names)]
    shard_grads = [h.reshape(2 * h.shape[1], h.shape[2]) for h in _share_halves(halves)]

    big_w = [(w_in, m_w_in, v_w_in), (w_out, m_w_out, v_w_out), (w_ffn_in, m_w_ffn_in, v_w_ffn_in),
             (w_ffn_out, m_w_ffn_out, v_w_ffn_out)]
    big_out = []
    for g, (w, m, v), n in zip(shard_grads, big_w, names):
        d_, m_, v_ = _adamw(w[0], g, m[0], v[0], _row_block(g.shape[0]), "adamw_" + n)
        big_out.append((g[None], d_[None], m_[None], v_[None]))

    d_ada = jnp.stack([acc1[0], acc1[1], acc2[5], acc2[2], acc2[1], acc2[0]]).reshape(1, 6 * D)
    small_g = _pack_small(d_ada, acc1[2], acc2[3], acc2[4], dln[0], dln[1], dlb[0:2], dbs[:, 0:NH].T, dgn[0], dws)
    gathered = _all_gather_rows(small_g, "gather_small")
    sw = _pack_small(b_ada, norm1_w, norm2_w, final_norm_w, v_ln_w, v_ln_b, lower_bounds, b_s, gn_w, w_s)
    sm = _pack_small(m_b_ada, m_norm1_w, m_norm2_w, m_final_norm_w, m_v_ln_w, m_v_ln_b, m_lower_bounds, m_b_s, m_gn_w, m_w_s)
    sv = _pack_small(v_b_ada, v_norm1_w, v_norm2_w, v_final_norm_w, v_v_ln_w, v_v_ln_b, v_lower_bounds, v_b_s, v_gn_w, v_w_s)
    small = [_unpack_small(p) for p in _small_finalize(gathered, sw, sm, sv)]

    dada_all = gathered.reshape(N_DEV, SMALL_ROWS, D)[:, 0:6, :].reshape(N_DEV, 6 * D)
    dada = lax.dynamic_slice_in_dim(dada_all, chip * n_ada, n_ada, axis=1)
    ada_out = [o[None] for o in _ada_wgrad_adam(cact.T, dada, w_ada[0], m_w_ada[0], v_w_ada[0])]

    loss = lax.psum(jnp.sum(acc2[6]), ("x", "y", "c"))

    order_names = ['w_ada', 'b_ada', 'norm1_w', 'w_in', 'w_s', 'b_s', 'v_ln_w', 'v_ln_b', 'lower_bounds', 'gn_w',
                   'w_out', 'norm2_w', 'w_ffn_in', 'w_ffn_out', 'final_norm_w']
    big_idx = {'w_in': 0, 'w_out': 1, 'w_ffn_in': 2, 'w_ffn_out': 3}
    outs = [loss, grad_x.reshape(1, T, D)]
    for kind in range(4):
        for n in order_names:
            if n == 'w_ada':
                outs.append(ada_out[kind])
            elif n in big_idx:
                outs.append(big_out[big_idx[n]][kind])
            else:
                outs.append(small[kind][n])
    return tuple(outs)
```

```python
import functools

import jax
import jax.numpy as jnp
from jax import lax
from jax.experimental import pallas as pl
from jax.experimental.pallas import tpu as pltpu

F32 = jnp.float32
BF16 = jnp.bfloat16
SDS = jax.ShapeDtypeStruct
MESH = pl.DeviceIdType.MESH
HIGHEST = lax.Precision.HIGHEST

D = 1024
DG = 512
DH = 512
NH = 4
HD = 128
BLK = 128
CH = 64
DFF = 2816
DIN = 3072
FFB = 1408
UNROLL = 8
N_CHIPS = 4
N_DEV = 8
EPS = 1e-6
LR, B1, B2, AEPS, WD, STEP = 0.001, 0.9, 0.999, 1e-08, 0.01, 10
SMALL_ROWS = 80

NT = (((1,), (1,)), ((), ()))
TN = (((0,), (0,)), ((), ()))


def _full(shape):
    nd = len(shape)
    return pl.BlockSpec(shape, lambda *_: (0,) * nd)


def _resident(shape):
    nd = len(shape)
    return pl.BlockSpec(shape, lambda *_: (0,) * nd, pipeline_mode=pl.Buffered(1))


def _arb(n=1):
    return pltpu.CompilerParams(dimension_semantics=("arbitrary",) * n)


def _dot(a, b, dims=None, precision=None):
    if dims is None:
        return jnp.dot(a, b, preferred_element_type=F32, precision=precision)
    return lax.dot_general(a, b, dims, preferred_element_type=F32, precision=precision)


def _sigmoid(x):
    return jax.nn.sigmoid(x)


def _gelu_parts(x):
    cdf = 0.5 * (1.0 + lax.erf(x * 0.7071067811865476))
    pdf = jnp.exp(-0.5 * x * x) * 0.3989422804014327
    return x * cdf, cdf + x * pdf


def _rms(x):
    return lax.rsqrt(jnp.mean(x * x, axis=-1, keepdims=True) + EPS)


def _rms_bwd(xhat, r, gw):
    return r * (gw - xhat * jnp.mean(xhat * gw, axis=-1, keepdims=True))


def _lower_bound(lbp_ref):
    l0, l1 = lbp_ref[0:1, :], lbp_ref[1:2, :]
    m = jnp.maximum(l0, l1)
    e0, e1 = jnp.exp(l0 - m), jnp.exp(l1 - m)
    return e0 / (e0 + e1), e1 / (e0 + e1)


def _proj_in(x, nw, sc, sh, w_in_b, tm):
    T = x.shape[0]

    def body(x_ref, nw_ref, sc_ref, sh_ref, w_ref, h_ref, p_ref):
        xv = x_ref[...]
        h = ((xv * _rms(xv)) * nw_ref[...]) * (1.0 + sc_ref[...]) + sh_ref[...]
        hb = h.astype(BF16)
        h_ref[...] = hb
        p_ref[...] = _dot(hb, w_ref[...])

    row = lambda i: (i, 0)
    return pl.pallas_call(
        body, grid=(T // tm,),
        in_specs=[pl.BlockSpec((tm, D), row), _full((1, D)), _full((1, D)), _full((1, D)), _resident((D, DIN))],
        out_specs=[pl.BlockSpec((tm, D), row), pl.BlockSpec((tm, DIN), row)],
        out_shape=[SDS((T, D), BF16), SDS((T, DIN), F32)],
        compiler_params=_arb(), name="proj_in")(x, nw, sc, sh, w_in_b)


def _gmlp_common(u, v, lnw, lnb, ws_ref, bst_ref):
    ug, dug = _gelu_parts(u)
    vg, dvg = _gelu_parts(v)
    mu = jnp.mean(vg, axis=-1, keepdims=True)
    vc = vg - mu
    rstd = lax.rsqrt(jnp.mean(vc * vc, axis=-1, keepdims=True) + EPS)
    vhat = vc * rstd
    vn = vhat * lnw + lnb
    vnb = vn.astype(BF16)
    mixed = []
    for h in range(NH):
        sl = slice(h * HD, (h + 1) * HD)
        mixed.append(_dot(ws_ref[h], vnb[:, sl]) + bst_ref[:, h:h + 1])
    return ug, dug, dvg, rstd, vhat, vnb, jnp.concatenate(mixed, axis=1)


def _gmlp_fwd(proj, ws_b, bst, lnw, lnb):
    T = proj.shape[0]

    def body(u_ref, v_ref, ws_ref, bst_ref, lnw_ref, lnb_ref, y_ref):
        ug, _, _, _, _, _, mixed = _gmlp_common(u_ref[...], v_ref[...], lnw_ref[...], lnb_ref[...], ws_ref, bst_ref)
        y_ref[...] = (ug * mixed).astype(BF16)

    return pl.pallas_call(
        body, grid=(T // BLK,),
        in_specs=[pl.BlockSpec((BLK, DG), lambda i: (i, 0)), pl.BlockSpec((BLK, DG), lambda i: (i, 1)),
                  _full((NH, BLK, BLK)), _full((BLK, NH)), _full((1, DG)), _full((1, DG))],
        out_specs=pl.BlockSpec((BLK, DG), lambda i: (i, 0)),
        out_shape=SDS((T, D), BF16),
        compiler_params=_arb(), name="gmlp_fwd")(proj, proj, ws_b, bst, lnw, lnb)


def _hgrn_prep(q, fl, lb, omlb):
    sq = _sigmoid(q)
    qf = q * sq
    sig = _sigmoid(fl)
    f = lb + omlb * sig
    lf = jnp.log(f)
    k = 1.0 - f
    r = lax.broadcasted_iota(jnp.int32, (CH, CH), 0)
    c = lax.broadcasted_iota(jnp.int32, (CH, CH), 1)
    tri = (r >= c).astype(F32)
    b = _dot(tri, lf, precision=HIGHEST)
    return sq, qf, sig, f, k, b


def _hgrn_fwd(proj, lower_bounds, gn_w, ycat):
    T = proj.shape[0]
    nc = T // CH

    def body(q_ref, f_ref, i_ref, g_ref, lbp_ref, gn_ref, ycat_any, y_ref, o_ref, at_ref, st_ref, s_scr, b_scr, q_scr):
        @pl.when(pl.program_id(0) == 0)
        def _():
            s_scr[...] = jnp.zeros_like(s_scr)

        lb, omlb = _lower_bound(lbp_ref)
        v = i_ref[...]
        g = g_ref[...]
        _, qf, _, _, k, b = _hgrn_prep(q_ref[...], f_ref[...], lb, omlb)
        eb = jnp.exp(b)
        bl = b[CH - 1:CH, :]
        ebl = jnp.exp(bl)
        kd = k * jnp.exp(bl - b)
        rows = lax.broadcasted_iota(jnp.int32, (CH, 1), 0)
        lanes = lax.broadcasted_iota(jnp.int32, (CH, CH), 1)
        for h in range(NH):
            sl = slice(h * HD, (h + 1) * HD)
            b_scr[h] = b[:, sl]
            q_scr[h] = qf[:, sl]
            st0 = s_scr[h]
            st_ref[0, h] = st0
            qe = (qf[:, sl] * eb[:, sl]).astype(BF16)
            inter = _dot(qe, st0.astype(BF16), NT)
            bh, kh = b[:, sl], k[:, sl]

            def step(i, at):
                base = pl.multiple_of(i * UNROLL, UNROLL)
                bt8 = b_scr[h, pl.ds(base, UNROLL), :]
                qt8 = q_scr[h, pl.ds(base, UNROLL), :]
                for j in range(UNROLL):
                    t = base + j
                    dm = jnp.exp(jnp.minimum(bt8[j:j + 1, :] - bh, 0.0))
                    col = jnp.sum(dm * kh * qt8[j:j + 1, :], axis=-1, keepdims=True)
                    col = jnp.where(rows <= t, col, 0.0)
                    at = jnp.where(lanes == t, col, at)
                return at

            at = lax.fori_loop(0, CH // UNROLL, step, jnp.zeros((CH, CH), F32))
            at_ref[0, h] = at
            vb = v[:, sl].astype(BF16)
            o = inter + _dot(at.astype(BF16), vb, TN)
            s_scr[h] = st0 * ebl[:, sl] + _dot(vb, kd[:, sl].astype(BF16), TN)
            o_ref[:, sl] = o
            gh = g[:, sl]
            y_ref[:, sl] = (((o * _rms(o)) * gn_ref[...]) * (gh * _sigmoid(gh))).astype(BF16)

    blk = lambda j: pl.BlockSpec((CH, DH), lambda c: (c, j))
    return pl.pallas_call(
        body, grid=(nc,),
        in_specs=[blk(2), blk(3), blk(4), blk(5), _full((2, DH)), _full((1, HD)),
                  pl.BlockSpec(memory_space=pl.ANY)],
        out_specs=[pl.BlockSpec((CH, DH), lambda c: (c, 1)),
                   pl.BlockSpec((CH, DH), lambda c: (c, 0)),
                   pl.BlockSpec((1, NH, CH, CH), lambda c: (c, 0, 0, 0)),
                   pl.BlockSpec((1, NH, HD, HD), lambda c: (c, 0, 0, 0))],
        out_shape=[SDS((T, D), BF16), SDS((T, DH), F32), SDS((nc, NH, CH, CH), F32), SDS((nc, NH, HD, HD), F32)],
        scratch_shapes=[pltpu.VMEM((NH, HD, HD), F32), pltpu.VMEM((NH, CH, HD), F32), pltpu.VMEM((NH, CH, HD), F32)],
        input_output_aliases={6: 0},
        compiler_params=_arb(), name="hgrn_fwd")(proj, proj, proj, proj, lower_bounds, gn_w, ycat)


def _token_local(x, ycat, tgt, g1, nw2, sc2, sh2, g2, fw, w_out_b, w_fi_b, w_fo_b, tm):
    T = x.shape[0]
    inv_d = 1.0 / D

    def body(x_ref, y_ref, t_ref, g1_ref, nw2_ref, sc2_ref, sh2_ref, g2_ref, fw_ref, wo_ref, wfi_ref, wfo_ref,
             dy_ref, dx1_ref, h2_ref, act_ref, dff_ref, dgu_ref, dmix_ref, acc_ref):
        @pl.when(pl.program_id(0) == 0)
        def _():
            acc_ref[...] = jnp.zeros_like(acc_ref)

        def acc(row, val):
            acc_ref[row:row + 1, :] += jnp.sum(val, axis=0, keepdims=True)

        g1v, g2v = g1_ref[...], g2_ref[...]
        mix = _dot(y_ref[...], wo_ref[...])
        x1 = x_ref[...] + g1v * mix
        r2 = _rms(x1)
        xh2 = x1 * r2
        n2 = xh2 * nw2_ref[...]
        osc2 = 1.0 + sc2_ref[...]
        h2b = (n2 * osc2 + sh2_ref[...]).astype(BF16)
        h2_ref[...] = h2b
        ff = jnp.zeros((tm, D), F32)
        saved = []
        for kb in range(DFF // FFB):
            gate = _dot(h2b, wfi_ref[:, kb * FFB:(kb + 1) * FFB])
            up = _dot(h2b, wfi_ref[:, DFF + kb * FFB:DFF + (kb + 1) * FFB])
            sg = _sigmoid(gate)
            actb = (gate * sg * up).astype(BF16)
            act_ref[:, kb * FFB:(kb + 1) * FFB] = actb
            ff = ff + _dot(actb, wfo_ref[kb * FFB:(kb + 1) * FFB, :])
            saved.append((gate, up, sg))
        x2 = x1 + g2v * ff
        r3 = _rms(x2)
        xh3 = x2 * r3
        err = xh3 * fw_ref[...] - t_ref[...]
        acc(6, (0.5 * inv_d) * err * err)
        dy = err * inv_d
        acc(4, dy * xh3)
        dx2 = _rms_bwd(xh3, r3, dy * fw_ref[...])
        acc(0, dx2 * ff)
        dffb = (dx2 * g2v).astype(BF16)
        dff_ref[...] = dffb
        dh2 = jnp.zeros((tm, D), F32)
        for kb in range(DFF // FFB):
            gate, up, sg = saved[kb]
            da = _dot(dffb, wfo_ref[kb * FFB:(kb + 1) * FFB, :], NT)
            dgate = (da * up * (sg * (1.0 + gate * (1.0 - sg)))).astype(BF16)
            dup = (da * gate * sg).astype(BF16)
            dgu_ref[:, kb * FFB:(kb + 1) * FFB] = dgate
            dgu_ref[:, DFF + kb * FFB:DFF + (kb + 1) * FFB] = dup
            dh2 = dh2 + _dot(dgate, wfi_ref[:, kb * FFB:(kb + 1) * FFB], NT)
            dh2 = dh2 + _dot(dup, wfi_ref[:, DFF + kb * FFB:DFF + (kb + 1) * FFB], NT)
        acc(2, dh2)
        acc(1, dh2 * n2)
        dn2 = dh2 * osc2
        acc(3, dn2 * xh2)
        dx1 = dx2 + _rms_bwd(xh2, r2, dn2 * nw2_ref[...])
        acc(5, dx1 * mix)
        dmixb = (dx1 * g1v).astype(BF16)
        dmix_ref[...] = dmixb
        dy_ref[...] = _dot(dmixb, wo_ref[...], NT)
        dx1_ref[...] = dx1

    row = lambda i: (i, 0)
    vec = _full((1, D))
    return pl.pallas_call(
        body, grid=(T // tm,),
        in_specs=[pl.BlockSpec((tm, D), row), pl.BlockSpec((tm, D), row), pl.BlockSpec((tm, D), row),
                  vec, vec, vec, vec, vec, vec,
                  _resident((D, D)), _resident((D, 2 * DFF)), _resident((DFF, D))],
        out_specs=[pl.BlockSpec((tm, D), row), pl.BlockSpec((tm, D), row), pl.BlockSpec((tm, D), row),
                   pl.BlockSpec((tm, DFF), row), pl.BlockSpec((tm, D), row), pl.BlockSpec((tm, 2 * DFF), row),
                   pl.BlockSpec((tm, D), row), _full((8, D))],
        out_shape=[SDS((T, D), F32), SDS((T, D), F32), SDS((T, D), BF16), SDS((T, DFF), BF16), SDS((T, D), BF16),
                   SDS((T, 2 * DFF), BF16), SDS((T, D), BF16), SDS((8, D), F32)],
        compiler_params=_arb(), name="token_local")(x, ycat, tgt, g1, nw2, sc2, sh2, g2, fw, w_out_b, w_fi_b, w_fo_b)


def _gmlp_bwd(proj, dycat, ws_b, bst, lnw, lnb):
    T = proj.shape[0]
    nb = T // BLK

    def body(u_ref, v_ref, dy_ref, ws_ref, bst_ref, lnw_ref, lnb_ref, dp_ref, dws_ref, dbs_ref, dln_ref, dbs_acc):
        i = pl.program_id(0)

        @pl.when(i == 0)
        def _():
            dws_ref[...] = jnp.zeros_like(dws_ref)
            dln_ref[...] = jnp.zeros_like(dln_ref)
            dbs_acc[...] = jnp.zeros_like(dbs_acc)

        ug, dug, dvg, rstd, vhat, vnb, mixed = _gmlp_common(u_ref[...], v_ref[...], lnw_ref[...], lnb_ref[...], ws_ref, bst_ref)
        dya = dy_ref[...]
        dp_ref[:, 0:DG] = (dya * mixed * dug).astype(BF16)
        dmixed = dya * ug
        dbs_acc[...] += dmixed
        dmb = dmixed.astype(BF16)
        r = lax.broadcasted_iota(jnp.int32, (BLK, BLK), 0) // CH
        c = lax.broadcasted_iota(jnp.int32, (BLK, BLK), 1) // CH
        dvn = []
        for h in range(NH):
            sl = slice(h * HD, (h + 1) * HD)
            dws_ref[h] += jnp.where(r >= c, _dot(dmb[:, sl], vnb[:, sl], NT), 0.0)
            dvn.append(_dot(ws_ref[h], dmb[:, sl], TN))
        dvn = jnp.concatenate(dvn, axis=1)
        dln_ref[0:1, :] += jnp.sum(dvn * vhat, axis=0, keepdims=True)
        dln_ref[1:2, :] += jnp.sum(dvn, axis=0, keepdims=True)
        dvh = dvn * lnw_ref[...]
        dvgel = rstd * (dvh - jnp.mean(dvh, axis=-1, keepdims=True) - vhat * jnp.mean(dvh * vhat, axis=-1, keepdims=True))
        dp_ref[:, DG:2 * DG] = (dvgel * dvg).astype(BF16)

        @pl.when(i == nb - 1)
        def _():
            lane = lax.broadcasted_iota(jnp.int32, (BLK, HD), 1)
            out = jnp.zeros((BLK, HD), F32)
            for h in range(NH):
                out = out + jnp.where(lane == h, jnp.sum(dbs_acc[:, h * HD:(h + 1) * HD], axis=-1, keepdims=True), 0.0)
            dbs_ref[...] = out

    return pl.pallas_call(
        body, grid=(nb,),
        in_specs=[pl.BlockSpec((BLK, DG), lambda i: (i, 0)), pl.BlockSpec((BLK, DG), lambda i: (i, 1)),
                  pl.BlockSpec((BLK, DG), lambda i: (i, 0)),
                  _full((NH, BLK, BLK)), _full((BLK, NH)), _full((1, DG)), _full((1, DG))],
        out_specs=[pl.BlockSpec((BLK, 2 * DG), lambda i: (i, 2)), _full((NH, BLK, BLK)), _full((BLK, HD)), _full((8, DG))],
        out_shape=[SDS((T, DIN), BF16), SDS((NH, BLK, BLK), F32), SDS((BLK, HD), F32), SDS((8, DG), F32)],
        scratch_shapes=[pltpu.VMEM((BLK, DG), F32)],
        compiler_params=_arb(), name="gmlp_bwd")(proj, proj, dycat, ws_b, bst, lnw, lnb)


def _hgrn_bwd(proj, o_pre, at_all, st_all, dycat, lower_bounds, gn_w, dproj):
    T = proj.shape[0]
    nc = T // CH

    def body(q_ref, f_ref, i_ref, g_ref, o_ref, at_ref, st_ref, dy_ref, lbp_ref, gn_ref, dp_any,
             dp_ref, dlb_ref, dgn_ref, ds_scr, b_scr, q_scr, dqi_scr):
        i = pl.program_id(0)

        @pl.when(i == 0)
        def _():
            ds_scr[...] = jnp.zeros_like(ds_scr)
            dlb_ref[...] = jnp.zeros_like(dlb_ref)
            dgn_ref[...] = jnp.zeros_like(dgn_ref)

        lb, omlb = _lower_bound(lbp_ref)
        q = q_ref[...]
        v = i_ref[...]
        g = g_ref[...]
        sq, qf, sig, f, k, b = _hgrn_prep(q, f_ref[...], lb, omlb)
        eb = jnp.exp(b)
        bl = b[CH - 1:CH, :]
        ebl = jnp.exp(bl)
        ekd = jnp.exp(bl - b)
        kd = k * ekd
        rows = lax.broadcasted_iota(jnp.int32, (CH, CH), 0)
        lanes = lax.broadcasted_iota(jnp.int32, (CH, CH), 1)
        row1 = lax.broadcasted_iota(jnp.int32, (CH, 1), 0)
        upper = (lanes >= rows).astype(F32)
        dgn = jnp.zeros((1, HD), F32)
        for h in range(NH):
            sl = slice(h * HD, (h + 1) * HD)
            b_scr[h] = b[:, sl]
            q_scr[h] = qf[:, sl]
            o = o_ref[:, sl]
            ro = _rms(o)
            oh = o * ro
            gh = g[:, sl]
            sg = _sigmoid(gh)
            dyb = dy_ref[:, sl]
            dgate = dyb * (oh * gn_ref[...])
            dg = dgate * (sg * (1.0 + gh * (1.0 - sg)))
            don = dyb * (gh * sg)
            dgn = dgn + jnp.sum(don * oh, axis=0, keepdims=True)
            do = _rms_bwd(oh, ro, don * gn_ref[...])
            dob = do.astype(BF16)
            vb = v[:, sl].astype(BF16)
            st0 = st_ref[0, h]
            dst1 = ds_scr[h]
            dst1b = dst1.astype(BF16)
            qfh, kh, bh, ebh, kdh = qf[:, sl], k[:, sl], b[:, sl], eb[:, sl], kd[:, sl]
            qe = qfh * ebh
            dqe = _dot(dob, st0.astype(BF16))
            ds_scr[h] = dst1 * ebl[:, sl] + _dot(dob, qe.astype(BF16), TN)
            dbl = ebl[:, sl] * jnp.sum(st0 * dst1, axis=0, keepdims=True)
            dkd = _dot(vb, dst1b)
            dv = _dot(at_ref[0, h].astype(BF16), dob) + _dot(kdh.astype(BF16), dst1b, NT)
            dat = jnp.where(rows <= lanes, _dot(vb, dob, NT), 0.0)

            def step(i, dki):
                base = pl.multiple_of(i * UNROLL, UNROLL)
                bt8 = b_scr[h, pl.ds(base, UNROLL), :]
                qt8 = q_scr[h, pl.ds(base, UNROLL), :]
                dq_rows = []
                for j in range(UNROLL):
                    t = base + j
                    dm = jnp.exp(jnp.minimum(bt8[j:j + 1, :] - bh, 0.0))
                    col = jnp.sum(jnp.where(lanes == t, dat, 0.0), axis=-1, keepdims=True)
                    xm = col * dm
                    dq_rows.append(jnp.sum(xm * kh, axis=0, keepdims=True))
                    dki = dki + xm * qt8[j:j + 1, :]
                dqi_scr[pl.ds(base, UNROLL), :] = jnp.concatenate(dq_rows, axis=0)
                return dki

            dki = lax.fori_loop(0, CH // UNROLL, step, jnp.zeros((CH, HD), F32))
            dqi = dqi_scr[...]
            dqf = dqe * ebh + dqi
            dk = dkd * ekd[:, sl] + dki
            dkk = dkd * kdh
            db = dqe * qe + qfh * dqi - kh * dki - dkk
            db = db + jnp.where(row1 == CH - 1, dbl + jnp.sum(dkk, axis=0, keepdims=True), 0.0)
            dlf = _dot(upper, db, precision=HIGHEST)
            df = dlf / f[:, sl] - dk
            sgf = sig[:, sl]
            dlb_ref[0:1, sl] += jnp.sum(df * (1.0 - sgf), axis=0, keepdims=True)
            dfl = df * omlb[:, sl] * sgf * (1.0 - sgf)
            sqh = sq[:, sl]
            dq = dqf * (sqh * (1.0 + q[:, sl] * (1.0 - sqh)))
            dp_ref[:, h * HD:(h + 1) * HD] = dq.astype(BF16)
            dp_ref[:, DH + h * HD:DH + (h + 1) * HD] = dfl.astype(BF16)
            dp_ref[:, 2 * DH + h * HD:2 * DH + (h + 1) * HD] = dv.astype(BF16)
            dp_ref[:, 3 * DH + h * HD:3 * DH + (h + 1) * HD] = dg.astype(BF16)
        dgn_ref[0:1, :] += dgn

        @pl.when(i == nc - 1)
        def _():
            gl = dlb_ref[0:1, :] * lb * omlb
            dlb_ref[0:1, :] = gl
            dlb_ref[1:2, :] = -gl

    rev = lambda j: pl.BlockSpec((CH, DH), lambda c: (nc - 1 - c, j))
    return pl.pallas_call(
        body, grid=(nc,),
        in_specs=[rev(2), rev(3), rev(4), rev(5), rev(0),
                  pl.BlockSpec((1, NH, CH, CH), lambda c: (nc - 1 - c, 0, 0, 0)),
                  pl.BlockSpec((1, NH, HD, HD), lambda c: (nc - 1 - c, 0, 0, 0)),
                  rev(1), _full((2, DH)), _full((1, HD)), pl.BlockSpec(memory_space=pl.ANY)],
        out_specs=[pl.BlockSpec((CH, 4 * DH), lambda c: (nc - 1 - c, 0)), _full((8, DH)), _full((8, HD))],
        out_shape=[SDS((T, DIN), BF16), SDS((8, DH), F32), SDS((8, HD), F32)],
        scratch_shapes=[pltpu.VMEM((NH, HD, HD), F32), pltpu.VMEM((NH, CH, HD), F32), pltpu.VMEM((NH, CH, HD), F32),
                        pltpu.VMEM((CH, HD), F32)],
        input_output_aliases={10: 0},
        compiler_params=_arb(), name="hgrn_bwd")(proj, proj, proj, proj, o_pre, at_all, st_all, dycat, lower_bounds, gn_w, dproj)


def _proj_in_bwd(dproj, x, dx1, nw, sc, w_in_b, tm):
    T = x.shape[0]

    def body(dp_ref, x_ref, dx1_ref, nw_ref, sc_ref, w_ref, gx_ref, acc_ref):
        @pl.when(pl.program_id(0) == 0)
        def _():
            acc_ref[...] = jnp.zeros_like(acc_ref)

        dh = _dot(dp_ref[:, 0:4 * DH], w_ref[:, 2 * DG:DIN], NT) + _dot(dp_ref[:, 4 * DH:DIN], w_ref[:, 0:2 * DG], NT)
        xv = x_ref[...]
        r = _rms(xv)
        xh = xv * r
        n1 = xh * nw_ref[...]
        acc_ref[0:1, :] += jnp.sum(dh, axis=0, keepdims=True)
        acc_ref[1:2, :] += jnp.sum(dh * n1, axis=0, keepdims=True)
        dn = dh * (1.0 + sc_ref[...])
        acc_ref[2:3, :] += jnp.sum(dn * xh, axis=0, keepdims=True)
        gx_ref[...] = dx1_ref[...] + _rms_bwd(xh, r, dn * nw_ref[...])

    row = lambda i: (i, 0)
    return pl.pallas_call(
        body, grid=(T // tm,),
        in_specs=[pl.BlockSpec((tm, DIN), row), pl.BlockSpec((tm, D), row), pl.BlockSpec((tm, D), row),
                  _full((1, D)), _full((1, D)), _resident((D, DIN))],
        out_specs=[pl.BlockSpec((tm, D), row), _full((8, D))],
        out_shape=[SDS((T, D), F32), SDS((8, D), F32)],
        compiler_params=_arb(), name="proj_in_bwd")(dproj, x, dx1, nw, sc, w_in_b)


def _wgrad(a, b, bk, bn, tt, name, b_col_block=None):
    T, K = a.shape
    N = b.shape[1]
    nn, nk, nt = N // bn, K // bk, T // tt
    bmap = (lambda n, k, t: (t, n)) if b_col_block is None else (lambda n, k, t: (t, b_col_block(n)))

    def body(a_ref, b_ref, o_ref):
        @pl.when(pl.program_id(2) == 0)
        def _():
            o_ref[...] = jnp.zeros_like(o_ref)

        o_ref[0] += _dot(a_ref[...], b_ref[...], TN)

    return pl.pallas_call(
        body, grid=(nn, nk, nt),
        in_specs=[pl.BlockSpec((tt, bk), lambda n, k, t: (t, k)), pl.BlockSpec((tt, bn), bmap)],
        out_specs=pl.BlockSpec((1, bk, bn), lambda n, k, t: (n, k, 0)),
        out_shape=SDS((nn, K, bn), F32),
        compiler_params=_arb(3), name=name)(a, b)


def _adam_math(w, g, m, v):
    m = B1 * m + (1.0 - B1) * g
    v = B2 * v + (1.0 - B2) * (g * g)
    m_hat = m / (1.0 - B1 ** STEP)
    v_hat = v / (1.0 - B2 ** STEP)
    return -LR * (m_hat / (jnp.sqrt(v_hat) + AEPS) + WD * w), m, v


def _adamw(w, g, m, v, rb, name):
    R, C = w.shape

    def body(w_ref, g_ref, m_ref, v_ref, d_out, m_out, v_out):
        d_out[...], m_out[...], v_out[...] = _adam_math(w_ref[...], g_ref[...], m_ref[...], v_ref[...])

    spec = pl.BlockSpec((rb, C), lambda i: (i, 0))
    return pl.pallas_call(
        body, grid=(R // rb,), in_specs=[spec] * 4, out_specs=[spec] * 3,
        out_shape=[SDS((R, C), F32)] * 3, compiler_params=_arb(), name=name)(w, g, m, v)


def _ada_forward(c_all, w_ada):
    n = w_ada.shape[1]

    def body(c_ref, w_ref, ca_ref, p_ref):
        cv = c_ref[...]
        ca = cv * _sigmoid(cv)
        ca_ref[...] = ca
        p_ref[...] = _dot(ca, w_ref[...], precision=HIGHEST)

    return pl.pallas_call(
        body, grid=(n // 512,),
        in_specs=[_full((N_DEV, D)), pl.BlockSpec((D, 512), lambda i: (0, i))],
        out_specs=[_full((N_DEV, D)), pl.BlockSpec((N_DEV, 512), lambda i: (0, i))],
        out_shape=[SDS((N_DEV, D), F32), SDS((N_DEV, n), F32)],
        compiler_params=_arb(), name="ada_forward")(c_all, w_ada)


def _ada_wgrad_adam(cact_t, dada, w, m, v):
    R, C = w.shape
    rb = 256

    def body(c_ref, d_ref, w_ref, m_ref, v_ref, g_out, d_out, m_out, v_out):
        g = _dot(c_ref[...], d_ref[...], precision=HIGHEST)
        g_out[...] = g
        d_out[...], m_out[...], v_out[...] = _adam_math(w_ref[...], g, m_ref[...], v_ref[...])

    spec = pl.BlockSpec((rb, C), lambda i: (i, 0))
    return pl.pallas_call(
        body, grid=(R // rb,),
        in_specs=[pl.BlockSpec((rb, N_DEV), lambda i: (i, 0)), _full((N_DEV, C)), spec, spec, spec],
        out_specs=[spec] * 4, out_shape=[SDS((R, C), F32)] * 4,
        compiler_params=_arb(), name="ada_wgrad_adam")(cact_t, dada, w, m, v)


def _small_finalize(gathered, w, m, v):
    def body(ga_ref, w_ref, m_ref, v_ref, g_out, d_out, m_out, v_out):
        g = ga_ref[0:SMALL_ROWS, :]
        for dev in range(1, N_DEV):
            g = g + ga_ref[dev * SMALL_ROWS:(dev + 1) * SMALL_ROWS, :]
        g_out[...] = g
        d_out[...], m_out[...], v_out[...] = _adam_math(w_ref[...], g, m_ref[...], v_ref[...])

    return pl.pallas_call(
        body, out_shape=[SDS((SMALL_ROWS, D), F32)] * 4, name="small_finalize")(gathered, w, m, v)


def _position():
    x, y, c = lax.axis_index("x"), lax.axis_index("y"), lax.axis_index("c")
    return x, y, c


def _chip_at(x, y, r):
    return (x ^ (r >> 1), y ^ (r & 1))


def _all_gather_rows(block, name):
    m_per, n = block.shape

    def body(x_ref, out_ref, send_sems, recv_sems, local_sem):
        x, y, c = _position()
        me, sibling = (x, y, c), (x, y, 1 - c)
        chips = [_chip_at(x, y, r) for r in (1, 2, 3)]

        def rows(px, py, pc):
            return out_ref.at[pl.ds((4 * px + 2 * py + pc) * m_per, m_per), :]

        def copy(k, blk, to, src=None):
            return pltpu.make_async_remote_copy(
                src_ref=rows(*blk) if src is None else src, dst_ref=rows(*blk),
                send_sem=send_sems.at[k], recv_sem=recv_sems.at[k], device_id=to, device_id_type=MESH)

        mine = pltpu.make_async_copy(x_ref, rows(*me), local_sem)
        mine.start()
        first = [copy(0, me, sibling, src=x_ref)]
        first += [copy(1 + j, me, (*chip, c), src=x_ref) for j, chip in enumerate(chips)]
        for cp in first:
            cp.start()
        passed = [copy(4 + j, (*chip, c), sibling) for j, chip in enumerate(chips)]
        for j, chip in enumerate(chips):
            copy(1 + j, (*chip, c), me).wait_recv()
            passed[j].start()
        copy(0, sibling, me).wait_recv()
        for j, chip in enumerate(chips):
            copy(4 + j, (*chip, 1 - c), me).wait_recv()
        for cp in first + passed:
            cp.wait_send()
        mine.wait()

    return pl.pallas_call(
        body, out_shape=SDS((N_DEV * m_per, n), block.dtype),
        in_specs=[pl.BlockSpec(memory_space=pltpu.VMEM)], out_specs=pl.BlockSpec(memory_space=pltpu.VMEM),
        scratch_shapes=[pltpu.SemaphoreType.DMA((7,)), pltpu.SemaphoreType.DMA((7,)), pltpu.SemaphoreType.DMA],
        name=name)(block)


def _gather_weights(shards):
    nw = len(shards)

    def body(*refs):
        ins, outs = refs[:nw], refs[nw:2 * nw]
        send_sems, recv_sems, local_sems = refs[2 * nw:]
        x, y, c = _position()
        j = 2 * x + y
        started = []
        for w, (arr, axis) in enumerate(shards):
            size = arr.shape[axis]

            def slot(chip_idx, w=w, axis=axis, size=size):
                if axis == 0:
                    return outs[w].at[pl.ds(chip_idx * size, size), :]
                return outs[w].at[:, pl.ds(chip_idx * size, size)]

            local = pltpu.make_async_copy(ins[w], slot(j), local_sems.at[w])
            local.start()
            started.append(local)
            for r in (1, 2, 3):
                cx, cy = _chip_at(x, y, r)
                k = 3 * w + r - 1
                cp = pltpu.make_async_remote_copy(
                    src_ref=ins[w], dst_ref=slot(j), send_sem=send_sems.at[k], recv_sem=recv_sems.at[k],
                    device_id=(cx, cy, c), device_id_type=MESH)
                cp.start()
                started.append((cp, slot(j ^ r), k, w))
        for item in started:
            if isinstance(item, tuple):
                cp, from_slot, k, w = item
                cp.wait_send()
                pltpu.make_async_remote_copy(
                    src_ref=ins[w], dst_ref=from_slot, send_sem=send_sems.at[k], recv_sem=recv_sems.at[k],
                    device_id=(x, y, c), device_id_type=MESH).wait_recv()
            else:
                item.wait()

    out_shape = []
    for arr, axis in shards:
        shp = list(arr.shape)
        shp[axis] *= N_CHIPS
        out_shape.append(SDS(tuple(shp), arr.dtype))
    anyspec = pl.BlockSpec(memory_space=pl.ANY)
    return pl.pallas_call(
        body, out_shape=out_shape, in_specs=[anyspec] * nw, out_specs=[anyspec] * nw,
        scratch_shapes=[pltpu.SemaphoreType.DMA((3 * nw,)), pltpu.SemaphoreType.DMA((3 * nw,)),
                        pltpu.SemaphoreType.DMA((nw,))],
        name="gather_weights")(*[a for a, _ in shards])


def _exchange_core_halves(grads):
    nw = len(grads)

    def body(*refs):
        ins, outs = refs[:nw], refs[nw:2 * nw]
        send_sems, recv_sems = refs[2 * nw:]
        x, y, c = _position()
        cps = []
        for w in range(nw):
            cp = pltpu.make_async_remote_copy(
                src_ref=ins[w].at[:, 1 - c], dst_ref=outs[w], send_sem=send_sems.at[w], recv_sem=recv_sems.at[w],
                device_id=(x, y, 1 - c), device_id_type=MESH)
            cp.start()
            cps.append(cp)
        for cp in cps:
            cp.wait()

    anyspec = pl.BlockSpec(memory_space=pl.ANY)
    return pl.pallas_call(
        body, out_shape=[SDS((g.shape[0], g.shape[2], g.shape[3]), F32) for g in grads],
        in_specs=[anyspec] * nw, out_specs=[anyspec] * nw,
        scratch_shapes=[pltpu.SemaphoreType.DMA((nw,)), pltpu.SemaphoreType.DMA((nw,))],
        name="exchange_core_halves")(*grads)


def _add_core_halves(g4, recv, c_idx, rb, name):
    ns, _, rh, C = g4.shape

    def body(c_ref, g_ref, r_ref, o_ref):
        o_ref[...] = g_ref[0] + r_ref[...]

    return pl.pallas_call(
        body,
        grid_spec=pltpu.PrefetchScalarGridSpec(
            num_scalar_prefetch=1, grid=(ns, rh // rb),
            in_specs=[pl.BlockSpec((1, 1, rb, C), lambda s, i, cr: (s, cr[0], i, 0)),
                      pl.BlockSpec((1, rb, C), lambda s, i, cr: (s, i, 0))],
            out_specs=pl.BlockSpec((1, rb, C), lambda s, i, cr: (s, i, 0))),
        out_shape=SDS((ns, rh, C), F32), compiler_params=_arb(2), name=name)(c_idx, g4, recv)


def _exchange_chips(sums):
    nw = len(sums)

    def body(*refs):
        ins, outs = refs[:nw], refs[nw:2 * nw]
        send_sems, recv_sems, local_sems = refs[2 * nw:]
        x, y, c = _position()
        j = 2 * x + y
        started = []
        for w in range(nw):
            local = pltpu.make_async_copy(ins[w].at[j], outs[w].at[0], local_sems.at[w])
            local.start()
            started.append(local)
            for r in (1, 2, 3):
                cx, cy = _chip_at(x, y, r)
                k = 3 * w + r - 1
                cp = pltpu.make_async_remote_copy(
                    src_ref=ins[w].at[j ^ r], dst_ref=outs[w].at[r], send_sem=send_sems.at[k], recv_sem=recv_sems.at[k],
                    device_id=(cx, cy, c), device_id_type=MESH)
                cp.start()
                started.append(cp)
        for cp in started:
            cp.wait()

    anyspec = pl.BlockSpec(memory_space=pl.ANY)
    return pl.pallas_call(
        body, out_shape=[SDS(s.shape, F32) for s in sums], in_specs=[anyspec] * nw, out_specs=[anyspec] * nw,
        scratch_shapes=[pltpu.SemaphoreType.DMA((3 * nw,)), pltpu.SemaphoreType.DMA((3 * nw,)),
                        pltpu.SemaphoreType.DMA((nw,))],
        name="exchange_chips")(*sums)


def _add_chips(slots, order, rb, name):
    _, rh, C = slots.shape

    def body(o_ref, a_ref, b_ref, c_ref, d_ref, out_ref):
        out_ref[...] = ((a_ref[0] + b_ref[0]) + c_ref[0]) + d_ref[0]

    def spec(i):
        return pl.BlockSpec((1, rb, C), lambda t, o: (o[i], t, 0))

    return pl.pallas_call(
        body,
        grid_spec=pltpu.PrefetchScalarGridSpec(
            num_scalar_prefetch=1, grid=(rh // rb,),
            in_specs=[spec(0), spec(1), spec(2), spec(3)],
            out_specs=pl.BlockSpec((rb, C), lambda t, o: (t, 0))),
        out_shape=SDS((rh, C), F32), compiler_params=_arb(), name=name)(order, slots, slots, slots, slots)


def _share_halves(halves):
    nw = len(halves)

    def body(*refs):
        ins, outs = refs[:nw], refs[nw:2 * nw]
        send_sems, recv_sems, local_sems = refs[2 * nw:]
        x, y, c = _position()
        started = []
        for w in range(nw):
            local = pltpu.make_async_copy(ins[w], outs[w].at[c], local_sems.at[w])
            local.start()
            started.append(local)
            cp = pltpu.make_async_remote_copy(
                src_ref=ins[w], dst_ref=outs[w].at[c], send_sem=send_sems.at[w], recv_sem=recv_sems.at[w],
                device_id=(x, y, 1 - c), device_id_type=MESH)
            cp.start()
            started.append(cp)
        for cp in started:
            cp.wait()

    anyspec = pl.BlockSpec(memory_space=pl.ANY)
    return pl.pallas_call(
        body, out_shape=[SDS((2,) + h.shape, F32) for h in halves], in_specs=[anyspec] * nw, out_specs=[anyspec] * nw,
        scratch_shapes=[pltpu.SemaphoreType.DMA((nw,)), pltpu.SemaphoreType.DMA((nw,)), pltpu.SemaphoreType.DMA((nw,))],
        name="share_halves")(*halves)


def _pack_small(b_ada, norm1_w, norm2_w, final_norm_w, v_ln_w, v_ln_b, lower_bounds, b_s, gn_w, w_s):
    parts = [b_ada, norm1_w, norm2_w, final_norm_w, v_ln_w, v_ln_b, lower_bounds, b_s, gn_w,
             jnp.zeros((D - NH * BLK - HD,), F32), w_s, jnp.zeros(((SMALL_ROWS - 76) * D,), F32)]
    return jnp.concatenate([p.reshape(-1) for p in parts]).reshape(SMALL_ROWS, D)


def _unpack_small(p):
    return dict(
        b_ada=p[0:6].reshape(1, 6 * D), norm1_w=p[6:7], norm2_w=p[7:8], final_norm_w=p[8],
        v_ln_w=p[9:10, 0:DG], v_ln_b=p[9:10, DG:D], lower_bounds=p[10].reshape(2, DH),
        b_s=p[11, 0:NH * BLK].reshape(1, NH, BLK), gn_w=p[11:12, NH * BLK:NH * BLK + HD],
        w_s=p[12:76].reshape(1, NH, BLK, BLK))


def _row_block(r):
    for cand in (256, 176, 128, 64, 32, 16, 8):
        if r % cand == 0:
            return cand
    return r


def kernel(x, c, w_ada, b_ada, norm1_w, w_in, w_s, b_s, v_ln_w, v_ln_b, lower_bounds, gn_w, w_out, norm2_w, w_ffn_in, w_ffn_out, final_norm_w, loss_target, m_w_ada, m_b_ada, m_norm1_w, m_w_in, m_w_s, m_b_s, m_v_ln_w, m_v_ln_b, m_lower_bounds, m_gn_w, m_w_out, m_norm2_w, m_w_ffn_in, m_w_ffn_out, m_final_norm_w, v_w_ada, v_b_ada, v_norm1_w, v_w_in, v_w_s, v_b_s, v_v_ln_w, v_v_ln_b, v_lower_bounds, v_gn_w, v_w_out, v_norm2_w, v_w_ffn_in, v_w_ffn_out, v_final_norm_w):
    T = x.shape[1]
    tm = min(256, T)
    px, py, pc = _position()
    chip = 2 * px + py
    me = 4 * px + 2 * py + pc
    x2d = x.reshape(T, D)
    tgt = loss_target.reshape(T, D)

    c_all = _all_gather_rows(jnp.broadcast_to(c, (8, D)), "gather_c").reshape(N_DEV, 8, D)[:, 0, :]
    cact, ada_part = _ada_forward(c_all, w_ada[0])
    n_ada = ada_part.shape[1]
    ada_all = _all_gather_rows(ada_part, "gather_ada").reshape(N_CHIPS, 2, N_DEV, n_ada)[:, 0]
    ada = lax.dynamic_index_in_dim(ada_all, me, axis=1, keepdims=False).reshape(1, 6 * D) + b_ada
    sh1, sc1, g1, sh2, sc2, g2 = [ada[:, i * D:(i + 1) * D] for i in range(6)]

    w_in_b, w_out_b, w_fi_b, w_fo_b = _gather_weights(
        [(w_in[0].astype(BF16), 1), (w_out[0].astype(BF16), 0), (w_ffn_in[0].astype(BF16), 1), (w_ffn_out[0].astype(BF16), 0)])

    rr = lax.broadcasted_iota(jnp.int32, (BLK, BLK), 0) // CH
    cc = lax.broadcasted_iota(jnp.int32, (BLK, BLK), 1) // CH
    ws_b = jnp.where((rr >= cc)[None], w_s[0], 0.0).astype(BF16)
    bst = b_s[0].T
    lnw, lnb = v_ln_w, v_ln_b
    nw1, nw2, fw = norm1_w, norm2_w, final_norm_w.reshape(1, D)

    h1, proj = _proj_in(x2d, nw1, sc1, sh1, w_in_b, tm)
    ycat = _gmlp_fwd(proj, ws_b, bst, lnw, lnb)
    ycat, o_pre, at_all, st_all = _hgrn_fwd(proj, lower_bounds, gn_w, ycat)

    dycat, dx1, h2, act, dff, dgu, dmix, acc2 = _token_local(
        x2d, ycat, tgt, g1, nw2, sc2, sh2, g2, fw, w_out_b, w_fi_b, w_fo_b, tm)

    dproj, dws, dbs, dln = _gmlp_bwd(proj, dycat, ws_b, bst, lnw, lnb)
    dproj, dlb, dgn = _hgrn_bwd(proj, o_pre, at_all, st_all, dycat, lower_bounds, gn_w, dproj)
    grad_x, acc1 = _proj_in_bwd(dproj, x2d, dx1, nw1, sc1, w_in_b, tm)

    tt = min(512, T)
    g_in = _wgrad(h1, dproj, D, 256, tt, "wgrad_in", b_col_block=lambda n: (n + 8) % 12)
    g_in = g_in.reshape(N_CHIPS, 3, D, 256).transpose(0, 2, 1, 3).reshape(N_CHIPS, D, 768)
    g_out = _wgrad(ycat, dmix, D, D, tt, "wgrad_out").reshape(N_CHIPS, D // N_CHIPS, D)
    g_fi = _wgrad(h2, dgu, D, FFB, tt, "wgrad_ffn_in")
    g_fo = _wgrad(act, dff, FFB, D, tt, "wgrad_ffn_out").reshape(N_CHIPS, DFF // N_CHIPS, D)

    big = [g_in, g_out, g_fi, g_fo]
    g4 = [g.reshape(N_CHIPS, 2, g.shape[1] // 2, g.shape[2]) for g in big]
    recv = _exchange_core_halves(g4)
    c_idx = jnp.reshape(pc, (1,)).astype(jnp.int32)
    names = ["in", "out", "ffn_in", "ffn_out"]
    sums = [_add_core_halves(a, b, c_idx, _row_block(a.shape[2]), "add_core_" + n) for a, b, n in zip(g4, recv, names)]
    slots = _exchange_chips(sums)
    order = (chip ^ jnp.arange(N_CHIPS, dtype=jnp.int32)).astype(jnp.int32)
    halves = [_add_chips(s, order, _row_block(s.shape[1]), "add_chips_" + n) for s, n in zip(slots, names)]
    shard_grads = [h.reshape(2 * h.shape[1], h.shape[2]) for h in _share_halves(halves)]

    big_w = [(w_in, m_w_in, v_w_in), (w_out, m_w_out, v_w_out), (w_ffn_in, m_w_ffn_in, v_w_ffn_in),
             (w_ffn_out, m_w_ffn_out, v_w_ffn_out)]
    big_out = []
    for g, (w, m, v), n in zip(shard_grads, big_w, names):
        d_, m_, v_ = _adamw(w[0], g, m[0], v[0], _row_block(g.shape[0]), "adamw_" + n)
        big_out.append((g[None], d_[None], m_[None], v_[None]))

    d_ada = jnp.stack([acc1[0], acc1[1], acc2[5], acc2[2], acc2[1], acc2[0]]).reshape(1, 6 * D)
    small_g = _pack_small(d_ada, acc1[2], acc2[3], acc2[4], dln[0], dln[1], dlb[0:2], dbs[:, 0:NH].T, dgn[0], dws)
    gathered = _all_gather_rows(small_g, "gather_small")
    sw = _pack_small(b_ada, norm1_w, norm2_w, final_norm_w, v_ln_w, v_ln_b, lower_bounds, b_s, gn_w, w_s)
    sm = _pack_small(m_b_ada, m_norm1_w, m_norm2_w, m_final_norm_w, m_v_ln_w, m_v_ln_b, m_lower_bounds, m_b_s, m_gn_w, m_w_s)
    sv = _pack_small(v_b_ada, v_norm1_w, v_norm2_w, v_final_norm_w, v_v_ln_w, v_v_ln_b, v_lower_bounds, v_b_s, v_gn_w, v_w_s)
    small = [_unpack_small(p) for p in _small_finalize(gathered, sw, sm, sv)]

    dada_all = gathered.reshape(N_DEV, SMALL_ROWS, D)[:, 0:6, :].reshape(N_DEV, 6 * D)
    dada = lax.dynamic_slice_in_dim(dada_all, chip * n_ada, n_ada, axis=1)
    ada_out = [o[None] for o in _ada_wgrad_adam(cact.T, dada, w_ada[0], m_w_ada[0], v_w_ada[0])]

    loss = lax.psum(jnp.sum(acc2[6]), ("x", "y", "c"))

    order_names = ['w_ada', 'b_ada', 'norm1_w', 'w_in', 'w_s', 'b_s', 'v_ln_w', 'v_ln_b', 'lower_bounds', 'gn_w',
                   'w_out', 'norm2_w', 'w_ffn_in', 'w_ffn_out', 'final_norm_w']
    big_idx = {'w_in': 0, 'w_out': 1, 'w_ffn_in': 2, 'w_ffn_out': 3}
    outs = [loss, grad_x.reshape(1, T, D)]
    for kind in range(4):
        for n in order_names:
            if n == 'w_ada':
                outs.append(ada_out[kind])
            elif n in big_idx:
                outs.append(big_out[big_idx[n]][kind])
            else:
                outs.append(small[kind][n])
    return tuple(outs)
```

```python
import functools

import jax
import jax.numpy as jnp
import numpy as np
from jax import lax
from jax.experimental import pallas as pl
from jax.experimental.pallas import tpu as pltpu

F32 = jnp.float32
BF16 = jnp.bfloat16
SDS = jax.ShapeDtypeStruct
MESH = pl.DeviceIdType.MESH
HIGHEST = lax.Precision.HIGHEST

D = 1024
DG = 512
DH = 512
NH = 4
HD = 128
BLK = 128
CH = 64
DFF = 2816
DIN = 3072
FFB = 1408
LEVELS = (64, 32, 16, 8, 4, 2)
N_CHIPS = 4
N_DEV = 8
EPS = 1e-6
LR, B1, B2, AEPS, WD, STEP = 0.001, 0.9, 0.999, 1e-08, 0.01, 10
SMALL_ROWS = 80

NT = (((1,), (1,)), ((), ()))
TN = (((0,), (0,)), ((), ()))


def _full(shape):
    nd = len(shape)
    return pl.BlockSpec(shape, lambda *_: (0,) * nd)


def _resident(shape):
    nd = len(shape)
    return pl.BlockSpec(shape, lambda *_: (0,) * nd, pipeline_mode=pl.Buffered(1))


def _arb(n=1):
    return pltpu.CompilerParams(dimension_semantics=("arbitrary",) * n)


def _dot(a, b, dims=None, precision=None):
    if dims is None:
        return jnp.dot(a, b, preferred_element_type=F32, precision=precision)
    return lax.dot_general(a, b, dims, preferred_element_type=F32, precision=precision)


def _sigmoid(x):
    return jax.nn.sigmoid(x)


def _gelu_parts(x):
    cdf = 0.5 * (1.0 + lax.erf(x * 0.7071067811865476))
    pdf = jnp.exp(-0.5 * x * x) * 0.3989422804014327
    return x * cdf, cdf + x * pdf


def _rms(x):
    return lax.rsqrt(jnp.mean(x * x, axis=-1, keepdims=True) + EPS)


def _rms_bwd(xhat, r, gw):
    return r * (gw - xhat * jnp.mean(xhat * gw, axis=-1, keepdims=True))


def _lower_bound(lbp_ref):
    l0, l1 = lbp_ref[0:1, :], lbp_ref[1:2, :]
    m = jnp.maximum(l0, l1)
    e0, e1 = jnp.exp(l0 - m), jnp.exp(l1 - m)
    return e0 / (e0 + e1), e1 / (e0 + e1)


def _proj_in(x, nw, sc, sh, w_in_b, tm):
    T = x.shape[0]

    def body(x_ref, nw_ref, sc_ref, sh_ref, w_ref, h_ref, p_ref):
        xv = x_ref[...]
        h = ((xv * _rms(xv)) * nw_ref[...]) * (1.0 + sc_ref[...]) + sh_ref[...]
        hb = h.astype(BF16)
        h_ref[...] = hb
        p_ref[...] = _dot(hb, w_ref[...])

    row = lambda i: (i, 0)
    return pl.pallas_call(
        body, grid=(T // tm,),
        in_specs=[pl.BlockSpec((tm, D), row), _full((1, D)), _full((1, D)), _full((1, D)), _resident((D, DIN))],
        out_specs=[pl.BlockSpec((tm, D), row), pl.BlockSpec((tm, DIN), row)],
        out_shape=[SDS((T, D), BF16), SDS((T, DIN), F32)],
        compiler_params=_arb(), name="proj_in")(x, nw, sc, sh, w_in_b)


def _gmlp_common(u, v, lnw, lnb, ws_ref, bst_ref):
    ug, dug = _gelu_parts(u)
    vg, dvg = _gelu_parts(v)
    mu = jnp.mean(vg, axis=-1, keepdims=True)
    vc = vg - mu
    rstd = lax.rsqrt(jnp.mean(vc * vc, axis=-1, keepdims=True) + EPS)
    vhat = vc * rstd
    vn = vhat * lnw + lnb
    vnb = vn.astype(BF16)
    mixed = []
    for h in range(NH):
        sl = slice(h * HD, (h + 1) * HD)
        mixed.append(_dot(ws_ref[h], vnb[:, sl]) + bst_ref[:, h:h + 1])
    return ug, dug, dvg, rstd, vhat, vnb, jnp.concatenate(mixed, axis=1)


def _gmlp_fwd(proj, ws_b, bst, lnw, lnb):
    T = proj.shape[0]

    def body(u_ref, v_ref, ws_ref, bst_ref, lnw_ref, lnb_ref, y_ref):
        ug, _, _, _, _, _, mixed = _gmlp_common(u_ref[...], v_ref[...], lnw_ref[...], lnb_ref[...], ws_ref, bst_ref)
        y_ref[...] = (ug * mixed).astype(BF16)

    return pl.pallas_call(
        body, grid=(T // BLK,),
        in_specs=[pl.BlockSpec((BLK, DG), lambda i: (i, 0)), pl.BlockSpec((BLK, DG), lambda i: (i, 1)),
                  _full((NH, BLK, BLK)), _full((BLK, NH)), _full((1, DG)), _full((1, DG))],
        out_specs=pl.BlockSpec((BLK, DG), lambda i: (i, 0)),
        out_shape=SDS((T, D), BF16),
        compiler_params=_arb(), name="gmlp_fwd")(proj, proj, ws_b, bst, lnw, lnb)


def _hgrn_tables():
    t = np.arange(CH)[:, None]
    j = np.arange(CH)[None, :]
    blocks = [j <= t, j > t]
    masks = []
    for n in LEVELS:
        mid = t - t % n + n // 2
        blocks.append(np.where(t >= mid, (j >= mid) & (j <= t), (j > t) & (j < mid)))
        masks.append((t // n == j // n) & (t % n >= n // 2) & (j % n < n // 2))
    w = np.concatenate(blocks, axis=0).astype(np.float32)
    m = np.stack(masks).astype(np.float32)
    return (jnp.asarray(w, BF16), jnp.asarray(w.T, BF16), jnp.asarray(m), jnp.asarray(m.transpose(0, 2, 1)))


def _split_dot(w, x, parts):
    acc = None
    for _ in range(parts):
        piece = x.astype(BF16)
        term = _dot(w, piece)
        acc = term if acc is None else acc + term
        x = x - piece.astype(F32)
    return acc


def _hgrn_gates(q, fl, lb, omlb, w_ref):
    sq = _sigmoid(q)
    qf = q * sq
    sig = _sigmoid(fl)
    f = lb + omlb * sig
    k = 1.0 - f
    e = jnp.exp(_split_dot(w_ref[...], jnp.log(f), 3))
    return sq, qf, sig, f, k, e


def _level_factor(e, li, sl, row, qh, kh):
    el = e[(2 + li) * CH:(3 + li) * CH, sl]
    up = (row & (LEVELS[li] // 2)) != 0
    return el, up, el * jnp.where(up, qh, kh)


def _hgrn_fwd(proj, lower_bounds, gn_w, ycat, tables):
    T = proj.shape[0]
    nc = T // CH
    w_st, _, masks, _ = tables

    def body(q_ref, f_ref, i_ref, g_ref, lbp_ref, gn_ref, w_ref, m_ref, ycat_any, y_ref, o_ref, a_ref, st_ref, s_scr):
        @pl.when(pl.program_id(0) == 0)
        def _():
            s_scr[...] = jnp.zeros_like(s_scr)

        lb, omlb = _lower_bound(lbp_ref)
        v = i_ref[...]
        g = g_ref[...]
        _, qf, _, _, k, e = _hgrn_gates(q_ref[...], f_ref[...], lb, omlb, w_ref)
        eb = e[0:CH]
        ebl = eb[CH - 1:CH, :]
        kd = k * e[CH:2 * CH]
        row = lax.broadcasted_iota(jnp.int32, (CH, 1), 0)
        eye = lax.broadcasted_iota(jnp.int32, (CH, CH), 0) == lax.broadcasted_iota(jnp.int32, (CH, CH), 1)
        for h in range(NH):
            sl = slice(h * HD, (h + 1) * HD)
            st0 = s_scr[h]
            st_ref[0, h] = st0
            qh, kh = qf[:, sl], k[:, sl]
            inter = _dot((qh * eb[:, sl]).astype(BF16), st0.astype(BF16), NT)
            a = jnp.where(eye, jnp.sum(qh * kh, axis=-1, keepdims=True), 0.0)
            for li in range(len(LEVELS)):
                _, _, y = _level_factor(e, li, sl, row, qh, kh)
                yb = y.astype(BF16)
                a = a + m_ref[li] * _dot(yb, yb, NT)
            a_ref[0, h] = a
            vb = v[:, sl].astype(BF16)
            o = inter + _dot(a.astype(BF16), vb)
            s_scr[h] = st0 * ebl[:, sl] + _dot(vb, kd[:, sl].astype(BF16), TN)
            o_ref[:, sl] = o
            gh = g[:, sl]
            y_ref[:, sl] = (((o * _rms(o)) * gn_ref[...]) * (gh * _sigmoid(gh))).astype(BF16)

    blk = lambda j: pl.BlockSpec((CH, DH), lambda c: (c, j))
    return pl.pallas_call(
        body, grid=(nc,),
        in_specs=[blk(2), blk(3), blk(4), blk(5), _full((2, DH)), _full((1, HD)),
                  _full(w_st.shape), _full(masks.shape), pl.BlockSpec(memory_space=pl.ANY)],
        out_specs=[pl.BlockSpec((CH, DH), lambda c: (c, 1)),
                   pl.BlockSpec((CH, DH), lambda c: (c, 0)),
                   pl.BlockSpec((1, NH, CH, CH), lambda c: (c, 0, 0, 0)),
                   pl.BlockSpec((1, NH, HD, HD), lambda c: (c, 0, 0, 0))],
        out_shape=[SDS((T, D), BF16), SDS((T, DH), F32), SDS((nc, NH, CH, CH), F32), SDS((nc, NH, HD, HD), F32)],
        scratch_shapes=[pltpu.VMEM((NH, HD, HD), F32)],
        input_output_aliases={8: 0},
        compiler_params=_arb(), name="hgrn_fwd")(proj, proj, proj, proj, lower_bounds, gn_w, w_st, masks, ycat)


def _token_local(x, ycat, tgt, g1, nw2, sc2, sh2, g2, fw, w_out_b, w_fi_b, w_fo_b, tm):
    T = x.shape[0]
    inv_d = 1.0 / D

    def body(x_ref, y_ref, t_ref, g1_ref, nw2_ref, sc2_ref, sh2_ref, g2_ref, fw_ref, wo_ref, wfi_ref, wfo_ref,
             dy_ref, dx1_ref, h2_ref, act_ref, dff_ref, dgu_ref, dmix_ref, acc_ref):
        @pl.when(pl.program_id(0) == 0)
        def _():
            acc_ref[...] = jnp.zeros_like(acc_ref)

        def acc(row, val):
            acc_ref[row:row + 1, :] += jnp.sum(val, axis=0, keepdims=True)

        g1v, g2v = g1_ref[...], g2_ref[...]
        mix = _dot(y_ref[...], wo_ref[...])
        x1 = x_ref[...] + g1v * mix
        r2 = _rms(x1)
        xh2 = x1 * r2
        n2 = xh2 * nw2_ref[...]
        osc2 = 1.0 + sc2_ref[...]
        h2b = (n2 * osc2 + sh2_ref[...]).astype(BF16)
        h2_ref[...] = h2b
        ff = jnp.zeros((tm, D), F32)
        saved = []
        for kb in range(DFF // FFB):
            gate = _dot(h2b, wfi_ref[:, kb * FFB:(kb + 1) * FFB])
            up = _dot(h2b, wfi_ref[:, DFF + kb * FFB:DFF + (kb + 1) * FFB])
            sg = _sigmoid(gate)
            actb = (gate * sg * up).astype(BF16)
            act_ref[:, kb * FFB:(kb + 1) * FFB] = actb
            ff = ff + _dot(actb, wfo_ref[kb * FFB:(kb + 1) * FFB, :])
            saved.append((gate, up, sg))
        x2 = x1 + g2v * ff
        r3 = _rms(x2)
        xh3 = x2 * r3
        err = xh3 * fw_ref[...] - t_ref[...]
        acc(6, (0.5 * inv_d) * err * err)
        dy = err * inv_d
        acc(4, dy * xh3)
        dx2 = _rms_bwd(xh3, r3, dy * fw_ref[...])
        acc(0, dx2 * ff)
        dffb = (dx2 * g2v).astype(BF16)
        dff_ref[...] = dffb
        dh2 = jnp.zeros((tm, D), F32)
        for kb in range(DFF // FFB):
            gate, up, sg = saved[kb]
            da = _dot(dffb, wfo_ref[kb * FFB:(kb + 1) * FFB, :], NT)
            dgate = (da * up * (sg * (1.0 + gate * (1.0 - sg)))).astype(BF16)
            dup = (da * gate * sg).astype(BF16)
            dgu_ref[:, kb * FFB:(kb + 1) * FFB] = dgate
            dgu_ref[:, DFF + kb * FFB:DFF + (kb + 1) * FFB] = dup
            dh2 = dh2 + _dot(dgate, wfi_ref[:, kb * FFB:(kb + 1) * FFB], NT)
            dh2 = dh2 + _dot(dup, wfi_ref[:, DFF + kb * FFB:DFF + (kb + 1) * FFB], NT)
        acc(2, dh2)
        acc(1, dh2 * n2)
        dn2 = dh2 * osc2
        acc(3, dn2 * xh2)
        dx1 = dx2 + _rms_bwd(xh2, r2, dn2 * nw2_ref[...])
        acc(5, dx1 * mix)
        dmixb = (dx1 * g1v).astype(BF16)
        dmix_ref[...] = dmixb
        dy_ref[...] = _dot(dmixb, wo_ref[...], NT)
        dx1_ref[...] = dx1

    row = lambda i: (i, 0)
    vec = _full((1, D))
    return pl.pallas_call(
        body, grid=(T // tm,),
        in_specs=[pl.BlockSpec((tm, D), row), pl.BlockSpec((tm, D), row), pl.BlockSpec((tm, D), row),
                  vec, vec, vec, vec, vec, vec,
                  _resident((D, D)), _resident((D, 2 * DFF)), _resident((DFF, D))],
        out_specs=[pl.BlockSpec((tm, D), row), pl.BlockSpec((tm, D), row), pl.BlockSpec((tm, D), row),
                   pl.BlockSpec((tm, DFF), row), pl.BlockSpec((tm, D), row), pl.BlockSpec((tm, 2 * DFF), row),
                   pl.BlockSpec((tm, D), row), _full((8, D))],
        out_shape=[SDS((T, D), F32), SDS((T, D), F32), SDS((T, D), BF16), SDS((T, DFF), BF16), SDS((T, D), BF16),
                   SDS((T, 2 * DFF), BF16), SDS((T, D), BF16), SDS((8, D), F32)],
        compiler_params=_arb(), name="token_local")(x, ycat, tgt, g1, nw2, sc2, sh2, g2, fw, w_out_b, w_fi_b, w_fo_b)


def _gmlp_bwd(proj, dycat, ws_b, bst, lnw, lnb):
    T = proj.shape[0]
    nb = T // BLK

    def body(u_ref, v_ref, dy_ref, ws_ref, bst_ref, lnw_ref, lnb_ref, dp_ref, dws_ref, dbs_ref, dln_ref, dbs_acc):
        i = pl.program_id(0)

        @pl.when(i == 0)
        def _():
            dws_ref[...] = jnp.zeros_like(dws_ref)
            dln_ref[...] = jnp.zeros_like(dln_ref)
            dbs_acc[...] = jnp.zeros_like(dbs_acc)

        ug, dug, dvg, rstd, vhat, vnb, mixed = _gmlp_common(u_ref[...], v_ref[...], lnw_ref[...], lnb_ref[...], ws_ref, bst_ref)
        dya = dy_ref[...]
        dp_ref[:, 0:DG] = (dya * mixed * dug).astype(BF16)
        dmixed = dya * ug
        dbs_acc[...] += dmixed
        dmb = dmixed.astype(BF16)
        r = lax.broadcasted_iota(jnp.int32, (BLK, BLK), 0) // CH
        c = lax.broadcasted_iota(jnp.int32, (BLK, BLK), 1) // CH
        dvn = []
        for h in range(NH):
            sl = slice(h * HD, (h + 1) * HD)
            dws_ref[h] += jnp.where(r >= c, _dot(dmb[:, sl], vnb[:, sl], NT), 0.0)
            dvn.append(_dot(ws_ref[h], dmb[:, sl], TN))
        dvn = jnp.concatenate(dvn, axis=1)
        dln_ref[0:1, :] += jnp.sum(dvn * vhat, axis=0, keepdims=True)
        dln_ref[1:2, :] += jnp.sum(dvn, axis=0, keepdims=True)
        dvh = dvn * lnw_ref[...]
        dvgel = rstd * (dvh - jnp.mean(dvh, axis=-1, keepdims=True) - vhat * jnp.mean(dvh * vhat, axis=-1, keepdims=True))
        dp_ref[:, DG:2 * DG] = (dvgel * dvg).astype(BF16)

        @pl.when(i == nb - 1)
        def _():
            lane = lax.broadcasted_iota(jnp.int32, (BLK, HD), 1)
            out = jnp.zeros((BLK, HD), F32)
            for h in range(NH):
                out = out + jnp.where(lane == h, jnp.sum(dbs_acc[:, h * HD:(h + 1) * HD], axis=-1, keepdims=True), 0.0)
            dbs_ref[...] = out

    return pl.pallas_call(
        body, grid=(nb,),
        in_specs=[pl.BlockSpec((BLK, DG), lambda i: (i, 0)), pl.BlockSpec((BLK, DG), lambda i: (i, 1)),
                  pl.BlockSpec((BLK, DG), lambda i: (i, 0)),
                  _full((NH, BLK, BLK)), _full((BLK, NH)), _full((1, DG)), _full((1, DG))],
        out_specs=[pl.BlockSpec((BLK, 2 * DG), lambda i: (i, 2)), _full((NH, BLK, BLK)), _full((BLK, HD)), _full((8, DG))],
        out_shape=[SDS((T, DIN), BF16), SDS((NH, BLK, BLK), F32), SDS((BLK, HD), F32), SDS((8, DG), F32)],
        scratch_shapes=[pltpu.VMEM((BLK, DG), F32)],
        compiler_params=_arb(), name="gmlp_bwd")(proj, proj, dycat, ws_b, bst, lnw, lnb)


def _hgrn_bwd(proj, o_pre, a_all, st_all, dycat, lower_bounds, gn_w, dproj, tables):
    T = proj.shape[0]
    nc = T // CH
    w_st, w_st_t, masks, masks_t = tables
    n_lev = len(LEVELS)

    def body(q_ref, f_ref, i_ref, g_ref, o_ref, a_ref, st_ref, dy_ref, lbp_ref, gn_ref, w_ref, wt_ref, m_ref, mt_ref,
             dp_any, dp_ref, dlb_ref, dgn_ref, ds_scr, dx_scr):
        i = pl.program_id(0)

        @pl.when(i == 0)
        def _():
            ds_scr[...] = jnp.zeros_like(ds_scr)
            dlb_ref[...] = jnp.zeros_like(dlb_ref)
            dgn_ref[...] = jnp.zeros_like(dgn_ref)

        lb, omlb = _lower_bound(lbp_ref)
        q = q_ref[...]
        v = i_ref[...]
        g = g_ref[...]
        sq, qf, sig, f, k, e = _hgrn_gates(q, f_ref[...], lb, omlb, w_ref)
        eb = e[0:CH]
        ebl = eb[CH - 1:CH, :]
        ekd = e[CH:2 * CH]
        kd = k * ekd
        row = lax.broadcasted_iota(jnp.int32, (CH, 1), 0)
        eye = lax.broadcasted_iota(jnp.int32, (CH, CH), 0) == lax.broadcasted_iota(jnp.int32, (CH, CH), 1)
        dgn = jnp.zeros((1, HD), F32)
        dqf_h, dk_h, dv_h, dg_h = [], [], [], []
        for h in range(NH):
            sl = slice(h * HD, (h + 1) * HD)
            o = o_ref[:, sl]
            ro = _rms(o)
            oh = o * ro
            gh = g[:, sl]
            sg = _sigmoid(gh)
            dyb = dy_ref[:, sl]
            dgate = dyb * (oh * gn_ref[...])
            dg = dgate * (sg * (1.0 + gh * (1.0 - sg)))
            don = dyb * (gh * sg)
            dgn = dgn + jnp.sum(don * oh, axis=0, keepdims=True)
            do = _rms_bwd(oh, ro, don * gn_ref[...])
            dob = do.astype(BF16)
            vb = v[:, sl].astype(BF16)
            st0 = st_ref[0, h]
            dst1 = ds_scr[h]
            dst1b = dst1.astype(BF16)
            qh, kh, ebh, kdh = qf[:, sl], k[:, sl], eb[:, sl], kd[:, sl]
            qe = qh * ebh
            dqe = _dot(dob, st0.astype(BF16))
            ds_scr[h] = dst1 * ebl[:, sl] + _dot(dob, qe.astype(BF16), TN)
            dbl = ebl[:, sl] * jnp.sum(st0 * dst1, axis=0, keepdims=True)
            dkd = _dot(vb, dst1b)
            dv_h.append(_dot(a_ref[0, h].astype(BF16), dob, TN) + _dot(kdh.astype(BF16), dst1b, NT))
            da = _dot(dob, vb, NT)
            dat = _dot(vb, dob, NT)
            ddiag = jnp.sum(jnp.where(eye, da, 0.0), axis=-1, keepdims=True)
            dqi = ddiag * kh
            dki = ddiag * qh
            for li in range(n_lev):
                el, up, y = _level_factor(e, li, sl, row, qh, kh)
                dgs = m_ref[li] * da + mt_ref[li] * dat
                dyv = _dot(dgs.astype(BF16), y.astype(BF16))
                dx_scr[(2 + li) * CH:(3 + li) * CH, sl] = dyv * y
                dye = dyv * el
                dqi = dqi + jnp.where(up, dye, 0.0)
                dki = dki + jnp.where(up, 0.0, dye)
            dx_scr[0:CH, sl] = dqe * qe + jnp.where(row == CH - 1, dbl, 0.0)
            dx_scr[CH:2 * CH, sl] = dkd * kdh
            dqf_h.append(dqe * ebh + dqi)
            dk_h.append(dkd * ekd[:, sl] + dki)
            dg_h.append(dg)
        dgn_ref[0:1, :] += dgn
        dlf = _split_dot(wt_ref[...], dx_scr[...], 2)
        df = dlf / f - jnp.concatenate(dk_h, axis=1)
        dlb_ref[0:1, :] += jnp.sum(df * (1.0 - sig), axis=0, keepdims=True)
        dp_ref[:, 0:DH] = (jnp.concatenate(dqf_h, axis=1) * (sq * (1.0 + q * (1.0 - sq)))).astype(BF16)
        dp_ref[:, DH:2 * DH] = (df * omlb * sig * (1.0 - sig)).astype(BF16)
        dp_ref[:, 2 * DH:3 * DH] = jnp.concatenate(dv_h, axis=1).astype(BF16)
        dp_ref[:, 3 * DH:4 * DH] = jnp.concatenate(dg_h, axis=1).astype(BF16)

        @pl.when(i == nc - 1)
        def _():
            gl = dlb_ref[0:1, :] * lb * omlb
            dlb_ref[0:1, :] = gl
            dlb_ref[1:2, :] = -gl

    rev = lambda j: pl.BlockSpec((CH, DH), lambda c: (nc - 1 - c, j))
    return pl.pallas_call(
        body, grid=(nc,),
        in_specs=[rev(2), rev(3), rev(4), rev(5), rev(0),
                  pl.BlockSpec((1, NH, CH, CH), lambda c: (nc - 1 - c, 0, 0, 0)),
                  pl.BlockSpec((1, NH, HD, HD), lambda c: (nc - 1 - c, 0, 0, 0)),
                  rev(1), _full((2, DH)), _full((1, HD)),
                  _full(w_st.shape), _full(w_st_t.shape), _full(masks.shape), _full(masks_t.shape),
                  pl.BlockSpec(memory_space=pl.ANY)],
        out_specs=[pl.BlockSpec((CH, 4 * DH), lambda c: (nc - 1 - c, 0)), _full((8, DH)), _full((8, HD))],
        out_shape=[SDS((T, DIN), BF16), SDS((8, DH), F32), SDS((8, HD), F32)],
        scratch_shapes=[pltpu.VMEM((NH, HD, HD), F32), pltpu.VMEM(((2 + n_lev) * CH, DH), F32)],
        input_output_aliases={14: 0},
        compiler_params=_arb(), name="hgrn_bwd")(proj, proj, proj, proj, o_pre, a_all, st_all, dycat, lower_bounds, gn_w,
                                                 w_st, w_st_t, masks, masks_t, dproj)


def _proj_in_bwd(dproj, x, dx1, nw, sc, w_in_b, tm):
    T = x.shape[0]

    def body(dp_ref, x_ref, dx1_ref, nw_ref, sc_ref, w_ref, gx_ref, acc_ref):
        @pl.when(pl.program_id(0) == 0)
        def _():
            acc_ref[...] = jnp.zeros_like(acc_ref)

        dh = _dot(dp_ref[:, 0:4 * DH], w_ref[:, 2 * DG:DIN], NT) + _dot(dp_ref[:, 4 * DH:DIN], w_ref[:, 0:2 * DG], NT)
        xv = x_ref[...]
        r = _rms(xv)
        xh = xv * r
        n1 = xh * nw_ref[...]
        acc_ref[0:1, :] += jnp.sum(dh, axis=0, keepdims=True)
        acc_ref[1:2, :] += jnp.sum(dh * n1, axis=0, keepdims=True)
        dn = dh * (1.0 + sc_ref[...])
        acc_ref[2:3, :] += jnp.sum(dn * xh, axis=0, keepdims=True)
        gx_ref[...] = dx1_ref[...] + _rms_bwd(xh, r, dn * nw_ref[...])

    row = lambda i: (i, 0)
    return pl.pallas_call(
        body, grid=(T // tm,),
        in_specs=[pl.BlockSpec((tm, DIN), row), pl.BlockSpec((tm, D), row), pl.BlockSpec((tm, D), row),
                  _full((1, D)), _full((1, D)), _resident((D, DIN))],
        out_specs=[pl.BlockSpec((tm, D), row), _full((8, D))],
        out_shape=[SDS((T, D), F32), SDS((8, D), F32)],
        compiler_params=_arb(), name="proj_in_bwd")(dproj, x, dx1, nw, sc, w_in_b)


def _wgrad(a, b, bk, bn, tt, name, b_col_block=None):
    T, K = a.shape
    N = b.shape[1]
    nn, nk, nt = N // bn, K // bk, T // tt
    bmap = (lambda n, k, t: (t, n)) if b_col_block is None else (lambda n, k, t: (t, b_col_block(n)))

    def body(a_ref, b_ref, o_ref):
        @pl.when(pl.program_id(2) == 0)
        def _():
            o_ref[...] = jnp.zeros_like(o_ref)

        o_ref[0] += _dot(a_ref[...], b_ref[...], TN)

    return pl.pallas_call(
        body, grid=(nn, nk, nt),
        in_specs=[pl.BlockSpec((tt, bk), lambda n, k, t: (t, k)), pl.BlockSpec((tt, bn), bmap)],
        out_specs=pl.BlockSpec((1, bk, bn), lambda n, k, t: (n, k, 0)),
        out_shape=SDS((nn, K, bn), F32),
        compiler_params=_arb(3), name=name)(a, b)


def _adam_math(w, g, m, v):
    m = B1 * m + (1.0 - B1) * g
    v = B2 * v + (1.0 - B2) * (g * g)
    m_hat = m / (1.0 - B1 ** STEP)
    v_hat = v / (1.0 - B2 ** STEP)
    return -LR * (m_hat / (jnp.sqrt(v_hat) + AEPS) + WD * w), m, v


def _adamw(w, g, m, v, rb, name):
    R, C = w.shape

    def body(w_ref, g_ref, m_ref, v_ref, d_out, m_out, v_out):
        d_out[...], m_out[...], v_out[...] = _adam_math(w_ref[...], g_ref[...], m_ref[...], v_ref[...])

    spec = pl.BlockSpec((rb, C), lambda i: (i, 0))
    return pl.pallas_call(
        body, grid=(R // rb,), in_specs=[spec] * 4, out_specs=[spec] * 3,
        out_shape=[SDS((R, C), F32)] * 3, compiler_params=_arb(), name=name)(w, g, m, v)


def _ada_forward(c_all, w_ada):
    n = w_ada.shape[1]

    def body(c_ref, w_ref, ca_ref, p_ref):
        cv = c_ref[...]
        ca = cv * _sigmoid(cv)
        ca_ref[...] = ca
        p_ref[...] = _dot(ca, w_ref[...], precision=HIGHEST)

    return pl.pallas_call(
        body, grid=(n // 512,),
        in_specs=[_full((N_DEV, D)), pl.BlockSpec((D, 512), lambda i: (0, i))],
        out_specs=[_full((N_DEV, D)), pl.BlockSpec((N_DEV, 512), lambda i: (0, i))],
        out_shape=[SDS((N_DEV, D), F32), SDS((N_DEV, n), F32)],
        compiler_params=_arb(), name="ada_forward")(c_all, w_ada)


def _ada_wgrad_adam(cact_t, dada, w, m, v):
    R, C = w.shape
    rb = 256

    def body(c_ref, d_ref, w_ref, m_ref, v_ref, g_out, d_out, m_out, v_out):
        g = _dot(c_ref[...], d_ref[...], precision=HIGHEST)
        g_out[...] = g
        d_out[...], m_out[...], v_out[...] = _adam_math(w_ref[...], g, m_ref[...], v_ref[...])

    spec = pl.BlockSpec((rb, C), lambda i: (i, 0))
    return pl.pallas_call(
        body, grid=(R // rb,),
        in_specs=[pl.BlockSpec((rb, N_DEV), lambda i: (i, 0)), _full((N_DEV, C)), spec, spec, spec],
        out_specs=[spec] * 4, out_shape=[SDS((R, C), F32)] * 4,
        compiler_params=_arb(), name="ada_wgrad_adam")(cact_t, dada, w, m, v)


def _small_finalize(gathered, w, m, v):
    def body(ga_ref, w_ref, m_ref, v_ref, g_out, d_out, m_out, v_out):
        g = ga_ref[0:SMALL_ROWS, :]
        for dev in range(1, N_DEV):
            g = g + ga_ref[dev * SMALL_ROWS:(dev + 1) * SMALL_ROWS, :]
        g_out[...] = g
        d_out[...], m_out[...], v_out[...] = _adam_math(w_ref[...], g, m_ref[...], v_ref[...])

    return pl.pallas_call(
        body, out_shape=[SDS((SMALL_ROWS, D), F32)] * 4, name="small_finalize")(gathered, w, m, v)


def _position():
    x, y, c = lax.axis_index("x"), lax.axis_index("y"), lax.axis_index("c")
    return x, y, c


def _chip_at(x, y, r):
    return (x ^ (r >> 1), y ^ (r & 1))


def _all_gather_rows(block, name):
    m_per, n = block.shape

    def body(x_ref, out_ref, send_sems, recv_sems, local_sem):
        x, y, c = _position()
        me, sibling = (x, y, c), (x, y, 1 - c)
        chips = [_chip_at(x, y, r) for r in (1, 2, 3)]

        def rows(px, py, pc):
            return out_ref.at[pl.ds((4 * px + 2 * py + pc) * m_per, m_per), :]

        def copy(k, blk, to, src=None):
            return pltpu.make_async_remote_copy(
                src_ref=rows(*blk) if src is None else src, dst_ref=rows(*blk),
                send_sem=send_sems.at[k], recv_sem=recv_sems.at[k], device_id=to, device_id_type=MESH)

        mine = pltpu.make_async_copy(x_ref, rows(*me), local_sem)
        mine.start()
        first = [copy(0, me, sibling, src=x_ref)]
        first += [copy(1 + j, me, (*chip, c), src=x_ref) for j, chip in enumerate(chips)]
        for cp in first:
            cp.start()
        passed = [copy(4 + j, (*chip, c), sibling) for j, chip in enumerate(chips)]
        for j, chip in enumerate(chips):
            copy(1 + j, (*chip, c), me).wait_recv()
            passed[j].start()
        copy(0, sibling, me).wait_recv()
        for j, chip in enumerate(chips):
            copy(4 + j, (*chip, 1 - c), me).wait_recv()
        for cp in first + passed:
            cp.wait_send()
        mine.wait()

    return pl.pallas_call(
        body, out_shape=SDS((N_DEV * m_per, n), block.dtype),
        in_specs=[pl.BlockSpec(memory_space=pltpu.VMEM)], out_specs=pl.BlockSpec(memory_space=pltpu.VMEM),
        scratch_shapes=[pltpu.SemaphoreType.DMA((7,)), pltpu.SemaphoreType.DMA((7,)), pltpu.SemaphoreType.DMA],
        name=name)(block)


def _gather_weights(shards):
    nw = len(shards)

    def body(*refs):
        ins, outs = refs[:nw], refs[nw:2 * nw]
        send_sems, recv_sems, local_sems = refs[2 * nw:]
        x, y, c = _position()
        j = 2 * x + y
        started = []
        for w, (arr, axis) in enumerate(shards):
            size = arr.shape[axis]

            def slot(chip_idx, w=w, axis=axis, size=size):
                if axis == 0:
                    return outs[w].at[pl.ds(chip_idx * size, size), :]
                return outs[w].at[:, pl.ds(chip_idx * size, size)]

            local = pltpu.make_async_copy(ins[w], slot(j), local_sems.at[w])
            local.start()
            started.append(local)
            for r in (1, 2, 3):
                cx, cy = _chip_at(x, y, r)
                k = 3 * w + r - 1
                cp = pltpu.make_async_remote_copy(
                    src_ref=ins[w], dst_ref=slot(j), send_sem=send_sems.at[k], recv_sem=recv_sems.at[k],
                    device_id=(cx, cy, c), device_id_type=MESH)
                cp.start()
                started.append((cp, slot(j ^ r), k, w))
        for item in started:
            if isinstance(item, tuple):
                cp, from_slot, k, w = item
                cp.wait_send()
                pltpu.make_async_remote_copy(
                    src_ref=ins[w], dst_ref=from_slot, send_sem=send_sems.at[k], recv_sem=recv_sems.at[k],
                    device_id=(x, y, c), device_id_type=MESH).wait_recv()
            else:
                item.wait()

    out_shape = []
    for arr, axis in shards:
        shp = list(arr.shape)
        shp[axis] *= N_CHIPS
        out_shape.append(SDS(tuple(shp), arr.dtype))
    anyspec = pl.BlockSpec(memory_space=pl.ANY)
    return pl.pallas_call(
        body, out_shape=out_shape, in_specs=[anyspec] * nw, out_specs=[anyspec] * nw,
        scratch_shapes=[pltpu.SemaphoreType.DMA((3 * nw,)), pltpu.SemaphoreType.DMA((3 * nw,)),
                        pltpu.SemaphoreType.DMA((nw,))],
        name="gather_weights")(*[a for a, _ in shards])


def _exchange_core_halves(grads):
    nw = len(grads)

    def body(*refs):
        ins, outs = refs[:nw], refs[nw:2 * nw]
        send_sems, recv_sems = refs[2 * nw:]
        x, y, c = _position()
        cps = []
        for w in range(nw):
            cp = pltpu.make_async_remote_copy(
                src_ref=ins[w].at[:, 1 - c], dst_ref=outs[w], send_sem=send_sems.at[w], recv_sem=recv_sems.at[w],
                device_id=(x, y, 1 - c), device_id_type=MESH)
            cp.start()
            cps.append(cp)
        for cp in cps:
            cp.wait()

    anyspec = pl.BlockSpec(memory_space=pl.ANY)
    return pl.pallas_call(
        body, out_shape=[SDS((g.shape[0], g.shape[2], g.shape[3]), F32) for g in grads],
        in_specs=[anyspec] * nw, out_specs=[anyspec] * nw,
        scratch_shapes=[pltpu.SemaphoreType.DMA((nw,)), pltpu.SemaphoreType.DMA((nw,))],
        name="exchange_core_halves")(*grads)


def _add_core_halves(g4, recv, c_idx, rb, name):
    ns, _, rh, C = g4.shape

    def body(c_ref, g_ref, r_ref, o_ref):
        o_ref[...] = g_ref[0] + r_ref[...]

    return pl.pallas_call(
        body,
        grid_spec=pltpu.PrefetchScalarGridSpec(
            num_scalar_prefetch=1, grid=(ns, rh // rb),
            in_specs=[pl.BlockSpec((1, 1, rb, C), lambda s, i, cr: (s, cr[0], i, 0)),
                      pl.BlockSpec((1, rb, C), lambda s, i, cr: (s, i, 0))],
            out_specs=pl.BlockSpec((1, rb, C), lambda s, i, cr: (s, i, 0))),
        out_shape=SDS((ns, rh, C), F32), compiler_params=_arb(2), name=name)(c_idx, g4, recv)


def _exchange_chips(sums):
    nw = len(sums)

    def body(*refs):
        ins, outs = refs[:nw], refs[nw:2 * nw]
        send_sems, recv_sems, local_sems = refs[2 * nw:]
        x, y, c = _position()
        j = 2 * x + y
        started = []
        for w in range(nw):
            local = pltpu.make_async_copy(ins[w].at[j], outs[w].at[0], local_sems.at[w])
            local.start()
            started.append(local)
            for r in (1, 2, 3):
                cx, cy = _chip_at(x, y, r)
                k = 3 * w + r - 1
                cp = pltpu.make_async_remote_copy(
                    src_ref=ins[w].at[j ^ r], dst_ref=outs[w].at[r], send_sem=send_sems.at[k], recv_sem=recv_sems.at[k],
                    device_id=(cx, cy, c), device_id_type=MESH)
                cp.start()
                started.append(cp)
        for cp in started:
            cp.wait()

    anyspec = pl.BlockSpec(memory_space=pl.ANY)
    return pl.pallas_call(
        body, out_shape=[SDS(s.shape, F32) for s in sums], in_specs=[anyspec] * nw, out_specs=[anyspec] * nw,
        scratch_shapes=[pltpu.SemaphoreType.DMA((3 * nw,)), pltpu.SemaphoreType.DMA((3 * nw,)),
                        pltpu.SemaphoreType.DMA((nw,))],
        name="exchange_chips")(*sums)


def _add_chips(slots, order, rb, name):
    _, rh, C = slots.shape

    def body(o_ref, a_ref, b_ref, c_ref, d_ref, out_ref):
        out_ref[...] = ((a_ref[0] + b_ref[0]) + c_ref[0]) + d_ref[0]

    def spec(i):
        return pl.BlockSpec((1, rb, C), lambda t, o: (o[i], t, 0))

    return pl.pallas_call(
        body,
        grid_spec=pltpu.PrefetchScalarGridSpec(
            num_scalar_prefetch=1, grid=(rh // rb,),
            in_specs=[spec(0), spec(1), spec(2), spec(3)],
            out_specs=pl.BlockSpec((rb, C), lambda t, o: (t, 0))),
        out_shape=SDS((rh, C), F32), compiler_params=_arb(), name=name)(order, slots, slots, slots, slots)


def _share_halves(halves):
    nw = len(halves)

    def body(*refs):
        ins, outs = refs[:nw], refs[nw:2 * nw]
        send_sems, recv_sems, local_sems = refs[2 * nw:]
        x, y, c = _position()
        started = []
        for w in range(nw):
            local = pltpu.make_async_copy(ins[w], outs[w].at[c], local_sems.at[w])
            local.start()
            started.append(local)
            cp = pltpu.make_async_remote_copy(
                src_ref=ins[w], dst_ref=outs[w].at[c], send_sem=send_sems.at[w], recv_sem=recv_sems.at[w],
                device_id=(x, y, 1 - c), device_id_type=MESH)
            cp.start()
            started.append(cp)
        for cp in started:
            cp.wait()

    anyspec = pl.BlockSpec(memory_space=pl.ANY)
    return pl.pallas_call(
        body, out_shape=[SDS((2,) + h.shape, F32) for h in halves], in_specs=[anyspec] * nw, out_specs=[anyspec] * nw,
        scratch_shapes=[pltpu.SemaphoreType.DMA((nw,)), pltpu.SemaphoreType.DMA((nw,)), pltpu.SemaphoreType.DMA((nw,))],
        name="share_halves")(*halves)


def _pack_small(b_ada, norm1_w, norm2_w, final_norm_w, v_ln_w, v_ln_b, lower_bounds, b_s, gn_w, w_s):
    parts = [b_ada, norm1_w, norm2_w, final_norm_w, v_ln_w, v_ln_b, lower_bounds, b_s, gn_w,
             jnp.zeros((D - NH * BLK - HD,), F32), w_s, jnp.zeros(((SMALL_ROWS - 76) * D,), F32)]
    return jnp.concatenate([p.reshape(-1) for p in parts]).reshape(SMALL_ROWS, D)


def _unpack_small(p):
    return dict(
        b_ada=p[0:6].reshape(1, 6 * D), norm1_w=p[6:7], norm2_w=p[7:8], final_norm_w=p[8],
        v_ln_w=p[9:10, 0:DG], v_ln_b=p[9:10, DG:D], lower_bounds=p[10].reshape(2, DH),
        b_s=p[11, 0:NH * BLK].reshape(1, NH, BLK), gn_w=p[11:12, NH * BLK:NH * BLK + HD],
        w_s=p[12:76].reshape(1, NH, BLK, BLK))


def _row_block(r):
    for cand in (256, 176, 128, 64, 32, 16, 8):
        if r % cand == 0:
            return cand
    return r


def kernel(x, c, w_ada, b_ada, norm1_w, w_in, w_s, b_s, v_ln_w, v_ln_b, lower_bounds, gn_w, w_out, norm2_w, w_ffn_in, w_ffn_out, final_norm_w, loss_target, m_w_ada, m_b_ada, m_norm1_w, m_w_in, m_w_s, m_b_s, m_v_ln_w, m_v_ln_b, m_lower_bounds, m_gn_w, m_w_out, m_norm2_w, m_w_ffn_in, m_w_ffn_out, m_final_norm_w, v_w_ada, v_b_ada, v_norm1_w, v_w_in, v_w_s, v_b_s, v_v_ln_w, v_v_ln_b, v_lower_bounds, v_gn_w, v_w_out, v_norm2_w, v_w_ffn_in, v_w_ffn_out, v_final_norm_w):
    T = x.shape[1]
    tm = min(256, T)
    px, py, pc = _position()
    chip = 2 * px + py
    me = 4 * px + 2 * py + pc
    x2d = x.reshape(T, D)
    tgt = loss_target.reshape(T, D)

    c_all = _all_gather_rows(jnp.broadcast_to(c, (8, D)), "gather_c").reshape(N_DEV, 8, D)[:, 0, :]
    cact, ada_part = _ada_forward(c_all, w_ada[0])
    n_ada = ada_part.shape[1]
    ada_all = _all_gather_rows(ada_part, "gather_ada").reshape(N_CHIPS, 2, N_DEV, n_ada)[:, 0]
    ada = lax.dynamic_index_in_dim(ada_all, me, axis=1, keepdims=False).reshape(1, 6 * D) + b_ada
    sh1, sc1, g1, sh2, sc2, g2 = [ada[:, i * D:(i + 1) * D] for i in range(6)]

    w_in_b, w_out_b, w_fi_b, w_fo_b = _gather_weights(
        [(w_in[0].astype(BF16), 1), (w_out[0].astype(BF16), 0), (w_ffn_in[0].astype(BF16), 1), (w_ffn_out[0].astype(BF16), 0)])

    rr = lax.broadcasted_iota(jnp.int32, (BLK, BLK), 0) // CH
    cc = lax.broadcasted_iota(jnp.int32, (BLK, BLK), 1) // CH
    ws_b = jnp.where((rr >= cc)[None], w_s[0], 0.0).astype(BF16)
    bst = b_s[0].T
    lnw, lnb = v_ln_w, v_ln_b
    nw1, nw2, fw = norm1_w, norm2_w, final_norm_w.reshape(1, D)

    h1, proj = _proj_in(x2d, nw1, sc1, sh1, w_in_b, tm)
    ycat = _gmlp_fwd(proj, ws_b, bst, lnw, lnb)
    tables = _hgrn_tables()
    ycat, o_pre, a_all, st_all = _hgrn_fwd(proj, lower_bounds, gn_w, ycat, tables)

    dycat, dx1, h2, act, dff, dgu, dmix, acc2 = _token_local(
        x2d, ycat, tgt, g1, nw2, sc2, sh2, g2, fw, w_out_b, w_fi_b, w_fo_b, tm)

    dproj, dws, dbs, dln = _gmlp_bwd(proj, dycat, ws_b, bst, lnw, lnb)
    dproj, dlb, dgn = _hgrn_bwd(proj, o_pre, a_all, st_all, dycat, lower_bounds, gn_w, dproj, tables)
    grad_x, acc1 = _proj_in_bwd(dproj, x2d, dx1, nw1, sc1, w_in_b, tm)

    tt = min(512, T)
    g_in = _wgrad(h1, dproj, D, 256, tt, "wgrad_in", b_col_block=lambda n: (n + 8) % 12)
    g_in = g_in.reshape(N_CHIPS, 3, D, 256).transpose(0, 2, 1, 3).reshape(N_CHIPS, D, 768)
    g_out = _wgrad(ycat, dmix, D, D, tt, "wgrad_out").reshape(N_CHIPS, D // N_CHIPS, D)
    g_fi = _wgrad(h2, dgu, D, FFB, tt, "wgrad_ffn_in")
    g_fo = _wgrad(act, dff, FFB, D, tt, "wgrad_ffn_out").reshape(N_CHIPS, DFF // N_CHIPS, D)

    big = [g_in, g_out, g_fi, g_fo]
    g4 = [g.reshape(N_CHIPS, 2, g.shape[1] // 2, g.shape[2]) for g in big]
    recv = _exchange_core_halves(g4)
    c_idx = jnp.reshape(pc, (1,)).astype(jnp.int32)
    names = ["in", "out", "ffn_in", "ffn_out"]
    sums = [_add_core_halves(a, b, c_idx, _row_block(a.shape[2]), "add_core_" + n) for a, b, n in zip(g4, recv, names)]
    slots = _exchange_chips(sums)
    order = (chip ^ jnp.arange(N_CHIPS, dtype=jnp.int32)).astype(jnp.int32)
    halves = [_add_chips(s, order, _row_block(s.shape[1]), "add_chips_" + n) for s, n in zip(slots, names)]
    shard_grads = [h.reshape(2 * h.shape[1], h.shape[2]) for h in _share_halves(halves)]

    big_w = [(w_in, m_w_in, v_w_in), (w_out, m_w_out, v_w_out), (w_ffn_in, m_w_ffn_in, v_w_ffn_in),
             (w_ffn_out, m_w_ffn_out, v_w_ffn_out)]
    big_out = []
    for g, (w, m, v), n in zip(shard_grads, big_w, names):
        d_, m_, v_ = _adamw(w[0], g, m[0], v[0], _row_block(g.shape[0]), "adamw_" + n)
        big_out.append((g[None], d_[None], m_[None], v_[None]))

    d_ada = jnp.stack([acc1[0], acc1[1], acc2[5], acc2[2], acc2[1], acc2[0]]).reshape(1, 6 * D)
    small_g = _pack_small(d_ada, acc1[2], acc2[3], acc2[4], dln[0], dln[1], dlb[0:2], dbs[:, 0:NH].T, dgn[0], dws)
    gathered = _all_gather_rows(small_g, "gather_small")
    sw = _pack_small(b_ada, norm1_w, norm2_w, final_norm_w, v_ln_w, v_ln_b, lower_bounds, b_s, gn_w, w_s)
    sm = _pack_small(m_b_ada, m_norm1_w, m_norm2_w, m_final_norm_w, m_v_ln_w, m_v_ln_b, m_lower_bounds, m_b_s, m_gn_w, m_w_s)
    sv = _pack_small(v_b_ada, v_norm1_w, v_norm2_w, v_final_norm_w, v_v_ln_w, v_v_ln_b, v_lower_bounds, v_b_s, v_gn_w, v_w_s)
    small = [_unpack_small(p) for p in _small_finalize(gathered, sw, sm, sv)]

    dada_all = gathered.reshape(N_DEV, SMALL_ROWS, D)[:, 0:6, :].reshape(N_DEV, 6 * D)
    dada = lax.dynamic_slice_in_dim(dada_all, chip * n_ada, n_ada, axis=1)
    ada_out = [o[None] for o in _ada_wgrad_adam(cact.T, dada, w_ada[0], m_w_ada[0], v_w_ada[0])]

    loss = lax.psum(jnp.sum(acc2[6]), ("x", "y", "c"))

    order_names = ['w_ada', 'b_ada', 'norm1_w', 'w_in', 'w_s', 'b_s', 'v_ln_w', 'v_ln_b', 'lower_bounds', 'gn_w',
                   'w_out', 'norm2_w', 'w_ffn_in', 'w_ffn_out', 'final_norm_w']
    big_idx = {'w_in': 0, 'w_out': 1, 'w_ffn_in': 2, 'w_ffn_out': 3}
    outs = [loss, grad_x.reshape(1, T, D)]
    for kind in range(4):
        for n in order_names:
            if n == 'w_ada':
                outs.append(ada_out[kind])
            elif n in big_idx:
                outs.append(big_out[big_idx[n]][kind])
            else:
                outs.append(small[kind][n])
    return tuple(outs)
```

```python
import functools

import jax
import jax.numpy as jnp
import numpy as np
from jax import lax
from jax.experimental import pallas as pl
from jax.experimental.pallas import tpu as pltpu

F32 = jnp.float32
BF16 = jnp.bfloat16
SDS = jax.ShapeDtypeStruct
MESH = pl.DeviceIdType.MESH
HIGHEST = lax.Precision.HIGHEST

D = 1024
DG = 512
DH = 512
NH = 4
HD = 128
BLK = 128
CH = 64
DFF = 2816
DIN = 3072
FFB = 1408
LEVELS = (64, 32, 16, 8, 4, 2)
N_CHIPS = 4
N_DEV = 8
EPS = 1e-6
LR, B1, B2, AEPS, WD, STEP = 0.001, 0.9, 0.999, 1e-08, 0.01, 10
SMALL_ROWS = 80

NT = (((1,), (1,)), ((), ()))
TN = (((0,), (0,)), ((), ()))


def _full(shape):
    nd = len(shape)
    return pl.BlockSpec(shape, lambda *_: (0,) * nd)


def _resident(shape):
    nd = len(shape)
    return pl.BlockSpec(shape, lambda *_: (0,) * nd, pipeline_mode=pl.Buffered(1))


def _arb(n=1):
    return pltpu.CompilerParams(dimension_semantics=("arbitrary",) * n)


def _dot(a, b, dims=None, precision=None):
    if dims is None:
        return jnp.dot(a, b, preferred_element_type=F32, precision=precision)
    return lax.dot_general(a, b, dims, preferred_element_type=F32, precision=precision)


def _sigmoid(x):
    return jax.nn.sigmoid(x)


def _gelu_parts(x):
    cdf = 0.5 * (1.0 + lax.erf(x * 0.7071067811865476))
    pdf = jnp.exp(-0.5 * x * x) * 0.3989422804014327
    return x * cdf, cdf + x * pdf


def _rms(x):
    return lax.rsqrt(jnp.mean(x * x, axis=-1, keepdims=True) + EPS)


def _rms_bwd(xhat, r, gw):
    return r * (gw - xhat * jnp.mean(xhat * gw, axis=-1, keepdims=True))


def _lower_bound(lbp_ref):
    l0, l1 = lbp_ref[0:1, :], lbp_ref[1:2, :]
    m = jnp.maximum(l0, l1)
    e0, e1 = jnp.exp(l0 - m), jnp.exp(l1 - m)
    return e0 / (e0 + e1), e1 / (e0 + e1)


def _proj_in(x, nw, sc, sh, w_in_b, tm):
    T = x.shape[0]

    def body(x_ref, nw_ref, sc_ref, sh_ref, w_ref, h_ref, p_ref):
        xv = x_ref[...]
        h = ((xv * _rms(xv)) * nw_ref[...]) * (1.0 + sc_ref[...]) + sh_ref[...]
        hb = h.astype(BF16)
        h_ref[...] = hb
        p_ref[...] = _dot(hb, w_ref[...])

    row = lambda i: (i, 0)
    return pl.pallas_call(
        body, grid=(T // tm,),
        in_specs=[pl.BlockSpec((tm, D), row), _full((1, D)), _full((1, D)), _full((1, D)), _resident((D, DIN))],
        out_specs=[pl.BlockSpec((tm, D), row), pl.BlockSpec((tm, DIN), row)],
        out_shape=[SDS((T, D), BF16), SDS((T, DIN), F32)],
        compiler_params=_arb(), name="proj_in")(x, nw, sc, sh, w_in_b)


def _gmlp_common(u, v, lnw, lnb, ws_ref, bst_ref):
    ug, dug = _gelu_parts(u)
    vg, dvg = _gelu_parts(v)
    mu = jnp.mean(vg, axis=-1, keepdims=True)
    vc = vg - mu
    rstd = lax.rsqrt(jnp.mean(vc * vc, axis=-1, keepdims=True) + EPS)
    vhat = vc * rstd
    vn = vhat * lnw + lnb
    vnb = vn.astype(BF16)
    mixed = []
    for h in range(NH):
        sl = slice(h * HD, (h + 1) * HD)
        mixed.append(_dot(ws_ref[h], vnb[:, sl]) + bst_ref[:, h:h + 1])
    return ug, dug, dvg, rstd, vhat, vnb, jnp.concatenate(mixed, axis=1)


def _gmlp_fwd(proj, ws_b, bst, lnw, lnb):
    T = proj.shape[0]

    def body(u_ref, v_ref, ws_ref, bst_ref, lnw_ref, lnb_ref, y_ref):
        ug, _, _, _, _, _, mixed = _gmlp_common(u_ref[...], v_ref[...], lnw_ref[...], lnb_ref[...], ws_ref, bst_ref)
        y_ref[...] = (ug * mixed).astype(BF16)

    return pl.pallas_call(
        body, grid=(T // BLK,),
        in_specs=[pl.BlockSpec((BLK, DG), lambda i: (i, 0)), pl.BlockSpec((BLK, DG), lambda i: (i, 1)),
                  _full((NH, BLK, BLK)), _full((BLK, NH)), _full((1, DG)), _full((1, DG))],
        out_specs=pl.BlockSpec((BLK, DG), lambda i: (i, 0)),
        out_shape=SDS((T, D), BF16),
        compiler_params=_arb(), name="gmlp_fwd")(proj, proj, ws_b, bst, lnw, lnb)


def _hgrn_tables():
    t = np.arange(CH)[:, None]
    j = np.arange(CH)[None, :]
    blocks = [j <= t, j > t]
    masks = []
    for n in LEVELS:
        mid = t - t % n + n // 2
        blocks.append(np.where(t >= mid, (j >= mid) & (j <= t), (j > t) & (j < mid)))
        masks.append((t // n == j // n) & (t % n >= n // 2) & (j % n < n // 2))
    w = np.concatenate(blocks, axis=0).astype(np.float32)
    m = np.stack(masks).astype(np.float32)
    return (jnp.asarray(w, BF16), jnp.asarray(w.T, BF16), jnp.asarray(m), jnp.asarray(m.transpose(0, 2, 1)))


def _split_dot(w, x, parts):
    acc = None
    for _ in range(parts):
        piece = x.astype(BF16)
        term = _dot(w, piece)
        acc = term if acc is None else acc + term
        x = x - piece.astype(F32)
    return acc


def _hgrn_gates(q, fl, lb, omlb, w_ref):
    sq = _sigmoid(q)
    qf = q * sq
    sig = _sigmoid(fl)
    f = lb + omlb * sig
    k = 1.0 - f
    e = jnp.exp(_split_dot(w_ref[...], jnp.log(f), 3))
    return sq, qf, sig, f, k, e


def _level_factor(e, li, sl, row, qh, kh):
    el = e[(2 + li) * CH:(3 + li) * CH, sl]
    up = (row & (LEVELS[li] // 2)) != 0
    return el, up, el * jnp.where(up, qh, kh)


def _hgrn_fwd(proj, lower_bounds, gn_w, ycat, tables):
    T = proj.shape[0]
    nc = T // CH
    w_st, _, masks, _ = tables

    def body(q_ref, f_ref, i_ref, g_ref, lbp_ref, gn_ref, w_ref, m_ref, ycat_any, y_ref, o_ref, a_ref, st_ref, s_scr):
        @pl.when(pl.program_id(0) == 0)
        def _():
            s_scr[...] = jnp.zeros_like(s_scr)

        lb, omlb = _lower_bound(lbp_ref)
        v = i_ref[...]
        g = g_ref[...]
        _, qf, _, _, k, e = _hgrn_gates(q_ref[...], f_ref[...], lb, omlb, w_ref)
        eb = e[0:CH]
        ebl = eb[CH - 1:CH, :]
        kd = k * e[CH:2 * CH]
        row = lax.broadcasted_iota(jnp.int32, (CH, 1), 0)
        eye = lax.broadcasted_iota(jnp.int32, (CH, CH), 0) == lax.broadcasted_iota(jnp.int32, (CH, CH), 1)
        for h in range(NH):
            sl = slice(h * HD, (h + 1) * HD)
            st0 = s_scr[h]
            st_ref[0, h] = st0
            qh, kh = qf[:, sl], k[:, sl]
            inter = _dot((qh * eb[:, sl]).astype(BF16), st0.astype(BF16), NT)
            a = jnp.where(eye, jnp.sum(qh * kh, axis=-1, keepdims=True), 0.0)
            for li in range(len(LEVELS)):
                _, _, y = _level_factor(e, li, sl, row, qh, kh)
                yb = y.astype(BF16)
                a = a + m_ref[li] * _dot(yb, yb, NT)
            a_ref[0, h] = a
            vb = v[:, sl].astype(BF16)
            o = inter + _dot(a.astype(BF16), vb)
            s_scr[h] = st0 * ebl[:, sl] + _dot(vb, kd[:, sl].astype(BF16), TN)
            o_ref[:, sl] = o
            gh = g[:, sl]
            y_ref[:, sl] = (((o * _rms(o)) * gn_ref[...]) * (gh * _sigmoid(gh))).astype(BF16)

    blk = lambda j: pl.BlockSpec((CH, DH), lambda c: (c, j))
    return pl.pallas_call(
        body, grid=(nc,),
        in_specs=[blk(2), blk(3), blk(4), blk(5), _full((2, DH)), _full((1, HD)),
                  _full(w_st.shape), _full(masks.shape), pl.BlockSpec(memory_space=pl.ANY)],
        out_specs=[pl.BlockSpec((CH, DH), lambda c: (c, 1)),
                   pl.BlockSpec((CH, DH), lambda c: (c, 0)),
                   pl.BlockSpec((1, NH, CH, CH), lambda c: (c, 0, 0, 0)),
                   pl.BlockSpec((1, NH, HD, HD), lambda c: (c, 0, 0, 0))],
        out_shape=[SDS((T, D), BF16), SDS((T, DH), F32), SDS((nc, NH, CH, CH), F32), SDS((nc, NH, HD, HD), F32)],
        scratch_shapes=[pltpu.VMEM((NH, HD, HD), F32)],
        input_output_aliases={8: 0},
        compiler_params=_arb(), name="hgrn_fwd")(proj, proj, proj, proj, lower_bounds, gn_w, w_st, masks, ycat)


def _token_local(x, ycat, tgt, g1, nw2, sc2, sh2, g2, fw, w_out_b, w_fi_b, w_fo_b, tm):
    T = x.shape[0]
    inv_d = 1.0 / D

    def body(x_ref, y_ref, t_ref, g1_ref, nw2_ref, sc2_ref, sh2_ref, g2_ref, fw_ref, wo_ref, wfi_ref, wfo_ref,
             dy_ref, dx1_ref, h2_ref, act_ref, dff_ref, dgu_ref, dmix_ref, acc_ref):
        @pl.when(pl.program_id(0) == 0)
        def _():
            acc_ref[...] = jnp.zeros_like(acc_ref)

        def acc(row, val):
            acc_ref[row:row + 1, :] += jnp.sum(val, axis=0, keepdims=True)

        g1v, g2v = g1_ref[...], g2_ref[...]
        mix = _dot(y_ref[...], wo_ref[...])
        x1 = x_ref[...] + g1v * mix
        r2 = _rms(x1)
        xh2 = x1 * r2
        n2 = xh2 * nw2_ref[...]
        osc2 = 1.0 + sc2_ref[...]
        h2b = (n2 * osc2 + sh2_ref[...]).astype(BF16)
        h2_ref[...] = h2b
        ff = jnp.zeros((tm, D), F32)
        saved = []
        for kb in range(DFF // FFB):
            gate = _dot(h2b, wfi_ref[:, kb * FFB:(kb + 1) * FFB])
            up = _dot(h2b, wfi_ref[:, DFF + kb * FFB:DFF + (kb + 1) * FFB])
            sg = _sigmoid(gate)
            actb = (gate * sg * up).astype(BF16)
            act_ref[:, kb * FFB:(kb + 1) * FFB] = actb
            ff = ff + _dot(actb, wfo_ref[kb * FFB:(kb + 1) * FFB, :])
            saved.append((gate, up, sg))
        x2 = x1 + g2v * ff
        r3 = _rms(x2)
        xh3 = x2 * r3
        err = xh3 * fw_ref[...] - t_ref[...]
        acc(6, (0.5 * inv_d) * err * err)
        dy = err * inv_d
        acc(4, dy * xh3)
        dx2 = _rms_bwd(xh3, r3, dy * fw_ref[...])
        acc(0, dx2 * ff)
        dffb = (dx2 * g2v).astype(BF16)
        dff_ref[...] = dffb
        dh2 = jnp.zeros((tm, D), F32)
        for kb in range(DFF // FFB):
            gate, up, sg = saved[kb]
            da = _dot(dffb, wfo_ref[kb * FFB:(kb + 1) * FFB, :], NT)
            dgate = (da * up * (sg * (1.0 + gate * (1.0 - sg)))).astype(BF16)
            dup = (da * gate * sg).astype(BF16)
            dgu_ref[:, kb * FFB:(kb + 1) * FFB] = dgate
            dgu_ref[:, DFF + kb * FFB:DFF + (kb + 1) * FFB] = dup
            dh2 = dh2 + _dot(dgate, wfi_ref[:, kb * FFB:(kb + 1) * FFB], NT)
            dh2 = dh2 + _dot(dup, wfi_ref[:, DFF + kb * FFB:DFF + (kb + 1) * FFB], NT)
        acc(2, dh2)
        acc(1, dh2 * n2)
        dn2 = dh2 * osc2
        acc(3, dn2 * xh2)
        dx1 = dx2 + _rms_bwd(xh2, r2, dn2 * nw2_ref[...])
        acc(5, dx1 * mix)
        dmixb = (dx1 * g1v).astype(BF16)
        dmix_ref[...] = dmixb
        dy_ref[...] = _dot(dmixb, wo_ref[...], NT)
        dx1_ref[...] = dx1

    row = lambda i: (i, 0)
    vec = _full((1, D))
    return pl.pallas_call(
        body, grid=(T // tm,),
        in_specs=[pl.BlockSpec((tm, D), row), pl.BlockSpec((tm, D), row), pl.BlockSpec((tm, D), row),
                  vec, vec, vec, vec, vec, vec,
                  _resident((D, D)), _resident((D, 2 * DFF)), _resident((DFF, D))],
        out_specs=[pl.BlockSpec((tm, D), row), pl.BlockSpec((tm, D), row), pl.BlockSpec((tm, D), row),
                   pl.BlockSpec((tm, DFF), row), pl.BlockSpec((tm, D), row), pl.BlockSpec((tm, 2 * DFF), row),
                   pl.BlockSpec((tm, D), row), _full((8, D))],
        out_shape=[SDS((T, D), F32), SDS((T, D), F32), SDS((T, D), BF16), SDS((T, DFF), BF16), SDS((T, D), BF16),
                   SDS((T, 2 * DFF), BF16), SDS((T, D), BF16), SDS((8, D), F32)],
        compiler_params=_arb(), name="token_local")(x, ycat, tgt, g1, nw2, sc2, sh2, g2, fw, w_out_b, w_fi_b, w_fo_b)


def _gmlp_bwd(proj, dycat, ws_b, bst, lnw, lnb):
    T = proj.shape[0]
    nb = T // BLK

    def body(u_ref, v_ref, dy_ref, ws_ref, bst_ref, lnw_ref, lnb_ref, dp_ref, dws_ref, dbs_ref, dln_ref, dbs_acc):
        i = pl.program_id(0)

        @pl.when(i == 0)
        def _():
            dws_ref[...] = jnp.zeros_like(dws_ref)
            dln_ref[...] = jnp.zeros_like(dln_ref)
            dbs_acc[...] = jnp.zeros_like(dbs_acc)

        ug, dug, dvg, rstd, vhat, vnb, mixed = _gmlp_common(u_ref[...], v_ref[...], lnw_ref[...], lnb_ref[...], ws_ref, bst_ref)
        dya = dy_ref[...]
        dp_ref[:, 0:DG] = (dya * mixed * dug).astype(BF16)
        dmixed = dya * ug
        dbs_acc[...] += dmixed
        dmb = dmixed.astype(BF16)
        r = lax.broadcasted_iota(jnp.int32, (BLK, BLK), 0) // CH
        c = lax.broadcasted_iota(jnp.int32, (BLK, BLK), 1) // CH
        dvn = []
        for h in range(NH):
            sl = slice(h * HD, (h + 1) * HD)
            dws_ref[h] += jnp.where(r >= c, _dot(dmb[:, sl], vnb[:, sl], NT), 0.0)
            dvn.append(_dot(ws_ref[h], dmb[:, sl], TN))
        dvn = jnp.concatenate(dvn, axis=1)
        dln_ref[0:1, :] += jnp.sum(dvn * vhat, axis=0, keepdims=True)
        dln_ref[1:2, :] += jnp.sum(dvn, axis=0, keepdims=True)
        dvh = dvn * lnw_ref[...]
        dvgel = rstd * (dvh - jnp.mean(dvh, axis=-1, keepdims=True) - vhat * jnp.mean(dvh * vhat, axis=-1, keepdims=True))
        dp_ref[:, DG:2 * DG] = (dvgel * dvg).astype(BF16)

        @pl.when(i == nb - 1)
        def _():
            lane = lax.broadcasted_iota(jnp.int32, (BLK, HD), 1)
            out = jnp.zeros((BLK, HD), F32)
            for h in range(NH):
                out = out + jnp.where(lane == h, jnp.sum(dbs_acc[:, h * HD:(h + 1) * HD], axis=-1, keepdims=True), 0.0)
            dbs_ref[...] = out

    return pl.pallas_call(
        body, grid=(nb,),
        in_specs=[pl.BlockSpec((BLK, DG), lambda i: (i, 0)), pl.BlockSpec((BLK, DG), lambda i: (i, 1)),
                  pl.BlockSpec((BLK, DG), lambda i: (i, 0)),
                  _full((NH, BLK, BLK)), _full((BLK, NH)), _full((1, DG)), _full((1, DG))],
        out_specs=[pl.BlockSpec((BLK, 2 * DG), lambda i: (i, 2)), _full((NH, BLK, BLK)), _full((BLK, HD)), _full((8, DG))],
        out_shape=[SDS((T, DIN), BF16), SDS((NH, BLK, BLK), F32), SDS((BLK, HD), F32), SDS((8, DG), F32)],
        scratch_shapes=[pltpu.VMEM((BLK, DG), F32)],
        compiler_params=_arb(), name="gmlp_bwd")(proj, proj, dycat, ws_b, bst, lnw, lnb)


def _hgrn_bwd(proj, o_pre, a_all, st_all, dycat, lower_bounds, gn_w, dproj, tables):
    T = proj.shape[0]
    nc = T // CH
    w_st, w_st_t, masks, masks_t = tables
    n_lev = len(LEVELS)

    def body(q_ref, f_ref, i_ref, g_ref, o_ref, a_ref, st_ref, dy_ref, lbp_ref, gn_ref, w_ref, wt_ref, m_ref, mt_ref,
             dp_any, dp_ref, dlb_ref, dgn_ref, ds_scr, dx_scr):
        i = pl.program_id(0)

        @pl.when(i == 0)
        def _():
            ds_scr[...] = jnp.zeros_like(ds_scr)
            dlb_ref[...] = jnp.zeros_like(dlb_ref)
            dgn_ref[...] = jnp.zeros_like(dgn_ref)

        lb, omlb = _lower_bound(lbp_ref)
        q = q_ref[...]
        v = i_ref[...]
        g = g_ref[...]
        sq, qf, sig, f, k, e = _hgrn_gates(q, f_ref[...], lb, omlb, w_ref)
        eb = e[0:CH]
        ebl = eb[CH - 1:CH, :]
        ekd = e[CH:2 * CH]
        kd = k * ekd
        row = lax.broadcasted_iota(jnp.int32, (CH, 1), 0)
        eye = lax.broadcasted_iota(jnp.int32, (CH, CH), 0) == lax.broadcasted_iota(jnp.int32, (CH, CH), 1)
        dgn = jnp.zeros((1, HD), F32)
        dqf_h, dk_h, dv_h, dg_h = [], [], [], []
        for h in range(NH):
            sl = slice(h * HD, (h + 1) * HD)
            o = o_ref[:, sl]
            ro = _rms(o)
            oh = o * ro
            gh = g[:, sl]
            sg = _sigmoid(gh)
            dyb = dy_ref[:, sl]
            dgate = dyb * (oh * gn_ref[...])
            dg = dgate * (sg * (1.0 + gh * (1.0 - sg)))
            don = dyb * (gh * sg)
            dgn = dgn + jnp.sum(don * oh, axis=0, keepdims=True)
            do = _rms_bwd(oh, ro, don * gn_ref[...])
            dob = do.astype(BF16)
            vb = v[:, sl].astype(BF16)
            st0 = st_ref[0, h]
            dst1 = ds_scr[h]
            dst1b = dst1.astype(BF16)
            qh, kh, ebh, kdh = qf[:, sl], k[:, sl], eb[:, sl], kd[:, sl]
            qe = qh * ebh
            dqe = _dot(dob, st0.astype(BF16))
            ds_scr[h] = dst1 * ebl[:, sl] + _dot(dob, qe.astype(BF16), TN)
            dbl = ebl[:, sl] * jnp.sum(st0 * dst1, axis=0, keepdims=True)
            dkd = _dot(vb, dst1b)
            dv_h.append(_dot(a_ref[0, h].astype(BF16), dob, TN) + _dot(kdh.astype(BF16), dst1b, NT))
            da = _dot(dob, vb, NT)
            dat = _dot(vb, dob, NT)
            ddiag = jnp.sum(jnp.where(eye, da, 0.0), axis=-1, keepdims=True)
            dqi = ddiag * kh
            dki = ddiag * qh
            for li in range(n_lev):
                el, up, y = _level_factor(e, li, sl, row, qh, kh)
                dgs = m_ref[li] * da + mt_ref[li] * dat
                dyv = _dot(dgs.astype(BF16), y.astype(BF16))
                dx_scr[(2 + li) * CH:(3 + li) * CH, sl] = dyv * y
                dye = dyv * el
                dqi = dqi + jnp.where(up, dye, 0.0)
                dki = dki + jnp.where(up, 0.0, dye)
            dx_scr[0:CH, sl] = dqe * qe + jnp.where(row == CH - 1, dbl, 0.0)
            dx_scr[CH:2 * CH, sl] = dkd * kdh
            dqf_h.append(dqe * ebh + dqi)
            dk_h.append(dkd * ekd[:, sl] + dki)
            dg_h.append(dg)
        dgn_ref[0:1, :] += dgn
        dlf = _split_dot(wt_ref[...], dx_scr[...], 2)
        df = dlf / f - jnp.concatenate(dk_h, axis=1)
        dlb_ref[0:1, :] += jnp.sum(df * (1.0 - sig), axis=0, keepdims=True)
        dp_ref[:, 0:DH] = (jnp.concatenate(dqf_h, axis=1) * (sq * (1.0 + q * (1.0 - sq)))).astype(BF16)
        dp_ref[:, DH:2 * DH] = (df * omlb * sig * (1.0 - sig)).astype(BF16)
        dp_ref[:, 2 * DH:3 * DH] = jnp.concatenate(dv_h, axis=1).astype(BF16)
        dp_ref[:, 3 * DH:4 * DH] = jnp.concatenate(dg_h, axis=1).astype(BF16)

        @pl.when(i == nc - 1)
        def _():
            gl = dlb_ref[0:1, :] * lb * omlb
            dlb_ref[0:1, :] = gl
            dlb_ref[1:2, :] = -gl

    rev = lambda j: pl.BlockSpec((CH, DH), lambda c: (nc - 1 - c, j))
    return pl.pallas_call(
        body, grid=(nc,),
        in_specs=[rev(2), rev(3), rev(4), rev(5), rev(0),
                  pl.BlockSpec((1, NH, CH, CH), lambda c: (nc - 1 - c, 0, 0, 0)),
                  pl.BlockSpec((1, NH, HD, HD), lambda c: (nc - 1 - c, 0, 0, 0)),
                  rev(1), _full((2, DH)), _full((1, HD)),
                  _full(w_st.shape), _full(w_st_t.shape), _full(masks.shape), _full(masks_t.shape),
                  pl.BlockSpec(memory_space=pl.ANY)],
        out_specs=[pl.BlockSpec((CH, 4 * DH), lambda c: (nc - 1 - c, 0)), _full((8, DH)), _full((8, HD))],
        out_shape=[SDS((T, DIN), BF16), SDS((8, DH), F32), SDS((8, HD), F32)],
        scratch_shapes=[pltpu.VMEM((NH, HD, HD), F32), pltpu.VMEM(((2 + n_lev) * CH, DH), F32)],
        input_output_aliases={14: 0},
        compiler_params=_arb(), name="hgrn_bwd")(proj, proj, proj, proj, o_pre, a_all, st_all, dycat, lower_bounds, gn_w,
                                                 w_st, w_st_t, masks, masks_t, dproj)


def _proj_in_bwd(dproj, x, dx1, nw, sc, w_in_b, tm):
    T = x.shape[0]

    def body(dp_ref, x_ref, dx1_ref, nw_ref, sc_ref, w_ref, gx_ref, acc_ref):
        @pl.when(pl.program_id(0) == 0)
        def _():
            acc_ref[...] = jnp.zeros_like(acc_ref)

        dh = _dot(dp_ref[:, 0:4 * DH], w_ref[:, 2 * DG:DIN], NT) + _dot(dp_ref[:, 4 * DH:DIN], w_ref[:, 0:2 * DG], NT)
        xv = x_ref[...]
        r = _rms(xv)
        xh = xv * r
        n1 = xh * nw_ref[...]
        acc_ref[0:1, :] += jnp.sum(dh, axis=0, keepdims=True)
        acc_ref[1:2, :] += jnp.sum(dh * n1, axis=0, keepdims=True)
        dn = dh * (1.0 + sc_ref[...])
        acc_ref[2:3, :] += jnp.sum(dn * xh, axis=0, keepdims=True)
        gx_ref[...] = dx1_ref[...] + _rms_bwd(xh, r, dn * nw_ref[...])

    row = lambda i: (i, 0)
    return pl.pallas_call(
        body, grid=(T // tm,),
        in_specs=[pl.BlockSpec((tm, DIN), row), pl.BlockSpec((tm, D), row), pl.BlockSpec((tm, D), row),
                  _full((1, D)), _full((1, D)), _resident((D, DIN))],
        out_specs=[pl.BlockSpec((tm, D), row), _full((8, D))],
        out_shape=[SDS((T, D), F32), SDS((8, D), F32)],
        compiler_params=_arb(), name="proj_in_bwd")(dproj, x, dx1, nw, sc, w_in_b)


def _wgrad(a, b, bk, bn, tt, name):
    T, K = a.shape
    N = b.shape[1]
    nn, nk, nt = N // bn, K // bk, T // tt
    bmap = lambda n, k, t: (t, n)

    def body(a_ref, b_ref, o_ref):
        @pl.when(pl.program_id(2) == 0)
        def _():
            o_ref[...] = jnp.zeros_like(o_ref)

        o_ref[0] += _dot(a_ref[...], b_ref[...], TN)

    return pl.pallas_call(
        body, grid=(nn, nk, nt),
        in_specs=[pl.BlockSpec((tt, bk), lambda n, k, t: (t, k)), pl.BlockSpec((tt, bn), bmap)],
        out_specs=pl.BlockSpec((1, bk, bn), lambda n, k, t: (n, k, 0)),
        out_shape=SDS((nn, K, bn), F32),
        compiler_params=_arb(3), name=name)(a, b)


def _adam_math(w, g, m, v):
    m = B1 * m + (1.0 - B1) * g
    v = B2 * v + (1.0 - B2) * (g * g)
    m_hat = m / (1.0 - B1 ** STEP)
    v_hat = v / (1.0 - B2 ** STEP)
    return -LR * (m_hat / (jnp.sqrt(v_hat) + AEPS) + WD * w), m, v


def _adamw_halves(w, mine, sibling, m, v, c_idx, rb, name):
    R, C = w.shape
    nb = (R // 2) // rb

    def body(c_ref, w_ref, a_ref, b_ref, m_ref, v_ref, g_out, d_out, m_out, v_out):
        g = jnp.where(pl.program_id(0) == c_ref[0], a_ref[...], b_ref[...])
        g_out[...] = g
        d_out[...], m_out[...], v_out[...] = _adam_math(w_ref[...], g, m_ref[...], v_ref[...])

    whole = pl.BlockSpec((rb, C), lambda hh, i, cr: (hh * nb + i, 0))
    half = pl.BlockSpec((rb, C), lambda hh, i, cr: (i, 0))
    return pl.pallas_call(
        body,
        grid_spec=pltpu.PrefetchScalarGridSpec(
            num_scalar_prefetch=1, grid=(2, nb), in_specs=[whole, half, half, whole, whole], out_specs=[whole] * 4),
        out_shape=[SDS((R, C), F32)] * 4, compiler_params=_arb(2), name=name)(c_idx, w, mine, sibling, m, v)


def _ada_forward(c_all, w_ada):
    n = w_ada.shape[1]

    def body(c_ref, w_ref, ca_ref, p_ref):
        cv = c_ref[...]
        ca = cv * _sigmoid(cv)
        ca_ref[...] = ca
        p_ref[...] = _dot(ca, w_ref[...], precision=HIGHEST)

    return pl.pallas_call(
        body, grid=(n // 512,),
        in_specs=[_full((N_DEV, D)), pl.BlockSpec((D, 512), lambda i: (0, i))],
        out_specs=[_full((N_DEV, D)), pl.BlockSpec((N_DEV, 512), lambda i: (0, i))],
        out_shape=[SDS((N_DEV, D), F32), SDS((N_DEV, n), F32)],
        compiler_params=_arb(), name="ada_forward")(c_all, w_ada)


def _ada_wgrad_adam(cact_t, dada, w, m, v):
    R, C = w.shape
    rb = 256

    def body(c_ref, d_ref, w_ref, m_ref, v_ref, g_out, d_out, m_out, v_out):
        g = _dot(c_ref[...], d_ref[...], precision=HIGHEST)
        g_out[...] = g
        d_out[...], m_out[...], v_out[...] = _adam_math(w_ref[...], g, m_ref[...], v_ref[...])

    spec = pl.BlockSpec((rb, C), lambda i: (i, 0))
    return pl.pallas_call(
        body, grid=(R // rb,),
        in_specs=[pl.BlockSpec((rb, N_DEV), lambda i: (i, 0)), _full((N_DEV, C)), spec, spec, spec],
        out_specs=[spec] * 4, out_shape=[SDS((R, C), F32)] * 4,
        compiler_params=_arb(), name="ada_wgrad_adam")(cact_t, dada, w, m, v)


def _small_finalize(gathered, w, m, v):
    def body(ga_ref, w_ref, m_ref, v_ref, g_out, d_out, m_out, v_out):
        g = ga_ref[0:SMALL_ROWS, :]
        for dev in range(1, N_DEV):
            g = g + ga_ref[dev * SMALL_ROWS:(dev + 1) * SMALL_ROWS, :]
        g_out[...] = g
        d_out[...], m_out[...], v_out[...] = _adam_math(w_ref[...], g, m_ref[...], v_ref[...])

    return pl.pallas_call(
        body, out_shape=[SDS((SMALL_ROWS, D), F32)] * 4, name="small_finalize")(gathered, w, m, v)


def _position():
    x, y, c = lax.axis_index("x"), lax.axis_index("y"), lax.axis_index("c")
    return x, y, c


def _chip_at(x, y, r):
    return (x ^ (r >> 1), y ^ (r & 1))


def _all_gather_rows(block, name):
    m_per, n = block.shape

    def body(x_ref, out_ref, send_sems, recv_sems, local_sem):
        x, y, c = _position()
        me, sibling = (x, y, c), (x, y, 1 - c)
        chips = [_chip_at(x, y, r) for r in (1, 2, 3)]

        def rows(px, py, pc):
            return out_ref.at[pl.ds((4 * px + 2 * py + pc) * m_per, m_per), :]

        def copy(k, blk, to, src=None):
            return pltpu.make_async_remote_copy(
                src_ref=rows(*blk) if src is None else src, dst_ref=rows(*blk),
                send_sem=send_sems.at[k], recv_sem=recv_sems.at[k], device_id=to, device_id_type=MESH)

        mine = pltpu.make_async_copy(x_ref, rows(*me), local_sem)
        mine.start()
        first = [copy(0, me, sibling, src=x_ref)]
        first += [copy(1 + j, me, (*chip, c), src=x_ref) for j, chip in enumerate(chips)]
        for cp in first:
            cp.start()
        passed = [copy(4 + j, (*chip, c), sibling) for j, chip in enumerate(chips)]
        for j, chip in enumerate(chips):
            copy(1 + j, (*chip, c), me).wait_recv()
            passed[j].start()
        copy(0, sibling, me).wait_recv()
        for j, chip in enumerate(chips):
            copy(4 + j, (*chip, 1 - c), me).wait_recv()
        for cp in first + passed:
            cp.wait_send()
        mine.wait()

    return pl.pallas_call(
        body, out_shape=SDS((N_DEV * m_per, n), block.dtype),
        in_specs=[pl.BlockSpec(memory_space=pltpu.VMEM)], out_specs=pl.BlockSpec(memory_space=pltpu.VMEM),
        scratch_shapes=[pltpu.SemaphoreType.DMA((7,)), pltpu.SemaphoreType.DMA((7,)), pltpu.SemaphoreType.DMA],
        name=name)(block)


def _gather_weights(shards):
    nw = len(shards)

    def body(*refs):
        ins, outs = refs[:nw], refs[nw:2 * nw]
        send_sems, recv_sems, local_sems = refs[2 * nw:]
        x, y, c = _position()
        j = 2 * x + y
        started = []
        for w, (arr, axis) in enumerate(shards):
            size = arr.shape[axis]

            def slot(chip_idx, w=w, axis=axis, size=size):
                if axis == 0:
                    return outs[w].at[pl.ds(chip_idx * size, size), :]
                return outs[w].at[:, pl.ds(chip_idx * size, size)]

            local = pltpu.make_async_copy(ins[w], slot(j), local_sems.at[w])
            local.start()
            started.append(local)
            for r in (1, 2, 3):
                cx, cy = _chip_at(x, y, r)
                k = 3 * w + r - 1
                cp = pltpu.make_async_remote_copy(
                    src_ref=ins[w], dst_ref=slot(j), send_sem=send_sems.at[k], recv_sem=recv_sems.at[k],
                    device_id=(cx, cy, c), device_id_type=MESH)
                cp.start()
                started.append((cp, slot(j ^ r), k, w))
        for item in started:
            if isinstance(item, tuple):
                cp, from_slot, k, w = item
                cp.wait_send()
                pltpu.make_async_remote_copy(
                    src_ref=ins[w], dst_ref=from_slot, send_sem=send_sems.at[k], recv_sem=recv_sems.at[k],
                    device_id=(x, y, c), device_id_type=MESH).wait_recv()
            else:
                item.wait()

    out_shape = []
    for arr, axis in shards:
        shp = list(arr.shape)
        shp[axis] *= N_CHIPS
        out_shape.append(SDS(tuple(shp), arr.dtype))
    anyspec = pl.BlockSpec(memory_space=pl.ANY)
    return pl.pallas_call(
        body, out_shape=out_shape, in_specs=[anyspec] * nw, out_specs=[anyspec] * nw,
        scratch_shapes=[pltpu.SemaphoreType.DMA((3 * nw,)), pltpu.SemaphoreType.DMA((3 * nw,)),
                        pltpu.SemaphoreType.DMA((nw,))],
        name="gather_weights")(*[a for a, _ in shards])


def _exchange_core_halves(grads):
    nw = len(grads)

    def body(*refs):
        ins, outs = refs[:nw], refs[nw:2 * nw]
        send_sems, recv_sems = refs[2 * nw:]
        x, y, c = _position()
        cps = []
        for w in range(nw):
            cp = pltpu.make_async_remote_copy(
                src_ref=ins[w].at[:, 1 - c], dst_ref=outs[w], send_sem=send_sems.at[w], recv_sem=recv_sems.at[w],
                device_id=(x, y, 1 - c), device_id_type=MESH)
            cp.start()
            cps.append(cp)
        for cp in cps:
            cp.wait()

    anyspec = pl.BlockSpec(memory_space=pl.ANY)
    return pl.pallas_call(
        body, out_shape=[SDS((g.shape[0], g.shape[2], g.shape[3]), F32) for g in grads],
        in_specs=[anyspec] * nw, out_specs=[anyspec] * nw,
        scratch_shapes=[pltpu.SemaphoreType.DMA((nw,)), pltpu.SemaphoreType.DMA((nw,))],
        name="exchange_core_halves")(*grads)


def _add_core_halves(g4, recv, c_idx, rb, name):
    ns, _, rh, C = g4.shape

    def body(c_ref, g_ref, r_ref, o_ref):
        o_ref[...] = (g_ref[0] + r_ref[...]).astype(BF16)

    return pl.pallas_call(
        body,
        grid_spec=pltpu.PrefetchScalarGridSpec(
            num_scalar_prefetch=1, grid=(ns, rh // rb),
            in_specs=[pl.BlockSpec((1, 1, rb, C), lambda s, i, cr: (s, cr[0], i, 0)),
                      pl.BlockSpec((1, rb, C), lambda s, i, cr: (s, i, 0))],
            out_specs=pl.BlockSpec((1, rb, C), lambda s, i, cr: (s, i, 0))),
        out_shape=SDS((ns, rh, C), BF16), compiler_params=_arb(2), name=name)(c_idx, g4, recv)


def _exchange_chips(sums):
    nw = len(sums)

    def body(*refs):
        ins, outs = refs[:nw], refs[nw:2 * nw]
        send_sems, recv_sems = refs[2 * nw:]
        x, y, c = _position()
        j = 2 * x + y
        started = []
        for w in range(nw):
            for r in (1, 2, 3):
                cx, cy = _chip_at(x, y, r)
                k = 3 * w + r - 1
                cp = pltpu.make_async_remote_copy(
                    src_ref=ins[w].at[j ^ r], dst_ref=outs[w].at[r - 1], send_sem=send_sems.at[k], recv_sem=recv_sems.at[k],
                    device_id=(cx, cy, c), device_id_type=MESH)
                cp.start()
                started.append(cp)
        for cp in started:
            cp.wait()

    anyspec = pl.BlockSpec(memory_space=pl.ANY)
    return pl.pallas_call(
        body, out_shape=[SDS((3,) + s.shape[1:], s.dtype) for s in sums], in_specs=[anyspec] * nw, out_specs=[anyspec] * nw,
        scratch_shapes=[pltpu.SemaphoreType.DMA((3 * nw,)), pltpu.SemaphoreType.DMA((3 * nw,))],
        name="exchange_chips")(*sums)


def _add_chips(own, slots, order, rb, name):
    _, rh, C = slots.shape

    def body(o_ref, own_ref, a_ref, b_ref, c_ref, d_ref, out_ref):
        mine = own_ref[0].astype(F32)
        t = [jnp.where(o_ref[i] == 0, mine, r[0].astype(F32)) for i, r in enumerate((a_ref, b_ref, c_ref, d_ref))]
        out_ref[...] = ((t[0] + t[1]) + t[2]) + t[3]

    def spec(i):
        return pl.BlockSpec((1, rb, C), lambda t, o: (jnp.maximum(o[i], 1) - 1, t, 0))

    return pl.pallas_call(
        body,
        grid_spec=pltpu.PrefetchScalarGridSpec(
            num_scalar_prefetch=1, grid=(rh // rb,),
            in_specs=[pl.BlockSpec((1, rb, C), lambda t, o: (o[4], t, 0)), spec(0), spec(1), spec(2), spec(3)],
            out_specs=pl.BlockSpec((rb, C), lambda t, o: (t, 0))),
        out_shape=SDS((rh, C), F32), compiler_params=_arb(), name=name)(order, own, slots, slots, slots, slots)


def _share_halves(halves):
    nw = len(halves)

    def body(*refs):
        ins, outs = refs[:nw], refs[nw:2 * nw]
        send_sems, recv_sems = refs[2 * nw:]
        x, y, c = _position()
        started = []
        for w in range(nw):
            cp = pltpu.make_async_remote_copy(
                src_ref=ins[w], dst_ref=outs[w], send_sem=send_sems.at[w], recv_sem=recv_sems.at[w],
                device_id=(x, y, 1 - c), device_id_type=MESH)
            cp.start()
            started.append(cp)
        for cp in started:
            cp.wait()

    anyspec = pl.BlockSpec(memory_space=pl.ANY)
    return pl.pallas_call(
        body, out_shape=[SDS(h.shape, F32) for h in halves], in_specs=[anyspec] * nw, out_specs=[anyspec] * nw,
        scratch_shapes=[pltpu.SemaphoreType.DMA((nw,)), pltpu.SemaphoreType.DMA((nw,))],
        name="share_halves")(*halves)


def _pack_small(b_ada, norm1_w, norm2_w, final_norm_w, v_ln_w, v_ln_b, lower_bounds, b_s, gn_w, w_s):
    parts = [b_ada, norm1_w, norm2_w, final_norm_w, v_ln_w, v_ln_b, lower_bounds, b_s, gn_w,
             jnp.zeros((D - NH * BLK - HD,), F32), w_s, jnp.zeros(((SMALL_ROWS - 76) * D,), F32)]
    return jnp.concatenate([p.reshape(-1) for p in parts]).reshape(SMALL_ROWS, D)


def _unpack_small(p):
    return dict(
        b_ada=p[0:6].reshape(1, 6 * D), norm1_w=p[6:7], norm2_w=p[7:8], final_norm_w=p[8],
        v_ln_w=p[9:10, 0:DG], v_ln_b=p[9:10, DG:D], lower_bounds=p[10].reshape(2, DH),
        b_s=p[11, 0:NH * BLK].reshape(1, NH, BLK), gn_w=p[11:12, NH * BLK:NH * BLK + HD],
        w_s=p[12:76].reshape(1, NH, BLK, BLK))


def _row_block(r):
    for cand in (256, 176, 128, 64, 32, 16, 8):
        if r % cand == 0:
            return cand
    return r


def kernel(x, c, w_ada, b_ada, norm1_w, w_in, w_s, b_s, v_ln_w, v_ln_b, lower_bounds, gn_w, w_out, norm2_w, w_ffn_in, w_ffn_out, final_norm_w, loss_target, m_w_ada, m_b_ada, m_norm1_w, m_w_in, m_w_s, m_b_s, m_v_ln_w, m_v_ln_b, m_lower_bounds, m_gn_w, m_w_out, m_norm2_w, m_w_ffn_in, m_w_ffn_out, m_final_norm_w, v_w_ada, v_b_ada, v_norm1_w, v_w_in, v_w_s, v_b_s, v_v_ln_w, v_v_ln_b, v_lower_bounds, v_gn_w, v_w_out, v_norm2_w, v_w_ffn_in, v_w_ffn_out, v_final_norm_w):
    T = x.shape[1]
    tm = min(256, T)
    px, py, pc = _position()
    chip = 2 * px + py
    me = 4 * px + 2 * py + pc
    x2d = x.reshape(T, D)
    tgt = loss_target.reshape(T, D)

    c_all = _all_gather_rows(jnp.broadcast_to(c, (8, D)), "gather_c").reshape(N_DEV, 8, D)[:, 0, :]
    cact, ada_part = _ada_forward(c_all, w_ada[0])
    n_ada = ada_part.shape[1]
    ada_all = _all_gather_rows(ada_part, "gather_ada").reshape(N_CHIPS, 2, N_DEV, n_ada)[:, 0]
    ada = lax.dynamic_index_in_dim(ada_all, me, axis=1, keepdims=False).reshape(1, 6 * D) + b_ada
    sh1, sc1, g1, sh2, sc2, g2 = [ada[:, i * D:(i + 1) * D] for i in range(6)]

    w_in_b, w_out_b, w_fi_b, w_fo_b = _gather_weights(
        [(w_in[0].astype(BF16), 1), (w_out[0].astype(BF16), 0), (w_ffn_in[0].astype(BF16), 1), (w_ffn_out[0].astype(BF16), 0)])

    rr = lax.broadcasted_iota(jnp.int32, (BLK, BLK), 0) // CH
    cc = lax.broadcasted_iota(jnp.int32, (BLK, BLK), 1) // CH
    ws_b = jnp.where((rr >= cc)[None], w_s[0], 0.0).astype(BF16)
    bst = b_s[0].T
    lnw, lnb = v_ln_w, v_ln_b
    nw1, nw2, fw = norm1_w, norm2_w, final_norm_w.reshape(1, D)

    h1, proj = _proj_in(x2d, nw1, sc1, sh1, w_in_b, tm)
    ycat = _gmlp_fwd(proj, ws_b, bst, lnw, lnb)
    tables = _hgrn_tables()
    ycat, o_pre, a_all, st_all = _hgrn_fwd(proj, lower_bounds, gn_w, ycat, tables)

    dycat, dx1, h2, act, dff, dgu, dmix, acc2 = _token_local(
        x2d, ycat, tgt, g1, nw2, sc2, sh2, g2, fw, w_out_b, w_fi_b, w_fo_b, tm)

    dproj, dws, dbs, dln = _gmlp_bwd(proj, dycat, ws_b, bst, lnw, lnb)
    dproj, dlb, dgn = _hgrn_bwd(proj, o_pre, a_all, st_all, dycat, lower_bounds, gn_w, dproj, tables)
    grad_x, acc1 = _proj_in_bwd(dproj, x2d, dx1, nw1, sc1, w_in_b, tm)

    tt = min(512, T)
    g_in = _wgrad(h1, dproj, D, D, tt, "wgrad_in")
    g_in = jnp.concatenate([g_in[2], g_in[0], g_in[1]], axis=1).reshape(D, N_CHIPS, DIN // N_CHIPS).transpose(1, 0, 2)
    g_out = _wgrad(ycat, dmix, D, D, tt, "wgrad_out").reshape(N_CHIPS, D // N_CHIPS, D)
    g_fi = _wgrad(h2, dgu, D, FFB, tt, "wgrad_ffn_in")
    g_fo = _wgrad(act, dff, FFB, D, tt, "wgrad_ffn_out").reshape(N_CHIPS, DFF // N_CHIPS, D)

    big = [g_in, g_out, g_fi, g_fo]
    g4 = [g.reshape(N_CHIPS, 2, g.shape[1] // 2, g.shape[2]) for g in big]
    recv = _exchange_core_halves(g4)
    c_idx = jnp.reshape(pc, (1,)).astype(jnp.int32)
    names = ["in", "out", "ffn_in", "ffn_out"]
    sums = [_add_core_halves(a, b, c_idx, _row_block(a.shape[2]), "add_core_" + n) for a, b, n in zip(g4, recv, names)]
    slots = _exchange_chips(sums)
    order = jnp.concatenate([chip ^ jnp.arange(N_CHIPS, dtype=jnp.int32), jnp.reshape(chip, (1,))]).astype(jnp.int32)
    halves = [_add_chips(o, s, order, _row_block(s.shape[1]), "add_chips_" + n) for o, s, n in zip(sums, slots, names)]
    sibling_halves = _share_halves(halves)

    big_w = [(w_in, m_w_in, v_w_in), (w_out, m_w_out, v_w_out), (w_ffn_in, m_w_ffn_in, v_w_ffn_in),
             (w_ffn_out, m_w_ffn_out, v_w_ffn_out)]
    big_out = []
    for mine, sib, (w, m, v), n in zip(halves, sibling_halves, big_w, names):
        res = _adamw_halves(w[0], mine, sib, m[0], v[0], c_idx, _row_block(mine.shape[0]), "adamw_" + n)
        big_out.append([r[None] for r in res])

    d_ada = jnp.stack([acc1[0], acc1[1], acc2[5], acc2[2], acc2[1], acc2[0]]).reshape(1, 6 * D)
    small_g = _pack_small(d_ada, acc1[2], acc2[3], acc2[4], dln[0], dln[1], dlb[0:2], dbs[:, 0:NH].T, dgn[0], dws)
    gathered = _all_gather_rows(small_g, "gather_small")
    sw = _pack_small(b_ada, norm1_w, norm2_w, final_norm_w, v_ln_w, v_ln_b, lower_bounds, b_s, gn_w, w_s)
    sm = _pack_small(m_b_ada, m_norm1_w, m_norm2_w, m_final_norm_w, m_v_ln_w, m_v_ln_b, m_lower_bounds, m_b_s, m_gn_w, m_w_s)
    sv = _pack_small(v_b_ada, v_norm1_w, v_norm2_w, v_final_norm_w, v_v_ln_w, v_v_ln_b, v_lower_bounds, v_b_s, v_gn_w, v_w_s)
    small = [_unpack_small(p) for p in _small_finalize(gathered, sw, sm, sv)]

    dada_all = gathered.reshape(N_DEV, SMALL_ROWS, D)[:, 0:6, :].reshape(N_DEV, 6 * D)
    dada = lax.dynamic_slice_in_dim(dada_all, chip * n_ada, n_ada, axis=1)
    ada_out = [o[None] for o in _ada_wgrad_adam(cact.T, dada, w_ada[0], m_w_ada[0], v_w_ada[0])]

    loss = lax.psum(jnp.sum(acc2[6]), ("x", "y", "c"))

    order_names = ['w_ada', 'b_ada', 'norm1_w', 'w_in', 'w_s', 'b_s', 'v_ln_w', 'v_ln_b', 'lower_bounds', 'gn_w',
                   'w_out', 'norm2_w', 'w_ffn_in', 'w_ffn_out', 'final_norm_w']
    big_idx = {'w_in': 0, 'w_out': 1, 'w_ffn_in': 2, 'w_ffn_out': 3}
    outs = [loss, grad_x.reshape(1, T, D)]
    for kind in range(4):
        for n in order_names:
            if n == 'w_ada':
                outs.append(ada_out[kind])
            elif n in big_idx:
                outs.append(big_out[big_idx[n]][kind])
            else:
                outs.append(small[kind][n])
    return tuple(outs)
```

```python
import functools

import jax
import jax.numpy as jnp
import numpy as np
from jax import lax
from jax.experimental import pallas as pl
from jax.experimental.pallas import tpu as pltpu

F32 = jnp.float32
BF16 = jnp.bfloat16
SDS = jax.ShapeDtypeStruct
MESH = pl.DeviceIdType.MESH
HIGHEST = lax.Precision.HIGHEST

D = 1024
DG = 512
DH = 512
NH = 4
HD = 128
BLK = 128
CH = 64
DFF = 2816
DIN = 3072
FFB = 1408
LEVELS = (64, 32, 16, 8, 4, 2)
N_CHIPS = 4
N_DEV = 8
EPS = 1e-6
LR, B1, B2, AEPS, WD, STEP = 0.001, 0.9, 0.999, 1e-08, 0.01, 10
SMALL_ROWS = 80

NT = (((1,), (1,)), ((), ()))
TN = (((0,), (0,)), ((), ()))


def _full(shape):
    nd = len(shape)
    return pl.BlockSpec(shape, lambda *_: (0,) * nd)


def _resident(shape):
    nd = len(shape)
    return pl.BlockSpec(shape, lambda *_: (0,) * nd, pipeline_mode=pl.Buffered(1))


def _arb(n=1):
    return pltpu.CompilerParams(dimension_semantics=("arbitrary",) * n)


def _dot(a, b, dims=None, precision=None):
    if dims is None:
        return jnp.dot(a, b, preferred_element_type=F32, precision=precision)
    return lax.dot_general(a, b, dims, preferred_element_type=F32, precision=precision)


def _sigmoid(x):
    return jax.nn.sigmoid(x)


def _gelu_parts(x):
    cdf = 0.5 * (1.0 + lax.erf(x * 0.7071067811865476))
    pdf = jnp.exp(-0.5 * x * x) * 0.3989422804014327
    return x * cdf, cdf + x * pdf


def _rms(x):
    return lax.rsqrt(jnp.mean(x * x, axis=-1, keepdims=True) + EPS)


def _rms_bwd(xhat, r, gw):
    return r * (gw - xhat * jnp.mean(xhat * gw, axis=-1, keepdims=True))


def _lower_bound(lbp_ref):
    l0, l1 = lbp_ref[0:1, :], lbp_ref[1:2, :]
    m = jnp.maximum(l0, l1)
    e0, e1 = jnp.exp(l0 - m), jnp.exp(l1 - m)
    return e0 / (e0 + e1), e1 / (e0 + e1)


def _proj_in(x, nw, sc, sh, w_in_b, tm):
    T = x.shape[0]

    def body(x_ref, nw_ref, sc_ref, sh_ref, w_ref, h_ref, p_ref):
        xv = x_ref[...]
        h = ((xv * _rms(xv)) * nw_ref[...]) * (1.0 + sc_ref[...]) + sh_ref[...]
        hb = h.astype(BF16)
        h_ref[...] = hb
        p_ref[...] = _dot(hb, w_ref[...])

    row = lambda i: (i, 0)
    return pl.pallas_call(
        body, grid=(T // tm,),
        in_specs=[pl.BlockSpec((tm, D), row), _full((1, D)), _full((1, D)), _full((1, D)), _resident((D, DIN))],
        out_specs=[pl.BlockSpec((tm, D), row), pl.BlockSpec((tm, DIN), row)],
        out_shape=[SDS((T, D), BF16), SDS((T, DIN), F32)],
        compiler_params=_arb(), name="proj_in")(x, nw, sc, sh, w_in_b)


def _gmlp_common(u, v, lnw, lnb, ws_ref, bst_ref):
    ug, dug = _gelu_parts(u)
    vg, dvg = _gelu_parts(v)
    mu = jnp.mean(vg, axis=-1, keepdims=True)
    vc = vg - mu
    rstd = lax.rsqrt(jnp.mean(vc * vc, axis=-1, keepdims=True) + EPS)
    vhat = vc * rstd
    vn = vhat * lnw + lnb
    vnb = vn.astype(BF16)
    mixed = []
    for h in range(NH):
        sl = slice(h * HD, (h + 1) * HD)
        mixed.append(_dot(ws_ref[h], vnb[:, sl]) + bst_ref[:, h:h + 1])
    return ug, dug, dvg, rstd, vhat, vnb, jnp.concatenate(mixed, axis=1)


def _gmlp_fwd(proj, ws_b, bst, lnw, lnb):
    T = proj.shape[0]

    def body(u_ref, v_ref, ws_ref, bst_ref, lnw_ref, lnb_ref, y_ref):
        ug, _, _, _, _, _, mixed = _gmlp_common(u_ref[...], v_ref[...], lnw_ref[...], lnb_ref[...], ws_ref, bst_ref)
        y_ref[...] = (ug * mixed).astype(BF16)

    return pl.pallas_call(
        body, grid=(T // BLK,),
        in_specs=[pl.BlockSpec((BLK, DG), lambda i: (i, 0)), pl.BlockSpec((BLK, DG), lambda i: (i, 1)),
                  _full((NH, BLK, BLK)), _full((BLK, NH)), _full((1, DG)), _full((1, DG))],
        out_specs=pl.BlockSpec((BLK, DG), lambda i: (i, 0)),
        out_shape=SDS((T, D), BF16),
        compiler_params=_arb(), name="gmlp_fwd")(proj, proj, ws_b, bst, lnw, lnb)


def _hgrn_tables():
    t = np.arange(CH)[:, None]
    j = np.arange(CH)[None, :]
    blocks = [j <= t, j > t]
    masks = []
    for n in LEVELS:
        mid = t - t % n + n // 2
        blocks.append(np.where(t >= mid, (j >= mid) & (j <= t), (j > t) & (j < mid)))
        masks.append((t // n == j // n) & (t % n >= n // 2) & (j % n < n // 2))
    w = np.concatenate(blocks, axis=0).astype(np.float32)
    m = np.stack(masks).astype(np.float32)
    return (jnp.asarray(w, BF16), jnp.asarray(w.T, BF16), jnp.asarray(m), jnp.asarray(m.transpose(0, 2, 1)))


def _split_dot(w, x, parts):
    acc = None
    for _ in range(parts):
        piece = x.astype(BF16)
        term = _dot(w, piece)
        acc = term if acc is None else acc + term
        x = x - piece.astype(F32)
    return acc


def _hgrn_gates(q, fl, lb, omlb, w_ref):
    sq = _sigmoid(q)
    qf = q * sq
    sig = _sigmoid(fl)
    f = lb + omlb * sig
    k = 1.0 - f
    e = jnp.exp(_split_dot(w_ref[...], jnp.log(f), 3))
    return sq, qf, sig, f, k, e


def _level_factor(e, li, sl, row, qh, kh):
    el = e[(2 + li) * CH:(3 + li) * CH, sl]
    up = (row & (LEVELS[li] // 2)) != 0
    return el, up, el * jnp.where(up, qh, kh)


def _hgrn_fwd(proj, lower_bounds, gn_w, ycat, tables, placed, axes):
    T = proj.shape[0]
    nc = T // CH
    w_st, _, masks, _ = tables
    nw = len(placed)
    pass_step = (5 * nc) // 8

    def body(*refs):
        q_ref, f_ref, i_ref, g_ref, lbp_ref, gn_ref, w_ref, m_ref = refs[:8]
        y_ref, o_ref, a_ref, st_ref = refs[9 + nw:13 + nw]
        s_scr, send_sems, recv_sems = refs[13 + 2 * nw:]
        gather = _WeightGather(refs[13 + nw:13 + 2 * nw], axes, send_sems, recv_sems)
        step = pl.program_id(0)

        @pl.when(step == 0)
        def _():
            gather.start()
            s_scr[...] = jnp.zeros_like(s_scr)

        @pl.when(step == pass_step)
        def _():
            gather.forward()

        lb, omlb = _lower_bound(lbp_ref)
        v = i_ref[...]
        g = g_ref[...]
        _, qf, _, _, k, e = _hgrn_gates(q_ref[...], f_ref[...], lb, omlb, w_ref)
        eb = e[0:CH]
        ebl = eb[CH - 1:CH, :]
        kd = k * e[CH:2 * CH]
        row = lax.broadcasted_iota(jnp.int32, (CH, 1), 0)
        eye = lax.broadcasted_iota(jnp.int32, (CH, CH), 0) == lax.broadcasted_iota(jnp.int32, (CH, CH), 1)
        for h in range(NH):
            sl = slice(h * HD, (h + 1) * HD)
            st0 = s_scr[h]
            st_ref[0, h] = st0
            qh, kh = qf[:, sl], k[:, sl]
            inter = _dot((qh * eb[:, sl]).astype(BF16), st0.astype(BF16), NT)
            a = jnp.where(eye, jnp.sum(qh * kh, axis=-1, keepdims=True), 0.0)
            for li in range(len(LEVELS)):
                _, _, y = _level_factor(e, li, sl, row, qh, kh)
                yb = y.astype(BF16)
                a = a + m_ref[li] * _dot(yb, yb, NT)
            a_ref[0, h] = a
            vb = v[:, sl].astype(BF16)
            o = inter + _dot(a.astype(BF16), vb)
            s_scr[h] = st0 * ebl[:, sl] + _dot(vb, kd[:, sl].astype(BF16), TN)
            o_ref[:, sl] = o
            gh = g[:, sl]
            y_ref[:, sl] = (((o * _rms(o)) * gn_ref[...]) * (gh * _sigmoid(gh))).astype(BF16)

        @pl.when(step == nc - 1)
        def _():
            gather.finish()

    blk = lambda j: pl.BlockSpec((CH, DH), lambda c: (c, j))
    anyspec = pl.BlockSpec(memory_space=pl.ANY)
    res = pl.pallas_call(
        body, grid=(nc,),
        in_specs=[blk(2), blk(3), blk(4), blk(5), _full((2, DH)), _full((1, HD)),
                  _full(w_st.shape), _full(masks.shape), anyspec] + [anyspec] * nw,
        out_specs=[pl.BlockSpec((CH, DH), lambda c: (c, 1)),
                   pl.BlockSpec((CH, DH), lambda c: (c, 0)),
                   pl.BlockSpec((1, NH, CH, CH), lambda c: (c, 0, 0, 0)),
                   pl.BlockSpec((1, NH, HD, HD), lambda c: (c, 0, 0, 0))] + [anyspec] * nw,
        out_shape=[SDS((T, D), BF16), SDS((T, DH), F32), SDS((nc, NH, CH, CH), F32), SDS((nc, NH, HD, HD), F32)]
        + [SDS(a.shape, a.dtype) for a in placed],
        scratch_shapes=[pltpu.VMEM((NH, HD, HD), F32)] + _gather_sems(nw),
        input_output_aliases={8: 0, **{9 + i: 4 + i for i in range(nw)}},
        compiler_params=_arb(), name="hgrn_fwd")(proj, proj, proj, proj, lower_bounds, gn_w, w_st, masks, ycat, *placed)
    return res[:4], res[4:]


def _token_local(x, ycat, tgt, g1, nw2, sc2, sh2, g2, fw, w_out_b, w_fi_b, w_fo_b, tm):
    T = x.shape[0]
    inv_d = 1.0 / D

    def body(x_ref, y_ref, t_ref, g1_ref, nw2_ref, sc2_ref, sh2_ref, g2_ref, fw_ref, wo_ref, wfi_ref, wfo_ref,
             dy_ref, dx1_ref, h2_ref, act_ref, dff_ref, dgu_ref, dmix_ref, acc_ref):
        @pl.when(pl.program_id(0) == 0)
        def _():
            acc_ref[...] = jnp.zeros_like(acc_ref)

        def acc(row, val):
            acc_ref[row:row + 1, :] += jnp.sum(val, axis=0, keepdims=True)

        g1v, g2v = g1_ref[...], g2_ref[...]
        mix = _dot(y_ref[...], wo_ref[...])
        x1 = x_ref[...] + g1v * mix
        r2 = _rms(x1)
        xh2 = x1 * r2
        n2 = xh2 * nw2_ref[...]
        osc2 = 1.0 + sc2_ref[...]
        h2b = (n2 * osc2 + sh2_ref[...]).astype(BF16)
        h2_ref[...] = h2b
        ff = jnp.zeros((tm, D), F32)
        saved = []
        for kb in range(DFF // FFB):
            gate = _dot(h2b, wfi_ref[:, kb * FFB:(kb + 1) * FFB])
            up = _dot(h2b, wfi_ref[:, DFF + kb * FFB:DFF + (kb + 1) * FFB])
            sg = _sigmoid(gate)
            actb = (gate * sg * up).astype(BF16)
            act_ref[:, kb * FFB:(kb + 1) * FFB] = actb
            ff = ff + _dot(actb, wfo_ref[kb * FFB:(kb + 1) * FFB, :])
            saved.append((gate, up, sg))
        x2 = x1 + g2v * ff
        r3 = _rms(x2)
        xh3 = x2 * r3
        err = xh3 * fw_ref[...] - t_ref[...]
        acc(6, (0.5 * inv_d) * err * err)
        dy = err * inv_d
        acc(4, dy * xh3)
        dx2 = _rms_bwd(xh3, r3, dy * fw_ref[...])
        acc(0, dx2 * ff)
        dffb = (dx2 * g2v).astype(BF16)
        dff_ref[...] = dffb
        dh2 = jnp.zeros((tm, D), F32)
        for kb in range(DFF // FFB):
            gate, up, sg = saved[kb]
            da = _dot(dffb, wfo_ref[kb * FFB:(kb + 1) * FFB, :], NT)
            dgate = (da * up * (sg * (1.0 + gate * (1.0 - sg)))).astype(BF16)
            dup = (da * gate * sg).astype(BF16)
            dgu_ref[:, kb * FFB:(kb + 1) * FFB] = dgate
            dgu_ref[:, DFF + kb * FFB:DFF + (kb + 1) * FFB] = dup
            dh2 = dh2 + _dot(dgate, wfi_ref[:, kb * FFB:(kb + 1) * FFB], NT)
            dh2 = dh2 + _dot(dup, wfi_ref[:, DFF + kb * FFB:DFF + (kb + 1) * FFB], NT)
        acc(2, dh2)
        acc(1, dh2 * n2)
        dn2 = dh2 * osc2
        acc(3, dn2 * xh2)
        dx1 = dx2 + _rms_bwd(xh2, r2, dn2 * nw2_ref[...])
        acc(5, dx1 * mix)
        dmixb = (dx1 * g1v).astype(BF16)
        dmix_ref[...] = dmixb
        dy_ref[...] = _dot(dmixb, wo_ref[...], NT)
        dx1_ref[...] = dx1

    row = lambda i: (i, 0)
    vec = _full((1, D))
    return pl.pallas_call(
        body, grid=(T // tm,),
        in_specs=[pl.BlockSpec((tm, D), row), pl.BlockSpec((tm, D), row), pl.BlockSpec((tm, D), row),
                  vec, vec, vec, vec, vec, vec,
                  _resident((D, D)), _resident((D, 2 * DFF)), _resident((DFF, D))],
        out_specs=[pl.BlockSpec((tm, D), row), pl.BlockSpec((tm, D), row), pl.BlockSpec((tm, D), row),
                   pl.BlockSpec((tm, DFF), row), pl.BlockSpec((tm, D), row), pl.BlockSpec((tm, 2 * DFF), row),
                   pl.BlockSpec((tm, D), row), _full((8, D))],
        out_shape=[SDS((T, D), F32), SDS((T, D), F32), SDS((T, D), BF16), SDS((T, DFF), BF16), SDS((T, D), BF16),
                   SDS((T, 2 * DFF), BF16), SDS((T, D), BF16), SDS((8, D), F32)],
        compiler_params=_arb(), name="token_local")(x, ycat, tgt, g1, nw2, sc2, sh2, g2, fw, w_out_b, w_fi_b, w_fo_b)


def _gmlp_bwd(proj, dycat, ws_b, bst, lnw, lnb):
    T = proj.shape[0]
    nb = T // BLK

    def body(u_ref, v_ref, dy_ref, ws_ref, bst_ref, lnw_ref, lnb_ref, dp_ref, dws_ref, dbs_ref, dln_ref, dbs_acc):
        i = pl.program_id(0)

        @pl.when(i == 0)
        def _():
            dws_ref[...] = jnp.zeros_like(dws_ref)
            dln_ref[...] = jnp.zeros_like(dln_ref)
            dbs_acc[...] = jnp.zeros_like(dbs_acc)

        ug, dug, dvg, rstd, vhat, vnb, mixed = _gmlp_common(u_ref[...], v_ref[...], lnw_ref[...], lnb_ref[...], ws_ref, bst_ref)
        dya = dy_ref[...]
        dp_ref[:, 0:DG] = (dya * mixed * dug).astype(BF16)
        dmixed = dya * ug
        dbs_acc[...] += dmixed
        dmb = dmixed.astype(BF16)
        r = lax.broadcasted_iota(jnp.int32, (BLK, BLK), 0) // CH
        c = lax.broadcasted_iota(jnp.int32, (BLK, BLK), 1) // CH
        dvn = []
        for h in range(NH):
            sl = slice(h * HD, (h + 1) * HD)
            dws_ref[h] += jnp.where(r >= c, _dot(dmb[:, sl], vnb[:, sl], NT), 0.0)
            dvn.append(_dot(ws_ref[h], dmb[:, sl], TN))
        dvn = jnp.concatenate(dvn, axis=1)
        dln_ref[0:1, :] += jnp.sum(dvn * vhat, axis=0, keepdims=True)
        dln_ref[1:2, :] += jnp.sum(dvn, axis=0, keepdims=True)
        dvh = dvn * lnw_ref[...]
        dvgel = rstd * (dvh - jnp.mean(dvh, axis=-1, keepdims=True) - vhat * jnp.mean(dvh * vhat, axis=-1, keepdims=True))
        dp_ref[:, DG:2 * DG] = (dvgel * dvg).astype(BF16)

        @pl.when(i == nb - 1)
        def _():
            lane = lax.broadcasted_iota(jnp.int32, (BLK, HD), 1)
            out = jnp.zeros((BLK, HD), F32)
            for h in range(NH):
                out = out + jnp.where(lane == h, jnp.sum(dbs_acc[:, h * HD:(h + 1) * HD], axis=-1, keepdims=True), 0.0)
            dbs_ref[...] = out

    return pl.pallas_call(
        body, grid=(nb,),
        in_specs=[pl.BlockSpec((BLK, DG), lambda i: (i, 0)), pl.BlockSpec((BLK, DG), lambda i: (i, 1)),
                  pl.BlockSpec((BLK, DG), lambda i: (i, 0)),
                  _full((NH, BLK, BLK)), _full((BLK, NH)), _full((1, DG)), _full((1, DG))],
        out_specs=[pl.BlockSpec((BLK, 2 * DG), lambda i: (i, 2)), _full((NH, BLK, BLK)), _full((BLK, HD)), _full((8, DG))],
        out_shape=[SDS((T, DIN), BF16), SDS((NH, BLK, BLK), F32), SDS((BLK, HD), F32), SDS((8, DG), F32)],
        scratch_shapes=[pltpu.VMEM((BLK, DG), F32)],
        compiler_params=_arb(), name="gmlp_bwd")(proj, proj, dycat, ws_b, bst, lnw, lnb)


def _hgrn_bwd(proj, o_pre, a_all, st_all, dycat, lower_bounds, gn_w, dproj, tables, sums):
    T = proj.shape[0]
    nc = T // CH
    w_st, w_st_t, masks, masks_t = tables
    n_lev = len(LEVELS)
    nw = len(sums)

    def body(*refs):
        q_ref, f_ref, i_ref, g_ref, o_ref, a_ref, st_ref, dy_ref, lbp_ref, gn_ref, w_ref, wt_ref, m_ref, mt_ref = refs[:14]
        dp_ref, dlb_ref, dgn_ref = refs[15 + nw:18 + nw]
        ds_scr, dx_scr, send_sems, recv_sems = refs[18 + 2 * nw:]
        exchange = _ChipExchange(refs[15:15 + nw], refs[18 + nw:18 + 2 * nw], send_sems, recv_sems)
        i = pl.program_id(0)

        @pl.when(i == 0)
        def _():
            exchange.start()
            ds_scr[...] = jnp.zeros_like(ds_scr)
            dlb_ref[...] = jnp.zeros_like(dlb_ref)
            dgn_ref[...] = jnp.zeros_like(dgn_ref)

        lb, omlb = _lower_bound(lbp_ref)
        q = q_ref[...]
        v = i_ref[...]
        g = g_ref[...]
        sq, qf, sig, f, k, e = _hgrn_gates(q, f_ref[...], lb, omlb, w_ref)
        eb = e[0:CH]
        ebl = eb[CH - 1:CH, :]
        ekd = e[CH:2 * CH]
        kd = k * ekd
        row = lax.broadcasted_iota(jnp.int32, (CH, 1), 0)
        eye = lax.broadcasted_iota(jnp.int32, (CH, CH), 0) == lax.broadcasted_iota(jnp.int32, (CH, CH), 1)
        dgn = jnp.zeros((1, HD), F32)
        dqf_h, dk_h, dv_h, dg_h = [], [], [], []
        for h in range(NH):
            sl = slice(h * HD, (h + 1) * HD)
            o = o_ref[:, sl]
            ro = _rms(o)
            oh = o * ro
            gh = g[:, sl]
            sg = _sigmoid(gh)
            dyb = dy_ref[:, sl]
            dgate = dyb * (oh * gn_ref[...])
            dg = dgate * (sg * (1.0 + gh * (1.0 - sg)))
            don = dyb * (gh * sg)
            dgn = dgn + jnp.sum(don * oh, axis=0, keepdims=True)
            do = _rms_bwd(oh, ro, don * gn_ref[...])
            dob = do.astype(BF16)
            vb = v[:, sl].astype(BF16)
            st0 = st_ref[0, h]
            dst1 = ds_scr[h]
            dst1b = dst1.astype(BF16)
            qh, kh, ebh, kdh = qf[:, sl], k[:, sl], eb[:, sl], kd[:, sl]
            qe = qh * ebh
            dqe = _dot(dob, st0.astype(BF16))
            ds_scr[h] = dst1 * ebl[:, sl] + _dot(dob, qe.astype(BF16), TN)
            dbl = ebl[:, sl] * jnp.sum(st0 * dst1, axis=0, keepdims=True)
            dkd = _dot(vb, dst1b)
            dv_h.append(_dot(a_ref[0, h].astype(BF16), dob, TN) + _dot(kdh.astype(BF16), dst1b, NT))
            da = _dot(dob, vb, NT)
            dat = _dot(vb, dob, NT)
            ddiag = jnp.sum(jnp.where(eye, da, 0.0), axis=-1, keepdims=True)
            dqi = ddiag * kh
            dki = ddiag * qh
            for li in range(n_lev):
                el, up, y = _level_factor(e, li, sl, row, qh, kh)
                dgs = m_ref[li] * da + mt_ref[li] * dat
                dyv = _dot(dgs.astype(BF16), y.astype(BF16))
                dx_scr[(2 + li) * CH:(3 + li) * CH, sl] = dyv * y
                dye = dyv * el
                dqi = dqi + jnp.where(up, dye, 0.0)
                dki = dki + jnp.where(up, 0.0, dye)
            dx_scr[0:CH, sl] = dqe * qe + jnp.where(row == CH - 1, dbl, 0.0)
            dx_scr[CH:2 * CH, sl] = dkd * kdh
            dqf_h.append(dqe * ebh + dqi)
            dk_h.append(dkd * ekd[:, sl] + dki)
            dg_h.append(dg)
        dgn_ref[0:1, :] += dgn
        dlf = _split_dot(wt_ref[...], dx_scr[...], 2)
        df = dlf / f - jnp.concatenate(dk_h, axis=1)
        dlb_ref[0:1, :] += jnp.sum(df * (1.0 - sig), axis=0, keepdims=True)
        dp_ref[:, 0:DH] = (jnp.concatenate(dqf_h, axis=1) * (sq * (1.0 + q * (1.0 - sq)))).astype(BF16)
        dp_ref[:, DH:2 * DH] = (df * omlb * sig * (1.0 - sig)).astype(BF16)
        dp_ref[:, 2 * DH:3 * DH] = jnp.concatenate(dv_h, axis=1).astype(BF16)
        dp_ref[:, 3 * DH:4 * DH] = jnp.concatenate(dg_h, axis=1).astype(BF16)

        @pl.when(i == nc - 1)
        def _():
            gl = dlb_ref[0:1, :] * lb * omlb
            dlb_ref[0:1, :] = gl
            dlb_ref[1:2, :] = -gl
            exchange.finish()

    rev = lambda j: pl.BlockSpec((CH, DH), lambda c: (nc - 1 - c, j))
    anyspec = pl.BlockSpec(memory_space=pl.ANY)
    res = pl.pallas_call(
        body, grid=(nc,),
        in_specs=[rev(2), rev(3), rev(4), rev(5), rev(0),
                  pl.BlockSpec((1, NH, CH, CH), lambda c: (nc - 1 - c, 0, 0, 0)),
                  pl.BlockSpec((1, NH, HD, HD), lambda c: (nc - 1 - c, 0, 0, 0)),
                  rev(1), _full((2, DH)), _full((1, HD)),
                  _full(w_st.shape), _full(w_st_t.shape), _full(masks.shape), _full(masks_t.shape),
                  anyspec] + [anyspec] * nw,
        out_specs=[pl.BlockSpec((CH, 4 * DH), lambda c: (nc - 1 - c, 0)), _full((8, DH)), _full((8, HD))] + [anyspec] * nw,
        out_shape=[SDS((T, DIN), BF16), SDS((8, DH), F32), SDS((8, HD), F32)] + _slot_shapes(sums),
        scratch_shapes=[pltpu.VMEM((NH, HD, HD), F32), pltpu.VMEM(((2 + n_lev) * CH, DH), F32)] + _exchange_sems(nw),
        input_output_aliases={14: 0},
        compiler_params=_arb(), name="hgrn_bwd")(proj, proj, proj, proj, o_pre, a_all, st_all, dycat, lower_bounds, gn_w,
                                                 w_st, w_st_t, masks, masks_t, dproj, *sums)
    return res[:3], res[3:]


def _proj_in_bwd(dproj, x, dx1, nw, sc, w_in_b, tm):
    T = x.shape[0]

    def body(dp_ref, x_ref, dx1_ref, nw_ref, sc_ref, w_ref, gx_ref, acc_ref):
        @pl.when(pl.program_id(0) == 0)
        def _():
            acc_ref[...] = jnp.zeros_like(acc_ref)

        dh = _dot(dp_ref[:, 0:4 * DH], w_ref[:, 2 * DG:DIN], NT) + _dot(dp_ref[:, 4 * DH:DIN], w_ref[:, 0:2 * DG], NT)
        xv = x_ref[...]
        r = _rms(xv)
        xh = xv * r
        n1 = xh * nw_ref[...]
        acc_ref[0:1, :] += jnp.sum(dh, axis=0, keepdims=True)
        acc_ref[1:2, :] += jnp.sum(dh * n1, axis=0, keepdims=True)
        dn = dh * (1.0 + sc_ref[...])
        acc_ref[2:3, :] += jnp.sum(dn * xh, axis=0, keepdims=True)
        gx_ref[...] = dx1_ref[...] + _rms_bwd(xh, r, dn * nw_ref[...])

    row = lambda i: (i, 0)
    return pl.pallas_call(
        body, grid=(T // tm,),
        in_specs=[pl.BlockSpec((tm, DIN), row), pl.BlockSpec((tm, D), row), pl.BlockSpec((tm, D), row),
                  _full((1, D)), _full((1, D)), _resident((D, DIN))],
        out_specs=[pl.BlockSpec((tm, D), row), _full((8, D))],
        out_shape=[SDS((T, D), F32), SDS((8, D), F32)],
        compiler_params=_arb(), name="proj_in_bwd")(dproj, x, dx1, nw, sc, w_in_b)


def _wgrad(a, b, bk, bn, tt, name):
    T, K = a.shape
    N = b.shape[1]
    nn, nk, nt = N // bn, K // bk, T // tt
    bmap = lambda n, k, t: (t, n)

    def body(a_ref, b_ref, o_ref):
        @pl.when(pl.program_id(2) == 0)
        def _():
            o_ref[...] = jnp.zeros_like(o_ref)

        o_ref[0] += _dot(a_ref[...], b_ref[...], TN)

    return pl.pallas_call(
        body, grid=(nn, nk, nt),
        in_specs=[pl.BlockSpec((tt, bk), lambda n, k, t: (t, k)), pl.BlockSpec((tt, bn), bmap)],
        out_specs=pl.BlockSpec((1, bk, bn), lambda n, k, t: (n, k, 0)),
        out_shape=SDS((nn, K, bn), F32),
        compiler_params=_arb(3), name=name)(a, b)


def _adam_math(w, g, m, v):
    m = B1 * m + (1.0 - B1) * g
    v = B2 * v + (1.0 - B2) * (g * g)
    m_hat = m / (1.0 - B1 ** STEP)
    v_hat = v / (1.0 - B2 ** STEP)
    return -LR * (m_hat / (jnp.sqrt(v_hat) + AEPS) + WD * w), m, v


def _adamw_halves(w, mine, sibling, m, v, c_idx, rb, name):
    R, C = w.shape
    nb = (R // 2) // rb

    def body(c_ref, w_ref, a_ref, b_ref, m_ref, v_ref, g_out, d_out, m_out, v_out):
        g = jnp.where(pl.program_id(0) == c_ref[0], a_ref[...], b_ref[...])
        g_out[...] = g
        d_out[...], m_out[...], v_out[...] = _adam_math(w_ref[...], g, m_ref[...], v_ref[...])

    whole = pl.BlockSpec((rb, C), lambda hh, i, cr: (hh * nb + i, 0))
    half = pl.BlockSpec((rb, C), lambda hh, i, cr: (i, 0))
    return pl.pallas_call(
        body,
        grid_spec=pltpu.PrefetchScalarGridSpec(
            num_scalar_prefetch=1, grid=(2, nb), in_specs=[whole, half, half, whole, whole], out_specs=[whole] * 4),
        out_shape=[SDS((R, C), F32)] * 4, compiler_params=_arb(2), name=name)(c_idx, w, mine, sibling, m, v)


def _ada_forward(c_all, w_ada):
    n = w_ada.shape[1]

    def body(c_ref, w_ref, ca_ref, p_ref):
        cv = c_ref[...]
        ca = cv * _sigmoid(cv)
        ca_ref[...] = ca
        p_ref[...] = _dot(ca, w_ref[...], precision=HIGHEST)

    return pl.pallas_call(
        body, grid=(n // 512,),
        in_specs=[_full((N_DEV, D)), pl.BlockSpec((D, 512), lambda i: (0, i))],
        out_specs=[_full((N_DEV, D)), pl.BlockSpec((N_DEV, 512), lambda i: (0, i))],
        out_shape=[SDS((N_DEV, D), F32), SDS((N_DEV, n), F32)],
        compiler_params=_arb(), name="ada_forward")(c_all, w_ada)


def _ada_wgrad_adam(cact_t, dada, w, m, v):
    R, C = w.shape
    rb = 256

    def body(c_ref, d_ref, w_ref, m_ref, v_ref, g_out, d_out, m_out, v_out):
        g = _dot(c_ref[...], d_ref[...], precision=HIGHEST)
        g_out[...] = g
        d_out[...], m_out[...], v_out[...] = _adam_math(w_ref[...], g, m_ref[...], v_ref[...])

    spec = pl.BlockSpec((rb, C), lambda i: (i, 0))
    return pl.pallas_call(
        body, grid=(R // rb,),
        in_specs=[pl.BlockSpec((rb, N_DEV), lambda i: (i, 0)), _full((N_DEV, C)), spec, spec, spec],
        out_specs=[spec] * 4, out_shape=[SDS((R, C), F32)] * 4,
        compiler_params=_arb(), name="ada_wgrad_adam")(cact_t, dada, w, m, v)


def _small_finalize(gathered, w, m, v):
    def body(ga_ref, w_ref, m_ref, v_ref, g_out, d_out, m_out, v_out):
        g = ga_ref[0:SMALL_ROWS, :]
        for dev in range(1, N_DEV):
            g = g + ga_ref[dev * SMALL_ROWS:(dev + 1) * SMALL_ROWS, :]
        g_out[...] = g
        d_out[...], m_out[...], v_out[...] = _adam_math(w_ref[...], g, m_ref[...], v_ref[...])

    return pl.pallas_call(
        body, out_shape=[SDS((SMALL_ROWS, D), F32)] * 4, name="small_finalize")(gathered, w, m, v)


def _position():
    x, y, c = lax.axis_index("x"), lax.axis_index("y"), lax.axis_index("c")
    return x, y, c


def _chip_at(x, y, r):
    return (x ^ (r >> 1), y ^ (r & 1))


def _all_gather_rows(block, name):
    m_per, n = block.shape

    def body(x_ref, out_ref, send_sems, recv_sems, local_sem):
        x, y, c = _position()
        me, sibling = (x, y, c), (x, y, 1 - c)
        chips = [_chip_at(x, y, r) for r in (1, 2, 3)]

        def rows(px, py, pc):
            return out_ref.at[pl.ds((4 * px + 2 * py + pc) * m_per, m_per), :]

        def copy(k, blk, to, src=None):
            return pltpu.make_async_remote_copy(
                src_ref=rows(*blk) if src is None else src, dst_ref=rows(*blk),
                send_sem=send_sems.at[k], recv_sem=recv_sems.at[k], device_id=to, device_id_type=MESH)

        mine = pltpu.make_async_copy(x_ref, rows(*me), local_sem)
        mine.start()
        first = [copy(0, me, sibling, src=x_ref)]
        first += [copy(1 + j, me, (*chip, c), src=x_ref) for j, chip in enumerate(chips)]
        for cp in first:
            cp.start()
        passed = [copy(4 + j, (*chip, c), sibling) for j, chip in enumerate(chips)]
        for j, chip in enumerate(chips):
            copy(1 + j, (*chip, c), me).wait_recv()
            passed[j].start()
        copy(0, sibling, me).wait_recv()
        for j, chip in enumerate(chips):
            copy(4 + j, (*chip, 1 - c), me).wait_recv()
        for cp in first + passed:
            cp.wait_send()
        mine.wait()

    return pl.pallas_call(
        body, out_shape=SDS((N_DEV * m_per, n), block.dtype),
        in_specs=[pl.BlockSpec(memory_space=pltpu.VMEM)], out_specs=pl.BlockSpec(memory_space=pltpu.VMEM),
        scratch_shapes=[pltpu.SemaphoreType.DMA((7,)), pltpu.SemaphoreType.DMA((7,)), pltpu.SemaphoreType.DMA],
        name=name)(block)


def _place_shard(w_shard, axis, chip_idx, name):
    R, C = w_shard.shape
    rb = _row_block(R)
    nb = R // rb
    full = (R * N_CHIPS, C) if axis == 0 else (R, C * N_CHIPS)
    omap = (lambda i, j: (j[0] * nb + i, 0)) if axis == 0 else (lambda i, j: (i, j[0]))

    def body(j_ref, w_ref, o_ref):
        o_ref[...] = w_ref[...].astype(BF16)

    return pl.pallas_call(
        body,
        grid_spec=pltpu.PrefetchScalarGridSpec(
            num_scalar_prefetch=1, grid=(nb,), in_specs=[pl.BlockSpec((rb, C), lambda i, j: (i, 0))],
            out_specs=pl.BlockSpec((rb, C), omap)),
        out_shape=SDS(full, BF16), compiler_params=_arb(), name=name)(chip_idx, w_shard)


class _WeightGather:
    def __init__(self, refs, axes, send_sems, recv_sems):
        self.refs, self.axes, self.send_sems, self.recv_sems = refs, axes, send_sems, recv_sems
        self.x, self.y, self.c = _position()
        self.j = 2 * self.x + self.y
        self.n = 3 * len(refs)

    def _half(self, w, chip_idx, half):
        ref, axis = self.refs[w], self.axes[w]
        if axis == 0:
            size = ref.shape[0] // N_CHIPS
            return ref.at[pl.ds(chip_idx * size + half * (size // 2), size // 2), :]
        size = ref.shape[1] // N_CHIPS
        rows = ref.shape[0] // 2
        return ref.at[pl.ds(half * rows, rows), pl.ds(chip_idx * size, size)]

    def _ici(self, w, r, chip_idx):
        k = 3 * w + r - 1
        piece = self._half(w, chip_idx, self.c)
        return pltpu.make_async_remote_copy(
            src_ref=piece, dst_ref=piece, send_sem=self.send_sems.at[k], recv_sem=self.recv_sems.at[k],
            device_id=(*_chip_at(self.x, self.y, r), self.c), device_id_type=MESH)

    def _d2d(self, w, r, half):
        k = self.n + 3 * w + r - 1
        piece = self._half(w, self.j ^ r, half)
        return pltpu.make_async_remote_copy(
            src_ref=piece, dst_ref=piece, send_sem=self.send_sems.at[k], recv_sem=self.recv_sems.at[k],
            device_id=(self.x, self.y, 1 - self.c), device_id_type=MESH)

    def _each(self):
        return [(w, r) for w in range(len(self.refs)) for r in (1, 2, 3)]

    def start(self):
        for w, r in self._each():
            self._ici(w, r, self.j).start()

    def forward(self):
        for w, r in self._each():
            self._ici(w, r, self.j ^ r).wait_recv()
            self._d2d(w, r, self.c).start()

    def finish(self):
        for w, r in self._each():
            self._ici(w, r, self.j).wait_send()
            self._d2d(w, r, self.c).wait_send()
            self._d2d(w, r, 1 - self.c).wait_recv()


def _gather_sems(n_weights):
    return [pltpu.SemaphoreType.DMA((6 * n_weights,)), pltpu.SemaphoreType.DMA((6 * n_weights,))]


def _gather_weights(placed, axes, name):
    nw = len(placed)

    def body(*refs):
        outs = refs[nw:2 * nw]
        g = _WeightGather(outs, axes, *refs[2 * nw:])
        g.start()
        g.forward()
        g.finish()

    anyspec = pl.BlockSpec(memory_space=pl.ANY)
    return pl.pallas_call(
        body, out_shape=[SDS(a.shape, a.dtype) for a in placed], in_specs=[anyspec] * nw, out_specs=[anyspec] * nw,
        scratch_shapes=_gather_sems(nw), input_output_aliases={i: i for i in range(nw)},
        name=name)(*placed)


class _ChipExchange:
    def __init__(self, ins, outs, send_sems, recv_sems):
        self.ins, self.outs, self.send_sems, self.recv_sems = ins, outs, send_sems, recv_sems
        self.x, self.y, self.c = _position()
        self.j = 2 * self.x + self.y

    def _copies(self):
        for w in range(len(self.ins)):
            for r in (1, 2, 3):
                k = 3 * w + r - 1
                yield pltpu.make_async_remote_copy(
                    src_ref=self.ins[w].at[self.j ^ r], dst_ref=self.outs[w].at[r - 1],
                    send_sem=self.send_sems.at[k], recv_sem=self.recv_sems.at[k],
                    device_id=(*_chip_at(self.x, self.y, r), self.c), device_id_type=MESH)

    def start(self):
        for cp in self._copies():
            cp.start()

    def finish(self):
        for cp in self._copies():
            cp.wait()


def _exchange_sems(n_weights):
    return [pltpu.SemaphoreType.DMA((3 * n_weights,)), pltpu.SemaphoreType.DMA((3 * n_weights,))]


def _exchange_core_halves(grads, name):
    nw = len(grads)

    def body(*refs):
        ins, outs = refs[:nw], refs[nw:2 * nw]
        send_sems, recv_sems = refs[2 * nw:]
        x, y, c = _position()
        cps = []
        for w in range(nw):
            cp = pltpu.make_async_remote_copy(
                src_ref=ins[w].at[:, 1 - c], dst_ref=outs[w], send_sem=send_sems.at[w], recv_sem=recv_sems.at[w],
                device_id=(x, y, 1 - c), device_id_type=MESH)
            cp.start()
            cps.append(cp)
        for cp in cps:
            cp.wait()

    anyspec = pl.BlockSpec(memory_space=pl.ANY)
    return pl.pallas_call(
        body, out_shape=[SDS((g.shape[0], g.shape[2], g.shape[3]), F32) for g in grads],
        in_specs=[anyspec] * nw, out_specs=[anyspec] * nw,
        scratch_shapes=[pltpu.SemaphoreType.DMA((nw,)), pltpu.SemaphoreType.DMA((nw,))],
        name=name)(*grads)


def _add_core_halves(g4, recv, c_idx, rb, name):
    ns, _, rh, C = g4.shape

    def body(c_ref, g_ref, r_ref, o_ref):
        o_ref[...] = (g_ref[0] + r_ref[...]).astype(BF16)

    return pl.pallas_call(
        body,
        grid_spec=pltpu.PrefetchScalarGridSpec(
            num_scalar_prefetch=1, grid=(ns, rh // rb),
            in_specs=[pl.BlockSpec((1, 1, rb, C), lambda s, i, cr: (s, cr[0], i, 0)),
                      pl.BlockSpec((1, rb, C), lambda s, i, cr: (s, i, 0))],
            out_specs=pl.BlockSpec((1, rb, C), lambda s, i, cr: (s, i, 0))),
        out_shape=SDS((ns, rh, C), BF16), compiler_params=_arb(2), name=name)(c_idx, g4, recv)


def _slot_shapes(sums):
    return [SDS((3,) + s.shape[1:], s.dtype) for s in sums]


def _exchange_chips(sums, name):
    nw = len(sums)

    def body(*refs):
        ex = _ChipExchange(refs[:nw], refs[nw:2 * nw], *refs[2 * nw:])
        ex.start()
        ex.finish()

    anyspec = pl.BlockSpec(memory_space=pl.ANY)
    return pl.pallas_call(
        body, out_shape=_slot_shapes(sums), in_specs=[anyspec] * nw, out_specs=[anyspec] * nw,
        scratch_shapes=_exchange_sems(nw), name=name)(*sums)


def _add_chips(own, slots, order, rb, name):
    _, rh, C = slots.shape

    def body(o_ref, own_ref, a_ref, b_ref, c_ref, d_ref, out_ref):
        mine = own_ref[0].astype(F32)
        t = [jnp.where(o_ref[i] == 0, mine, r[0].astype(F32)) for i, r in enumerate((a_ref, b_ref, c_ref, d_ref))]
        out_ref[...] = ((t[0] + t[1]) + t[2]) + t[3]

    def spec(i):
        return pl.BlockSpec((1, rb, C), lambda t, o: (jnp.maximum(o[i], 1) - 1, t, 0))

    return pl.pallas_call(
        body,
        grid_spec=pltpu.PrefetchScalarGridSpec(
            num_scalar_prefetch=1, grid=(rh // rb,),
            in_specs=[pl.BlockSpec((1, rb, C), lambda t, o: (o[4], t, 0)), spec(0), spec(1), spec(2), spec(3)],
            out_specs=pl.BlockSpec((rb, C), lambda t, o: (t, 0))),
        out_shape=SDS((rh, C), F32), compiler_params=_arb(), name=name)(order, own, slots, slots, slots, slots)


def _share_halves(halves):
    nw = len(halves)

    def body(*refs):
        ins, outs = refs[:nw], refs[nw:2 * nw]
        send_sems, recv_sems = refs[2 * nw:]
        x, y, c = _position()
        started = []
        for w in range(nw):
            cp = pltpu.make_async_remote_copy(
                src_ref=ins[w], dst_ref=outs[w], send_sem=send_sems.at[w], recv_sem=recv_sems.at[w],
                device_id=(x, y, 1 - c), device_id_type=MESH)
            cp.start()
            started.append(cp)
        for cp in started:
            cp.wait()

    anyspec = pl.BlockSpec(memory_space=pl.ANY)
    return pl.pallas_call(
        body, out_shape=[SDS(h.shape, F32) for h in halves], in_specs=[anyspec] * nw, out_specs=[anyspec] * nw,
        scratch_shapes=[pltpu.SemaphoreType.DMA((nw,)), pltpu.SemaphoreType.DMA((nw,))],
        name="share_halves")(*halves)


def _pack_small(b_ada, norm1_w, norm2_w, final_norm_w, v_ln_w, v_ln_b, lower_bounds, b_s, gn_w, w_s):
    parts = [b_ada, norm1_w, norm2_w, final_norm_w, v_ln_w, v_ln_b, lower_bounds, b_s, gn_w,
             jnp.zeros((D - NH * BLK - HD,), F32), w_s, jnp.zeros(((SMALL_ROWS - 76) * D,), F32)]
    return jnp.concatenate([p.reshape(-1) for p in parts]).reshape(SMALL_ROWS, D)


def _unpack_small(p):
    return dict(
        b_ada=p[0:6].reshape(1, 6 * D), norm1_w=p[6:7], norm2_w=p[7:8], final_norm_w=p[8],
        v_ln_w=p[9:10, 0:DG], v_ln_b=p[9:10, DG:D], lower_bounds=p[10].reshape(2, DH),
        b_s=p[11, 0:NH * BLK].reshape(1, NH, BLK), gn_w=p[11:12, NH * BLK:NH * BLK + HD],
        w_s=p[12:76].reshape(1, NH, BLK, BLK))


def _row_block(r):
    for cand in (256, 176, 128, 64, 32, 16, 8):
        if r % cand == 0:
            return cand
    return r


def kernel(x, c, w_ada, b_ada, norm1_w, w_in, w_s, b_s, v_ln_w, v_ln_b, lower_bounds, gn_w, w_out, norm2_w, w_ffn_in, w_ffn_out, final_norm_w, loss_target, m_w_ada, m_b_ada, m_norm1_w, m_w_in, m_w_s, m_b_s, m_v_ln_w, m_v_ln_b, m_lower_bounds, m_gn_w, m_w_out, m_norm2_w, m_w_ffn_in, m_w_ffn_out, m_final_norm_w, v_w_ada, v_b_ada, v_norm1_w, v_w_in, v_w_s, v_b_s, v_v_ln_w, v_v_ln_b, v_lower_bounds, v_gn_w, v_w_out, v_norm2_w, v_w_ffn_in, v_w_ffn_out, v_final_norm_w):
    T = x.shape[1]
    tm = min(256, T)
    px, py, pc = _position()
    chip = 2 * px + py
    me = 4 * px + 2 * py + pc
    x2d = x.reshape(T, D)
    tgt = loss_target.reshape(T, D)

    c_all = _all_gather_rows(jnp.broadcast_to(c, (8, D)), "gather_c").reshape(N_DEV, 8, D)[:, 0, :]
    cact, ada_part = _ada_forward(c_all, w_ada[0])
    n_ada = ada_part.shape[1]
    ada_all = _all_gather_rows(ada_part, "gather_ada").reshape(N_CHIPS, 2, N_DEV, n_ada)[:, 0]
    ada = lax.dynamic_index_in_dim(ada_all, me, axis=1, keepdims=False).reshape(1, 6 * D) + b_ada
    sh1, sc1, g1, sh2, sc2, g2 = [ada[:, i * D:(i + 1) * D] for i in range(6)]

    chip_idx = jnp.reshape(chip, (1,)).astype(jnp.int32)
    c_idx = jnp.reshape(pc, (1,)).astype(jnp.int32)
    (w_in_b,) = _gather_weights([_place_shard(w_in[0], 1, chip_idx, "place_in")], [1], "gather_w_in")
    placed = [_place_shard(w_out[0], 0, chip_idx, "place_out"), _place_shard(w_ffn_in[0], 1, chip_idx, "place_ffn_in"),
              _place_shard(w_ffn_out[0], 0, chip_idx, "place_ffn_out")]

    rr = lax.broadcasted_iota(jnp.int32, (BLK, BLK), 0) // CH
    cc = lax.broadcasted_iota(jnp.int32, (BLK, BLK), 1) // CH
    ws_b = jnp.where((rr >= cc)[None], w_s[0], 0.0).astype(BF16)
    bst = b_s[0].T
    lnw, lnb = v_ln_w, v_ln_b
    nw1, nw2, fw = norm1_w, norm2_w, final_norm_w.reshape(1, D)

    h1, proj = _proj_in(x2d, nw1, sc1, sh1, w_in_b, tm)
    ycat = _gmlp_fwd(proj, ws_b, bst, lnw, lnb)
    tables = _hgrn_tables()
    (ycat, o_pre, a_all, st_all), (w_out_b, w_fi_b, w_fo_b) = _hgrn_fwd(
        proj, lower_bounds, gn_w, ycat, tables, placed, [0, 1, 0])

    dycat, dx1, h2, act, dff, dgu, dmix, acc2 = _token_local(
        x2d, ycat, tgt, g1, nw2, sc2, sh2, g2, fw, w_out_b, w_fi_b, w_fo_b, tm)

    tt = min(512, T)
    order = jnp.concatenate([chip ^ jnp.arange(N_CHIPS, dtype=jnp.int32), chip_idx]).astype(jnp.int32)

    def core_pair_sums(grads, names, tag):
        g4 = [g.reshape(N_CHIPS, 2, g.shape[1] // 2, g.shape[2]) for g in grads]
        recv = _exchange_core_halves(g4, "exchange_core_halves_" + tag)
        return [_add_core_halves(a, b, c_idx, _row_block(a.shape[2]), "add_core_" + n) for a, b, n in zip(g4, recv, names)]

    def chip_sums(sums, slots, names):
        return [_add_chips(o, s, order, _row_block(s.shape[1]), "add_chips_" + n) for o, s, n in zip(sums, slots, names)]

    g_out = _wgrad(ycat, dmix, D, D, tt, "wgrad_out").reshape(N_CHIPS, D // N_CHIPS, D)
    g_fi = _wgrad(h2, dgu, D, FFB, tt, "wgrad_ffn_in")
    g_fo = _wgrad(act, dff, FFB, D, tt, "wgrad_ffn_out").reshape(N_CHIPS, DFF // N_CHIPS, D)
    late_names = ["out", "ffn_in", "ffn_out"]
    late_sums = core_pair_sums([g_out, g_fi, g_fo], late_names, "ffn")

    dproj, dws, dbs, dln = _gmlp_bwd(proj, dycat, ws_b, bst, lnw, lnb)
    (dproj, dlb, dgn), late_slots = _hgrn_bwd(
        proj, o_pre, a_all, st_all, dycat, lower_bounds, gn_w, dproj, tables, late_sums)
    grad_x, acc1 = _proj_in_bwd(dproj, x2d, dx1, nw1, sc1, w_in_b, tm)

    g_in = _wgrad(h1, dproj, D, D, tt, "wgrad_in")
    g_in = jnp.concatenate([g_in[2], g_in[0], g_in[1]], axis=1).reshape(D, N_CHIPS, DIN // N_CHIPS).transpose(1, 0, 2)
    in_sums = core_pair_sums([g_in], ["in"], "in")
    in_slots = _exchange_chips(in_sums, "exchange_chips_in")
    names = ["in"] + late_names
    halves = chip_sums(in_sums, in_slots, ["in"]) + chip_sums(late_sums, late_slots, late_names)
    sibling_halves = _share_halves(halves)

    big_w = [(w_in, m_w_in, v_w_in), (w_out, m_w_out, v_w_out), (w_ffn_in, m_w_ffn_in, v_w_ffn_in),
             (w_ffn_out, m_w_ffn_out, v_w_ffn_out)]
    big_out = []
    for mine, sib, (w, m, v), n in zip(halves, sibling_halves, big_w, names):
        res = _adamw_halves(w[0], mine, sib, m[0], v[0], c_idx, _row_block(mine.shape[0]), "adamw_" + n)
        big_out.append([r[None] for r in res])

    d_ada = jnp.stack([acc1[0], acc1[1], acc2[5], acc2[2], acc2[1], acc2[0]]).reshape(1, 6 * D)
    small_g = _pack_small(d_ada, acc1[2], acc2[3], acc2[4], dln[0], dln[1], dlb[0:2], dbs[:, 0:NH].T, dgn[0], dws)
    gathered = _all_gather_rows(small_g, "gather_small")
    sw = _pack_small(b_ada, norm1_w, norm2_w, final_norm_w, v_ln_w, v_ln_b, lower_bounds, b_s, gn_w, w_s)
    sm = _pack_small(m_b_ada, m_norm1_w, m_norm2_w, m_final_norm_w, m_v_ln_w, m_v_ln_b, m_lower_bounds, m_b_s, m_gn_w, m_w_s)
    sv = _pack_small(v_b_ada, v_norm1_w, v_norm2_w, v_final_norm_w, v_v_ln_w, v_v_ln_b, v_lower_bounds, v_b_s, v_gn_w, v_w_s)
    small = [_unpack_small(p) for p in _small_finalize(gathered, sw, sm, sv)]

    dada_all = gathered.reshape(N_DEV, SMALL_ROWS, D)[:, 0:6, :].reshape(N_DEV, 6 * D)
    dada = lax.dynamic_slice_in_dim(dada_all, chip * n_ada, n_ada, axis=1)
    ada_out = [o[None] for o in _ada_wgrad_adam(cact.T, dada, w_ada[0], m_w_ada[0], v_w_ada[0])]

    loss = lax.psum(jnp.sum(acc2[6]), ("x", "y", "c"))

    order_names = ['w_ada', 'b_ada', 'norm1_w', 'w_in', 'w_s', 'b_s', 'v_ln_w', 'v_ln_b', 'lower_bounds', 'gn_w',
                   'w_out', 'norm2_w', 'w_ffn_in', 'w_ffn_out', 'final_norm_w']
    big_idx = {'w_in': 0, 'w_out': 1, 'w_ffn_in': 2, 'w_ffn_out': 3}
    outs = [loss, grad_x.reshape(1, T, D)]
    for kind in range(4):
        for n in order_names:
            if n == 'w_ada':
                outs.append(ada_out[kind])
            elif n in big_idx:
                outs.append(big_out[big_idx[n]][kind])
            else:
                outs.append(small[kind][n])
    return tuple(outs)
```

```python
import functools

import jax
import jax.numpy as jnp
import numpy as np
from jax import lax
from jax.experimental import pallas as pl
from jax.experimental.pallas import tpu as pltpu

F32 = jnp.float32
BF16 = jnp.bfloat16
SDS = jax.ShapeDtypeStruct
MESH = pl.DeviceIdType.MESH
HIGHEST = lax.Precision.HIGHEST

D = 1024
DG = 512
DH = 512
NH = 4
HD = 128
BLK = 128
CH = 64
DFF = 2816
DIN = 3072
FFB = 1408
LEVELS = (64, 32, 16, 8, 4, 2)
HGRN_CHUNKS_PER_STEP = 4
N_CHIPS = 4
N_DEV = 8
EPS = 1e-6
LR, B1, B2, AEPS, WD, STEP = 0.001, 0.9, 0.999, 1e-08, 0.01, 10
SMALL_ROWS = 80

NT = (((1,), (1,)), ((), ()))
TN = (((0,), (0,)), ((), ()))


def _full(shape):
    nd = len(shape)
    return pl.BlockSpec(shape, lambda *_: (0,) * nd)


def _resident(shape):
    nd = len(shape)
    return pl.BlockSpec(shape, lambda *_: (0,) * nd, pipeline_mode=pl.Buffered(1))


def _arb(n=1):
    return pltpu.CompilerParams(dimension_semantics=("arbitrary",) * n)


def _dot(a, b, dims=None, precision=None):
    if dims is None:
        return jnp.dot(a, b, preferred_element_type=F32, precision=precision)
    return lax.dot_general(a, b, dims, preferred_element_type=F32, precision=precision)


def _sigmoid(x):
    return jax.nn.sigmoid(x)


def _gelu_parts(x):
    cdf = 0.5 * (1.0 + lax.erf(x * 0.7071067811865476))
    pdf = jnp.exp(-0.5 * x * x) * 0.3989422804014327
    return x * cdf, cdf + x * pdf


def _rms(x):
    return lax.rsqrt(jnp.mean(x * x, axis=-1, keepdims=True) + EPS)


def _rms_bwd(xhat, r, gw):
    return r * (gw - xhat * jnp.mean(xhat * gw, axis=-1, keepdims=True))


def _lower_bound(lbp_ref):
    l0, l1 = lbp_ref[0:1, :], lbp_ref[1:2, :]
    m = jnp.maximum(l0, l1)
    e0, e1 = jnp.exp(l0 - m), jnp.exp(l1 - m)
    return e0 / (e0 + e1), e1 / (e0 + e1)


def _proj_in(x, nw, sc, sh, w_in_b, tm):
    T = x.shape[0]

    def body(x_ref, nw_ref, sc_ref, sh_ref, w_ref, h_ref, p_ref):
        xv = x_ref[...]
        h = ((xv * _rms(xv)) * nw_ref[...]) * (1.0 + sc_ref[...]) + sh_ref[...]
        hb = h.astype(BF16)
        h_ref[...] = hb
        p_ref[...] = _dot(hb, w_ref[...])

    row = lambda i: (i, 0)
    return pl.pallas_call(
        body, grid=(T // tm,),
        in_specs=[pl.BlockSpec((tm, D), row), _full((1, D)), _full((1, D)), _full((1, D)), _resident((D, DIN))],
        out_specs=[pl.BlockSpec((tm, D), row), pl.BlockSpec((tm, DIN), row)],
        out_shape=[SDS((T, D), BF16), SDS((T, DIN), F32)],
        compiler_params=_arb(), name="proj_in")(x, nw, sc, sh, w_in_b)


def _gmlp_common(u, v, lnw, lnb, ws_ref, bst_ref):
    ug, dug = _gelu_parts(u)
    vg, dvg = _gelu_parts(v)
    mu = jnp.mean(vg, axis=-1, keepdims=True)
    vc = vg - mu
    rstd = lax.rsqrt(jnp.mean(vc * vc, axis=-1, keepdims=True) + EPS)
    vhat = vc * rstd
    vn = vhat * lnw + lnb
    vnb = vn.astype(BF16)
    mixed = []
    for h in range(NH):
        sl = slice(h * HD, (h + 1) * HD)
        mixed.append(_dot(ws_ref[h], vnb[:, sl]) + bst_ref[:, h:h + 1])
    return ug, dug, dvg, rstd, vhat, vnb, jnp.concatenate(mixed, axis=1)


def _gmlp_fwd(proj, ws_b, bst, lnw, lnb):
    T = proj.shape[0]

    def body(u_ref, v_ref, ws_ref, bst_ref, lnw_ref, lnb_ref, y_ref):
        ug, _, _, _, _, _, mixed = _gmlp_common(u_ref[...], v_ref[...], lnw_ref[...], lnb_ref[...], ws_ref, bst_ref)
        y_ref[...] = (ug * mixed).astype(BF16)

    return pl.pallas_call(
        body, grid=(T // BLK,),
        in_specs=[pl.BlockSpec((BLK, DG), lambda i: (i, 0)), pl.BlockSpec((BLK, DG), lambda i: (i, 1)),
                  _full((NH, BLK, BLK)), _full((BLK, NH)), _full((1, DG)), _full((1, DG))],
        out_specs=pl.BlockSpec((BLK, DG), lambda i: (i, 0)),
        out_shape=SDS((T, D), BF16),
        compiler_params=_arb(), name="gmlp_fwd")(proj, proj, ws_b, bst, lnw, lnb)


def _hgrn_tables():
    t = np.arange(CH)[:, None]
    j = np.arange(CH)[None, :]
    blocks = [j <= t, j > t]
    masks = []
    for n in LEVELS:
        mid = t - t % n + n // 2
        blocks.append(np.where(t >= mid, (j >= mid) & (j <= t), (j > t) & (j < mid)))
        masks.append((t // n == j // n) & (t % n >= n // 2) & (j % n < n // 2))
    w = np.concatenate(blocks, axis=0).astype(np.float32)
    m = np.stack(masks).astype(np.float32)
    return (jnp.asarray(w, BF16), jnp.asarray(w.T, BF16), jnp.asarray(m), jnp.asarray(m.transpose(0, 2, 1)))


def _split_dot(w, x, parts):
    acc = None
    for _ in range(parts):
        piece = x.astype(BF16)
        term = _dot(w, piece)
        acc = term if acc is None else acc + term
        x = x - piece.astype(F32)
    return acc


def _hgrn_gates(q, fl, lb, omlb, w_ref):
    sq = _sigmoid(q)
    qf = q * sq
    sig = _sigmoid(fl)
    f = lb + omlb * sig
    k = 1.0 - f
    e = jnp.exp(_split_dot(w_ref[...], jnp.log(f), 3))
    return sq, qf, sig, f, k, e


def _level_factor(e, li, sl, row, qh, kh):
    el = e[(2 + li) * CH:(3 + li) * CH, sl]
    up = (row & (LEVELS[li] // 2)) != 0
    return el, up, el * jnp.where(up, qh, kh)


def _hgrn_fwd(proj, lower_bounds, gn_w, ycat, tables, placed, axes):
    T = proj.shape[0]
    nc = T // CH
    nch = min(HGRN_CHUNKS_PER_STEP, nc)
    steps = nc // nch
    w_st, _, masks, _ = tables
    nw = len(placed)
    pass_step = (5 * steps) // 8

    def body(*refs):
        q_ref, f_ref, i_ref, g_ref, lbp_ref, gn_ref, w_ref, m_ref = refs[:8]
        y_ref, o_ref, a_ref, st_ref = refs[9 + nw:13 + nw]
        s_scr, send_sems, recv_sems = refs[13 + 2 * nw:]
        gather = _WeightGather(refs[13 + nw:13 + 2 * nw], axes, send_sems, recv_sems)
        step = pl.program_id(0)

        @pl.when(step == 0)
        def _():
            gather.start()
            s_scr[...] = jnp.zeros_like(s_scr)

        @pl.when(step == pass_step)
        def _():
            gather.forward()

        lb, omlb = _lower_bound(lbp_ref)
        row = lax.broadcasted_iota(jnp.int32, (CH, 1), 0)
        eye = lax.broadcasted_iota(jnp.int32, (CH, CH), 0) == lax.broadcasted_iota(jnp.int32, (CH, CH), 1)
        pre = []
        for ci in range(nch):
            rs = slice(ci * CH, (ci + 1) * CH)
            _, qf, _, _, k, e = _hgrn_gates(q_ref[rs, :], f_ref[rs, :], lb, omlb, w_ref)
            mats = []
            for h in range(NH):
                sl = slice(h * HD, (h + 1) * HD)
                qh, kh = qf[:, sl], k[:, sl]
                a = jnp.where(eye, jnp.sum(qh * kh, axis=-1, keepdims=True), 0.0)
                for li in range(len(LEVELS)):
                    _, _, y = _level_factor(e, li, sl, row, qh, kh)
                    yb = y.astype(BF16)
                    a = a + m_ref[li] * _dot(yb, yb, NT)
                a_ref[ci, h] = a
                mats.append(a.astype(BF16))
            eb = e[0:CH]
            pre.append(((qf * eb).astype(BF16), eb[CH - 1:CH, :], (k * e[CH:2 * CH]).astype(BF16), mats))
        for ci in range(nch):
            rs = slice(ci * CH, (ci + 1) * CH)
            qe, ebl, kd, mats = pre[ci]
            v = i_ref[rs, :]
            g = g_ref[rs, :]
            for h in range(NH):
                sl = slice(h * HD, (h + 1) * HD)
                st0 = s_scr[h]
                st_ref[ci, h] = st0
                vb = v[:, sl].astype(BF16)
                o = _dot(qe[:, sl], st0.astype(BF16), NT) + _dot(mats[h], vb)
                s_scr[h] = st0 * ebl[:, sl] + _dot(vb, kd[:, sl], TN)
                o_ref[rs, sl] = o
                gh = g[:, sl]
                y_ref[rs, sl] = (((o * _rms(o)) * gn_ref[...]) * (gh * _sigmoid(gh))).astype(BF16)

        @pl.when(step == steps - 1)
        def _():
            gather.finish()

    blk = lambda j: pl.BlockSpec((nch * CH, DH), lambda c: (c, j))
    anyspec = pl.BlockSpec(memory_space=pl.ANY)
    res = pl.pallas_call(
        body, grid=(steps,),
        in_specs=[blk(2), blk(3), blk(4), blk(5), _full((2, DH)), _full((1, HD)),
                  _full(w_st.shape), _full(masks.shape), anyspec] + [anyspec] * nw,
        out_specs=[pl.BlockSpec((nch * CH, DH), lambda c: (c, 1)),
                   pl.BlockSpec((nch * CH, DH), lambda c: (c, 0)),
                   pl.BlockSpec((nch, NH, CH, CH), lambda c: (c, 0, 0, 0)),
                   pl.BlockSpec((nch, NH, HD, HD), lambda c: (c, 0, 0, 0))] + [anyspec] * nw,
        out_shape=[SDS((T, D), BF16), SDS((T, DH), F32), SDS((nc, NH, CH, CH), F32), SDS((nc, NH, HD, HD), F32)]
        + [SDS(a.shape, a.dtype) for a in placed],
        scratch_shapes=[pltpu.VMEM((NH, HD, HD), F32)] + _gather_sems(nw),
        input_output_aliases={8: 0, **{9 + i: 4 + i for i in range(nw)}},
        compiler_params=_arb(), name="hgrn_fwd")(proj, proj, proj, proj, lower_bounds, gn_w, w_st, masks, ycat, *placed)
    return res[:4], res[4:]


def _token_local(x, ycat, tgt, g1, nw2, sc2, sh2, g2, fw, w_out_b, w_fi_b, w_fo_b, tm):
    T = x.shape[0]
    inv_d = 1.0 / D

    def body(x_ref, y_ref, t_ref, g1_ref, nw2_ref, sc2_ref, sh2_ref, g2_ref, fw_ref, wo_ref, wfi_ref, wfo_ref,
             dy_ref, dx1_ref, h2_ref, act_ref, dff_ref, dgu_ref, dmix_ref, acc_ref):
        @pl.when(pl.program_id(0) == 0)
        def _():
            acc_ref[...] = jnp.zeros_like(acc_ref)

        def acc(row, val):
            acc_ref[row:row + 1, :] += jnp.sum(val, axis=0, keepdims=True)

        g1v, g2v = g1_ref[...], g2_ref[...]
        mix = _dot(y_ref[...], wo_ref[...])
        x1 = x_ref[...] + g1v * mix
        r2 = _rms(x1)
        xh2 = x1 * r2
        n2 = xh2 * nw2_ref[...]
        osc2 = 1.0 + sc2_ref[...]
        h2b = (n2 * osc2 + sh2_ref[...]).astype(BF16)
        h2_ref[...] = h2b
        ff = jnp.zeros((tm, D), F32)
        saved = []
        for kb in range(DFF // FFB):
            gate = _dot(h2b, wfi_ref[:, kb * FFB:(kb + 1) * FFB])
            up = _dot(h2b, wfi_ref[:, DFF + kb * FFB:DFF + (kb + 1) * FFB])
            sg = _sigmoid(gate)
            actb = (gate * sg * up).astype(BF16)
            act_ref[:, kb * FFB:(kb + 1) * FFB] = actb
            ff = ff + _dot(actb, wfo_ref[kb * FFB:(kb + 1) * FFB, :])
            saved.append((gate, up, sg))
        x2 = x1 + g2v * ff
        r3 = _rms(x2)
        xh3 = x2 * r3
        err = xh3 * fw_ref[...] - t_ref[...]
        acc(6, (0.5 * inv_d) * err * err)
        dy = err * inv_d
        acc(4, dy * xh3)
        dx2 = _rms_bwd(xh3, r3, dy * fw_ref[...])
        acc(0, dx2 * ff)
        dffb = (dx2 * g2v).astype(BF16)
        dff_ref[...] = dffb
        dh2 = jnp.zeros((tm, D), F32)
        for kb in range(DFF // FFB):
            gate, up, sg = saved[kb]
            da = _dot(dffb, wfo_ref[kb * FFB:(kb + 1) * FFB, :], NT)
            dgate = (da * up * (sg * (1.0 + gate * (1.0 - sg)))).astype(BF16)
            dup = (da * gate * sg).astype(BF16)
            dgu_ref[:, kb * FFB:(kb + 1) * FFB] = dgate
            dgu_ref[:, DFF + kb * FFB:DFF + (kb + 1) * FFB] = dup
            dh2 = dh2 + _dot(dgate, wfi_ref[:, kb * FFB:(kb + 1) * FFB], NT)
            dh2 = dh2 + _dot(dup, wfi_ref[:, DFF + kb * FFB:DFF + (kb + 1) * FFB], NT)
        acc(2, dh2)
        acc(1, dh2 * n2)
        dn2 = dh2 * osc2
        acc(3, dn2 * xh2)
        dx1 = dx2 + _rms_bwd(xh2, r2, dn2 * nw2_ref[...])
        acc(5, dx1 * mix)
        dmixb = (dx1 * g1v).astype(BF16)
        dmix_ref[...] = dmixb
        dy_ref[...] = _dot(dmixb, wo_ref[...], NT)
        dx1_ref[...] = dx1

    row = lambda i: (i, 0)
    vec = _full((1, D))
    return pl.pallas_call(
        body, grid=(T // tm,),
        in_specs=[pl.BlockSpec((tm, D), row), pl.BlockSpec((tm, D), row), pl.BlockSpec((tm, D), row),
                  vec, vec, vec, vec, vec, vec,
                  _resident((D, D)), _resident((D, 2 * DFF)), _resident((DFF, D))],
        out_specs=[pl.BlockSpec((tm, D), row), pl.BlockSpec((tm, D), row), pl.BlockSpec((tm, D), row),
                   pl.BlockSpec((tm, DFF), row), pl.BlockSpec((tm, D), row), pl.BlockSpec((tm, 2 * DFF), row),
                   pl.BlockSpec((tm, D), row), _full((8, D))],
        out_shape=[SDS((T, D), F32), SDS((T, D), F32), SDS((T, D), BF16), SDS((T, DFF), BF16), SDS((T, D), BF16),
                   SDS((T, 2 * DFF), BF16), SDS((T, D), BF16), SDS((8, D), F32)],
        compiler_params=_arb(), name="token_local")(x, ycat, tgt, g1, nw2, sc2, sh2, g2, fw, w_out_b, w_fi_b, w_fo_b)


def _gmlp_bwd(proj, dycat, ws_b, bst, lnw, lnb):
    T = proj.shape[0]
    nb = T // BLK

    def body(u_ref, v_ref, dy_ref, ws_ref, bst_ref, lnw_ref, lnb_ref, dp_ref, dws_ref, dbs_ref, dln_ref, dbs_acc):
        i = pl.program_id(0)

        @pl.when(i == 0)
        def _():
            dws_ref[...] = jnp.zeros_like(dws_ref)
            dln_ref[...] = jnp.zeros_like(dln_ref)
            dbs_acc[...] = jnp.zeros_like(dbs_acc)

        ug, dug, dvg, rstd, vhat, vnb, mixed = _gmlp_common(u_ref[...], v_ref[...], lnw_ref[...], lnb_ref[...], ws_ref, bst_ref)
        dya = dy_ref[...]
        dp_ref[:, 0:DG] = (dya * mixed * dug).astype(BF16)
        dmixed = dya * ug
        dbs_acc[...] += dmixed
        dmb = dmixed.astype(BF16)
        r = lax.broadcasted_iota(jnp.int32, (BLK, BLK), 0) // CH
        c = lax.broadcasted_iota(jnp.int32, (BLK, BLK), 1) // CH
        dvn = []
        for h in range(NH):
            sl = slice(h * HD, (h + 1) * HD)
            dws_ref[h] += jnp.where(r >= c, _dot(dmb[:, sl], vnb[:, sl], NT), 0.0)
            dvn.append(_dot(ws_ref[h], dmb[:, sl], TN))
        dvn = jnp.concatenate(dvn, axis=1)
        dln_ref[0:1, :] += jnp.sum(dvn * vhat, axis=0, keepdims=True)
        dln_ref[1:2, :] += jnp.sum(dvn, axis=0, keepdims=True)
        dvh = dvn * lnw_ref[...]
        dvgel = rstd * (dvh - jnp.mean(dvh, axis=-1, keepdims=True) - vhat * jnp.mean(dvh * vhat, axis=-1, keepdims=True))
        dp_ref[:, DG:2 * DG] = (dvgel * dvg).astype(BF16)

        @pl.when(i == nb - 1)
        def _():
            lane = lax.broadcasted_iota(jnp.int32, (BLK, HD), 1)
            out = jnp.zeros((BLK, HD), F32)
            for h in range(NH):
                out = out + jnp.where(lane == h, jnp.sum(dbs_acc[:, h * HD:(h + 1) * HD], axis=-1, keepdims=True), 0.0)
            dbs_ref[...] = out

    return pl.pallas_call(
        body, grid=(nb,),
        in_specs=[pl.BlockSpec((BLK, DG), lambda i: (i, 0)), pl.BlockSpec((BLK, DG), lambda i: (i, 1)),
                  pl.BlockSpec((BLK, DG), lambda i: (i, 0)),
                  _full((NH, BLK, BLK)), _full((BLK, NH)), _full((1, DG)), _full((1, DG))],
        out_specs=[pl.BlockSpec((BLK, 2 * DG), lambda i: (i, 2)), _full((NH, BLK, BLK)), _full((BLK, HD)), _full((8, DG))],
        out_shape=[SDS((T, DIN), BF16), SDS((NH, BLK, BLK), F32), SDS((BLK, HD), F32), SDS((8, DG), F32)],
        scratch_shapes=[pltpu.VMEM((BLK, DG), F32)],
        compiler_params=_arb(), name="gmlp_bwd")(proj, proj, dycat, ws_b, bst, lnw, lnb)


def _hgrn_bwd(proj, o_pre, a_all, st_all, dycat, lower_bounds, gn_w, dproj, tables, sums):
    T = proj.shape[0]
    nc = T // CH
    nch = min(HGRN_CHUNKS_PER_STEP, nc)
    steps = nc // nch
    w_st, w_st_t, masks, masks_t = tables
    n_lev = len(LEVELS)
    nw = len(sums)

    def body(*refs):
        q_ref, f_ref, i_ref, g_ref, o_ref, a_ref, st_ref, dy_ref, lbp_ref, gn_ref, w_ref, wt_ref, m_ref, mt_ref = refs[:14]
        dp_ref, dlb_ref, dgn_ref = refs[15 + nw:18 + nw]
        ds_scr, dx_scr, send_sems, recv_sems = refs[18 + 2 * nw:]
        exchange = _ChipExchange(refs[15:15 + nw], refs[18 + nw:18 + 2 * nw], send_sems, recv_sems)
        i = pl.program_id(0)

        @pl.when(i == 0)
        def _():
            exchange.start()
            ds_scr[...] = jnp.zeros_like(ds_scr)
            dlb_ref[...] = jnp.zeros_like(dlb_ref)
            dgn_ref[...] = jnp.zeros_like(dgn_ref)

        lb, omlb = _lower_bound(lbp_ref)
        row = lax.broadcasted_iota(jnp.int32, (CH, 1), 0)
        eye = lax.broadcasted_iota(jnp.int32, (CH, CH), 0) == lax.broadcasted_iota(jnp.int32, (CH, CH), 1)
        dgn = jnp.zeros((1, HD), F32)
        pre = []
        for ci in range(nch):
            rs = slice(ci * CH, (ci + 1) * CH)
            q = q_ref[rs, :]
            v = i_ref[rs, :]
            g = g_ref[rs, :]
            sq, qf, sig, f, k, e = _hgrn_gates(q, f_ref[rs, :], lb, omlb, w_ref)
            eb = e[0:CH]
            ekd = e[CH:2 * CH]
            kd = k * ekd
            qe = qf * eb
            dob_h, dqe_h, dqf_h, dki_h, dv_h, dg_h = [], [], [], [], [], []
            for h in range(NH):
                sl = slice(h * HD, (h + 1) * HD)
                o = o_ref[rs, sl]
                ro = _rms(o)
                oh = o * ro
                gh = g[:, sl]
                sg = _sigmoid(gh)
                dyb = dy_ref[rs, sl]
                dg_h.append(dyb * (oh * gn_ref[...]) * (sg * (1.0 + gh * (1.0 - sg))))
                don = dyb * (gh * sg)
                dgn = dgn + jnp.sum(don * oh, axis=0, keepdims=True)
                dob = _rms_bwd(oh, ro, don * gn_ref[...]).astype(BF16)
                vb = v[:, sl].astype(BF16)
                qh, kh = qf[:, sl], k[:, sl]
                dqe = _dot(dob, st_ref[ci, h].astype(BF16))
                da = _dot(dob, vb, NT)
                dat = _dot(vb, dob, NT)
                ddiag = jnp.sum(jnp.where(eye, da, 0.0), axis=-1, keepdims=True)
                dqi = ddiag * kh
                dki = ddiag * qh
                for li in range(n_lev):
                    el, up, y = _level_factor(e, li, sl, row, qh, kh)
                    dgs = m_ref[li] * da + mt_ref[li] * dat
                    dyv = _dot(dgs.astype(BF16), y.astype(BF16))
                    dx_scr[ci, (2 + li) * CH:(3 + li) * CH, sl] = dyv * y
                    dye = dyv * el
                    dqi = dqi + jnp.where(up, dye, 0.0)
                    dki = dki + jnp.where(up, 0.0, dye)
                dob_h.append(dob)
                dqe_h.append(dqe)
                dqf_h.append(dqe * eb[:, sl] + dqi)
                dki_h.append(dki)
                dv_h.append(_dot(a_ref[ci, h].astype(BF16), dob, TN))
            dp_ref[rs, 0:DH] = (jnp.concatenate(dqf_h, axis=1) * (sq * (1.0 + q * (1.0 - sq)))).astype(BF16)
            dp_ref[rs, 3 * DH:4 * DH] = jnp.concatenate(dg_h, axis=1).astype(BF16)
            pre.append((v, sig, f, eb, ekd, kd, qe, dob_h, jnp.concatenate(dqe_h, axis=1), dki_h, dv_h))
        dgn_ref[0:1, :] += dgn
        for ci in reversed(range(nch)):
            rs = slice(ci * CH, (ci + 1) * CH)
            v, sig, f, eb, ekd, kd, qe, dob_h, dqe, dki_h, dv_h = pre[ci]
            ebl = eb[CH - 1:CH, :]
            dbl_h, dkd_h, dv2_h = [], [], []
            for h in range(NH):
                sl = slice(h * HD, (h + 1) * HD)
                dst1 = ds_scr[h]
                dst1b = dst1.astype(BF16)
                ds_scr[h] = dst1 * ebl[:, sl] + _dot(dob_h[h], qe[:, sl].astype(BF16), TN)
                dbl_h.append(ebl[:, sl] * jnp.sum(st_ref[ci, h] * dst1, axis=0, keepdims=True))
                dkd_h.append(_dot(v[:, sl].astype(BF16), dst1b))
                dv2_h.append(dv_h[h] + _dot(kd[:, sl].astype(BF16), dst1b, NT))
            dkd = jnp.concatenate(dkd_h, axis=1)
            dx_scr[ci, 0:CH, :] = dqe * qe + jnp.where(row == CH - 1, jnp.concatenate(dbl_h, axis=1), 0.0)
            dx_scr[ci, CH:2 * CH, :] = dkd * kd
            dlf = _split_dot(wt_ref[...], dx_scr[ci], 2)
            df = dlf / f - (dkd * ekd + jnp.concatenate(dki_h, axis=1))
            dlb_ref[0:1, :] += jnp.sum(df * (1.0 - sig), axis=0, keepdims=True)
            dp_ref[rs, DH:2 * DH] = (df * omlb * sig * (1.0 - sig)).astype(BF16)
            dp_ref[rs, 2 * DH:3 * DH] = jnp.concatenate(dv2_h, axis=1).astype(BF16)

        @pl.when(i == steps - 1)
        def _():
            gl = dlb_ref[0:1, :] * lb * omlb
            dlb_ref[0:1, :] = gl
            dlb_ref[1:2, :] = -gl
            exchange.finish()

    rev = lambda j: pl.BlockSpec((nch * CH, DH), lambda c: (steps - 1 - c, j))
    anyspec = pl.BlockSpec(memory_space=pl.ANY)
    res = pl.pallas_call(
        body, grid=(steps,),
        in_specs=[rev(2), rev(3), rev(4), rev(5), rev(0),
                  pl.BlockSpec((nch, NH, CH, CH), lambda c: (steps - 1 - c, 0, 0, 0)),
                  pl.BlockSpec((nch, NH, HD, HD), lambda c: (steps - 1 - c, 0, 0, 0)),
                  rev(1), _full((2, DH)), _full((1, HD)),
                  _full(w_st.shape), _full(w_st_t.shape), _full(masks.shape), _full(masks_t.shape),
                  anyspec] + [anyspec] * nw,
        out_specs=[pl.BlockSpec((nch * CH, 4 * DH), lambda c: (steps - 1 - c, 0)), _full((8, DH)), _full((8, HD))]
        + [anyspec] * nw,
        out_shape=[SDS((T, DIN), BF16), SDS((8, DH), F32), SDS((8, HD), F32)] + _slot_shapes(sums),
        scratch_shapes=[pltpu.VMEM((NH, HD, HD), F32), pltpu.VMEM((nch, (2 + n_lev) * CH, DH), F32)] + _exchange_sems(nw),
        input_output_aliases={14: 0},
        compiler_params=_arb(), name="hgrn_bwd")(proj, proj, proj, proj, o_pre, a_all, st_all, dycat, lower_bounds, gn_w,
                                                 w_st, w_st_t, masks, masks_t, dproj, *sums)
    return res[:3], res[3:]


def _proj_in_bwd(dproj, x, dx1, nw, sc, w_in_b, tm):
    T = x.shape[0]

    def body(dp_ref, x_ref, dx1_ref, nw_ref, sc_ref, w_ref, gx_ref, acc_ref):
        @pl.when(pl.program_id(0) == 0)
        def _():
            acc_ref[...] = jnp.zeros_like(acc_ref)

        dh = _dot(dp_ref[:, 0:4 * DH], w_ref[:, 2 * DG:DIN], NT) + _dot(dp_ref[:, 4 * DH:DIN], w_ref[:, 0:2 * DG], NT)
        xv = x_ref[...]
        r = _rms(xv)
        xh = xv * r
        n1 = xh * nw_ref[...]
        acc_ref[0:1, :] += jnp.sum(dh, axis=0, keepdims=True)
        acc_ref[1:2, :] += jnp.sum(dh * n1, axis=0, keepdims=True)
        dn = dh * (1.0 + sc_ref[...])
        acc_ref[2:3, :] += jnp.sum(dn * xh, axis=0, keepdims=True)
        gx_ref[...] = dx1_ref[...] + _rms_bwd(xh, r, dn * nw_ref[...])

    row = lambda i: (i, 0)
    return pl.pallas_call(
        body, grid=(T // tm,),
        in_specs=[pl.BlockSpec((tm, DIN), row), pl.BlockSpec((tm, D), row), pl.BlockSpec((tm, D), row),
                  _full((1, D)), _full((1, D)), _resident((D, DIN))],
        out_specs=[pl.BlockSpec((tm, D), row), _full((8, D))],
        out_shape=[SDS((T, D), F32), SDS((8, D), F32)],
        compiler_params=_arb(), name="proj_in_bwd")(dproj, x, dx1, nw, sc, w_in_b)


def _wgrad(a, b, bk, bn, tt, name):
    T, K = a.shape
    N = b.shape[1]
    nn, nk, nt = N // bn, K // bk, T // tt
    bmap = lambda n, k, t: (t, n)

    def body(a_ref, b_ref, o_ref):
        @pl.when(pl.program_id(2) == 0)
        def _():
            o_ref[...] = jnp.zeros_like(o_ref)

        o_ref[0] += _dot(a_ref[...], b_ref[...], TN)

    return pl.pallas_call(
        body, grid=(nn, nk, nt),
        in_specs=[pl.BlockSpec((tt, bk), lambda n, k, t: (t, k)), pl.BlockSpec((tt, bn), bmap)],
        out_specs=pl.BlockSpec((1, bk, bn), lambda n, k, t: (n, k, 0)),
        out_shape=SDS((nn, K, bn), F32),
        compiler_params=_arb(3), name=name)(a, b)


def _adam_math(w, g, m, v):
    m = B1 * m + (1.0 - B1) * g
    v = B2 * v + (1.0 - B2) * (g * g)
    m_hat = m / (1.0 - B1 ** STEP)
    v_hat = v / (1.0 - B2 ** STEP)
    return -LR * (m_hat / (jnp.sqrt(v_hat) + AEPS) + WD * w), m, v


def _adamw_halves(w, mine, sibling, m, v, c_idx, rb, name):
    R, C = w.shape
    nb = (R // 2) // rb

    def body(c_ref, w_ref, a_ref, b_ref, m_ref, v_ref, g_out, d_out, m_out, v_out):
        g = jnp.where(pl.program_id(0) == c_ref[0], a_ref[...], b_ref[...])
        g_out[...] = g
        d_out[...], m_out[...], v_out[...] = _adam_math(w_ref[...], g, m_ref[...], v_ref[...])

    whole = pl.BlockSpec((rb, C), lambda hh, i, cr: (hh * nb + i, 0))
    half = pl.BlockSpec((rb, C), lambda hh, i, cr: (i, 0))
    return pl.pallas_call(
        body,
        grid_spec=pltpu.PrefetchScalarGridSpec(
            num_scalar_prefetch=1, grid=(2, nb), in_specs=[whole, half, half, whole, whole], out_specs=[whole] * 4),
        out_shape=[SDS((R, C), F32)] * 4, compiler_params=_arb(2), name=name)(c_idx, w, mine, sibling, m, v)


def _ada_forward(c_all, w_ada):
    n = w_ada.shape[1]

    def body(c_ref, w_ref, ca_ref, p_ref):
        cv = c_ref[...]
        ca = cv * _sigmoid(cv)
        ca_ref[...] = ca
        p_ref[...] = _dot(ca, w_ref[...], precision=HIGHEST)

    return pl.pallas_call(
        body, grid=(n // 512,),
        in_specs=[_full((N_DEV, D)), pl.BlockSpec((D, 512), lambda i: (0, i))],
        out_specs=[_full((N_DEV, D)), pl.BlockSpec((N_DEV, 512), lambda i: (0, i))],
        out_shape=[SDS((N_DEV, D), F32), SDS((N_DEV, n), F32)],
        compiler_params=_arb(), name="ada_forward")(c_all, w_ada)


def _ada_wgrad_adam(cact_t, dada, w, m, v):
    R, C = w.shape
    rb = 256

    def body(c_ref, d_ref, w_ref, m_ref, v_ref, g_out, d_out, m_out, v_out):
        g = _dot(c_ref[...], d_ref[...], precision=HIGHEST)
        g_out[...] = g
        d_out[...], m_out[...], v_out[...] = _adam_math(w_ref[...], g, m_ref[...], v_ref[...])

    spec = pl.BlockSpec((rb, C), lambda i: (i, 0))
    return pl.pallas_call(
        body, grid=(R // rb,),
        in_specs=[pl.BlockSpec((rb, N_DEV), lambda i: (i, 0)), _full((N_DEV, C)), spec, spec, spec],
        out_specs=[spec] * 4, out_shape=[SDS((R, C), F32)] * 4,
        compiler_params=_arb(), name="ada_wgrad_adam")(cact_t, dada, w, m, v)


def _small_finalize(gathered, w, m, v):
    def body(ga_ref, w_ref, m_ref, v_ref, g_out, d_out, m_out, v_out):
        g = ga_ref[0:SMALL_ROWS, :]
        for dev in range(1, N_DEV):
            g = g + ga_ref[dev * SMALL_ROWS:(dev + 1) * SMALL_ROWS, :]
        g_out[...] = g
        d_out[...], m_out[...], v_out[...] = _adam_math(w_ref[...], g, m_ref[...], v_ref[...])

    return pl.pallas_call(
        body, out_shape=[SDS((SMALL_ROWS, D), F32)] * 4, name="small_finalize")(gathered, w, m, v)


def _position():
    x, y, c = lax.axis_index("x"), lax.axis_index("y"), lax.axis_index("c")
    return x, y, c


def _chip_at(x, y, r):
    return (x ^ (r >> 1), y ^ (r & 1))


def _all_gather_rows(block, name):
    m_per, n = block.shape

    def body(x_ref, out_ref, send_sems, recv_sems, local_sem):
        x, y, c = _position()
        me, sibling = (x, y, c), (x, y, 1 - c)
        chips = [_chip_at(x, y, r) for r in (1, 2, 3)]

        def rows(px, py, pc):
            return out_ref.at[pl.ds((4 * px + 2 * py + pc) * m_per, m_per), :]

        def copy(k, blk, to, src=None):
            return pltpu.make_async_remote_copy(
                src_ref=rows(*blk) if src is None else src, dst_ref=rows(*blk),
                send_sem=send_sems.at[k], recv_sem=recv_sems.at[k], device_id=to, device_id_type=MESH)

        mine = pltpu.make_async_copy(x_ref, rows(*me), local_sem)
        mine.start()
        first = [copy(0, me, sibling, src=x_ref)]
        first += [copy(1 + j, me, (*chip, c), src=x_ref) for j, chip in enumerate(chips)]
        for cp in first:
            cp.start()
        passed = [copy(4 + j, (*chip, c), sibling) for j, chip in enumerate(chips)]
        for j, chip in enumerate(chips):
            copy(1 + j, (*chip, c), me).wait_recv()
            passed[j].start()
        copy(0, sibling, me).wait_recv()
        for j, chip in enumerate(chips):
            copy(4 + j, (*chip, 1 - c), me).wait_recv()
        for cp in first + passed:
            cp.wait_send()
        mine.wait()

    return pl.pallas_call(
        body, out_shape=SDS((N_DEV * m_per, n), block.dtype),
        in_specs=[pl.BlockSpec(memory_space=pltpu.VMEM)], out_specs=pl.BlockSpec(memory_space=pltpu.VMEM),
        scratch_shapes=[pltpu.SemaphoreType.DMA((7,)), pltpu.SemaphoreType.DMA((7,)), pltpu.SemaphoreType.DMA],
        name=name)(block)


def _place_shard(w_shard, axis, chip_idx, name):
    R, C = w_shard.shape
    rb = _row_block(R)
    nb = R // rb
    full = (R * N_CHIPS, C) if axis == 0 else (R, C * N_CHIPS)
    omap = (lambda i, j: (j[0] * nb + i, 0)) if axis == 0 else (lambda i, j: (i, j[0]))

    def body(j_ref, w_ref, o_ref):
        o_ref[...] = w_ref[...].astype(BF16)

    return pl.pallas_call(
        body,
        grid_spec=pltpu.PrefetchScalarGridSpec(
            num_scalar_prefetch=1, grid=(nb,), in_specs=[pl.BlockSpec((rb, C), lambda i, j: (i, 0))],
            out_specs=pl.BlockSpec((rb, C), omap)),
        out_shape=SDS(full, BF16), compiler_params=_arb(), name=name)(chip_idx, w_shard)


class _WeightGather:
    def __init__(self, refs, axes, send_sems, recv_sems):
        self.refs, self.axes, self.send_sems, self.recv_sems = refs, axes, send_sems, recv_sems
        self.x, self.y, self.c = _position()
        self.j = 2 * self.x + self.y
        self.n = 3 * len(refs)

    def _half(self, w, chip_idx, half):
        ref, axis = self.refs[w], self.axes[w]
        if axis == 0:
            size = ref.shape[0] // N_CHIPS
            return ref.at[pl.ds(chip_idx * size + half * (size // 2), size // 2), :]
        size = ref.shape[1] // N_CHIPS
        rows = ref.shape[0] // 2
        return ref.at[pl.ds(half * rows, rows), pl.ds(chip_idx * size, size)]

    def _ici(self, w, r, chip_idx):
        k = 3 * w + r - 1
        piece = self._half(w, chip_idx, self.c)
        return pltpu.make_async_remote_copy(
            src_ref=piece, dst_ref=piece, send_sem=self.send_sems.at[k], recv_sem=self.recv_sems.at[k],
            device_id=(*_chip_at(self.x, self.y, r), self.c), device_id_type=MESH)

    def _d2d(self, w, r, half):
        k = self.n + 3 * w + r - 1
        piece = self._half(w, self.j ^ r, half)
        return pltpu.make_async_remote_copy(
            src_ref=piece, dst_ref=piece, send_sem=self.send_sems.at[k], recv_sem=self.recv_sems.at[k],
            device_id=(self.x, self.y, 1 - self.c), device_id_type=MESH)

    def _each(self):
        return [(w, r) for w in range(len(self.refs)) for r in (1, 2, 3)]

    def start(self):
        for w, r in self._each():
            self._ici(w, r, self.j).start()

    def forward(self):
        for w, r in self._each():
            self._ici(w, r, self.j ^ r).wait_recv()
            self._d2d(w, r, self.c).start()

    def finish(self):
        for w, r in self._each():
            self._ici(w, r, self.j).wait_send()
            self._d2d(w, r, self.c).wait_send()
            self._d2d(w, r, 1 - self.c).wait_recv()


def _gather_sems(n_weights):
    return [pltpu.SemaphoreType.DMA((6 * n_weights,)), pltpu.SemaphoreType.DMA((6 * n_weights,))]


def _gather_weights(placed, axes, name):
    nw = len(placed)

    def body(*refs):
        outs = refs[nw:2 * nw]
        g = _WeightGather(outs, axes, *refs[2 * nw:])
        g.start()
        g.forward()
        g.finish()

    anyspec = pl.BlockSpec(memory_space=pl.ANY)
    return pl.pallas_call(
        body, out_shape=[SDS(a.shape, a.dtype) for a in placed], in_specs=[anyspec] * nw, out_specs=[anyspec] * nw,
        scratch_shapes=_gather_sems(nw), input_output_aliases={i: i for i in range(nw)},
        name=name)(*placed)


class _ChipExchange:
    def __init__(self, ins, outs, send_sems, recv_sems):
        self.ins, self.outs, self.send_sems, self.recv_sems = ins, outs, send_sems, recv_sems
        self.x, self.y, self.c = _position()
        self.j = 2 * self.x + self.y

    def _copies(self):
        for w in range(len(self.ins)):
            for r in (1, 2, 3):
                k = 3 * w + r - 1
                yield pltpu.make_async_remote_copy(
                    src_ref=self.ins[w].at[self.j ^ r], dst_ref=self.outs[w].at[r - 1],
                    send_sem=self.send_sems.at[k], recv_sem=self.recv_sems.at[k],
                    device_id=(*_chip_at(self.x, self.y, r), self.c), device_id_type=MESH)

    def start(self):
        for cp in self._copies():
            cp.start()

    def finish(self):
        for cp in self._copies():
            cp.wait()


def _exchange_sems(n_weights):
    return [pltpu.SemaphoreType.DMA((3 * n_weights,)), pltpu.SemaphoreType.DMA((3 * n_weights,))]


def _exchange_core_halves(grads, name):
    nw = len(grads)

    def body(*refs):
        ins, outs = refs[:nw], refs[nw:2 * nw]
        send_sems, recv_sems = refs[2 * nw:]
        x, y, c = _position()
        cps = []
        for w in range(nw):
            cp = pltpu.make_async_remote_copy(
                src_ref=ins[w].at[:, 1 - c], dst_ref=outs[w], send_sem=send_sems.at[w], recv_sem=recv_sems.at[w],
                device_id=(x, y, 1 - c), device_id_type=MESH)
            cp.start()
            cps.append(cp)
        for cp in cps:
            cp.wait()

    anyspec = pl.BlockSpec(memory_space=pl.ANY)
    return pl.pallas_call(
        body, out_shape=[SDS((g.shape[0], g.shape[2], g.shape[3]), F32) for g in grads],
        in_specs=[anyspec] * nw, out_specs=[anyspec] * nw,
        scratch_shapes=[pltpu.SemaphoreType.DMA((nw,)), pltpu.SemaphoreType.DMA((nw,))],
        name=name)(*grads)


def _add_core_halves(g4, recv, c_idx, rb, name):
    ns, _, rh, C = g4.shape

    def body(c_ref, g_ref, r_ref, o_ref):
        o_ref[...] = (g_ref[0] + r_ref[...]).astype(BF16)

    return pl.pallas_call(
        body,
        grid_spec=pltpu.PrefetchScalarGridSpec(
            num_scalar_prefetch=1, grid=(ns, rh // rb),
            in_specs=[pl.BlockSpec((1, 1, rb, C), lambda s, i, cr: (s, cr[0], i, 0)),
                      pl.BlockSpec((1, rb, C), lambda s, i, cr: (s, i, 0))],
            out_specs=pl.BlockSpec((1, rb, C), lambda s, i, cr: (s, i, 0))),
        out_shape=SDS((ns, rh, C), BF16), compiler_params=_arb(2), name=name)(c_idx, g4, recv)


def _slot_shapes(sums):
    return [SDS((3,) + s.shape[1:], s.dtype) for s in sums]


def _exchange_chips(sums, name):
    nw = len(sums)

    def body(*refs):
        ex = _ChipExchange(refs[:nw], refs[nw:2 * nw], *refs[2 * nw:])
        ex.start()
        ex.finish()

    anyspec = pl.BlockSpec(memory_space=pl.ANY)
    return pl.pallas_call(
        body, out_shape=_slot_shapes(sums), in_specs=[anyspec] * nw, out_specs=[anyspec] * nw,
        scratch_shapes=_exchange_sems(nw), name=name)(*sums)


def _add_chips(own, slots, order, rb, name):
    _, rh, C = slots.shape

    def body(o_ref, own_ref, a_ref, b_ref, c_ref, d_ref, out_ref):
        mine = own_ref[0].astype(F32)
        t = [jnp.where(o_ref[i] == 0, mine, r[0].astype(F32)) for i, r in enumerate((a_ref, b_ref, c_ref, d_ref))]
        out_ref[...] = ((t[0] + t[1]) + t[2]) + t[3]

    def spec(i):
        return pl.BlockSpec((1, rb, C), lambda t, o: (jnp.maximum(o[i], 1) - 1, t, 0))

    return pl.pallas_call(
        body,
        grid_spec=pltpu.PrefetchScalarGridSpec(
            num_scalar_prefetch=1, grid=(rh // rb,),
            in_specs=[pl.BlockSpec((1, rb, C), lambda t, o: (o[4], t, 0)), spec(0), spec(1), spec(2), spec(3)],
            out_specs=pl.BlockSpec((rb, C), lambda t, o: (t, 0))),
        out_shape=SDS((rh, C), F32), compiler_params=_arb(), name=name)(order, own, slots, slots, slots, slots)


def _share_halves(halves):
    nw = len(halves)

    def body(*refs):
        ins, outs = refs[:nw], refs[nw:2 * nw]
        send_sems, recv_sems = refs[2 * nw:]
        x, y, c = _position()
        started = []
        for w in range(nw):
            cp = pltpu.make_async_remote_copy(
                src_ref=ins[w], dst_ref=outs[w], send_sem=send_sems.at[w], recv_sem=recv_sems.at[w],
                device_id=(x, y, 1 - c), device_id_type=MESH)
            cp.start()
            started.append(cp)
        for cp in started:
            cp.wait()

    anyspec = pl.BlockSpec(memory_space=pl.ANY)
    return pl.pallas_call(
        body, out_shape=[SDS(h.shape, F32) for h in halves], in_specs=[anyspec] * nw, out_specs=[anyspec] * nw,
        scratch_shapes=[pltpu.SemaphoreType.DMA((nw,)), pltpu.SemaphoreType.DMA((nw,))],
        name="share_halves")(*halves)


def _pack_small(b_ada, norm1_w, norm2_w, final_norm_w, v_ln_w, v_ln_b, lower_bounds, b_s, gn_w, w_s):
    parts = [b_ada, norm1_w, norm2_w, final_norm_w, v_ln_w, v_ln_b, lower_bounds, b_s, gn_w,
             jnp.zeros((D - NH * BLK - HD,), F32), w_s, jnp.zeros(((SMALL_ROWS - 76) * D,), F32)]
    return jnp.concatenate([p.reshape(-1) for p in parts]).reshape(SMALL_ROWS, D)


def _unpack_small(p):
    return dict(
        b_ada=p[0:6].reshape(1, 6 * D), norm1_w=p[6:7], norm2_w=p[7:8], final_norm_w=p[8],
        v_ln_w=p[9:10, 0:DG], v_ln_b=p[9:10, DG:D], lower_bounds=p[10].reshape(2, DH),
        b_s=p[11, 0:NH * BLK].reshape(1, NH, BLK), gn_w=p[11:12, NH * BLK:NH * BLK + HD],
        w_s=p[12:76].reshape(1, NH, BLK, BLK))


def _row_block(r):
    for cand in (256, 176, 128, 64, 32, 16, 8):
        if r % cand == 0:
            return cand
    return r


def kernel(x, c, w_ada, b_ada, norm1_w, w_in, w_s, b_s, v_ln_w, v_ln_b, lower_bounds, gn_w, w_out, norm2_w, w_ffn_in, w_ffn_out, final_norm_w, loss_target, m_w_ada, m_b_ada, m_norm1_w, m_w_in, m_w_s, m_b_s, m_v_ln_w, m_v_ln_b, m_lower_bounds, m_gn_w, m_w_out, m_norm2_w, m_w_ffn_in, m_w_ffn_out, m_final_norm_w, v_w_ada, v_b_ada, v_norm1_w, v_w_in, v_w_s, v_b_s, v_v_ln_w, v_v_ln_b, v_lower_bounds, v_gn_w, v_w_out, v_norm2_w, v_w_ffn_in, v_w_ffn_out, v_final_norm_w):
    T = x.shape[1]
    tm = min(256, T)
    px, py, pc = _position()
    chip = 2 * px + py
    me = 4 * px + 2 * py + pc
    x2d = x.reshape(T, D)
    tgt = loss_target.reshape(T, D)

    c_all = _all_gather_rows(jnp.broadcast_to(c, (8, D)), "gather_c").reshape(N_DEV, 8, D)[:, 0, :]
    cact, ada_part = _ada_forward(c_all, w_ada[0])
    n_ada = ada_part.shape[1]
    ada_all = _all_gather_rows(ada_part, "gather_ada").reshape(N_CHIPS, 2, N_DEV, n_ada)[:, 0]
    ada = lax.dynamic_index_in_dim(ada_all, me, axis=1, keepdims=False).reshape(1, 6 * D) + b_ada
    sh1, sc1, g1, sh2, sc2, g2 = [ada[:, i * D:(i + 1) * D] for i in range(6)]

    chip_idx = jnp.reshape(chip, (1,)).astype(jnp.int32)
    c_idx = jnp.reshape(pc, (1,)).astype(jnp.int32)
    (w_in_b,) = _gather_weights([_place_shard(w_in[0], 1, chip_idx, "place_in")], [1], "gather_w_in")
    placed = [_place_shard(w_out[0], 0, chip_idx, "place_out"), _place_shard(w_ffn_in[0], 1, chip_idx, "place_ffn_in"),
              _place_shard(w_ffn_out[0], 0, chip_idx, "place_ffn_out")]

    rr = lax.broadcasted_iota(jnp.int32, (BLK, BLK), 0) // CH
    cc = lax.broadcasted_iota(jnp.int32, (BLK, BLK), 1) // CH
    ws_b = jnp.where((rr >= cc)[None], w_s[0], 0.0).astype(BF16)
    bst = b_s[0].T
    lnw, lnb = v_ln_w, v_ln_b
    nw1, nw2, fw = norm1_w, norm2_w, final_norm_w.reshape(1, D)

    h1, proj = _proj_in(x2d, nw1, sc1, sh1, w_in_b, tm)
    ycat = _gmlp_fwd(proj, ws_b, bst, lnw, lnb)
    tables = _hgrn_tables()
    (ycat, o_pre, a_all, st_all), (w_out_b, w_fi_b, w_fo_b) = _hgrn_fwd(
        proj, lower_bounds, gn_w, ycat, tables, placed, [0, 1, 0])

    dycat, dx1, h2, act, dff, dgu, dmix, acc2 = _token_local(
        x2d, ycat, tgt, g1, nw2, sc2, sh2, g2, fw, w_out_b, w_fi_b, w_fo_b, tm)

    tt = min(512, T)
    order = jnp.concatenate([chip ^ jnp.arange(N_CHIPS, dtype=jnp.int32), chip_idx]).astype(jnp.int32)

    def core_pair_sums(grads, names, tag):
        g4 = [g.reshape(N_CHIPS, 2, g.shape[1] // 2, g.shape[2]) for g in grads]
        recv = _exchange_core_halves(g4, "exchange_core_halves_" + tag)
        return [_add_core_halves(a, b, c_idx, _row_block(a.shape[2]), "add_core_" + n) for a, b, n in zip(g4, recv, names)]

    def chip_sums(sums, slots, names):
        return [_add_chips(o, s, order, _row_block(s.shape[1]), "add_chips_" + n) for o, s, n in zip(sums, slots, names)]

    g_out = _wgrad(ycat, dmix, D, D, tt, "wgrad_out").reshape(N_CHIPS, D // N_CHIPS, D)
    g_fi = _wgrad(h2, dgu, D, FFB, tt, "wgrad_ffn_in")
    g_fo = _wgrad(act, dff, FFB, D, tt, "wgrad_ffn_out").reshape(N_CHIPS, DFF // N_CHIPS, D)
    late_names = ["out", "ffn_in", "ffn_out"]
    late_sums = core_pair_sums([g_out, g_fi, g_fo], late_names, "ffn")

    dproj, dws, dbs, dln = _gmlp_bwd(proj, dycat, ws_b, bst, lnw, lnb)
    (dproj, dlb, dgn), late_slots = _hgrn_bwd(
        proj, o_pre, a_all, st_all, dycat, lower_bounds, gn_w, dproj, tables, late_sums)
    grad_x, acc1 = _proj_in_bwd(dproj, x2d, dx1, nw1, sc1, w_in_b, tm)

    g_in = _wgrad(h1, dproj, D, D, tt, "wgrad_in")
    g_in = jnp.concatenate([g_in[2], g_in[0], g_in[1]], axis=1).reshape(D, N_CHIPS, DIN // N_CHIPS).transpose(1, 0, 2)
    in_sums = core_pair_sums([g_in], ["in"], "in")
    in_slots = _exchange_chips(in_sums, "exchange_chips_in")
    names = ["in"] + late_names
    halves = chip_sums(in_sums, in_slots, ["in"]) + chip_sums(late_sums, late_slots, late_names)
    sibling_halves = _share_halves(halves)

    big_w = [(w_in, m_w_in, v_w_in), (w_out, m_w_out, v_w_out), (w_ffn_in, m_w_ffn_in, v_w_ffn_in),
             (w_ffn_out, m_w_ffn_out, v_w_ffn_out)]
    big_out = []
    for mine, sib, (w, m, v), n in zip(halves, sibling_halves, big_w, names):
        res = _adamw_halves(w[0], mine, sib, m[0], v[0], c_idx, _row_block(mine.shape[0]), "adamw_" + n)
        big_out.append([r[None] for r in res])

    d_ada = jnp.stack([acc1[0], acc1[1], acc2[5], acc2[2], acc2[1], acc2[0]]).reshape(1, 6 * D)
    small_g = _pack_small(d_ada, acc1[2], acc2[3], acc2[4], dln[0], dln[1], dlb[0:2], dbs[:, 0:NH].T, dgn[0], dws)
    gathered = _all_gather_rows(small_g, "gather_small")
    sw = _pack_small(b_ada, norm1_w, norm2_w, final_norm_w, v_ln_w, v_ln_b, lower_bounds, b_s, gn_w, w_s)
    sm = _pack_small(m_b_ada, m_norm1_w, m_norm2_w, m_final_norm_w, m_v_ln_w, m_v_ln_b, m_lower_bounds, m_b_s, m_gn_w, m_w_s)
    sv = _pack_small(v_b_ada, v_norm1_w, v_norm2_w, v_final_norm_w, v_v_ln_w, v_v_ln_b, v_lower_bounds, v_b_s, v_gn_w, v_w_s)
    small = [_unpack_small(p) for p in _small_finalize(gathered, sw, sm, sv)]

    dada_all = gathered.reshape(N_DEV, SMALL_ROWS, D)[:, 0:6, :].reshape(N_DEV, 6 * D)
    dada = lax.dynamic_slice_in_dim(dada_all, chip * n_ada, n_ada, axis=1)
    ada_out = [o[None] for o in _ada_wgrad_adam(cact.T, dada, w_ada[0], m_w_ada[0], v_w_ada[0])]

    loss = lax.psum(jnp.sum(acc2[6]), ("x", "y", "c"))

    order_names = ['w_ada', 'b_ada', 'norm1_w', 'w_in', 'w_s', 'b_s', 'v_ln_w', 'v_ln_b', 'lower_bounds', 'gn_w',
                   'w_out', 'norm2_w', 'w_ffn_in', 'w_ffn_out', 'final_norm_w']
    big_idx = {'w_in': 0, 'w_out': 1, 'w_ffn_in': 2, 'w_ffn_out': 3}
    outs = [loss, grad_x.reshape(1, T, D)]
    for kind in range(4):
        for n in order_names:
            if n == 'w_ada':
                outs.append(ada_out[kind])
            elif n in big_idx:
                outs.append(big_out[big_idx[n]][kind])
            else:
                outs.append(small[kind][n])
    return tuple(outs)
```

```python
import functools

import jax
import jax.numpy as jnp
import numpy as np
from jax import lax
from jax.experimental import pallas as pl
from jax.experimental.pallas import tpu as pltpu

F32 = jnp.float32
BF16 = jnp.bfloat16
SDS = jax.ShapeDtypeStruct
MESH = pl.DeviceIdType.MESH
HIGHEST = lax.Precision.HIGHEST

D = 1024
DG = 512
DH = 512
NH = 4
HD = 128
BLK = 128
CH = 64
DFF = 2816
DIN = 3072
FFB = 1408
LEVELS = (64, 32, 16, 8, 4, 2)
HGRN_CHUNKS_PER_STEP = 4
GMLP_ROWS_PER_STEP = 512
TOKEN_TILE = 256
PROJ_TILE = 512
WGRAD_TOKENS = 2048
N_CHIPS = 4
N_DEV = 8
EPS = 1e-6
LR, B1, B2, AEPS, WD, STEP = 0.001, 0.9, 0.999, 1e-08, 0.01, 10
SMALL_ROWS = 80

NT = (((1,), (1,)), ((), ()))
TN = (((0,), (0,)), ((), ()))


def _full(shape):
    nd = len(shape)
    return pl.BlockSpec(shape, lambda *_: (0,) * nd)


def _resident(shape):
    nd = len(shape)
    return pl.BlockSpec(shape, lambda *_: (0,) * nd, pipeline_mode=pl.Buffered(1))


def _arb(n=1):
    return pltpu.CompilerParams(dimension_semantics=("arbitrary",) * n)


def _dot(a, b, dims=None, precision=None):
    if dims is None:
        return jnp.dot(a, b, preferred_element_type=F32, precision=precision)
    return lax.dot_general(a, b, dims, preferred_element_type=F32, precision=precision)


def _sigmoid(x):
    return jax.nn.sigmoid(x)


def _gelu_parts(x):
    cdf = 0.5 * (1.0 + lax.erf(x * 0.7071067811865476))
    pdf = jnp.exp(-0.5 * x * x) * 0.3989422804014327
    return x * cdf, cdf + x * pdf


def _rms(x):
    return lax.rsqrt(jnp.mean(x * x, axis=-1, keepdims=True) + EPS)


def _rms_bwd(xhat, r, gw):
    return r * (gw - xhat * jnp.mean(xhat * gw, axis=-1, keepdims=True))


def _lower_bound(lbp_ref):
    l0, l1 = lbp_ref[0:1, :], lbp_ref[1:2, :]
    m = jnp.maximum(l0, l1)
    e0, e1 = jnp.exp(l0 - m), jnp.exp(l1 - m)
    return e0 / (e0 + e1), e1 / (e0 + e1)


def _proj_in(x, nw, sc, sh, w_in_b, tm):
    T = x.shape[0]

    def body(x_ref, nw_ref, sc_ref, sh_ref, w_ref, h_ref, p_ref):
        xv = x_ref[...]
        h = ((xv * _rms(xv)) * nw_ref[...]) * (1.0 + sc_ref[...]) + sh_ref[...]
        hb = h.astype(BF16)
        h_ref[...] = hb
        p_ref[...] = _dot(hb, w_ref[...])

    row = lambda i: (i, 0)
    return pl.pallas_call(
        body, grid=(T // tm,),
        in_specs=[pl.BlockSpec((tm, D), row), _full((1, D)), _full((1, D)), _full((1, D)), _resident((D, DIN))],
        out_specs=[pl.BlockSpec((tm, D), row), pl.BlockSpec((tm, DIN), row)],
        out_shape=[SDS((T, D), BF16), SDS((T, DIN), F32)],
        compiler_params=_arb(), name="proj_in")(x, nw, sc, sh, w_in_b)


def _gmlp_common(u, v, lnw, lnb, ws_ref, bst_ref):
    ug, dug = _gelu_parts(u)
    vg, dvg = _gelu_parts(v)
    mu = jnp.mean(vg, axis=-1, keepdims=True)
    vc = vg - mu
    rstd = lax.rsqrt(jnp.mean(vc * vc, axis=-1, keepdims=True) + EPS)
    vhat = vc * rstd
    vn = vhat * lnw + lnb
    vnb = vn.astype(BF16)
    mixed = []
    for h in range(NH):
        sl = slice(h * HD, (h + 1) * HD)
        mixed.append(_dot(ws_ref[h], vnb[:, sl]) + bst_ref[:, h:h + 1])
    return ug, dug, dvg, rstd, vhat, vnb, jnp.concatenate(mixed, axis=1)


def _gmlp_fwd(proj, ws_b, bst, lnw, lnb):
    T = proj.shape[0]
    rows = min(GMLP_ROWS_PER_STEP, T)

    def body(u_ref, v_ref, ws_ref, bst_ref, lnw_ref, lnb_ref, y_ref):
        for bi in range(rows // BLK):
            rs = slice(bi * BLK, (bi + 1) * BLK)
            ug, _, _, _, _, _, mixed = _gmlp_common(u_ref[rs, :], v_ref[rs, :], lnw_ref[...], lnb_ref[...], ws_ref, bst_ref)
            y_ref[rs, :] = (ug * mixed).astype(BF16)

    return pl.pallas_call(
        body, grid=(T // rows,),
        in_specs=[pl.BlockSpec((rows, DG), lambda i: (i, 0)), pl.BlockSpec((rows, DG), lambda i: (i, 1)),
                  _full((NH, BLK, BLK)), _full((BLK, NH)), _full((1, DG)), _full((1, DG))],
        out_specs=pl.BlockSpec((rows, DG), lambda i: (i, 0)),
        out_shape=SDS((T, D), BF16),
        compiler_params=_arb(), name="gmlp_fwd")(proj, proj, ws_b, bst, lnw, lnb)


def _hgrn_tables():
    t = np.arange(CH)[:, None]
    j = np.arange(CH)[None, :]
    blocks = [j <= t, j > t]
    masks = []
    for n in LEVELS:
        mid = t - t % n + n // 2
        blocks.append(np.where(t >= mid, (j >= mid) & (j <= t), (j > t) & (j < mid)))
        masks.append((t // n == j // n) & (t % n >= n // 2) & (j % n < n // 2))
    w = np.concatenate(blocks, axis=0).astype(np.float32)
    m = np.stack(masks).astype(np.float32)
    return (jnp.asarray(w, BF16), jnp.asarray(w.T, BF16), jnp.asarray(m), jnp.asarray(m + m.transpose(0, 2, 1)))


def _split_dot(w, x, parts):
    acc = None
    for _ in range(parts):
        piece = x.astype(BF16)
        term = _dot(w, piece)
        acc = term if acc is None else acc + term
        x = x - piece.astype(F32)
    return acc


def _hgrn_gates(q, fl, lb, omlb, w_ref):
    sq = _sigmoid(q)
    qf = q * sq
    sig = _sigmoid(fl)
    f = lb + omlb * sig
    k = 1.0 - f
    e = jnp.exp(_split_dot(w_ref[...], jnp.log(f), 3))
    return sq, qf, sig, f, k, e


def _level_factor(e, li, sl, row, qh, kh):
    el = e[(2 + li) * CH:(3 + li) * CH, sl]
    up = (row & (LEVELS[li] // 2)) != 0
    return el, up, el * jnp.where(up, qh, kh)


def _hgrn_fwd(proj, lower_bounds, gn_w, ycat, tables, placed, axes):
    T = proj.shape[0]
    nc = T // CH
    nch = min(HGRN_CHUNKS_PER_STEP, nc)
    steps = nc // nch
    w_st, _, masks, _ = tables
    nw = len(placed)
    pass_step = (5 * steps) // 8

    def body(*refs):
        q_ref, f_ref, i_ref, g_ref, lbp_ref, gn_ref, w_ref, m_ref = refs[:8]
        y_ref, o_ref, a_ref, st_ref = refs[9 + nw:13 + nw]
        s_scr, send_sems, recv_sems = refs[13 + 2 * nw:]
        gather = _WeightGather(refs[13 + nw:13 + 2 * nw], axes, send_sems, recv_sems)
        step = pl.program_id(0)

        @pl.when(step == 0)
        def _():
            gather.start()
            s_scr[...] = jnp.zeros_like(s_scr)

        @pl.when(step == pass_step)
        def _():
            gather.forward()

        lb, omlb = _lower_bound(lbp_ref)
        row = lax.broadcasted_iota(jnp.int32, (CH, 1), 0)
        eye = lax.broadcasted_iota(jnp.int32, (CH, CH), 0) == lax.broadcasted_iota(jnp.int32, (CH, CH), 1)
        pre = []
        for ci in range(nch):
            rs = slice(ci * CH, (ci + 1) * CH)
            _, qf, _, _, k, e = _hgrn_gates(q_ref[rs, :], f_ref[rs, :], lb, omlb, w_ref)
            mats = []
            for h in range(NH):
                sl = slice(h * HD, (h + 1) * HD)
                qh, kh = qf[:, sl], k[:, sl]
                a = jnp.where(eye, jnp.sum(qh * kh, axis=-1, keepdims=True), 0.0)
                for li in range(len(LEVELS)):
                    _, _, y = _level_factor(e, li, sl, row, qh, kh)
                    yb = y.astype(BF16)
                    a = a + m_ref[li] * _dot(yb, yb, NT)
                a_ref[ci, h] = a
                mats.append(a.astype(BF16))
            eb = e[0:CH]
            pre.append(((qf * eb).astype(BF16), eb[CH - 1:CH, :], (k * e[CH:2 * CH]).astype(BF16), mats))
        for ci in range(nch):
            rs = slice(ci * CH, (ci + 1) * CH)
            qe, ebl, kd, mats = pre[ci]
            v = i_ref[rs, :]
            g = g_ref[rs, :]
            for h in range(NH):
                sl = slice(h * HD, (h + 1) * HD)
                st0 = s_scr[h]
                st_ref[ci, h] = st0
                vb = v[:, sl].astype(BF16)
                o = _dot(qe[:, sl], st0.astype(BF16), NT) + _dot(mats[h], vb)
                s_scr[h] = st0 * ebl[:, sl] + _dot(vb, kd[:, sl], TN)
                o_ref[rs, sl] = o
                gh = g[:, sl]
                y_ref[rs, sl] = (((o * _rms(o)) * gn_ref[...]) * (gh * _sigmoid(gh))).astype(BF16)

        @pl.when(step == steps - 1)
        def _():
            gather.finish()

    blk = lambda j: pl.BlockSpec((nch * CH, DH), lambda c: (c, j))
    anyspec = pl.BlockSpec(memory_space=pl.ANY)
    res = pl.pallas_call(
        body, grid=(steps,),
        in_specs=[blk(2), blk(3), blk(4), blk(5), _full((2, DH)), _full((1, HD)),
                  _full(w_st.shape), _full(masks.shape), anyspec] + [anyspec] * nw,
        out_specs=[pl.BlockSpec((nch * CH, DH), lambda c: (c, 1)),
                   pl.BlockSpec((nch * CH, DH), lambda c: (c, 0)),
                   pl.BlockSpec((nch, NH, CH, CH), lambda c: (c, 0, 0, 0)),
                   pl.BlockSpec((nch, NH, HD, HD), lambda c: (c, 0, 0, 0))] + [anyspec] * nw,
        out_shape=[SDS((T, D), BF16), SDS((T, DH), F32), SDS((nc, NH, CH, CH), F32), SDS((nc, NH, HD, HD), F32)]
        + [SDS(a.shape, a.dtype) for a in placed],
        scratch_shapes=[pltpu.VMEM((NH, HD, HD), F32)] + _gather_sems(nw),
        input_output_aliases={8: 0, **{9 + i: 4 + i for i in range(nw)}},
        compiler_params=_arb(), name="hgrn_fwd")(proj, proj, proj, proj, lower_bounds, gn_w, w_st, masks, ycat, *placed)
    return res[:4], res[4:]


def _token_local(x, ycat, tgt, g1, nw2, sc2, sh2, g2, fw, w_out_b, w_fi_b, w_fo_b, tm):
    T = x.shape[0]
    inv_d = 1.0 / D

    def body(x_ref, y_ref, t_ref, g1_ref, nw2_ref, sc2_ref, sh2_ref, g2_ref, fw_ref, wo_ref, wfi_ref, wfo_ref,
             dy_ref, dx1_ref, h2_ref, act_ref, dff_ref, dgu_ref, dmix_ref, acc_ref):
        @pl.when(pl.program_id(0) == 0)
        def _():
            acc_ref[...] = jnp.zeros_like(acc_ref)

        def acc(row, val):
            acc_ref[row:row + 1, :] += jnp.sum(val, axis=0, keepdims=True)

        g1v, g2v = g1_ref[...], g2_ref[...]
        mix = _dot(y_ref[...], wo_ref[...])
        x1 = x_ref[...] + g1v * mix
        r2 = _rms(x1)
        xh2 = x1 * r2
        n2 = xh2 * nw2_ref[...]
        osc2 = 1.0 + sc2_ref[...]
        h2b = (n2 * osc2 + sh2_ref[...]).astype(BF16)
        h2_ref[...] = h2b
        ff = jnp.zeros((tm, D), F32)
        saved = []
        for kb in range(DFF // FFB):
            gate = _dot(h2b, wfi_ref[:, kb * FFB:(kb + 1) * FFB])
            up = _dot(h2b, wfi_ref[:, DFF + kb * FFB:DFF + (kb + 1) * FFB])
            sg = _sigmoid(gate)
            actb = (gate * sg * up).astype(BF16)
            act_ref[:, kb * FFB:(kb + 1) * FFB] = actb
            ff = ff + _dot(actb, wfo_ref[kb * FFB:(kb + 1) * FFB, :])
            saved.append((gate, up, sg))
        x2 = x1 + g2v * ff
        r3 = _rms(x2)
        xh3 = x2 * r3
        err = xh3 * fw_ref[...] - t_ref[...]
        acc(6, (0.5 * inv_d) * err * err)
        dy = err * inv_d
        acc(4, dy * xh3)
        dx2 = _rms_bwd(xh3, r3, dy * fw_ref[...])
        acc(0, dx2 * ff)
        dffb = (dx2 * g2v).astype(BF16)
        dff_ref[...] = dffb
        dh2 = jnp.zeros((tm, D), F32)
        for kb in range(DFF // FFB):
            gate, up, sg = saved[kb]
            da = _dot(dffb, wfo_ref[kb * FFB:(kb + 1) * FFB, :], NT)
            dgate = (da * up * (sg * (1.0 + gate * (1.0 - sg)))).astype(BF16)
            dup = (da * gate * sg).astype(BF16)
            dgu_ref[:, kb * FFB:(kb + 1) * FFB] = dgate
            dgu_ref[:, DFF + kb * FFB:DFF + (kb + 1) * FFB] = dup
            dh2 = dh2 + _dot(dgate, wfi_ref[:, kb * FFB:(kb + 1) * FFB], NT)
            dh2 = dh2 + _dot(dup, wfi_ref[:, DFF + kb * FFB:DFF + (kb + 1) * FFB], NT)
        acc(2, dh2)
        acc(1, dh2 * n2)
        dn2 = dh2 * osc2
        acc(3, dn2 * xh2)
        dx1 = dx2 + _rms_bwd(xh2, r2, dn2 * nw2_ref[...])
        acc(5, dx1 * mix)
        dmixb = (dx1 * g1v).astype(BF16)
        dmix_ref[...] = dmixb
        dy_ref[...] = _dot(dmixb, wo_ref[...], NT)
        dx1_ref[...] = dx1

    row = lambda i: (i, 0)
    vec = _full((1, D))
    return pl.pallas_call(
        body, grid=(T // tm,),
        in_specs=[pl.BlockSpec((tm, D), row), pl.BlockSpec((tm, D), row), pl.BlockSpec((tm, D), row),
                  vec, vec, vec, vec, vec, vec,
                  _resident((D, D)), _resident((D, 2 * DFF)), _resident((DFF, D))],
        out_specs=[pl.BlockSpec((tm, D), row), pl.BlockSpec((tm, D), row), pl.BlockSpec((tm, D), row),
                   pl.BlockSpec((tm, DFF), row), pl.BlockSpec((tm, D), row), pl.BlockSpec((tm, 2 * DFF), row),
                   pl.BlockSpec((tm, D), row), _full((8, D))],
        out_shape=[SDS((T, D), F32), SDS((T, D), F32), SDS((T, D), BF16), SDS((T, DFF), BF16), SDS((T, D), BF16),
                   SDS((T, 2 * DFF), BF16), SDS((T, D), BF16), SDS((8, D), F32)],
        compiler_params=_arb(), name="token_local")(x, ycat, tgt, g1, nw2, sc2, sh2, g2, fw, w_out_b, w_fi_b, w_fo_b)


def _gmlp_bwd(proj, dycat, ws_b, bst, lnw, lnb):
    T = proj.shape[0]
    rows = min(GMLP_ROWS_PER_STEP, T)
    nb = T // rows

    def body(u_ref, v_ref, dy_ref, ws_ref, bst_ref, lnw_ref, lnb_ref, dp_ref, dws_ref, dbs_ref, dln_ref, dbs_acc):
        i = pl.program_id(0)

        @pl.when(i == 0)
        def _():
            dws_ref[...] = jnp.zeros_like(dws_ref)
            dln_ref[...] = jnp.zeros_like(dln_ref)
            dbs_acc[...] = jnp.zeros_like(dbs_acc)

        r = lax.broadcasted_iota(jnp.int32, (BLK, BLK), 0) // CH
        c = lax.broadcasted_iota(jnp.int32, (BLK, BLK), 1) // CH
        for bi in range(rows // BLK):
            rs = slice(bi * BLK, (bi + 1) * BLK)
            ug, dug, dvg, rstd, vhat, vnb, mixed = _gmlp_common(
                u_ref[rs, :], v_ref[rs, :], lnw_ref[...], lnb_ref[...], ws_ref, bst_ref)
            dya = dy_ref[rs, :]
            dp_ref[rs, 0:DG] = (dya * mixed * dug).astype(BF16)
            dmixed = dya * ug
            dbs_acc[...] += dmixed
            dmb = dmixed.astype(BF16)
            dvn = []
            for h in range(NH):
                sl = slice(h * HD, (h + 1) * HD)
                dws_ref[h] += jnp.where(r >= c, _dot(dmb[:, sl], vnb[:, sl], NT), 0.0)
                dvn.append(_dot(ws_ref[h], dmb[:, sl], TN))
            dvn = jnp.concatenate(dvn, axis=1)
            dln_ref[0:1, :] += jnp.sum(dvn * vhat, axis=0, keepdims=True)
            dln_ref[1:2, :] += jnp.sum(dvn, axis=0, keepdims=True)
            dvh = dvn * lnw_ref[...]
            dvgel = rstd * (dvh - jnp.mean(dvh, axis=-1, keepdims=True) - vhat * jnp.mean(dvh * vhat, axis=-1, keepdims=True))
            dp_ref[rs, DG:2 * DG] = (dvgel * dvg).astype(BF16)

        @pl.when(i == nb - 1)
        def _():
            lane = lax.broadcasted_iota(jnp.int32, (BLK, HD), 1)
            out = jnp.zeros((BLK, HD), F32)
            for h in range(NH):
                out = out + jnp.where(lane == h, jnp.sum(dbs_acc[:, h * HD:(h + 1) * HD], axis=-1, keepdims=True), 0.0)
            dbs_ref[...] = out

    return pl.pallas_call(
        body, grid=(nb,),
        in_specs=[pl.BlockSpec((rows, DG), lambda i: (i, 0)), pl.BlockSpec((rows, DG), lambda i: (i, 1)),
                  pl.BlockSpec((rows, DG), lambda i: (i, 0)),
                  _full((NH, BLK, BLK)), _full((BLK, NH)), _full((1, DG)), _full((1, DG))],
        out_specs=[pl.BlockSpec((rows, 2 * DG), lambda i: (i, 2)), _full((NH, BLK, BLK)), _full((BLK, HD)), _full((8, DG))],
        out_shape=[SDS((T, DIN), BF16), SDS((NH, BLK, BLK), F32), SDS((BLK, HD), F32), SDS((8, DG), F32)],
        scratch_shapes=[pltpu.VMEM((BLK, DG), F32)],
        compiler_params=_arb(), name="gmlp_bwd")(proj, proj, dycat, ws_b, bst, lnw, lnb)


def _hgrn_bwd(proj, o_pre, a_all, st_all, dycat, lower_bounds, gn_w, dproj, tables, sums):
    T = proj.shape[0]
    nc = T // CH
    nch = min(HGRN_CHUNKS_PER_STEP, nc)
    steps = nc // nch
    w_st, w_st_t, _, masks_sym = tables
    n_lev = len(LEVELS)
    nw = len(sums)

    def body(*refs):
        q_ref, f_ref, i_ref, g_ref, o_ref, a_ref, st_ref, dy_ref, lbp_ref, gn_ref, w_ref, wt_ref, ms_ref = refs[:13]
        dp_ref, dlb_ref, dgn_ref = refs[14 + nw:17 + nw]
        ds_scr, dx_scr, send_sems, recv_sems = refs[17 + 2 * nw:]
        exchange = _ChipExchange(refs[14:14 + nw], refs[17 + nw:17 + 2 * nw], send_sems, recv_sems)
        i = pl.program_id(0)

        @pl.when(i == 0)
        def _():
            exchange.start()
            ds_scr[...] = jnp.zeros_like(ds_scr)
            dlb_ref[...] = jnp.zeros_like(dlb_ref)
            dgn_ref[...] = jnp.zeros_like(dgn_ref)

        lb, omlb = _lower_bound(lbp_ref)
        row = lax.broadcasted_iota(jnp.int32, (CH, 1), 0)
        eye = lax.broadcasted_iota(jnp.int32, (CH, CH), 0) == lax.broadcasted_iota(jnp.int32, (CH, CH), 1)
        lower = lax.broadcasted_iota(jnp.int32, (CH, CH), 0) > lax.broadcasted_iota(jnp.int32, (CH, CH), 1)
        dgn = jnp.zeros((1, HD), F32)
        pre = []
        for ci in range(nch):
            rs = slice(ci * CH, (ci + 1) * CH)
            q = q_ref[rs, :]
            v = i_ref[rs, :]
            g = g_ref[rs, :]
            sq, qf, sig, f, k, e = _hgrn_gates(q, f_ref[rs, :], lb, omlb, w_ref)
            eb = e[0:CH]
            ekd = e[CH:2 * CH]
            kd = k * ekd
            qe = qf * eb
            dob_h, dqe_h, dqf_h, dki_h, dv_h, dg_h = [], [], [], [], [], []
            for h in range(NH):
                sl = slice(h * HD, (h + 1) * HD)
                o = o_ref[rs, sl]
                ro = _rms(o)
                oh = o * ro
                gh = g[:, sl]
                sg = _sigmoid(gh)
                dyb = dy_ref[rs, sl]
                dg_h.append(dyb * (oh * gn_ref[...]) * (sg * (1.0 + gh * (1.0 - sg))))
                don = dyb * (gh * sg)
                dgn = dgn + jnp.sum(don * oh, axis=0, keepdims=True)
                dob = _rms_bwd(oh, ro, don * gn_ref[...]).astype(BF16)
                vb = v[:, sl].astype(BF16)
                qh, kh = qf[:, sl], k[:, sl]
                dqe = _dot(dob, st_ref[ci, h].astype(BF16))
                da = _dot(dob, vb, NT)
                ddiag = jnp.sum(jnp.where(eye, da, 0.0), axis=-1, keepdims=True)
                dsym = jnp.where(lower, da, _dot(vb, dob, NT))
                upper_part = jnp.zeros((CH, HD), F32)
                both = jnp.zeros((CH, HD), F32)
                for li in range(n_lev):
                    el, up, y = _level_factor(e, li, sl, row, qh, kh)
                    dyv = _dot((ms_ref[li] * dsym).astype(BF16), y.astype(BF16))
                    dx_scr[ci, (2 + li) * CH:(3 + li) * CH, sl] = dyv * y
                    dye = dyv * el
                    upper_part = upper_part + jnp.where(up, dye, 0.0)
                    both = both + dye
                dob_h.append(dob)
                dqe_h.append(dqe)
                dqf_h.append(dqe * eb[:, sl] + ddiag * kh + upper_part)
                dki_h.append(ddiag * qh + (both - upper_part))
                dv_h.append(_dot(a_ref[ci, h].astype(BF16), dob, TN))
            dp_ref[rs, 0:DH] = (jnp.concatenate(dqf_h, axis=1) * (sq * (1.0 + q * (1.0 - sq)))).astype(BF16)
            dp_ref[rs, 3 * DH:4 * DH] = jnp.concatenate(dg_h, axis=1).astype(BF16)
            pre.append((v, sig, f, eb, ekd, kd, qe, dob_h, jnp.concatenate(dqe_h, axis=1), dki_h, dv_h))
        dgn_ref[0:1, :] += dgn
        for ci in reversed(range(nch)):
            rs = slice(ci * CH, (ci + 1) * CH)
            v, sig, f, eb, ekd, kd, qe, dob_h, dqe, dki_h, dv_h = pre[ci]
            ebl = eb[CH - 1:CH, :]
            dbl_h, dkd_h, dv2_h = [], [], []
            for h in range(NH):
                sl = slice(h * HD, (h + 1) * HD)
                dst1 = ds_scr[h]
                dst1b = dst1.astype(BF16)
                ds_scr[h] = dst1 * ebl[:, sl] + _dot(dob_h[h], qe[:, sl].astype(BF16), TN)
                dbl_h.append(ebl[:, sl] * jnp.sum(st_ref[ci, h] * dst1, axis=0, keepdims=True))
                dkd_h.append(_dot(v[:, sl].astype(BF16), dst1b))
                dv2_h.append(dv_h[h] + _dot(kd[:, sl].astype(BF16), dst1b, NT))
            dkd = jnp.concatenate(dkd_h, axis=1)
            dx_scr[ci, 0:CH, :] = dqe * qe + jnp.where(row == CH - 1, jnp.concatenate(dbl_h, axis=1), 0.0)
            dx_scr[ci, CH:2 * CH, :] = dkd * kd
            dlf = _split_dot(wt_ref[...], dx_scr[ci], 2)
            df = dlf / f - (dkd * ekd + jnp.concatenate(dki_h, axis=1))
            dlb_ref[0:1, :] += jnp.sum(df * (1.0 - sig), axis=0, keepdims=True)
            dp_ref[rs, DH:2 * DH] = (df * omlb * sig * (1.0 - sig)).astype(BF16)
            dp_ref[rs, 2 * DH:3 * DH] = jnp.concatenate(dv2_h, axis=1).astype(BF16)

        @pl.when(i == steps - 1)
        def _():
            gl = dlb_ref[0:1, :] * lb * omlb
            dlb_ref[0:1, :] = gl
            dlb_ref[1:2, :] = -gl
            exchange.finish()

    rev = lambda j: pl.BlockSpec((nch * CH, DH), lambda c: (steps - 1 - c, j))
    anyspec = pl.BlockSpec(memory_space=pl.ANY)
    res = pl.pallas_call(
        body, grid=(steps,),
        in_specs=[rev(2), rev(3), rev(4), rev(5), rev(0),
                  pl.BlockSpec((nch, NH, CH, CH), lambda c: (steps - 1 - c, 0, 0, 0)),
                  pl.BlockSpec((nch, NH, HD, HD), lambda c: (steps - 1 - c, 0, 0, 0)),
                  rev(1), _full((2, DH)), _full((1, HD)),
                  _full(w_st.shape), _full(w_st_t.shape), _full(masks_sym.shape),
                  anyspec] + [anyspec] * nw,
        out_specs=[pl.BlockSpec((nch * CH, 4 * DH), lambda c: (steps - 1 - c, 0)), _full((8, DH)), _full((8, HD))]
        + [anyspec] * nw,
        out_shape=[SDS((T, DIN), BF16), SDS((8, DH), F32), SDS((8, HD), F32)] + _slot_shapes(sums),
        scratch_shapes=[pltpu.VMEM((NH, HD, HD), F32), pltpu.VMEM((nch, (2 + n_lev) * CH, DH), F32)] + _exchange_sems(nw),
        input_output_aliases={13: 0},
        compiler_params=_arb(), name="hgrn_bwd")(proj, proj, proj, proj, o_pre, a_all, st_all, dycat, lower_bounds, gn_w,
                                                 w_st, w_st_t, masks_sym, dproj, *sums)
    return res[:3], res[3:]


def _proj_in_bwd(dproj, x, dx1, nw, sc, w_in_b, tm):
    T = x.shape[0]

    def body(dp_ref, x_ref, dx1_ref, nw_ref, sc_ref, w_ref, gx_ref, acc_ref):
        @pl.when(pl.program_id(0) == 0)
        def _():
            acc_ref[...] = jnp.zeros_like(acc_ref)

        dh = _dot(dp_ref[:, 0:4 * DH], w_ref[:, 2 * DG:DIN], NT) + _dot(dp_ref[:, 4 * DH:DIN], w_ref[:, 0:2 * DG], NT)
        xv = x_ref[...]
        r = _rms(xv)
        xh = xv * r
        n1 = xh * nw_ref[...]
        acc_ref[0:1, :] += jnp.sum(dh, axis=0, keepdims=True)
        acc_ref[1:2, :] += jnp.sum(dh * n1, axis=0, keepdims=True)
        dn = dh * (1.0 + sc_ref[...])
        acc_ref[2:3, :] += jnp.sum(dn * xh, axis=0, keepdims=True)
        gx_ref[...] = dx1_ref[...] + _rms_bwd(xh, r, dn * nw_ref[...])

    row = lambda i: (i, 0)
    return pl.pallas_call(
        body, grid=(T // tm,),
        in_specs=[pl.BlockSpec((tm, DIN), row), pl.BlockSpec((tm, D), row), pl.BlockSpec((tm, D), row),
                  _full((1, D)), _full((1, D)), _resident((D, DIN))],
        out_specs=[pl.BlockSpec((tm, D), row), _full((8, D))],
        out_shape=[SDS((T, D), F32), SDS((8, D), F32)],
        compiler_params=_arb(), name="proj_in_bwd")(dproj, x, dx1, nw, sc, w_in_b)


def _wgrad(a, b, bk, bn, tt, name):
    T, K = a.shape
    N = b.shape[1]
    nn, nk, nt = N // bn, K // bk, T // tt
    bmap = lambda n, k, t: (t, n)

    def body(a_ref, b_ref, o_ref):
        @pl.when(pl.program_id(2) == 0)
        def _():
            o_ref[...] = jnp.zeros_like(o_ref)

        o_ref[0] += _dot(a_ref[...], b_ref[...], TN)

    return pl.pallas_call(
        body, grid=(nn, nk, nt),
        in_specs=[pl.BlockSpec((tt, bk), lambda n, k, t: (t, k)), pl.BlockSpec((tt, bn), bmap)],
        out_specs=pl.BlockSpec((1, bk, bn), lambda n, k, t: (n, k, 0)),
        out_shape=SDS((nn, K, bn), F32),
        compiler_params=_arb(3), name=name)(a, b)


def _adam_math(w, g, m, v):
    m = B1 * m + (1.0 - B1) * g
    v = B2 * v + (1.0 - B2) * (g * g)
    m_hat = m / (1.0 - B1 ** STEP)
    v_hat = v / (1.0 - B2 ** STEP)
    return -LR * (m_hat / (jnp.sqrt(v_hat) + AEPS) + WD * w), m, v


def _adamw_halves(w, mine, sibling, m, v, c_idx, rb, name):
    R, C = w.shape
    nb = (R // 2) // rb

    def body(c_ref, w_ref, a_ref, b_ref, m_ref, v_ref, g_out, d_out, m_out, v_out):
        g = jnp.where(pl.program_id(0) == c_ref[0], a_ref[...], b_ref[...])
        g_out[...] = g
        d_out[...], m_out[...], v_out[...] = _adam_math(w_ref[...], g, m_ref[...], v_ref[...])

    whole = pl.BlockSpec((rb, C), lambda hh, i, cr: (hh * nb + i, 0))
    half = pl.BlockSpec((rb, C), lambda hh, i, cr: (i, 0))
    return pl.pallas_call(
        body,
        grid_spec=pltpu.PrefetchScalarGridSpec(
            num_scalar_prefetch=1, grid=(2, nb), in_specs=[whole, half, half, whole, whole], out_specs=[whole] * 4),
        out_shape=[SDS((R, C), F32)] * 4, compiler_params=_arb(2), name=name)(c_idx, w, mine, sibling, m, v)


def _ada_forward(c_all, w_ada):
    n = w_ada.shape[1]

    def body(c_ref, w_ref, ca_ref, p_ref):
        cv = c_ref[...]
        ca = cv * _sigmoid(cv)
        ca_ref[...] = ca
        p_ref[...] = _dot(ca, w_ref[...], precision=HIGHEST)

    return pl.pallas_call(
        body, grid=(n // 512,),
        in_specs=[_full((N_DEV, D)), pl.BlockSpec((D, 512), lambda i: (0, i))],
        out_specs=[_full((N_DEV, D)), pl.BlockSpec((N_DEV, 512), lambda i: (0, i))],
        out_shape=[SDS((N_DEV, D), F32), SDS((N_DEV, n), F32)],
        compiler_params=_arb(), name="ada_forward")(c_all, w_ada)


def _ada_wgrad_adam(cact_t, dada, w, m, v):
    R, C = w.shape
    rb = 256

    def body(c_ref, d_ref, w_ref, m_ref, v_ref, g_out, d_out, m_out, v_out):
        g = _dot(c_ref[...], d_ref[...], precision=HIGHEST)
        g_out[...] = g
        d_out[...], m_out[...], v_out[...] = _adam_math(w_ref[...], g, m_ref[...], v_ref[...])

    spec = pl.BlockSpec((rb, C), lambda i: (i, 0))
    return pl.pallas_call(
        body, grid=(R // rb,),
        in_specs=[pl.BlockSpec((rb, N_DEV), lambda i: (i, 0)), _full((N_DEV, C)), spec, spec, spec],
        out_specs=[spec] * 4, out_shape=[SDS((R, C), F32)] * 4,
        compiler_params=_arb(), name="ada_wgrad_adam")(cact_t, dada, w, m, v)


def _small_finalize(gathered, w, m, v):
    def body(ga_ref, w_ref, m_ref, v_ref, g_out, d_out, m_out, v_out):
        g = ga_ref[0:SMALL_ROWS, :]
        for dev in range(1, N_DEV):
            g = g + ga_ref[dev * SMALL_ROWS:(dev + 1) * SMALL_ROWS, :]
        g_out[...] = g
        d_out[...], m_out[...], v_out[...] = _adam_math(w_ref[...], g, m_ref[...], v_ref[...])

    return pl.pallas_call(
        body, out_shape=[SDS((SMALL_ROWS, D), F32)] * 4, name="small_finalize")(gathered, w, m, v)


def _position():
    x, y, c = lax.axis_index("x"), lax.axis_index("y"), lax.axis_index("c")
    return x, y, c


def _chip_at(x, y, r):
    return (x ^ (r >> 1), y ^ (r & 1))


def _all_gather_rows(block, name):
    m_per, n = block.shape

    def body(x_ref, out_ref, send_sems, recv_sems, local_sem):
        x, y, c = _position()
        me, sibling = (x, y, c), (x, y, 1 - c)
        chips = [_chip_at(x, y, r) for r in (1, 2, 3)]

        def rows(px, py, pc):
            return out_ref.at[pl.ds((4 * px + 2 * py + pc) * m_per, m_per), :]

        def copy(k, blk, to, src=None):
            return pltpu.make_async_remote_copy(
                src_ref=rows(*blk) if src is None else src, dst_ref=rows(*blk),
                send_sem=send_sems.at[k], recv_sem=recv_sems.at[k], device_id=to, device_id_type=MESH)

        mine = pltpu.make_async_copy(x_ref, rows(*me), local_sem)
        mine.start()
        first = [copy(0, me, sibling, src=x_ref)]
        first += [copy(1 + j, me, (*chip, c), src=x_ref) for j, chip in enumerate(chips)]
        for cp in first:
            cp.start()
        passed = [copy(4 + j, (*chip, c), sibling) for j, chip in enumerate(chips)]
        for j, chip in enumerate(chips):
            copy(1 + j, (*chip, c), me).wait_recv()
            passed[j].start()
        copy(0, sibling, me).wait_recv()
        for j, chip in enumerate(chips):
            copy(4 + j, (*chip, 1 - c), me).wait_recv()
        for cp in first + passed:
            cp.wait_send()
        mine.wait()

    return pl.pallas_call(
        body, out_shape=SDS((N_DEV * m_per, n), block.dtype),
        in_specs=[pl.BlockSpec(memory_space=pltpu.VMEM)], out_specs=pl.BlockSpec(memory_space=pltpu.VMEM),
        scratch_shapes=[pltpu.SemaphoreType.DMA((7,)), pltpu.SemaphoreType.DMA((7,)), pltpu.SemaphoreType.DMA],
        name=name)(block)


def _place_shard(w_shard, axis, chip_idx, name):
    R, C = w_shard.shape
    rb = _row_block(R)
    nb = R // rb
    full = (R * N_CHIPS, C) if axis == 0 else (R, C * N_CHIPS)
    omap = (lambda i, j: (j[0] * nb + i, 0)) if axis == 0 else (lambda i, j: (i, j[0]))

    def body(j_ref, w_ref, o_ref):
        o_ref[...] = w_ref[...].astype(BF16)

    return pl.pallas_call(
        body,
        grid_spec=pltpu.PrefetchScalarGridSpec(
            num_scalar_prefetch=1, grid=(nb,), in_specs=[pl.BlockSpec((rb, C), lambda i, j: (i, 0))],
            out_specs=pl.BlockSpec((rb, C), omap)),
        out_shape=SDS(full, BF16), compiler_params=_arb(), name=name)(chip_idx, w_shard)


class _WeightGather:
    def __init__(self, refs, axes, send_sems, recv_sems):
        self.refs, self.axes, self.send_sems, self.recv_sems = refs, axes, send_sems, recv_sems
        self.x, self.y, self.c = _position()
        self.j = 2 * self.x + self.y
        self.n = 3 * len(refs)

    def _half(self, w, chip_idx, half):
        ref, axis = self.refs[w], self.axes[w]
        if axis == 0:
            size = ref.shape[0] // N_CHIPS
            return ref.at[pl.ds(chip_idx * size + half * (size // 2), size // 2), :]
        size = ref.shape[1] // N_CHIPS
        rows = ref.shape[0] // 2
        return ref.at[pl.ds(half * rows, rows), pl.ds(chip_idx * size, size)]

    def _ici(self, w, r, chip_idx):
        k = 3 * w + r - 1
        piece = self._half(w, chip_idx, self.c)
        return pltpu.make_async_remote_copy(
            src_ref=piece, dst_ref=piece, send_sem=self.send_sems.at[k], recv_sem=self.recv_sems.at[k],
            device_id=(*_chip_at(self.x, self.y, r), self.c), device_id_type=MESH)

    def _d2d(self, w, r, half):
        k = self.n + 3 * w + r - 1
        piece = self._half(w, self.j ^ r, half)
        return pltpu.make_async_remote_copy(
            src_ref=piece, dst_ref=piece, send_sem=self.send_sems.at[k], recv_sem=self.recv_sems.at[k],
            device_id=(self.x, self.y, 1 - self.c), device_id_type=MESH)

    def _each(self):
        return [(w, r) for w in range(len(self.refs)) for r in (1, 2, 3)]

    def start(self):
        for w, r in self._each():
            self._ici(w, r, self.j).start()

    def forward(self):
        for w, r in self._each():
            self._ici(w, r, self.j ^ r).wait_recv()
            self._d2d(w, r, self.c).start()

    def finish(self):
        for w, r in self._each():
            self._ici(w, r, self.j).wait_send()
            self._d2d(w, r, self.c).wait_send()
            self._d2d(w, r, 1 - self.c).wait_recv()


def _gather_sems(n_weights):
    return [pltpu.SemaphoreType.DMA((6 * n_weights,)), pltpu.SemaphoreType.DMA((6 * n_weights,))]


def _gather_weights(placed, axes, name):
    nw = len(placed)

    def body(*refs):
        outs = refs[nw:2 * nw]
        g = _WeightGather(outs, axes, *refs[2 * nw:])
        g.start()
        g.forward()
        g.finish()

    anyspec = pl.BlockSpec(memory_space=pl.ANY)
    return pl.pallas_call(
        body, out_shape=[SDS(a.shape, a.dtype) for a in placed], in_specs=[anyspec] * nw, out_specs=[anyspec] * nw,
        scratch_shapes=_gather_sems(nw), input_output_aliases={i: i for i in range(nw)},
        name=name)(*placed)


class _ChipExchange:
    def __init__(self, ins, outs, send_sems, recv_sems):
        self.ins, self.outs, self.send_sems, self.recv_sems = ins, outs, send_sems, recv_sems
        self.x, self.y, self.c = _position()
        self.j = 2 * self.x + self.y

    def _copies(self):
        for w in range(len(self.ins)):
            for r in (1, 2, 3):
                k = 3 * w + r - 1
                yield pltpu.make_async_remote_copy(
                    src_ref=self.ins[w].at[self.j ^ r], dst_ref=self.outs[w].at[r - 1],
                    send_sem=self.send_sems.at[k], recv_sem=self.recv_sems.at[k],
                    device_id=(*_chip_at(self.x, self.y, r), self.c), device_id_type=MESH)

    def start(self):
        for cp in self._copies():
            cp.start()

    def finish(self):
        for cp in self._copies():
            cp.wait()


def _exchange_sems(n_weights):
    return [pltpu.SemaphoreType.DMA((3 * n_weights,)), pltpu.SemaphoreType.DMA((3 * n_weights,))]


def _exchange_core_halves(grads, name):
    nw = len(grads)

    def body(*refs):
        ins, outs = refs[:nw], refs[nw:2 * nw]
        send_sems, recv_sems = refs[2 * nw:]
        x, y, c = _position()
        cps = []
        for w in range(nw):
            cp = pltpu.make_async_remote_copy(
                src_ref=ins[w].at[:, 1 - c], dst_ref=outs[w], send_sem=send_sems.at[w], recv_sem=recv_sems.at[w],
                device_id=(x, y, 1 - c), device_id_type=MESH)
            cp.start()
            cps.append(cp)
        for cp in cps:
            cp.wait()

    anyspec = pl.BlockSpec(memory_space=pl.ANY)
    return pl.pallas_call(
        body, out_shape=[SDS((g.shape[0], g.shape[2], g.shape[3]), F32) for g in grads],
        in_specs=[anyspec] * nw, out_specs=[anyspec] * nw,
        scratch_shapes=[pltpu.SemaphoreType.DMA((nw,)), pltpu.SemaphoreType.DMA((nw,))],
        name=name)(*grads)


def _add_core_halves(g4, recv, c_idx, rb, name):
    ns, _, rh, C = g4.shape

    def body(c_ref, g_ref, r_ref, o_ref):
        o_ref[...] = (g_ref[0] + r_ref[...]).astype(BF16)

    return pl.pallas_call(
        body,
        grid_spec=pltpu.PrefetchScalarGridSpec(
            num_scalar_prefetch=1, grid=(ns, rh // rb),
            in_specs=[pl.BlockSpec((1, 1, rb, C), lambda s, i, cr: (s, cr[0], i, 0)),
                      pl.BlockSpec((1, rb, C), lambda s, i, cr: (s, i, 0))],
            out_specs=pl.BlockSpec((1, rb, C), lambda s, i, cr: (s, i, 0))),
        out_shape=SDS((ns, rh, C), BF16), compiler_params=_arb(2), name=name)(c_idx, g4, recv)


def _slot_shapes(sums):
    return [SDS((3,) + s.shape[1:], s.dtype) for s in sums]


def _exchange_chips(sums, name):
    nw = len(sums)

    def body(*refs):
        ex = _ChipExchange(refs[:nw], refs[nw:2 * nw], *refs[2 * nw:])
        ex.start()
        ex.finish()

    anyspec = pl.BlockSpec(memory_space=pl.ANY)
    return pl.pallas_call(
        body, out_shape=_slot_shapes(sums), in_specs=[anyspec] * nw, out_specs=[anyspec] * nw,
        scratch_shapes=_exchange_sems(nw), name=name)(*sums)


def _add_chips(own, slots, order, rb, name):
    _, rh, C = slots.shape

    def body(o_ref, own_ref, a_ref, b_ref, c_ref, d_ref, out_ref):
        mine = own_ref[0].astype(F32)
        t = [jnp.where(o_ref[i] == 0, mine, r[0].astype(F32)) for i, r in enumerate((a_ref, b_ref, c_ref, d_ref))]
        out_ref[...] = ((t[0] + t[1]) + t[2]) + t[3]

    def spec(i):
        return pl.BlockSpec((1, rb, C), lambda t, o: (jnp.maximum(o[i], 1) - 1, t, 0))

    return pl.pallas_call(
        body,
        grid_spec=pltpu.PrefetchScalarGridSpec(
            num_scalar_prefetch=1, grid=(rh // rb,),
            in_specs=[pl.BlockSpec((1, rb, C), lambda t, o: (o[4], t, 0)), spec(0), spec(1), spec(2), spec(3)],
            out_specs=pl.BlockSpec((rb, C), lambda t, o: (t, 0))),
        out_shape=SDS((rh, C), F32), compiler_params=_arb(), name=name)(order, own, slots, slots, slots, slots)


def _share_halves(halves):
    nw = len(halves)

    def body(*refs):
        ins, outs = refs[:nw], refs[nw:2 * nw]
        send_sems, recv_sems = refs[2 * nw:]
        x, y, c = _position()
        started = []
        for w in range(nw):
            cp = pltpu.make_async_remote_copy(
                src_ref=ins[w], dst_ref=outs[w], send_sem=send_sems.at[w], recv_sem=recv_sems.at[w],
                device_id=(x, y, 1 - c), device_id_type=MESH)
            cp.start()
            started.append(cp)
        for cp in started:
            cp.wait()

    anyspec = pl.BlockSpec(memory_space=pl.ANY)
    return pl.pallas_call(
        body, out_shape=[SDS(h.shape, F32) for h in halves], in_specs=[anyspec] * nw, out_specs=[anyspec] * nw,
        scratch_shapes=[pltpu.SemaphoreType.DMA((nw,)), pltpu.SemaphoreType.DMA((nw,))],
        name="share_halves")(*halves)


def _pack_small(b_ada, norm1_w, norm2_w, final_norm_w, v_ln_w, v_ln_b, lower_bounds, b_s, gn_w, w_s):
    parts = [b_ada, norm1_w, norm2_w, final_norm_w, v_ln_w, v_ln_b, lower_bounds, b_s, gn_w,
             jnp.zeros((D - NH * BLK - HD,), F32), w_s, jnp.zeros(((SMALL_ROWS - 76) * D,), F32)]
    return jnp.concatenate([p.reshape(-1) for p in parts]).reshape(SMALL_ROWS, D)


def _unpack_small(p):
    return dict(
        b_ada=p[0:6].reshape(1, 6 * D), norm1_w=p[6:7], norm2_w=p[7:8], final_norm_w=p[8],
        v_ln_w=p[9:10, 0:DG], v_ln_b=p[9:10, DG:D], lower_bounds=p[10].reshape(2, DH),
        b_s=p[11, 0:NH * BLK].reshape(1, NH, BLK), gn_w=p[11:12, NH * BLK:NH * BLK + HD],
        w_s=p[12:76].reshape(1, NH, BLK, BLK))


def _row_block(r):
    for cand in (256, 176, 128, 64, 32, 16, 8):
        if r % cand == 0:
            return cand
    return r


def kernel(x, c, w_ada, b_ada, norm1_w, w_in, w_s, b_s, v_ln_w, v_ln_b, lower_bounds, gn_w, w_out, norm2_w, w_ffn_in, w_ffn_out, final_norm_w, loss_target, m_w_ada, m_b_ada, m_norm1_w, m_w_in, m_w_s, m_b_s, m_v_ln_w, m_v_ln_b, m_lower_bounds, m_gn_w, m_w_out, m_norm2_w, m_w_ffn_in, m_w_ffn_out, m_final_norm_w, v_w_ada, v_b_ada, v_norm1_w, v_w_in, v_w_s, v_b_s, v_v_ln_w, v_v_ln_b, v_lower_bounds, v_gn_w, v_w_out, v_norm2_w, v_w_ffn_in, v_w_ffn_out, v_final_norm_w):
    T = x.shape[1]
    tm, tp = min(TOKEN_TILE, T), min(PROJ_TILE, T)
    px, py, pc = _position()
    chip = 2 * px + py
    me = 4 * px + 2 * py + pc
    x2d = x.reshape(T, D)
    tgt = loss_target.reshape(T, D)

    c_all = _all_gather_rows(jnp.broadcast_to(c, (8, D)), "gather_c").reshape(N_DEV, 8, D)[:, 0, :]
    cact, ada_part = _ada_forward(c_all, w_ada[0])
    n_ada = ada_part.shape[1]
    ada_all = _all_gather_rows(ada_part, "gather_ada").reshape(N_CHIPS, 2, N_DEV, n_ada)[:, 0]
    ada = lax.dynamic_index_in_dim(ada_all, me, axis=1, keepdims=False).reshape(1, 6 * D) + b_ada
    sh1, sc1, g1, sh2, sc2, g2 = [ada[:, i * D:(i + 1) * D] for i in range(6)]

    chip_idx = jnp.reshape(chip, (1,)).astype(jnp.int32)
    c_idx = jnp.reshape(pc, (1,)).astype(jnp.int32)
    (w_in_b,) = _gather_weights([_place_shard(w_in[0], 1, chip_idx, "place_in")], [1], "gather_w_in")
    placed = [_place_shard(w_out[0], 0, chip_idx, "place_out"), _place_shard(w_ffn_in[0], 1, chip_idx, "place_ffn_in"),
              _place_shard(w_ffn_out[0], 0, chip_idx, "place_ffn_out")]

    rr = lax.broadcasted_iota(jnp.int32, (BLK, BLK), 0) // CH
    cc = lax.broadcasted_iota(jnp.int32, (BLK, BLK), 1) // CH
    ws_b = jnp.where((rr >= cc)[None], w_s[0], 0.0).astype(BF16)
    bst = b_s[0].T
    lnw, lnb = v_ln_w, v_ln_b
    nw1, nw2, fw = norm1_w, norm2_w, final_norm_w.reshape(1, D)

    h1, proj = _proj_in(x2d, nw1, sc1, sh1, w_in_b, tp)
    ycat = _gmlp_fwd(proj, ws_b, bst, lnw, lnb)
    tables = _hgrn_tables()
    (ycat, o_pre, a_all, st_all), (w_out_b, w_fi_b, w_fo_b) = _hgrn_fwd(
        proj, lower_bounds, gn_w, ycat, tables, placed, [0, 1, 0])

    dycat, dx1, h2, act, dff, dgu, dmix, acc2 = _token_local(
        x2d, ycat, tgt, g1, nw2, sc2, sh2, g2, fw, w_out_b, w_fi_b, w_fo_b, tm)

    tt = min(WGRAD_TOKENS, T)
    order = jnp.concatenate([chip ^ jnp.arange(N_CHIPS, dtype=jnp.int32), chip_idx]).astype(jnp.int32)

    def core_pair_sums(grads, names, tag):
        g4 = [g.reshape(N_CHIPS, 2, g.shape[1] // 2, g.shape[2]) for g in grads]
        recv = _exchange_core_halves(g4, "exchange_core_halves_" + tag)
        return [_add_core_halves(a, b, c_idx, _row_block(a.shape[2]), "add_core_" + n) for a, b, n in zip(g4, recv, names)]

    def chip_sums(sums, slots, names):
        return [_add_chips(o, s, order, _row_block(s.shape[1]), "add_chips_" + n) for o, s, n in zip(sums, slots, names)]

    g_out = _wgrad(ycat, dmix, D, D, tt, "wgrad_out").reshape(N_CHIPS, D // N_CHIPS, D)
    g_fi = _wgrad(h2, dgu, D, FFB, tt, "wgrad_ffn_in")
    g_fo = _wgrad(act, dff, FFB, D, tt, "wgrad_ffn_out").reshape(N_CHIPS, DFF // N_CHIPS, D)
    late_names = ["out", "ffn_in", "ffn_out"]
    late_sums = core_pair_sums([g_out, g_fi, g_fo], late_names, "ffn")

    dproj, dws, dbs, dln = _gmlp_bwd(proj, dycat, ws_b, bst, lnw, lnb)
    (dproj, dlb, dgn), late_slots = _hgrn_bwd(
        proj, o_pre, a_all, st_all, dycat, lower_bounds, gn_w, dproj, tables, late_sums)
    grad_x, acc1 = _proj_in_bwd(dproj, x2d, dx1, nw1, sc1, w_in_b, tp)

    g_in = _wgrad(h1, dproj, D, D, tt, "wgrad_in")
    g_in = jnp.concatenate([g_in[2], g_in[0], g_in[1]], axis=1).reshape(D, N_CHIPS, DIN // N_CHIPS).transpose(1, 0, 2)
    in_sums = core_pair_sums([g_in], ["in"], "in")
    in_slots = _exchange_chips(in_sums, "exchange_chips_in")
    names = ["in"] + late_names
    halves = chip_sums(in_sums, in_slots, ["in"]) + chip_sums(late_sums, late_slots, late_names)
    sibling_halves = _share_halves(halves)

    big_w = [(w_in, m_w_in, v_w_in), (w_out, m_w_out, v_w_out), (w_ffn_in, m_w_ffn_in, v_w_ffn_in),
             (w_ffn_out, m_w_ffn_out, v_w_ffn_out)]
    big_out = []
    for mine, sib, (w, m, v), n in zip(halves, sibling_halves, big_w, names):
        res = _adamw_halves(w[0], mine, sib, m[0], v[0], c_idx, _row_block(mine.shape[0]), "adamw_" + n)
        big_out.append([r[None] for r in res])

    d_ada = jnp.stack([acc1[0], acc1[1], acc2[5], acc2[2], acc2[1], acc2[0]]).reshape(1, 6 * D)
    small_g = _pack_small(d_ada, acc1[2], acc2[3], acc2[4], dln[0], dln[1], dlb[0:2], dbs[:, 0:NH].T, dgn[0], dws)
    gathered = _all_gather_rows(small_g, "gather_small")
    sw = _pack_small(b_ada, norm1_w, norm2_w, final_norm_w, v_ln_w, v_ln_b, lower_bounds, b_s, gn_w, w_s)
    sm = _pack_small(m_b_ada, m_norm1_w, m_norm2_w, m_final_norm_w, m_v_ln_w, m_v_ln_b, m_lower_bounds, m_b_s, m_gn_w, m_w_s)
    sv = _pack_small(v_b_ada, v_norm1_w, v_norm2_w, v_final_norm_w, v_v_ln_w, v_v_ln_b, v_lower_bounds, v_b_s, v_gn_w, v_w_s)
    small = [_unpack_small(p) for p in _small_finalize(gathered, sw, sm, sv)]

    dada_all = gathered.reshape(N_DEV, SMALL_ROWS, D)[:, 0:6, :].reshape(N_DEV, 6 * D)
    dada = lax.dynamic_slice_in_dim(dada_all, chip * n_ada, n_ada, axis=1)
    ada_out = [o[None] for o in _ada_wgrad_adam(cact.T, dada, w_ada[0], m_w_ada[0], v_w_ada[0])]

    loss = lax.psum(jnp.sum(acc2[6]), ("x", "y", "c"))

    order_names = ['w_ada', 'b_ada', 'norm1_w', 'w_in', 'w_s', 'b_s', 'v_ln_w', 'v_ln_b', 'lower_bounds', 'gn_w',
                   'w_out', 'norm2_w', 'w_ffn_in', 'w_ffn_out', 'final_norm_w']
    big_idx = {'w_in': 0, 'w_out': 1, 'w_ffn_in': 2, 'w_ffn_out': 3}
    outs = [loss, grad_x.reshape(1, T, D)]
    for kind in range(4):
        for n in order_names:
            if n == 'w_ada':
                outs.append(ada_out[kind])
            elif n in big_idx:
                outs.append(big_out[big_idx[n]][kind])
            else:
                outs.append(small[kind][n])
    return tuple(outs)
```

```python
import functools

import jax
import jax.numpy as jnp
import numpy as np
from jax import lax
from jax.experimental import pallas as pl
from jax.experimental.pallas import tpu as pltpu

F32 = jnp.float32
BF16 = jnp.bfloat16
SDS = jax.ShapeDtypeStruct
MESH = pl.DeviceIdType.MESH
HIGHEST = lax.Precision.HIGHEST

D = 1024
DG = 512
DH = 512
NH = 4
HD = 128
BLK = 128
CH = 64
DFF = 2816
DIN = 3072
FFB = 1408
LEVELS = (64, 32, 16, 8, 4, 2)
HGRN_CHUNKS_PER_STEP = 4
GMLP_ROWS_PER_STEP = 512
TOKEN_TILE = 256
PROJ_TILE = 512
WGRAD_TOKENS = 2048
N_CHIPS = 4
N_DEV = 8
EPS = 1e-6
LR, B1, B2, AEPS, WD, STEP = 0.001, 0.9, 0.999, 1e-08, 0.01, 10
SMALL_ROWS = 80

NT = (((1,), (1,)), ((), ()))
TN = (((0,), (0,)), ((), ()))


def _full(shape):
    nd = len(shape)
    return pl.BlockSpec(shape, lambda *_: (0,) * nd)


def _resident(shape):
    nd = len(shape)
    return pl.BlockSpec(shape, lambda *_: (0,) * nd, pipeline_mode=pl.Buffered(1))


def _arb(n=1):
    return pltpu.CompilerParams(dimension_semantics=("arbitrary",) * n)


def _dot(a, b, dims=None, precision=None):
    if dims is None:
        return jnp.dot(a, b, preferred_element_type=F32, precision=precision)
    return lax.dot_general(a, b, dims, preferred_element_type=F32, precision=precision)


def _sigmoid(x):
    return jax.nn.sigmoid(x)


def _gelu_parts(x):
    cdf = 0.5 * (1.0 + lax.erf(x * 0.7071067811865476))
    pdf = jnp.exp(-0.5 * x * x) * 0.3989422804014327
    return x * cdf, cdf + x * pdf


def _rms(x):
    return lax.rsqrt(jnp.mean(x * x, axis=-1, keepdims=True) + EPS)


def _rms_bwd(xhat, r, gw):
    return r * (gw - xhat * jnp.mean(xhat * gw, axis=-1, keepdims=True))


def _lower_bound(lbp_ref):
    l0, l1 = lbp_ref[0:1, :], lbp_ref[1:2, :]
    m = jnp.maximum(l0, l1)
    e0, e1 = jnp.exp(l0 - m), jnp.exp(l1 - m)
    return e0 / (e0 + e1), e1 / (e0 + e1)


def _proj_in(x, nw, sc, sh, w_in_b, tm):
    T = x.shape[0]

    def body(x_ref, nw_ref, sc_ref, sh_ref, w_ref, h_ref, p_ref):
        xv = x_ref[...]
        h = ((xv * _rms(xv)) * nw_ref[...]) * (1.0 + sc_ref[...]) + sh_ref[...]
        hb = h.astype(BF16)
        h_ref[...] = hb
        p_ref[...] = _dot(hb, w_ref[...])

    row = lambda i: (i, 0)
    return pl.pallas_call(
        body, grid=(T // tm,),
        in_specs=[pl.BlockSpec((tm, D), row), _full((1, D)), _full((1, D)), _full((1, D)), _resident((D, DIN))],
        out_specs=[pl.BlockSpec((tm, D), row), pl.BlockSpec((tm, DIN), row)],
        out_shape=[SDS((T, D), BF16), SDS((T, DIN), F32)],
        compiler_params=_arb(), name="proj_in")(x, nw, sc, sh, w_in_b)


def _gmlp_common(u, v, lnw, lnb, ws_ref, bst_ref):
    ug, dug = _gelu_parts(u)
    vg, dvg = _gelu_parts(v)
    mu = jnp.mean(vg, axis=-1, keepdims=True)
    vc = vg - mu
    rstd = lax.rsqrt(jnp.mean(vc * vc, axis=-1, keepdims=True) + EPS)
    vhat = vc * rstd
    vn = vhat * lnw + lnb
    vnb = vn.astype(BF16)
    mixed = []
    for h in range(NH):
        sl = slice(h * HD, (h + 1) * HD)
        mixed.append(_dot(ws_ref[h], vnb[:, sl]) + bst_ref[:, h:h + 1])
    return ug, dug, dvg, rstd, vhat, vnb, jnp.concatenate(mixed, axis=1)


def _gmlp_fwd(proj, ws_b, bst, lnw, lnb):
    T = proj.shape[0]
    rows = min(GMLP_ROWS_PER_STEP, T)

    def body(u_ref, v_ref, ws_ref, bst_ref, lnw_ref, lnb_ref, y_ref):
        for bi in range(rows // BLK):
            rs = slice(bi * BLK, (bi + 1) * BLK)
            ug, _, _, _, _, _, mixed = _gmlp_common(u_ref[rs, :], v_ref[rs, :], lnw_ref[...], lnb_ref[...], ws_ref, bst_ref)
            y_ref[rs, :] = (ug * mixed).astype(BF16)

    return pl.pallas_call(
        body, grid=(T // rows,),
        in_specs=[pl.BlockSpec((rows, DG), lambda i: (i, 0)), pl.BlockSpec((rows, DG), lambda i: (i, 1)),
                  _full((NH, BLK, BLK)), _full((BLK, NH)), _full((1, DG)), _full((1, DG))],
        out_specs=pl.BlockSpec((rows, DG), lambda i: (i, 0)),
        out_shape=SDS((T, D), BF16),
        compiler_params=_arb(), name="gmlp_fwd")(proj, proj, ws_b, bst, lnw, lnb)


def _hgrn_tables():
    t = np.arange(CH)[:, None]
    j = np.arange(CH)[None, :]
    blocks = [j <= t, j > t]
    masks = []
    for n in LEVELS:
        mid = t - t % n + n // 2
        blocks.append(np.where(t >= mid, (j >= mid) & (j <= t), (j > t) & (j < mid)))
        masks.append((t // n == j // n) & (t % n >= n // 2) & (j % n < n // 2))
    w = np.concatenate(blocks, axis=0).astype(np.float32)
    m = np.stack(masks).astype(np.float32)
    return (jnp.asarray(w, BF16), jnp.asarray(w.T, BF16), jnp.asarray(m), jnp.asarray(m + m.transpose(0, 2, 1)))


def _split_dot(w, x, parts):
    acc = None
    for _ in range(parts):
        piece = x.astype(BF16)
        term = _dot(w, piece)
        acc = term if acc is None else acc + term
        x = x - piece.astype(F32)
    return acc


def _hgrn_gates(q, fl, lb, omlb, w_ref):
    sq = _sigmoid(q)
    qf = q * sq
    sig = _sigmoid(fl)
    f = lb + omlb * sig
    k = 1.0 - f
    e = jnp.exp(_split_dot(w_ref[...], jnp.log(f), 3))
    return sq, qf, sig, f, k, e


def _level_factor(e, li, sl, row, qh, kh):
    el = e[(2 + li) * CH:(3 + li) * CH, sl]
    up = (row & (LEVELS[li] // 2)) != 0
    return el, up, el * jnp.where(up, qh, kh)


def _hgrn_fwd(proj, lower_bounds, gn_w, ycat, tables, placed, axes):
    T = proj.shape[0]
    nc = T // CH
    nch = min(HGRN_CHUNKS_PER_STEP, nc)
    steps = nc // nch
    w_st, _, masks, _ = tables
    nw = len(placed)
    pass_step = (5 * steps) // 8

    def body(*refs):
        q_ref, f_ref, i_ref, g_ref, lbp_ref, gn_ref, w_ref, m_ref = refs[:8]
        y_ref, o_ref, a_ref, st_ref = refs[9 + nw:13 + nw]
        s_scr, send_sems, recv_sems = refs[13 + 2 * nw:]
        gather = _WeightGather(refs[13 + nw:13 + 2 * nw], axes, send_sems, recv_sems)
        step = pl.program_id(0)

        @pl.when(step == 0)
        def _():
            gather.start()
            s_scr[...] = jnp.zeros_like(s_scr)

        @pl.when(step == pass_step)
        def _():
            gather.forward()

        lb, omlb = _lower_bound(lbp_ref)
        row = lax.broadcasted_iota(jnp.int32, (CH, 1), 0)
        eye = lax.broadcasted_iota(jnp.int32, (CH, CH), 0) == lax.broadcasted_iota(jnp.int32, (CH, CH), 1)
        pre = []
        for ci in range(nch):
            rs = slice(ci * CH, (ci + 1) * CH)
            _, qf, _, _, k, e = _hgrn_gates(q_ref[rs, :], f_ref[rs, :], lb, omlb, w_ref)
            mats = []
            for h in range(NH):
                sl = slice(h * HD, (h + 1) * HD)
                qh, kh = qf[:, sl], k[:, sl]
                a = jnp.where(eye, jnp.sum(qh * kh, axis=-1, keepdims=True), 0.0)
                for li in range(len(LEVELS)):
                    _, _, y = _level_factor(e, li, sl, row, qh, kh)
                    yb = y.astype(BF16)
                    a = a + m_ref[li] * _dot(yb, yb, NT)
                a_ref[ci, h] = a
                mats.append(a.astype(BF16))
            eb = e[0:CH]
            pre.append(((qf * eb).astype(BF16), eb[CH - 1:CH, :], (k * e[CH:2 * CH]).astype(BF16), mats))
        for ci in range(nch):
            rs = slice(ci * CH, (ci + 1) * CH)
            qe, ebl, kd, mats = pre[ci]
            v = i_ref[rs, :]
            g = g_ref[rs, :]
            for h in range(NH):
                sl = slice(h * HD, (h + 1) * HD)
                st0 = s_scr[h]
                st_ref[ci, h] = st0
                vb = v[:, sl].astype(BF16)
                o = _dot(qe[:, sl], st0.astype(BF16), NT) + _dot(mats[h], vb)
                s_scr[h] = st0 * ebl[:, sl] + _dot(vb, kd[:, sl], TN)
                o_ref[rs, sl] = o
                gh = g[:, sl]
                y_ref[rs, sl] = (((o * _rms(o)) * gn_ref[...]) * (gh * _sigmoid(gh))).astype(BF16)

        @pl.when(step == steps - 1)
        def _():
            gather.finish()

    blk = lambda j: pl.BlockSpec((nch * CH, DH), lambda c: (c, j))
    anyspec = pl.BlockSpec(memory_space=pl.ANY)
    res = pl.pallas_call(
        body, grid=(steps,),
        in_specs=[blk(2), blk(3), blk(4), blk(5), _full((2, DH)), _full((1, HD)),
                  _full(w_st.shape), _full(masks.shape), anyspec] + [anyspec] * nw,
        out_specs=[pl.BlockSpec((nch * CH, DH), lambda c: (c, 1)),
                   pl.BlockSpec((nch * CH, DH), lambda c: (c, 0)),
                   pl.BlockSpec((nch, NH, CH, CH), lambda c: (c, 0, 0, 0)),
                   pl.BlockSpec((nch, NH, HD, HD), lambda c: (c, 0, 0, 0))] + [anyspec] * nw,
        out_shape=[SDS((T, D), BF16), SDS((T, DH), F32), SDS((nc, NH, CH, CH), F32), SDS((nc, NH, HD, HD), F32)]
        + [SDS(a.shape, a.dtype) for a in placed],
        scratch_shapes=[pltpu.VMEM((NH, HD, HD), F32)] + _gather_sems(nw),
        input_output_aliases={8: 0, **{9 + i: 4 + i for i in range(nw)}},
        compiler_params=_arb(), name="hgrn_fwd")(proj, proj, proj, proj, lower_bounds, gn_w, w_st, masks, ycat, *placed)
    return res[:4], res[4:]


def _token_local(x, ycat, tgt, g1, nw2, sc2, sh2, g2, fw, w_out_b, w_fi_b, w_fo_b, tm):
    T = x.shape[0]
    inv_d = 1.0 / D

    def body(x_ref, y_ref, t_ref, g1_ref, nw2_ref, sc2_ref, sh2_ref, g2_ref, fw_ref, wo_ref, wfi_ref, wfo_ref,
             dy_ref, dx1_ref, h2_ref, act_ref, dff_ref, dgu_ref, dmix_ref, acc_ref):
        @pl.when(pl.program_id(0) == 0)
        def _():
            acc_ref[...] = jnp.zeros_like(acc_ref)

        def acc(row, val):
            acc_ref[row:row + 1, :] += jnp.sum(val, axis=0, keepdims=True)

        g1v, g2v = g1_ref[...], g2_ref[...]
        mix = _dot(y_ref[...], wo_ref[...])
        x1 = x_ref[...] + g1v * mix
        r2 = _rms(x1)
        xh2 = x1 * r2
        n2 = xh2 * nw2_ref[...]
        osc2 = 1.0 + sc2_ref[...]
        h2b = (n2 * osc2 + sh2_ref[...]).astype(BF16)
        h2_ref[...] = h2b
        ff = jnp.zeros((tm, D), F32)
        saved = []
        for kb in range(DFF // FFB):
            gate = _dot(h2b, wfi_ref[:, kb * FFB:(kb + 1) * FFB])
            up = _dot(h2b, wfi_ref[:, DFF + kb * FFB:DFF + (kb + 1) * FFB])
            sg = _sigmoid(gate)
            actb = (gate * sg * up).astype(BF16)
            act_ref[:, kb * FFB:(kb + 1) * FFB] = actb
            ff = ff + _dot(actb, wfo_ref[kb * FFB:(kb + 1) * FFB, :])
            saved.append((gate, up, sg))
        x2 = x1 + g2v * ff
        r3 = _rms(x2)
        xh3 = x2 * r3
        err = xh3 * fw_ref[...] - t_ref[...]
        acc(6, (0.5 * inv_d) * err * err)
        dy = err * inv_d
        acc(4, dy * xh3)
        dx2 = _rms_bwd(xh3, r3, dy * fw_ref[...])
        acc(0, dx2 * ff)
        dffb = (dx2 * g2v).astype(BF16)
        dff_ref[...] = dffb
        dh2 = jnp.zeros((tm, D), F32)
        for kb in range(DFF // FFB):
            gate, up, sg = saved[kb]
            da = _dot(dffb, wfo_ref[kb * FFB:(kb + 1) * FFB, :], NT)
            dgate = (da * up * (sg * (1.0 + gate * (1.0 - sg)))).astype(BF16)
            dup = (da * gate * sg).astype(BF16)
            dgu_ref[:, kb * FFB:(kb + 1) * FFB] = dgate
            dgu_ref[:, DFF + kb * FFB:DFF + (kb + 1) * FFB] = dup
            dh2 = dh2 + _dot(dgate, wfi_ref[:, kb * FFB:(kb + 1) * FFB], NT)
            dh2 = dh2 + _dot(dup, wfi_ref[:, DFF + kb * FFB:DFF + (kb + 1) * FFB], NT)
        acc(2, dh2)
        acc(1, dh2 * n2)
        dn2 = dh2 * osc2
        acc(3, dn2 * xh2)
        dx1 = dx2 + _rms_bwd(xh2, r2, dn2 * nw2_ref[...])
        acc(5, dx1 * mix)
        dmixb = (dx1 * g1v).astype(BF16)
        dmix_ref[...] = dmixb
        dy_ref[...] = _dot(dmixb, wo_ref[...], NT)
        dx1_ref[...] = dx1

    row = lambda i: (i, 0)
    vec = _full((1, D))
    return pl.pallas_call(
        body, grid=(T // tm,),
        in_specs=[pl.BlockSpec((tm, D), row), pl.BlockSpec((tm, D), row), pl.BlockSpec((tm, D), row),
                  vec, vec, vec, vec, vec, vec,
                  _resident((D, D)), _resident((D, 2 * DFF)), _resident((DFF, D))],
        out_specs=[pl.BlockSpec((tm, D), row), pl.BlockSpec((tm, D), row), pl.BlockSpec((tm, D), row),
                   pl.BlockSpec((tm, DFF), row), pl.BlockSpec((tm, D), row), pl.BlockSpec((tm, 2 * DFF), row),
                   pl.BlockSpec((tm, D), row), _full((8, D))],
        out_shape=[SDS((T, D), F32), SDS((T, D), F32), SDS((T, D), BF16), SDS((T, DFF), BF16), SDS((T, D), BF16),
                   SDS((T, 2 * DFF), BF16), SDS((T, D), BF16), SDS((8, D), F32)],
        compiler_params=_arb(), name="token_local")(x, ycat, tgt, g1, nw2, sc2, sh2, g2, fw, w_out_b, w_fi_b, w_fo_b)


def _gmlp_bwd(proj, dycat, ws_b, bst, lnw, lnb, grads):
    T = proj.shape[0]
    rows = min(GMLP_ROWS_PER_STEP, T)
    nb = T // rows
    nw = len(grads)

    def body(*refs):
        u_ref, v_ref, dy_ref, ws_ref, bst_ref, lnw_ref, lnb_ref = refs[:7]
        dp_ref, dws_ref, dbs_ref, dln_ref = refs[7 + nw:11 + nw]
        dbs_acc, send_sems, recv_sems = refs[11 + 2 * nw:]
        exchange = _CoreExchange(refs[7:7 + nw], refs[11 + nw:11 + 2 * nw], send_sems, recv_sems)
        i = pl.program_id(0)

        @pl.when(i == 0)
        def _():
            exchange.start()
            dws_ref[...] = jnp.zeros_like(dws_ref)
            dln_ref[...] = jnp.zeros_like(dln_ref)
            dbs_acc[...] = jnp.zeros_like(dbs_acc)

        r = lax.broadcasted_iota(jnp.int32, (BLK, BLK), 0) // CH
        c = lax.broadcasted_iota(jnp.int32, (BLK, BLK), 1) // CH
        for bi in range(rows // BLK):
            rs = slice(bi * BLK, (bi + 1) * BLK)
            ug, dug, dvg, rstd, vhat, vnb, mixed = _gmlp_common(
                u_ref[rs, :], v_ref[rs, :], lnw_ref[...], lnb_ref[...], ws_ref, bst_ref)
            dya = dy_ref[rs, :]
            dp_ref[rs, 0:DG] = (dya * mixed * dug).astype(BF16)
            dmixed = dya * ug
            dbs_acc[...] += dmixed
            dmb = dmixed.astype(BF16)
            dvn = []
            for h in range(NH):
                sl = slice(h * HD, (h + 1) * HD)
                dws_ref[h] += jnp.where(r >= c, _dot(dmb[:, sl], vnb[:, sl], NT), 0.0)
                dvn.append(_dot(ws_ref[h], dmb[:, sl], TN))
            dvn = jnp.concatenate(dvn, axis=1)
            dln_ref[0:1, :] += jnp.sum(dvn * vhat, axis=0, keepdims=True)
            dln_ref[1:2, :] += jnp.sum(dvn, axis=0, keepdims=True)
            dvh = dvn * lnw_ref[...]
            dvgel = rstd * (dvh - jnp.mean(dvh, axis=-1, keepdims=True) - vhat * jnp.mean(dvh * vhat, axis=-1, keepdims=True))
            dp_ref[rs, DG:2 * DG] = (dvgel * dvg).astype(BF16)

        @pl.when(i == nb - 1)
        def _():
            lane = lax.broadcasted_iota(jnp.int32, (BLK, HD), 1)
            out = jnp.zeros((BLK, HD), F32)
            for h in range(NH):
                out = out + jnp.where(lane == h, jnp.sum(dbs_acc[:, h * HD:(h + 1) * HD], axis=-1, keepdims=True), 0.0)
            dbs_ref[...] = out
            exchange.finish()

    anyspec = pl.BlockSpec(memory_space=pl.ANY)
    res = pl.pallas_call(
        body, grid=(nb,),
        in_specs=[pl.BlockSpec((rows, DG), lambda i: (i, 0)), pl.BlockSpec((rows, DG), lambda i: (i, 1)),
                  pl.BlockSpec((rows, DG), lambda i: (i, 0)),
                  _full((NH, BLK, BLK)), _full((BLK, NH)), _full((1, DG)), _full((1, DG))] + [anyspec] * nw,
        out_specs=[pl.BlockSpec((rows, 2 * DG), lambda i: (i, 2)), _full((NH, BLK, BLK)), _full((BLK, HD)), _full((8, DG))]
        + [anyspec] * nw,
        out_shape=[SDS((T, DIN), BF16), SDS((NH, BLK, BLK), F32), SDS((BLK, HD), F32), SDS((8, DG), F32)]
        + _core_exchange_shapes(grads),
        scratch_shapes=[pltpu.VMEM((BLK, DG), F32)] + _core_exchange_sems(nw),
        compiler_params=_arb(), name="gmlp_bwd")(proj, proj, dycat, ws_b, bst, lnw, lnb, *grads)
    return res[:4], res[4:]


def _hgrn_bwd(proj, o_pre, a_all, st_all, dycat, lower_bounds, gn_w, dproj, tables, sums):
    T = proj.shape[0]
    nc = T // CH
    nch = min(HGRN_CHUNKS_PER_STEP, nc)
    steps = nc // nch
    w_st, w_st_t, _, masks_sym = tables
    n_lev = len(LEVELS)
    nw = len(sums)

    def body(*refs):
        q_ref, f_ref, i_ref, g_ref, o_ref, a_ref, st_ref, dy_ref, lbp_ref, gn_ref, w_ref, wt_ref, ms_ref = refs[:13]
        dp_ref, dlb_ref, dgn_ref = refs[14 + nw:17 + nw]
        ds_scr, dx_scr, send_sems, recv_sems = refs[17 + 2 * nw:]
        exchange = _ChipExchange(refs[14:14 + nw], refs[17 + nw:17 + 2 * nw], send_sems, recv_sems)
        i = pl.program_id(0)

        @pl.when(i == 0)
        def _():
            exchange.start()
            ds_scr[...] = jnp.zeros_like(ds_scr)
            dlb_ref[...] = jnp.zeros_like(dlb_ref)
            dgn_ref[...] = jnp.zeros_like(dgn_ref)

        lb, omlb = _lower_bound(lbp_ref)
        row = lax.broadcasted_iota(jnp.int32, (CH, 1), 0)
        eye = lax.broadcasted_iota(jnp.int32, (CH, CH), 0) == lax.broadcasted_iota(jnp.int32, (CH, CH), 1)
        lower = lax.broadcasted_iota(jnp.int32, (CH, CH), 0) > lax.broadcasted_iota(jnp.int32, (CH, CH), 1)
        dgn = jnp.zeros((1, HD), F32)
        pre = []
        for ci in range(nch):
            rs = slice(ci * CH, (ci + 1) * CH)
            q = q_ref[rs, :]
            v = i_ref[rs, :]
            g = g_ref[rs, :]
            sq, qf, sig, f, k, e = _hgrn_gates(q, f_ref[rs, :], lb, omlb, w_ref)
            eb = e[0:CH]
            ekd = e[CH:2 * CH]
            kd = k * ekd
            qe = qf * eb
            dob_h, dqe_h, dqf_h, dki_h, dv_h, dg_h = [], [], [], [], [], []
            for h in range(NH):
                sl = slice(h * HD, (h + 1) * HD)
                o = o_ref[rs, sl]
                ro = _rms(o)
                oh = o * ro
                gh = g[:, sl]
                sg = _sigmoid(gh)
                dyb = dy_ref[rs, sl]
                dg_h.append(dyb * (oh * gn_ref[...]) * (sg * (1.0 + gh * (1.0 - sg))))
                don = dyb * (gh * sg)
                dgn = dgn + jnp.sum(don * oh, axis=0, keepdims=True)
                dob = _rms_bwd(oh, ro, don * gn_ref[...]).astype(BF16)
                vb = v[:, sl].astype(BF16)
                qh, kh = qf[:, sl], k[:, sl]
                dqe = _dot(dob, st_ref[ci, h].astype(BF16))
                da = _dot(dob, vb, NT)
                ddiag = jnp.sum(jnp.where(eye, da, 0.0), axis=-1, keepdims=True)
                dsym = jnp.where(lower, da, _dot(vb, dob, NT))
                upper_part = jnp.zeros((CH, HD), F32)
                both = jnp.zeros((CH, HD), F32)
                for li in range(n_lev):
                    el, up, y = _level_factor(e, li, sl, row, qh, kh)
                    dyv = _dot((ms_ref[li] * dsym).astype(BF16), y.astype(BF16))
                    dx_scr[ci, (2 + li) * CH:(3 + li) * CH, sl] = dyv * y
                    dye = dyv * el
                    upper_part = upper_part + jnp.where(up, dye, 0.0)
                    both = both + dye
                dob_h.append(dob)
                dqe_h.append(dqe)
                dqf_h.append(dqe * eb[:, sl] + ddiag * kh + upper_part)
                dki_h.append(ddiag * qh + (both - upper_part))
                dv_h.append(_dot(a_ref[ci, h].astype(BF16), dob, TN))
            dp_ref[rs, 0:DH] = (jnp.concatenate(dqf_h, axis=1) * (sq * (1.0 + q * (1.0 - sq)))).astype(BF16)
            dp_ref[rs, 3 * DH:4 * DH] = jnp.concatenate(dg_h, axis=1).astype(BF16)
            pre.append((v, sig, f, eb, ekd, kd, qe, dob_h, jnp.concatenate(dqe_h, axis=1), dki_h, dv_h))
        dgn_ref[0:1, :] += dgn
        for ci in reversed(range(nch)):
            rs = slice(ci * CH, (ci + 1) * CH)
            v, sig, f, eb, ekd, kd, qe, dob_h, dqe, dki_h, dv_h = pre[ci]
            ebl = eb[CH - 1:CH, :]
            dbl_h, dkd_h, dv2_h = [], [], []
            for h in range(NH):
                sl = slice(h * HD, (h + 1) * HD)
                dst1 = ds_scr[h]
                dst1b = dst1.astype(BF16)
                ds_scr[h] = dst1 * ebl[:, sl] + _dot(dob_h[h], qe[:, sl].astype(BF16), TN)
                dbl_h.append(ebl[:, sl] * jnp.sum(st_ref[ci, h] * dst1, axis=0, keepdims=True))
                dkd_h.append(_dot(v[:, sl].astype(BF16), dst1b))
                dv2_h.append(dv_h[h] + _dot(kd[:, sl].astype(BF16), dst1b, NT))
            dkd = jnp.concatenate(dkd_h, axis=1)
            dx_scr[ci, 0:CH, :] = dqe * qe + jnp.where(row == CH - 1, jnp.concatenate(dbl_h, axis=1), 0.0)
            dx_scr[ci, CH:2 * CH, :] = dkd * kd
            dlf = _split_dot(wt_ref[...], dx_scr[ci], 2)
            df = dlf / f - (dkd * ekd + jnp.concatenate(dki_h, axis=1))
            dlb_ref[0:1, :] += jnp.sum(df * (1.0 - sig), axis=0, keepdims=True)
            dp_ref[rs, DH:2 * DH] = (df * omlb * sig * (1.0 - sig)).astype(BF16)
            dp_ref[rs, 2 * DH:3 * DH] = jnp.concatenate(dv2_h, axis=1).astype(BF16)

        @pl.when(i == steps - 1)
        def _():
            gl = dlb_ref[0:1, :] * lb * omlb
            dlb_ref[0:1, :] = gl
            dlb_ref[1:2, :] = -gl
            exchange.finish()

    rev = lambda j: pl.BlockSpec((nch * CH, DH), lambda c: (steps - 1 - c, j))
    anyspec = pl.BlockSpec(memory_space=pl.ANY)
    res = pl.pallas_call(
        body, grid=(steps,),
        in_specs=[rev(2), rev(3), rev(4), rev(5), rev(0),
                  pl.BlockSpec((nch, NH, CH, CH), lambda c: (steps - 1 - c, 0, 0, 0)),
                  pl.BlockSpec((nch, NH, HD, HD), lambda c: (steps - 1 - c, 0, 0, 0)),
                  rev(1), _full((2, DH)), _full((1, HD)),
                  _full(w_st.shape), _full(w_st_t.shape), _full(masks_sym.shape),
                  anyspec] + [anyspec] * nw,
        out_specs=[pl.BlockSpec((nch * CH, 4 * DH), lambda c: (steps - 1 - c, 0)), _full((8, DH)), _full((8, HD))]
        + [anyspec] * nw,
        out_shape=[SDS((T, DIN), BF16), SDS((8, DH), F32), SDS((8, HD), F32)] + _slot_shapes(sums),
        scratch_shapes=[pltpu.VMEM((NH, HD, HD), F32), pltpu.VMEM((nch, (2 + n_lev) * CH, DH), F32)] + _exchange_sems(nw),
        input_output_aliases={13: 0},
        compiler_params=_arb(), name="hgrn_bwd")(proj, proj, proj, proj, o_pre, a_all, st_all, dycat, lower_bounds, gn_w,
                                                 w_st, w_st_t, masks_sym, dproj, *sums)
    return res[:3], res[3:]


def _proj_in_bwd(dproj, x, dx1, nw, sc, w_in_b, tm, sums):
    T = x.shape[0]
    ns = len(sums)
    steps = T // tm

    def body(*refs):
        dp_ref, x_ref, dx1_ref, nw_ref, sc_ref, w_ref = refs[:6]
        gx_ref, acc_ref = refs[6 + ns:8 + ns]
        exchange = _ChipExchange(refs[6:6 + ns], refs[8 + ns:8 + 2 * ns], *refs[8 + 2 * ns:])

        @pl.when(pl.program_id(0) == 0)
        def _():
            exchange.start()
            acc_ref[...] = jnp.zeros_like(acc_ref)

        dh = _dot(dp_ref[:, 0:4 * DH], w_ref[:, 2 * DG:DIN], NT) + _dot(dp_ref[:, 4 * DH:DIN], w_ref[:, 0:2 * DG], NT)
        xv = x_ref[...]
        r = _rms(xv)
        xh = xv * r
        n1 = xh * nw_ref[...]
        acc_ref[0:1, :] += jnp.sum(dh, axis=0, keepdims=True)
        acc_ref[1:2, :] += jnp.sum(dh * n1, axis=0, keepdims=True)
        dn = dh * (1.0 + sc_ref[...])
        acc_ref[2:3, :] += jnp.sum(dn * xh, axis=0, keepdims=True)
        gx_ref[...] = dx1_ref[...] + _rms_bwd(xh, r, dn * nw_ref[...])

        @pl.when(pl.program_id(0) == steps - 1)
        def _():
            exchange.finish()

    row = lambda i: (i, 0)
    anyspec = pl.BlockSpec(memory_space=pl.ANY)
    res = pl.pallas_call(
        body, grid=(steps,),
        in_specs=[pl.BlockSpec((tm, DIN), row), pl.BlockSpec((tm, D), row), pl.BlockSpec((tm, D), row),
                  _full((1, D)), _full((1, D)), _resident((D, DIN))] + [anyspec] * ns,
        out_specs=[pl.BlockSpec((tm, D), row), _full((8, D))] + [anyspec] * ns,
        out_shape=[SDS((T, D), F32), SDS((8, D), F32)] + _slot_shapes(sums),
        scratch_shapes=_exchange_sems(ns),
        compiler_params=_arb(), name="proj_in_bwd")(dproj, x, dx1, nw, sc, w_in_b, *sums)
    return res[:2], res[2:]


def _wgrad(a, b, bk, bn, tt, name):
    T, K = a.shape
    N = b.shape[1]
    nn, nk, nt = N // bn, K // bk, T // tt
    bmap = lambda n, k, t: (t, n)

    def body(a_ref, b_ref, o_ref):
        @pl.when(pl.program_id(2) == 0)
        def _():
            o_ref[...] = jnp.zeros_like(o_ref)

        o_ref[0] += _dot(a_ref[...], b_ref[...], TN)

    return pl.pallas_call(
        body, grid=(nn, nk, nt),
        in_specs=[pl.BlockSpec((tt, bk), lambda n, k, t: (t, k)), pl.BlockSpec((tt, bn), bmap)],
        out_specs=pl.BlockSpec((1, bk, bn), lambda n, k, t: (n, k, 0)),
        out_shape=SDS((nn, K, bn), F32),
        compiler_params=_arb(3), name=name)(a, b)


def _adam_math(w, g, m, v):
    m = B1 * m + (1.0 - B1) * g
    v = B2 * v + (1.0 - B2) * (g * g)
    m_hat = m / (1.0 - B1 ** STEP)
    v_hat = v / (1.0 - B2 ** STEP)
    return -LR * (m_hat / (jnp.sqrt(v_hat) + AEPS) + WD * w), m, v


def _adamw_halves(w, mine, sibling, m, v, c_idx, rb, name):
    R, C = w.shape
    nb = (R // 2) // rb

    def body(c_ref, w_ref, a_ref, b_ref, m_ref, v_ref, g_out, d_out, m_out, v_out):
        g = jnp.where(pl.program_id(0) == c_ref[0], a_ref[...], b_ref[...])
        g_out[...] = g
        d_out[...], m_out[...], v_out[...] = _adam_math(w_ref[...], g, m_ref[...], v_ref[...])

    whole = pl.BlockSpec((rb, C), lambda hh, i, cr: (hh * nb + i, 0))
    half = pl.BlockSpec((rb, C), lambda hh, i, cr: (i, 0))
    return pl.pallas_call(
        body,
        grid_spec=pltpu.PrefetchScalarGridSpec(
            num_scalar_prefetch=1, grid=(2, nb), in_specs=[whole, half, half, whole, whole], out_specs=[whole] * 4),
        out_shape=[SDS((R, C), F32)] * 4, compiler_params=_arb(2), name=name)(c_idx, w, mine, sibling, m, v)


def _ada_forward(c_all, w_ada):
    n = w_ada.shape[1]

    def body(c_ref, w_ref, ca_ref, p_ref):
        cv = c_ref[...]
        ca = cv * _sigmoid(cv)
        ca_ref[...] = ca
        p_ref[...] = _dot(ca, w_ref[...], precision=HIGHEST)

    return pl.pallas_call(
        body, grid=(n // 512,),
        in_specs=[_full((N_DEV, D)), pl.BlockSpec((D, 512), lambda i: (0, i))],
        out_specs=[_full((N_DEV, D)), pl.BlockSpec((N_DEV, 512), lambda i: (0, i))],
        out_shape=[SDS((N_DEV, D), F32), SDS((N_DEV, n), F32)],
        compiler_params=_arb(), name="ada_forward")(c_all, w_ada)


def _ada_wgrad_adam(cact_t, dada, w, m, v):
    R, C = w.shape
    rb = 256

    def body(c_ref, d_ref, w_ref, m_ref, v_ref, g_out, d_out, m_out, v_out):
        g = _dot(c_ref[...], d_ref[...], precision=HIGHEST)
        g_out[...] = g
        d_out[...], m_out[...], v_out[...] = _adam_math(w_ref[...], g, m_ref[...], v_ref[...])

    spec = pl.BlockSpec((rb, C), lambda i: (i, 0))
    return pl.pallas_call(
        body, grid=(R // rb,),
        in_specs=[pl.BlockSpec((rb, N_DEV), lambda i: (i, 0)), _full((N_DEV, C)), spec, spec, spec],
        out_specs=[spec] * 4, out_shape=[SDS((R, C), F32)] * 4,
        compiler_params=_arb(), name="ada_wgrad_adam")(cact_t, dada, w, m, v)


def _small_finalize(gathered, w, m, v):
    def body(ga_ref, w_ref, m_ref, v_ref, g_out, d_out, m_out, v_out):
        g = ga_ref[0:SMALL_ROWS, :]
        for dev in range(1, N_DEV):
            g = g + ga_ref[dev * SMALL_ROWS:(dev + 1) * SMALL_ROWS, :]
        g_out[...] = g
        d_out[...], m_out[...], v_out[...] = _adam_math(w_ref[...], g, m_ref[...], v_ref[...])

    return pl.pallas_call(
        body, out_shape=[SDS((SMALL_ROWS, D), F32)] * 4, name="small_finalize")(gathered, w, m, v)


def _position():
    x, y, c = lax.axis_index("x"), lax.axis_index("y"), lax.axis_index("c")
    return x, y, c


def _chip_at(x, y, r):
    return (x ^ (r >> 1), y ^ (r & 1))


def _all_gather_rows(block, name):
    m_per, n = block.shape

    def body(x_ref, out_ref, send_sems, recv_sems, local_sem):
        x, y, c = _position()
        me, sibling = (x, y, c), (x, y, 1 - c)
        chips = [_chip_at(x, y, r) for r in (1, 2, 3)]

        def rows(px, py, pc):
            return out_ref.at[pl.ds((4 * px + 2 * py + pc) * m_per, m_per), :]

        def copy(k, blk, to, src=None):
            return pltpu.make_async_remote_copy(
                src_ref=rows(*blk) if src is None else src, dst_ref=rows(*blk),
                send_sem=send_sems.at[k], recv_sem=recv_sems.at[k], device_id=to, device_id_type=MESH)

        mine = pltpu.make_async_copy(x_ref, rows(*me), local_sem)
        mine.start()
        first = [copy(0, me, sibling, src=x_ref)]
        first += [copy(1 + j, me, (*chip, c), src=x_ref) for j, chip in enumerate(chips)]
        for cp in first:
            cp.start()
        passed = [copy(4 + j, (*chip, c), sibling) for j, chip in enumerate(chips)]
        for j, chip in enumerate(chips):
            copy(1 + j, (*chip, c), me).wait_recv()
            passed[j].start()
        copy(0, sibling, me).wait_recv()
        for j, chip in enumerate(chips):
            copy(4 + j, (*chip, 1 - c), me).wait_recv()
        for cp in first + passed:
            cp.wait_send()
        mine.wait()

    return pl.pallas_call(
        body, out_shape=SDS((N_DEV * m_per, n), block.dtype),
        in_specs=[pl.BlockSpec(memory_space=pltpu.VMEM)], out_specs=pl.BlockSpec(memory_space=pltpu.VMEM),
        scratch_shapes=[pltpu.SemaphoreType.DMA((7,)), pltpu.SemaphoreType.DMA((7,)), pltpu.SemaphoreType.DMA],
        name=name)(block)


def _place_shard(w_shard, axis, chip_idx, name):
    R, C = w_shard.shape
    rb = _row_block(R)
    nb = R // rb
    full = (R * N_CHIPS, C) if axis == 0 else (R, C * N_CHIPS)
    omap = (lambda i, j: (j[0] * nb + i, 0)) if axis == 0 else (lambda i, j: (i, j[0]))

    def body(j_ref, w_ref, o_ref):
        o_ref[...] = w_ref[...].astype(BF16)

    return pl.pallas_call(
        body,
        grid_spec=pltpu.PrefetchScalarGridSpec(
            num_scalar_prefetch=1, grid=(nb,), in_specs=[pl.BlockSpec((rb, C), lambda i, j: (i, 0))],
            out_specs=pl.BlockSpec((rb, C), omap)),
        out_shape=SDS(full, BF16), compiler_params=_arb(), name=name)(chip_idx, w_shard)


class _WeightGather:
    def __init__(self, refs, axes, send_sems, recv_sems):
        self.refs, self.axes, self.send_sems, self.recv_sems = refs, axes, send_sems, recv_sems
        self.x, self.y, self.c = _position()
        self.j = 2 * self.x + self.y
        self.n = 3 * len(refs)

    def _half(self, w, chip_idx, half):
        ref, axis = self.refs[w], self.axes[w]
        if axis == 0:
            size = ref.shape[0] // N_CHIPS
            return ref.at[pl.ds(chip_idx * size + half * (size // 2), size // 2), :]
        size = ref.shape[1] // N_CHIPS
        rows = ref.shape[0] // 2
        return ref.at[pl.ds(half * rows, rows), pl.ds(chip_idx * size, size)]

    def _ici(self, w, r, chip_idx):
        k = 3 * w + r - 1
        piece = self._half(w, chip_idx, self.c)
        return pltpu.make_async_remote_copy(
            src_ref=piece, dst_ref=piece, send_sem=self.send_sems.at[k], recv_sem=self.recv_sems.at[k],
            device_id=(*_chip_at(self.x, self.y, r), self.c), device_id_type=MESH)

    def _d2d(self, w, r, half):
        k = self.n + 3 * w + r - 1
        piece = self._half(w, self.j ^ r, half)
        return pltpu.make_async_remote_copy(
            src_ref=piece, dst_ref=piece, send_sem=self.send_sems.at[k], recv_sem=self.recv_sems.at[k],
            device_id=(self.x, self.y, 1 - self.c), device_id_type=MESH)

    def _each(self):
        return [(w, r) for w in range(len(self.refs)) for r in (1, 2, 3)]

    def start(self):
        for w, r in self._each():
            self._ici(w, r, self.j).start()

    def forward(self):
        for w, r in self._each():
            self._ici(w, r, self.j ^ r).wait_recv()
            self._d2d(w, r, self.c).start()

    def finish(self):
        for w, r in self._each():
            self._ici(w, r, self.j).wait_send()
            self._d2d(w, r, self.c).wait_send()
            self._d2d(w, r, 1 - self.c).wait_recv()


def _gather_sems(n_weights):
    return [pltpu.SemaphoreType.DMA((6 * n_weights,)), pltpu.SemaphoreType.DMA((6 * n_weights,))]


def _gather_weights(placed, axes, name):
    nw = len(placed)

    def body(*refs):
        outs = refs[nw:2 * nw]
        g = _WeightGather(outs, axes, *refs[2 * nw:])
        g.start()
        g.forward()
        g.finish()

    anyspec = pl.BlockSpec(memory_space=pl.ANY)
    return pl.pallas_call(
        body, out_shape=[SDS(a.shape, a.dtype) for a in placed], in_specs=[anyspec] * nw, out_specs=[anyspec] * nw,
        scratch_shapes=_gather_sems(nw), input_output_aliases={i: i for i in range(nw)},
        name=name)(*placed)


class _ChipExchange:
    def __init__(self, ins, outs, send_sems, recv_sems):
        self.ins, self.outs, self.send_sems, self.recv_sems = ins, outs, send_sems, recv_sems
        self.x, self.y, self.c = _position()
        self.j = 2 * self.x + self.y

    def _copies(self):
        for w in range(len(self.ins)):
            for r in (1, 2, 3):
                k = 3 * w + r - 1
                yield pltpu.make_async_remote_copy(
                    src_ref=self.ins[w].at[self.j ^ r], dst_ref=self.outs[w].at[r - 1],
                    send_sem=self.send_sems.at[k], recv_sem=self.recv_sems.at[k],
                    device_id=(*_chip_at(self.x, self.y, r), self.c), device_id_type=MESH)

    def start(self):
        for cp in self._copies():
            cp.start()

    def finish(self):
        for cp in self._copies():
            cp.wait()


def _exchange_sems(n_weights):
    return [pltpu.SemaphoreType.DMA((3 * n_weights,)), pltpu.SemaphoreType.DMA((3 * n_weights,))]


class _CoreExchange:
    def __init__(self, ins, outs, send_sems, recv_sems):
        self.ins, self.outs, self.send_sems, self.recv_sems = ins, outs, send_sems, recv_sems
        self.x, self.y, self.c = _position()

    def _copies(self):
        for w in range(len(self.ins)):
            yield pltpu.make_async_remote_copy(
                src_ref=self.ins[w].at[:, 1 - self.c], dst_ref=self.outs[w],
                send_sem=self.send_sems.at[w], recv_sem=self.recv_sems.at[w],
                device_id=(self.x, self.y, 1 - self.c), device_id_type=MESH)

    def start(self):
        for cp in self._copies():
            cp.start()

    def finish(self):
        for cp in self._copies():
            cp.wait()


def _core_exchange_shapes(grads):
    return [SDS((g.shape[0], g.shape[2], g.shape[3]), F32) for g in grads]


def _core_exchange_sems(n):
    return [pltpu.SemaphoreType.DMA((n,)), pltpu.SemaphoreType.DMA((n,))]


def _exchange_core_halves(grads, name):
    nw = len(grads)

    def body(*refs):
        ex = _CoreExchange(refs[:nw], refs[nw:2 * nw], *refs[2 * nw:])
        ex.start()
        ex.finish()

    anyspec = pl.BlockSpec(memory_space=pl.ANY)
    return pl.pallas_call(
        body, out_shape=_core_exchange_shapes(grads), in_specs=[anyspec] * nw, out_specs=[anyspec] * nw,
        scratch_shapes=_core_exchange_sems(nw), name=name)(*grads)


def _add_core_halves(g4, recv, c_idx, rb, name):
    ns, _, rh, C = g4.shape

    def body(c_ref, g_ref, r_ref, o_ref):
        o_ref[...] = (g_ref[0] + r_ref[...]).astype(BF16)

    return pl.pallas_call(
        body,
        grid_spec=pltpu.PrefetchScalarGridSpec(
            num_scalar_prefetch=1, grid=(ns, rh // rb),
            in_specs=[pl.BlockSpec((1, 1, rb, C), lambda s, i, cr: (s, cr[0], i, 0)),
                      pl.BlockSpec((1, rb, C), lambda s, i, cr: (s, i, 0))],
            out_specs=pl.BlockSpec((1, rb, C), lambda s, i, cr: (s, i, 0))),
        out_shape=SDS((ns, rh, C), BF16), compiler_params=_arb(2), name=name)(c_idx, g4, recv)


def _slot_shapes(sums):
    return [SDS((3,) + s.shape[1:], s.dtype) for s in sums]


def _add_chips(own, slots, order, rb, name):
    _, rh, C = slots.shape

    def body(o_ref, own_ref, a_ref, b_ref, c_ref, d_ref, out_ref):
        mine = own_ref[0].astype(F32)
        t = [jnp.where(o_ref[i] == 0, mine, r[0].astype(F32)) for i, r in enumerate((a_ref, b_ref, c_ref, d_ref))]
        out_ref[...] = ((t[0] + t[1]) + t[2]) + t[3]

    def spec(i):
        return pl.BlockSpec((1, rb, C), lambda t, o: (jnp.maximum(o[i], 1) - 1, t, 0))

    return pl.pallas_call(
        body,
        grid_spec=pltpu.PrefetchScalarGridSpec(
            num_scalar_prefetch=1, grid=(rh // rb,),
            in_specs=[pl.BlockSpec((1, rb, C), lambda t, o: (o[4], t, 0)), spec(0), spec(1), spec(2), spec(3)],
            out_specs=pl.BlockSpec((rb, C), lambda t, o: (t, 0))),
        out_shape=SDS((rh, C), F32), compiler_params=_arb(), name=name)(order, own, slots, slots, slots, slots)


def _share_halves(halves):
    nw = len(halves)

    def body(*refs):
        ins, outs = refs[:nw], refs[nw:2 * nw]
        send_sems, recv_sems = refs[2 * nw:]
        x, y, c = _position()
        started = []
        for w in range(nw):
            cp = pltpu.make_async_remote_copy(
                src_ref=ins[w], dst_ref=outs[w], send_sem=send_sems.at[w], recv_sem=recv_sems.at[w],
                device_id=(x, y, 1 - c), device_id_type=MESH)
            cp.start()
            started.append(cp)
        for cp in started:
            cp.wait()

    anyspec = pl.BlockSpec(memory_space=pl.ANY)
    return pl.pallas_call(
        body, out_shape=[SDS(h.shape, F32) for h in halves], in_specs=[anyspec] * nw, out_specs=[anyspec] * nw,
        scratch_shapes=[pltpu.SemaphoreType.DMA((nw,)), pltpu.SemaphoreType.DMA((nw,))],
        name="share_halves")(*halves)


def _pack_small(b_ada, norm1_w, norm2_w, final_norm_w, v_ln_w, v_ln_b, lower_bounds, b_s, gn_w, w_s):
    parts = [b_ada, norm1_w, norm2_w, final_norm_w, v_ln_w, v_ln_b, lower_bounds, b_s, gn_w,
             jnp.zeros((D - NH * BLK - HD,), F32), w_s, jnp.zeros(((SMALL_ROWS - 76) * D,), F32)]
    return jnp.concatenate([p.reshape(-1) for p in parts]).reshape(SMALL_ROWS, D)


def _unpack_small(p):
    return dict(
        b_ada=p[0:6].reshape(1, 6 * D), norm1_w=p[6:7], norm2_w=p[7:8], final_norm_w=p[8],
        v_ln_w=p[9:10, 0:DG], v_ln_b=p[9:10, DG:D], lower_bounds=p[10].reshape(2, DH),
        b_s=p[11, 0:NH * BLK].reshape(1, NH, BLK), gn_w=p[11:12, NH * BLK:NH * BLK + HD],
        w_s=p[12:76].reshape(1, NH, BLK, BLK))


def _row_block(r):
    for cand in (256, 176, 128, 64, 32, 16, 8):
        if r % cand == 0:
            return cand
    return r


def kernel(x, c, w_ada, b_ada, norm1_w, w_in, w_s, b_s, v_ln_w, v_ln_b, lower_bounds, gn_w, w_out, norm2_w, w_ffn_in, w_ffn_out, final_norm_w, loss_target, m_w_ada, m_b_ada, m_norm1_w, m_w_in, m_w_s, m_b_s, m_v_ln_w, m_v_ln_b, m_lower_bounds, m_gn_w, m_w_out, m_norm2_w, m_w_ffn_in, m_w_ffn_out, m_final_norm_w, v_w_ada, v_b_ada, v_norm1_w, v_w_in, v_w_s, v_b_s, v_v_ln_w, v_v_ln_b, v_lower_bounds, v_gn_w, v_w_out, v_norm2_w, v_w_ffn_in, v_w_ffn_out, v_final_norm_w):
    T = x.shape[1]
    tm, tp = min(TOKEN_TILE, T), min(PROJ_TILE, T)
    px, py, pc = _position()
    chip = 2 * px + py
    me = 4 * px + 2 * py + pc
    x2d = x.reshape(T, D)
    tgt = loss_target.reshape(T, D)

    c_all = _all_gather_rows(jnp.broadcast_to(c, (8, D)), "gather_c").reshape(N_DEV, 8, D)[:, 0, :]
    cact, ada_part = _ada_forward(c_all, w_ada[0])
    n_ada = ada_part.shape[1]
    ada_all = _all_gather_rows(ada_part, "gather_ada").reshape(N_CHIPS, 2, N_DEV, n_ada)[:, 0]
    ada = lax.dynamic_index_in_dim(ada_all, me, axis=1, keepdims=False).reshape(1, 6 * D) + b_ada
    sh1, sc1, g1, sh2, sc2, g2 = [ada[:, i * D:(i + 1) * D] for i in range(6)]

    chip_idx = jnp.reshape(chip, (1,)).astype(jnp.int32)
    c_idx = jnp.reshape(pc, (1,)).astype(jnp.int32)
    (w_in_b,) = _gather_weights([_place_shard(w_in[0], 1, chip_idx, "place_in")], [1], "gather_w_in")
    placed = [_place_shard(w_out[0], 0, chip_idx, "place_out"), _place_shard(w_ffn_in[0], 1, chip_idx, "place_ffn_in"),
              _place_shard(w_ffn_out[0], 0, chip_idx, "place_ffn_out")]

    rr = lax.broadcasted_iota(jnp.int32, (BLK, BLK), 0) // CH
    cc = lax.broadcasted_iota(jnp.int32, (BLK, BLK), 1) // CH
    ws_b = jnp.where((rr >= cc)[None], w_s[0], 0.0).astype(BF16)
    bst = b_s[0].T
    lnw, lnb = v_ln_w, v_ln_b
    nw1, nw2, fw = norm1_w, norm2_w, final_norm_w.reshape(1, D)

    h1, proj = _proj_in(x2d, nw1, sc1, sh1, w_in_b, tp)
    ycat = _gmlp_fwd(proj, ws_b, bst, lnw, lnb)
    tables = _hgrn_tables()
    (ycat, o_pre, a_all, st_all), (w_out_b, w_fi_b, w_fo_b) = _hgrn_fwd(
        proj, lower_bounds, gn_w, ycat, tables, placed, [0, 1, 0])

    dycat, dx1, h2, act, dff, dgu, dmix, acc2 = _token_local(
        x2d, ycat, tgt, g1, nw2, sc2, sh2, g2, fw, w_out_b, w_fi_b, w_fo_b, tm)

    tt = min(WGRAD_TOKENS, T)
    order = jnp.concatenate([chip ^ jnp.arange(N_CHIPS, dtype=jnp.int32), chip_idx]).astype(jnp.int32)

    def by_core_half(g):
        return g.reshape(N_CHIPS, 2, g.shape[1] // 2, g.shape[2])

    def core_sums(g4, recv, names):
        return [_add_core_halves(a, b, c_idx, _row_block(a.shape[2]), "add_core_" + n) for a, b, n in zip(g4, recv, names)]

    def chip_sums(sums, slots, names):
        return [_add_chips(o, s, order, _row_block(s.shape[1]), "add_chips_" + n) for o, s, n in zip(sums, slots, names)]

    g_out = _wgrad(ycat, dmix, D, D, tt, "wgrad_out").reshape(N_CHIPS, D // N_CHIPS, D)
    g_fi = _wgrad(h2, dgu, D, FFB, tt, "wgrad_ffn_in")
    g_fo = _wgrad(act, dff, FFB, D, tt, "wgrad_ffn_out").reshape(N_CHIPS, DFF // N_CHIPS, D)
    late_names = ["out", "ffn_in", "ffn_out"]
    late_g4 = [by_core_half(g) for g in (g_out, g_fi, g_fo)]

    (dproj, dws, dbs, dln), late_recv = _gmlp_bwd(proj, dycat, ws_b, bst, lnw, lnb, late_g4)
    late_sums = core_sums(late_g4, late_recv, late_names)
    (dproj, dlb, dgn), late_slots = _hgrn_bwd(
        proj, o_pre, a_all, st_all, dycat, lower_bounds, gn_w, dproj, tables, late_sums)

    g_in = _wgrad(h1, dproj, D, D, tt, "wgrad_in")
    g_in = jnp.concatenate([g_in[2], g_in[0], g_in[1]], axis=1).reshape(D, N_CHIPS, DIN // N_CHIPS).transpose(1, 0, 2)
    in_g4 = [by_core_half(g_in)]
    in_sums = core_sums(in_g4, _exchange_core_halves(in_g4, "exchange_core_halves_in"), ["in"])
    (grad_x, acc1), in_slots = _proj_in_bwd(dproj, x2d, dx1, nw1, sc1, w_in_b, tp, in_sums)
    names = ["in"] + late_names
    halves = chip_sums(in_sums, in_slots, ["in"]) + chip_sums(late_sums, late_slots, late_names)
    sibling_halves = _share_halves(halves)

    big_w = [(w_in, m_w_in, v_w_in), (w_out, m_w_out, v_w_out), (w_ffn_in, m_w_ffn_in, v_w_ffn_in),
             (w_ffn_out, m_w_ffn_out, v_w_ffn_out)]
    big_out = []
    for mine, sib, (w, m, v), n in zip(halves, sibling_halves, big_w, names):
        res = _adamw_halves(w[0], mine, sib, m[0], v[0], c_idx, _row_block(mine.shape[0]), "adamw_" + n)
        big_out.append([r[None] for r in res])

    d_ada = jnp.stack([acc1[0], acc1[1], acc2[5], acc2[2], acc2[1], acc2[0]]).reshape(1, 6 * D)
    small_g = _pack_small(d_ada, acc1[2], acc2[3], acc2[4], dln[0], dln[1], dlb[0:2], dbs[:, 0:NH].T, dgn[0], dws)
    gathered = _all_gather_rows(small_g, "gather_small")
    sw = _pack_small(b_ada, norm1_w, norm2_w, final_norm_w, v_ln_w, v_ln_b, lower_bounds, b_s, gn_w, w_s)
    sm = _pack_small(m_b_ada, m_norm1_w, m_norm2_w, m_final_norm_w, m_v_ln_w, m_v_ln_b, m_lower_bounds, m_b_s, m_gn_w, m_w_s)
    sv = _pack_small(v_b_ada, v_norm1_w, v_norm2_w, v_final_norm_w, v_v_ln_w, v_v_ln_b, v_lower_bounds, v_b_s, v_gn_w, v_w_s)
    small = [_unpack_small(p) for p in _small_finalize(gathered, sw, sm, sv)]

    dada_all = gathered.reshape(N_DEV, SMALL_ROWS, D)[:, 0:6, :].reshape(N_DEV, 6 * D)
    dada = lax.dynamic_slice_in_dim(dada_all, chip * n_ada, n_ada, axis=1)
    ada_out = [o[None] for o in _ada_wgrad_adam(cact.T, dada, w_ada[0], m_w_ada[0], v_w_ada[0])]

    loss = lax.psum(jnp.sum(acc2[6]), ("x", "y", "c"))

    order_names = ['w_ada', 'b_ada', 'norm1_w', 'w_in', 'w_s', 'b_s', 'v_ln_w', 'v_ln_b', 'lower_bounds', 'gn_w',
                   'w_out', 'norm2_w', 'w_ffn_in', 'w_ffn_out', 'final_norm_w']
    big_idx = {'w_in': 0, 'w_out': 1, 'w_ffn_in': 2, 'w_ffn_out': 3}
    outs = [loss, grad_x.reshape(1, T, D)]
    for kind in range(4):
        for n in order_names:
            if n == 'w_ada':
                outs.append(ada_out[kind])
            elif n in big_idx:
                outs.append(big_out[big_idx[n]][kind])
            else:
                outs.append(small[kind][n])
    return tuple(outs)
```

```python
import functools

import jax
import jax.numpy as jnp
import numpy as np
from jax import lax
from jax.experimental import pallas as pl
from jax.experimental.pallas import tpu as pltpu

F32 = jnp.float32
BF16 = jnp.bfloat16
SDS = jax.ShapeDtypeStruct
MESH = pl.DeviceIdType.MESH
HIGHEST = lax.Precision.HIGHEST

D = 1024
DG = 512
DH = 512
NH = 4
HD = 128
BLK = 128
CH = 64
DFF = 2816
DIN = 3072
FFB = 1408
LEVELS = (64, 32, 16, 8, 4, 2)
HGRN_CHUNKS_PER_STEP = 4
GMLP_ROWS_PER_STEP = 512
TOKEN_TILE = 256
PROJ_TILE = 512
WGRAD_TOKENS = 2048
N_CHIPS = 4
N_DEV = 8
EPS = 1e-6
LR, B1, B2, AEPS, WD, STEP = 0.001, 0.9, 0.999, 1e-08, 0.01, 10

NT = (((1,), (1,)), ((), ()))
TN = (((0,), (0,)), ((), ()))


def _full(shape):
    nd = len(shape)
    return pl.BlockSpec(shape, lambda *_: (0,) * nd)


ADA_SH1, ADA_SC1, ADA_G1, ADA_SH2, ADA_SC2, ADA_G2 = range(6)


def _ada_part(k):
    return pl.BlockSpec((1, D), lambda *_: (0, k))


def _resident(shape):
    nd = len(shape)
    return pl.BlockSpec(shape, lambda *_: (0,) * nd, pipeline_mode=pl.Buffered(1))


def _arb(n=1):
    return pltpu.CompilerParams(dimension_semantics=("arbitrary",) * n)


def _dot(a, b, dims=None, precision=None):
    if dims is None:
        return jnp.dot(a, b, preferred_element_type=F32, precision=precision)
    return lax.dot_general(a, b, dims, preferred_element_type=F32, precision=precision)


def _sigmoid(x):
    return jax.nn.sigmoid(x)


def _gelu_parts(x):
    cdf = 0.5 * (1.0 + lax.erf(x * 0.7071067811865476))
    pdf = jnp.exp(-0.5 * x * x) * 0.3989422804014327
    return x * cdf, cdf + x * pdf


def _rms(x):
    return lax.rsqrt(jnp.mean(x * x, axis=-1, keepdims=True) + EPS)


def _rms_bwd(xhat, r, gw):
    return r * (gw - xhat * jnp.mean(xhat * gw, axis=-1, keepdims=True))


def _lower_bound(lbp_ref):
    l0, l1 = lbp_ref[0:1, :], lbp_ref[1:2, :]
    m = jnp.maximum(l0, l1)
    e0, e1 = jnp.exp(l0 - m), jnp.exp(l1 - m)
    return e0 / (e0 + e1), e1 / (e0 + e1)


def _proj_in(x, nw, sc, sh, w_in_b, tm):
    T = x.shape[0]

    def body(x_ref, nw_ref, sc_ref, sh_ref, w_ref, h_ref, p_ref):
        xv = x_ref[...]
        h = ((xv * _rms(xv)) * nw_ref[...]) * (1.0 + sc_ref[...]) + sh_ref[...]
        hb = h.astype(BF16)
        h_ref[...] = hb
        p_ref[...] = _dot(hb, w_ref[...])

    row = lambda i: (i, 0)
    return pl.pallas_call(
        body, grid=(T // tm,),
        in_specs=[pl.BlockSpec((tm, D), row), _full((1, D)), _ada_part(ADA_SC1), _ada_part(ADA_SH1), _resident((D, DIN))],
        out_specs=[pl.BlockSpec((tm, D), row), pl.BlockSpec((tm, DIN), row)],
        out_shape=[SDS((T, D), BF16), SDS((T, DIN), F32)],
        compiler_params=_arb(), name="proj_in")(x, nw, sc, sh, w_in_b)


def _gmlp_common(u, v, lnw, lnb, ws_ref, bst_ref):
    ug, dug = _gelu_parts(u)
    vg, dvg = _gelu_parts(v)
    mu = jnp.mean(vg, axis=-1, keepdims=True)
    vc = vg - mu
    rstd = lax.rsqrt(jnp.mean(vc * vc, axis=-1, keepdims=True) + EPS)
    vhat = vc * rstd
    vn = vhat * lnw + lnb
    vnb = vn.astype(BF16)
    mixed = []
    for h in range(NH):
        sl = slice(h * HD, (h + 1) * HD)
        mixed.append(_dot(ws_ref[h], vnb[:, sl]) + bst_ref[:, h:h + 1])
    return ug, dug, dvg, rstd, vhat, vnb, jnp.concatenate(mixed, axis=1)


def _gmlp_fwd(proj, ws_b, bst, lnw, lnb):
    T = proj.shape[0]
    rows = min(GMLP_ROWS_PER_STEP, T)

    def body(u_ref, v_ref, ws_ref, bst_ref, lnw_ref, lnb_ref, y_ref):
        for bi in range(rows // BLK):
            rs = slice(bi * BLK, (bi + 1) * BLK)
            ug, _, _, _, _, _, mixed = _gmlp_common(u_ref[rs, :], v_ref[rs, :], lnw_ref[...], lnb_ref[...], ws_ref, bst_ref)
            y_ref[rs, :] = (ug * mixed).astype(BF16)

    return pl.pallas_call(
        body, grid=(T // rows,),
        in_specs=[pl.BlockSpec((rows, DG), lambda i: (i, 0)), pl.BlockSpec((rows, DG), lambda i: (i, 1)),
                  _full((NH, BLK, BLK)), _full((BLK, NH)), _full((1, DG)), _full((1, DG))],
        out_specs=pl.BlockSpec((rows, DG), lambda i: (i, 0)),
        out_shape=SDS((T, D), BF16),
        compiler_params=_arb(), name="gmlp_fwd")(proj, proj, ws_b, bst, lnw, lnb)


def _hgrn_tables():
    t = np.arange(CH)[:, None]
    j = np.arange(CH)[None, :]
    blocks = [j <= t, j > t]
    masks = []
    for n in LEVELS:
        mid = t - t % n + n // 2
        blocks.append(np.where(t >= mid, (j >= mid) & (j <= t), (j > t) & (j < mid)))
        masks.append((t // n == j // n) & (t % n >= n // 2) & (j % n < n // 2))
    w = np.concatenate(blocks, axis=0).astype(np.float32)
    m = np.stack(masks).astype(np.float32)
    return (jnp.asarray(w, BF16), jnp.asarray(w.T, BF16), jnp.asarray(m), jnp.asarray(m + m.transpose(0, 2, 1)))


def _split_dot(w, x, parts):
    acc = None
    for _ in range(parts):
        piece = x.astype(BF16)
        term = _dot(w, piece)
        acc = term if acc is None else acc + term
        x = x - piece.astype(F32)
    return acc


def _hgrn_gates(q, fl, lb, omlb, w_ref):
    sq = _sigmoid(q)
    qf = q * sq
    sig = _sigmoid(fl)
    f = lb + omlb * sig
    k = 1.0 - f
    e = jnp.exp(_split_dot(w_ref[...], jnp.log(f), 3))
    return sq, qf, sig, f, k, e


def _level_factor(e, li, sl, row, qh, kh):
    el = e[(2 + li) * CH:(3 + li) * CH, sl]
    up = (row & (LEVELS[li] // 2)) != 0
    return el, up, el * jnp.where(up, qh, kh)


def _hgrn_fwd(proj, lower_bounds, gn_w, ycat, tables, placed, axes):
    T = proj.shape[0]
    nc = T // CH
    nch = min(HGRN_CHUNKS_PER_STEP, nc)
    steps = nc // nch
    w_st, _, masks, _ = tables
    nw = len(placed)
    pass_step = (5 * steps) // 8

    def body(*refs):
        q_ref, f_ref, i_ref, g_ref, lbp_ref, gn_ref, w_ref, m_ref = refs[:8]
        y_ref, o_ref, a_ref, st_ref = refs[9 + nw:13 + nw]
        s_scr, send_sems, recv_sems = refs[13 + 2 * nw:]
        gather = _WeightGather(refs[13 + nw:13 + 2 * nw], axes, send_sems, recv_sems)
        step = pl.program_id(0)

        @pl.when(step == 0)
        def _():
            gather.start()
            s_scr[...] = jnp.zeros_like(s_scr)

        @pl.when(step == pass_step)
        def _():
            gather.forward()

        lb, omlb = _lower_bound(lbp_ref)
        row = lax.broadcasted_iota(jnp.int32, (CH, 1), 0)
        eye = lax.broadcasted_iota(jnp.int32, (CH, CH), 0) == lax.broadcasted_iota(jnp.int32, (CH, CH), 1)
        pre = []
        for ci in range(nch):
            rs = slice(ci * CH, (ci + 1) * CH)
            _, qf, _, _, k, e = _hgrn_gates(q_ref[rs, :], f_ref[rs, :], lb, omlb, w_ref)
            mats = []
            for h in range(NH):
                sl = slice(h * HD, (h + 1) * HD)
                qh, kh = qf[:, sl], k[:, sl]
                a = jnp.where(eye, jnp.sum(qh * kh, axis=-1, keepdims=True), 0.0)
                for li in range(len(LEVELS)):
                    _, _, y = _level_factor(e, li, sl, row, qh, kh)
                    yb = y.astype(BF16)
                    a = a + m_ref[li] * _dot(yb, yb, NT)
                a_ref[ci, h] = a
                mats.append(a.astype(BF16))
            eb = e[0:CH]
            pre.append(((qf * eb).astype(BF16), eb[CH - 1:CH, :], (k * e[CH:2 * CH]).astype(BF16), mats))
        for ci in range(nch):
            rs = slice(ci * CH, (ci + 1) * CH)
            qe, ebl, kd, mats = pre[ci]
            v = i_ref[rs, :]
            g = g_ref[rs, :]
            for h in range(NH):
                sl = slice(h * HD, (h + 1) * HD)
                st0 = s_scr[h]
                st_ref[ci, h] = st0
                vb = v[:, sl].astype(BF16)
                o = _dot(qe[:, sl], st0.astype(BF16), NT) + _dot(mats[h], vb)
                s_scr[h] = st0 * ebl[:, sl] + _dot(vb, kd[:, sl], TN)
                o_ref[rs, sl] = o
                gh = g[:, sl]
                y_ref[rs, sl] = (((o * _rms(o)) * gn_ref[...]) * (gh * _sigmoid(gh))).astype(BF16)

        @pl.when(step == steps - 1)
        def _():
            gather.finish()

    blk = lambda j: pl.BlockSpec((nch * CH, DH), lambda c: (c, j))
    anyspec = pl.BlockSpec(memory_space=pl.ANY)
    res = pl.pallas_call(
        body, grid=(steps,),
        in_specs=[blk(2), blk(3), blk(4), blk(5), _full((2, DH)), _full((1, HD)),
                  _full(w_st.shape), _full(masks.shape), anyspec] + [anyspec] * nw,
        out_specs=[pl.BlockSpec((nch * CH, DH), lambda c: (c, 1)),
                   pl.BlockSpec((nch * CH, DH), lambda c: (c, 0)),
                   pl.BlockSpec((nch, NH, CH, CH), lambda c: (c, 0, 0, 0)),
                   pl.BlockSpec((nch, NH, HD, HD), lambda c: (c, 0, 0, 0))] + [anyspec] * nw,
        out_shape=[SDS((T, D), BF16), SDS((T, DH), F32), SDS((nc, NH, CH, CH), F32), SDS((nc, NH, HD, HD), F32)]
        + [SDS(a.shape, a.dtype) for a in placed],
        scratch_shapes=[pltpu.VMEM((NH, HD, HD), F32)] + _gather_sems(nw),
        input_output_aliases={8: 0, **{9 + i: 4 + i for i in range(nw)}},
        compiler_params=_arb(), name="hgrn_fwd")(proj, proj, proj, proj, lower_bounds, gn_w, w_st, masks, ycat, *placed)
    return res[:4], res[4:]


def _token_local(x, ycat, tgt, g1, nw2, sc2, sh2, g2, fw, w_out_b, w_fi_b, w_fo_b, tm):
    T = x.shape[0]
    inv_d = 1.0 / D

    def body(x_ref, y_ref, t_ref, g1_ref, nw2_ref, sc2_ref, sh2_ref, g2_ref, fw_ref, wo_ref, wfi_ref, wfo_ref,
             dy_ref, dx1_ref, h2_ref, act_ref, dff_ref, dgu_ref, dmix_ref, acc_ref):
        @pl.when(pl.program_id(0) == 0)
        def _():
            acc_ref[...] = jnp.zeros_like(acc_ref)

        def acc(row, val):
            acc_ref[row:row + 1, :] += jnp.sum(val, axis=0, keepdims=True)

        g1v, g2v = g1_ref[...], g2_ref[...]
        mix = _dot(y_ref[...], wo_ref[...])
        x1 = x_ref[...] + g1v * mix
        r2 = _rms(x1)
        xh2 = x1 * r2
        n2 = xh2 * nw2_ref[...]
        osc2 = 1.0 + sc2_ref[...]
        h2b = (n2 * osc2 + sh2_ref[...]).astype(BF16)
        h2_ref[...] = h2b
        ff = jnp.zeros((tm, D), F32)
        saved = []
        for kb in range(DFF // FFB):
            gate = _dot(h2b, wfi_ref[:, kb * FFB:(kb + 1) * FFB])
            up = _dot(h2b, wfi_ref[:, DFF + kb * FFB:DFF + (kb + 1) * FFB])
            sg = _sigmoid(gate)
            actb = (gate * sg * up).astype(BF16)
            act_ref[:, kb * FFB:(kb + 1) * FFB] = actb
            ff = ff + _dot(actb, wfo_ref[kb * FFB:(kb + 1) * FFB, :])
            saved.append((gate, up, sg))
        x2 = x1 + g2v * ff
        r3 = _rms(x2)
        xh3 = x2 * r3
        err = xh3 * fw_ref[...] - t_ref[...]
        acc(6, (0.5 * inv_d) * err * err)
        dy = err * inv_d
        acc(4, dy * xh3)
        dx2 = _rms_bwd(xh3, r3, dy * fw_ref[...])
        acc(0, dx2 * ff)
        dffb = (dx2 * g2v).astype(BF16)
        dff_ref[...] = dffb
        dh2 = jnp.zeros((tm, D), F32)
        for kb in range(DFF // FFB):
            gate, up, sg = saved[kb]
            da = _dot(dffb, wfo_ref[kb * FFB:(kb + 1) * FFB, :], NT)
            dgate = (da * up * (sg * (1.0 + gate * (1.0 - sg)))).astype(BF16)
            dup = (da * gate * sg).astype(BF16)
            dgu_ref[:, kb * FFB:(kb + 1) * FFB] = dgate
            dgu_ref[:, DFF + kb * FFB:DFF + (kb + 1) * FFB] = dup
            dh2 = dh2 + _dot(dgate, wfi_ref[:, kb * FFB:(kb + 1) * FFB], NT)
            dh2 = dh2 + _dot(dup, wfi_ref[:, DFF + kb * FFB:DFF + (kb + 1) * FFB], NT)
        acc(2, dh2)
        acc(1, dh2 * n2)
        dn2 = dh2 * osc2
        acc(3, dn2 * xh2)
        dx1 = dx2 + _rms_bwd(xh2, r2, dn2 * nw2_ref[...])
        acc(5, dx1 * mix)
        dmixb = (dx1 * g1v).astype(BF16)
        dmix_ref[...] = dmixb
        dy_ref[...] = _dot(dmixb, wo_ref[...], NT)
        dx1_ref[...] = dx1

    row = lambda i: (i, 0)
    vec = _full((1, D))
    return pl.pallas_call(
        body, grid=(T // tm,),
        in_specs=[pl.BlockSpec((tm, D), row), pl.BlockSpec((tm, D), row), pl.BlockSpec((tm, D), row),
                  _ada_part(ADA_G1), vec, _ada_part(ADA_SC2), _ada_part(ADA_SH2), _ada_part(ADA_G2), vec,
                  _resident((D, D)), _resident((D, 2 * DFF)), _resident((DFF, D))],
        out_specs=[pl.BlockSpec((tm, D), row), pl.BlockSpec((tm, D), row), pl.BlockSpec((tm, D), row),
                   pl.BlockSpec((tm, DFF), row), pl.BlockSpec((tm, D), row), pl.BlockSpec((tm, 2 * DFF), row),
                   pl.BlockSpec((tm, D), row), _full((8, D))],
        out_shape=[SDS((T, D), F32), SDS((T, D), F32), SDS((T, D), BF16), SDS((T, DFF), BF16), SDS((T, D), BF16),
                   SDS((T, 2 * DFF), BF16), SDS((T, D), BF16), SDS((8, D), F32)],
        compiler_params=_arb(), name="token_local")(x, ycat, tgt, g1, nw2, sc2, sh2, g2, fw, w_out_b, w_fi_b, w_fo_b)


def _gmlp_bwd(proj, dycat, ws_b, bst, lnw, lnb, grads):
    T = proj.shape[0]
    rows = min(GMLP_ROWS_PER_STEP, T)
    nb = T // rows
    nw = len(grads)

    def body(*refs):
        u_ref, v_ref, dy_ref, ws_ref, bst_ref, lnw_ref, lnb_ref = refs[:7]
        dp_ref, dws_ref, dbs_ref, dln_ref = refs[7 + nw:11 + nw]
        dbs_acc, send_sems, recv_sems = refs[11 + 2 * nw:]
        exchange = _CoreExchange(refs[7:7 + nw], refs[11 + nw:11 + 2 * nw], send_sems, recv_sems)
        i = pl.program_id(0)

        @pl.when(i == 0)
        def _():
            exchange.start()
            dws_ref[...] = jnp.zeros_like(dws_ref)
            dln_ref[...] = jnp.zeros_like(dln_ref)
            dbs_acc[...] = jnp.zeros_like(dbs_acc)

        r = lax.broadcasted_iota(jnp.int32, (BLK, BLK), 0) // CH
        c = lax.broadcasted_iota(jnp.int32, (BLK, BLK), 1) // CH
        for bi in range(rows // BLK):
            rs = slice(bi * BLK, (bi + 1) * BLK)
            ug, dug, dvg, rstd, vhat, vnb, mixed = _gmlp_common(
                u_ref[rs, :], v_ref[rs, :], lnw_ref[...], lnb_ref[...], ws_ref, bst_ref)
            dya = dy_ref[rs, :]
            dp_ref[rs, 0:DG] = (dya * mixed * dug).astype(BF16)
            dmixed = dya * ug
            dbs_acc[...] += dmixed
            dmb = dmixed.astype(BF16)
            dvn = []
            for h in range(NH):
                sl = slice(h * HD, (h + 1) * HD)
                dws_ref[h * BLK:(h + 1) * BLK, :] += jnp.where(r >= c, _dot(dmb[:, sl], vnb[:, sl], NT), 0.0)
                dvn.append(_dot(ws_ref[h], dmb[:, sl], TN))
            dvn = jnp.concatenate(dvn, axis=1)
            dln_ref[0:1, :] += jnp.sum(dvn * vhat, axis=0, keepdims=True)
            dln_ref[1:2, :] += jnp.sum(dvn, axis=0, keepdims=True)
            dvh = dvn * lnw_ref[...]
            dvgel = rstd * (dvh - jnp.mean(dvh, axis=-1, keepdims=True) - vhat * jnp.mean(dvh * vhat, axis=-1, keepdims=True))
            dp_ref[rs, DG:2 * DG] = (dvgel * dvg).astype(BF16)

        @pl.when(i == nb - 1)
        def _():
            head = lax.broadcasted_iota(jnp.int32, (8, BLK), 0)
            ones = jnp.ones((8, HD), F32)
            out = jnp.zeros((8, BLK), F32)
            for h in range(NH):
                sums = _dot(ones, dbs_acc[:, h * HD:(h + 1) * HD], NT, precision=HIGHEST)
                out = out + jnp.where(head == h, sums, 0.0)
            dbs_ref[...] = out
            exchange.finish()

    anyspec = pl.BlockSpec(memory_space=pl.ANY)
    res = pl.pallas_call(
        body, grid=(nb,),
        in_specs=[pl.BlockSpec((rows, DG), lambda i: (i, 0)), pl.BlockSpec((rows, DG), lambda i: (i, 1)),
                  pl.BlockSpec((rows, DG), lambda i: (i, 0)),
                  _full((NH, BLK, BLK)), _full((BLK, NH)), _full((1, DG)), _full((1, DG))] + [anyspec] * nw,
        out_specs=[pl.BlockSpec((rows, 2 * DG), lambda i: (i, 2)), _full((NH * BLK, BLK)), _full((8, BLK)), _full((8, DG))]
        + [anyspec] * nw,
        out_shape=[SDS((T, DIN), BF16), SDS((NH * BLK, BLK), F32), SDS((8, BLK), F32), SDS((8, DG), F32)]
        + _core_exchange_shapes(grads),
        scratch_shapes=[pltpu.VMEM((BLK, DG), F32)] + _core_exchange_sems(nw),
        compiler_params=_arb(), name="gmlp_bwd")(proj, proj, dycat, ws_b, bst, lnw, lnb, *grads)
    return res[:4], res[4:]


def _hgrn_bwd(proj, o_pre, a_all, st_all, dycat, lower_bounds, gn_w, dproj, tables, sums):
    T = proj.shape[0]
    nc = T // CH
    nch = min(HGRN_CHUNKS_PER_STEP, nc)
    steps = nc // nch
    w_st, w_st_t, _, masks_sym = tables
    n_lev = len(LEVELS)
    nw = len(sums)

    def body(*refs):
        q_ref, f_ref, i_ref, g_ref, o_ref, a_ref, st_ref, dy_ref, lbp_ref, gn_ref, w_ref, wt_ref, ms_ref = refs[:13]
        dp_ref, dlb_ref, dgn_ref = refs[14 + nw:17 + nw]
        ds_scr, dx_scr, send_sems, recv_sems = refs[17 + 2 * nw:]
        exchange = _ChipExchange(refs[14:14 + nw], refs[17 + nw:17 + 2 * nw], send_sems, recv_sems)
        i = pl.program_id(0)

        @pl.when(i == 0)
        def _():
            exchange.start()
            ds_scr[...] = jnp.zeros_like(ds_scr)
            dlb_ref[...] = jnp.zeros_like(dlb_ref)
            dgn_ref[...] = jnp.zeros_like(dgn_ref)

        lb, omlb = _lower_bound(lbp_ref)
        row = lax.broadcasted_iota(jnp.int32, (CH, 1), 0)
        eye = lax.broadcasted_iota(jnp.int32, (CH, CH), 0) == lax.broadcasted_iota(jnp.int32, (CH, CH), 1)
        lower = lax.broadcasted_iota(jnp.int32, (CH, CH), 0) > lax.broadcasted_iota(jnp.int32, (CH, CH), 1)
        dgn = jnp.zeros((1, HD), F32)
        pre = []
        for ci in range(nch):
            rs = slice(ci * CH, (ci + 1) * CH)
            q = q_ref[rs, :]
            v = i_ref[rs, :]
            g = g_ref[rs, :]
            sq, qf, sig, f, k, e = _hgrn_gates(q, f_ref[rs, :], lb, omlb, w_ref)
            eb = e[0:CH]
            ekd = e[CH:2 * CH]
            kd = k * ekd
            qe = qf * eb
            dob_h, dqe_h, dqf_h, dki_h, dv_h, dg_h = [], [], [], [], [], []
            for h in range(NH):
                sl = slice(h * HD, (h + 1) * HD)
                o = o_ref[rs, sl]
                ro = _rms(o)
                oh = o * ro
                gh = g[:, sl]
                sg = _sigmoid(gh)
                dyb = dy_ref[rs, sl]
                dg_h.append(dyb * (oh * gn_ref[...]) * (sg * (1.0 + gh * (1.0 - sg))))
                don = dyb * (gh * sg)
                dgn = dgn + jnp.sum(don * oh, axis=0, keepdims=True)
                dob = _rms_bwd(oh, ro, don * gn_ref[...]).astype(BF16)
                vb = v[:, sl].astype(BF16)
                qh, kh = qf[:, sl], k[:, sl]
                dqe = _dot(dob, st_ref[ci, h].astype(BF16))
                da = _dot(dob, vb, NT)
                ddiag = jnp.sum(jnp.where(eye, da, 0.0), axis=-1, keepdims=True)
                dsym = jnp.where(lower, da, _dot(vb, dob, NT))
                upper_part = jnp.zeros((CH, HD), F32)
                both = jnp.zeros((CH, HD), F32)
                for li in range(n_lev):
                    el, up, y = _level_factor(e, li, sl, row, qh, kh)
                    dyv = _dot((ms_ref[li] * dsym).astype(BF16), y.astype(BF16))
                    dx_scr[ci, (2 + li) * CH:(3 + li) * CH, sl] = dyv * y
                    dye = dyv * el
                    upper_part = upper_part + jnp.where(up, dye, 0.0)
                    both = both + dye
                dob_h.append(dob)
                dqe_h.append(dqe)
                dqf_h.append(dqe * eb[:, sl] + ddiag * kh + upper_part)
                dki_h.append(ddiag * qh + (both - upper_part))
                dv_h.append(_dot(a_ref[ci, h].astype(BF16), dob, TN))
            dp_ref[rs, 0:DH] = (jnp.concatenate(dqf_h, axis=1) * (sq * (1.0 + q * (1.0 - sq)))).astype(BF16)
            dp_ref[rs, 3 * DH:4 * DH] = jnp.concatenate(dg_h, axis=1).astype(BF16)
            pre.append((v, sig, f, eb, ekd, kd, qe, dob_h, jnp.concatenate(dqe_h, axis=1), dki_h, dv_h))
        dgn_ref[0:1, :] += dgn
        for ci in reversed(range(nch)):
            rs = slice(ci * CH, (ci + 1) * CH)
            v, sig, f, eb, ekd, kd, qe, dob_h, dqe, dki_h, dv_h = pre[ci]
            ebl = eb[CH - 1:CH, :]
            dbl_h, dkd_h, dv2_h = [], [], []
            for h in range(NH):
                sl = slice(h * HD, (h + 1) * HD)
                dst1 = ds_scr[h]
                dst1b = dst1.astype(BF16)
                ds_scr[h] = dst1 * ebl[:, sl] + _dot(dob_h[h], qe[:, sl].astype(BF16), TN)
                dbl_h.append(ebl[:, sl] * jnp.sum(st_ref[ci, h] * dst1, axis=0, keepdims=True))
                dkd_h.append(_dot(v[:, sl].astype(BF16), dst1b))
                dv2_h.append(dv_h[h] + _dot(kd[:, sl].astype(BF16), dst1b, NT))
            dkd = jnp.concatenate(dkd_h, axis=1)
            dx_scr[ci, 0:CH, :] = dqe * qe + jnp.where(row == CH - 1, jnp.concatenate(dbl_h, axis=1), 0.0)
            dx_scr[ci, CH:2 * CH, :] = dkd * kd
            dlf = _split_dot(wt_ref[...], dx_scr[ci], 2)
            df = dlf / f - (dkd * ekd + jnp.concatenate(dki_h, axis=1))
            dlb_ref[0:1, :] += jnp.sum(df * (1.0 - sig), axis=0, keepdims=True)
            dp_ref[rs, DH:2 * DH] = (df * omlb * sig * (1.0 - sig)).astype(BF16)
            dp_ref[rs, 2 * DH:3 * DH] = jnp.concatenate(dv2_h, axis=1).astype(BF16)

        @pl.when(i == steps - 1)
        def _():
            gl = dlb_ref[0:1, :] * lb * omlb
            dlb_ref[0:1, :] = gl
            dlb_ref[1:2, :] = -gl
            exchange.finish()

    rev = lambda j: pl.BlockSpec((nch * CH, DH), lambda c: (steps - 1 - c, j))
    anyspec = pl.BlockSpec(memory_space=pl.ANY)
    res = pl.pallas_call(
        body, grid=(steps,),
        in_specs=[rev(2), rev(3), rev(4), rev(5), rev(0),
                  pl.BlockSpec((nch, NH, CH, CH), lambda c: (steps - 1 - c, 0, 0, 0)),
                  pl.BlockSpec((nch, NH, HD, HD), lambda c: (steps - 1 - c, 0, 0, 0)),
                  rev(1), _full((2, DH)), _full((1, HD)),
                  _full(w_st.shape), _full(w_st_t.shape), _full(masks_sym.shape),
                  anyspec] + [anyspec] * nw,
        out_specs=[pl.BlockSpec((nch * CH, 4 * DH), lambda c: (steps - 1 - c, 0)), _full((8, DH)), _full((8, HD))]
        + [anyspec] * nw,
        out_shape=[SDS((T, DIN), BF16), SDS((8, DH), F32), SDS((8, HD), F32)] + _slot_shapes(sums),
        scratch_shapes=[pltpu.VMEM((NH, HD, HD), F32), pltpu.VMEM((nch, (2 + n_lev) * CH, DH), F32)] + _exchange_sems(nw),
        input_output_aliases={13: 0},
        compiler_params=_arb(), name="hgrn_bwd")(proj, proj, proj, proj, o_pre, a_all, st_all, dycat, lower_bounds, gn_w,
                                                 w_st, w_st_t, masks_sym, dproj, *sums)
    return res[:3], res[3:]


def _proj_in_bwd(dproj, x, dx1, nw, sc, w_in_b, tm, sums):
    T = x.shape[0]
    ns = len(sums)
    steps = T // tm

    def body(*refs):
        dp_ref, x_ref, dx1_ref, nw_ref, sc_ref, w_ref = refs[:6]
        gx_ref, acc_ref = refs[6 + ns:8 + ns]
        exchange = _ChipExchange(refs[6:6 + ns], refs[8 + ns:8 + 2 * ns], *refs[8 + 2 * ns:])

        @pl.when(pl.program_id(0) == 0)
        def _():
            exchange.start()
            acc_ref[...] = jnp.zeros_like(acc_ref)

        dh = _dot(dp_ref[:, 0:4 * DH], w_ref[:, 2 * DG:DIN], NT) + _dot(dp_ref[:, 4 * DH:DIN], w_ref[:, 0:2 * DG], NT)
        xv = x_ref[...]
        r = _rms(xv)
        xh = xv * r
        n1 = xh * nw_ref[...]
        acc_ref[0:1, :] += jnp.sum(dh, axis=0, keepdims=True)
        acc_ref[1:2, :] += jnp.sum(dh * n1, axis=0, keepdims=True)
        dn = dh * (1.0 + sc_ref[...])
        acc_ref[2:3, :] += jnp.sum(dn * xh, axis=0, keepdims=True)
        gx_ref[...] = dx1_ref[...] + _rms_bwd(xh, r, dn * nw_ref[...])

        @pl.when(pl.program_id(0) == steps - 1)
        def _():
            exchange.finish()

    row = lambda i: (i, 0)
    anyspec = pl.BlockSpec(memory_space=pl.ANY)
    res = pl.pallas_call(
        body, grid=(steps,),
        in_specs=[pl.BlockSpec((tm, DIN), row), pl.BlockSpec((tm, D), row), pl.BlockSpec((tm, D), row),
                  _full((1, D)), _ada_part(ADA_SC1), _resident((D, DIN))] + [anyspec] * ns,
        out_specs=[pl.BlockSpec((tm, D), row), _full((8, D))] + [anyspec] * ns,
        out_shape=[SDS((T, D), F32), SDS((8, D), F32)] + _slot_shapes(sums),
        scratch_shapes=_exchange_sems(ns),
        compiler_params=_arb(), name="proj_in_bwd")(dproj, x, dx1, nw, sc, w_in_b, *sums)
    return res[:2], res[2:]


def _wgrad(a, b, bk, bn, tt, name):
    T, K = a.shape
    N = b.shape[1]
    nn, nk, nt = N // bn, K // bk, T // tt
    bmap = lambda n, k, t: (t, n)

    def body(a_ref, b_ref, o_ref):
        @pl.when(pl.program_id(2) == 0)
        def _():
            o_ref[...] = jnp.zeros_like(o_ref)

        o_ref[0] += _dot(a_ref[...], b_ref[...], TN)

    return pl.pallas_call(
        body, grid=(nn, nk, nt),
        in_specs=[pl.BlockSpec((tt, bk), lambda n, k, t: (t, k)), pl.BlockSpec((tt, bn), bmap)],
        out_specs=pl.BlockSpec((1, bk, bn), lambda n, k, t: (n, k, 0)),
        out_shape=SDS((nn, K, bn), F32),
        compiler_params=_arb(3), name=name)(a, b)


def _adam_math(w, g, m, v):
    m = B1 * m + (1.0 - B1) * g
    v = B2 * v + (1.0 - B2) * (g * g)
    m_hat = m / (1.0 - B1 ** STEP)
    v_hat = v / (1.0 - B2 ** STEP)
    return -LR * (m_hat / (jnp.sqrt(v_hat) + AEPS) + WD * w), m, v


def _adamw_halves(w, mine, sibling, m, v, c_idx, rb, name):
    R, C = w.shape
    nb = (R // 2) // rb

    def body(c_ref, w_ref, a_ref, b_ref, m_ref, v_ref, g_out, d_out, m_out, v_out):
        g = jnp.where(pl.program_id(0) == c_ref[0], a_ref[...], b_ref[...])
        g_out[...] = g
        d_out[...], m_out[...], v_out[...] = _adam_math(w_ref[...], g, m_ref[...], v_ref[...])

    whole = pl.BlockSpec((rb, C), lambda hh, i, cr: (hh * nb + i, 0))
    half = pl.BlockSpec((rb, C), lambda hh, i, cr: (i, 0))
    return pl.pallas_call(
        body,
        grid_spec=pltpu.PrefetchScalarGridSpec(
            num_scalar_prefetch=1, grid=(2, nb), in_specs=[whole, half, half, whole, whole], out_specs=[whole] * 4),
        out_shape=[SDS((R, C), F32)] * 4, compiler_params=_arb(2), name=name)(c_idx, w, mine, sibling, m, v)


def _ada_forward(c_all, w_ada):
    n = w_ada.shape[1]

    def body(c_ref, w_ref, ca_ref, p_ref):
        cv = c_ref[...]
        ca = cv * _sigmoid(cv)
        ca_ref[...] = ca
        p_ref[...] = _dot(ca, w_ref[...], precision=HIGHEST)

    return pl.pallas_call(
        body, grid=(n // 512,),
        in_specs=[_full((N_DEV, D)), pl.BlockSpec((D, 512), lambda i: (0, i))],
        out_specs=[_full((N_DEV, D)), pl.BlockSpec((N_DEV, 512), lambda i: (0, i))],
        out_shape=[SDS((N_DEV, D), F32), SDS((N_DEV, n), F32)],
        compiler_params=_arb(), name="ada_forward")(c_all, w_ada)


def _ada_wgrad_adam(cact_t, dada_all, w, m, v, chip_idx):
    R, C = w.shape
    rb = 256

    def body(j_ref, c_ref, d_ref, w_ref, m_ref, v_ref, g_out, d_out, m_out, v_out):
        g = _dot(c_ref[...], d_ref[...], precision=HIGHEST)
        g_out[...] = g
        d_out[...], m_out[...], v_out[...] = _adam_math(w_ref[...], g, m_ref[...], v_ref[...])

    spec = pl.BlockSpec((rb, C), lambda i, j: (i, 0))
    return pl.pallas_call(
        body,
        grid_spec=pltpu.PrefetchScalarGridSpec(
            num_scalar_prefetch=1, grid=(R // rb,),
            in_specs=[pl.BlockSpec((rb, N_DEV), lambda i, j: (i, 0)), pl.BlockSpec((N_DEV, C), lambda i, j: (0, j[0])),
                      spec, spec, spec],
            out_specs=[spec] * 4),
        out_shape=[SDS((R, C), F32)] * 4,
        compiler_params=_arb(), name="ada_wgrad_adam")(chip_idx, cact_t, dada_all, w, m, v)


SMALL_NAMES = ('b_ada', 'norm1_w', 'norm2_w', 'final_norm_w', 'v_ln_w', 'v_ln_b', 'lower_bounds', 'gn_w', 'b_s', 'w_s')


def _small_finalize(gathered, params, moms, vels):
    n_in = len(gathered)

    def body(*refs):
        acc1, acc2, dln, dlb, dgn, dbs, dws = refs[:n_in]
        prm = [dict(zip(SMALL_NAMES, refs[n_in + k * 10:n_in + (k + 1) * 10])) for k in range(3)]
        outs = [dict(zip(SMALL_NAMES, refs[n_in + 30 + k * 10:n_in + 30 + (k + 1) * 10])) for k in range(4)]
        loss_ref, dada_ref = refs[n_in + 70:n_in + 72]

        def dev_sum(ref, first, n):
            per = ref.shape[0] // N_DEV
            g = ref[first:first + n, :]
            for dev in range(1, N_DEV):
                g = g + ref[dev * per + first:dev * per + first + n, :]
            return g

        def update(n, g, cols=slice(None)):
            outs[0][n][:, cols] = g
            outs[1][n][:, cols], outs[2][n][:, cols], outs[3][n][:, cols] = _adam_math(
                prm[0][n][:, cols], g, prm[1][n][:, cols], prm[2][n][:, cols])

        ada_rows = ((acc1, 0), (acc1, 1), (acc2, 5), (acc2, 2), (acc2, 1), (acc2, 0))
        for k, (ref, r) in enumerate(ada_rows):
            update('b_ada', dev_sum(ref, r, 1), slice(k * D, (k + 1) * D))
            for dev in range(N_DEV):
                dada_ref[dev:dev + 1, k * D:(k + 1) * D] = ref[8 * dev + r:8 * dev + r + 1, :]
        update('norm1_w', dev_sum(acc1, 2, 1))
        update('norm2_w', dev_sum(acc2, 3, 1))
        update('final_norm_w', dev_sum(acc2, 4, 1))
        update('v_ln_w', dev_sum(dln, 0, 1))
        update('v_ln_b', dev_sum(dln, 1, 1))
        update('lower_bounds', dev_sum(dlb, 0, 2))
        update('gn_w', dev_sum(dgn, 0, 1))
        update('b_s', dev_sum(dbs, 0, NH))
        update('w_s', dev_sum(dws, 0, NH * BLK))
        loss_ref[...] = jnp.sum(dev_sum(acc2, 6, 1), axis=-1, keepdims=True)

    shapes = [SDS(params[n].shape, F32) for n in SMALL_NAMES]
    res = pl.pallas_call(
        body, out_shape=shapes * 4 + [SDS((1, 1), F32), SDS((N_DEV, 6 * D), F32)], name="small_finalize")(
            *gathered, *[d[n] for d in (params, moms, vels) for n in SMALL_NAMES])
    return [dict(zip(SMALL_NAMES, res[k * 10:(k + 1) * 10])) for k in range(4)], res[40], res[41]


def _position():
    x, y, c = lax.axis_index("x"), lax.axis_index("y"), lax.axis_index("c")
    return x, y, c


def _chip_at(x, y, r):
    return (x ^ (r >> 1), y ^ (r & 1))


def _all_gather_rows(blocks, name):
    nb = len(blocks)

    def body(*refs):
        ins, outs = refs[:nb], refs[nb:2 * nb]
        send_sems, recv_sems, local_sems = refs[2 * nb:]
        x, y, c = _position()
        me, sibling = (x, y, c), (x, y, 1 - c)
        chips = [_chip_at(x, y, r) for r in (1, 2, 3)]

        def rows(b, px, py, pc):
            m_per = ins[b].shape[0]
            return outs[b].at[pl.ds((4 * px + 2 * py + pc) * m_per, m_per), :]

        def copy(b, k, blk, to, src=None):
            return pltpu.make_async_remote_copy(
                src_ref=rows(b, *blk) if src is None else src, dst_ref=rows(b, *blk),
                send_sem=send_sems.at[7 * b + k], recv_sem=recv_sems.at[7 * b + k], device_id=to, device_id_type=MESH)

        local, sent = [], []
        for b in range(nb):
            mine = pltpu.make_async_copy(ins[b], rows(b, *me), local_sems.at[b])
            mine.start()
            local.append(mine)
            first = [copy(b, 0, me, sibling, src=ins[b])]
            first += [copy(b, 1 + j, me, (*chip, c), src=ins[b]) for j, chip in enumerate(chips)]
            for cp in first:
                cp.start()
            sent += first
        for b in range(nb):
            for j, chip in enumerate(chips):
                copy(b, 1 + j, (*chip, c), me).wait_recv()
                passed = copy(b, 4 + j, (*chip, c), sibling)
                passed.start()
                sent.append(passed)
        for b in range(nb):
            copy(b, 0, sibling, me).wait_recv()
            for j, chip in enumerate(chips):
                copy(b, 4 + j, (*chip, 1 - c), me).wait_recv()
        for cp in sent:
            cp.wait_send()
        for cp in local:
            cp.wait()

    vmem = pl.BlockSpec(memory_space=pltpu.VMEM)
    return pl.pallas_call(
        body, out_shape=[SDS((N_DEV * b.shape[0], b.shape[1]), b.dtype) for b in blocks],
        in_specs=[vmem] * nb, out_specs=[vmem] * nb,
        scratch_shapes=[pltpu.SemaphoreType.DMA((7 * nb,)), pltpu.SemaphoreType.DMA((7 * nb,)),
                        pltpu.SemaphoreType.DMA((nb,))],
        name=name)(*blocks)


def _place_shard(w_shard, axis, chip_idx, name):
    R, C = w_shard.shape
    rb = _row_block(R)
    nb = R // rb
    full = (R * N_CHIPS, C) if axis == 0 else (R, C * N_CHIPS)
    omap = (lambda i, j: (j[0] * nb + i, 0)) if axis == 0 else (lambda i, j: (i, j[0]))

    def body(j_ref, w_ref, o_ref):
        o_ref[...] = w_ref[...].astype(BF16)

    return pl.pallas_call(
        body,
        grid_spec=pltpu.PrefetchScalarGridSpec(
            num_scalar_prefetch=1, grid=(nb,), in_specs=[pl.BlockSpec((rb, C), lambda i, j: (i, 0))],
            out_specs=pl.BlockSpec((rb, C), omap)),
        out_shape=SDS(full, BF16), compiler_params=_arb(), name=name)(chip_idx, w_shard)


class _WeightGather:
    def __init__(self, refs, axes, send_sems, recv_sems):
        self.refs, self.axes, self.send_sems, self.recv_sems = refs, axes, send_sems, recv_sems
        self.x, self.y, self.c = _position()
        self.j = 2 * self.x + self.y
        self.n = 3 * len(refs)

    def _half(self, w, chip_idx, half):
        ref, axis = self.refs[w], self.axes[w]
        if axis == 0:
            size = ref.shape[0] // N_CHIPS
            return ref.at[pl.ds(chip_idx * size + half * (size // 2), size // 2), :]
        size = ref.shape[1] // N_CHIPS
        rows = ref.shape[0] // 2
        return ref.at[pl.ds(half * rows, rows), pl.ds(chip_idx * size, size)]

    def _ici(self, w, r, chip_idx):
        k = 3 * w + r - 1
        piece = self._half(w, chip_idx, self.c)
        return pltpu.make_async_remote_copy(
            src_ref=piece, dst_ref=piece, send_sem=self.send_sems.at[k], recv_sem=self.recv_sems.at[k],
            device_id=(*_chip_at(self.x, self.y, r), self.c), device_id_type=MESH)

    def _d2d(self, w, r, half):
        k = self.n + 3 * w + r - 1
        piece = self._half(w, self.j ^ r, half)
        return pltpu.make_async_remote_copy(
            src_ref=piece, dst_ref=piece, send_sem=self.send_sems.at[k], recv_sem=self.recv_sems.at[k],
            device_id=(self.x, self.y, 1 - self.c), device_id_type=MESH)

    def _each(self):
        return [(w, r) for w in range(len(self.refs)) for r in (1, 2, 3)]

    def start(self):
        for w, r in self._each():
            self._ici(w, r, self.j).start()

    def forward(self):
        for w, r in self._each():
            self._ici(w, r, self.j ^ r).wait_recv()
            self._d2d(w, r, self.c).start()

    def finish(self):
        for w, r in self._each():
            self._ici(w, r, self.j).wait_send()
            self._d2d(w, r, self.c).wait_send()
            self._d2d(w, r, 1 - self.c).wait_recv()


def _gather_sems(n_weights):
    return [pltpu.SemaphoreType.DMA((6 * n_weights,)), pltpu.SemaphoreType.DMA((6 * n_weights,))]


def _gather_weights(placed, axes, name):
    nw = len(placed)

    def body(*refs):
        outs = refs[nw:2 * nw]
        g = _WeightGather(outs, axes, *refs[2 * nw:])
        g.start()
        g.forward()
        g.finish()

    anyspec = pl.BlockSpec(memory_space=pl.ANY)
    return pl.pallas_call(
        body, out_shape=[SDS(a.shape, a.dtype) for a in placed], in_specs=[anyspec] * nw, out_specs=[anyspec] * nw,
        scratch_shapes=_gather_sems(nw), input_output_aliases={i: i for i in range(nw)},
        name=name)(*placed)


class _ChipExchange:
    def __init__(self, ins, outs, send_sems, recv_sems):
        self.ins, self.outs, self.send_sems, self.recv_sems = ins, outs, send_sems, recv_sems
        self.x, self.y, self.c = _position()
        self.j = 2 * self.x + self.y

    def _copies(self):
        for w in range(len(self.ins)):
            for r in (1, 2, 3):
                k = 3 * w + r - 1
                yield pltpu.make_async_remote_copy(
                    src_ref=self.ins[w].at[self.j ^ r], dst_ref=self.outs[w].at[r - 1],
                    send_sem=self.send_sems.at[k], recv_sem=self.recv_sems.at[k],
                    device_id=(*_chip_at(self.x, self.y, r), self.c), device_id_type=MESH)

    def start(self):
        for cp in self._copies():
            cp.start()

    def finish(self):
        for cp in self._copies():
            cp.wait()


def _exchange_sems(n_weights):
    return [pltpu.SemaphoreType.DMA((3 * n_weights,)), pltpu.SemaphoreType.DMA((3 * n_weights,))]


class _CoreExchange:
    def __init__(self, ins, outs, send_sems, recv_sems):
        self.ins, self.outs, self.send_sems, self.recv_sems = ins, outs, send_sems, recv_sems
        self.x, self.y, self.c = _position()

    def _copies(self):
        for w in range(len(self.ins)):
            yield pltpu.make_async_remote_copy(
                src_ref=self.ins[w].at[:, 1 - self.c], dst_ref=self.outs[w],
                send_sem=self.send_sems.at[w], recv_sem=self.recv_sems.at[w],
                device_id=(self.x, self.y, 1 - self.c), device_id_type=MESH)

    def start(self):
        for cp in self._copies():
            cp.start()

    def finish(self):
        for cp in self._copies():
            cp.wait()


def _core_exchange_shapes(grads):
    return [SDS((g.shape[0], g.shape[2], g.shape[3]), F32) for g in grads]


def _core_exchange_sems(n):
    return [pltpu.SemaphoreType.DMA((n,)), pltpu.SemaphoreType.DMA((n,))]


def _exchange_core_halves(grads, name):
    nw = len(grads)

    def body(*refs):
        ex = _CoreExchange(refs[:nw], refs[nw:2 * nw], *refs[2 * nw:])
        ex.start()
        ex.finish()

    anyspec = pl.BlockSpec(memory_space=pl.ANY)
    return pl.pallas_call(
        body, out_shape=_core_exchange_shapes(grads), in_specs=[anyspec] * nw, out_specs=[anyspec] * nw,
        scratch_shapes=_core_exchange_sems(nw), name=name)(*grads)


def _add_core_halves(g4, recv, c_idx, rb, name):
    ns, _, rh, C = g4.shape

    def body(c_ref, g_ref, r_ref, o_ref):
        o_ref[...] = (g_ref[0] + r_ref[...]).astype(BF16)

    return pl.pallas_call(
        body,
        grid_spec=pltpu.PrefetchScalarGridSpec(
            num_scalar_prefetch=1, grid=(ns, rh // rb),
            in_specs=[pl.BlockSpec((1, 1, rb, C), lambda s, i, cr: (s, cr[0], i, 0)),
                      pl.BlockSpec((1, rb, C), lambda s, i, cr: (s, i, 0))],
            out_specs=pl.BlockSpec((1, rb, C), lambda s, i, cr: (s, i, 0))),
        out_shape=SDS((ns, rh, C), BF16), compiler_params=_arb(2), name=name)(c_idx, g4, recv)


def _slot_shapes(sums):
    return [SDS((3,) + s.shape[1:], s.dtype) for s in sums]


def _add_chips(own, slots, order, rb, name):
    _, rh, C = slots.shape

    def body(o_ref, own_ref, a_ref, b_ref, c_ref, d_ref, out_ref):
        mine = own_ref[0].astype(F32)
        t = [jnp.where(o_ref[i] == 0, mine, r[0].astype(F32)) for i, r in enumerate((a_ref, b_ref, c_ref, d_ref))]
        out_ref[...] = ((t[0] + t[1]) + t[2]) + t[3]

    def spec(i):
        return pl.BlockSpec((1, rb, C), lambda t, o: (jnp.maximum(o[i], 1) - 1, t, 0))

    return pl.pallas_call(
        body,
        grid_spec=pltpu.PrefetchScalarGridSpec(
            num_scalar_prefetch=1, grid=(rh // rb,),
            in_specs=[pl.BlockSpec((1, rb, C), lambda t, o: (o[4], t, 0)), spec(0), spec(1), spec(2), spec(3)],
            out_specs=pl.BlockSpec((rb, C), lambda t, o: (t, 0))),
        out_shape=SDS((rh, C), F32), compiler_params=_arb(), name=name)(order, own, slots, slots, slots, slots)


def _share_halves(halves):
    nw = len(halves)

    def body(*refs):
        ins, outs = refs[:nw], refs[nw:2 * nw]
        send_sems, recv_sems = refs[2 * nw:]
        x, y, c = _position()
        started = []
        for w in range(nw):
            cp = pltpu.make_async_remote_copy(
                src_ref=ins[w], dst_ref=outs[w], send_sem=send_sems.at[w], recv_sem=recv_sems.at[w],
                device_id=(x, y, 1 - c), device_id_type=MESH)
            cp.start()
            started.append(cp)
        for cp in started:
            cp.wait()

    anyspec = pl.BlockSpec(memory_space=pl.ANY)
    return pl.pallas_call(
        body, out_shape=[SDS(h.shape, F32) for h in halves], in_specs=[anyspec] * nw, out_specs=[anyspec] * nw,
        scratch_shapes=[pltpu.SemaphoreType.DMA((nw,)), pltpu.SemaphoreType.DMA((nw,))],
        name="share_halves")(*halves)


def _small_2d(b_ada, norm1_w, norm2_w, final_norm_w, v_ln_w, v_ln_b, lower_bounds, gn_w, b_s, w_s):
    return dict(zip(SMALL_NAMES, (b_ada, norm1_w, norm2_w, final_norm_w.reshape(1, D), v_ln_w, v_ln_b, lower_bounds, gn_w,
                                  b_s.reshape(NH, BLK), w_s.reshape(NH * BLK, BLK))))


def _small_original_shapes(d):
    out = dict(d)
    out['final_norm_w'] = d['final_norm_w'].reshape(D)
    out['b_s'] = d['b_s'].reshape(1, NH, BLK)
    out['w_s'] = d['w_s'].reshape(1, NH, BLK, BLK)
    return out


def _row_block(r):
    for cand in (256, 176, 128, 64, 32, 16, 8):
        if r % cand == 0:
            return cand
    return r


def kernel(x, c, w_ada, b_ada, norm1_w, w_in, w_s, b_s, v_ln_w, v_ln_b, lower_bounds, gn_w, w_out, norm2_w, w_ffn_in, w_ffn_out, final_norm_w, loss_target, m_w_ada, m_b_ada, m_norm1_w, m_w_in, m_w_s, m_b_s, m_v_ln_w, m_v_ln_b, m_lower_bounds, m_gn_w, m_w_out, m_norm2_w, m_w_ffn_in, m_w_ffn_out, m_final_norm_w, v_w_ada, v_b_ada, v_norm1_w, v_w_in, v_w_s, v_b_s, v_v_ln_w, v_v_ln_b, v_lower_bounds, v_gn_w, v_w_out, v_norm2_w, v_w_ffn_in, v_w_ffn_out, v_final_norm_w):
    T = x.shape[1]
    tm, tp = min(TOKEN_TILE, T), min(PROJ_TILE, T)
    px, py, pc = _position()
    chip = 2 * px + py
    me = 4 * px + 2 * py + pc
    x2d = x.reshape(T, D)
    tgt = loss_target.reshape(T, D)

    (c_all,) = _all_gather_rows([jnp.broadcast_to(c, (8, D))], "gather_c")
    cact, ada_part = _ada_forward(c_all.reshape(N_DEV, 8, D)[:, 0, :], w_ada[0])
    n_ada = ada_part.shape[1]
    (ada_all,) = _all_gather_rows([ada_part], "gather_ada")
    ada_all = ada_all.reshape(N_CHIPS, 2, N_DEV, n_ada)[:, 0]
    ada = lax.dynamic_index_in_dim(ada_all, me, axis=1, keepdims=False).reshape(1, 6 * D) + b_ada

    chip_idx = jnp.reshape(chip, (1,)).astype(jnp.int32)
    c_idx = jnp.reshape(pc, (1,)).astype(jnp.int32)
    (w_in_b,) = _gather_weights([_place_shard(w_in[0], 1, chip_idx, "place_in")], [1], "gather_w_in")
    placed = [_place_shard(w_out[0], 0, chip_idx, "place_out"), _place_shard(w_ffn_in[0], 1, chip_idx, "place_ffn_in"),
              _place_shard(w_ffn_out[0], 0, chip_idx, "place_ffn_out")]

    rr = lax.broadcasted_iota(jnp.int32, (BLK, BLK), 0) // CH
    cc = lax.broadcasted_iota(jnp.int32, (BLK, BLK), 1) // CH
    ws_b = jnp.where((rr >= cc)[None], w_s[0], 0.0).astype(BF16)
    bst = b_s[0].T
    lnw, lnb = v_ln_w, v_ln_b
    nw1, nw2, fw = norm1_w, norm2_w, final_norm_w.reshape(1, D)

    h1, proj = _proj_in(x2d, nw1, ada, ada, w_in_b, tp)
    ycat = _gmlp_fwd(proj, ws_b, bst, lnw, lnb)
    tables = _hgrn_tables()
    (ycat, o_pre, a_all, st_all), (w_out_b, w_fi_b, w_fo_b) = _hgrn_fwd(
        proj, lower_bounds, gn_w, ycat, tables, placed, [0, 1, 0])

    dycat, dx1, h2, act, dff, dgu, dmix, acc2 = _token_local(
        x2d, ycat, tgt, ada, nw2, ada, ada, ada, fw, w_out_b, w_fi_b, w_fo_b, tm)

    tt = min(WGRAD_TOKENS, T)
    order = jnp.concatenate([chip ^ jnp.arange(N_CHIPS, dtype=jnp.int32), chip_idx]).astype(jnp.int32)

    def by_core_half(g):
        return g.reshape(N_CHIPS, 2, g.shape[1] // 2, g.shape[2])

    def core_sums(g4, recv, names):
        return [_add_core_halves(a, b, c_idx, _row_block(a.shape[2]), "add_core_" + n) for a, b, n in zip(g4, recv, names)]

    def chip_sums(sums, slots, names):
        return [_add_chips(o, s, order, _row_block(s.shape[1]), "add_chips_" + n) for o, s, n in zip(sums, slots, names)]

    g_out = _wgrad(ycat, dmix, D, D, tt, "wgrad_out").reshape(N_CHIPS, D // N_CHIPS, D)
    g_fi = _wgrad(h2, dgu, D, FFB, tt, "wgrad_ffn_in")
    g_fo = _wgrad(act, dff, FFB, D, tt, "wgrad_ffn_out").reshape(N_CHIPS, DFF // N_CHIPS, D)
    late_names = ["out", "ffn_in", "ffn_out"]
    late_g4 = [by_core_half(g) for g in (g_out, g_fi, g_fo)]

    (dproj, dws, dbs, dln), late_recv = _gmlp_bwd(proj, dycat, ws_b, bst, lnw, lnb, late_g4)
    late_sums = core_sums(late_g4, late_recv, late_names)
    (dproj, dlb, dgn), late_slots = _hgrn_bwd(
        proj, o_pre, a_all, st_all, dycat, lower_bounds, gn_w, dproj, tables, late_sums)

    g_in = _wgrad(h1, dproj, D, D, tt, "wgrad_in")
    g_in = jnp.concatenate([g_in[2], g_in[0], g_in[1]], axis=1).reshape(D, N_CHIPS, DIN // N_CHIPS).transpose(1, 0, 2)
    in_g4 = [by_core_half(g_in)]
    in_sums = core_sums(in_g4, _exchange_core_halves(in_g4, "exchange_core_halves_in"), ["in"])
    (grad_x, acc1), in_slots = _proj_in_bwd(dproj, x2d, dx1, nw1, ada, w_in_b, tp, in_sums)
    names = ["in"] + late_names
    halves = chip_sums(in_sums, in_slots, ["in"]) + chip_sums(late_sums, late_slots, late_names)
    sibling_halves = _share_halves(halves)

    big_w = [(w_in, m_w_in, v_w_in), (w_out, m_w_out, v_w_out), (w_ffn_in, m_w_ffn_in, v_w_ffn_in),
             (w_ffn_out, m_w_ffn_out, v_w_ffn_out)]
    big_out = []
    for mine, sib, (w, m, v), n in zip(halves, sibling_halves, big_w, names):
        res = _adamw_halves(w[0], mine, sib, m[0], v[0], c_idx, _row_block(mine.shape[0]), "adamw_" + n)
        big_out.append([r[None] for r in res])

    gathered = _all_gather_rows([acc1, acc2, dln, dlb, dgn, dbs, dws], "gather_small")
    small, loss, dada_all = _small_finalize(
        gathered,
        _small_2d(b_ada, norm1_w, norm2_w, final_norm_w, v_ln_w, v_ln_b, lower_bounds, gn_w, b_s, w_s),
        _small_2d(m_b_ada, m_norm1_w, m_norm2_w, m_final_norm_w, m_v_ln_w, m_v_ln_b, m_lower_bounds, m_gn_w, m_b_s, m_w_s),
        _small_2d(v_b_ada, v_norm1_w, v_norm2_w, v_final_norm_w, v_v_ln_w, v_v_ln_b, v_lower_bounds, v_gn_w, v_b_s, v_w_s))
    small = [_small_original_shapes(d) for d in small]
    loss = loss.reshape(())

    ada_out = [o[None] for o in _ada_wgrad_adam(cact.T, dada_all, w_ada[0], m_w_ada[0], v_w_ada[0], chip_idx)]

    order_names = ['w_ada', 'b_ada', 'norm1_w', 'w_in', 'w_s', 'b_s', 'v_ln_w', 'v_ln_b', 'lower_bounds', 'gn_w',
                   'w_out', 'norm2_w', 'w_ffn_in', 'w_ffn_out', 'final_norm_w']
    big_idx = {'w_in': 0, 'w_out': 1, 'w_ffn_in': 2, 'w_ffn_out': 3}
    outs = [loss, grad_x.reshape(1, T, D)]
    for kind in range(4):
        for n in order_names:
            if n == 'w_ada':
                outs.append(ada_out[kind])
            elif n in big_idx:
                outs.append(big_out[big_idx[n]][kind])
            else:
                outs.append(small[kind][n])
    return tuple(outs)
```

```python
import functools

import jax
import jax.numpy as jnp
import numpy as np
from jax import lax
from jax.experimental import pallas as pl
from jax.experimental.pallas import tpu as pltpu

F32 = jnp.float32
BF16 = jnp.bfloat16
SDS = jax.ShapeDtypeStruct
MESH = pl.DeviceIdType.MESH
HIGHEST = lax.Precision.HIGHEST

D = 1024
DG = 512
DH = 512
NH = 4
HD = 128
BLK = 128
CH = 64
DFF = 2816
DIN = 3072
FFB = 1408
LEVELS = (64, 32, 16, 8, 4, 2)
HGRN_CHUNKS_PER_STEP = 4
GMLP_ROWS_PER_STEP = 512
TOKEN_TILE = 256
PROJ_TILE = 512
WGRAD_TOKENS = 2048
N_CHIPS = 4
N_DEV = 8
EPS = 1e-6
LR, B1, B2, AEPS, WD, STEP = 0.001, 0.9, 0.999, 1e-08, 0.01, 10

NT = (((1,), (1,)), ((), ()))
TN = (((0,), (0,)), ((), ()))


def _full(shape):
    nd = len(shape)
    return pl.BlockSpec(shape, lambda *_: (0,) * nd)


ADA_SH1, ADA_SC1, ADA_G1, ADA_SH2, ADA_SC2, ADA_G2 = range(6)


def _ada_part(k):
    return pl.BlockSpec((1, D), lambda *_: (0, k))


def _resident(shape):
    nd = len(shape)
    return pl.BlockSpec(shape, lambda *_: (0,) * nd, pipeline_mode=pl.Buffered(1))


def _arb(n=1):
    return pltpu.CompilerParams(dimension_semantics=("arbitrary",) * n)


def _dot(a, b, dims=None, precision=None):
    if dims is None:
        return jnp.dot(a, b, preferred_element_type=F32, precision=precision)
    return lax.dot_general(a, b, dims, preferred_element_type=F32, precision=precision)


def _sigmoid(x):
    return jax.nn.sigmoid(x)


def _gelu_parts(x):
    cdf = 0.5 * (1.0 + lax.erf(x * 0.7071067811865476))
    pdf = jnp.exp(-0.5 * x * x) * 0.3989422804014327
    return x * cdf, cdf + x * pdf


def _rms(x):
    return lax.rsqrt(jnp.mean(x * x, axis=-1, keepdims=True) + EPS)


def _rms_bwd(xhat, r, gw):
    return r * (gw - xhat * jnp.mean(xhat * gw, axis=-1, keepdims=True))


def _lower_bound(lbp_ref):
    l0, l1 = lbp_ref[0:1, :], lbp_ref[1:2, :]
    m = jnp.maximum(l0, l1)
    e0, e1 = jnp.exp(l0 - m), jnp.exp(l1 - m)
    return e0 / (e0 + e1), e1 / (e0 + e1)


def _proj_in(x, nw, sc, sh, w_in_b, tm):
    T = x.shape[0]

    def body(x_ref, nw_ref, sc_ref, sh_ref, w_ref, h_ref, p_ref):
        xv = x_ref[...]
        h = ((xv * _rms(xv)) * nw_ref[...]) * (1.0 + sc_ref[...]) + sh_ref[...]
        hb = h.astype(BF16)
        h_ref[...] = hb
        p_ref[...] = _dot(hb, w_ref[...])

    row = lambda i: (i, 0)
    return pl.pallas_call(
        body, grid=(T // tm,),
        in_specs=[pl.BlockSpec((tm, D), row), _full((1, D)), _ada_part(ADA_SC1), _ada_part(ADA_SH1), _resident((D, DIN))],
        out_specs=[pl.BlockSpec((tm, D), row), pl.BlockSpec((tm, DIN), row)],
        out_shape=[SDS((T, D), BF16), SDS((T, DIN), F32)],
        compiler_params=_arb(), name="proj_in")(x, nw, sc, sh, w_in_b)


def _gmlp_common(u, v, lnw, lnb, ws_ref, bst_ref):
    ug, dug = _gelu_parts(u)
    vg, dvg = _gelu_parts(v)
    mu = jnp.mean(vg, axis=-1, keepdims=True)
    vc = vg - mu
    rstd = lax.rsqrt(jnp.mean(vc * vc, axis=-1, keepdims=True) + EPS)
    vhat = vc * rstd
    vn = vhat * lnw + lnb
    vnb = vn.astype(BF16)
    mixed = []
    for h in range(NH):
        sl = slice(h * HD, (h + 1) * HD)
        mixed.append(_dot(ws_ref[h], vnb[:, sl]) + bst_ref[:, h:h + 1])
    return ug, dug, dvg, rstd, vhat, vnb, jnp.concatenate(mixed, axis=1)


def _gmlp_fwd(proj, ws_b, bst, lnw, lnb):
    T = proj.shape[0]
    rows = min(GMLP_ROWS_PER_STEP, T)

    def body(u_ref, v_ref, ws_ref, bst_ref, lnw_ref, lnb_ref, y_ref):
        for bi in range(rows // BLK):
            rs = slice(bi * BLK, (bi + 1) * BLK)
            ug, _, _, _, _, _, mixed = _gmlp_common(u_ref[rs, :], v_ref[rs, :], lnw_ref[...], lnb_ref[...], ws_ref, bst_ref)
            y_ref[rs, :] = (ug * mixed).astype(BF16)

    return pl.pallas_call(
        body, grid=(T // rows,),
        in_specs=[pl.BlockSpec((rows, DG), lambda i: (i, 0)), pl.BlockSpec((rows, DG), lambda i: (i, 1)),
                  _full((NH, BLK, BLK)), _full((BLK, NH)), _full((1, DG)), _full((1, DG))],
        out_specs=pl.BlockSpec((rows, DG), lambda i: (i, 0)),
        out_shape=SDS((T, D), BF16),
        compiler_params=_arb(), name="gmlp_fwd")(proj, proj, ws_b, bst, lnw, lnb)


def _hgrn_tables():
    t = np.arange(CH)[:, None]
    j = np.arange(CH)[None, :]
    blocks = [j <= t, j > t]
    masks = []
    for n in LEVELS:
        mid = t - t % n + n // 2
        blocks.append(np.where(t >= mid, (j >= mid) & (j <= t), (j > t) & (j < mid)))
        masks.append((t // n == j // n) & (t % n >= n // 2) & (j % n < n // 2))
    w = np.concatenate(blocks, axis=0).astype(np.float32)
    m = np.stack(masks).astype(np.float32)
    return (jnp.asarray(w, BF16), jnp.asarray(w.T, BF16), jnp.asarray(m), jnp.asarray(m + m.transpose(0, 2, 1)))


def _split_dot(w, x, parts):
    acc = None
    for _ in range(parts):
        piece = x.astype(BF16)
        term = _dot(w, piece)
        acc = term if acc is None else acc + term
        x = x - piece.astype(F32)
    return acc


def _hgrn_exponents(lf, w_ref):
    b = _split_dot(w_ref[0:CH, :], lf, 3)
    row = lax.broadcasted_iota(jnp.int32, (CH, 1), 0)
    blocks = [b, b[CH - 1:CH, :] - b]
    for n in LEVELS:
        up = (row & (n // 2)) != 0
        if n >= 8:
            ref = b.reshape(CH // n, n, DH)[:, n // 2 - 1:n // 2, :]
            ref = jnp.broadcast_to(ref, (CH // n, n, DH)).reshape(CH, DH)
            blocks.append(jnp.where(up, b - ref, ref - b))
        elif n == 4:
            r4 = row & 3
            two = jnp.where(r4 == 3, pltpu.roll(lf, 1, 0) + lf, 0.0)
            blocks.append(jnp.where(r4 == 0, pltpu.roll(lf, CH - 1, 0), jnp.where(r4 == 2, lf, two)))
        else:
            blocks.append(jnp.where(up, lf, 0.0))
    return blocks


def _hgrn_gates(q, fl, lb, omlb, w_ref):
    sq = _sigmoid(q)
    qf = q * sq
    sig = _sigmoid(fl)
    f = lb + omlb * sig
    k = 1.0 - f
    e = [jnp.exp(x) for x in _hgrn_exponents(jnp.log(f), w_ref)]
    return sq, qf, sig, f, k, e


def _level_factor(e, li, sl, row, qh, kh):
    el = e[2 + li][:, sl]
    up = (row & (LEVELS[li] // 2)) != 0
    return el, up, el * jnp.where(up, qh, kh)


def _hgrn_fwd(proj, lower_bounds, gn_w, ycat, tables, placed, axes):
    T = proj.shape[0]
    nc = T // CH
    nch = min(HGRN_CHUNKS_PER_STEP, nc)
    steps = nc // nch
    w_st, _, masks, _ = tables
    nw = len(placed)
    pass_step = (5 * steps) // 8

    def body(*refs):
        q_ref, f_ref, i_ref, g_ref, lbp_ref, gn_ref, w_ref, m_ref = refs[:8]
        y_ref, o_ref, a_ref, st_ref = refs[9 + nw:13 + nw]
        s_scr, send_sems, recv_sems = refs[13 + 2 * nw:]
        gather = _WeightGather(refs[13 + nw:13 + 2 * nw], axes, send_sems, recv_sems)
        step = pl.program_id(0)

        @pl.when(step == 0)
        def _():
            gather.start()
            s_scr[...] = jnp.zeros_like(s_scr)

        @pl.when(step == pass_step)
        def _():
            gather.forward()

        lb, omlb = _lower_bound(lbp_ref)
        row = lax.broadcasted_iota(jnp.int32, (CH, 1), 0)
        eye = lax.broadcasted_iota(jnp.int32, (CH, CH), 0) == lax.broadcasted_iota(jnp.int32, (CH, CH), 1)
        pre = []
        for ci in range(nch):
            rs = slice(ci * CH, (ci + 1) * CH)
            _, qf, _, _, k, e = _hgrn_gates(q_ref[rs, :], f_ref[rs, :], lb, omlb, w_ref)
            mats = []
            for h in range(NH):
                sl = slice(h * HD, (h + 1) * HD)
                qh, kh = qf[:, sl], k[:, sl]
                a = jnp.where(eye, jnp.sum(qh * kh, axis=-1, keepdims=True), 0.0)
                for li in range(len(LEVELS)):
                    _, _, y = _level_factor(e, li, sl, row, qh, kh)
                    yb = y.astype(BF16)
                    a = a + m_ref[li] * _dot(yb, yb, NT)
                a_ref[ci, h] = a
                mats.append(a.astype(BF16))
            eb = e[0]
            pre.append(((qf * eb).astype(BF16), eb[CH - 1:CH, :], (k * e[1]).astype(BF16), mats))
        for ci in range(nch):
            rs = slice(ci * CH, (ci + 1) * CH)
            qe, ebl, kd, mats = pre[ci]
            v = i_ref[rs, :]
            g = g_ref[rs, :]
            for h in range(NH):
                sl = slice(h * HD, (h + 1) * HD)
                st0 = s_scr[h]
                st_ref[ci, h] = st0
                vb = v[:, sl].astype(BF16)
                o = _dot(qe[:, sl], st0.astype(BF16), NT) + _dot(mats[h], vb)
                s_scr[h] = st0 * ebl[:, sl] + _dot(vb, kd[:, sl], TN)
                o_ref[rs, sl] = o
                gh = g[:, sl]
                y_ref[rs, sl] = (((o * _rms(o)) * gn_ref[...]) * (gh * _sigmoid(gh))).astype(BF16)

        @pl.when(step == steps - 1)
        def _():
            gather.finish()

    blk = lambda j: pl.BlockSpec((nch * CH, DH), lambda c: (c, j))
    anyspec = pl.BlockSpec(memory_space=pl.ANY)
    res = pl.pallas_call(
        body, grid=(steps,),
        in_specs=[blk(2), blk(3), blk(4), blk(5), _full((2, DH)), _full((1, HD)),
                  _full(w_st.shape), _full(masks.shape), anyspec] + [anyspec] * nw,
        out_specs=[pl.BlockSpec((nch * CH, DH), lambda c: (c, 1)),
                   pl.BlockSpec((nch * CH, DH), lambda c: (c, 0)),
                   pl.BlockSpec((nch, NH, CH, CH), lambda c: (c, 0, 0, 0)),
                   pl.BlockSpec((nch, NH, HD, HD), lambda c: (c, 0, 0, 0))] + [anyspec] * nw,
        out_shape=[SDS((T, D), BF16), SDS((T, DH), F32), SDS((nc, NH, CH, CH), F32), SDS((nc, NH, HD, HD), F32)]
        + [SDS(a.shape, a.dtype) for a in placed],
        scratch_shapes=[pltpu.VMEM((NH, HD, HD), F32)] + _gather_sems(nw),
        input_output_aliases={8: 0, **{9 + i: 4 + i for i in range(nw)}},
        compiler_params=_arb(), name="hgrn_fwd")(proj, proj, proj, proj, lower_bounds, gn_w, w_st, masks, ycat, *placed)
    return res[:4], res[4:]


def _token_local(x, ycat, tgt, g1, nw2, sc2, sh2, g2, fw, w_out_b, w_fi_b, w_fo_b, tm):
    T = x.shape[0]
    inv_d = 1.0 / D

    def body(x_ref, y_ref, t_ref, g1_ref, nw2_ref, sc2_ref, sh2_ref, g2_ref, fw_ref, wo_ref, wfi_ref, wfo_ref,
             dy_ref, dx1_ref, h2_ref, act_ref, dff_ref, dgu_ref, dmix_ref, acc_ref):
        @pl.when(pl.program_id(0) == 0)
        def _():
            acc_ref[...] = jnp.zeros_like(acc_ref)

        def acc(row, val):
            acc_ref[row:row + 1, :] += jnp.sum(val, axis=0, keepdims=True)

        g1v, g2v = g1_ref[...], g2_ref[...]
        mix = _dot(y_ref[...], wo_ref[...])
        x1 = x_ref[...] + g1v * mix
        r2 = _rms(x1)
        xh2 = x1 * r2
        n2 = xh2 * nw2_ref[...]
        osc2 = 1.0 + sc2_ref[...]
        h2b = (n2 * osc2 + sh2_ref[...]).astype(BF16)
        h2_ref[...] = h2b
        ff = jnp.zeros((tm, D), F32)
        saved = []
        for kb in range(DFF // FFB):
            gate = _dot(h2b, wfi_ref[:, kb * FFB:(kb + 1) * FFB])
            up = _dot(h2b, wfi_ref[:, DFF + kb * FFB:DFF + (kb + 1) * FFB])
            sg = _sigmoid(gate)
            actb = (gate * sg * up).astype(BF16)
            act_ref[:, kb * FFB:(kb + 1) * FFB] = actb
            ff = ff + _dot(actb, wfo_ref[kb * FFB:(kb + 1) * FFB, :])
            saved.append((gate, up, sg))
        x2 = x1 + g2v * ff
        r3 = _rms(x2)
        xh3 = x2 * r3
        err = xh3 * fw_ref[...] - t_ref[...]
        acc(6, (0.5 * inv_d) * err * err)
        dy = err * inv_d
        acc(4, dy * xh3)
        dx2 = _rms_bwd(xh3, r3, dy * fw_ref[...])
        acc(0, dx2 * ff)
        dffb = (dx2 * g2v).astype(BF16)
        dff_ref[...] = dffb
        dh2 = jnp.zeros((tm, D), F32)
        for kb in range(DFF // FFB):
            gate, up, sg = saved[kb]
            da = _dot(dffb, wfo_ref[kb * FFB:(kb + 1) * FFB, :], NT)
            dgate = (da * up * (sg * (1.0 + gate * (1.0 - sg)))).astype(BF16)
            dup = (da * gate * sg).astype(BF16)
            dgu_ref[:, kb * FFB:(kb + 1) * FFB] = dgate
            dgu_ref[:, DFF + kb * FFB:DFF + (kb + 1) * FFB] = dup
            dh2 = dh2 + _dot(dgate, wfi_ref[:, kb * FFB:(kb + 1) * FFB], NT)
            dh2 = dh2 + _dot(dup, wfi_ref[:, DFF + kb * FFB:DFF + (kb + 1) * FFB], NT)
        acc(2, dh2)
        acc(1, dh2 * n2)
        dn2 = dh2 * osc2
        acc(3, dn2 * xh2)
        dx1 = dx2 + _rms_bwd(xh2, r2, dn2 * nw2_ref[...])
        acc(5, dx1 * mix)
        dmixb = (dx1 * g1v).astype(BF16)
        dmix_ref[...] = dmixb
        dy_ref[...] = _dot(dmixb, wo_ref[...], NT)
        dx1_ref[...] = dx1

    row = lambda i: (i, 0)
    vec = _full((1, D))
    return pl.pallas_call(
        body, grid=(T // tm,),
        in_specs=[pl.BlockSpec((tm, D), row), pl.BlockSpec((tm, D), row), pl.BlockSpec((tm, D), row),
                  _ada_part(ADA_G1), vec, _ada_part(ADA_SC2), _ada_part(ADA_SH2), _ada_part(ADA_G2), vec,
                  _resident((D, D)), _resident((D, 2 * DFF)), _resident((DFF, D))],
        out_specs=[pl.BlockSpec((tm, D), row), pl.BlockSpec((tm, D), row), pl.BlockSpec((tm, D), row),
                   pl.BlockSpec((tm, DFF), row), pl.BlockSpec((tm, D), row), pl.BlockSpec((tm, 2 * DFF), row),
                   pl.BlockSpec((tm, D), row), _full((8, D))],
        out_shape=[SDS((T, D), F32), SDS((T, D), F32), SDS((T, D), BF16), SDS((T, DFF), BF16), SDS((T, D), BF16),
                   SDS((T, 2 * DFF), BF16), SDS((T, D), BF16), SDS((8, D), F32)],
        compiler_params=_arb(), name="token_local")(x, ycat, tgt, g1, nw2, sc2, sh2, g2, fw, w_out_b, w_fi_b, w_fo_b)


def _gmlp_bwd(proj, dycat, ws_b, bst, lnw, lnb, grads):
    T = proj.shape[0]
    rows = min(GMLP_ROWS_PER_STEP, T)
    nb = T // rows
    nw = len(grads)

    def body(*refs):
        u_ref, v_ref, dy_ref, ws_ref, bst_ref, lnw_ref, lnb_ref = refs[:7]
        dp_ref, dws_ref, dbs_ref, dln_ref = refs[7 + nw:11 + nw]
        dbs_acc, send_sems, recv_sems = refs[11 + 2 * nw:]
        exchange = _CoreExchange(refs[7:7 + nw], refs[11 + nw:11 + 2 * nw], send_sems, recv_sems)
        i = pl.program_id(0)

        @pl.when(i == 0)
        def _():
            exchange.start()
            dws_ref[...] = jnp.zeros_like(dws_ref)
            dln_ref[...] = jnp.zeros_like(dln_ref)
            dbs_acc[...] = jnp.zeros_like(dbs_acc)

        r = lax.broadcasted_iota(jnp.int32, (BLK, BLK), 0) // CH
        c = lax.broadcasted_iota(jnp.int32, (BLK, BLK), 1) // CH
        for bi in range(rows // BLK):
            rs = slice(bi * BLK, (bi + 1) * BLK)
            ug, dug, dvg, rstd, vhat, vnb, mixed = _gmlp_common(
                u_ref[rs, :], v_ref[rs, :], lnw_ref[...], lnb_ref[...], ws_ref, bst_ref)
            dya = dy_ref[rs, :]
            dp_ref[rs, 0:DG] = (dya * mixed * dug).astype(BF16)
            dmixed = dya * ug
            dbs_acc[...] += dmixed
            dmb = dmixed.astype(BF16)
            dvn = []
            for h in range(NH):
                sl = slice(h * HD, (h + 1) * HD)
                dws_ref[h * BLK:(h + 1) * BLK, :] += jnp.where(r >= c, _dot(dmb[:, sl], vnb[:, sl], NT), 0.0)
                dvn.append(_dot(ws_ref[h], dmb[:, sl], TN))
            dvn = jnp.concatenate(dvn, axis=1)
            dln_ref[0:1, :] += jnp.sum(dvn * vhat, axis=0, keepdims=True)
            dln_ref[1:2, :] += jnp.sum(dvn, axis=0, keepdims=True)
            dvh = dvn * lnw_ref[...]
            dvgel = rstd * (dvh - jnp.mean(dvh, axis=-1, keepdims=True) - vhat * jnp.mean(dvh * vhat, axis=-1, keepdims=True))
            dp_ref[rs, DG:2 * DG] = (dvgel * dvg).astype(BF16)

        @pl.when(i == nb - 1)
        def _():
            head = lax.broadcasted_iota(jnp.int32, (8, BLK), 0)
            ones = jnp.ones((8, HD), F32)
            out = jnp.zeros((8, BLK), F32)
            for h in range(NH):
                sums = _dot(ones, dbs_acc[:, h * HD:(h + 1) * HD], NT, precision=HIGHEST)
                out = out + jnp.where(head == h, sums, 0.0)
            dbs_ref[...] = out
            exchange.finish()

    anyspec = pl.BlockSpec(memory_space=pl.ANY)
    res = pl.pallas_call(
        body, grid=(nb,),
        in_specs=[pl.BlockSpec((rows, DG), lambda i: (i, 0)), pl.BlockSpec((rows, DG), lambda i: (i, 1)),
                  pl.BlockSpec((rows, DG), lambda i: (i, 0)),
                  _full((NH, BLK, BLK)), _full((BLK, NH)), _full((1, DG)), _full((1, DG))] + [anyspec] * nw,
        out_specs=[pl.BlockSpec((rows, 2 * DG), lambda i: (i, 2)), _full((NH * BLK, BLK)), _full((8, BLK)), _full((8, DG))]
        + [anyspec] * nw,
        out_shape=[SDS((T, DIN), BF16), SDS((NH * BLK, BLK), F32), SDS((8, BLK), F32), SDS((8, DG), F32)]
        + _core_exchange_shapes(grads),
        scratch_shapes=[pltpu.VMEM((BLK, DG), F32)] + _core_exchange_sems(nw),
        compiler_params=_arb(), name="gmlp_bwd")(proj, proj, dycat, ws_b, bst, lnw, lnb, *grads)
    return res[:4], res[4:]


def _hgrn_bwd(proj, o_pre, a_all, st_all, dycat, lower_bounds, gn_w, dproj, tables, sums):
    T = proj.shape[0]
    nc = T // CH
    nch = min(HGRN_CHUNKS_PER_STEP, nc)
    steps = nc // nch
    w_st, w_st_t, _, masks_sym = tables
    n_lev = len(LEVELS)
    nw = len(sums)

    def body(*refs):
        q_ref, f_ref, i_ref, g_ref, o_ref, a_ref, st_ref, dy_ref, lbp_ref, gn_ref, w_ref, wt_ref, ms_ref = refs[:13]
        dp_ref, dlb_ref, dgn_ref = refs[14 + nw:17 + nw]
        ds_scr, dx_scr, send_sems, recv_sems = refs[17 + 2 * nw:]
        exchange = _ChipExchange(refs[14:14 + nw], refs[17 + nw:17 + 2 * nw], send_sems, recv_sems)
        i = pl.program_id(0)

        @pl.when(i == 0)
        def _():
            exchange.start()
            ds_scr[...] = jnp.zeros_like(ds_scr)
            dlb_ref[...] = jnp.zeros_like(dlb_ref)
            dgn_ref[...] = jnp.zeros_like(dgn_ref)

        lb, omlb = _lower_bound(lbp_ref)
        row = lax.broadcasted_iota(jnp.int32, (CH, 1), 0)
        eye = lax.broadcasted_iota(jnp.int32, (CH, CH), 0) == lax.broadcasted_iota(jnp.int32, (CH, CH), 1)
        lower = lax.broadcasted_iota(jnp.int32, (CH, CH), 0) > lax.broadcasted_iota(jnp.int32, (CH, CH), 1)
        dgn = jnp.zeros((1, HD), F32)
        pre = []
        for ci in range(nch):
            rs = slice(ci * CH, (ci + 1) * CH)
            q = q_ref[rs, :]
            v = i_ref[rs, :]
            g = g_ref[rs, :]
            sq, qf, sig, f, k, e = _hgrn_gates(q, f_ref[rs, :], lb, omlb, w_ref)
            eb = e[0]
            ekd = e[1]
            kd = k * ekd
            qe = qf * eb
            dob_h, dqe_h, dqf_h, dki_h, dv_h, dg_h = [], [], [], [], [], []
            for h in range(NH):
                sl = slice(h * HD, (h + 1) * HD)
                o = o_ref[rs, sl]
                ro = _rms(o)
                oh = o * ro
                gh = g[:, sl]
                sg = _sigmoid(gh)
                dyb = dy_ref[rs, sl]
                dg_h.append(dyb * (oh * gn_ref[...]) * (sg * (1.0 + gh * (1.0 - sg))))
                don = dyb * (gh * sg)
                dgn = dgn + jnp.sum(don * oh, axis=0, keepdims=True)
                dob = _rms_bwd(oh, ro, don * gn_ref[...]).astype(BF16)
                vb = v[:, sl].astype(BF16)
                qh, kh = qf[:, sl], k[:, sl]
                dqe = _dot(dob, st_ref[ci, h].astype(BF16))
                da = _dot(dob, vb, NT)
                ddiag = jnp.sum(jnp.where(eye, da, 0.0), axis=-1, keepdims=True)
                dsym = jnp.where(lower, da, _dot(vb, dob, NT))
                upper_part = jnp.zeros((CH, HD), F32)
                both = jnp.zeros((CH, HD), F32)
                for li in range(n_lev):
                    el, up, y = _level_factor(e, li, sl, row, qh, kh)
                    dyv = _dot((ms_ref[li] * dsym).astype(BF16), y.astype(BF16))
                    dx_scr[ci, (2 + li) * CH:(3 + li) * CH, sl] = dyv * y
                    dye = dyv * el
                    upper_part = upper_part + jnp.where(up, dye, 0.0)
                    both = both + dye
                dob_h.append(dob)
                dqe_h.append(dqe)
                dqf_h.append(dqe * eb[:, sl] + ddiag * kh + upper_part)
                dki_h.append(ddiag * qh + (both - upper_part))
                dv_h.append(_dot(a_ref[ci, h].astype(BF16), dob, TN))
            dp_ref[rs, 0:DH] = (jnp.concatenate(dqf_h, axis=1) * (sq * (1.0 + q * (1.0 - sq)))).astype(BF16)
            dp_ref[rs, 3 * DH:4 * DH] = jnp.concatenate(dg_h, axis=1).astype(BF16)
            pre.append((v, sig, f, eb, ekd, kd, qe, dob_h, jnp.concatenate(dqe_h, axis=1), dki_h, dv_h))
        dgn_ref[0:1, :] += dgn
        for ci in reversed(range(nch)):
            rs = slice(ci * CH, (ci + 1) * CH)
            v, sig, f, eb, ekd, kd, qe, dob_h, dqe, dki_h, dv_h = pre[ci]
            ebl = eb[CH - 1:CH, :]
            dbl_h, dkd_h, dv2_h = [], [], []
            for h in range(NH):
                sl = slice(h * HD, (h + 1) * HD)
                dst1 = ds_scr[h]
                dst1b = dst1.astype(BF16)
                ds_scr[h] = dst1 * ebl[:, sl] + _dot(dob_h[h], qe[:, sl].astype(BF16), TN)
                dbl_h.append(ebl[:, sl] * jnp.sum(st_ref[ci, h] * dst1, axis=0, keepdims=True))
                dkd_h.append(_dot(v[:, sl].astype(BF16), dst1b))
                dv2_h.append(dv_h[h] + _dot(kd[:, sl].astype(BF16), dst1b, NT))
            dkd = jnp.concatenate(dkd_h, axis=1)
            dx_scr[ci, 0:CH, :] = dqe * qe + jnp.where(row == CH - 1, jnp.concatenate(dbl_h, axis=1), 0.0)
            dx_scr[ci, CH:2 * CH, :] = dkd * kd
            dlf = _split_dot(wt_ref[...], dx_scr[ci], 2)
            df = dlf / f - (dkd * ekd + jnp.concatenate(dki_h, axis=1))
            dlb_ref[0:1, :] += jnp.sum(df * (1.0 - sig), axis=0, keepdims=True)
            dp_ref[rs, DH:2 * DH] = (df * omlb * sig * (1.0 - sig)).astype(BF16)
            dp_ref[rs, 2 * DH:3 * DH] = jnp.concatenate(dv2_h, axis=1).astype(BF16)

        @pl.when(i == steps - 1)
        def _():
            gl = dlb_ref[0:1, :] * lb * omlb
            dlb_ref[0:1, :] = gl
            dlb_ref[1:2, :] = -gl
            exchange.finish()

    rev = lambda j: pl.BlockSpec((nch * CH, DH), lambda c: (steps - 1 - c, j))
    anyspec = pl.BlockSpec(memory_space=pl.ANY)
    res = pl.pallas_call(
        body, grid=(steps,),
        in_specs=[rev(2), rev(3), rev(4), rev(5), rev(0),
                  pl.BlockSpec((nch, NH, CH, CH), lambda c: (steps - 1 - c, 0, 0, 0)),
                  pl.BlockSpec((nch, NH, HD, HD), lambda c: (steps - 1 - c, 0, 0, 0)),
                  rev(1), _full((2, DH)), _full((1, HD)),
                  _full(w_st.shape), _full(w_st_t.shape), _full(masks_sym.shape),
                  anyspec] + [anyspec] * nw,
        out_specs=[pl.BlockSpec((nch * CH, 4 * DH), lambda c: (steps - 1 - c, 0)), _full((8, DH)), _full((8, HD))]
        + [anyspec] * nw,
        out_shape=[SDS((T, DIN), BF16), SDS((8, DH), F32), SDS((8, HD), F32)] + _slot_shapes(sums),
        scratch_shapes=[pltpu.VMEM((NH, HD, HD), F32), pltpu.VMEM((nch, (2 + n_lev) * CH, DH), F32)] + _exchange_sems(nw),
        input_output_aliases={13: 0},
        compiler_params=_arb(), name="hgrn_bwd")(proj, proj, proj, proj, o_pre, a_all, st_all, dycat, lower_bounds, gn_w,
                                                 w_st, w_st_t, masks_sym, dproj, *sums)
    return res[:3], res[3:]


def _proj_in_bwd(dproj, x, dx1, nw, sc, w_in_b, tm, sums):
    T = x.shape[0]
    ns = len(sums)
    steps = T // tm

    def body(*refs):
        dp_ref, x_ref, dx1_ref, nw_ref, sc_ref, w_ref = refs[:6]
        gx_ref, acc_ref = refs[6 + ns:8 + ns]
        exchange = _ChipExchange(refs[6:6 + ns], refs[8 + ns:8 + 2 * ns], *refs[8 + 2 * ns:])

        @pl.when(pl.program_id(0) == 0)
        def _():
            exchange.start()
            acc_ref[...] = jnp.zeros_like(acc_ref)

        dh = _dot(dp_ref[:, 0:4 * DH], w_ref[:, 2 * DG:DIN], NT) + _dot(dp_ref[:, 4 * DH:DIN], w_ref[:, 0:2 * DG], NT)
        xv = x_ref[...]
        r = _rms(xv)
        xh = xv * r
        n1 = xh * nw_ref[...]
        acc_ref[0:1, :] += jnp.sum(dh, axis=0, keepdims=True)
        acc_ref[1:2, :] += jnp.sum(dh * n1, axis=0, keepdims=True)
        dn = dh * (1.0 + sc_ref[...])
        acc_ref[2:3, :] += jnp.sum(dn * xh, axis=0, keepdims=True)
        gx_ref[...] = dx1_ref[...] + _rms_bwd(xh, r, dn * nw_ref[...])

        @pl.when(pl.program_id(0) == steps - 1)
        def _():
            exchange.finish()

    row = lambda i: (i, 0)
    anyspec = pl.BlockSpec(memory_space=pl.ANY)
    res = pl.pallas_call(
        body, grid=(steps,),
        in_specs=[pl.BlockSpec((tm, DIN), row), pl.BlockSpec((tm, D), row), pl.BlockSpec((tm, D), row),
                  _full((1, D)), _ada_part(ADA_SC1), _resident((D, DIN))] + [anyspec] * ns,
        out_specs=[pl.BlockSpec((tm, D), row), _full((8, D))] + [anyspec] * ns,
        out_shape=[SDS((T, D), F32), SDS((8, D), F32)] + _slot_shapes(sums),
        scratch_shapes=_exchange_sems(ns),
        compiler_params=_arb(), name="proj_in_bwd")(dproj, x, dx1, nw, sc, w_in_b, *sums)
    return res[:2], res[2:]


def _wgrad(a, b, bk, bn, tt, name):
    T, K = a.shape
    N = b.shape[1]
    nn, nk, nt = N // bn, K // bk, T // tt
    bmap = lambda n, k, t: (t, n)

    def body(a_ref, b_ref, o_ref):
        @pl.when(pl.program_id(2) == 0)
        def _():
            o_ref[...] = jnp.zeros_like(o_ref)

        o_ref[0] += _dot(a_ref[...], b_ref[...], TN)

    return pl.pallas_call(
        body, grid=(nn, nk, nt),
        in_specs=[pl.BlockSpec((tt, bk), lambda n, k, t: (t, k)), pl.BlockSpec((tt, bn), bmap)],
        out_specs=pl.BlockSpec((1, bk, bn), lambda n, k, t: (n, k, 0)),
        out_shape=SDS((nn, K, bn), F32),
        compiler_params=_arb(3), name=name)(a, b)


def _adam_math(w, g, m, v):
    m = B1 * m + (1.0 - B1) * g
    v = B2 * v + (1.0 - B2) * (g * g)
    m_hat = m / (1.0 - B1 ** STEP)
    v_hat = v / (1.0 - B2 ** STEP)
    return -LR * (m_hat / (jnp.sqrt(v_hat) + AEPS) + WD * w), m, v


def _adamw_halves(w, mine, sibling, m, v, c_idx, rb, name):
    R, C = w.shape
    nb = (R // 2) // rb

    def body(c_ref, w_ref, a_ref, b_ref, m_ref, v_ref, g_out, d_out, m_out, v_out):
        g = jnp.where(pl.program_id(0) == c_ref[0], a_ref[...], b_ref[...])
        g_out[...] = g
        d_out[...], m_out[...], v_out[...] = _adam_math(w_ref[...], g, m_ref[...], v_ref[...])

    whole = pl.BlockSpec((rb, C), lambda hh, i, cr: (hh * nb + i, 0))
    half = pl.BlockSpec((rb, C), lambda hh, i, cr: (i, 0))
    return pl.pallas_call(
        body,
        grid_spec=pltpu.PrefetchScalarGridSpec(
            num_scalar_prefetch=1, grid=(2, nb), in_specs=[whole, half, half, whole, whole], out_specs=[whole] * 4),
        out_shape=[SDS((R, C), F32)] * 4, compiler_params=_arb(2), name=name)(c_idx, w, mine, sibling, m, v)


def _ada_forward(c_all, w_ada):
    n = w_ada.shape[1]

    def body(c_ref, w_ref, ca_ref, p_ref):
        cv = c_ref[...]
        ca = cv * _sigmoid(cv)
        ca_ref[...] = ca
        p_ref[...] = _dot(ca, w_ref[...], precision=HIGHEST)

    return pl.pallas_call(
        body, grid=(n // 512,),
        in_specs=[_full((N_DEV, D)), pl.BlockSpec((D, 512), lambda i: (0, i))],
        out_specs=[_full((N_DEV, D)), pl.BlockSpec((N_DEV, 512), lambda i: (0, i))],
        out_shape=[SDS((N_DEV, D), F32), SDS((N_DEV, n), F32)],
        compiler_params=_arb(), name="ada_forward")(c_all, w_ada)


def _ada_wgrad_adam(cact_t, dada_all, w, m, v, chip_idx):
    R, C = w.shape
    rb = 256

    def body(j_ref, c_ref, d_ref, w_ref, m_ref, v_ref, g_out, d_out, m_out, v_out):
        g = _dot(c_ref[...], d_ref[...], precision=HIGHEST)
        g_out[...] = g
        d_out[...], m_out[...], v_out[...] = _adam_math(w_ref[...], g, m_ref[...], v_ref[...])

    spec = pl.BlockSpec((rb, C), lambda i, j: (i, 0))
    return pl.pallas_call(
        body,
        grid_spec=pltpu.PrefetchScalarGridSpec(
            num_scalar_prefetch=1, grid=(R // rb,),
            in_specs=[pl.BlockSpec((rb, N_DEV), lambda i, j: (i, 0)), pl.BlockSpec((N_DEV, C), lambda i, j: (0, j[0])),
                      spec, spec, spec],
            out_specs=[spec] * 4),
        out_shape=[SDS((R, C), F32)] * 4,
        compiler_params=_arb(), name="ada_wgrad_adam")(chip_idx, cact_t, dada_all, w, m, v)


SMALL_NAMES = ('b_ada', 'norm1_w', 'norm2_w', 'final_norm_w', 'v_ln_w', 'v_ln_b', 'lower_bounds', 'gn_w', 'b_s', 'w_s')


def _small_finalize(gathered, params, moms, vels):
    n_in = len(gathered)

    def body(*refs):
        acc1, acc2, dln, dlb, dgn, dbs, dws = refs[:n_in]
        prm = [dict(zip(SMALL_NAMES, refs[n_in + k * 10:n_in + (k + 1) * 10])) for k in range(3)]
        outs = [dict(zip(SMALL_NAMES, refs[n_in + 30 + k * 10:n_in + 30 + (k + 1) * 10])) for k in range(4)]
        loss_ref, dada_ref = refs[n_in + 70:n_in + 72]

        def dev_sum(ref, first, n):
            per = ref.shape[0] // N_DEV
            g = ref[first:first + n, :]
            for dev in range(1, N_DEV):
                g = g + ref[dev * per + first:dev * per + first + n, :]
            return g

        def update(n, g, cols=slice(None)):
            outs[0][n][:, cols] = g
            outs[1][n][:, cols], outs[2][n][:, cols], outs[3][n][:, cols] = _adam_math(
                prm[0][n][:, cols], g, prm[1][n][:, cols], prm[2][n][:, cols])

        ada_rows = ((acc1, 0), (acc1, 1), (acc2, 5), (acc2, 2), (acc2, 1), (acc2, 0))
        for k, (ref, r) in enumerate(ada_rows):
            update('b_ada', dev_sum(ref, r, 1), slice(k * D, (k + 1) * D))
            for dev in range(N_DEV):
                dada_ref[dev:dev + 1, k * D:(k + 1) * D] = ref[8 * dev + r:8 * dev + r + 1, :]
        update('norm1_w', dev_sum(acc1, 2, 1))
        update('norm2_w', dev_sum(acc2, 3, 1))
        update('final_norm_w', dev_sum(acc2, 4, 1))
        update('v_ln_w', dev_sum(dln, 0, 1))
        update('v_ln_b', dev_sum(dln, 1, 1))
        update('lower_bounds', dev_sum(dlb, 0, 2))
        update('gn_w', dev_sum(dgn, 0, 1))
        update('b_s', dev_sum(dbs, 0, NH))
        update('w_s', dev_sum(dws, 0, NH * BLK))
        loss_ref[...] = jnp.sum(dev_sum(acc2, 6, 1), axis=-1, keepdims=True)

    shapes = [SDS(params[n].shape, F32) for n in SMALL_NAMES]
    res = pl.pallas_call(
        body, out_shape=shapes * 4 + [SDS((1, 1), F32), SDS((N_DEV, 6 * D), F32)], name="small_finalize")(
            *gathered, *[d[n] for d in (params, moms, vels) for n in SMALL_NAMES])
    return [dict(zip(SMALL_NAMES, res[k * 10:(k + 1) * 10])) for k in range(4)], res[40], res[41]


def _position():
    x, y, c = lax.axis_index("x"), lax.axis_index("y"), lax.axis_index("c")
    return x, y, c


def _chip_at(x, y, r):
    return (x ^ (r >> 1), y ^ (r & 1))


def _all_gather_rows(blocks, name):
    nb = len(blocks)

    def body(*refs):
        ins, outs = refs[:nb], refs[nb:2 * nb]
        send_sems, recv_sems, local_sems = refs[2 * nb:]
        x, y, c = _position()
        me, sibling = (x, y, c), (x, y, 1 - c)
        chips = [_chip_at(x, y, r) for r in (1, 2, 3)]

        def rows(b, px, py, pc):
            m_per = ins[b].shape[0]
            return outs[b].at[pl.ds((4 * px + 2 * py + pc) * m_per, m_per), :]

        def copy(b, k, blk, to, src=None):
            return pltpu.make_async_remote_copy(
                src_ref=rows(b, *blk) if src is None else src, dst_ref=rows(b, *blk),
                send_sem=send_sems.at[7 * b + k], recv_sem=recv_sems.at[7 * b + k], device_id=to, device_id_type=MESH)

        local, sent = [], []
        for b in range(nb):
            mine = pltpu.make_async_copy(ins[b], rows(b, *me), local_sems.at[b])
            mine.start()
            local.append(mine)
            first = [copy(b, 0, me, sibling, src=ins[b])]
            first += [copy(b, 1 + j, me, (*chip, c), src=ins[b]) for j, chip in enumerate(chips)]
            for cp in first:
                cp.start()
            sent += first
        for b in range(nb):
            for j, chip in enumerate(chips):
                copy(b, 1 + j, (*chip, c), me).wait_recv()
                passed = copy(b, 4 + j, (*chip, c), sibling)
                passed.start()
                sent.append(passed)
        for b in range(nb):
            copy(b, 0, sibling, me).wait_recv()
            for j, chip in enumerate(chips):
                copy(b, 4 + j, (*chip, 1 - c), me).wait_recv()
        for cp in sent:
            cp.wait_send()
        for cp in local:
            cp.wait()

    vmem = pl.BlockSpec(memory_space=pltpu.VMEM)
    return pl.pallas_call(
        body, out_shape=[SDS((N_DEV * b.shape[0], b.shape[1]), b.dtype) for b in blocks],
        in_specs=[vmem] * nb, out_specs=[vmem] * nb,
        scratch_shapes=[pltpu.SemaphoreType.DMA((7 * nb,)), pltpu.SemaphoreType.DMA((7 * nb,)),
                        pltpu.SemaphoreType.DMA((nb,))],
        name=name)(*blocks)


def _place_shard(w_shard, axis, chip_idx, name):
    R, C = w_shard.shape
    rb = _row_block(R)
    nb = R // rb
    full = (R * N_CHIPS, C) if axis == 0 else (R, C * N_CHIPS)
    omap = (lambda i, j: (j[0] * nb + i, 0)) if axis == 0 else (lambda i, j: (i, j[0]))

    def body(j_ref, w_ref, o_ref):
        o_ref[...] = w_ref[...].astype(BF16)

    return pl.pallas_call(
        body,
        grid_spec=pltpu.PrefetchScalarGridSpec(
            num_scalar_prefetch=1, grid=(nb,), in_specs=[pl.BlockSpec((rb, C), lambda i, j: (i, 0))],
            out_specs=pl.BlockSpec((rb, C), omap)),
        out_shape=SDS(full, BF16), compiler_params=_arb(), name=name)(chip_idx, w_shard)


class _WeightGather:
    def __init__(self, refs, axes, send_sems, recv_sems):
        self.refs, self.axes, self.send_sems, self.recv_sems = refs, axes, send_sems, recv_sems
        self.x, self.y, self.c = _position()
        self.j = 2 * self.x + self.y
        self.n = 3 * len(refs)

    def _half(self, w, chip_idx, half):
        ref, axis = self.refs[w], self.axes[w]
        if axis == 0:
            size = ref.shape[0] // N_CHIPS
            return ref.at[pl.ds(chip_idx * size + half * (size // 2), size // 2), :]
        size = ref.shape[1] // N_CHIPS
        rows = ref.shape[0] // 2
        return ref.at[pl.ds(half * rows, rows), pl.ds(chip_idx * size, size)]

    def _ici(self, w, r, chip_idx):
        k = 3 * w + r - 1
        piece = self._half(w, chip_idx, self.c)
        return pltpu.make_async_remote_copy(
            src_ref=piece, dst_ref=piece, send_sem=self.send_sems.at[k], recv_sem=self.recv_sems.at[k],
            device_id=(*_chip_at(self.x, self.y, r), self.c), device_id_type=MESH)

    def _d2d(self, w, r, half):
        k = self.n + 3 * w + r - 1
        piece = self._half(w, self.j ^ r, half)
        return pltpu.make_async_remote_copy(
            src_ref=piece, dst_ref=piece, send_sem=self.send_sems.at[k], recv_sem=self.recv_sems.at[k],
            device_id=(self.x, self.y, 1 - self.c), device_id_type=MESH)

    def _each(self):
        return [(w, r) for w in range(len(self.refs)) for r in (1, 2, 3)]

    def start(self):
        for w, r in self._each():
            self._ici(w, r, self.j).start()

    def forward(self):
        for w, r in self._each():
            self._ici(w, r, self.j ^ r).wait_recv()
            self._d2d(w, r, self.c).start()

    def finish(self):
        for w, r in self._each():
            self._ici(w, r, self.j).wait_send()
            self._d2d(w, r, self.c).wait_send()
            self._d2d(w, r, 1 - self.c).wait_recv()


def _gather_sems(n_weights):
    return [pltpu.SemaphoreType.DMA((6 * n_weights,)), pltpu.SemaphoreType.DMA((6 * n_weights,))]


def _gather_weights(placed, axes, name):
    nw = len(placed)

    def body(*refs):
        outs = refs[nw:2 * nw]
        g = _WeightGather(outs, axes, *refs[2 * nw:])
        g.start()
        g.forward()
        g.finish()

    anyspec = pl.BlockSpec(memory_space=pl.ANY)
    return pl.pallas_call(
        body, out_shape=[SDS(a.shape, a.dtype) for a in placed], in_specs=[anyspec] * nw, out_specs=[anyspec] * nw,
        scratch_shapes=_gather_sems(nw), input_output_aliases={i: i for i in range(nw)},
        name=name)(*placed)


class _ChipExchange:
    def __init__(self, ins, outs, send_sems, recv_sems):
        self.ins, self.outs, self.send_sems, self.recv_sems = ins, outs, send_sems, recv_sems
        self.x, self.y, self.c = _position()
        self.j = 2 * self.x + self.y

    def _copies(self):
        for w in range(len(self.ins)):
            for r in (1, 2, 3):
                k = 3 * w + r - 1
                yield pltpu.make_async_remote_copy(
                    src_ref=self.ins[w].at[self.j ^ r], dst_ref=self.outs[w].at[r - 1],
                    send_sem=self.send_sems.at[k], recv_sem=self.recv_sems.at[k],
                    device_id=(*_chip_at(self.x, self.y, r), self.c), device_id_type=MESH)

    def start(self):
        for cp in self._copies():
            cp.start()

    def finish(self):
        for cp in self._copies():
            cp.wait()


def _exchange_sems(n_weights):
    return [pltpu.SemaphoreType.DMA((3 * n_weights,)), pltpu.SemaphoreType.DMA((3 * n_weights,))]


class _CoreExchange:
    def __init__(self, ins, outs, send_sems, recv_sems):
        self.ins, self.outs, self.send_sems, self.recv_sems = ins, outs, send_sems, recv_sems
        self.x, self.y, self.c = _position()

    def _copies(self):
        for w in range(len(self.ins)):
            yield pltpu.make_async_remote_copy(
                src_ref=self.ins[w].at[:, 1 - self.c], dst_ref=self.outs[w],
                send_sem=self.send_sems.at[w], recv_sem=self.recv_sems.at[w],
                device_id=(self.x, self.y, 1 - self.c), device_id_type=MESH)

    def start(self):
        for cp in self._copies():
            cp.start()

    def finish(self):
        for cp in self._copies():
            cp.wait()


def _core_exchange_shapes(grads):
    return [SDS((g.shape[0], g.shape[2], g.shape[3]), F32) for g in grads]


def _core_exchange_sems(n):
    return [pltpu.SemaphoreType.DMA((n,)), pltpu.SemaphoreType.DMA((n,))]


def _exchange_core_halves(grads, name):
    nw = len(grads)

    def body(*refs):
        ex = _CoreExchange(refs[:nw], refs[nw:2 * nw], *refs[2 * nw:])
        ex.start()
        ex.finish()

    anyspec = pl.BlockSpec(memory_space=pl.ANY)
    return pl.pallas_call(
        body, out_shape=_core_exchange_shapes(grads), in_specs=[anyspec] * nw, out_specs=[anyspec] * nw,
        scratch_shapes=_core_exchange_sems(nw), name=name)(*grads)


def _add_core_halves(g4, recv, c_idx, rb, name):
    ns, _, rh, C = g4.shape

    def body(c_ref, g_ref, r_ref, o_ref):
        o_ref[...] = (g_ref[0] + r_ref[...]).astype(BF16)

    return pl.pallas_call(
        body,
        grid_spec=pltpu.PrefetchScalarGridSpec(
            num_scalar_prefetch=1, grid=(ns, rh // rb),
            in_specs=[pl.BlockSpec((1, 1, rb, C), lambda s, i, cr: (s, cr[0], i, 0)),
                      pl.BlockSpec((1, rb, C), lambda s, i, cr: (s, i, 0))],
            out_specs=pl.BlockSpec((1, rb, C), lambda s, i, cr: (s, i, 0))),
        out_shape=SDS((ns, rh, C), BF16), compiler_params=_arb(2), name=name)(c_idx, g4, recv)


def _slot_shapes(sums):
    return [SDS((3,) + s.shape[1:], s.dtype) for s in sums]


def _add_chips(own, slots, order, rb, name):
    _, rh, C = slots.shape

    def body(o_ref, own_ref, a_ref, b_ref, c_ref, d_ref, out_ref):
        mine = own_ref[0].astype(F32)
        t = [jnp.where(o_ref[i] == 0, mine, r[0].astype(F32)) for i, r in enumerate((a_ref, b_ref, c_ref, d_ref))]
        out_ref[...] = ((t[0] + t[1]) + t[2]) + t[3]

    def spec(i):
        return pl.BlockSpec((1, rb, C), lambda t, o: (jnp.maximum(o[i], 1) - 1, t, 0))

    return pl.pallas_call(
        body,
        grid_spec=pltpu.PrefetchScalarGridSpec(
            num_scalar_prefetch=1, grid=(rh // rb,),
            in_specs=[pl.BlockSpec((1, rb, C), lambda t, o: (o[4], t, 0)), spec(0), spec(1), spec(2), spec(3)],
            out_specs=pl.BlockSpec((rb, C), lambda t, o: (t, 0))),
        out_shape=SDS((rh, C), F32), compiler_params=_arb(), name=name)(order, own, slots, slots, slots, slots)


def _share_halves(halves):
    nw = len(halves)

    def body(*refs):
        ins, outs = refs[:nw], refs[nw:2 * nw]
        send_sems, recv_sems = refs[2 * nw:]
        x, y, c = _position()
        started = []
        for w in range(nw):
            cp = pltpu.make_async_remote_copy(
                src_ref=ins[w], dst_ref=outs[w], send_sem=send_sems.at[w], recv_sem=recv_sems.at[w],
                device_id=(x, y, 1 - c), device_id_type=MESH)
            cp.start()
            started.append(cp)
        for cp in started:
            cp.wait()

    anyspec = pl.BlockSpec(memory_space=pl.ANY)
    return pl.pallas_call(
        body, out_shape=[SDS(h.shape, F32) for h in halves], in_specs=[anyspec] * nw, out_specs=[anyspec] * nw,
        scratch_shapes=[pltpu.SemaphoreType.DMA((nw,)), pltpu.SemaphoreType.DMA((nw,))],
        name="share_halves")(*halves)


def _small_2d(b_ada, norm1_w, norm2_w, final_norm_w, v_ln_w, v_ln_b, lower_bounds, gn_w, b_s, w_s):
    return dict(zip(SMALL_NAMES, (b_ada, norm1_w, norm2_w, final_norm_w.reshape(1, D), v_ln_w, v_ln_b, lower_bounds, gn_w,
                                  b_s.reshape(NH, BLK), w_s.reshape(NH * BLK, BLK))))


def _small_original_shapes(d):
    out = dict(d)
    out['final_norm_w'] = d['final_norm_w'].reshape(D)
    out['b_s'] = d['b_s'].reshape(1, NH, BLK)
    out['w_s'] = d['w_s'].reshape(1, NH, BLK, BLK)
    return out


def _row_block(r):
    for cand in (256, 176, 128, 64, 32, 16, 8):
        if r % cand == 0:
            return cand
    return r


def kernel(x, c, w_ada, b_ada, norm1_w, w_in, w_s, b_s, v_ln_w, v_ln_b, lower_bounds, gn_w, w_out, norm2_w, w_ffn_in, w_ffn_out, final_norm_w, loss_target, m_w_ada, m_b_ada, m_norm1_w, m_w_in, m_w_s, m_b_s, m_v_ln_w, m_v_ln_b, m_lower_bounds, m_gn_w, m_w_out, m_norm2_w, m_w_ffn_in, m_w_ffn_out, m_final_norm_w, v_w_ada, v_b_ada, v_norm1_w, v_w_in, v_w_s, v_b_s, v_v_ln_w, v_v_ln_b, v_lower_bounds, v_gn_w, v_w_out, v_norm2_w, v_w_ffn_in, v_w_ffn_out, v_final_norm_w):
    T = x.shape[1]
    tm, tp = min(TOKEN_TILE, T), min(PROJ_TILE, T)
    px, py, pc = _position()
    chip = 2 * px + py
    me = 4 * px + 2 * py + pc
    x2d = x.reshape(T, D)
    tgt = loss_target.reshape(T, D)

    (c_all,) = _all_gather_rows([jnp.broadcast_to(c, (8, D))], "gather_c")
    cact, ada_part = _ada_forward(c_all.reshape(N_DEV, 8, D)[:, 0, :], w_ada[0])
    n_ada = ada_part.shape[1]
    (ada_all,) = _all_gather_rows([ada_part], "gather_ada")
    ada_all = ada_all.reshape(N_CHIPS, 2, N_DEV, n_ada)[:, 0]
    ada = lax.dynamic_index_in_dim(ada_all, me, axis=1, keepdims=False).reshape(1, 6 * D) + b_ada

    chip_idx = jnp.reshape(chip, (1,)).astype(jnp.int32)
    c_idx = jnp.reshape(pc, (1,)).astype(jnp.int32)
    (w_in_b,) = _gather_weights([_place_shard(w_in[0], 1, chip_idx, "place_in")], [1], "gather_w_in")
    placed = [_place_shard(w_out[0], 0, chip_idx, "place_out"), _place_shard(w_ffn_in[0], 1, chip_idx, "place_ffn_in"),
              _place_shard(w_ffn_out[0], 0, chip_idx, "place_ffn_out")]

    rr = lax.broadcasted_iota(jnp.int32, (BLK, BLK), 0) // CH
    cc = lax.broadcasted_iota(jnp.int32, (BLK, BLK), 1) // CH
    ws_b = jnp.where((rr >= cc)[None], w_s[0], 0.0).astype(BF16)
    bst = b_s[0].T
    lnw, lnb = v_ln_w, v_ln_b
    nw1, nw2, fw = norm1_w, norm2_w, final_norm_w.reshape(1, D)

    h1, proj = _proj_in(x2d, nw1, ada, ada, w_in_b, tp)
    ycat = _gmlp_fwd(proj, ws_b, bst, lnw, lnb)
    tables = _hgrn_tables()
    (ycat, o_pre, a_all, st_all), (w_out_b, w_fi_b, w_fo_b) = _hgrn_fwd(
        proj, lower_bounds, gn_w, ycat, tables, placed, [0, 1, 0])

    dycat, dx1, h2, act, dff, dgu, dmix, acc2 = _token_local(
        x2d, ycat, tgt, ada, nw2, ada, ada, ada, fw, w_out_b, w_fi_b, w_fo_b, tm)

    tt = min(WGRAD_TOKENS, T)
    order = jnp.concatenate([chip ^ jnp.arange(N_CHIPS, dtype=jnp.int32), chip_idx]).astype(jnp.int32)

    def by_core_half(g):
        return g.reshape(N_CHIPS, 2, g.shape[1] // 2, g.shape[2])

    def core_sums(g4, recv, names):
        return [_add_core_halves(a, b, c_idx, _row_block(a.shape[2]), "add_core_" + n) for a, b, n in zip(g4, recv, names)]

    def chip_sums(sums, slots, names):
        return [_add_chips(o, s, order, _row_block(s.shape[1]), "add_chips_" + n) for o, s, n in zip(sums, slots, names)]

    g_out = _wgrad(ycat, dmix, D, D, tt, "wgrad_out").reshape(N_CHIPS, D // N_CHIPS, D)
    g_fi = _wgrad(h2, dgu, D, FFB, tt, "wgrad_ffn_in")
    g_fo = _wgrad(act, dff, FFB, D, tt, "wgrad_ffn_out").reshape(N_CHIPS, DFF // N_CHIPS, D)
    late_names = ["out", "ffn_in", "ffn_out"]
    late_g4 = [by_core_half(g) for g in (g_out, g_fi, g_fo)]

    (dproj, dws, dbs, dln), late_recv = _gmlp_bwd(proj, dycat, ws_b, bst, lnw, lnb, late_g4)
    late_sums = core_sums(late_g4, late_recv, late_names)
    (dproj, dlb, dgn), late_slots = _hgrn_bwd(
        proj, o_pre, a_all, st_all, dycat, lower_bounds, gn_w, dproj, tables, late_sums)

    g_in = _wgrad(h1, dproj, D, D, tt, "wgrad_in")
    g_in = jnp.concatenate([g_in[2], g_in[0], g_in[1]], axis=1).reshape(D, N_CHIPS, DIN // N_CHIPS).transpose(1, 0, 2)
    in_g4 = [by_core_half(g_in)]
    in_sums = core_sums(in_g4, _exchange_core_halves(in_g4, "exchange_core_halves_in"), ["in"])
    (grad_x, acc1), in_slots = _proj_in_bwd(dproj, x2d, dx1, nw1, ada, w_in_b, tp, in_sums)
    names = ["in"] + late_names
    halves = chip_sums(in_sums, in_slots, ["in"]) + chip_sums(late_sums, late_slots, late_names)
    sibling_halves = _share_halves(halves)

    big_w = [(w_in, m_w_in, v_w_in), (w_out, m_w_out, v_w_out), (w_ffn_in, m_w_ffn_in, v_w_ffn_in),
             (w_ffn_out, m_w_ffn_out, v_w_ffn_out)]
    big_out = []
    for mine, sib, (w, m, v), n in zip(halves, sibling_halves, big_w, names):
        res = _adamw_halves(w[0], mine, sib, m[0], v[0], c_idx, _row_block(mine.shape[0]), "adamw_" + n)
        big_out.append([r[None] for r in res])

    gathered = _all_gather_rows([acc1, acc2, dln, dlb, dgn, dbs, dws], "gather_small")
    small, loss, dada_all = _small_finalize(
        gathered,
        _small_2d(b_ada, norm1_w, norm2_w, final_norm_w, v_ln_w, v_ln_b, lower_bounds, gn_w, b_s, w_s),
        _small_2d(m_b_ada, m_norm1_w, m_norm2_w, m_final_norm_w, m_v_ln_w, m_v_ln_b, m_lower_bounds, m_gn_w, m_b_s, m_w_s),
        _small_2d(v_b_ada, v_norm1_w, v_norm2_w, v_final_norm_w, v_v_ln_w, v_v_ln_b, v_lower_bounds, v_gn_w, v_b_s, v_w_s))
    small = [_small_original_shapes(d) for d in small]
    loss = loss.reshape(())

    ada_out = [o[None] for o in _ada_wgrad_adam(cact.T, dada_all, w_ada[0], m_w_ada[0], v_w_ada[0], chip_idx)]

    order_names = ['w_ada', 'b_ada', 'norm1_w', 'w_in', 'w_s', 'b_s', 'v_ln_w', 'v_ln_b', 'lower_bounds', 'gn_w',
                   'w_out', 'norm2_w', 'w_ffn_in', 'w_ffn_out', 'final_norm_w']
    big_idx = {'w_in': 0, 'w_out': 1, 'w_ffn_in': 2, 'w_ffn_out': 3}
    outs = [loss, grad_x.reshape(1, T, D)]
    for kind in range(4):
        for n in order_names:
            if n == 'w_ada':
                outs.append(ada_out[kind])
            elif n in big_idx:
                outs.append(big_out[big_idx[n]][kind])
            else:
                outs.append(small[kind][n])
    return tuple(outs)
```

```python
import functools

import jax
import jax.numpy as jnp
import numpy as np
from jax import lax
from jax.experimental import pallas as pl
from jax.experimental.pallas import tpu as pltpu

F32 = jnp.float32
BF16 = jnp.bfloat16
SDS = jax.ShapeDtypeStruct
MESH = pl.DeviceIdType.MESH
HIGHEST = lax.Precision.HIGHEST

D = 1024
DG = 512
DH = 512
NH = 4
HD = 128
BLK = 128
CH = 64
DFF = 2816
DIN = 3072
FFB = 1408
LEVELS = (64, 32, 16, 8, 4, 2)
HGRN_CHUNKS_PER_STEP = 8
GMLP_ROWS_PER_STEP = 1024
TOKEN_TILE = 256
PROJ_TILE = 512
WGRAD_TOKENS = 2048
N_CHIPS = 4
N_DEV = 8
EPS = 1e-6
LR, B1, B2, AEPS, WD, STEP = 0.001, 0.9, 0.999, 1e-08, 0.01, 10

NT = (((1,), (1,)), ((), ()))
TN = (((0,), (0,)), ((), ()))


def _full(shape):
    nd = len(shape)
    return pl.BlockSpec(shape, lambda *_: (0,) * nd)


ADA_SH1, ADA_SC1, ADA_G1, ADA_SH2, ADA_SC2, ADA_G2 = range(6)


def _ada_part(k):
    return pl.BlockSpec((1, D), lambda *_: (0, k))


def _resident(shape):
    nd = len(shape)
    return pl.BlockSpec(shape, lambda *_: (0,) * nd, pipeline_mode=pl.Buffered(1))


def _arb(n=1):
    return pltpu.CompilerParams(dimension_semantics=("arbitrary",) * n)


def _dot(a, b, dims=None, precision=None):
    if dims is None:
        return jnp.dot(a, b, preferred_element_type=F32, precision=precision)
    return lax.dot_general(a, b, dims, preferred_element_type=F32, precision=precision)


def _sigmoid(x):
    return jax.nn.sigmoid(x)


def _gelu_parts(x):
    cdf = 0.5 * (1.0 + lax.erf(x * 0.7071067811865476))
    pdf = jnp.exp(-0.5 * x * x) * 0.3989422804014327
    return x * cdf, cdf + x * pdf


def _rms(x):
    return lax.rsqrt(jnp.mean(x * x, axis=-1, keepdims=True) + EPS)


def _rms_bwd(xhat, r, gw):
    return r * (gw - xhat * jnp.mean(xhat * gw, axis=-1, keepdims=True))


def _lower_bound(lbp_ref):
    l0, l1 = lbp_ref[0:1, :], lbp_ref[1:2, :]
    m = jnp.maximum(l0, l1)
    e0, e1 = jnp.exp(l0 - m), jnp.exp(l1 - m)
    return e0 / (e0 + e1), e1 / (e0 + e1)


def _proj_in(x, nw, sc, sh, w_in_b, tm):
    T = x.shape[0]

    def body(x_ref, nw_ref, sc_ref, sh_ref, w_ref, h_ref, p_ref):
        xv = x_ref[...]
        h = ((xv * _rms(xv)) * nw_ref[...]) * (1.0 + sc_ref[...]) + sh_ref[...]
        hb = h.astype(BF16)
        h_ref[...] = hb
        p_ref[...] = _dot(hb, w_ref[...])

    row = lambda i: (i, 0)
    return pl.pallas_call(
        body, grid=(T // tm,),
        in_specs=[pl.BlockSpec((tm, D), row), _full((1, D)), _ada_part(ADA_SC1), _ada_part(ADA_SH1), _resident((D, DIN))],
        out_specs=[pl.BlockSpec((tm, D), row), pl.BlockSpec((tm, DIN), row)],
        out_shape=[SDS((T, D), BF16), SDS((T, DIN), F32)],
        compiler_params=_arb(), name="proj_in")(x, nw, sc, sh, w_in_b)


def _gmlp_common(u, v, lnw, lnb, ws_ref, bst_ref):
    ug, dug = _gelu_parts(u)
    vg, dvg = _gelu_parts(v)
    mu = jnp.mean(vg, axis=-1, keepdims=True)
    vc = vg - mu
    rstd = lax.rsqrt(jnp.mean(vc * vc, axis=-1, keepdims=True) + EPS)
    vhat = vc * rstd
    vn = vhat * lnw + lnb
    vnb = vn.astype(BF16)
    mixed = []
    for h in range(NH):
        sl = slice(h * HD, (h + 1) * HD)
        mixed.append(_dot(ws_ref[h], vnb[:, sl]) + bst_ref[:, h:h + 1])
    return ug, dug, dvg, rstd, vhat, vnb, jnp.concatenate(mixed, axis=1)


def _gmlp_fwd(proj, ws_b, bst, lnw, lnb):
    T = proj.shape[0]
    rows = min(GMLP_ROWS_PER_STEP, T)

    def body(u_ref, v_ref, ws_ref, bst_ref, lnw_ref, lnb_ref, y_ref):
        for bi in range(rows // BLK):
            rs = slice(bi * BLK, (bi + 1) * BLK)
            ug, _, _, _, _, _, mixed = _gmlp_common(u_ref[rs, :], v_ref[rs, :], lnw_ref[...], lnb_ref[...], ws_ref, bst_ref)
            y_ref[rs, :] = (ug * mixed).astype(BF16)

    return pl.pallas_call(
        body, grid=(T // rows,),
        in_specs=[pl.BlockSpec((rows, DG), lambda i: (i, 0)), pl.BlockSpec((rows, DG), lambda i: (i, 1)),
                  _full((NH, BLK, BLK)), _full((BLK, NH)), _full((1, DG)), _full((1, DG))],
        out_specs=pl.BlockSpec((rows, DG), lambda i: (i, 0)),
        out_shape=SDS((T, D), BF16),
        compiler_params=_arb(), name="gmlp_fwd")(proj, proj, ws_b, bst, lnw, lnb)


def _hgrn_tables():
    t = np.arange(CH)[:, None]
    j = np.arange(CH)[None, :]
    blocks = [j <= t, j > t]
    masks = []
    for n in LEVELS:
        mid = t - t % n + n // 2
        blocks.append(np.where(t >= mid, (j >= mid) & (j <= t), (j > t) & (j < mid)))
        masks.append((t // n == j // n) & (t % n >= n // 2) & (j % n < n // 2))
    w = np.concatenate(blocks, axis=0).astype(np.float32)
    m = np.stack(masks).astype(np.float32)
    return (jnp.asarray(w, BF16), jnp.asarray(w.T, BF16), jnp.asarray(m), jnp.asarray(m + m.transpose(0, 2, 1)))


def _split_dot(w, x, parts):
    acc = None
    for _ in range(parts):
        piece = x.astype(BF16)
        term = _dot(w, piece)
        acc = term if acc is None else acc + term
        x = x - piece.astype(F32)
    return acc


def _hgrn_exponents(lf, w_ref):
    b = _split_dot(w_ref[0:CH, :], lf, 3)
    row = lax.broadcasted_iota(jnp.int32, (CH, 1), 0)
    blocks = [b, b[CH - 1:CH, :] - b]
    for n in LEVELS:
        up = (row & (n // 2)) != 0
        if n >= 8:
            ref = b.reshape(CH // n, n, DH)[:, n // 2 - 1:n // 2, :]
            ref = jnp.broadcast_to(ref, (CH // n, n, DH)).reshape(CH, DH)
            blocks.append(jnp.where(up, b - ref, ref - b))
        elif n == 4:
            r4 = row & 3
            two = jnp.where(r4 == 3, pltpu.roll(lf, 1, 0) + lf, 0.0)
            blocks.append(jnp.where(r4 == 0, pltpu.roll(lf, CH - 1, 0), jnp.where(r4 == 2, lf, two)))
        else:
            blocks.append(jnp.where(up, lf, 0.0))
    return blocks


def _hgrn_gates(q, fl, lb, omlb, w_ref):
    sq = _sigmoid(q)
    qf = q * sq
    sig = _sigmoid(fl)
    f = lb + omlb * sig
    k = 1.0 - f
    e = [jnp.exp(x) for x in _hgrn_exponents(jnp.log(f), w_ref)]
    return sq, qf, sig, f, k, e


def _level_factor(e, li, sl, row, qh, kh):
    el = e[2 + li][:, sl]
    up = (row & (LEVELS[li] // 2)) != 0
    return el, up, el * jnp.where(up, qh, kh)


def _hgrn_fwd(proj, lower_bounds, gn_w, ycat, tables, placed, axes):
    T = proj.shape[0]
    nc = T // CH
    nch = min(HGRN_CHUNKS_PER_STEP, nc)
    steps = nc // nch
    w_st, _, masks, _ = tables
    nw = len(placed)
    pass_step = (5 * steps) // 8

    def body(*refs):
        q_ref, f_ref, i_ref, g_ref, lbp_ref, gn_ref, w_ref, m_ref = refs[:8]
        y_ref, o_ref, a_ref, st_ref = refs[9 + nw:13 + nw]
        s_scr, send_sems, recv_sems = refs[13 + 2 * nw:]
        gather = _WeightGather(refs[13 + nw:13 + 2 * nw], axes, send_sems, recv_sems)
        step = pl.program_id(0)

        @pl.when(step == 0)
        def _():
            gather.start()
            s_scr[...] = jnp.zeros_like(s_scr)

        @pl.when(step == pass_step)
        def _():
            gather.forward()

        lb, omlb = _lower_bound(lbp_ref)
        row = lax.broadcasted_iota(jnp.int32, (CH, 1), 0)
        eye = lax.broadcasted_iota(jnp.int32, (CH, CH), 0) == lax.broadcasted_iota(jnp.int32, (CH, CH), 1)
        pre = []
        for ci in range(nch):
            rs = slice(ci * CH, (ci + 1) * CH)
            _, qf, _, _, k, e = _hgrn_gates(q_ref[rs, :], f_ref[rs, :], lb, omlb, w_ref)
            mats = []
            for h in range(NH):
                sl = slice(h * HD, (h + 1) * HD)
                qh, kh = qf[:, sl], k[:, sl]
                a = jnp.where(eye, jnp.sum(qh * kh, axis=-1, keepdims=True), 0.0)
                for li in range(len(LEVELS)):
                    _, _, y = _level_factor(e, li, sl, row, qh, kh)
                    yb = y.astype(BF16)
                    a = a + m_ref[li] * _dot(yb, yb, NT)
                a_ref[ci, h] = a
                mats.append(a.astype(BF16))
            eb = e[0]
            pre.append(((qf * eb).astype(BF16), eb[CH - 1:CH, :], (k * e[1]).astype(BF16), mats))
        for ci in range(nch):
            rs = slice(ci * CH, (ci + 1) * CH)
            qe, ebl, kd, mats = pre[ci]
            v = i_ref[rs, :]
            g = g_ref[rs, :]
            for h in range(NH):
                sl = slice(h * HD, (h + 1) * HD)
                st0 = s_scr[h]
                st_ref[ci, h] = st0
                vb = v[:, sl].astype(BF16)
                o = _dot(qe[:, sl], st0.astype(BF16), NT) + _dot(mats[h], vb)
                s_scr[h] = st0 * ebl[:, sl] + _dot(vb, kd[:, sl], TN)
                o_ref[rs, sl] = o
                gh = g[:, sl]
                y_ref[rs, sl] = (((o * _rms(o)) * gn_ref[...]) * (gh * _sigmoid(gh))).astype(BF16)

        @pl.when(step == steps - 1)
        def _():
            gather.finish()

    blk = lambda j: pl.BlockSpec((nch * CH, DH), lambda c: (c, j))
    anyspec = pl.BlockSpec(memory_space=pl.ANY)
    res = pl.pallas_call(
        body, grid=(steps,),
        in_specs=[blk(2), blk(3), blk(4), blk(5), _full((2, DH)), _full((1, HD)),
                  _full(w_st.shape), _full(masks.shape), anyspec] + [anyspec] * nw,
        out_specs=[pl.BlockSpec((nch * CH, DH), lambda c: (c, 1)),
                   pl.BlockSpec((nch * CH, DH), lambda c: (c, 0)),
                   pl.BlockSpec((nch, NH, CH, CH), lambda c: (c, 0, 0, 0)),
                   pl.BlockSpec((nch, NH, HD, HD), lambda c: (c, 0, 0, 0))] + [anyspec] * nw,
        out_shape=[SDS((T, D), BF16), SDS((T, DH), F32), SDS((nc, NH, CH, CH), F32), SDS((nc, NH, HD, HD), F32)]
        + [SDS(a.shape, a.dtype) for a in placed],
        scratch_shapes=[pltpu.VMEM((NH, HD, HD), F32)] + _gather_sems(nw),
        input_output_aliases={8: 0, **{9 + i: 4 + i for i in range(nw)}},
        compiler_params=_arb(), name="hgrn_fwd")(proj, proj, proj, proj, lower_bounds, gn_w, w_st, masks, ycat, *placed)
    return res[:4], res[4:]


def _token_local(x, ycat, tgt, g1, nw2, sc2, sh2, g2, fw, w_out_b, w_fi_b, w_fo_b, tm):
    T = x.shape[0]
    inv_d = 1.0 / D

    def body(x_ref, y_ref, t_ref, g1_ref, nw2_ref, sc2_ref, sh2_ref, g2_ref, fw_ref, wo_ref, wfi_ref, wfo_ref,
             dy_ref, dx1_ref, h2_ref, act_ref, dff_ref, dgu_ref, dmix_ref, acc_ref):
        @pl.when(pl.program_id(0) == 0)
        def _():
            acc_ref[...] = jnp.zeros_like(acc_ref)

        def acc(row, val):
            acc_ref[row:row + 1, :] += jnp.sum(val, axis=0, keepdims=True)

        g1v, g2v = g1_ref[...], g2_ref[...]
        mix = _dot(y_ref[...], wo_ref[...])
        x1 = x_ref[...] + g1v * mix
        r2 = _rms(x1)
        xh2 = x1 * r2
        n2 = xh2 * nw2_ref[...]
        osc2 = 1.0 + sc2_ref[...]
        h2b = (n2 * osc2 + sh2_ref[...]).astype(BF16)
        h2_ref[...] = h2b
        ff = jnp.zeros((tm, D), F32)
        saved = []
        for kb in range(DFF // FFB):
            gate = _dot(h2b, wfi_ref[:, kb * FFB:(kb + 1) * FFB])
            up = _dot(h2b, wfi_ref[:, DFF + kb * FFB:DFF + (kb + 1) * FFB])
            sg = _sigmoid(gate)
            actb = (gate * sg * up).astype(BF16)
            act_ref[:, kb * FFB:(kb + 1) * FFB] = actb
            ff = ff + _dot(actb, wfo_ref[kb * FFB:(kb + 1) * FFB, :])
            saved.append((gate, up, sg))
        x2 = x1 + g2v * ff
        r3 = _rms(x2)
        xh3 = x2 * r3
        err = xh3 * fw_ref[...] - t_ref[...]
        acc(6, (0.5 * inv_d) * err * err)
        dy = err * inv_d
        acc(4, dy * xh3)
        dx2 = _rms_bwd(xh3, r3, dy * fw_ref[...])
        acc(0, dx2 * ff)
        dffb = (dx2 * g2v).astype(BF16)
        dff_ref[...] = dffb
        dh2 = jnp.zeros((tm, D), F32)
        for kb in range(DFF // FFB):
            gate, up, sg = saved[kb]
            da = _dot(dffb, wfo_ref[kb * FFB:(kb + 1) * FFB, :], NT)
            dgate = (da * up * (sg * (1.0 + gate * (1.0 - sg)))).astype(BF16)
            dup = (da * gate * sg).astype(BF16)
            dgu_ref[:, kb * FFB:(kb + 1) * FFB] = dgate
            dgu_ref[:, DFF + kb * FFB:DFF + (kb + 1) * FFB] = dup
            dh2 = dh2 + _dot(dgate, wfi_ref[:, kb * FFB:(kb + 1) * FFB], NT)
            dh2 = dh2 + _dot(dup, wfi_ref[:, DFF + kb * FFB:DFF + (kb + 1) * FFB], NT)
        acc(2, dh2)
        acc(1, dh2 * n2)
        dn2 = dh2 * osc2
        acc(3, dn2 * xh2)
        dx1 = dx2 + _rms_bwd(xh2, r2, dn2 * nw2_ref[...])
        acc(5, dx1 * mix)
        dmixb = (dx1 * g1v).astype(BF16)
        dmix_ref[...] = dmixb
        dy_ref[...] = _dot(dmixb, wo_ref[...], NT)
        dx1_ref[...] = dx1

    row = lambda i: (i, 0)
    vec = _full((1, D))
    return pl.pallas_call(
        body, grid=(T // tm,),
        in_specs=[pl.BlockSpec((tm, D), row), pl.BlockSpec((tm, D), row), pl.BlockSpec((tm, D), row),
                  _ada_part(ADA_G1), vec, _ada_part(ADA_SC2), _ada_part(ADA_SH2), _ada_part(ADA_G2), vec,
                  _resident((D, D)), _resident((D, 2 * DFF)), _resident((DFF, D))],
        out_specs=[pl.BlockSpec((tm, D), row), pl.BlockSpec((tm, D), row), pl.BlockSpec((tm, D), row),
                   pl.BlockSpec((tm, DFF), row), pl.BlockSpec((tm, D), row), pl.BlockSpec((tm, 2 * DFF), row),
                   pl.BlockSpec((tm, D), row), _full((8, D))],
        out_shape=[SDS((T, D), F32), SDS((T, D), F32), SDS((T, D), BF16), SDS((T, DFF), BF16), SDS((T, D), BF16),
                   SDS((T, 2 * DFF), BF16), SDS((T, D), BF16), SDS((8, D), F32)],
        compiler_params=_arb(), name="token_local")(x, ycat, tgt, g1, nw2, sc2, sh2, g2, fw, w_out_b, w_fi_b, w_fo_b)


def _gmlp_bwd(proj, dycat, ws_b, bst, lnw, lnb, grads):
    T = proj.shape[0]
    rows = min(GMLP_ROWS_PER_STEP, T)
    nb = T // rows
    nw = len(grads)

    def body(*refs):
        u_ref, v_ref, dy_ref, ws_ref, bst_ref, lnw_ref, lnb_ref = refs[:7]
        dp_ref, dws_ref, dbs_ref, dln_ref = refs[7 + nw:11 + nw]
        dbs_acc, send_sems, recv_sems = refs[11 + 2 * nw:]
        exchange = _CoreExchange(refs[7:7 + nw], refs[11 + nw:11 + 2 * nw], send_sems, recv_sems)
        i = pl.program_id(0)

        @pl.when(i == 0)
        def _():
            exchange.start()
            dws_ref[...] = jnp.zeros_like(dws_ref)
            dln_ref[...] = jnp.zeros_like(dln_ref)
            dbs_acc[...] = jnp.zeros_like(dbs_acc)

        r = lax.broadcasted_iota(jnp.int32, (BLK, BLK), 0) // CH
        c = lax.broadcasted_iota(jnp.int32, (BLK, BLK), 1) // CH
        for bi in range(rows // BLK):
            rs = slice(bi * BLK, (bi + 1) * BLK)
            ug, dug, dvg, rstd, vhat, vnb, mixed = _gmlp_common(
                u_ref[rs, :], v_ref[rs, :], lnw_ref[...], lnb_ref[...], ws_ref, bst_ref)
            dya = dy_ref[rs, :]
            dp_ref[rs, 0:DG] = (dya * mixed * dug).astype(BF16)
            dmixed = dya * ug
            dbs_acc[...] += dmixed
            dmb = dmixed.astype(BF16)
            dvn = []
            for h in range(NH):
                sl = slice(h * HD, (h + 1) * HD)
                dws_ref[h * BLK:(h + 1) * BLK, :] += jnp.where(r >= c, _dot(dmb[:, sl], vnb[:, sl], NT), 0.0)
                dvn.append(_dot(ws_ref[h], dmb[:, sl], TN))
            dvn = jnp.concatenate(dvn, axis=1)
            dln_ref[0:1, :] += jnp.sum(dvn * vhat, axis=0, keepdims=True)
            dln_ref[1:2, :] += jnp.sum(dvn, axis=0, keepdims=True)
            dvh = dvn * lnw_ref[...]
            dvgel = rstd * (dvh - jnp.mean(dvh, axis=-1, keepdims=True) - vhat * jnp.mean(dvh * vhat, axis=-1, keepdims=True))
            dp_ref[rs, DG:2 * DG] = (dvgel * dvg).astype(BF16)

        @pl.when(i == nb - 1)
        def _():
            head = lax.broadcasted_iota(jnp.int32, (8, BLK), 0)
            ones = jnp.ones((8, HD), F32)
            out = jnp.zeros((8, BLK), F32)
            for h in range(NH):
                sums = _dot(ones, dbs_acc[:, h * HD:(h + 1) * HD], NT, precision=HIGHEST)
                out = out + jnp.where(head == h, sums, 0.0)
            dbs_ref[...] = out
            exchange.finish()

    anyspec = pl.BlockSpec(memory_space=pl.ANY)
    res = pl.pallas_call(
        body, grid=(nb,),
        in_specs=[pl.BlockSpec((rows, DG), lambda i: (i, 0)), pl.BlockSpec((rows, DG), lambda i: (i, 1)),
                  pl.BlockSpec((rows, DG), lambda i: (i, 0)),
                  _full((NH, BLK, BLK)), _full((BLK, NH)), _full((1, DG)), _full((1, DG))] + [anyspec] * nw,
        out_specs=[pl.BlockSpec((rows, 2 * DG), lambda i: (i, 2)), _full((NH * BLK, BLK)), _full((8, BLK)), _full((8, DG))]
        + [anyspec] * nw,
        out_shape=[SDS((T, DIN), BF16), SDS((NH * BLK, BLK), F32), SDS((8, BLK), F32), SDS((8, DG), F32)]
        + _core_exchange_shapes(grads),
        scratch_shapes=[pltpu.VMEM((BLK, DG), F32)] + _core_exchange_sems(nw),
        compiler_params=_arb(), name="gmlp_bwd")(proj, proj, dycat, ws_b, bst, lnw, lnb, *grads)
    return res[:4], res[4:]


def _hgrn_bwd(proj, o_pre, a_all, st_all, dycat, lower_bounds, gn_w, dproj, tables, sums):
    T = proj.shape[0]
    nc = T // CH
    nch = min(HGRN_CHUNKS_PER_STEP, nc)
    steps = nc // nch
    w_st, w_st_t, _, masks_sym = tables
    n_lev = len(LEVELS)
    nw = len(sums)

    def body(*refs):
        q_ref, f_ref, i_ref, g_ref, o_ref, a_ref, st_ref, dy_ref, lbp_ref, gn_ref, w_ref, wt_ref, ms_ref = refs[:13]
        dp_ref, dlb_ref, dgn_ref = refs[14 + nw:17 + nw]
        ds_scr, dx_scr, send_sems, recv_sems = refs[17 + 2 * nw:]
        exchange = _ChipExchange(refs[14:14 + nw], refs[17 + nw:17 + 2 * nw], send_sems, recv_sems)
        i = pl.program_id(0)

        @pl.when(i == 0)
        def _():
            exchange.start()
            ds_scr[...] = jnp.zeros_like(ds_scr)
            dlb_ref[...] = jnp.zeros_like(dlb_ref)
            dgn_ref[...] = jnp.zeros_like(dgn_ref)

        lb, omlb = _lower_bound(lbp_ref)
        row = lax.broadcasted_iota(jnp.int32, (CH, 1), 0)
        eye = lax.broadcasted_iota(jnp.int32, (CH, CH), 0) == lax.broadcasted_iota(jnp.int32, (CH, CH), 1)
        lower = lax.broadcasted_iota(jnp.int32, (CH, CH), 0) > lax.broadcasted_iota(jnp.int32, (CH, CH), 1)
        dgn = jnp.zeros((1, HD), F32)
        pre = []
        for ci in range(nch):
            rs = slice(ci * CH, (ci + 1) * CH)
            q = q_ref[rs, :]
            v = i_ref[rs, :]
            g = g_ref[rs, :]
            sq, qf, sig, f, k, e = _hgrn_gates(q, f_ref[rs, :], lb, omlb, w_ref)
            eb = e[0]
            ekd = e[1]
            kd = k * ekd
            qe = qf * eb
            dob_h, dqe_h, dqf_h, dki_h, dv_h, dg_h = [], [], [], [], [], []
            for h in range(NH):
                sl = slice(h * HD, (h + 1) * HD)
                o = o_ref[rs, sl]
                ro = _rms(o)
                oh = o * ro
                gh = g[:, sl]
                sg = _sigmoid(gh)
                dyb = dy_ref[rs, sl]
                dg_h.append(dyb * (oh * gn_ref[...]) * (sg * (1.0 + gh * (1.0 - sg))))
                don = dyb * (gh * sg)
                dgn = dgn + jnp.sum(don * oh, axis=0, keepdims=True)
                dob = _rms_bwd(oh, ro, don * gn_ref[...]).astype(BF16)
                vb = v[:, sl].astype(BF16)
                qh, kh = qf[:, sl], k[:, sl]
                dqe = _dot(dob, st_ref[ci, h].astype(BF16))
                da = _dot(dob, vb, NT)
                ddiag = jnp.sum(jnp.where(eye, da, 0.0), axis=-1, keepdims=True)
                dsym = jnp.where(lower, da, _dot(vb, dob, NT))
                upper_part = jnp.zeros((CH, HD), F32)
                both = jnp.zeros((CH, HD), F32)
                for li in range(n_lev):
                    el, up, y = _level_factor(e, li, sl, row, qh, kh)
                    dyv = _dot((ms_ref[li] * dsym).astype(BF16), y.astype(BF16))
                    dx_scr[ci, (2 + li) * CH:(3 + li) * CH, sl] = dyv * y
                    dye = dyv * el
                    upper_part = upper_part + jnp.where(up, dye, 0.0)
                    both = both + dye
                dob_h.append(dob)
                dqe_h.append(dqe)
                dqf_h.append(dqe * eb[:, sl] + ddiag * kh + upper_part)
                dki_h.append(ddiag * qh + (both - upper_part))
                dv_h.append(_dot(a_ref[ci, h].astype(BF16), dob, TN))
            dp_ref[rs, 0:DH] = (jnp.concatenate(dqf_h, axis=1) * (sq * (1.0 + q * (1.0 - sq)))).astype(BF16)
            dp_ref[rs, 3 * DH:4 * DH] = jnp.concatenate(dg_h, axis=1).astype(BF16)
            pre.append((v, sig, f, eb, ekd, kd, qe, dob_h, jnp.concatenate(dqe_h, axis=1), dki_h, dv_h))
        dgn_ref[0:1, :] += dgn
        for ci in reversed(range(nch)):
            rs = slice(ci * CH, (ci + 1) * CH)
            v, sig, f, eb, ekd, kd, qe, dob_h, dqe, dki_h, dv_h = pre[ci]
            ebl = eb[CH - 1:CH, :]
            dbl_h, dkd_h, dv2_h = [], [], []
            for h in range(NH):
                sl = slice(h * HD, (h + 1) * HD)
                dst1 = ds_scr[h]
                dst1b = dst1.astype(BF16)
                ds_scr[h] = dst1 * ebl[:, sl] + _dot(dob_h[h], qe[:, sl].astype(BF16), TN)
                dbl_h.append(ebl[:, sl] * jnp.sum(st_ref[ci, h] * dst1, axis=0, keepdims=True))
                dkd_h.append(_dot(v[:, sl].astype(BF16), dst1b))
                dv2_h.append(dv_h[h] + _dot(kd[:, sl].astype(BF16), dst1b, NT))
            dkd = jnp.concatenate(dkd_h, axis=1)
            dx_scr[ci, 0:CH, :] = dqe * qe + jnp.where(row == CH - 1, jnp.concatenate(dbl_h, axis=1), 0.0)
            dx_scr[ci, CH:2 * CH, :] = dkd * kd
            dlf = _split_dot(wt_ref[...], dx_scr[ci], 2)
            df = dlf / f - (dkd * ekd + jnp.concatenate(dki_h, axis=1))
            dlb_ref[0:1, :] += jnp.sum(df * (1.0 - sig), axis=0, keepdims=True)
            dp_ref[rs, DH:2 * DH] = (df * omlb * sig * (1.0 - sig)).astype(BF16)
            dp_ref[rs, 2 * DH:3 * DH] = jnp.concatenate(dv2_h, axis=1).astype(BF16)

        @pl.when(i == steps - 1)
        def _():
            gl = dlb_ref[0:1, :] * lb * omlb
            dlb_ref[0:1, :] = gl
            dlb_ref[1:2, :] = -gl
            exchange.finish()

    rev = lambda j: pl.BlockSpec((nch * CH, DH), lambda c: (steps - 1 - c, j))
    anyspec = pl.BlockSpec(memory_space=pl.ANY)
    res = pl.pallas_call(
        body, grid=(steps,),
        in_specs=[rev(2), rev(3), rev(4), rev(5), rev(0),
                  pl.BlockSpec((nch, NH, CH, CH), lambda c: (steps - 1 - c, 0, 0, 0)),
                  pl.BlockSpec((nch, NH, HD, HD), lambda c: (steps - 1 - c, 0, 0, 0)),
                  rev(1), _full((2, DH)), _full((1, HD)),
                  _full(w_st.shape), _full(w_st_t.shape), _full(masks_sym.shape),
                  anyspec] + [anyspec] * nw,
        out_specs=[pl.BlockSpec((nch * CH, 4 * DH), lambda c: (steps - 1 - c, 0)), _full((8, DH)), _full((8, HD))]
        + [anyspec] * nw,
        out_shape=[SDS((T, DIN), BF16), SDS((8, DH), F32), SDS((8, HD), F32)] + _slot_shapes(sums),
        scratch_shapes=[pltpu.VMEM((NH, HD, HD), F32), pltpu.VMEM((nch, (2 + n_lev) * CH, DH), F32)] + _exchange_sems(nw),
        input_output_aliases={13: 0},
        compiler_params=_arb(), name="hgrn_bwd")(proj, proj, proj, proj, o_pre, a_all, st_all, dycat, lower_bounds, gn_w,
                                                 w_st, w_st_t, masks_sym, dproj, *sums)
    return res[:3], res[3:]


def _proj_in_bwd(dproj, x, dx1, nw, sc, w_in_b, tm, sums):
    T = x.shape[0]
    ns = len(sums)
    steps = T // tm

    def body(*refs):
        dp_ref, x_ref, dx1_ref, nw_ref, sc_ref, w_ref = refs[:6]
        gx_ref, acc_ref = refs[6 + ns:8 + ns]
        exchange = _ChipExchange(refs[6:6 + ns], refs[8 + ns:8 + 2 * ns], *refs[8 + 2 * ns:])

        @pl.when(pl.program_id(0) == 0)
        def _():
            exchange.start()
            acc_ref[...] = jnp.zeros_like(acc_ref)

        dh = _dot(dp_ref[:, 0:4 * DH], w_ref[:, 2 * DG:DIN], NT) + _dot(dp_ref[:, 4 * DH:DIN], w_ref[:, 0:2 * DG], NT)
        xv = x_ref[...]
        r = _rms(xv)
        xh = xv * r
        n1 = xh * nw_ref[...]
        acc_ref[0:1, :] += jnp.sum(dh, axis=0, keepdims=True)
        acc_ref[1:2, :] += jnp.sum(dh * n1, axis=0, keepdims=True)
        dn = dh * (1.0 + sc_ref[...])
        acc_ref[2:3, :] += jnp.sum(dn * xh, axis=0, keepdims=True)
        gx_ref[...] = dx1_ref[...] + _rms_bwd(xh, r, dn * nw_ref[...])

        @pl.when(pl.program_id(0) == steps - 1)
        def _():
            exchange.finish()

    row = lambda i: (i, 0)
    anyspec = pl.BlockSpec(memory_space=pl.ANY)
    res = pl.pallas_call(
        body, grid=(steps,),
        in_specs=[pl.BlockSpec((tm, DIN), row), pl.BlockSpec((tm, D), row), pl.BlockSpec((tm, D), row),
                  _full((1, D)), _ada_part(ADA_SC1), _resident((D, DIN))] + [anyspec] * ns,
        out_specs=[pl.BlockSpec((tm, D), row), _full((8, D))] + [anyspec] * ns,
        out_shape=[SDS((T, D), F32), SDS((8, D), F32)] + _slot_shapes(sums),
        scratch_shapes=_exchange_sems(ns),
        compiler_params=_arb(), name="proj_in_bwd")(dproj, x, dx1, nw, sc, w_in_b, *sums)
    return res[:2], res[2:]


def _wgrad(a, b, bk, bn, tt, name):
    T, K = a.shape
    N = b.shape[1]
    nn, nk, nt = N // bn, K // bk, T // tt
    bmap = lambda n, k, t: (t, n)

    def body(a_ref, b_ref, o_ref):
        @pl.when(pl.program_id(2) == 0)
        def _():
            o_ref[...] = jnp.zeros_like(o_ref)

        o_ref[0] += _dot(a_ref[...], b_ref[...], TN)

    return pl.pallas_call(
        body, grid=(nn, nk, nt),
        in_specs=[pl.BlockSpec((tt, bk), lambda n, k, t: (t, k)), pl.BlockSpec((tt, bn), bmap)],
        out_specs=pl.BlockSpec((1, bk, bn), lambda n, k, t: (n, k, 0)),
        out_shape=SDS((nn, K, bn), F32),
        compiler_params=_arb(3), name=name)(a, b)


def _adam_math(w, g, m, v):
    m = B1 * m + (1.0 - B1) * g
    v = B2 * v + (1.0 - B2) * (g * g)
    m_hat = m / (1.0 - B1 ** STEP)
    v_hat = v / (1.0 - B2 ** STEP)
    return -LR * (m_hat / (jnp.sqrt(v_hat) + AEPS) + WD * w), m, v


def _adamw_halves(w, mine, sibling, m, v, c_idx, rb, name):
    R, C = w.shape
    nb = (R // 2) // rb

    def body(c_ref, w_ref, a_ref, b_ref, m_ref, v_ref, g_out, d_out, m_out, v_out):
        g = jnp.where(pl.program_id(0) == c_ref[0], a_ref[...], b_ref[...])
        g_out[...] = g
        d_out[...], m_out[...], v_out[...] = _adam_math(w_ref[...], g, m_ref[...], v_ref[...])

    whole = pl.BlockSpec((rb, C), lambda hh, i, cr: (hh * nb + i, 0))
    half = pl.BlockSpec((rb, C), lambda hh, i, cr: (i, 0))
    return pl.pallas_call(
        body,
        grid_spec=pltpu.PrefetchScalarGridSpec(
            num_scalar_prefetch=1, grid=(2, nb), in_specs=[whole, half, half, whole, whole], out_specs=[whole] * 4),
        out_shape=[SDS((R, C), F32)] * 4, compiler_params=_arb(2), name=name)(c_idx, w, mine, sibling, m, v)


def _ada_forward(c_all, w_ada):
    n = w_ada.shape[1]

    def body(c_ref, w_ref, ca_ref, p_ref):
        cv = c_ref[...]
        ca = cv * _sigmoid(cv)
        ca_ref[...] = ca
        p_ref[...] = _dot(ca, w_ref[...], precision=HIGHEST)

    return pl.pallas_call(
        body, grid=(n // 512,),
        in_specs=[_full((N_DEV, D)), pl.BlockSpec((D, 512), lambda i: (0, i))],
        out_specs=[_full((N_DEV, D)), pl.BlockSpec((N_DEV, 512), lambda i: (0, i))],
        out_shape=[SDS((N_DEV, D), F32), SDS((N_DEV, n), F32)],
        compiler_params=_arb(), name="ada_forward")(c_all, w_ada)


def _ada_wgrad_adam(cact_t, dada_all, w, m, v, chip_idx):
    R, C = w.shape
    rb = 256

    def body(j_ref, c_ref, d_ref, w_ref, m_ref, v_ref, g_out, d_out, m_out, v_out):
        g = _dot(c_ref[...], d_ref[...], precision=HIGHEST)
        g_out[...] = g
        d_out[...], m_out[...], v_out[...] = _adam_math(w_ref[...], g, m_ref[...], v_ref[...])

    spec = pl.BlockSpec((rb, C), lambda i, j: (i, 0))
    return pl.pallas_call(
        body,
        grid_spec=pltpu.PrefetchScalarGridSpec(
            num_scalar_prefetch=1, grid=(R // rb,),
            in_specs=[pl.BlockSpec((rb, N_DEV), lambda i, j: (i, 0)), pl.BlockSpec((N_DEV, C), lambda i, j: (0, j[0])),
                      spec, spec, spec],
            out_specs=[spec] * 4),
        out_shape=[SDS((R, C), F32)] * 4,
        compiler_params=_arb(), name="ada_wgrad_adam")(chip_idx, cact_t, dada_all, w, m, v)


SMALL_NAMES = ('b_ada', 'norm1_w', 'norm2_w', 'final_norm_w', 'v_ln_w', 'v_ln_b', 'lower_bounds', 'gn_w', 'b_s', 'w_s')


def _small_finalize(gathered, params, moms, vels):
    n_in = len(gathered)

    def body(*refs):
        acc1, acc2, dln, dlb, dgn, dbs, dws = refs[:n_in]
        prm = [dict(zip(SMALL_NAMES, refs[n_in + k * 10:n_in + (k + 1) * 10])) for k in range(3)]
        outs = [dict(zip(SMALL_NAMES, refs[n_in + 30 + k * 10:n_in + 30 + (k + 1) * 10])) for k in range(4)]
        loss_ref, dada_ref = refs[n_in + 70:n_in + 72]

        def dev_sum(ref, first, n):
            per = ref.shape[0] // N_DEV
            g = ref[first:first + n, :]
            for dev in range(1, N_DEV):
                g = g + ref[dev * per + first:dev * per + first + n, :]
            return g

        def update(n, g, cols=slice(None)):
            outs[0][n][:, cols] = g
            outs[1][n][:, cols], outs[2][n][:, cols], outs[3][n][:, cols] = _adam_math(
                prm[0][n][:, cols], g, prm[1][n][:, cols], prm[2][n][:, cols])

        ada_rows = ((acc1, 0), (acc1, 1), (acc2, 5), (acc2, 2), (acc2, 1), (acc2, 0))
        for k, (ref, r) in enumerate(ada_rows):
            update('b_ada', dev_sum(ref, r, 1), slice(k * D, (k + 1) * D))
            for dev in range(N_DEV):
                dada_ref[dev:dev + 1, k * D:(k + 1) * D] = ref[8 * dev + r:8 * dev + r + 1, :]
        update('norm1_w', dev_sum(acc1, 2, 1))
        update('norm2_w', dev_sum(acc2, 3, 1))
        update('final_norm_w', dev_sum(acc2, 4, 1))
        update('v_ln_w', dev_sum(dln, 0, 1))
        update('v_ln_b', dev_sum(dln, 1, 1))
        update('lower_bounds', dev_sum(dlb, 0, 2))
        update('gn_w', dev_sum(dgn, 0, 1))
        update('b_s', dev_sum(dbs, 0, NH))
        update('w_s', dev_sum(dws, 0, NH * BLK))
        loss_ref[...] = jnp.sum(dev_sum(acc2, 6, 1), axis=-1, keepdims=True)

    shapes = [SDS(params[n].shape, F32) for n in SMALL_NAMES]
    res = pl.pallas_call(
        body, out_shape=shapes * 4 + [SDS((1, 1), F32), SDS((N_DEV, 6 * D), F32)], name="small_finalize")(
            *gathered, *[d[n] for d in (params, moms, vels) for n in SMALL_NAMES])
    return [dict(zip(SMALL_NAMES, res[k * 10:(k + 1) * 10])) for k in range(4)], res[40], res[41]


def _position():
    x, y, c = lax.axis_index("x"), lax.axis_index("y"), lax.axis_index("c")
    return x, y, c


def _chip_at(x, y, r):
    return (x ^ (r >> 1), y ^ (r & 1))


def _gather_rows(ins, outs, send_sems, recv_sems, local_sems):
    nb = len(ins)
    x, y, c = _position()
    me, sibling = (x, y, c), (x, y, 1 - c)
    chips = [_chip_at(x, y, r) for r in (1, 2, 3)]

    def rows(b, px, py, pc):
        m_per = ins[b].shape[0]
        return outs[b].at[pl.ds((4 * px + 2 * py + pc) * m_per, m_per), :]

    def copy(b, k, blk, to, src=None):
        return pltpu.make_async_remote_copy(
            src_ref=rows(b, *blk) if src is None else src, dst_ref=rows(b, *blk),
            send_sem=send_sems.at[7 * b + k], recv_sem=recv_sems.at[7 * b + k], device_id=to, device_id_type=MESH)

    local, sent = [], []
    for b in range(nb):
        mine = pltpu.make_async_copy(ins[b], rows(b, *me), local_sems.at[b])
        mine.start()
        local.append(mine)
        first = [copy(b, 0, me, sibling, src=ins[b])]
        first += [copy(b, 1 + j, me, (*chip, c), src=ins[b]) for j, chip in enumerate(chips)]
        for cp in first:
            cp.start()
        sent += first
    for b in range(nb):
        for j, chip in enumerate(chips):
            copy(b, 1 + j, (*chip, c), me).wait_recv()
            passed = copy(b, 4 + j, (*chip, c), sibling)
            passed.start()
            sent.append(passed)
    for b in range(nb):
        copy(b, 0, sibling, me).wait_recv()
        for j, chip in enumerate(chips):
            copy(b, 4 + j, (*chip, 1 - c), me).wait_recv()
    for cp in sent:
        cp.wait_send()
    for cp in local:
        cp.wait()


def _gather_rows_shapes(blocks):
    return [SDS((N_DEV * b.shape[0], b.shape[1]), b.dtype) for b in blocks]


def _gather_rows_sems(nb):
    return [pltpu.SemaphoreType.DMA((7 * nb,)), pltpu.SemaphoreType.DMA((7 * nb,)), pltpu.SemaphoreType.DMA((nb,))]


def _all_gather_rows(blocks, name):
    nb = len(blocks)

    def body(*refs):
        _gather_rows(refs[:nb], refs[nb:2 * nb], *refs[2 * nb:])

    vmem = pl.BlockSpec(memory_space=pltpu.VMEM)
    return pl.pallas_call(
        body, out_shape=_gather_rows_shapes(blocks), in_specs=[vmem] * nb, out_specs=[vmem] * nb,
        scratch_shapes=_gather_rows_sems(nb), name=name)(*blocks)


def _place_shard(w_shard, axis, chip_idx, name):
    R, C = w_shard.shape
    rb = _row_block(R)
    nb = R // rb
    full = (R * N_CHIPS, C) if axis == 0 else (R, C * N_CHIPS)
    omap = (lambda i, j: (j[0] * nb + i, 0)) if axis == 0 else (lambda i, j: (i, j[0]))

    def body(j_ref, w_ref, o_ref):
        o_ref[...] = w_ref[...].astype(BF16)

    return pl.pallas_call(
        body,
        grid_spec=pltpu.PrefetchScalarGridSpec(
            num_scalar_prefetch=1, grid=(nb,), in_specs=[pl.BlockSpec((rb, C), lambda i, j: (i, 0))],
            out_specs=pl.BlockSpec((rb, C), omap)),
        out_shape=SDS(full, BF16), compiler_params=_arb(), name=name)(chip_idx, w_shard)


class _WeightGather:
    def __init__(self, refs, axes, send_sems, recv_sems):
        self.refs, self.axes, self.send_sems, self.recv_sems = refs, axes, send_sems, recv_sems
        self.x, self.y, self.c = _position()
        self.j = 2 * self.x + self.y
        self.n = 3 * len(refs)

    def _half(self, w, chip_idx, half):
        ref, axis = self.refs[w], self.axes[w]
        if axis == 0:
            size = ref.shape[0] // N_CHIPS
            return ref.at[pl.ds(chip_idx * size + half * (size // 2), size // 2), :]
        size = ref.shape[1] // N_CHIPS
        rows = ref.shape[0] // 2
        return ref.at[pl.ds(half * rows, rows), pl.ds(chip_idx * size, size)]

    def _ici(self, w, r, chip_idx):
        k = 3 * w + r - 1
        piece = self._half(w, chip_idx, self.c)
        return pltpu.make_async_remote_copy(
            src_ref=piece, dst_ref=piece, send_sem=self.send_sems.at[k], recv_sem=self.recv_sems.at[k],
            device_id=(*_chip_at(self.x, self.y, r), self.c), device_id_type=MESH)

    def _d2d(self, w, r, half):
        k = self.n + 3 * w + r - 1
        piece = self._half(w, self.j ^ r, half)
        return pltpu.make_async_remote_copy(
            src_ref=piece, dst_ref=piece, send_sem=self.send_sems.at[k], recv_sem=self.recv_sems.at[k],
            device_id=(self.x, self.y, 1 - self.c), device_id_type=MESH)

    def _each(self):
        return [(w, r) for w in range(len(self.refs)) for r in (1, 2, 3)]

    def start(self):
        for w, r in self._each():
            self._ici(w, r, self.j).start()

    def forward(self):
        for w, r in self._each():
            self._ici(w, r, self.j ^ r).wait_recv()
            self._d2d(w, r, self.c).start()

    def finish(self):
        for w, r in self._each():
            self._ici(w, r, self.j).wait_send()
            self._d2d(w, r, self.c).wait_send()
            self._d2d(w, r, 1 - self.c).wait_recv()


def _gather_sems(n_weights):
    return [pltpu.SemaphoreType.DMA((6 * n_weights,)), pltpu.SemaphoreType.DMA((6 * n_weights,))]


def _gather_weights(placed, axes, row_blocks, name):
    nw, nb = len(placed), len(row_blocks)

    def body(*refs):
        w_outs, b_ins, b_outs = refs[nw + nb:2 * nw + nb], refs[nw:nw + nb], refs[2 * nw + nb:2 * (nw + nb)]
        sems = refs[2 * (nw + nb):]
        g = _WeightGather(w_outs, axes, *sems[:2])
        g.start()
        _gather_rows(b_ins, b_outs, *sems[2:])
        g.forward()
        g.finish()

    anyspec = pl.BlockSpec(memory_space=pl.ANY)
    vmem = pl.BlockSpec(memory_space=pltpu.VMEM)
    res = pl.pallas_call(
        body, out_shape=[SDS(a.shape, a.dtype) for a in placed] + _gather_rows_shapes(row_blocks),
        in_specs=[anyspec] * nw + [vmem] * nb, out_specs=[anyspec] * nw + [vmem] * nb,
        scratch_shapes=_gather_sems(nw) + _gather_rows_sems(nb), input_output_aliases={i: i for i in range(nw)},
        name=name)(*placed, *row_blocks)
    return res[:nw], res[nw:]


class _ChipExchange:
    def __init__(self, ins, outs, send_sems, recv_sems):
        self.ins, self.outs, self.send_sems, self.recv_sems = ins, outs, send_sems, recv_sems
        self.x, self.y, self.c = _position()
        self.j = 2 * self.x + self.y

    def _copies(self):
        for w in range(len(self.ins)):
            for r in (1, 2, 3):
                k = 3 * w + r - 1
                yield pltpu.make_async_remote_copy(
                    src_ref=self.ins[w].at[self.j ^ r], dst_ref=self.outs[w].at[r - 1],
                    send_sem=self.send_sems.at[k], recv_sem=self.recv_sems.at[k],
                    device_id=(*_chip_at(self.x, self.y, r), self.c), device_id_type=MESH)

    def start(self):
        for cp in self._copies():
            cp.start()

    def finish(self):
        for cp in self._copies():
            cp.wait()


def _exchange_sems(n_weights):
    return [pltpu.SemaphoreType.DMA((3 * n_weights,)), pltpu.SemaphoreType.DMA((3 * n_weights,))]


class _CoreExchange:
    def __init__(self, ins, outs, send_sems, recv_sems):
        self.ins, self.outs, self.send_sems, self.recv_sems = ins, outs, send_sems, recv_sems
        self.x, self.y, self.c = _position()

    def _copies(self):
        for w in range(len(self.ins)):
            yield pltpu.make_async_remote_copy(
                src_ref=self.ins[w].at[:, 1 - self.c], dst_ref=self.outs[w],
                send_sem=self.send_sems.at[w], recv_sem=self.recv_sems.at[w],
                device_id=(self.x, self.y, 1 - self.c), device_id_type=MESH)

    def start(self):
        for cp in self._copies():
            cp.start()

    def finish(self):
        for cp in self._copies():
            cp.wait()


def _core_exchange_shapes(grads):
    return [SDS((g.shape[0], g.shape[2], g.shape[3]), F32) for g in grads]


def _core_exchange_sems(n):
    return [pltpu.SemaphoreType.DMA((n,)), pltpu.SemaphoreType.DMA((n,))]


def _exchange_core_halves(grads, name):
    nw = len(grads)

    def body(*refs):
        ex = _CoreExchange(refs[:nw], refs[nw:2 * nw], *refs[2 * nw:])
        ex.start()
        ex.finish()

    anyspec = pl.BlockSpec(memory_space=pl.ANY)
    return pl.pallas_call(
        body, out_shape=_core_exchange_shapes(grads), in_specs=[anyspec] * nw, out_specs=[anyspec] * nw,
        scratch_shapes=_core_exchange_sems(nw), name=name)(*grads)


def _add_core_halves(g4, recv, c_idx, rb, name):
    ns, _, rh, C = g4.shape

    def body(c_ref, g_ref, r_ref, o_ref):
        o_ref[...] = (g_ref[0] + r_ref[...]).astype(BF16)

    return pl.pallas_call(
        body,
        grid_spec=pltpu.PrefetchScalarGridSpec(
            num_scalar_prefetch=1, grid=(ns, rh // rb),
            in_specs=[pl.BlockSpec((1, 1, rb, C), lambda s, i, cr: (s, cr[0], i, 0)),
                      pl.BlockSpec((1, rb, C), lambda s, i, cr: (s, i, 0))],
            out_specs=pl.BlockSpec((1, rb, C), lambda s, i, cr: (s, i, 0))),
        out_shape=SDS((ns, rh, C), BF16), compiler_params=_arb(2), name=name)(c_idx, g4, recv)


def _slot_shapes(sums):
    return [SDS((3,) + s.shape[1:], s.dtype) for s in sums]


def _add_chips(own, slots, order, rb, name):
    _, rh, C = slots.shape

    def body(o_ref, own_ref, a_ref, b_ref, c_ref, d_ref, out_ref):
        mine = own_ref[0].astype(F32)
        t = [jnp.where(o_ref[i] == 0, mine, r[0].astype(F32)) for i, r in enumerate((a_ref, b_ref, c_ref, d_ref))]
        out_ref[...] = ((t[0] + t[1]) + t[2]) + t[3]

    def spec(i):
        return pl.BlockSpec((1, rb, C), lambda t, o: (jnp.maximum(o[i], 1) - 1, t, 0))

    return pl.pallas_call(
        body,
        grid_spec=pltpu.PrefetchScalarGridSpec(
            num_scalar_prefetch=1, grid=(rh // rb,),
            in_specs=[pl.BlockSpec((1, rb, C), lambda t, o: (o[4], t, 0)), spec(0), spec(1), spec(2), spec(3)],
            out_specs=pl.BlockSpec((rb, C), lambda t, o: (t, 0))),
        out_shape=SDS((rh, C), F32), compiler_params=_arb(), name=name)(order, own, slots, slots, slots, slots)


def _share_halves(halves):
    nw = len(halves)

    def body(*refs):
        ins, outs = refs[:nw], refs[nw:2 * nw]
        send_sems, recv_sems = refs[2 * nw:]
        x, y, c = _position()
        started = []
        for w in range(nw):
            cp = pltpu.make_async_remote_copy(
                src_ref=ins[w], dst_ref=outs[w], send_sem=send_sems.at[w], recv_sem=recv_sems.at[w],
                device_id=(x, y, 1 - c), device_id_type=MESH)
            cp.start()
            started.append(cp)
        for cp in started:
            cp.wait()

    anyspec = pl.BlockSpec(memory_space=pl.ANY)
    return pl.pallas_call(
        body, out_shape=[SDS(h.shape, F32) for h in halves], in_specs=[anyspec] * nw, out_specs=[anyspec] * nw,
        scratch_shapes=[pltpu.SemaphoreType.DMA((nw,)), pltpu.SemaphoreType.DMA((nw,))],
        name="share_halves")(*halves)


def _small_2d(b_ada, norm1_w, norm2_w, final_norm_w, v_ln_w, v_ln_b, lower_bounds, gn_w, b_s, w_s):
    return dict(zip(SMALL_NAMES, (b_ada, norm1_w, norm2_w, final_norm_w.reshape(1, D), v_ln_w, v_ln_b, lower_bounds, gn_w,
                                  b_s.reshape(NH, BLK), w_s.reshape(NH * BLK, BLK))))


def _small_original_shapes(d):
    out = dict(d)
    out['final_norm_w'] = d['final_norm_w'].reshape(D)
    out['b_s'] = d['b_s'].reshape(1, NH, BLK)
    out['w_s'] = d['w_s'].reshape(1, NH, BLK, BLK)
    return out


def _row_block(r):
    for cand in (256, 176, 128, 64, 32, 16, 8):
        if r % cand == 0:
            return cand
    return r


def kernel(x, c, w_ada, b_ada, norm1_w, w_in, w_s, b_s, v_ln_w, v_ln_b, lower_bounds, gn_w, w_out, norm2_w, w_ffn_in, w_ffn_out, final_norm_w, loss_target, m_w_ada, m_b_ada, m_norm1_w, m_w_in, m_w_s, m_b_s, m_v_ln_w, m_v_ln_b, m_lower_bounds, m_gn_w, m_w_out, m_norm2_w, m_w_ffn_in, m_w_ffn_out, m_final_norm_w, v_w_ada, v_b_ada, v_norm1_w, v_w_in, v_w_s, v_b_s, v_v_ln_w, v_v_ln_b, v_lower_bounds, v_gn_w, v_w_out, v_norm2_w, v_w_ffn_in, v_w_ffn_out, v_final_norm_w):
    T = x.shape[1]
    tm, tp = min(TOKEN_TILE, T), min(PROJ_TILE, T)
    px, py, pc = _position()
    chip = 2 * px + py
    me = 4 * px + 2 * py + pc
    x2d = x.reshape(T, D)
    tgt = loss_target.reshape(T, D)

    chip_idx = jnp.reshape(chip, (1,)).astype(jnp.int32)
    c_idx = jnp.reshape(pc, (1,)).astype(jnp.int32)
    (w_in_b,), (c_all,) = _gather_weights(
        [_place_shard(w_in[0], 1, chip_idx, "place_in")], [1], [jnp.broadcast_to(c, (8, D))], "gather_w_in_and_c")
    placed = [_place_shard(w_out[0], 0, chip_idx, "place_out"), _place_shard(w_ffn_in[0], 1, chip_idx, "place_ffn_in"),
              _place_shard(w_ffn_out[0], 0, chip_idx, "place_ffn_out")]

    cact, ada_part = _ada_forward(c_all.reshape(N_DEV, 8, D)[:, 0, :], w_ada[0])
    n_ada = ada_part.shape[1]
    (ada_all,) = _all_gather_rows([ada_part], "gather_ada")
    ada_all = ada_all.reshape(N_CHIPS, 2, N_DEV, n_ada)[:, 0]
    ada = lax.dynamic_index_in_dim(ada_all, me, axis=1, keepdims=False).reshape(1, 6 * D) + b_ada

    rr = lax.broadcasted_iota(jnp.int32, (BLK, BLK), 0) // CH
    cc = lax.broadcasted_iota(jnp.int32, (BLK, BLK), 1) // CH
    ws_b = jnp.where((rr >= cc)[None], w_s[0], 0.0).astype(BF16)
    bst = b_s[0].T
    lnw, lnb = v_ln_w, v_ln_b
    nw1, nw2, fw = norm1_w, norm2_w, final_norm_w.reshape(1, D)

    h1, proj = _proj_in(x2d, nw1, ada, ada, w_in_b, tp)
    ycat = _gmlp_fwd(proj, ws_b, bst, lnw, lnb)
    tables = _hgrn_tables()
    (ycat, o_pre, a_all, st_all), (w_out_b, w_fi_b, w_fo_b) = _hgrn_fwd(
        proj, lower_bounds, gn_w, ycat, tables, placed, [0, 1, 0])

    dycat, dx1, h2, act, dff, dgu, dmix, acc2 = _token_local(
        x2d, ycat, tgt, ada, nw2, ada, ada, ada, fw, w_out_b, w_fi_b, w_fo_b, tm)

    tt = min(WGRAD_TOKENS, T)
    order = jnp.concatenate([chip ^ jnp.arange(N_CHIPS, dtype=jnp.int32), chip_idx]).astype(jnp.int32)

    def by_core_half(g):
        return g.reshape(N_CHIPS, 2, g.shape[1] // 2, g.shape[2])

    def core_sums(g4, recv, names):
        return [_add_core_halves(a, b, c_idx, _row_block(a.shape[2]), "add_core_" + n) for a, b, n in zip(g4, recv, names)]

    def chip_sums(sums, slots, names):
        return [_add_chips(o, s, order, _row_block(s.shape[1]), "add_chips_" + n) for o, s, n in zip(sums, slots, names)]

    g_out = _wgrad(ycat, dmix, D, D, tt, "wgrad_out").reshape(N_CHIPS, D // N_CHIPS, D)
    g_fi = _wgrad(h2, dgu, D, FFB, tt, "wgrad_ffn_in")
    g_fo = _wgrad(act, dff, FFB, D, tt, "wgrad_ffn_out").reshape(N_CHIPS, DFF // N_CHIPS, D)
    late_names = ["out", "ffn_in", "ffn_out"]
    late_g4 = [by_core_half(g) for g in (g_out, g_fi, g_fo)]

    (dproj, dws, dbs, dln), late_recv = _gmlp_bwd(proj, dycat, ws_b, bst, lnw, lnb, late_g4)
    late_sums = core_sums(late_g4, late_recv, late_names)
    (dproj, dlb, dgn), late_slots = _hgrn_bwd(
        proj, o_pre, a_all, st_all, dycat, lower_bounds, gn_w, dproj, tables, late_sums)

    g_in = _wgrad(h1, dproj, D, D, tt, "wgrad_in")
    g_in = jnp.concatenate([g_in[2], g_in[0], g_in[1]], axis=1).reshape(D, N_CHIPS, DIN // N_CHIPS).transpose(1, 0, 2)
    in_g4 = [by_core_half(g_in)]
    in_sums = core_sums(in_g4, _exchange_core_halves(in_g4, "exchange_core_halves_in"), ["in"])
    (grad_x, acc1), in_slots = _proj_in_bwd(dproj, x2d, dx1, nw1, ada, w_in_b, tp, in_sums)
    names = ["in"] + late_names
    halves = chip_sums(in_sums, in_slots, ["in"]) + chip_sums(late_sums, late_slots, late_names)
    sibling_halves = _share_halves(halves)

    big_w = [(w_in, m_w_in, v_w_in), (w_out, m_w_out, v_w_out), (w_ffn_in, m_w_ffn_in, v_w_ffn_in),
             (w_ffn_out, m_w_ffn_out, v_w_ffn_out)]
    big_out = []
    for mine, sib, (w, m, v), n in zip(halves, sibling_halves, big_w, names):
        res = _adamw_halves(w[0], mine, sib, m[0], v[0], c_idx, _row_block(mine.shape[0]), "adamw_" + n)
        big_out.append([r[None] for r in res])

    gathered = _all_gather_rows([acc1, acc2, dln, dlb, dgn, dbs, dws], "gather_small")
    small, loss, dada_all = _small_finalize(
        gathered,
        _small_2d(b_ada, norm1_w, norm2_w, final_norm_w, v_ln_w, v_ln_b, lower_bounds, gn_w, b_s, w_s),
        _small_2d(m_b_ada, m_norm1_w, m_norm2_w, m_final_norm_w, m_v_ln_w, m_v_ln_b, m_lower_bounds, m_gn_w, m_b_s, m_w_s),
        _small_2d(v_b_ada, v_norm1_w, v_norm2_w, v_final_norm_w, v_v_ln_w, v_v_ln_b, v_lower_bounds, v_gn_w, v_b_s, v_w_s))
    small = [_small_original_shapes(d) for d in small]
    loss = loss.reshape(())

    ada_out = [o[None] for o in _ada_wgrad_adam(cact.T, dada_all, w_ada[0], m_w_ada[0], v_w_ada[0], chip_idx)]

    order_names = ['w_ada', 'b_ada', 'norm1_w', 'w_in', 'w_s', 'b_s', 'v_ln_w', 'v_ln_b', 'lower_bounds', 'gn_w',
                   'w_out', 'norm2_w', 'w_ffn_in', 'w_ffn_out', 'final_norm_w']
    big_idx = {'w_in': 0, 'w_out': 1, 'w_ffn_in': 2, 'w_ffn_out': 3}
    outs = [loss, grad_x.reshape(1, T, D)]
    for kind in range(4):
        for n in order_names:
            if n == 'w_ada':
                outs.append(ada_out[kind])
            elif n in big_idx:
                outs.append(big_out[big_idx[n]][kind])
            else:
                outs.append(small[kind][n])
    return tuple(outs)
```

```python
import functools

import jax
import jax.numpy as jnp
import numpy as np
from jax import lax
from jax.experimental import pallas as pl
from jax.experimental.pallas import tpu as pltpu

F32 = jnp.float32
BF16 = jnp.bfloat16
SDS = jax.ShapeDtypeStruct
MESH = pl.DeviceIdType.MESH
HIGHEST = lax.Precision.HIGHEST

D = 1024
DG = 512
DH = 512
NH = 4
HD = 128
BLK = 128
CH = 64
DFF = 2816
DIN = 3072
FFB = 1408
LEVELS = (64, 32, 16, 8, 4, 2)
HGRN_CHUNKS_PER_STEP = 8
GMLP_ROWS_PER_STEP = 1024
TOKEN_TILE = 256
PROJ_TILE = 512
WGRAD_TOKENS = 2048
N_CHIPS = 4
N_DEV = 8
EPS = 1e-6
LR, B1, B2, AEPS, WD, STEP = 0.001, 0.9, 0.999, 1e-08, 0.01, 10

NT = (((1,), (1,)), ((), ()))
TN = (((0,), (0,)), ((), ()))


def _full(shape):
    nd = len(shape)
    return pl.BlockSpec(shape, lambda *_: (0,) * nd)


ADA_SH1, ADA_SC1, ADA_G1, ADA_SH2, ADA_SC2, ADA_G2 = range(6)


def _ada_part(k):
    return pl.BlockSpec((1, D), lambda *_: (0, k))


def _resident(shape):
    nd = len(shape)
    return pl.BlockSpec(shape, lambda *_: (0,) * nd, pipeline_mode=pl.Buffered(1))


def _arb(n=1):
    return pltpu.CompilerParams(dimension_semantics=("arbitrary",) * n)


def _dot(a, b, dims=None, precision=None):
    if dims is None:
        return jnp.dot(a, b, preferred_element_type=F32, precision=precision)
    return lax.dot_general(a, b, dims, preferred_element_type=F32, precision=precision)


def _sigmoid(x):
    return jax.nn.sigmoid(x)


def _gelu_parts(x):
    cdf = 0.5 * (1.0 + lax.erf(x * 0.7071067811865476))
    pdf = jnp.exp(-0.5 * x * x) * 0.3989422804014327
    return x * cdf, cdf + x * pdf


def _rms(x):
    return lax.rsqrt(jnp.mean(x * x, axis=-1, keepdims=True) + EPS)


def _rms_bwd(xhat, r, gw):
    return r * (gw - xhat * jnp.mean(xhat * gw, axis=-1, keepdims=True))


def _lower_bound(lbp_ref):
    l0, l1 = lbp_ref[0:1, :], lbp_ref[1:2, :]
    m = jnp.maximum(l0, l1)
    e0, e1 = jnp.exp(l0 - m), jnp.exp(l1 - m)
    return e0 / (e0 + e1), e1 / (e0 + e1)


def _proj_in(x, nw, sc, sh, w_in_b, tm):
    T = x.shape[0]

    def body(x_ref, nw_ref, sc_ref, sh_ref, w_ref, h_ref, p_ref):
        xv = x_ref[...]
        h = ((xv * _rms(xv)) * nw_ref[...]) * (1.0 + sc_ref[...]) + sh_ref[...]
        hb = h.astype(BF16)
        h_ref[...] = hb
        p_ref[...] = _dot(hb, w_ref[...])

    row = lambda i: (i, 0)
    return pl.pallas_call(
        body, grid=(T // tm,),
        in_specs=[pl.BlockSpec((tm, D), row), _full((1, D)), _ada_part(ADA_SC1), _ada_part(ADA_SH1), _resident((D, DIN))],
        out_specs=[pl.BlockSpec((tm, D), row), pl.BlockSpec((tm, DIN), row)],
        out_shape=[SDS((T, D), BF16), SDS((T, DIN), F32)],
        compiler_params=_arb(), name="proj_in")(x, nw, sc, sh, w_in_b)


def _gmlp_common(u, v, lnw, lnb, ws_ref, bst_ref):
    ug, dug = _gelu_parts(u)
    vg, dvg = _gelu_parts(v)
    mu = jnp.mean(vg, axis=-1, keepdims=True)
    vc = vg - mu
    rstd = lax.rsqrt(jnp.mean(vc * vc, axis=-1, keepdims=True) + EPS)
    vhat = vc * rstd
    vn = vhat * lnw + lnb
    vnb = vn.astype(BF16)
    mixed = []
    for h in range(NH):
        sl = slice(h * HD, (h + 1) * HD)
        mixed.append(_dot(ws_ref[h], vnb[:, sl]) + bst_ref[:, h:h + 1])
    return ug, dug, dvg, rstd, vhat, vnb, jnp.concatenate(mixed, axis=1)


def _gmlp_fwd(proj, ws_b, bst, lnw, lnb):
    T = proj.shape[0]
    rows = min(GMLP_ROWS_PER_STEP, T)

    def body(u_ref, v_ref, ws_ref, bst_ref, lnw_ref, lnb_ref, y_ref):
        for bi in range(rows // BLK):
            rs = slice(bi * BLK, (bi + 1) * BLK)
            ug, _, _, _, _, _, mixed = _gmlp_common(u_ref[rs, :], v_ref[rs, :], lnw_ref[...], lnb_ref[...], ws_ref, bst_ref)
            y_ref[rs, :] = (ug * mixed).astype(BF16)

    return pl.pallas_call(
        body, grid=(T // rows,),
        in_specs=[pl.BlockSpec((rows, DG), lambda i: (i, 0)), pl.BlockSpec((rows, DG), lambda i: (i, 1)),
                  _full((NH, BLK, BLK)), _full((BLK, NH)), _full((1, DG)), _full((1, DG))],
        out_specs=pl.BlockSpec((rows, DG), lambda i: (i, 0)),
        out_shape=SDS((T, D), BF16),
        compiler_params=_arb(), name="gmlp_fwd")(proj, proj, ws_b, bst, lnw, lnb)


def _hgrn_tables():
    t = np.arange(CH)[:, None]
    j = np.arange(CH)[None, :]
    blocks = [j <= t, j > t]
    masks = []
    for n in LEVELS:
        mid = t - t % n + n // 2
        blocks.append(np.where(t >= mid, (j >= mid) & (j <= t), (j > t) & (j < mid)))
        masks.append((t // n == j // n) & (t % n >= n // 2) & (j % n < n // 2))
    w = np.concatenate(blocks, axis=0).astype(np.float32)
    m = np.stack(masks).astype(np.float32)
    return (jnp.asarray(w, BF16), jnp.asarray(w.T, BF16), jnp.asarray(m), jnp.asarray(m + m.transpose(0, 2, 1)))


def _split_dot(w, x, parts):
    acc = None
    for _ in range(parts):
        piece = x.astype(BF16)
        term = _dot(w, piece)
        acc = term if acc is None else acc + term
        x = x - piece.astype(F32)
    return acc


def _hgrn_exponents(lf, w_ref):
    b = _split_dot(w_ref[0:CH, :], lf, 3)
    row = lax.broadcasted_iota(jnp.int32, (CH, 1), 0)
    blocks = [b, b[CH - 1:CH, :] - b]
    for n in LEVELS:
        up = (row & (n // 2)) != 0
        if n >= 8:
            ref = b.reshape(CH // n, n, DH)[:, n // 2 - 1:n // 2, :]
            ref = jnp.broadcast_to(ref, (CH // n, n, DH)).reshape(CH, DH)
            blocks.append(jnp.where(up, b - ref, ref - b))
        elif n == 4:
            r4 = row & 3
            two = jnp.where(r4 == 3, pltpu.roll(lf, 1, 0) + lf, 0.0)
            blocks.append(jnp.where(r4 == 0, pltpu.roll(lf, CH - 1, 0), jnp.where(r4 == 2, lf, two)))
        else:
            blocks.append(jnp.where(up, lf, 0.0))
    return blocks


def _hgrn_gates(q, fl, lb, omlb, w_ref):
    sq = _sigmoid(q)
    qf = q * sq
    sig = _sigmoid(fl)
    f = lb + omlb * sig
    k = 1.0 - f
    e = [jnp.exp(x) for x in _hgrn_exponents(jnp.log(f), w_ref)]
    return sq, qf, sig, f, k, e


def _level_factor(e, li, sl, row, qh, kh):
    el = e[2 + li][:, sl]
    up = (row & (LEVELS[li] // 2)) != 0
    return el, up, el * jnp.where(up, qh, kh)


def _hgrn_fwd(proj, lower_bounds, gn_w, ycat, tables, placed, axes):
    T = proj.shape[0]
    nc = T // CH
    nch = min(HGRN_CHUNKS_PER_STEP, nc)
    steps = nc // nch
    w_st, _, masks, _ = tables
    nw = len(placed)
    pass_step = (5 * steps) // 8

    def body(*refs):
        q_ref, f_ref, i_ref, g_ref, lbp_ref, gn_ref, w_ref, m_ref = refs[:8]
        y_ref, o_ref, a_ref, st_ref = refs[9 + nw:13 + nw]
        s_scr, send_sems, recv_sems = refs[13 + 2 * nw:]
        gather = _WeightGather(refs[13 + nw:13 + 2 * nw], axes, send_sems, recv_sems)
        step = pl.program_id(0)

        @pl.when(step == 0)
        def _():
            gather.start()
            s_scr[...] = jnp.zeros_like(s_scr)

        @pl.when(step == pass_step)
        def _():
            gather.forward()

        lb, omlb = _lower_bound(lbp_ref)
        row = lax.broadcasted_iota(jnp.int32, (CH, 1), 0)
        eye = lax.broadcasted_iota(jnp.int32, (CH, CH), 0) == lax.broadcasted_iota(jnp.int32, (CH, CH), 1)
        pre = []
        for ci in range(nch):
            rs = slice(ci * CH, (ci + 1) * CH)
            _, qf, _, _, k, e = _hgrn_gates(q_ref[rs, :], f_ref[rs, :], lb, omlb, w_ref)
            mats = []
            for h in range(NH):
                sl = slice(h * HD, (h + 1) * HD)
                qh, kh = qf[:, sl], k[:, sl]
                a = jnp.where(eye, jnp.sum(qh * kh, axis=-1, keepdims=True), 0.0)
                for li in range(len(LEVELS)):
                    _, _, y = _level_factor(e, li, sl, row, qh, kh)
                    yb = y.astype(BF16)
                    a = a + m_ref[li] * _dot(yb, yb, NT)
                a_ref[ci, h] = a
                mats.append(a.astype(BF16))
            eb = e[0]
            pre.append(((qf * eb).astype(BF16), eb[CH - 1:CH, :], (k * e[1]).astype(BF16), mats))
        for ci in range(nch):
            rs = slice(ci * CH, (ci + 1) * CH)
            qe, ebl, kd, mats = pre[ci]
            v = i_ref[rs, :]
            g = g_ref[rs, :]
            for h in range(NH):
                sl = slice(h * HD, (h + 1) * HD)
                st0 = s_scr[h]
                st_ref[ci, h] = st0
                vb = v[:, sl].astype(BF16)
                o = _dot(qe[:, sl], st0.astype(BF16), NT) + _dot(mats[h], vb)
                s_scr[h] = st0 * ebl[:, sl] + _dot(vb, kd[:, sl], TN)
                o_ref[rs, sl] = o
                gh = g[:, sl]
                y_ref[rs, sl] = (((o * _rms(o)) * gn_ref[...]) * (gh * _sigmoid(gh))).astype(BF16)

        @pl.when(step == steps - 1)
        def _():
            gather.finish()

    blk = lambda j: pl.BlockSpec((nch * CH, DH), lambda c: (c, j))
    anyspec = pl.BlockSpec(memory_space=pl.ANY)
    res = pl.pallas_call(
        body, grid=(steps,),
        in_specs=[blk(2), blk(3), blk(4), blk(5), _full((2, DH)), _full((1, HD)),
                  _full(w_st.shape), _full(masks.shape), anyspec] + [anyspec] * nw,
        out_specs=[pl.BlockSpec((nch * CH, DH), lambda c: (c, 1)),
                   pl.BlockSpec((nch * CH, DH), lambda c: (c, 0)),
                   pl.BlockSpec((nch, NH, CH, CH), lambda c: (c, 0, 0, 0)),
                   pl.BlockSpec((nch, NH, HD, HD), lambda c: (c, 0, 0, 0))] + [anyspec] * nw,
        out_shape=[SDS((T, D), BF16), SDS((T, DH), F32), SDS((nc, NH, CH, CH), F32), SDS((nc, NH, HD, HD), F32)]
        + [SDS(a.shape, a.dtype) for a in placed],
        scratch_shapes=[pltpu.VMEM((NH, HD, HD), F32)] + _gather_sems(nw),
        input_output_aliases={8: 0, **{9 + i: 4 + i for i in range(nw)}},
        compiler_params=_arb(), name="hgrn_fwd")(proj, proj, proj, proj, lower_bounds, gn_w, w_st, masks, ycat, *placed)
    return res[:4], res[4:]


def _token_local(x, ycat, tgt, g1, nw2, sc2, sh2, g2, fw, w_out_b, w_fi_b, w_fo_b, tm):
    T = x.shape[0]
    inv_d = 1.0 / D

    def body(x_ref, y_ref, t_ref, g1_ref, nw2_ref, sc2_ref, sh2_ref, g2_ref, fw_ref, wo_ref, wfi_ref, wfo_ref,
             dy_ref, dx1_ref, h2_ref, act_ref, dff_ref, dgu_ref, dmix_ref, acc_ref):
        @pl.when(pl.program_id(0) == 0)
        def _():
            acc_ref[...] = jnp.zeros_like(acc_ref)

        def acc(row, val):
            acc_ref[row:row + 1, :] += jnp.sum(val, axis=0, keepdims=True)

        g1v, g2v = g1_ref[...], g2_ref[...]
        mix = _dot(y_ref[...], wo_ref[...])
        x1 = x_ref[...] + g1v * mix
        r2 = _rms(x1)
        xh2 = x1 * r2
        n2 = xh2 * nw2_ref[...]
        osc2 = 1.0 + sc2_ref[...]
        h2b = (n2 * osc2 + sh2_ref[...]).astype(BF16)
        h2_ref[...] = h2b
        ff = jnp.zeros((tm, D), F32)
        saved = []
        for kb in range(DFF // FFB):
            gate = _dot(h2b, wfi_ref[:, kb * FFB:(kb + 1) * FFB])
            up = _dot(h2b, wfi_ref[:, DFF + kb * FFB:DFF + (kb + 1) * FFB])
            sg = _sigmoid(gate)
            actb = (gate * sg * up).astype(BF16)
            act_ref[:, kb * FFB:(kb + 1) * FFB] = actb
            ff = ff + _dot(actb, wfo_ref[kb * FFB:(kb + 1) * FFB, :])
            saved.append((gate, up, sg))
        x2 = x1 + g2v * ff
        r3 = _rms(x2)
        xh3 = x2 * r3
        err = xh3 * fw_ref[...] - t_ref[...]
        acc(6, (0.5 * inv_d) * err * err)
        dy = err * inv_d
        acc(4, dy * xh3)
        dx2 = _rms_bwd(xh3, r3, dy * fw_ref[...])
        acc(0, dx2 * ff)
        dffb = (dx2 * g2v).astype(BF16)
        dff_ref[...] = dffb
        dh2 = jnp.zeros((tm, D), F32)
        for kb in range(DFF // FFB):
            gate, up, sg = saved[kb]
            da = _dot(dffb, wfo_ref[kb * FFB:(kb + 1) * FFB, :], NT)
            dgate = (da * up * (sg * (1.0 + gate * (1.0 - sg)))).astype(BF16)
            dup = (da * gate * sg).astype(BF16)
            dgu_ref[:, kb * FFB:(kb + 1) * FFB] = dgate
            dgu_ref[:, DFF + kb * FFB:DFF + (kb + 1) * FFB] = dup
            dh2 = dh2 + _dot(dgate, wfi_ref[:, kb * FFB:(kb + 1) * FFB], NT)
            dh2 = dh2 + _dot(dup, wfi_ref[:, DFF + kb * FFB:DFF + (kb + 1) * FFB], NT)
        acc(2, dh2)
        acc(1, dh2 * n2)
        dn2 = dh2 * osc2
        acc(3, dn2 * xh2)
        dx1 = dx2 + _rms_bwd(xh2, r2, dn2 * nw2_ref[...])
        acc(5, dx1 * mix)
        dmixb = (dx1 * g1v).astype(BF16)
        dmix_ref[...] = dmixb
        dy_ref[...] = _dot(dmixb, wo_ref[...], NT)
        dx1_ref[...] = dx1

    row = lambda i: (i, 0)
    vec = _full((1, D))
    return pl.pallas_call(
        body, grid=(T // tm,),
        in_specs=[pl.BlockSpec((tm, D), row), pl.BlockSpec((tm, D), row), pl.BlockSpec((tm, D), row),
                  _ada_part(ADA_G1), vec, _ada_part(ADA_SC2), _ada_part(ADA_SH2), _ada_part(ADA_G2), vec,
                  _resident((D, D)), _resident((D, 2 * DFF)), _resident((DFF, D))],
        out_specs=[pl.BlockSpec((tm, D), row), pl.BlockSpec((tm, D), row), pl.BlockSpec((tm, D), row),
                   pl.BlockSpec((tm, DFF), row), pl.BlockSpec((tm, D), row), pl.BlockSpec((tm, 2 * DFF), row),
                   pl.BlockSpec((tm, D), row), _full((8, D))],
        out_shape=[SDS((T, D), F32), SDS((T, D), F32), SDS((T, D), BF16), SDS((T, DFF), BF16), SDS((T, D), BF16),
                   SDS((T, 2 * DFF), BF16), SDS((T, D), BF16), SDS((8, D), F32)],
        compiler_params=_arb(), name="token_local")(x, ycat, tgt, g1, nw2, sc2, sh2, g2, fw, w_out_b, w_fi_b, w_fo_b)


def _gmlp_bwd(proj, dycat, ws_b, bst, lnw, lnb, grads):
    T = proj.shape[0]
    rows = min(GMLP_ROWS_PER_STEP, T)
    nb = T // rows
    nw = len(grads)

    def body(*refs):
        u_ref, v_ref, dy_ref, ws_ref, bst_ref, lnw_ref, lnb_ref = refs[:7]
        dp_ref, dws_ref, dbs_ref, dln_ref = refs[7 + nw:11 + nw]
        dbs_acc, send_sems, recv_sems = refs[11 + 2 * nw:]
        exchange = _CoreExchange(refs[7:7 + nw], refs[11 + nw:11 + 2 * nw], send_sems, recv_sems)
        i = pl.program_id(0)

        @pl.when(i == 0)
        def _():
            exchange.start()
            dws_ref[...] = jnp.zeros_like(dws_ref)
            dln_ref[...] = jnp.zeros_like(dln_ref)
            dbs_acc[...] = jnp.zeros_like(dbs_acc)

        r = lax.broadcasted_iota(jnp.int32, (BLK, BLK), 0) // CH
        c = lax.broadcasted_iota(jnp.int32, (BLK, BLK), 1) // CH
        for bi in range(rows // BLK):
            rs = slice(bi * BLK, (bi + 1) * BLK)
            ug, dug, dvg, rstd, vhat, vnb, mixed = _gmlp_common(
                u_ref[rs, :], v_ref[rs, :], lnw_ref[...], lnb_ref[...], ws_ref, bst_ref)
            dya = dy_ref[rs, :]
            dp_ref[rs, 0:DG] = (dya * mixed * dug).astype(BF16)
            dmixed = dya * ug
            dbs_acc[...] += dmixed
            dmb = dmixed.astype(BF16)
            dvn = []
            for h in range(NH):
                sl = slice(h * HD, (h + 1) * HD)
                dws_ref[h * BLK:(h + 1) * BLK, :] += jnp.where(r >= c, _dot(dmb[:, sl], vnb[:, sl], NT), 0.0)
                dvn.append(_dot(ws_ref[h], dmb[:, sl], TN))
            dvn = jnp.concatenate(dvn, axis=1)
            dln_ref[0:1, :] += jnp.sum(dvn * vhat, axis=0, keepdims=True)
            dln_ref[1:2, :] += jnp.sum(dvn, axis=0, keepdims=True)
            dvh = dvn * lnw_ref[...]
            dvgel = rstd * (dvh - jnp.mean(dvh, axis=-1, keepdims=True) - vhat * jnp.mean(dvh * vhat, axis=-1, keepdims=True))
            dp_ref[rs, DG:2 * DG] = (dvgel * dvg).astype(BF16)

        @pl.when(i == nb - 1)
        def _():
            head = lax.broadcasted_iota(jnp.int32, (8, BLK), 0)
            ones = jnp.ones((8, HD), F32)
            out = jnp.zeros((8, BLK), F32)
            for h in range(NH):
                sums = _dot(ones, dbs_acc[:, h * HD:(h + 1) * HD], NT, precision=HIGHEST)
                out = out + jnp.where(head == h, sums, 0.0)
            dbs_ref[...] = out
            exchange.finish()

    anyspec = pl.BlockSpec(memory_space=pl.ANY)
    res = pl.pallas_call(
        body, grid=(nb,),
        in_specs=[pl.BlockSpec((rows, DG), lambda i: (i, 0)), pl.BlockSpec((rows, DG), lambda i: (i, 1)),
                  pl.BlockSpec((rows, DG), lambda i: (i, 0)),
                  _full((NH, BLK, BLK)), _full((BLK, NH)), _full((1, DG)), _full((1, DG))] + [anyspec] * nw,
        out_specs=[pl.BlockSpec((rows, 2 * DG), lambda i: (i, 2)), _full((NH * BLK, BLK)), _full((8, BLK)), _full((8, DG))]
        + [anyspec] * nw,
        out_shape=[SDS((T, DIN), BF16), SDS((NH * BLK, BLK), F32), SDS((8, BLK), F32), SDS((8, DG), F32)]
        + _core_exchange_shapes(grads),
        scratch_shapes=[pltpu.VMEM((BLK, DG), F32)] + _core_exchange_sems(nw),
        compiler_params=_arb(), name="gmlp_bwd")(proj, proj, dycat, ws_b, bst, lnw, lnb, *grads)
    return res[:4], res[4:]


def _hgrn_bwd(proj, o_pre, a_all, st_all, dycat, lower_bounds, gn_w, dproj, tables, sums):
    T = proj.shape[0]
    nc = T // CH
    nch = min(HGRN_CHUNKS_PER_STEP, nc)
    steps = nc // nch
    w_st, w_st_t, _, masks_sym = tables
    n_lev = len(LEVELS)
    nw = len(sums)

    def body(*refs):
        q_ref, f_ref, i_ref, g_ref, o_ref, a_ref, st_ref, dy_ref, lbp_ref, gn_ref, w_ref, wt_ref, ms_ref = refs[:13]
        dp_ref, dlb_ref, dgn_ref = refs[14 + nw:17 + nw]
        ds_scr, dx_scr, send_sems, recv_sems = refs[17 + 2 * nw:]
        exchange = _ChipExchange(refs[14:14 + nw], refs[17 + nw:17 + 2 * nw], send_sems, recv_sems)
        i = pl.program_id(0)

        @pl.when(i == 0)
        def _():
            exchange.start()
            ds_scr[...] = jnp.zeros_like(ds_scr)
            dlb_ref[...] = jnp.zeros_like(dlb_ref)
            dgn_ref[...] = jnp.zeros_like(dgn_ref)

        lb, omlb = _lower_bound(lbp_ref)
        row = lax.broadcasted_iota(jnp.int32, (CH, 1), 0)
        eye = lax.broadcasted_iota(jnp.int32, (CH, CH), 0) == lax.broadcasted_iota(jnp.int32, (CH, CH), 1)
        lower = lax.broadcasted_iota(jnp.int32, (CH, CH), 0) > lax.broadcasted_iota(jnp.int32, (CH, CH), 1)
        dgn = jnp.zeros((1, HD), F32)
        pre = []
        for ci in range(nch):
            rs = slice(ci * CH, (ci + 1) * CH)
            q = q_ref[rs, :]
            v = i_ref[rs, :]
            g = g_ref[rs, :]
            sq, qf, sig, f, k, e = _hgrn_gates(q, f_ref[rs, :], lb, omlb, w_ref)
            eb = e[0]
            ekd = e[1]
            kd = k * ekd
            qe = qf * eb
            dob_h, dqe_h, dqf_h, dki_h, dv_h, dg_h = [], [], [], [], [], []
            for h in range(NH):
                sl = slice(h * HD, (h + 1) * HD)
                o = o_ref[rs, sl]
                ro = _rms(o)
                oh = o * ro
                gh = g[:, sl]
                sg = _sigmoid(gh)
                dyb = dy_ref[rs, sl]
                dg_h.append(dyb * (oh * gn_ref[...]) * (sg * (1.0 + gh * (1.0 - sg))))
                don = dyb * (gh * sg)
                dgn = dgn + jnp.sum(don * oh, axis=0, keepdims=True)
                dob = _rms_bwd(oh, ro, don * gn_ref[...]).astype(BF16)
                vb = v[:, sl].astype(BF16)
                qh, kh = qf[:, sl], k[:, sl]
                dqe = _dot(dob, st_ref[ci, h].astype(BF16))
                da = _dot(dob, vb, NT)
                ddiag = jnp.sum(jnp.where(eye, da, 0.0), axis=-1, keepdims=True)
                dsym = jnp.where(lower, da, _dot(vb, dob, NT))
                upper_part = jnp.zeros((CH, HD), F32)
                both = jnp.zeros((CH, HD), F32)
                for li in range(n_lev):
                    el, up, y = _level_factor(e, li, sl, row, qh, kh)
                    dyv = _dot((ms_ref[li] * dsym).astype(BF16), y.astype(BF16))
                    dx_scr[ci, (2 + li) * CH:(3 + li) * CH, sl] = dyv * y
                    dye = dyv * el
                    upper_part = upper_part + jnp.where(up, dye, 0.0)
                    both = both + dye
                dob_h.append(dob)
                dqe_h.append(dqe)
                dqf_h.append(dqe * eb[:, sl] + ddiag * kh + upper_part)
                dki_h.append(ddiag * qh + (both - upper_part))
                dv_h.append(_dot(a_ref[ci, h].astype(BF16), dob, TN))
            dp_ref[rs, 0:DH] = (jnp.concatenate(dqf_h, axis=1) * (sq * (1.0 + q * (1.0 - sq)))).astype(BF16)
            dp_ref[rs, 3 * DH:4 * DH] = jnp.concatenate(dg_h, axis=1).astype(BF16)
            pre.append((v, sig, f, eb, ekd, kd, qe, dob_h, jnp.concatenate(dqe_h, axis=1), dki_h, dv_h))
        dgn_ref[0:1, :] += dgn
        for ci in reversed(range(nch)):
            rs = slice(ci * CH, (ci + 1) * CH)
            v, sig, f, eb, ekd, kd, qe, dob_h, dqe, dki_h, dv_h = pre[ci]
            ebl = eb[CH - 1:CH, :]
            dbl_h, dkd_h, dv2_h = [], [], []
            for h in range(NH):
                sl = slice(h * HD, (h + 1) * HD)
                dst1 = ds_scr[h]
                dst1b = dst1.astype(BF16)
                ds_scr[h] = dst1 * ebl[:, sl] + _dot(dob_h[h], qe[:, sl].astype(BF16), TN)
                dbl_h.append(ebl[:, sl] * jnp.sum(st_ref[ci, h] * dst1, axis=0, keepdims=True))
                dkd_h.append(_dot(v[:, sl].astype(BF16), dst1b))
                dv2_h.append(dv_h[h] + _dot(kd[:, sl].astype(BF16), dst1b, NT))
            dkd = jnp.concatenate(dkd_h, axis=1)
            dx_scr[ci, 0:CH, :] = dqe * qe + jnp.where(row == CH - 1, jnp.concatenate(dbl_h, axis=1), 0.0)
            dx_scr[ci, CH:2 * CH, :] = dkd * kd
            dlf = _split_dot(wt_ref[...], dx_scr[ci], 2)
            df = dlf / f - (dkd * ekd + jnp.concatenate(dki_h, axis=1))
            dlb_ref[0:1, :] += jnp.sum(df * (1.0 - sig), axis=0, keepdims=True)
            dp_ref[rs, DH:2 * DH] = (df * omlb * sig * (1.0 - sig)).astype(BF16)
            dp_ref[rs, 2 * DH:3 * DH] = jnp.concatenate(dv2_h, axis=1).astype(BF16)

        @pl.when(i == steps - 1)
        def _():
            gl = dlb_ref[0:1, :] * lb * omlb
            dlb_ref[0:1, :] = gl
            dlb_ref[1:2, :] = -gl
            exchange.finish()

    rev = lambda j: pl.BlockSpec((nch * CH, DH), lambda c: (steps - 1 - c, j))
    anyspec = pl.BlockSpec(memory_space=pl.ANY)
    res = pl.pallas_call(
        body, grid=(steps,),
        in_specs=[rev(2), rev(3), rev(4), rev(5), rev(0),
                  pl.BlockSpec((nch, NH, CH, CH), lambda c: (steps - 1 - c, 0, 0, 0)),
                  pl.BlockSpec((nch, NH, HD, HD), lambda c: (steps - 1 - c, 0, 0, 0)),
                  rev(1), _full((2, DH)), _full((1, HD)),
                  _full(w_st.shape), _full(w_st_t.shape), _full(masks_sym.shape),
                  anyspec] + [anyspec] * nw,
        out_specs=[pl.BlockSpec((nch * CH, 4 * DH), lambda c: (steps - 1 - c, 0)), _full((8, DH)), _full((8, HD))]
        + [anyspec] * nw,
        out_shape=[SDS((T, DIN), BF16), SDS((8, DH), F32), SDS((8, HD), F32)] + _slot_shapes(sums),
        scratch_shapes=[pltpu.VMEM((NH, HD, HD), F32), pltpu.VMEM((nch, (2 + n_lev) * CH, DH), F32)] + _exchange_sems(nw),
        input_output_aliases={13: 0},
        compiler_params=_arb(), name="hgrn_bwd")(proj, proj, proj, proj, o_pre, a_all, st_all, dycat, lower_bounds, gn_w,
                                                 w_st, w_st_t, masks_sym, dproj, *sums)
    return res[:3], res[3:]


def _proj_in_bwd(dproj, x, dx1, nw, sc, w_in_b, tm, sums):
    T = x.shape[0]
    ns = len(sums)
    steps = T // tm

    def body(*refs):
        dp_ref, x_ref, dx1_ref, nw_ref, sc_ref, w_ref = refs[:6]
        gx_ref, acc_ref = refs[6 + ns:8 + ns]
        exchange = _ChipExchange(refs[6:6 + ns], refs[8 + ns:8 + 2 * ns], *refs[8 + 2 * ns:])

        @pl.when(pl.program_id(0) == 0)
        def _():
            exchange.start()
            acc_ref[...] = jnp.zeros_like(acc_ref)

        dh = _dot(dp_ref[:, 0:4 * DH], w_ref[:, 2 * DG:DIN], NT) + _dot(dp_ref[:, 4 * DH:DIN], w_ref[:, 0:2 * DG], NT)
        xv = x_ref[...]
        r = _rms(xv)
        xh = xv * r
        n1 = xh * nw_ref[...]
        acc_ref[0:1, :] += jnp.sum(dh, axis=0, keepdims=True)
        acc_ref[1:2, :] += jnp.sum(dh * n1, axis=0, keepdims=True)
        dn = dh * (1.0 + sc_ref[...])
        acc_ref[2:3, :] += jnp.sum(dn * xh, axis=0, keepdims=True)
        gx_ref[...] = dx1_ref[...] + _rms_bwd(xh, r, dn * nw_ref[...])

        @pl.when(pl.program_id(0) == steps - 1)
        def _():
            exchange.finish()

    row = lambda i: (i, 0)
    anyspec = pl.BlockSpec(memory_space=pl.ANY)
    res = pl.pallas_call(
        body, grid=(steps,),
        in_specs=[pl.BlockSpec((tm, DIN), row), pl.BlockSpec((tm, D), row), pl.BlockSpec((tm, D), row),
                  _full((1, D)), _ada_part(ADA_SC1), _resident((D, DIN))] + [anyspec] * ns,
        out_specs=[pl.BlockSpec((tm, D), row), _full((8, D))] + [anyspec] * ns,
        out_shape=[SDS((T, D), F32), SDS((8, D), F32)] + _slot_shapes(sums),
        scratch_shapes=_exchange_sems(ns),
        compiler_params=_arb(), name="proj_in_bwd")(dproj, x, dx1, nw, sc, w_in_b, *sums)
    return res[:2], res[2:]


def _wgrad(a, b, bk, bn, tt, name):
    T, K = a.shape
    N = b.shape[1]
    nn, nk, nt = N // bn, K // bk, T // tt
    bmap = lambda n, k, t: (t, n)

    def body(a_ref, b_ref, o_ref):
        @pl.when(pl.program_id(2) == 0)
        def _():
            o_ref[...] = jnp.zeros_like(o_ref)

        o_ref[0] += _dot(a_ref[...], b_ref[...], TN)

    return pl.pallas_call(
        body, grid=(nn, nk, nt),
        in_specs=[pl.BlockSpec((tt, bk), lambda n, k, t: (t, k)), pl.BlockSpec((tt, bn), bmap)],
        out_specs=pl.BlockSpec((1, bk, bn), lambda n, k, t: (n, k, 0)),
        out_shape=SDS((nn, K, bn), F32),
        compiler_params=_arb(3), name=name)(a, b)


def _adam_math(w, g, m, v):
    m = B1 * m + (1.0 - B1) * g
    v = B2 * v + (1.0 - B2) * (g * g)
    m_hat = m / (1.0 - B1 ** STEP)
    v_hat = v / (1.0 - B2 ** STEP)
    return -LR * (m_hat / (jnp.sqrt(v_hat) + AEPS) + WD * w), m, v


def _adamw_halves(w, mine, sibling, m, v, c_idx, rb, name):
    R, C = w.shape
    nb = (R // 2) // rb

    def body(c_ref, w_ref, a_ref, b_ref, m_ref, v_ref, g_out, d_out, m_out, v_out):
        g = jnp.where(pl.program_id(0) == c_ref[0], a_ref[...], b_ref[...])
        g_out[...] = g
        d_out[...], m_out[...], v_out[...] = _adam_math(w_ref[...], g, m_ref[...], v_ref[...])

    whole = pl.BlockSpec((rb, C), lambda hh, i, cr: (hh * nb + i, 0))
    half = pl.BlockSpec((rb, C), lambda hh, i, cr: (i, 0))
    return pl.pallas_call(
        body,
        grid_spec=pltpu.PrefetchScalarGridSpec(
            num_scalar_prefetch=1, grid=(2, nb), in_specs=[whole, half, half, whole, whole], out_specs=[whole] * 4),
        out_shape=[SDS((R, C), F32)] * 4, compiler_params=_arb(2), name=name)(c_idx, w, mine, sibling, m, v)


def _ada_forward(c_all, w_ada):
    n = w_ada.shape[1]

    def body(c_ref, w_ref, ca_ref, p_ref):
        cv = c_ref[...]
        ca = cv * _sigmoid(cv)
        ca_ref[...] = ca
        p_ref[...] = _dot(ca, w_ref[...], precision=HIGHEST)

    return pl.pallas_call(
        body, grid=(n // 512,),
        in_specs=[_full((N_DEV, D)), pl.BlockSpec((D, 512), lambda i: (0, i))],
        out_specs=[_full((N_DEV, D)), pl.BlockSpec((N_DEV, 512), lambda i: (0, i))],
        out_shape=[SDS((N_DEV, D), F32), SDS((N_DEV, n), F32)],
        compiler_params=_arb(), name="ada_forward")(c_all, w_ada)


def _ada_wgrad_adam(cact_t, dada_all, w, m, v, chip_idx):
    R, C = w.shape
    rb = 256

    def body(j_ref, c_ref, d_ref, w_ref, m_ref, v_ref, g_out, d_out, m_out, v_out):
        g = _dot(c_ref[...], d_ref[...], precision=HIGHEST)
        g_out[...] = g
        d_out[...], m_out[...], v_out[...] = _adam_math(w_ref[...], g, m_ref[...], v_ref[...])

    spec = pl.BlockSpec((rb, C), lambda i, j: (i, 0))
    return pl.pallas_call(
        body,
        grid_spec=pltpu.PrefetchScalarGridSpec(
            num_scalar_prefetch=1, grid=(R // rb,),
            in_specs=[pl.BlockSpec((rb, N_DEV), lambda i, j: (i, 0)), pl.BlockSpec((N_DEV, C), lambda i, j: (0, j[0])),
                      spec, spec, spec],
            out_specs=[spec] * 4),
        out_shape=[SDS((R, C), F32)] * 4,
        compiler_params=_arb(), name="ada_wgrad_adam")(chip_idx, cact_t, dada_all, w, m, v)


SMALL_NAMES = ('b_ada', 'norm1_w', 'norm2_w', 'final_norm_w', 'v_ln_w', 'v_ln_b', 'lower_bounds', 'gn_w', 'b_s', 'w_s')


def _small_finalize(gathered, params, moms, vels):
    n_in = len(gathered)

    def body(*refs):
        acc1, acc2, dln, dlb, dgn, dbs, dws = refs[:n_in]
        prm = [dict(zip(SMALL_NAMES, refs[n_in + k * 10:n_in + (k + 1) * 10])) for k in range(3)]
        outs = [dict(zip(SMALL_NAMES, refs[n_in + 30 + k * 10:n_in + 30 + (k + 1) * 10])) for k in range(4)]
        loss_ref, dada_ref = refs[n_in + 70:n_in + 72]

        def dev_sum(ref, first, n):
            per = ref.shape[0] // N_DEV
            g = ref[first:first + n, :]
            for dev in range(1, N_DEV):
                g = g + ref[dev * per + first:dev * per + first + n, :]
            return g

        def update(n, g, cols=slice(None)):
            outs[0][n][:, cols] = g
            outs[1][n][:, cols], outs[2][n][:, cols], outs[3][n][:, cols] = _adam_math(
                prm[0][n][:, cols], g, prm[1][n][:, cols], prm[2][n][:, cols])

        ada_rows = ((acc1, 0), (acc1, 1), (acc2, 5), (acc2, 2), (acc2, 1), (acc2, 0))
        for k, (ref, r) in enumerate(ada_rows):
            update('b_ada', dev_sum(ref, r, 1), slice(k * D, (k + 1) * D))
            for dev in range(N_DEV):
                dada_ref[dev:dev + 1, k * D:(k + 1) * D] = ref[8 * dev + r:8 * dev + r + 1, :]
        update('norm1_w', dev_sum(acc1, 2, 1))
        update('norm2_w', dev_sum(acc2, 3, 1))
        update('final_norm_w', dev_sum(acc2, 4, 1))
        update('v_ln_w', dev_sum(dln, 0, 1))
        update('v_ln_b', dev_sum(dln, 1, 1))
        update('lower_bounds', dev_sum(dlb, 0, 2))
        update('gn_w', dev_sum(dgn, 0, 1))
        update('b_s', dev_sum(dbs, 0, NH))
        update('w_s', dev_sum(dws, 0, NH * BLK))
        loss_ref[...] = jnp.sum(dev_sum(acc2, 6, 1), axis=-1, keepdims=True)

    shapes = [SDS(params[n].shape, F32) for n in SMALL_NAMES]
    res = pl.pallas_call(
        body, out_shape=shapes * 4 + [SDS((1, 1), F32), SDS((N_DEV, 6 * D), F32)], name="small_finalize")(
            *gathered, *[d[n] for d in (params, moms, vels) for n in SMALL_NAMES])
    return [dict(zip(SMALL_NAMES, res[k * 10:(k + 1) * 10])) for k in range(4)], res[40], res[41]


def _position():
    x, y, c = lax.axis_index("x"), lax.axis_index("y"), lax.axis_index("c")
    return x, y, c


def _chip_at(x, y, r):
    return (x ^ (r >> 1), y ^ (r & 1))


def _gather_rows(ins, outs, send_sems, recv_sems, local_sems, after_issue=None):
    nb = len(ins)
    x, y, c = _position()
    me, sibling = (x, y, c), (x, y, 1 - c)
    chips = [_chip_at(x, y, r) for r in (1, 2, 3)]

    def rows(b, px, py, pc):
        m_per = ins[b].shape[0]
        return outs[b].at[pl.ds((4 * px + 2 * py + pc) * m_per, m_per), :]

    def copy(b, k, blk, to, src=None):
        return pltpu.make_async_remote_copy(
            src_ref=rows(b, *blk) if src is None else src, dst_ref=rows(b, *blk),
            send_sem=send_sems.at[7 * b + k], recv_sem=recv_sems.at[7 * b + k], device_id=to, device_id_type=MESH)

    local, sent = [], []
    for b in range(nb):
        mine = pltpu.make_async_copy(ins[b], rows(b, *me), local_sems.at[b])
        mine.start()
        local.append(mine)
        first = [copy(b, 0, me, sibling, src=ins[b])]
        first += [copy(b, 1 + j, me, (*chip, c), src=ins[b]) for j, chip in enumerate(chips)]
        for cp in first:
            cp.start()
        sent += first
    if after_issue is not None:
        after_issue()
    for b in range(nb):
        for j, chip in enumerate(chips):
            copy(b, 1 + j, (*chip, c), me).wait_recv()
            passed = copy(b, 4 + j, (*chip, c), sibling)
            passed.start()
            sent.append(passed)
    for b in range(nb):
        copy(b, 0, sibling, me).wait_recv()
        for j, chip in enumerate(chips):
            copy(b, 4 + j, (*chip, 1 - c), me).wait_recv()
    for cp in sent:
        cp.wait_send()
    for cp in local:
        cp.wait()


def _gather_rows_shapes(blocks):
    return [SDS((N_DEV * b.shape[0], b.shape[1]), b.dtype) for b in blocks]


def _gather_rows_sems(nb):
    return [pltpu.SemaphoreType.DMA((7 * nb,)), pltpu.SemaphoreType.DMA((7 * nb,)), pltpu.SemaphoreType.DMA((nb,))]


def _all_gather_rows(blocks, name):
    nb = len(blocks)

    def body(*refs):
        _gather_rows(refs[:nb], refs[nb:2 * nb], *refs[2 * nb:])

    vmem = pl.BlockSpec(memory_space=pltpu.VMEM)
    return pl.pallas_call(
        body, out_shape=_gather_rows_shapes(blocks), in_specs=[vmem] * nb, out_specs=[vmem] * nb,
        scratch_shapes=_gather_rows_sems(nb), name=name)(*blocks)


def _place_shard(w_shard, axis, chip_idx, name):
    R, C = w_shard.shape
    rb = _row_block(R)
    nb = R // rb
    full = (R * N_CHIPS, C) if axis == 0 else (R, C * N_CHIPS)
    omap = (lambda i, j: (j[0] * nb + i, 0)) if axis == 0 else (lambda i, j: (i, j[0]))

    def body(j_ref, w_ref, o_ref):
        o_ref[...] = w_ref[...].astype(BF16)

    return pl.pallas_call(
        body,
        grid_spec=pltpu.PrefetchScalarGridSpec(
            num_scalar_prefetch=1, grid=(nb,), in_specs=[pl.BlockSpec((rb, C), lambda i, j: (i, 0))],
            out_specs=pl.BlockSpec((rb, C), omap)),
        out_shape=SDS(full, BF16), compiler_params=_arb(), name=name)(chip_idx, w_shard)


class _WeightGather:
    def __init__(self, refs, axes, send_sems, recv_sems):
        self.refs, self.axes, self.send_sems, self.recv_sems = refs, axes, send_sems, recv_sems
        self.x, self.y, self.c = _position()
        self.j = 2 * self.x + self.y
        self.n = 3 * len(refs)

    def _half(self, w, chip_idx, half):
        ref, axis = self.refs[w], self.axes[w]
        if axis == 0:
            size = ref.shape[0] // N_CHIPS
            return ref.at[pl.ds(chip_idx * size + half * (size // 2), size // 2), :]
        size = ref.shape[1] // N_CHIPS
        rows = ref.shape[0] // 2
        return ref.at[pl.ds(half * rows, rows), pl.ds(chip_idx * size, size)]

    def _ici(self, w, r, chip_idx):
        k = 3 * w + r - 1
        piece = self._half(w, chip_idx, self.c)
        return pltpu.make_async_remote_copy(
            src_ref=piece, dst_ref=piece, send_sem=self.send_sems.at[k], recv_sem=self.recv_sems.at[k],
            device_id=(*_chip_at(self.x, self.y, r), self.c), device_id_type=MESH)

    def _d2d(self, w, r, half):
        k = self.n + 3 * w + r - 1
        piece = self._half(w, self.j ^ r, half)
        return pltpu.make_async_remote_copy(
            src_ref=piece, dst_ref=piece, send_sem=self.send_sems.at[k], recv_sem=self.recv_sems.at[k],
            device_id=(self.x, self.y, 1 - self.c), device_id_type=MESH)

    def _each(self):
        return [(w, r) for w in range(len(self.refs)) for r in (1, 2, 3)]

    def start(self):
        for w, r in self._each():
            self._ici(w, r, self.j).start()

    def forward(self):
        for w, r in self._each():
            self._ici(w, r, self.j ^ r).wait_recv()
            self._d2d(w, r, self.c).start()

    def finish(self):
        for w, r in self._each():
            self._ici(w, r, self.j).wait_send()
            self._d2d(w, r, self.c).wait_send()
            self._d2d(w, r, 1 - self.c).wait_recv()


def _gather_sems(n_weights):
    return [pltpu.SemaphoreType.DMA((6 * n_weights,)), pltpu.SemaphoreType.DMA((6 * n_weights,))]


def _gather_weights(placed, axes, row_blocks, name):
    nw, nb = len(placed), len(row_blocks)

    def body(*refs):
        w_outs, b_ins, b_outs = refs[nw + nb:2 * nw + nb], refs[nw:nw + nb], refs[2 * nw + nb:2 * (nw + nb)]
        sems = refs[2 * (nw + nb):]
        g = _WeightGather(w_outs, axes, *sems[:2])
        _gather_rows(b_ins, b_outs, *sems[2:], after_issue=g.start)
        g.forward()
        g.finish()

    anyspec = pl.BlockSpec(memory_space=pl.ANY)
    vmem = pl.BlockSpec(memory_space=pltpu.VMEM)
    res = pl.pallas_call(
        body, out_shape=[SDS(a.shape, a.dtype) for a in placed] + _gather_rows_shapes(row_blocks),
        in_specs=[anyspec] * nw + [vmem] * nb, out_specs=[anyspec] * nw + [vmem] * nb,
        scratch_shapes=_gather_sems(nw) + _gather_rows_sems(nb), input_output_aliases={i: i for i in range(nw)},
        name=name)(*placed, *row_blocks)
    return res[:nw], res[nw:]


class _ChipExchange:
    def __init__(self, ins, outs, send_sems, recv_sems):
        self.ins, self.outs, self.send_sems, self.recv_sems = ins, outs, send_sems, recv_sems
        self.x, self.y, self.c = _position()
        self.j = 2 * self.x + self.y

    def _copies(self):
        for w in range(len(self.ins)):
            for r in (1, 2, 3):
                k = 3 * w + r - 1
                yield pltpu.make_async_remote_copy(
                    src_ref=self.ins[w].at[self.j ^ r], dst_ref=self.outs[w].at[r - 1],
                    send_sem=self.send_sems.at[k], recv_sem=self.recv_sems.at[k],
                    device_id=(*_chip_at(self.x, self.y, r), self.c), device_id_type=MESH)

    def start(self):
        for cp in self._copies():
            cp.start()

    def finish(self):
        for cp in self._copies():
            cp.wait()


def _exchange_sems(n_weights):
    return [pltpu.SemaphoreType.DMA((3 * n_weights,)), pltpu.SemaphoreType.DMA((3 * n_weights,))]


class _CoreExchange:
    def __init__(self, ins, outs, send_sems, recv_sems):
        self.ins, self.outs, self.send_sems, self.recv_sems = ins, outs, send_sems, recv_sems
        self.x, self.y, self.c = _position()

    def _copies(self):
        for w in range(len(self.ins)):
            yield pltpu.make_async_remote_copy(
                src_ref=self.ins[w].at[:, 1 - self.c], dst_ref=self.outs[w],
                send_sem=self.send_sems.at[w], recv_sem=self.recv_sems.at[w],
                device_id=(self.x, self.y, 1 - self.c), device_id_type=MESH)

    def start(self):
        for cp in self._copies():
            cp.start()

    def finish(self):
        for cp in self._copies():
            cp.wait()


def _core_exchange_shapes(grads):
    return [SDS((g.shape[0], g.shape[2], g.shape[3]), F32) for g in grads]


def _core_exchange_sems(n):
    return [pltpu.SemaphoreType.DMA((n,)), pltpu.SemaphoreType.DMA((n,))]


def _exchange_core_halves(grads, name):
    nw = len(grads)

    def body(*refs):
        ex = _CoreExchange(refs[:nw], refs[nw:2 * nw], *refs[2 * nw:])
        ex.start()
        ex.finish()

    anyspec = pl.BlockSpec(memory_space=pl.ANY)
    return pl.pallas_call(
        body, out_shape=_core_exchange_shapes(grads), in_specs=[anyspec] * nw, out_specs=[anyspec] * nw,
        scratch_shapes=_core_exchange_sems(nw), name=name)(*grads)


def _add_core_halves(g4, recv, c_idx, rb, name):
    ns, _, rh, C = g4.shape

    def body(c_ref, g_ref, r_ref, o_ref):
        o_ref[...] = (g_ref[0] + r_ref[...]).astype(BF16)

    return pl.pallas_call(
        body,
        grid_spec=pltpu.PrefetchScalarGridSpec(
            num_scalar_prefetch=1, grid=(ns, rh // rb),
            in_specs=[pl.BlockSpec((1, 1, rb, C), lambda s, i, cr: (s, cr[0], i, 0)),
                      pl.BlockSpec((1, rb, C), lambda s, i, cr: (s, i, 0))],
            out_specs=pl.BlockSpec((1, rb, C), lambda s, i, cr: (s, i, 0))),
        out_shape=SDS((ns, rh, C), BF16), compiler_params=_arb(2), name=name)(c_idx, g4, recv)


def _add_core_halves_in(g4, recv, c_idx, name):
    n_slabs, _, rh, C = g4.shape
    cb = 256
    per_slab, per_chip, n_blocks = C // cb, DIN // N_CHIPS // cb, DIN // cb

    def stored(s, k):
        sb = (per_chip * s + k + 4 * DH // cb) % n_blocks
        return sb // per_slab, sb % per_slab

    def body(c_ref, g_ref, r_ref, o_ref):
        o_ref[...] = (g_ref[0] + r_ref[...]).astype(BF16)

    return pl.pallas_call(
        body,
        grid_spec=pltpu.PrefetchScalarGridSpec(
            num_scalar_prefetch=1, grid=(N_CHIPS, per_chip),
            in_specs=[pl.BlockSpec((1, 1, rh, cb), lambda s, k, cr: (stored(s, k)[0], cr[0], 0, stored(s, k)[1])),
                      pl.BlockSpec((1, rh, cb), lambda s, k, cr: (stored(s, k)[0], 0, stored(s, k)[1]))],
            out_specs=pl.BlockSpec((1, rh, cb), lambda s, k, cr: (s, 0, k))),
        out_shape=SDS((N_CHIPS, rh, DIN // N_CHIPS), BF16), compiler_params=_arb(2), name=name)(c_idx, g4, recv)


def _slot_shapes(sums):
    return [SDS((3,) + s.shape[1:], s.dtype) for s in sums]


def _add_chips(own, slots, order, rb, name):
    _, rh, C = slots.shape

    def body(o_ref, own_ref, a_ref, b_ref, c_ref, d_ref, out_ref):
        mine = own_ref[0].astype(F32)
        t = [jnp.where(o_ref[i] == 0, mine, r[0].astype(F32)) for i, r in enumerate((a_ref, b_ref, c_ref, d_ref))]
        out_ref[...] = ((t[0] + t[1]) + t[2]) + t[3]

    def spec(i):
        return pl.BlockSpec((1, rb, C), lambda t, o: (jnp.maximum(o[i], 1) - 1, t, 0))

    return pl.pallas_call(
        body,
        grid_spec=pltpu.PrefetchScalarGridSpec(
            num_scalar_prefetch=1, grid=(rh // rb,),
            in_specs=[pl.BlockSpec((1, rb, C), lambda t, o: (o[4], t, 0)), spec(0), spec(1), spec(2), spec(3)],
            out_specs=pl.BlockSpec((rb, C), lambda t, o: (t, 0))),
        out_shape=SDS((rh, C), F32), compiler_params=_arb(), name=name)(order, own, slots, slots, slots, slots)


def _share_halves(halves):
    nw = len(halves)

    def body(*refs):
        ins, outs = refs[:nw], refs[nw:2 * nw]
        send_sems, recv_sems = refs[2 * nw:]
        x, y, c = _position()
        started = []
        for w in range(nw):
            cp = pltpu.make_async_remote_copy(
                src_ref=ins[w], dst_ref=outs[w], send_sem=send_sems.at[w], recv_sem=recv_sems.at[w],
                device_id=(x, y, 1 - c), device_id_type=MESH)
            cp.start()
            started.append(cp)
        for cp in started:
            cp.wait()

    anyspec = pl.BlockSpec(memory_space=pl.ANY)
    return pl.pallas_call(
        body, out_shape=[SDS(h.shape, F32) for h in halves], in_specs=[anyspec] * nw, out_specs=[anyspec] * nw,
        scratch_shapes=[pltpu.SemaphoreType.DMA((nw,)), pltpu.SemaphoreType.DMA((nw,))],
        name="share_halves")(*halves)


def _small_2d(b_ada, norm1_w, norm2_w, final_norm_w, v_ln_w, v_ln_b, lower_bounds, gn_w, b_s, w_s):
    return dict(zip(SMALL_NAMES, (b_ada, norm1_w, norm2_w, final_norm_w.reshape(1, D), v_ln_w, v_ln_b, lower_bounds, gn_w,
                                  b_s.reshape(NH, BLK), w_s.reshape(NH * BLK, BLK))))


def _small_original_shapes(d):
    out = dict(d)
    out['final_norm_w'] = d['final_norm_w'].reshape(D)
    out['b_s'] = d['b_s'].reshape(1, NH, BLK)
    out['w_s'] = d['w_s'].reshape(1, NH, BLK, BLK)
    return out


def _row_block(r):
    for cand in (256, 176, 128, 64, 32, 16, 8):
        if r % cand == 0:
            return cand
    return r


def kernel(x, c, w_ada, b_ada, norm1_w, w_in, w_s, b_s, v_ln_w, v_ln_b, lower_bounds, gn_w, w_out, norm2_w, w_ffn_in, w_ffn_out, final_norm_w, loss_target, m_w_ada, m_b_ada, m_norm1_w, m_w_in, m_w_s, m_b_s, m_v_ln_w, m_v_ln_b, m_lower_bounds, m_gn_w, m_w_out, m_norm2_w, m_w_ffn_in, m_w_ffn_out, m_final_norm_w, v_w_ada, v_b_ada, v_norm1_w, v_w_in, v_w_s, v_b_s, v_v_ln_w, v_v_ln_b, v_lower_bounds, v_gn_w, v_w_out, v_norm2_w, v_w_ffn_in, v_w_ffn_out, v_final_norm_w):
    T = x.shape[1]
    tm, tp = min(TOKEN_TILE, T), min(PROJ_TILE, T)
    px, py, pc = _position()
    chip = 2 * px + py
    me = 4 * px + 2 * py + pc
    x2d = x.reshape(T, D)
    tgt = loss_target.reshape(T, D)

    chip_idx = jnp.reshape(chip, (1,)).astype(jnp.int32)
    c_idx = jnp.reshape(pc, (1,)).astype(jnp.int32)
    (w_in_b,), (c_all,) = _gather_weights(
        [_place_shard(w_in[0], 1, chip_idx, "place_in")], [1], [jnp.broadcast_to(c, (8, D))], "gather_w_in_and_c")
    placed = [_place_shard(w_out[0], 0, chip_idx, "place_out"), _place_shard(w_ffn_in[0], 1, chip_idx, "place_ffn_in"),
              _place_shard(w_ffn_out[0], 0, chip_idx, "place_ffn_out")]

    cact, ada_part = _ada_forward(c_all.reshape(N_DEV, 8, D)[:, 0, :], w_ada[0])
    n_ada = ada_part.shape[1]
    (ada_all,) = _all_gather_rows([ada_part], "gather_ada")
    ada_all = ada_all.reshape(N_CHIPS, 2, N_DEV, n_ada)[:, 0]
    ada = lax.dynamic_index_in_dim(ada_all, me, axis=1, keepdims=False).reshape(1, 6 * D) + b_ada

    rr = lax.broadcasted_iota(jnp.int32, (BLK, BLK), 0) // CH
    cc = lax.broadcasted_iota(jnp.int32, (BLK, BLK), 1) // CH
    ws_b = jnp.where((rr >= cc)[None], w_s[0], 0.0).astype(BF16)
    bst = b_s[0].T
    lnw, lnb = v_ln_w, v_ln_b
    nw1, nw2, fw = norm1_w, norm2_w, final_norm_w.reshape(1, D)

    h1, proj = _proj_in(x2d, nw1, ada, ada, w_in_b, tp)
    ycat = _gmlp_fwd(proj, ws_b, bst, lnw, lnb)
    tables = _hgrn_tables()
    (ycat, o_pre, a_all, st_all), (w_out_b, w_fi_b, w_fo_b) = _hgrn_fwd(
        proj, lower_bounds, gn_w, ycat, tables, placed, [0, 1, 0])

    dycat, dx1, h2, act, dff, dgu, dmix, acc2 = _token_local(
        x2d, ycat, tgt, ada, nw2, ada, ada, ada, fw, w_out_b, w_fi_b, w_fo_b, tm)

    tt = min(WGRAD_TOKENS, T)
    order = jnp.concatenate([chip ^ jnp.arange(N_CHIPS, dtype=jnp.int32), chip_idx]).astype(jnp.int32)

    def by_core_half(g):
        return g.reshape(g.shape[0], 2, g.shape[1] // 2, g.shape[2])

    def core_sums(g4, recv, names):
        return [_add_core_halves(a, b, c_idx, _row_block(a.shape[2]), "add_core_" + n) for a, b, n in zip(g4, recv, names)]

    def chip_sums(sums, slots, names):
        return [_add_chips(o, s, order, _row_block(s.shape[1]), "add_chips_" + n) for o, s, n in zip(sums, slots, names)]

    g_out = _wgrad(ycat, dmix, D, D, tt, "wgrad_out").reshape(N_CHIPS, D // N_CHIPS, D)
    g_fi = _wgrad(h2, dgu, D, FFB, tt, "wgrad_ffn_in")
    g_fo = _wgrad(act, dff, FFB, D, tt, "wgrad_ffn_out").reshape(N_CHIPS, DFF // N_CHIPS, D)
    late_names = ["out", "ffn_in", "ffn_out"]
    late_g4 = [by_core_half(g) for g in (g_out, g_fi, g_fo)]

    (dproj, dws, dbs, dln), late_recv = _gmlp_bwd(proj, dycat, ws_b, bst, lnw, lnb, late_g4)
    late_sums = core_sums(late_g4, late_recv, late_names)
    (dproj, dlb, dgn), late_slots = _hgrn_bwd(
        proj, o_pre, a_all, st_all, dycat, lower_bounds, gn_w, dproj, tables, late_sums)

    g_in = _wgrad(h1, dproj, D, D, tt, "wgrad_in")
    in_g4 = [by_core_half(g_in)]
    (in_recv,) = _exchange_core_halves(in_g4, "exchange_core_halves_in")
    in_sums = [_add_core_halves_in(in_g4[0], in_recv, c_idx, "add_core_in")]
    (grad_x, acc1), in_slots = _proj_in_bwd(dproj, x2d, dx1, nw1, ada, w_in_b, tp, in_sums)
    names = ["in"] + late_names
    halves = chip_sums(in_sums, in_slots, ["in"]) + chip_sums(late_sums, late_slots, late_names)
    sibling_halves = _share_halves(halves)

    big_w = [(w_in, m_w_in, v_w_in), (w_out, m_w_out, v_w_out), (w_ffn_in, m_w_ffn_in, v_w_ffn_in),
             (w_ffn_out, m_w_ffn_out, v_w_ffn_out)]
    big_out = []
    for mine, sib, (w, m, v), n in zip(halves, sibling_halves, big_w, names):
        res = _adamw_halves(w[0], mine, sib, m[0], v[0], c_idx, _row_block(mine.shape[0]), "adamw_" + n)
        big_out.append([r[None] for r in res])

    gathered = _all_gather_rows([acc1, acc2, dln, dlb, dgn, dbs, dws], "gather_small")
    small, loss, dada_all = _small_finalize(
        gathered,
        _small_2d(b_ada, norm1_w, norm2_w, final_norm_w, v_ln_w, v_ln_b, lower_bounds, gn_w, b_s, w_s),
        _small_2d(m_b_ada, m_norm1_w, m_norm2_w, m_final_norm_w, m_v_ln_w, m_v_ln_b, m_lower_bounds, m_gn_w, m_b_s, m_w_s),
        _small_2d(v_b_ada, v_norm1_w, v_norm2_w, v_final_norm_w, v_v_ln_w, v_v_ln_b, v_lower_bounds, v_gn_w, v_b_s, v_w_s))
    small = [_small_original_shapes(d) for d in small]
    loss = loss.reshape(())

    ada_out = [o[None] for o in _ada_wgrad_adam(cact.T, dada_all, w_ada[0], m_w_ada[0], v_w_ada[0], chip_idx)]

    order_names = ['w_ada', 'b_ada', 'norm1_w', 'w_in', 'w_s', 'b_s', 'v_ln_w', 'v_ln_b', 'lower_bounds', 'gn_w',
                   'w_out', 'norm2_w', 'w_ffn_in', 'w_ffn_out', 'final_norm_w']
    big_idx = {'w_in': 0, 'w_out': 1, 'w_ffn_in': 2, 'w_ffn_out': 3}
    outs = [loss, grad_x.reshape(1, T, D)]
    for kind in range(4):
        for n in order_names:
            if n == 'w_ada':
                outs.append(ada_out[kind])
            elif n in big_idx:
                outs.append(big_out[big_idx[n]][kind])
            else:
                outs.append(small[kind][n])
    return tuple(outs)
```

```python
import functools

import jax
import jax.numpy as jnp
import numpy as np
from jax import lax
from jax.experimental import pallas as pl
from jax.experimental.pallas import tpu as pltpu

F32 = jnp.float32
BF16 = jnp.bfloat16
SDS = jax.ShapeDtypeStruct
MESH = pl.DeviceIdType.MESH
HIGHEST = lax.Precision.HIGHEST

D = 1024
DG = 512
DH = 512
NH = 4
HD = 128
BLK = 128
CH = 64
DFF = 2816
DIN = 3072
FFB = 1408
LEVELS = (64, 32, 16, 8, 4, 2)
HGRN_CHUNKS_PER_STEP = 8
GMLP_ROWS_PER_STEP = 1024
TOKEN_TILE = 256
PROJ_TILE = 1024
WGRAD_TOKENS = 2048
N_CHIPS = 4
N_DEV = 8
EPS = 1e-6
LR, B1, B2, AEPS, WD, STEP = 0.001, 0.9, 0.999, 1e-08, 0.01, 10

NT = (((1,), (1,)), ((), ()))
TN = (((0,), (0,)), ((), ()))


def _full(shape):
    nd = len(shape)
    return pl.BlockSpec(shape, lambda *_: (0,) * nd)


ADA_SH1, ADA_SC1, ADA_G1, ADA_SH2, ADA_SC2, ADA_G2 = range(6)


def _ada_part(k):
    return pl.BlockSpec((1, D), lambda *_: (0, k))


def _resident(shape):
    nd = len(shape)
    return pl.BlockSpec(shape, lambda *_: (0,) * nd, pipeline_mode=pl.Buffered(1))


def _arb(n=1):
    return pltpu.CompilerParams(dimension_semantics=("arbitrary",) * n)


def _dot(a, b, dims=None, precision=None):
    if dims is None:
        return jnp.dot(a, b, preferred_element_type=F32, precision=precision)
    return lax.dot_general(a, b, dims, preferred_element_type=F32, precision=precision)


def _sigmoid(x):
    return jax.nn.sigmoid(x)


def _gelu_parts(x):
    cdf = 0.5 * (1.0 + lax.erf(x * 0.7071067811865476))
    pdf = jnp.exp(-0.5 * x * x) * 0.3989422804014327
    return x * cdf, cdf + x * pdf


def _rms(x):
    return lax.rsqrt(jnp.mean(x * x, axis=-1, keepdims=True) + EPS)


def _rms_bwd(xhat, r, gw):
    return r * (gw - xhat * jnp.mean(xhat * gw, axis=-1, keepdims=True))


def _lower_bound(lbp_ref):
    l0, l1 = lbp_ref[0:1, :], lbp_ref[1:2, :]
    m = jnp.maximum(l0, l1)
    e0, e1 = jnp.exp(l0 - m), jnp.exp(l1 - m)
    return e0 / (e0 + e1), e1 / (e0 + e1)


def _proj_in(x, nw, sc, sh, w_in_b, tm):
    T = x.shape[0]

    def body(x_ref, nw_ref, sc_ref, sh_ref, w_ref, h_ref, p_ref):
        xv = x_ref[...]
        h = ((xv * _rms(xv)) * nw_ref[...]) * (1.0 + sc_ref[...]) + sh_ref[...]
        hb = h.astype(BF16)
        h_ref[...] = hb
        p_ref[...] = _dot(hb, w_ref[...])

    row = lambda i: (i, 0)
    return pl.pallas_call(
        body, grid=(T // tm,),
        in_specs=[pl.BlockSpec((tm, D), row), _full((1, D)), _ada_part(ADA_SC1), _ada_part(ADA_SH1), _resident((D, DIN))],
        out_specs=[pl.BlockSpec((tm, D), row), pl.BlockSpec((tm, DIN), row)],
        out_shape=[SDS((T, D), BF16), SDS((T, DIN), F32)],
        compiler_params=_arb(), name="proj_in")(x, nw, sc, sh, w_in_b)


def _gmlp_common(u, v, lnw, lnb, ws_ref, bst_ref):
    ug, dug = _gelu_parts(u)
    vg, dvg = _gelu_parts(v)
    mu = jnp.mean(vg, axis=-1, keepdims=True)
    vc = vg - mu
    rstd = lax.rsqrt(jnp.mean(vc * vc, axis=-1, keepdims=True) + EPS)
    vhat = vc * rstd
    vn = vhat * lnw + lnb
    vnb = vn.astype(BF16)
    mixed = []
    for h in range(NH):
        sl = slice(h * HD, (h + 1) * HD)
        mixed.append(_dot(ws_ref[h], vnb[:, sl]) + bst_ref[:, h:h + 1])
    return ug, dug, dvg, rstd, vhat, vnb, jnp.concatenate(mixed, axis=1)


def _gmlp_fwd(proj, ws_b, bst, lnw, lnb):
    T = proj.shape[0]
    rows = min(GMLP_ROWS_PER_STEP, T)

    def body(u_ref, v_ref, ws_ref, bst_ref, lnw_ref, lnb_ref, y_ref):
        for bi in range(rows // BLK):
            rs = slice(bi * BLK, (bi + 1) * BLK)
            ug, _, _, _, _, _, mixed = _gmlp_common(u_ref[rs, :], v_ref[rs, :], lnw_ref[...], lnb_ref[...], ws_ref, bst_ref)
            y_ref[rs, :] = (ug * mixed).astype(BF16)

    return pl.pallas_call(
        body, grid=(T // rows,),
        in_specs=[pl.BlockSpec((rows, DG), lambda i: (i, 0)), pl.BlockSpec((rows, DG), lambda i: (i, 1)),
                  _full((NH, BLK, BLK)), _full((BLK, NH)), _full((1, DG)), _full((1, DG))],
        out_specs=pl.BlockSpec((rows, DG), lambda i: (i, 0)),
        out_shape=SDS((T, D), BF16),
        compiler_params=_arb(), name="gmlp_fwd")(proj, proj, ws_b, bst, lnw, lnb)


def _hgrn_tables():
    t = np.arange(CH)[:, None]
    j = np.arange(CH)[None, :]
    blocks = [j <= t, j > t]
    masks = []
    for n in LEVELS:
        mid = t - t % n + n // 2
        blocks.append(np.where(t >= mid, (j >= mid) & (j <= t), (j > t) & (j < mid)))
        masks.append((t // n == j // n) & (t % n >= n // 2) & (j % n < n // 2))
    w = np.concatenate(blocks, axis=0).astype(np.float32)
    m = np.stack(masks).astype(np.float32)
    return (jnp.asarray(w, BF16), jnp.asarray(w.T, BF16), jnp.asarray(m), jnp.asarray(m + m.transpose(0, 2, 1)))


def _split_dot(w, x, parts):
    acc = None
    for _ in range(parts):
        piece = x.astype(BF16)
        term = _dot(w, piece)
        acc = term if acc is None else acc + term
        x = x - piece.astype(F32)
    return acc


def _hgrn_decays(f, w_ref):
    b = _split_dot(w_ref[0:CH, :], jnp.log(f), 3)
    row = lax.broadcasted_iota(jnp.int32, (CH, 1), 0)
    blocks = [jnp.exp(b), jnp.exp(b[CH - 1:CH, :] - b)]
    for n in LEVELS:
        up = (row & (n // 2)) != 0
        if n >= 8:
            ref = b.reshape(CH // n, n, DH)[:, n // 2 - 1:n // 2, :]
            ref = jnp.broadcast_to(ref, (CH // n, n, DH)).reshape(CH, DH)
            blocks.append(jnp.exp(jnp.where(up, b - ref, ref - b)))
        elif n == 4:
            r4 = row & 3
            two = jnp.where(r4 == 3, pltpu.roll(f, 1, 0) * f, 1.0)
            blocks.append(jnp.where(r4 == 0, pltpu.roll(f, CH - 1, 0), jnp.where(r4 == 2, f, two)))
        else:
            blocks.append(jnp.where(up, f, 1.0))
    return blocks


def _hgrn_gates(q, fl, lb, omlb, w_ref):
    sq = _sigmoid(q)
    qf = q * sq
    sig = _sigmoid(fl)
    f = lb + omlb * sig
    k = 1.0 - f
    return sq, qf, sig, f, k, _hgrn_decays(f, w_ref)


def _level_factor(e, li, sl, row, qh, kh):
    el = e[2 + li][:, sl]
    up = (row & (LEVELS[li] // 2)) != 0
    return el, up, el * jnp.where(up, qh, kh)


def _hgrn_fwd(proj, lower_bounds, gn_w, ycat, tables, placed, axes):
    T = proj.shape[0]
    nc = T // CH
    nch = min(HGRN_CHUNKS_PER_STEP, nc)
    steps = nc // nch
    w_st, _, masks, _ = tables
    nw = len(placed)
    pass_step = (5 * steps) // 8

    def body(*refs):
        q_ref, f_ref, i_ref, g_ref, lbp_ref, gn_ref, w_ref, m_ref = refs[:8]
        y_ref, o_ref, a_ref, st_ref = refs[9 + nw:13 + nw]
        s_scr, send_sems, recv_sems = refs[13 + 2 * nw:]
        gather = _WeightGather(refs[13 + nw:13 + 2 * nw], axes, send_sems, recv_sems)
        step = pl.program_id(0)

        @pl.when(step == 0)
        def _():
            gather.start()
            s_scr[...] = jnp.zeros_like(s_scr)

        @pl.when(step == pass_step)
        def _():
            gather.forward()

        lb, omlb = _lower_bound(lbp_ref)
        row = lax.broadcasted_iota(jnp.int32, (CH, 1), 0)
        eye = lax.broadcasted_iota(jnp.int32, (CH, CH), 0) == lax.broadcasted_iota(jnp.int32, (CH, CH), 1)
        in_level = [m_ref[li] > 0.0 for li in range(len(LEVELS))]
        pre = []
        for ci in range(nch):
            rs = slice(ci * CH, (ci + 1) * CH)
            _, qf, _, _, k, e = _hgrn_gates(q_ref[rs, :], f_ref[rs, :], lb, omlb, w_ref)
            mats = []
            for h in range(NH):
                sl = slice(h * HD, (h + 1) * HD)
                qh, kh = qf[:, sl], k[:, sl]
                a = jnp.where(eye, jnp.sum(qh * kh, axis=-1, keepdims=True), 0.0)
                for li in range(len(LEVELS)):
                    _, _, y = _level_factor(e, li, sl, row, qh, kh)
                    yb = y.astype(BF16)
                    a = jnp.where(in_level[li], _dot(yb, yb, NT), a)
                a_ref[ci, h] = a
                mats.append(a.astype(BF16))
            eb = e[0]
            pre.append(((qf * eb).astype(BF16), eb[CH - 1:CH, :], (k * e[1]).astype(BF16), mats))
        for ci in range(nch):
            rs = slice(ci * CH, (ci + 1) * CH)
            qe, ebl, kd, mats = pre[ci]
            v = i_ref[rs, :]
            g = g_ref[rs, :]
            for h in range(NH):
                sl = slice(h * HD, (h + 1) * HD)
                st0 = s_scr[h]
                st_ref[ci, h] = st0
                vb = v[:, sl].astype(BF16)
                o = _dot(qe[:, sl], st0.astype(BF16), NT) + _dot(mats[h], vb)
                s_scr[h] = st0 * ebl[:, sl] + _dot(vb, kd[:, sl], TN)
                o_ref[rs, sl] = o
                gh = g[:, sl]
                y_ref[rs, sl] = (((o * _rms(o)) * gn_ref[...]) * (gh * _sigmoid(gh))).astype(BF16)

        @pl.when(step == steps - 1)
        def _():
            gather.finish()

    blk = lambda j: pl.BlockSpec((nch * CH, DH), lambda c: (c, j))
    anyspec = pl.BlockSpec(memory_space=pl.ANY)
    res = pl.pallas_call(
        body, grid=(steps,),
        in_specs=[blk(2), blk(3), blk(4), blk(5), _full((2, DH)), _full((1, HD)),
                  _full(w_st.shape), _full(masks.shape), anyspec] + [anyspec] * nw,
        out_specs=[pl.BlockSpec((nch * CH, DH), lambda c: (c, 1)),
                   pl.BlockSpec((nch * CH, DH), lambda c: (c, 0)),
                   pl.BlockSpec((nch, NH, CH, CH), lambda c: (c, 0, 0, 0)),
                   pl.BlockSpec((nch, NH, HD, HD), lambda c: (c, 0, 0, 0))] + [anyspec] * nw,
        out_shape=[SDS((T, D), BF16), SDS((T, DH), F32), SDS((nc, NH, CH, CH), F32), SDS((nc, NH, HD, HD), F32)]
        + [SDS(a.shape, a.dtype) for a in placed],
        scratch_shapes=[pltpu.VMEM((NH, HD, HD), F32)] + _gather_sems(nw),
        input_output_aliases={8: 0, **{9 + i: 4 + i for i in range(nw)}},
        compiler_params=_arb(), name="hgrn_fwd")(proj, proj, proj, proj, lower_bounds, gn_w, w_st, masks, ycat, *placed)
    return res[:4], res[4:]


def _token_local(x, ycat, tgt, g1, nw2, sc2, sh2, g2, fw, w_out_b, w_fi_b, w_fo_b, tm):
    T = x.shape[0]
    inv_d = 1.0 / D

    def body(x_ref, y_ref, t_ref, g1_ref, nw2_ref, sc2_ref, sh2_ref, g2_ref, fw_ref, wo_ref, wfi_ref, wfo_ref,
             dy_ref, dx1_ref, h2_ref, act_ref, dff_ref, dgu_ref, dmix_ref, acc_ref):
        @pl.when(pl.program_id(0) == 0)
        def _():
            acc_ref[...] = jnp.zeros_like(acc_ref)

        def acc(row, val):
            acc_ref[row:row + 1, :] += jnp.sum(val, axis=0, keepdims=True)

        g1v, g2v = g1_ref[...], g2_ref[...]
        mix = _dot(y_ref[...], wo_ref[...])
        x1 = x_ref[...] + g1v * mix
        r2 = _rms(x1)
        xh2 = x1 * r2
        n2 = xh2 * nw2_ref[...]
        osc2 = 1.0 + sc2_ref[...]
        h2b = (n2 * osc2 + sh2_ref[...]).astype(BF16)
        h2_ref[...] = h2b
        ff = jnp.zeros((tm, D), F32)
        saved = []
        for kb in range(DFF // FFB):
            gate = _dot(h2b, wfi_ref[:, kb * FFB:(kb + 1) * FFB])
            up = _dot(h2b, wfi_ref[:, DFF + kb * FFB:DFF + (kb + 1) * FFB])
            sg = _sigmoid(gate)
            actb = (gate * sg * up).astype(BF16)
            act_ref[:, kb * FFB:(kb + 1) * FFB] = actb
            ff = ff + _dot(actb, wfo_ref[kb * FFB:(kb + 1) * FFB, :])
            saved.append((gate, up, sg))
        x2 = x1 + g2v * ff
        r3 = _rms(x2)
        xh3 = x2 * r3
        err = xh3 * fw_ref[...] - t_ref[...]
        acc(6, (0.5 * inv_d) * err * err)
        dy = err * inv_d
        acc(4, dy * xh3)
        dx2 = _rms_bwd(xh3, r3, dy * fw_ref[...])
        acc(0, dx2 * ff)
        dffb = (dx2 * g2v).astype(BF16)
        dff_ref[...] = dffb
        dh2 = jnp.zeros((tm, D), F32)
        for kb in range(DFF // FFB):
            gate, up, sg = saved[kb]
            da = _dot(dffb, wfo_ref[kb * FFB:(kb + 1) * FFB, :], NT)
            dgate = (da * up * (sg * (1.0 + gate * (1.0 - sg)))).astype(BF16)
            dup = (da * gate * sg).astype(BF16)
            dgu_ref[:, kb * FFB:(kb + 1) * FFB] = dgate
            dgu_ref[:, DFF + kb * FFB:DFF + (kb + 1) * FFB] = dup
            dh2 = dh2 + _dot(dgate, wfi_ref[:, kb * FFB:(kb + 1) * FFB], NT)
            dh2 = dh2 + _dot(dup, wfi_ref[:, DFF + kb * FFB:DFF + (kb + 1) * FFB], NT)
        acc(2, dh2)
        acc(1, dh2 * n2)
        dn2 = dh2 * osc2
        acc(3, dn2 * xh2)
        dx1 = dx2 + _rms_bwd(xh2, r2, dn2 * nw2_ref[...])
        acc(5, dx1 * mix)
        dmixb = (dx1 * g1v).astype(BF16)
        dmix_ref[...] = dmixb
        dy_ref[...] = _dot(dmixb, wo_ref[...], NT)
        dx1_ref[...] = dx1

    row = lambda i: (i, 0)
    vec = _full((1, D))
    return pl.pallas_call(
        body, grid=(T // tm,),
        in_specs=[pl.BlockSpec((tm, D), row), pl.BlockSpec((tm, D), row), pl.BlockSpec((tm, D), row),
                  _ada_part(ADA_G1), vec, _ada_part(ADA_SC2), _ada_part(ADA_SH2), _ada_part(ADA_G2), vec,
                  _resident((D, D)), _resident((D, 2 * DFF)), _resident((DFF, D))],
        out_specs=[pl.BlockSpec((tm, D), row), pl.BlockSpec((tm, D), row), pl.BlockSpec((tm, D), row),
                   pl.BlockSpec((tm, DFF), row), pl.BlockSpec((tm, D), row), pl.BlockSpec((tm, 2 * DFF), row),
                   pl.BlockSpec((tm, D), row), _full((8, D))],
        out_shape=[SDS((T, D), F32), SDS((T, D), F32), SDS((T, D), BF16), SDS((T, DFF), BF16), SDS((T, D), BF16),
                   SDS((T, 2 * DFF), BF16), SDS((T, D), BF16), SDS((8, D), F32)],
        compiler_params=_arb(), name="token_local")(x, ycat, tgt, g1, nw2, sc2, sh2, g2, fw, w_out_b, w_fi_b, w_fo_b)


def _gmlp_bwd(proj, dycat, ws_b, bst, lnw, lnb, grads):
    T = proj.shape[0]
    rows = min(GMLP_ROWS_PER_STEP, T)
    nb = T // rows
    nw = len(grads)

    def body(*refs):
        u_ref, v_ref, dy_ref, ws_ref, bst_ref, lnw_ref, lnb_ref = refs[:7]
        dp_ref, dws_ref, dbs_ref, dln_ref = refs[7 + nw:11 + nw]
        dbs_acc, send_sems, recv_sems = refs[11 + 2 * nw:]
        exchange = _CoreExchange(refs[7:7 + nw], refs[11 + nw:11 + 2 * nw], send_sems, recv_sems)
        i = pl.program_id(0)

        @pl.when(i == 0)
        def _():
            exchange.start()
            dws_ref[...] = jnp.zeros_like(dws_ref)
            dln_ref[...] = jnp.zeros_like(dln_ref)
            dbs_acc[...] = jnp.zeros_like(dbs_acc)

        r = lax.broadcasted_iota(jnp.int32, (BLK, BLK), 0) // CH
        c = lax.broadcasted_iota(jnp.int32, (BLK, BLK), 1) // CH
        for bi in range(rows // BLK):
            rs = slice(bi * BLK, (bi + 1) * BLK)
            ug, dug, dvg, rstd, vhat, vnb, mixed = _gmlp_common(
                u_ref[rs, :], v_ref[rs, :], lnw_ref[...], lnb_ref[...], ws_ref, bst_ref)
            dya = dy_ref[rs, :]
            dp_ref[rs, 0:DG] = (dya * mixed * dug).astype(BF16)
            dmixed = dya * ug
            dbs_acc[...] += dmixed
            dmb = dmixed.astype(BF16)
            dvn = []
            for h in range(NH):
                sl = slice(h * HD, (h + 1) * HD)
                dws_ref[h * BLK:(h + 1) * BLK, :] += jnp.where(r >= c, _dot(dmb[:, sl], vnb[:, sl], NT), 0.0)
                dvn.append(_dot(ws_ref[h], dmb[:, sl], TN))
            dvn = jnp.concatenate(dvn, axis=1)
            dln_ref[0:1, :] += jnp.sum(dvn * vhat, axis=0, keepdims=True)
            dln_ref[1:2, :] += jnp.sum(dvn, axis=0, keepdims=True)
            dvh = dvn * lnw_ref[...]
            dvgel = rstd * (dvh - jnp.mean(dvh, axis=-1, keepdims=True) - vhat * jnp.mean(dvh * vhat, axis=-1, keepdims=True))
            dp_ref[rs, DG:2 * DG] = (dvgel * dvg).astype(BF16)

        @pl.when(i == nb - 1)
        def _():
            head = lax.broadcasted_iota(jnp.int32, (8, BLK), 0)
            ones = jnp.ones((8, HD), F32)
            out = jnp.zeros((8, BLK), F32)
            for h in range(NH):
                sums = _dot(ones, dbs_acc[:, h * HD:(h + 1) * HD], NT, precision=HIGHEST)
                out = out + jnp.where(head == h, sums, 0.0)
            dbs_ref[...] = out
            exchange.finish()

    anyspec = pl.BlockSpec(memory_space=pl.ANY)
    res = pl.pallas_call(
        body, grid=(nb,),
        in_specs=[pl.BlockSpec((rows, DG), lambda i: (i, 0)), pl.BlockSpec((rows, DG), lambda i: (i, 1)),
                  pl.BlockSpec((rows, DG), lambda i: (i, 0)),
                  _full((NH, BLK, BLK)), _full((BLK, NH)), _full((1, DG)), _full((1, DG))] + [anyspec] * nw,
        out_specs=[pl.BlockSpec((rows, 2 * DG), lambda i: (i, 2)), _full((NH * BLK, BLK)), _full((8, BLK)), _full((8, DG))]
        + [anyspec] * nw,
        out_shape=[SDS((T, DIN), BF16), SDS((NH * BLK, BLK), F32), SDS((8, BLK), F32), SDS((8, DG), F32)]
        + _core_exchange_shapes(grads),
        scratch_shapes=[pltpu.VMEM((BLK, DG), F32)] + _core_exchange_sems(nw),
        compiler_params=_arb(), name="gmlp_bwd")(proj, proj, dycat, ws_b, bst, lnw, lnb, *grads)
    return res[:4], res[4:]


def _hgrn_bwd(proj, o_pre, a_all, st_all, dycat, lower_bounds, gn_w, dproj, tables, sums):
    T = proj.shape[0]
    nc = T // CH
    nch = min(HGRN_CHUNKS_PER_STEP, nc)
    steps = nc // nch
    w_st, w_st_t, _, masks_sym = tables
    n_lev = len(LEVELS)
    nw = len(sums)

    def body(*refs):
        q_ref, f_ref, i_ref, g_ref, o_ref, a_ref, st_ref, dy_ref, lbp_ref, gn_ref, w_ref, wt_ref, ms_ref = refs[:13]
        dp_ref, dlb_ref, dgn_ref = refs[14 + nw:17 + nw]
        ds_scr, dx_scr, send_sems, recv_sems = refs[17 + 2 * nw:]
        exchange = _ChipExchange(refs[14:14 + nw], refs[17 + nw:17 + 2 * nw], send_sems, recv_sems)
        i = pl.program_id(0)

        @pl.when(i == 0)
        def _():
            exchange.start()
            ds_scr[...] = jnp.zeros_like(ds_scr)
            dlb_ref[...] = jnp.zeros_like(dlb_ref)
            dgn_ref[...] = jnp.zeros_like(dgn_ref)

        lb, omlb = _lower_bound(lbp_ref)
        row = lax.broadcasted_iota(jnp.int32, (CH, 1), 0)
        eye = lax.broadcasted_iota(jnp.int32, (CH, CH), 0) == lax.broadcasted_iota(jnp.int32, (CH, CH), 1)
        lower = lax.broadcasted_iota(jnp.int32, (CH, CH), 0) > lax.broadcasted_iota(jnp.int32, (CH, CH), 1)
        dgn = jnp.zeros((1, HD), F32)
        pre = []
        for ci in range(nch):
            rs = slice(ci * CH, (ci + 1) * CH)
            q = q_ref[rs, :]
            v = i_ref[rs, :]
            g = g_ref[rs, :]
            sq, qf, sig, f, k, e = _hgrn_gates(q, f_ref[rs, :], lb, omlb, w_ref)
            eb = e[0]
            ekd = e[1]
            kd = k * ekd
            qe = qf * eb
            dob_h, dqe_h, dqf_h, dki_h, dv_h, dg_h = [], [], [], [], [], []
            for h in range(NH):
                sl = slice(h * HD, (h + 1) * HD)
                o = o_ref[rs, sl]
                ro = _rms(o)
                oh = o * ro
                gh = g[:, sl]
                sg = _sigmoid(gh)
                dyb = dy_ref[rs, sl]
                dg_h.append(dyb * (oh * gn_ref[...]) * (sg * (1.0 + gh * (1.0 - sg))))
                don = dyb * (gh * sg)
                dgn = dgn + jnp.sum(don * oh, axis=0, keepdims=True)
                dob = _rms_bwd(oh, ro, don * gn_ref[...]).astype(BF16)
                vb = v[:, sl].astype(BF16)
                qh, kh = qf[:, sl], k[:, sl]
                dqe = _dot(dob, st_ref[ci, h].astype(BF16))
                da = _dot(dob, vb, NT)
                ddiag = jnp.sum(jnp.where(eye, da, 0.0), axis=-1, keepdims=True)
                dsym = jnp.where(lower, da, _dot(vb, dob, NT))
                upper_part = jnp.zeros((CH, HD), F32)
                both = jnp.zeros((CH, HD), F32)
                for li in range(n_lev):
                    el, up, y = _level_factor(e, li, sl, row, qh, kh)
                    dyv = _dot((ms_ref[li] * dsym).astype(BF16), y.astype(BF16))
                    dx_scr[ci, (2 + li) * CH:(3 + li) * CH, sl] = dyv * y
                    dye = dyv * el
                    upper_part = upper_part + jnp.where(up, dye, 0.0)
                    both = both + dye
                dob_h.append(dob)
                dqe_h.append(dqe)
                dqf_h.append(dqe * eb[:, sl] + ddiag * kh + upper_part)
                dki_h.append(ddiag * qh + (both - upper_part))
                dv_h.append(_dot(a_ref[ci, h].astype(BF16), dob, TN))
            dp_ref[rs, 0:DH] = (jnp.concatenate(dqf_h, axis=1) * (sq * (1.0 + q * (1.0 - sq)))).astype(BF16)
            dp_ref[rs, 3 * DH:4 * DH] = jnp.concatenate(dg_h, axis=1).astype(BF16)
            pre.append((v, sig, f, eb, ekd, kd, qe, dob_h, jnp.concatenate(dqe_h, axis=1), dki_h, dv_h))
        dgn_ref[0:1, :] += dgn
        for ci in reversed(range(nch)):
            rs = slice(ci * CH, (ci + 1) * CH)
            v, sig, f, eb, ekd, kd, qe, dob_h, dqe, dki_h, dv_h = pre[ci]
            ebl = eb[CH - 1:CH, :]
            dbl_h, dkd_h, dv2_h = [], [], []
            for h in range(NH):
                sl = slice(h * HD, (h + 1) * HD)
                dst1 = ds_scr[h]
                dst1b = dst1.astype(BF16)
                ds_scr[h] = dst1 * ebl[:, sl] + _dot(dob_h[h], qe[:, sl].astype(BF16), TN)
                dbl_h.append(ebl[:, sl] * jnp.sum(st_ref[ci, h] * dst1, axis=0, keepdims=True))
                dkd_h.append(_dot(v[:, sl].astype(BF16), dst1b))
                dv2_h.append(dv_h[h] + _dot(kd[:, sl].astype(BF16), dst1b, NT))
            dkd = jnp.concatenate(dkd_h, axis=1)
            dx_scr[ci, 0:CH, :] = dqe * qe + jnp.where(row == CH - 1, jnp.concatenate(dbl_h, axis=1), 0.0)
            dx_scr[ci, CH:2 * CH, :] = dkd * kd
            dlf = _split_dot(wt_ref[...], dx_scr[ci], 2)
            df = dlf / f - (dkd * ekd + jnp.concatenate(dki_h, axis=1))
            dlb_ref[0:1, :] += jnp.sum(df * (1.0 - sig), axis=0, keepdims=True)
            dp_ref[rs, DH:2 * DH] = (df * omlb * sig * (1.0 - sig)).astype(BF16)
            dp_ref[rs, 2 * DH:3 * DH] = jnp.concatenate(dv2_h, axis=1).astype(BF16)

        @pl.when(i == steps - 1)
        def _():
            gl = dlb_ref[0:1, :] * lb * omlb
            dlb_ref[0:1, :] = gl
            dlb_ref[1:2, :] = -gl
            exchange.finish()

    rev = lambda j: pl.BlockSpec((nch * CH, DH), lambda c: (steps - 1 - c, j))
    anyspec = pl.BlockSpec(memory_space=pl.ANY)
    res = pl.pallas_call(
        body, grid=(steps,),
        in_specs=[rev(2), rev(3), rev(4), rev(5), rev(0),
                  pl.BlockSpec((nch, NH, CH, CH), lambda c: (steps - 1 - c, 0, 0, 0)),
                  pl.BlockSpec((nch, NH, HD, HD), lambda c: (steps - 1 - c, 0, 0, 0)),
                  rev(1), _full((2, DH)), _full((1, HD)),
                  _full(w_st.shape), _full(w_st_t.shape), _full(masks_sym.shape),
                  anyspec] + [anyspec] * nw,
        out_specs=[pl.BlockSpec((nch * CH, 4 * DH), lambda c: (steps - 1 - c, 0)), _full((8, DH)), _full((8, HD))]
        + [anyspec] * nw,
        out_shape=[SDS((T, DIN), BF16), SDS((8, DH), F32), SDS((8, HD), F32)] + _slot_shapes(sums),
        scratch_shapes=[pltpu.VMEM((NH, HD, HD), F32), pltpu.VMEM((nch, (2 + n_lev) * CH, DH), F32)] + _exchange_sems(nw),
        input_output_aliases={13: 0},
        compiler_params=_arb(), name="hgrn_bwd")(proj, proj, proj, proj, o_pre, a_all, st_all, dycat, lower_bounds, gn_w,
                                                 w_st, w_st_t, masks_sym, dproj, *sums)
    return res[:3], res[3:]


def _proj_in_bwd(dproj, x, dx1, nw, sc, w_in_b, tm, sums):
    T = x.shape[0]
    ns = len(sums)
    steps = T // tm

    def body(*refs):
        dp_ref, x_ref, dx1_ref, nw_ref, sc_ref, w_ref = refs[:6]
        gx_ref, acc_ref = refs[6 + ns:8 + ns]
        exchange = _ChipExchange(refs[6:6 + ns], refs[8 + ns:8 + 2 * ns], *refs[8 + 2 * ns:])

        @pl.when(pl.program_id(0) == 0)
        def _():
            exchange.start()
            acc_ref[...] = jnp.zeros_like(acc_ref)

        dh = _dot(dp_ref[:, 0:4 * DH], w_ref[:, 2 * DG:DIN], NT) + _dot(dp_ref[:, 4 * DH:DIN], w_ref[:, 0:2 * DG], NT)
        xv = x_ref[...]
        r = _rms(xv)
        xh = xv * r
        n1 = xh * nw_ref[...]
        acc_ref[0:1, :] += jnp.sum(dh, axis=0, keepdims=True)
        acc_ref[1:2, :] += jnp.sum(dh * n1, axis=0, keepdims=True)
        dn = dh * (1.0 + sc_ref[...])
        acc_ref[2:3, :] += jnp.sum(dn * xh, axis=0, keepdims=True)
        gx_ref[...] = dx1_ref[...] + _rms_bwd(xh, r, dn * nw_ref[...])

        @pl.when(pl.program_id(0) == steps - 1)
        def _():
            exchange.finish()

    row = lambda i: (i, 0)
    anyspec = pl.BlockSpec(memory_space=pl.ANY)
    res = pl.pallas_call(
        body, grid=(steps,),
        in_specs=[pl.BlockSpec((tm, DIN), row), pl.BlockSpec((tm, D), row), pl.BlockSpec((tm, D), row),
                  _full((1, D)), _ada_part(ADA_SC1), _resident((D, DIN))] + [anyspec] * ns,
        out_specs=[pl.BlockSpec((tm, D), row), _full((8, D))] + [anyspec] * ns,
        out_shape=[SDS((T, D), F32), SDS((8, D), F32)] + _slot_shapes(sums),
        scratch_shapes=_exchange_sems(ns),
        compiler_params=_arb(), name="proj_in_bwd")(dproj, x, dx1, nw, sc, w_in_b, *sums)
    return res[:2], res[2:]


def _wgrad(a, b, bk, bn, tt, name):
    T, K = a.shape
    N = b.shape[1]
    nn, nk, nt = N // bn, K // bk, T // tt
    bmap = lambda n, k, t: (t, n)

    def body(a_ref, b_ref, o_ref):
        @pl.when(pl.program_id(2) == 0)
        def _():
            o_ref[...] = jnp.zeros_like(o_ref)

        o_ref[0] += _dot(a_ref[...], b_ref[...], TN)

    return pl.pallas_call(
        body, grid=(nn, nk, nt),
        in_specs=[pl.BlockSpec((tt, bk), lambda n, k, t: (t, k)), pl.BlockSpec((tt, bn), bmap)],
        out_specs=pl.BlockSpec((1, bk, bn), lambda n, k, t: (n, k, 0)),
        out_shape=SDS((nn, K, bn), F32),
        compiler_params=_arb(3), name=name)(a, b)


def _adam_math(w, g, m, v):
    m = B1 * m + (1.0 - B1) * g
    v = B2 * v + (1.0 - B2) * (g * g)
    m_hat = m / (1.0 - B1 ** STEP)
    v_hat = v / (1.0 - B2 ** STEP)
    return -LR * (m_hat / (jnp.sqrt(v_hat) + AEPS) + WD * w), m, v


def _adamw_halves(w, mine, sibling, m, v, c_idx, rb, name):
    R, C = w.shape
    nb = (R // 2) // rb

    def body(c_ref, w_ref, a_ref, b_ref, m_ref, v_ref, g_out, d_out, m_out, v_out):
        g = jnp.where(pl.program_id(0) == c_ref[0], a_ref[...], b_ref[...])
        g_out[...] = g
        d_out[...], m_out[...], v_out[...] = _adam_math(w_ref[...], g, m_ref[...], v_ref[...])

    whole = pl.BlockSpec((rb, C), lambda hh, i, cr: (hh * nb + i, 0))
    half = pl.BlockSpec((rb, C), lambda hh, i, cr: (i, 0))
    return pl.pallas_call(
        body,
        grid_spec=pltpu.PrefetchScalarGridSpec(
            num_scalar_prefetch=1, grid=(2, nb), in_specs=[whole, half, half, whole, whole], out_specs=[whole] * 4),
        out_shape=[SDS((R, C), F32)] * 4, compiler_params=_arb(2), name=name)(c_idx, w, mine, sibling, m, v)


def _ada_forward(c_all, w_ada):
    n = w_ada.shape[1]

    def body(c_ref, w_ref, ca_ref, p_ref):
        cv = c_ref[...]
        ca = cv * _sigmoid(cv)
        ca_ref[...] = ca
        p_ref[...] = _dot(ca, w_ref[...], precision=HIGHEST)

    return pl.pallas_call(
        body, grid=(n // 512,),
        in_specs=[_full((N_DEV, D)), pl.BlockSpec((D, 512), lambda i: (0, i))],
        out_specs=[_full((N_DEV, D)), pl.BlockSpec((N_DEV, 512), lambda i: (0, i))],
        out_shape=[SDS((N_DEV, D), F32), SDS((N_DEV, n), F32)],
        compiler_params=_arb(), name="ada_forward")(c_all, w_ada)


def _ada_wgrad_adam(cact_t, dada_all, w, m, v, chip_idx):
    R, C = w.shape
    rb = 256

    def body(j_ref, c_ref, d_ref, w_ref, m_ref, v_ref, g_out, d_out, m_out, v_out):
        g = _dot(c_ref[...], d_ref[...], precision=HIGHEST)
        g_out[...] = g
        d_out[...], m_out[...], v_out[...] = _adam_math(w_ref[...], g, m_ref[...], v_ref[...])

    spec = pl.BlockSpec((rb, C), lambda i, j: (i, 0))
    return pl.pallas_call(
        body,
        grid_spec=pltpu.PrefetchScalarGridSpec(
            num_scalar_prefetch=1, grid=(R // rb,),
            in_specs=[pl.BlockSpec((rb, N_DEV), lambda i, j: (i, 0)), pl.BlockSpec((N_DEV, C), lambda i, j: (0, j[0])),
                      spec, spec, spec],
            out_specs=[spec] * 4),
        out_shape=[SDS((R, C), F32)] * 4,
        compiler_params=_arb(), name="ada_wgrad_adam")(chip_idx, cact_t, dada_all, w, m, v)


SMALL_NAMES = ('b_ada', 'norm1_w', 'norm2_w', 'final_norm_w', 'v_ln_w', 'v_ln_b', 'lower_bounds', 'gn_w', 'b_s', 'w_s')


def _small_finalize(gathered, params, moms, vels):
    n_in = len(gathered)

    def body(*refs):
        acc1, acc2, dln, dlb, dgn, dbs, dws = refs[:n_in]
        prm = [dict(zip(SMALL_NAMES, refs[n_in + k * 10:n_in + (k + 1) * 10])) for k in range(3)]
        outs = [dict(zip(SMALL_NAMES, refs[n_in + 30 + k * 10:n_in + 30 + (k + 1) * 10])) for k in range(4)]
        loss_ref, dada_ref = refs[n_in + 70:n_in + 72]

        def dev_sum(ref, first, n):
            per = ref.shape[0] // N_DEV
            g = ref[first:first + n, :]
            for dev in range(1, N_DEV):
                g = g + ref[dev * per + first:dev * per + first + n, :]
            return g

        def update(n, g, cols=slice(None)):
            outs[0][n][:, cols] = g
            outs[1][n][:, cols], outs[2][n][:, cols], outs[3][n][:, cols] = _adam_math(
                prm[0][n][:, cols], g, prm[1][n][:, cols], prm[2][n][:, cols])

        ada_rows = ((acc1, 0), (acc1, 1), (acc2, 5), (acc2, 2), (acc2, 1), (acc2, 0))
        for k, (ref, r) in enumerate(ada_rows):
            update('b_ada', dev_sum(ref, r, 1), slice(k * D, (k + 1) * D))
            for dev in range(N_DEV):
                dada_ref[dev:dev + 1, k * D:(k + 1) * D] = ref[8 * dev + r:8 * dev + r + 1, :]
        update('norm1_w', dev_sum(acc1, 2, 1))
        update('norm2_w', dev_sum(acc2, 3, 1))
        update('final_norm_w', dev_sum(acc2, 4, 1))
        update('v_ln_w', dev_sum(dln, 0, 1))
        update('v_ln_b', dev_sum(dln, 1, 1))
        update('lower_bounds', dev_sum(dlb, 0, 2))
        update('gn_w', dev_sum(dgn, 0, 1))
        update('b_s', dev_sum(dbs, 0, NH))
        update('w_s', dev_sum(dws, 0, NH * BLK))
        loss_ref[...] = jnp.sum(dev_sum(acc2, 6, 1), axis=-1, keepdims=True)

    shapes = [SDS(params[n].shape, F32) for n in SMALL_NAMES]
    res = pl.pallas_call(
        body, out_shape=shapes * 4 + [SDS((1, 1), F32), SDS((N_DEV, 6 * D), F32)], name="small_finalize")(
            *gathered, *[d[n] for d in (params, moms, vels) for n in SMALL_NAMES])
    return [dict(zip(SMALL_NAMES, res[k * 10:(k + 1) * 10])) for k in range(4)], res[40], res[41]


def _position():
    x, y, c = lax.axis_index("x"), lax.axis_index("y"), lax.axis_index("c")
    return x, y, c


def _chip_at(x, y, r):
    return (x ^ (r >> 1), y ^ (r & 1))


def _gather_rows(ins, outs, send_sems, recv_sems, local_sems, after_issue=None):
    nb = len(ins)
    x, y, c = _position()
    me, sibling = (x, y, c), (x, y, 1 - c)
    chips = [_chip_at(x, y, r) for r in (1, 2, 3)]

    def rows(b, px, py, pc):
        m_per = ins[b].shape[0]
        return outs[b].at[pl.ds((4 * px + 2 * py + pc) * m_per, m_per), :]

    def copy(b, k, blk, to, src=None):
        return pltpu.make_async_remote_copy(
            src_ref=rows(b, *blk) if src is None else src, dst_ref=rows(b, *blk),
            send_sem=send_sems.at[7 * b + k], recv_sem=recv_sems.at[7 * b + k], device_id=to, device_id_type=MESH)

    local, sent = [], []
    for b in range(nb):
        mine = pltpu.make_async_copy(ins[b], rows(b, *me), local_sems.at[b])
        mine.start()
        local.append(mine)
        first = [copy(b, 0, me, sibling, src=ins[b])]
        first += [copy(b, 1 + j, me, (*chip, c), src=ins[b]) for j, chip in enumerate(chips)]
        for cp in first:
            cp.start()
        sent += first
    if after_issue is not None:
        after_issue()
    for b in range(nb):
        for j, chip in enumerate(chips):
            copy(b, 1 + j, (*chip, c), me).wait_recv()
            passed = copy(b, 4 + j, (*chip, c), sibling)
            passed.start()
            sent.append(passed)
    for b in range(nb):
        copy(b, 0, sibling, me).wait_recv()
        for j, chip in enumerate(chips):
            copy(b, 4 + j, (*chip, 1 - c), me).wait_recv()
    for cp in sent:
        cp.wait_send()
    for cp in local:
        cp.wait()


def _gather_rows_shapes(blocks):
    return [SDS((N_DEV * b.shape[0], b.shape[1]), b.dtype) for b in blocks]


def _gather_rows_sems(nb):
    return [pltpu.SemaphoreType.DMA((7 * nb,)), pltpu.SemaphoreType.DMA((7 * nb,)), pltpu.SemaphoreType.DMA((nb,))]


def _all_gather_rows(blocks, name):
    nb = len(blocks)

    def body(*refs):
        _gather_rows(refs[:nb], refs[nb:2 * nb], *refs[2 * nb:])

    vmem = pl.BlockSpec(memory_space=pltpu.VMEM)
    return pl.pallas_call(
        body, out_shape=_gather_rows_shapes(blocks), in_specs=[vmem] * nb, out_specs=[vmem] * nb,
        scratch_shapes=_gather_rows_sems(nb), name=name)(*blocks)


def _place_shard(w_shard, axis, chip_idx, name):
    R, C = w_shard.shape
    rb = _row_block(R)
    nb = R // rb
    full = (R * N_CHIPS, C) if axis == 0 else (R, C * N_CHIPS)
    omap = (lambda i, j: (j[0] * nb + i, 0)) if axis == 0 else (lambda i, j: (i, j[0]))

    def body(j_ref, w_ref, o_ref):
        o_ref[...] = w_ref[...].astype(BF16)

    return pl.pallas_call(
        body,
        grid_spec=pltpu.PrefetchScalarGridSpec(
            num_scalar_prefetch=1, grid=(nb,), in_specs=[pl.BlockSpec((rb, C), lambda i, j: (i, 0))],
            out_specs=pl.BlockSpec((rb, C), omap)),
        out_shape=SDS(full, BF16), compiler_params=_arb(), name=name)(chip_idx, w_shard)


class _WeightGather:
    def __init__(self, refs, axes, send_sems, recv_sems):
        self.refs, self.axes, self.send_sems, self.recv_sems = refs, axes, send_sems, recv_sems
        self.x, self.y, self.c = _position()
        self.j = 2 * self.x + self.y
        self.n = 3 * len(refs)

    def _half(self, w, chip_idx, half):
        ref, axis = self.refs[w], self.axes[w]
        if axis == 0:
            size = ref.shape[0] // N_CHIPS
            return ref.at[pl.ds(chip_idx * size + half * (size // 2), size // 2), :]
        size = ref.shape[1] // N_CHIPS
        rows = ref.shape[0] // 2
        return ref.at[pl.ds(half * rows, rows), pl.ds(chip_idx * size, size)]

    def _ici(self, w, r, chip_idx):
        k = 3 * w + r - 1
        piece = self._half(w, chip_idx, self.c)
        return pltpu.make_async_remote_copy(
            src_ref=piece, dst_ref=piece, send_sem=self.send_sems.at[k], recv_sem=self.recv_sems.at[k],
            device_id=(*_chip_at(self.x, self.y, r), self.c), device_id_type=MESH)

    def _d2d(self, w, r, half):
        k = self.n + 3 * w + r - 1
        piece = self._half(w, self.j ^ r, half)
        return pltpu.make_async_remote_copy(
            src_ref=piece, dst_ref=piece, send_sem=self.send_sems.at[k], recv_sem=self.recv_sems.at[k],
            device_id=(self.x, self.y, 1 - self.c), device_id_type=MESH)

    def _each(self):
        return [(w, r) for w in range(len(self.refs)) for r in (1, 2, 3)]

    def start(self):
        for w, r in self._each():
            self._ici(w, r, self.j).start()

    def forward(self):
        for w, r in self._each():
            self._ici(w, r, self.j ^ r).wait_recv()
            self._d2d(w, r, self.c).start()

    def finish(self):
        for w, r in self._each():
            self._ici(w, r, self.j).wait_send()
            self._d2d(w, r, self.c).wait_send()
            self._d2d(w, r, 1 - self.c).wait_recv()


def _gather_sems(n_weights):
    return [pltpu.SemaphoreType.DMA((6 * n_weights,)), pltpu.SemaphoreType.DMA((6 * n_weights,))]


def _gather_weights(placed, axes, row_blocks, name):
    nw, nb = len(placed), len(row_blocks)

    def body(*refs):
        w_outs, b_ins, b_outs = refs[nw + nb:2 * nw + nb], refs[nw:nw + nb], refs[2 * nw + nb:2 * (nw + nb)]
        sems = refs[2 * (nw + nb):]
        g = _WeightGather(w_outs, axes, *sems[:2])
        _gather_rows(b_ins, b_outs, *sems[2:], after_issue=g.start)
        g.forward()
        g.finish()

    anyspec = pl.BlockSpec(memory_space=pl.ANY)
    vmem = pl.BlockSpec(memory_space=pltpu.VMEM)
    res = pl.pallas_call(
        body, out_shape=[SDS(a.shape, a.dtype) for a in placed] + _gather_rows_shapes(row_blocks),
        in_specs=[anyspec] * nw + [vmem] * nb, out_specs=[anyspec] * nw + [vmem] * nb,
        scratch_shapes=_gather_sems(nw) + _gather_rows_sems(nb), input_output_aliases={i: i for i in range(nw)},
        name=name)(*placed, *row_blocks)
    return res[:nw], res[nw:]


class _ChipExchange:
    def __init__(self, ins, outs, send_sems, recv_sems):
        self.ins, self.outs, self.send_sems, self.recv_sems = ins, outs, send_sems, recv_sems
        self.x, self.y, self.c = _position()
        self.j = 2 * self.x + self.y

    def _copies(self):
        for w in range(len(self.ins)):
            for r in (1, 2, 3):
                k = 3 * w + r - 1
                yield pltpu.make_async_remote_copy(
                    src_ref=self.ins[w].at[self.j ^ r], dst_ref=self.outs[w].at[r - 1],
                    send_sem=self.send_sems.at[k], recv_sem=self.recv_sems.at[k],
                    device_id=(*_chip_at(self.x, self.y, r), self.c), device_id_type=MESH)

    def start(self):
        for cp in self._copies():
            cp.start()

    def finish(self):
        for cp in self._copies():
            cp.wait()


def _exchange_sems(n_weights):
    return [pltpu.SemaphoreType.DMA((3 * n_weights,)), pltpu.SemaphoreType.DMA((3 * n_weights,))]


class _CoreExchange:
    def __init__(self, ins, outs, send_sems, recv_sems):
        self.ins, self.outs, self.send_sems, self.recv_sems = ins, outs, send_sems, recv_sems
        self.x, self.y, self.c = _position()

    def _copies(self):
        for w in range(len(self.ins)):
            yield pltpu.make_async_remote_copy(
                src_ref=self.ins[w].at[:, 1 - self.c], dst_ref=self.outs[w],
                send_sem=self.send_sems.at[w], recv_sem=self.recv_sems.at[w],
                device_id=(self.x, self.y, 1 - self.c), device_id_type=MESH)

    def start(self):
        for cp in self._copies():
            cp.start()

    def finish(self):
        for cp in self._copies():
            cp.wait()


def _core_exchange_shapes(grads):
    return [SDS((g.shape[0], g.shape[2], g.shape[3]), F32) for g in grads]


def _core_exchange_sems(n):
    return [pltpu.SemaphoreType.DMA((n,)), pltpu.SemaphoreType.DMA((n,))]


def _exchange_core_halves(grads, name):
    nw = len(grads)

    def body(*refs):
        ex = _CoreExchange(refs[:nw], refs[nw:2 * nw], *refs[2 * nw:])
        ex.start()
        ex.finish()

    anyspec = pl.BlockSpec(memory_space=pl.ANY)
    return pl.pallas_call(
        body, out_shape=_core_exchange_shapes(grads), in_specs=[anyspec] * nw, out_specs=[anyspec] * nw,
        scratch_shapes=_core_exchange_sems(nw), name=name)(*grads)


def _add_core_halves(g4, recv, c_idx, rb, name):
    ns, _, rh, C = g4.shape

    def body(c_ref, g_ref, r_ref, o_ref):
        o_ref[...] = (g_ref[0] + r_ref[...]).astype(BF16)

    return pl.pallas_call(
        body,
        grid_spec=pltpu.PrefetchScalarGridSpec(
            num_scalar_prefetch=1, grid=(ns, rh // rb),
            in_specs=[pl.BlockSpec((1, 1, rb, C), lambda s, i, cr: (s, cr[0], i, 0)),
                      pl.BlockSpec((1, rb, C), lambda s, i, cr: (s, i, 0))],
            out_specs=pl.BlockSpec((1, rb, C), lambda s, i, cr: (s, i, 0))),
        out_shape=SDS((ns, rh, C), BF16), compiler_params=_arb(2), name=name)(c_idx, g4, recv)


def _add_core_halves_in(g4, recv, c_idx, name):
    n_slabs, _, rh, C = g4.shape
    cb = 256
    per_slab, per_chip, n_blocks = C // cb, DIN // N_CHIPS // cb, DIN // cb

    def stored(s, k):
        sb = (per_chip * s + k + 4 * DH // cb) % n_blocks
        return sb // per_slab, sb % per_slab

    def body(c_ref, g_ref, r_ref, o_ref):
        o_ref[...] = (g_ref[0] + r_ref[...]).astype(BF16)

    return pl.pallas_call(
        body,
        grid_spec=pltpu.PrefetchScalarGridSpec(
            num_scalar_prefetch=1, grid=(N_CHIPS, per_chip),
            in_specs=[pl.BlockSpec((1, 1, rh, cb), lambda s, k, cr: (stored(s, k)[0], cr[0], 0, stored(s, k)[1])),
                      pl.BlockSpec((1, rh, cb), lambda s, k, cr: (stored(s, k)[0], 0, stored(s, k)[1]))],
            out_specs=pl.BlockSpec((1, rh, cb), lambda s, k, cr: (s, 0, k))),
        out_shape=SDS((N_CHIPS, rh, DIN // N_CHIPS), BF16), compiler_params=_arb(2), name=name)(c_idx, g4, recv)


def _slot_shapes(sums):
    return [SDS((3,) + s.shape[1:], s.dtype) for s in sums]


def _add_chips(own, slots, order, rb, name):
    _, rh, C = slots.shape

    def body(o_ref, own_ref, a_ref, b_ref, c_ref, d_ref, out_ref):
        mine = own_ref[0].astype(F32)
        t = [jnp.where(o_ref[i] == 0, mine, r[0].astype(F32)) for i, r in enumerate((a_ref, b_ref, c_ref, d_ref))]
        out_ref[...] = ((t[0] + t[1]) + t[2]) + t[3]

    def spec(i):
        return pl.BlockSpec((1, rb, C), lambda t, o: (jnp.maximum(o[i], 1) - 1, t, 0))

    return pl.pallas_call(
        body,
        grid_spec=pltpu.PrefetchScalarGridSpec(
            num_scalar_prefetch=1, grid=(rh // rb,),
            in_specs=[pl.BlockSpec((1, rb, C), lambda t, o: (o[4], t, 0)), spec(0), spec(1), spec(2), spec(3)],
            out_specs=pl.BlockSpec((rb, C), lambda t, o: (t, 0))),
        out_shape=SDS((rh, C), F32), compiler_params=_arb(), name=name)(order, own, slots, slots, slots, slots)


def _share_halves(halves):
    nw = len(halves)

    def body(*refs):
        ins, outs = refs[:nw], refs[nw:2 * nw]
        send_sems, recv_sems = refs[2 * nw:]
        x, y, c = _position()
        started = []
        for w in range(nw):
            cp = pltpu.make_async_remote_copy(
                src_ref=ins[w], dst_ref=outs[w], send_sem=send_sems.at[w], recv_sem=recv_sems.at[w],
                device_id=(x, y, 1 - c), device_id_type=MESH)
            cp.start()
            started.append(cp)
        for cp in started:
            cp.wait()

    anyspec = pl.BlockSpec(memory_space=pl.ANY)
    return pl.pallas_call(
        body, out_shape=[SDS(h.shape, F32) for h in halves], in_specs=[anyspec] * nw, out_specs=[anyspec] * nw,
        scratch_shapes=[pltpu.SemaphoreType.DMA((nw,)), pltpu.SemaphoreType.DMA((nw,))],
        name="share_halves")(*halves)


def _small_2d(b_ada, norm1_w, norm2_w, final_norm_w, v_ln_w, v_ln_b, lower_bounds, gn_w, b_s, w_s):
    return dict(zip(SMALL_NAMES, (b_ada, norm1_w, norm2_w, final_norm_w.reshape(1, D), v_ln_w, v_ln_b, lower_bounds, gn_w,
                                  b_s.reshape(NH, BLK), w_s.reshape(NH * BLK, BLK))))


def _small_original_shapes(d):
    out = dict(d)
    out['final_norm_w'] = d['final_norm_w'].reshape(D)
    out['b_s'] = d['b_s'].reshape(1, NH, BLK)
    out['w_s'] = d['w_s'].reshape(1, NH, BLK, BLK)
    return out


def _row_block(r):
    for cand in (256, 176, 128, 64, 32, 16, 8):
        if r % cand == 0:
            return cand
    return r


def kernel(x, c, w_ada, b_ada, norm1_w, w_in, w_s, b_s, v_ln_w, v_ln_b, lower_bounds, gn_w, w_out, norm2_w, w_ffn_in, w_ffn_out, final_norm_w, loss_target, m_w_ada, m_b_ada, m_norm1_w, m_w_in, m_w_s, m_b_s, m_v_ln_w, m_v_ln_b, m_lower_bounds, m_gn_w, m_w_out, m_norm2_w, m_w_ffn_in, m_w_ffn_out, m_final_norm_w, v_w_ada, v_b_ada, v_norm1_w, v_w_in, v_w_s, v_b_s, v_v_ln_w, v_v_ln_b, v_lower_bounds, v_gn_w, v_w_out, v_norm2_w, v_w_ffn_in, v_w_ffn_out, v_final_norm_w):
    T = x.shape[1]
    tm, tp = min(TOKEN_TILE, T), min(PROJ_TILE, T)
    px, py, pc = _position()
    chip = 2 * px + py
    me = 4 * px + 2 * py + pc
    x2d = x.reshape(T, D)
    tgt = loss_target.reshape(T, D)

    chip_idx = jnp.reshape(chip, (1,)).astype(jnp.int32)
    c_idx = jnp.reshape(pc, (1,)).astype(jnp.int32)
    (w_in_b,), (c_all,) = _gather_weights(
        [_place_shard(w_in[0], 1, chip_idx, "place_in")], [1], [jnp.broadcast_to(c, (8, D))], "gather_w_in_and_c")
    placed = [_place_shard(w_out[0], 0, chip_idx, "place_out"), _place_shard(w_ffn_in[0], 1, chip_idx, "place_ffn_in"),
              _place_shard(w_ffn_out[0], 0, chip_idx, "place_ffn_out")]

    cact, ada_part = _ada_forward(c_all.reshape(N_DEV, 8, D)[:, 0, :], w_ada[0])
    n_ada = ada_part.shape[1]
    (ada_all,) = _all_gather_rows([ada_part], "gather_ada")
    ada_all = ada_all.reshape(N_CHIPS, 2, N_DEV, n_ada)[:, 0]
    ada = lax.dynamic_index_in_dim(ada_all, me, axis=1, keepdims=False).reshape(1, 6 * D) + b_ada

    rr = lax.broadcasted_iota(jnp.int32, (BLK, BLK), 0) // CH
    cc = lax.broadcasted_iota(jnp.int32, (BLK, BLK), 1) // CH
    ws_b = jnp.where((rr >= cc)[None], w_s[0], 0.0).astype(BF16)
    bst = b_s[0].T
    lnw, lnb = v_ln_w, v_ln_b
    nw1, nw2, fw = norm1_w, norm2_w, final_norm_w.reshape(1, D)

    h1, proj = _proj_in(x2d, nw1, ada, ada, w_in_b, tp)
    ycat = _gmlp_fwd(proj, ws_b, bst, lnw, lnb)
    tables = _hgrn_tables()
    (ycat, o_pre, a_all, st_all), (w_out_b, w_fi_b, w_fo_b) = _hgrn_fwd(
        proj, lower_bounds, gn_w, ycat, tables, placed, [0, 1, 0])

    dycat, dx1, h2, act, dff, dgu, dmix, acc2 = _token_local(
        x2d, ycat, tgt, ada, nw2, ada, ada, ada, fw, w_out_b, w_fi_b, w_fo_b, tm)

    tt = min(WGRAD_TOKENS, T)
    order = jnp.concatenate([chip ^ jnp.arange(N_CHIPS, dtype=jnp.int32), chip_idx]).astype(jnp.int32)

    def by_core_half(g):
        return g.reshape(g.shape[0], 2, g.shape[1] // 2, g.shape[2])

    def core_sums(g4, recv, names):
        return [_add_core_halves(a, b, c_idx, _row_block(a.shape[2]), "add_core_" + n) for a, b, n in zip(g4, recv, names)]

    def chip_sums(sums, slots, names):
        return [_add_chips(o, s, order, _row_block(s.shape[1]), "add_chips_" + n) for o, s, n in zip(sums, slots, names)]

    g_out = _wgrad(ycat, dmix, D, D, tt, "wgrad_out").reshape(N_CHIPS, D // N_CHIPS, D)
    g_fi = _wgrad(h2, dgu, D, FFB, tt, "wgrad_ffn_in")
    g_fo = _wgrad(act, dff, FFB, D, tt, "wgrad_ffn_out").reshape(N_CHIPS, DFF // N_CHIPS, D)
    late_names = ["out", "ffn_in", "ffn_out"]
    late_g4 = [by_core_half(g) for g in (g_out, g_fi, g_fo)]

    (dproj, dws, dbs, dln), late_recv = _gmlp_bwd(proj, dycat, ws_b, bst, lnw, lnb, late_g4)
    late_sums = core_sums(late_g4, late_recv, late_names)
    (dproj, dlb, dgn), late_slots = _hgrn_bwd(
        proj, o_pre, a_all, st_all, dycat, lower_bounds, gn_w, dproj, tables, late_sums)

    g_in = _wgrad(h1, dproj, D, D, tt, "wgrad_in")
    in_g4 = [by_core_half(g_in)]
    (in_recv,) = _exchange_core_halves(in_g4, "exchange_core_halves_in")
    in_sums = [_add_core_halves_in(in_g4[0], in_recv, c_idx, "add_core_in")]
    (grad_x, acc1), in_slots = _proj_in_bwd(dproj, x2d, dx1, nw1, ada, w_in_b, tp, in_sums)
    names = ["in"] + late_names
    halves = chip_sums(in_sums, in_slots, ["in"]) + chip_sums(late_sums, late_slots, late_names)
    sibling_halves = _share_halves(halves)

    big_w = [(w_in, m_w_in, v_w_in), (w_out, m_w_out, v_w_out), (w_ffn_in, m_w_ffn_in, v_w_ffn_in),
             (w_ffn_out, m_w_ffn_out, v_w_ffn_out)]
    big_out = []
    for mine, sib, (w, m, v), n in zip(halves, sibling_halves, big_w, names):
        res = _adamw_halves(w[0], mine, sib, m[0], v[0], c_idx, _row_block(mine.shape[0]), "adamw_" + n)
        big_out.append([r[None] for r in res])

    gathered = _all_gather_rows([acc1, acc2, dln, dlb, dgn, dbs, dws], "gather_small")
    small, loss, dada_all = _small_finalize(
        gathered,
        _small_2d(b_ada, norm1_w, norm2_w, final_norm_w, v_ln_w, v_ln_b, lower_bounds, gn_w, b_s, w_s),
        _small_2d(m_b_ada, m_norm1_w, m_norm2_w, m_final_norm_w, m_v_ln_w, m_v_ln_b, m_lower_bounds, m_gn_w, m_b_s, m_w_s),
        _small_2d(v_b_ada, v_norm1_w, v_norm2_w, v_final_norm_w, v_v_ln_w, v_v_ln_b, v_lower_bounds, v_gn_w, v_b_s, v_w_s))
    small = [_small_original_shapes(d) for d in small]
    loss = loss.reshape(())

    ada_out = [o[None] for o in _ada_wgrad_adam(cact.T, dada_all, w_ada[0], m_w_ada[0], v_w_ada[0], chip_idx)]

    order_names = ['w_ada', 'b_ada', 'norm1_w', 'w_in', 'w_s', 'b_s', 'v_ln_w', 'v_ln_b', 'lower_bounds', 'gn_w',
                   'w_out', 'norm2_w', 'w_ffn_in', 'w_ffn_out', 'final_norm_w']
    big_idx = {'w_in': 0, 'w_out': 1, 'w_ffn_in': 2, 'w_ffn_out': 3}
    outs = [loss, grad_x.reshape(1, T, D)]
    for kind in range(4):
        for n in order_names:
            if n == 'w_ada':
                outs.append(ada_out[kind])
            elif n in big_idx:
                outs.append(big_out[big_idx[n]][kind])
            else:
                outs.append(small[kind][n])
    return tuple(outs)
```

```python
import functools

import jax
import jax.numpy as jnp
import numpy as np
from jax import lax
from jax.experimental import pallas as pl
from jax.experimental.pallas import tpu as pltpu

F32 = jnp.float32
BF16 = jnp.bfloat16
SDS = jax.ShapeDtypeStruct
MESH = pl.DeviceIdType.MESH
HIGHEST = lax.Precision.HIGHEST

D = 1024
DG = 512
DH = 512
NH = 4
HD = 128
BLK = 128
CH = 64
DFF = 2816
DIN = 3072
FFB = 1408
LEVELS = (64, 32, 16, 8, 4, 2)
HGRN_CHUNKS_PER_STEP = 8
GMLP_ROWS_PER_STEP = 1024
TOKEN_TILE = 256
PROJ_TILE = 1024
WGRAD_TOKENS = 2048
N_CHIPS = 4
N_DEV = 8
EPS = 1e-6
LR, B1, B2, AEPS, WD, STEP = 0.001, 0.9, 0.999, 1e-08, 0.01, 10

NT = (((1,), (1,)), ((), ()))
TN = (((0,), (0,)), ((), ()))


def _full(shape):
    nd = len(shape)
    return pl.BlockSpec(shape, lambda *_: (0,) * nd)


ADA_SH1, ADA_SC1, ADA_G1, ADA_SH2, ADA_SC2, ADA_G2 = range(6)


def _ada_part(k):
    return pl.BlockSpec((1, D), lambda *_: (0, k))


def _resident(shape):
    nd = len(shape)
    return pl.BlockSpec(shape, lambda *_: (0,) * nd, pipeline_mode=pl.Buffered(1))


def _arb(n=1):
    return pltpu.CompilerParams(dimension_semantics=("arbitrary",) * n)


def _dot(a, b, dims=None, precision=None):
    if dims is None:
        return jnp.dot(a, b, preferred_element_type=F32, precision=precision)
    return lax.dot_general(a, b, dims, preferred_element_type=F32, precision=precision)


def _sigmoid(x):
    return jax.nn.sigmoid(x)


def _gelu_parts(x):
    cdf = 0.5 * (1.0 + lax.erf(x * 0.7071067811865476))
    pdf = jnp.exp(-0.5 * x * x) * 0.3989422804014327
    return x * cdf, cdf + x * pdf


def _rms(x):
    return lax.rsqrt(jnp.mean(x * x, axis=-1, keepdims=True) + EPS)


def _rms_bwd(xhat, r, gw):
    return r * (gw - xhat * jnp.mean(xhat * gw, axis=-1, keepdims=True))


def _lower_bound(lbp_ref):
    l0, l1 = lbp_ref[0:1, :], lbp_ref[1:2, :]
    m = jnp.maximum(l0, l1)
    e0, e1 = jnp.exp(l0 - m), jnp.exp(l1 - m)
    return e0 / (e0 + e1), e1 / (e0 + e1)


def _proj_in(x, nw, sc, sh, w_in_b, tm):
    T = x.shape[0]

    def body(x_ref, nw_ref, sc_ref, sh_ref, w_ref, h_ref, p_ref):
        xv = x_ref[...]
        h = ((xv * _rms(xv)) * nw_ref[...]) * (1.0 + sc_ref[...]) + sh_ref[...]
        hb = h.astype(BF16)
        h_ref[...] = hb
        p_ref[...] = _dot(hb, w_ref[...])

    row = lambda i: (i, 0)
    return pl.pallas_call(
        body, grid=(T // tm,),
        in_specs=[pl.BlockSpec((tm, D), row), _full((1, D)), _ada_part(ADA_SC1), _ada_part(ADA_SH1), _resident((D, DIN))],
        out_specs=[pl.BlockSpec((tm, D), row), pl.BlockSpec((tm, DIN), row)],
        out_shape=[SDS((T, D), BF16), SDS((T, DIN), F32)],
        compiler_params=_arb(), name="proj_in")(x, nw, sc, sh, w_in_b)


def _gmlp_common(u, v, lnw, lnb, ws_ref, bst_ref):
    ug, dug = _gelu_parts(u)
    vg, dvg = _gelu_parts(v)
    mu = jnp.mean(vg, axis=-1, keepdims=True)
    vc = vg - mu
    rstd = lax.rsqrt(jnp.mean(vc * vc, axis=-1, keepdims=True) + EPS)
    vhat = vc * rstd
    vn = vhat * lnw + lnb
    vnb = vn.astype(BF16)
    mixed = []
    for h in range(NH):
        sl = slice(h * HD, (h + 1) * HD)
        mixed.append(_dot(ws_ref[h], vnb[:, sl]) + bst_ref[:, h:h + 1])
    return ug, dug, dvg, rstd, vhat, vnb, jnp.concatenate(mixed, axis=1)


def _gmlp_fwd(proj, ws_b, bst, lnw, lnb):
    T = proj.shape[0]
    rows = min(GMLP_ROWS_PER_STEP, T)

    def body(u_ref, v_ref, ws_ref, bst_ref, lnw_ref, lnb_ref, y_ref):
        for bi in range(rows // BLK):
            rs = slice(bi * BLK, (bi + 1) * BLK)
            ug, _, _, _, _, _, mixed = _gmlp_common(u_ref[rs, :], v_ref[rs, :], lnw_ref[...], lnb_ref[...], ws_ref, bst_ref)
            y_ref[rs, :] = (ug * mixed).astype(BF16)

    return pl.pallas_call(
        body, grid=(T // rows,),
        in_specs=[pl.BlockSpec((rows, DG), lambda i: (i, 0)), pl.BlockSpec((rows, DG), lambda i: (i, 1)),
                  _full((NH, BLK, BLK)), _full((BLK, NH)), _full((1, DG)), _full((1, DG))],
        out_specs=pl.BlockSpec((rows, DG), lambda i: (i, 0)),
        out_shape=SDS((T, D), BF16),
        compiler_params=_arb(), name="gmlp_fwd")(proj, proj, ws_b, bst, lnw, lnb)


def _hgrn_tables():
    t = np.arange(CH)[:, None]
    j = np.arange(CH)[None, :]
    blocks = [j <= t, j > t]
    masks = []
    for n in LEVELS:
        mid = t - t % n + n // 2
        blocks.append(np.where(t >= mid, (j >= mid) & (j <= t), (j > t) & (j < mid)))
        masks.append((t // n == j // n) & (t % n >= n // 2) & (j % n < n // 2))
    w = np.concatenate(blocks, axis=0).astype(np.float32)
    m = np.stack(masks).astype(np.float32)
    return (jnp.asarray(w, BF16), jnp.asarray(w.T, BF16), jnp.asarray(m), jnp.asarray(m + m.transpose(0, 2, 1)))


def _split_dot(w, x, parts):
    acc = None
    for _ in range(parts):
        piece = x.astype(BF16)
        term = _dot(w, piece)
        acc = term if acc is None else acc + term
        x = x - piece.astype(F32)
    return acc


def _hgrn_decays(f, w_ref):
    b = _split_dot(w_ref[0:CH, :], jnp.log(f), 3)
    row = lax.broadcasted_iota(jnp.int32, (CH, 1), 0)
    blocks = [jnp.exp(b), jnp.exp(b[CH - 1:CH, :] - b)]
    for n in LEVELS:
        up = (row & (n // 2)) != 0
        if n >= 8:
            ref = b.reshape(CH // n, n, DH)[:, n // 2 - 1:n // 2, :]
            ref = jnp.broadcast_to(ref, (CH // n, n, DH)).reshape(CH, DH)
            blocks.append(jnp.exp(jnp.where(up, b - ref, ref - b)))
        elif n == 4:
            r4 = row & 3
            two = jnp.where(r4 == 3, pltpu.roll(f, 1, 0) * f, 1.0)
            blocks.append(jnp.where(r4 == 0, pltpu.roll(f, CH - 1, 0), jnp.where(r4 == 2, f, two)))
        else:
            blocks.append(jnp.where(up, f, 1.0))
    return blocks


def _hgrn_gates(q, fl, lb, omlb, w_ref):
    sq = _sigmoid(q)
    qf = q * sq
    sig = _sigmoid(fl)
    f = lb + omlb * sig
    k = 1.0 - f
    return sq, qf, sig, f, k, _hgrn_decays(f, w_ref)


def _level_factor(e, li, sl, row, qh, kh):
    el = e[2 + li][:, sl]
    up = (row & (LEVELS[li] // 2)) != 0
    return el, up, el * jnp.where(up, qh, kh)


def _hgrn_fwd(proj, lower_bounds, gn_w, ycat, tables, placed, axes):
    T = proj.shape[0]
    nc = T // CH
    nch = min(HGRN_CHUNKS_PER_STEP, nc)
    steps = nc // nch
    w_st, _, masks, _ = tables
    nw = len(placed)
    pass_step = (3 * steps) // 4

    def body(*refs):
        q_ref, f_ref, i_ref, g_ref, lbp_ref, gn_ref, w_ref, m_ref = refs[:8]
        y_ref, o_ref, a_ref, st_ref = refs[9 + nw:13 + nw]
        s_scr, send_sems, recv_sems = refs[13 + 2 * nw:]
        gather = _WeightGather(refs[13 + nw:13 + 2 * nw], axes, send_sems, recv_sems)
        step = pl.program_id(0)

        @pl.when(step == 0)
        def _():
            gather.start()
            s_scr[...] = jnp.zeros_like(s_scr)

        @pl.when(step == pass_step)
        def _():
            gather.forward()

        lb, omlb = _lower_bound(lbp_ref)
        row = lax.broadcasted_iota(jnp.int32, (CH, 1), 0)
        eye = lax.broadcasted_iota(jnp.int32, (CH, CH), 0) == lax.broadcasted_iota(jnp.int32, (CH, CH), 1)
        in_level = [m_ref[li] > 0.0 for li in range(len(LEVELS))]
        pre = []
        for ci in range(nch):
            rs = slice(ci * CH, (ci + 1) * CH)
            _, qf, _, _, k, e = _hgrn_gates(q_ref[rs, :], f_ref[rs, :], lb, omlb, w_ref)
            mats = []
            for h in range(NH):
                sl = slice(h * HD, (h + 1) * HD)
                qh, kh = qf[:, sl], k[:, sl]
                a = jnp.where(eye, jnp.sum(qh * kh, axis=-1, keepdims=True), 0.0)
                for li in range(len(LEVELS)):
                    _, _, y = _level_factor(e, li, sl, row, qh, kh)
                    yb = y.astype(BF16)
                    a = jnp.where(in_level[li], _dot(yb, yb, NT), a)
                a_ref[ci, h] = a
                mats.append(a.astype(BF16))
            eb = e[0]
            pre.append(((qf * eb).astype(BF16), eb[CH - 1:CH, :], (k * e[1]).astype(BF16), mats))
        for ci in range(nch):
            rs = slice(ci * CH, (ci + 1) * CH)
            qe, ebl, kd, mats = pre[ci]
            v = i_ref[rs, :]
            g = g_ref[rs, :]
            for h in range(NH):
                sl = slice(h * HD, (h + 1) * HD)
                st0 = s_scr[h]
                st_ref[ci, h] = st0
                vb = v[:, sl].astype(BF16)
                o = _dot(qe[:, sl], st0.astype(BF16), NT) + _dot(mats[h], vb)
                s_scr[h] = st0 * ebl[:, sl] + _dot(vb, kd[:, sl], TN)
                o_ref[rs, sl] = o
                gh = g[:, sl]
                y_ref[rs, sl] = (((o * _rms(o)) * gn_ref[...]) * (gh * _sigmoid(gh))).astype(BF16)

        @pl.when(step == steps - 1)
        def _():
            gather.finish()

    blk = lambda j: pl.BlockSpec((nch * CH, DH), lambda c: (c, j))
    anyspec = pl.BlockSpec(memory_space=pl.ANY)
    res = pl.pallas_call(
        body, grid=(steps,),
        in_specs=[blk(2), blk(3), blk(4), blk(5), _full((2, DH)), _full((1, HD)),
                  _full(w_st.shape), _full(masks.shape), anyspec] + [anyspec] * nw,
        out_specs=[pl.BlockSpec((nch * CH, DH), lambda c: (c, 1)),
                   pl.BlockSpec((nch * CH, DH), lambda c: (c, 0)),
                   pl.BlockSpec((nch, NH, CH, CH), lambda c: (c, 0, 0, 0)),
                   pl.BlockSpec((nch, NH, HD, HD), lambda c: (c, 0, 0, 0))] + [anyspec] * nw,
        out_shape=[SDS((T, D), BF16), SDS((T, DH), F32), SDS((nc, NH, CH, CH), F32), SDS((nc, NH, HD, HD), F32)]
        + [SDS(a.shape, a.dtype) for a in placed],
        scratch_shapes=[pltpu.VMEM((NH, HD, HD), F32)] + _gather_sems(nw),
        input_output_aliases={8: 0, **{9 + i: 4 + i for i in range(nw)}},
        compiler_params=_arb(), name="hgrn_fwd")(proj, proj, proj, proj, lower_bounds, gn_w, w_st, masks, ycat, *placed)
    return res[:4], res[4:]


def _token_local(x, ycat, tgt, g1, nw2, sc2, sh2, g2, fw, w_out_b, w_fi_b, w_fo_b, tm):
    T = x.shape[0]
    inv_d = 1.0 / D

    def body(x_ref, y_ref, t_ref, g1_ref, nw2_ref, sc2_ref, sh2_ref, g2_ref, fw_ref, wo_ref, wfi_ref, wfo_ref,
             dy_ref, dx1_ref, h2_ref, act_ref, dff_ref, dgu_ref, dmix_ref, acc_ref):
        @pl.when(pl.program_id(0) == 0)
        def _():
            acc_ref[...] = jnp.zeros_like(acc_ref)

        def acc(row, val):
            acc_ref[row:row + 1, :] += jnp.sum(val, axis=0, keepdims=True)

        g1v, g2v = g1_ref[...], g2_ref[...]
        mix = _dot(y_ref[...], wo_ref[...])
        x1 = x_ref[...] + g1v * mix
        r2 = _rms(x1)
        xh2 = x1 * r2
        n2 = xh2 * nw2_ref[...]
        osc2 = 1.0 + sc2_ref[...]
        h2b = (n2 * osc2 + sh2_ref[...]).astype(BF16)
        h2_ref[...] = h2b
        ff = jnp.zeros((tm, D), F32)
        saved = []
        for kb in range(DFF // FFB):
            gate = _dot(h2b, wfi_ref[:, kb * FFB:(kb + 1) * FFB])
            up = _dot(h2b, wfi_ref[:, DFF + kb * FFB:DFF + (kb + 1) * FFB])
            sg = _sigmoid(gate)
            actb = (gate * sg * up).astype(BF16)
            act_ref[:, kb * FFB:(kb + 1) * FFB] = actb
            ff = ff + _dot(actb, wfo_ref[kb * FFB:(kb + 1) * FFB, :])
            saved.append((gate, up, sg))
        x2 = x1 + g2v * ff
        r3 = _rms(x2)
        xh3 = x2 * r3
        err = xh3 * fw_ref[...] - t_ref[...]
        acc(6, (0.5 * inv_d) * err * err)
        dy = err * inv_d
        acc(4, dy * xh3)
        dx2 = _rms_bwd(xh3, r3, dy * fw_ref[...])
        acc(0, dx2 * ff)
        dffb = (dx2 * g2v).astype(BF16)
        dff_ref[...] = dffb
        dh2 = jnp.zeros((tm, D), F32)
        for kb in range(DFF // FFB):
            gate, up, sg = saved[kb]
            da = _dot(dffb, wfo_ref[kb * FFB:(kb + 1) * FFB, :], NT)
            dgate = (da * up * (sg * (1.0 + gate * (1.0 - sg)))).astype(BF16)
            dup = (da * gate * sg).astype(BF16)
            dgu_ref[:, kb * FFB:(kb + 1) * FFB] = dgate
            dgu_ref[:, DFF + kb * FFB:DFF + (kb + 1) * FFB] = dup
            dh2 = dh2 + _dot(dgate, wfi_ref[:, kb * FFB:(kb + 1) * FFB], NT)
            dh2 = dh2 + _dot(dup, wfi_ref[:, DFF + kb * FFB:DFF + (kb + 1) * FFB], NT)
        acc(2, dh2)
        acc(1, dh2 * n2)
        dn2 = dh2 * osc2
        acc(3, dn2 * xh2)
        dx1 = dx2 + _rms_bwd(xh2, r2, dn2 * nw2_ref[...])
        acc(5, dx1 * mix)
        dmixb = (dx1 * g1v).astype(BF16)
        dmix_ref[...] = dmixb
        dy_ref[...] = _dot(dmixb, wo_ref[...], NT)
        dx1_ref[...] = dx1

    row = lambda i: (i, 0)
    vec = _full((1, D))
    return pl.pallas_call(
        body, grid=(T // tm,),
        in_specs=[pl.BlockSpec((tm, D), row), pl.BlockSpec((tm, D), row), pl.BlockSpec((tm, D), row),
                  _ada_part(ADA_G1), vec, _ada_part(ADA_SC2), _ada_part(ADA_SH2), _ada_part(ADA_G2), vec,
                  _resident((D, D)), _resident((D, 2 * DFF)), _resident((DFF, D))],
        out_specs=[pl.BlockSpec((tm, D), row), pl.BlockSpec((tm, D), row), pl.BlockSpec((tm, D), row),
                   pl.BlockSpec((tm, DFF), row), pl.BlockSpec((tm, D), row), pl.BlockSpec((tm, 2 * DFF), row),
                   pl.BlockSpec((tm, D), row), _full((8, D))],
        out_shape=[SDS((T, D), F32), SDS((T, D), F32), SDS((T, D), BF16), SDS((T, DFF), BF16), SDS((T, D), BF16),
                   SDS((T, 2 * DFF), BF16), SDS((T, D), BF16), SDS((8, D), F32)],
        compiler_params=_arb(), name="token_local")(x, ycat, tgt, g1, nw2, sc2, sh2, g2, fw, w_out_b, w_fi_b, w_fo_b)


def _gmlp_bwd(proj, dycat, ws_b, bst, lnw, lnb, grads):
    T = proj.shape[0]
    rows = min(GMLP_ROWS_PER_STEP, T)
    nb = T // rows
    nw = len(grads)

    def body(*refs):
        u_ref, v_ref, dy_ref, ws_ref, bst_ref, lnw_ref, lnb_ref = refs[:7]
        dp_ref, dws_ref, dbs_ref, dln_ref = refs[7 + nw:11 + nw]
        dbs_acc, send_sems, recv_sems = refs[11 + 2 * nw:]
        exchange = _CoreExchange(refs[7:7 + nw], refs[11 + nw:11 + 2 * nw], send_sems, recv_sems)
        i = pl.program_id(0)

        @pl.when(i == 0)
        def _():
            exchange.start()
            dws_ref[...] = jnp.zeros_like(dws_ref)
            dln_ref[...] = jnp.zeros_like(dln_ref)
            dbs_acc[...] = jnp.zeros_like(dbs_acc)

        r = lax.broadcasted_iota(jnp.int32, (BLK, BLK), 0) // CH
        c = lax.broadcasted_iota(jnp.int32, (BLK, BLK), 1) // CH
        for bi in range(rows // BLK):
            rs = slice(bi * BLK, (bi + 1) * BLK)
            ug, dug, dvg, rstd, vhat, vnb, mixed = _gmlp_common(
                u_ref[rs, :], v_ref[rs, :], lnw_ref[...], lnb_ref[...], ws_ref, bst_ref)
            dya = dy_ref[rs, :]
            dp_ref[rs, 0:DG] = (dya * mixed * dug).astype(BF16)
            dmixed = dya * ug
            dbs_acc[...] += dmixed
            dmb = dmixed.astype(BF16)
            dvn = []
            for h in range(NH):
                sl = slice(h * HD, (h + 1) * HD)
                dws_ref[h * BLK:(h + 1) * BLK, :] += jnp.where(r >= c, _dot(dmb[:, sl], vnb[:, sl], NT), 0.0)
                dvn.append(_dot(ws_ref[h], dmb[:, sl], TN))
            dvn = jnp.concatenate(dvn, axis=1)
            dln_ref[0:1, :] += jnp.sum(dvn * vhat, axis=0, keepdims=True)
            dln_ref[1:2, :] += jnp.sum(dvn, axis=0, keepdims=True)
            dvh = dvn * lnw_ref[...]
            dvgel = rstd * (dvh - jnp.mean(dvh, axis=-1, keepdims=True) - vhat * jnp.mean(dvh * vhat, axis=-1, keepdims=True))
            dp_ref[rs, DG:2 * DG] = (dvgel * dvg).astype(BF16)

        @pl.when(i == nb - 1)
        def _():
            head = lax.broadcasted_iota(jnp.int32, (8, BLK), 0)
            ones = jnp.ones((8, HD), F32)
            out = jnp.zeros((8, BLK), F32)
            for h in range(NH):
                sums = _dot(ones, dbs_acc[:, h * HD:(h + 1) * HD], NT, precision=HIGHEST)
                out = out + jnp.where(head == h, sums, 0.0)
            dbs_ref[...] = out
            exchange.finish()

    anyspec = pl.BlockSpec(memory_space=pl.ANY)
    res = pl.pallas_call(
        body, grid=(nb,),
        in_specs=[pl.BlockSpec((rows, DG), lambda i: (i, 0)), pl.BlockSpec((rows, DG), lambda i: (i, 1)),
                  pl.BlockSpec((rows, DG), lambda i: (i, 0)),
                  _full((NH, BLK, BLK)), _full((BLK, NH)), _full((1, DG)), _full((1, DG))] + [anyspec] * nw,
        out_specs=[pl.BlockSpec((rows, 2 * DG), lambda i: (i, 2)), _full((NH * BLK, BLK)), _full((8, BLK)), _full((8, DG))]
        + [anyspec] * nw,
        out_shape=[SDS((T, DIN), BF16), SDS((NH * BLK, BLK), F32), SDS((8, BLK), F32), SDS((8, DG), F32)]
        + _core_exchange_shapes(grads),
        scratch_shapes=[pltpu.VMEM((BLK, DG), F32)] + _core_exchange_sems(nw),
        compiler_params=_arb(), name="gmlp_bwd")(proj, proj, dycat, ws_b, bst, lnw, lnb, *grads)
    return res[:4], res[4:]


def _hgrn_bwd(proj, o_pre, a_all, st_all, dycat, lower_bounds, gn_w, dproj, tables, sums):
    T = proj.shape[0]
    nc = T // CH
    nch = min(HGRN_CHUNKS_PER_STEP, nc)
    steps = nc // nch
    w_st, w_st_t, _, masks_sym = tables
    n_lev = len(LEVELS)
    nw = len(sums)

    def body(*refs):
        q_ref, f_ref, i_ref, g_ref, o_ref, a_ref, st_ref, dy_ref, lbp_ref, gn_ref, w_ref, wt_ref, ms_ref = refs[:13]
        dp_ref, dlb_ref, dgn_ref = refs[14 + nw:17 + nw]
        ds_scr, dx_scr, send_sems, recv_sems = refs[17 + 2 * nw:]
        exchange = _ChipExchange(refs[14:14 + nw], refs[17 + nw:17 + 2 * nw], send_sems, recv_sems)
        i = pl.program_id(0)

        @pl.when(i == 0)
        def _():
            exchange.start()
            ds_scr[...] = jnp.zeros_like(ds_scr)
            dlb_ref[...] = jnp.zeros_like(dlb_ref)
            dgn_ref[...] = jnp.zeros_like(dgn_ref)

        lb, omlb = _lower_bound(lbp_ref)
        row = lax.broadcasted_iota(jnp.int32, (CH, 1), 0)
        eye = lax.broadcasted_iota(jnp.int32, (CH, CH), 0) == lax.broadcasted_iota(jnp.int32, (CH, CH), 1)
        lower = lax.broadcasted_iota(jnp.int32, (CH, CH), 0) > lax.broadcasted_iota(jnp.int32, (CH, CH), 1)
        dgn = jnp.zeros((1, HD), F32)
        pre = []
        for ci in range(nch):
            rs = slice(ci * CH, (ci + 1) * CH)
            q = q_ref[rs, :]
            v = i_ref[rs, :]
            g = g_ref[rs, :]
            sq, qf, sig, f, k, e = _hgrn_gates(q, f_ref[rs, :], lb, omlb, w_ref)
            eb = e[0]
            ekd = e[1]
            kd = k * ekd
            qe = qf * eb
            dob_h, dqe_h, dqf_h, dki_h, dv_h, dg_h = [], [], [], [], [], []
            for h in range(NH):
                sl = slice(h * HD, (h + 1) * HD)
                o = o_ref[rs, sl]
                ro = _rms(o)
                oh = o * ro
                gh = g[:, sl]
                sg = _sigmoid(gh)
                dyb = dy_ref[rs, sl]
                dg_h.append(dyb * (oh * gn_ref[...]) * (sg * (1.0 + gh * (1.0 - sg))))
                don = dyb * (gh * sg)
                dgn = dgn + jnp.sum(don * oh, axis=0, keepdims=True)
                dob = _rms_bwd(oh, ro, don * gn_ref[...]).astype(BF16)
                vb = v[:, sl].astype(BF16)
                qh, kh = qf[:, sl], k[:, sl]
                dqe = _dot(dob, st_ref[ci, h].astype(BF16))
                da = _dot(dob, vb, NT)
                ddiag = jnp.sum(jnp.where(eye, da, 0.0), axis=-1, keepdims=True)
                dsym = jnp.where(lower, da, _dot(vb, dob, NT))
                upper_part = jnp.zeros((CH, HD), F32)
                both = jnp.zeros((CH, HD), F32)
                for li in range(n_lev):
                    el, up, y = _level_factor(e, li, sl, row, qh, kh)
                    dyv = _dot((ms_ref[li] * dsym).astype(BF16), y.astype(BF16))
                    dx_scr[ci, (2 + li) * CH:(3 + li) * CH, sl] = dyv * y
                    dye = dyv * el
                    upper_part = upper_part + jnp.where(up, dye, 0.0)
                    both = both + dye
                dob_h.append(dob)
                dqe_h.append(dqe)
                dqf_h.append(dqe * eb[:, sl] + ddiag * kh + upper_part)
                dki_h.append(ddiag * qh + (both - upper_part))
                dv_h.append(_dot(a_ref[ci, h].astype(BF16), dob, TN))
            dp_ref[rs, 0:DH] = (jnp.concatenate(dqf_h, axis=1) * (sq * (1.0 + q * (1.0 - sq)))).astype(BF16)
            dp_ref[rs, 3 * DH:4 * DH] = jnp.concatenate(dg_h, axis=1).astype(BF16)
            pre.append((v, sig, f, eb, ekd, kd, qe, dob_h, jnp.concatenate(dqe_h, axis=1), dki_h, dv_h))
        dgn_ref[0:1, :] += dgn
        for ci in reversed(range(nch)):
            rs = slice(ci * CH, (ci + 1) * CH)
            v, sig, f, eb, ekd, kd, qe, dob_h, dqe, dki_h, dv_h = pre[ci]
            ebl = eb[CH - 1:CH, :]
            dbl_h, dkd_h, dv2_h = [], [], []
            for h in range(NH):
                sl = slice(h * HD, (h + 1) * HD)
                dst1 = ds_scr[h]
                dst1b = dst1.astype(BF16)
                ds_scr[h] = dst1 * ebl[:, sl] + _dot(dob_h[h], qe[:, sl].astype(BF16), TN)
                dbl_h.append(ebl[:, sl] * jnp.sum(st_ref[ci, h] * dst1, axis=0, keepdims=True))
                dkd_h.append(_dot(v[:, sl].astype(BF16), dst1b))
                dv2_h.append(dv_h[h] + _dot(kd[:, sl].astype(BF16), dst1b, NT))
            dkd = jnp.concatenate(dkd_h, axis=1)
            dx_scr[ci, 0:CH, :] = dqe * qe + jnp.where(row == CH - 1, jnp.concatenate(dbl_h, axis=1), 0.0)
            dx_scr[ci, CH:2 * CH, :] = dkd * kd
            dlf = _split_dot(wt_ref[...], dx_scr[ci], 2)
            df = dlf / f - (dkd * ekd + jnp.concatenate(dki_h, axis=1))
            dlb_ref[0:1, :] += jnp.sum(df * (1.0 - sig), axis=0, keepdims=True)
            dp_ref[rs, DH:2 * DH] = (df * omlb * sig * (1.0 - sig)).astype(BF16)
            dp_ref[rs, 2 * DH:3 * DH] = jnp.concatenate(dv2_h, axis=1).astype(BF16)

        @pl.when(i == steps - 1)
        def _():
            gl = dlb_ref[0:1, :] * lb * omlb
            dlb_ref[0:1, :] = gl
            dlb_ref[1:2, :] = -gl
            exchange.finish()

    rev = lambda j: pl.BlockSpec((nch * CH, DH), lambda c: (steps - 1 - c, j))
    anyspec = pl.BlockSpec(memory_space=pl.ANY)
    res = pl.pallas_call(
        body, grid=(steps,),
        in_specs=[rev(2), rev(3), rev(4), rev(5), rev(0),
                  pl.BlockSpec((nch, NH, CH, CH), lambda c: (steps - 1 - c, 0, 0, 0)),
                  pl.BlockSpec((nch, NH, HD, HD), lambda c: (steps - 1 - c, 0, 0, 0)),
                  rev(1), _full((2, DH)), _full((1, HD)),
                  _full(w_st.shape), _full(w_st_t.shape), _full(masks_sym.shape),
                  anyspec] + [anyspec] * nw,
        out_specs=[pl.BlockSpec((nch * CH, 4 * DH), lambda c: (steps - 1 - c, 0)), _full((8, DH)), _full((8, HD))]
        + [anyspec] * nw,
        out_shape=[SDS((T, DIN), BF16), SDS((8, DH), F32), SDS((8, HD), F32)] + _slot_shapes(sums),
        scratch_shapes=[pltpu.VMEM((NH, HD, HD), F32), pltpu.VMEM((nch, (2 + n_lev) * CH, DH), F32)] + _exchange_sems(nw),
        input_output_aliases={13: 0},
        compiler_params=_arb(), name="hgrn_bwd")(proj, proj, proj, proj, o_pre, a_all, st_all, dycat, lower_bounds, gn_w,
                                                 w_st, w_st_t, masks_sym, dproj, *sums)
    return res[:3], res[3:]


def _proj_in_bwd(dproj, x, dx1, nw, sc, w_in_b, tm, sums):
    T = x.shape[0]
    ns = len(sums)
    steps = T // tm

    def body(*refs):
        dp_ref, x_ref, dx1_ref, nw_ref, sc_ref, w_ref = refs[:6]
        gx_ref, acc_ref = refs[6 + ns:8 + ns]
        exchange = _ChipExchange(refs[6:6 + ns], refs[8 + ns:8 + 2 * ns], *refs[8 + 2 * ns:])

        @pl.when(pl.program_id(0) == 0)
        def _():
            exchange.start()
            acc_ref[...] = jnp.zeros_like(acc_ref)

        dh = _dot(dp_ref[:, 0:4 * DH], w_ref[:, 2 * DG:DIN], NT) + _dot(dp_ref[:, 4 * DH:DIN], w_ref[:, 0:2 * DG], NT)
        xv = x_ref[...]
        r = _rms(xv)
        xh = xv * r
        n1 = xh * nw_ref[...]
        acc_ref[0:1, :] += jnp.sum(dh, axis=0, keepdims=True)
        acc_ref[1:2, :] += jnp.sum(dh * n1, axis=0, keepdims=True)
        dn = dh * (1.0 + sc_ref[...])
        acc_ref[2:3, :] += jnp.sum(dn * xh, axis=0, keepdims=True)
        gx_ref[...] = dx1_ref[...] + _rms_bwd(xh, r, dn * nw_ref[...])

        @pl.when(pl.program_id(0) == steps - 1)
        def _():
            exchange.finish()

    row = lambda i: (i, 0)
    anyspec = pl.BlockSpec(memory_space=pl.ANY)
    res = pl.pallas_call(
        body, grid=(steps,),
        in_specs=[pl.BlockSpec((tm, DIN), row), pl.BlockSpec((tm, D), row), pl.BlockSpec((tm, D), row),
                  _full((1, D)), _ada_part(ADA_SC1), _resident((D, DIN))] + [anyspec] * ns,
        out_specs=[pl.BlockSpec((tm, D), row), _full((8, D))] + [anyspec] * ns,
        out_shape=[SDS((T, D), F32), SDS((8, D), F32)] + _slot_shapes(sums),
        scratch_shapes=_exchange_sems(ns),
        compiler_params=_arb(), name="proj_in_bwd")(dproj, x, dx1, nw, sc, w_in_b, *sums)
    return res[:2], res[2:]


def _wgrad(a, b, bk, bn, tt, name):
    T, K = a.shape
    N = b.shape[1]
    nn, nk, nt = N // bn, K // bk, T // tt
    bmap = lambda n, k, t: (t, n)

    def body(a_ref, b_ref, o_ref):
        @pl.when(pl.program_id(2) == 0)
        def _():
            o_ref[...] = jnp.zeros_like(o_ref)

        o_ref[0] += _dot(a_ref[...], b_ref[...], TN)

    return pl.pallas_call(
        body, grid=(nn, nk, nt),
        in_specs=[pl.BlockSpec((tt, bk), lambda n, k, t: (t, k)), pl.BlockSpec((tt, bn), bmap)],
        out_specs=pl.BlockSpec((1, bk, bn), lambda n, k, t: (n, k, 0)),
        out_shape=SDS((nn, K, bn), F32),
        compiler_params=_arb(3), name=name)(a, b)


def _adam_math(w, g, m, v):
    m = B1 * m + (1.0 - B1) * g
    v = B2 * v + (1.0 - B2) * (g * g)
    m_hat = m / (1.0 - B1 ** STEP)
    v_hat = v / (1.0 - B2 ** STEP)
    return -LR * (m_hat / (jnp.sqrt(v_hat) + AEPS) + WD * w), m, v


def _adamw_halves(w, mine, sibling, m, v, c_idx, rb, name):
    R, C = w.shape
    nb = (R // 2) // rb

    def body(c_ref, w_ref, a_ref, b_ref, m_ref, v_ref, g_out, d_out, m_out, v_out):
        g = jnp.where(pl.program_id(0) == c_ref[0], a_ref[...], b_ref[...])
        g_out[...] = g
        d_out[...], m_out[...], v_out[...] = _adam_math(w_ref[...], g, m_ref[...], v_ref[...])

    whole = pl.BlockSpec((rb, C), lambda hh, i, cr: (hh * nb + i, 0))
    half = pl.BlockSpec((rb, C), lambda hh, i, cr: (i, 0))
    return pl.pallas_call(
        body,
        grid_spec=pltpu.PrefetchScalarGridSpec(
            num_scalar_prefetch=1, grid=(2, nb), in_specs=[whole, half, half, whole, whole], out_specs=[whole] * 4),
        out_shape=[SDS((R, C), F32)] * 4, compiler_params=_arb(2), name=name)(c_idx, w, mine, sibling, m, v)


def _ada_forward(c_all, w_ada):
    n = w_ada.shape[1]

    def body(c_ref, w_ref, ca_ref, p_ref):
        cv = c_ref[...]
        ca = cv * _sigmoid(cv)
        ca_ref[...] = ca
        p_ref[...] = _dot(ca, w_ref[...], precision=HIGHEST)

    return pl.pallas_call(
        body, grid=(n // 512,),
        in_specs=[_full((N_DEV, D)), pl.BlockSpec((D, 512), lambda i: (0, i))],
        out_specs=[_full((N_DEV, D)), pl.BlockSpec((N_DEV, 512), lambda i: (0, i))],
        out_shape=[SDS((N_DEV, D), F32), SDS((N_DEV, n), F32)],
        compiler_params=_arb(), name="ada_forward")(c_all, w_ada)


def _ada_wgrad_adam(cact_t, dada_all, w, m, v, chip_idx):
    R, C = w.shape
    rb = 256

    def body(j_ref, c_ref, d_ref, w_ref, m_ref, v_ref, g_out, d_out, m_out, v_out):
        g = _dot(c_ref[...], d_ref[...], precision=HIGHEST)
        g_out[...] = g
        d_out[...], m_out[...], v_out[...] = _adam_math(w_ref[...], g, m_ref[...], v_ref[...])

    spec = pl.BlockSpec((rb, C), lambda i, j: (i, 0))
    return pl.pallas_call(
        body,
        grid_spec=pltpu.PrefetchScalarGridSpec(
            num_scalar_prefetch=1, grid=(R // rb,),
            in_specs=[pl.BlockSpec((rb, N_DEV), lambda i, j: (i, 0)), pl.BlockSpec((N_DEV, C), lambda i, j: (0, j[0])),
                      spec, spec, spec],
            out_specs=[spec] * 4),
        out_shape=[SDS((R, C), F32)] * 4,
        compiler_params=_arb(), name="ada_wgrad_adam")(chip_idx, cact_t, dada_all, w, m, v)


SMALL_NAMES = ('b_ada', 'norm1_w', 'norm2_w', 'final_norm_w', 'v_ln_w', 'v_ln_b', 'lower_bounds', 'gn_w', 'b_s', 'w_s')


def _small_finalize(gathered, params, moms, vels):
    n_in = len(gathered)

    def body(*refs):
        acc1, acc2, dln, dlb, dgn, dbs, dws = refs[:n_in]
        prm = [dict(zip(SMALL_NAMES, refs[n_in + k * 10:n_in + (k + 1) * 10])) for k in range(3)]
        outs = [dict(zip(SMALL_NAMES, refs[n_in + 30 + k * 10:n_in + 30 + (k + 1) * 10])) for k in range(4)]
        loss_ref, dada_ref = refs[n_in + 70:n_in + 72]

        def dev_sum(ref, first, n):
            per = ref.shape[0] // N_DEV
            g = ref[first:first + n, :]
            for dev in range(1, N_DEV):
                g = g + ref[dev * per + first:dev * per + first + n, :]
            return g

        def update(n, g, cols=slice(None)):
            outs[0][n][:, cols] = g
            outs[1][n][:, cols], outs[2][n][:, cols], outs[3][n][:, cols] = _adam_math(
                prm[0][n][:, cols], g, prm[1][n][:, cols], prm[2][n][:, cols])

        ada_rows = ((acc1, 0), (acc1, 1), (acc2, 5), (acc2, 2), (acc2, 1), (acc2, 0))
        for k, (ref, r) in enumerate(ada_rows):
            update('b_ada', dev_sum(ref, r, 1), slice(k * D, (k + 1) * D))
            for dev in range(N_DEV):
                dada_ref[dev:dev + 1, k * D:(k + 1) * D] = ref[8 * dev + r:8 * dev + r + 1, :]
        update('norm1_w', dev_sum(acc1, 2, 1))
        update('norm2_w', dev_sum(acc2, 3, 1))
        update('final_norm_w', dev_sum(acc2, 4, 1))
        update('v_ln_w', dev_sum(dln, 0, 1))
        update('v_ln_b', dev_sum(dln, 1, 1))
        update('lower_bounds', dev_sum(dlb, 0, 2))
        update('gn_w', dev_sum(dgn, 0, 1))
        update('b_s', dev_sum(dbs, 0, NH))
        update('w_s', dev_sum(dws, 0, NH * BLK))
        loss_ref[...] = jnp.sum(dev_sum(acc2, 6, 1), axis=-1, keepdims=True)

    shapes = [SDS(params[n].shape, F32) for n in SMALL_NAMES]
    res = pl.pallas_call(
        body, out_shape=shapes * 4 + [SDS((1, 1), F32), SDS((N_DEV, 6 * D), F32)], name="small_finalize")(
            *gathered, *[d[n] for d in (params, moms, vels) for n in SMALL_NAMES])
    return [dict(zip(SMALL_NAMES, res[k * 10:(k + 1) * 10])) for k in range(4)], res[40], res[41]


def _position():
    x, y, c = lax.axis_index("x"), lax.axis_index("y"), lax.axis_index("c")
    return x, y, c


def _chip_at(x, y, r):
    return (x ^ (r >> 1), y ^ (r & 1))


def _gather_rows(ins, outs, send_sems, recv_sems, local_sems, after_issue=None):
    nb = len(ins)
    x, y, c = _position()
    me, sibling = (x, y, c), (x, y, 1 - c)
    chips = [_chip_at(x, y, r) for r in (1, 2, 3)]

    def rows(b, px, py, pc):
        m_per = ins[b].shape[0]
        return outs[b].at[pl.ds((4 * px + 2 * py + pc) * m_per, m_per), :]

    def copy(b, k, blk, to, src=None):
        return pltpu.make_async_remote_copy(
            src_ref=rows(b, *blk) if src is None else src, dst_ref=rows(b, *blk),
            send_sem=send_sems.at[7 * b + k], recv_sem=recv_sems.at[7 * b + k], device_id=to, device_id_type=MESH)

    local, sent = [], []
    for b in range(nb):
        mine = pltpu.make_async_copy(ins[b], rows(b, *me), local_sems.at[b])
        mine.start()
        local.append(mine)
        first = [copy(b, 0, me, sibling, src=ins[b])]
        first += [copy(b, 1 + j, me, (*chip, c), src=ins[b]) for j, chip in enumerate(chips)]
        for cp in first:
            cp.start()
        sent += first
    if after_issue is not None:
        after_issue()
    for b in range(nb):
        for j, chip in enumerate(chips):
            copy(b, 1 + j, (*chip, c), me).wait_recv()
            passed = copy(b, 4 + j, (*chip, c), sibling)
            passed.start()
            sent.append(passed)
    for b in range(nb):
        copy(b, 0, sibling, me).wait_recv()
        for j, chip in enumerate(chips):
            copy(b, 4 + j, (*chip, 1 - c), me).wait_recv()
    for cp in sent:
        cp.wait_send()
    for cp in local:
        cp.wait()


def _gather_rows_shapes(blocks):
    return [SDS((N_DEV * b.shape[0], b.shape[1]), b.dtype) for b in blocks]


def _gather_rows_sems(nb):
    return [pltpu.SemaphoreType.DMA((7 * nb,)), pltpu.SemaphoreType.DMA((7 * nb,)), pltpu.SemaphoreType.DMA((nb,))]


def _all_gather_rows(blocks, name):
    nb = len(blocks)

    def body(*refs):
        _gather_rows(refs[:nb], refs[nb:2 * nb], *refs[2 * nb:])

    vmem = pl.BlockSpec(memory_space=pltpu.VMEM)
    return pl.pallas_call(
        body, out_shape=_gather_rows_shapes(blocks), in_specs=[vmem] * nb, out_specs=[vmem] * nb,
        scratch_shapes=_gather_rows_sems(nb), name=name)(*blocks)


def _place_shard(w_shard, axis, chip_idx, name):
    R, C = w_shard.shape
    rb = _row_block(R)
    nb = R // rb
    full = (R * N_CHIPS, C) if axis == 0 else (R, C * N_CHIPS)
    omap = (lambda i, j: (j[0] * nb + i, 0)) if axis == 0 else (lambda i, j: (i, j[0]))

    def body(j_ref, w_ref, o_ref):
        o_ref[...] = w_ref[...].astype(BF16)

    return pl.pallas_call(
        body,
        grid_spec=pltpu.PrefetchScalarGridSpec(
            num_scalar_prefetch=1, grid=(nb,), in_specs=[pl.BlockSpec((rb, C), lambda i, j: (i, 0))],
            out_specs=pl.BlockSpec((rb, C), omap)),
        out_shape=SDS(full, BF16), compiler_params=_arb(), name=name)(chip_idx, w_shard)


class _WeightGather:
    def __init__(self, refs, axes, send_sems, recv_sems):
        self.refs, self.axes, self.send_sems, self.recv_sems = refs, axes, send_sems, recv_sems
        self.x, self.y, self.c = _position()
        self.j = 2 * self.x + self.y
        self.n = 3 * len(refs)

    def _half(self, w, chip_idx, half):
        ref, axis = self.refs[w], self.axes[w]
        if axis == 0:
            size = ref.shape[0] // N_CHIPS
            return ref.at[pl.ds(chip_idx * size + half * (size // 2), size // 2), :]
        size = ref.shape[1] // N_CHIPS
        rows = ref.shape[0] // 2
        return ref.at[pl.ds(half * rows, rows), pl.ds(chip_idx * size, size)]

    def _ici(self, w, r, chip_idx):
        k = 3 * w + r - 1
        piece = self._half(w, chip_idx, self.c)
        return pltpu.make_async_remote_copy(
            src_ref=piece, dst_ref=piece, send_sem=self.send_sems.at[k], recv_sem=self.recv_sems.at[k],
            device_id=(*_chip_at(self.x, self.y, r), self.c), device_id_type=MESH)

    def _d2d(self, w, r, half):
        k = self.n + 3 * w + r - 1
        piece = self._half(w, self.j ^ r, half)
        return pltpu.make_async_remote_copy(
            src_ref=piece, dst_ref=piece, send_sem=self.send_sems.at[k], recv_sem=self.recv_sems.at[k],
            device_id=(self.x, self.y, 1 - self.c), device_id_type=MESH)

    def _each(self):
        return [(w, r) for w in range(len(self.refs)) for r in (1, 2, 3)]

    def start(self):
        for w, r in self._each():
            self._ici(w, r, self.j).start()

    def forward(self):
        for w, r in self._each():
            self._ici(w, r, self.j ^ r).wait_recv()
            self._d2d(w, r, self.c).start()

    def finish(self):
        for w, r in self._each():
            self._ici(w, r, self.j).wait_send()
            self._d2d(w, r, self.c).wait_send()
            self._d2d(w, r, 1 - self.c).wait_recv()


def _gather_sems(n_weights):
    return [pltpu.SemaphoreType.DMA((6 * n_weights,)), pltpu.SemaphoreType.DMA((6 * n_weights,))]


def _gather_weights(placed, axes, row_blocks, name):
    nw, nb = len(placed), len(row_blocks)

    def body(*refs):
        w_outs, b_ins, b_outs = refs[nw + nb:2 * nw + nb], refs[nw:nw + nb], refs[2 * nw + nb:2 * (nw + nb)]
        sems = refs[2 * (nw + nb):]
        g = _WeightGather(w_outs, axes, *sems[:2])
        _gather_rows(b_ins, b_outs, *sems[2:], after_issue=g.start)
        g.forward()
        g.finish()

    anyspec = pl.BlockSpec(memory_space=pl.ANY)
    vmem = pl.BlockSpec(memory_space=pltpu.VMEM)
    res = pl.pallas_call(
        body, out_shape=[SDS(a.shape, a.dtype) for a in placed] + _gather_rows_shapes(row_blocks),
        in_specs=[anyspec] * nw + [vmem] * nb, out_specs=[anyspec] * nw + [vmem] * nb,
        scratch_shapes=_gather_sems(nw) + _gather_rows_sems(nb), input_output_aliases={i: i for i in range(nw)},
        name=name)(*placed, *row_blocks)
    return res[:nw], res[nw:]


class _ChipExchange:
    def __init__(self, ins, outs, send_sems, recv_sems):
        self.ins, self.outs, self.send_sems, self.recv_sems = ins, outs, send_sems, recv_sems
        self.x, self.y, self.c = _position()
        self.j = 2 * self.x + self.y

    def _copies(self):
        for w in range(len(self.ins)):
            for r in (1, 2, 3):
                k = 3 * w + r - 1
                yield pltpu.make_async_remote_copy(
                    src_ref=self.ins[w].at[self.j ^ r], dst_ref=self.outs[w].at[r - 1],
                    send_sem=self.send_sems.at[k], recv_sem=self.recv_sems.at[k],
                    device_id=(*_chip_at(self.x, self.y, r), self.c), device_id_type=MESH)

    def start(self):
        for cp in self._copies():
            cp.start()

    def finish(self):
        for cp in self._copies():
            cp.wait()


def _exchange_sems(n_weights):
    return [pltpu.SemaphoreType.DMA((3 * n_weights,)), pltpu.SemaphoreType.DMA((3 * n_weights,))]


class _CoreExchange:
    def __init__(self, ins, outs, send_sems, recv_sems):
        self.ins, self.outs, self.send_sems, self.recv_sems = ins, outs, send_sems, recv_sems
        self.x, self.y, self.c = _position()

    def _copies(self):
        for w in range(len(self.ins)):
            yield pltpu.make_async_remote_copy(
                src_ref=self.ins[w].at[:, 1 - self.c], dst_ref=self.outs[w],
                send_sem=self.send_sems.at[w], recv_sem=self.recv_sems.at[w],
                device_id=(self.x, self.y, 1 - self.c), device_id_type=MESH)

    def start(self):
        for cp in self._copies():
            cp.start()

    def finish(self):
        for cp in self._copies():
            cp.wait()


def _core_exchange_shapes(grads):
    return [SDS((g.shape[0], g.shape[2], g.shape[3]), F32) for g in grads]


def _core_exchange_sems(n):
    return [pltpu.SemaphoreType.DMA((n,)), pltpu.SemaphoreType.DMA((n,))]


def _exchange_core_halves(grads, name):
    nw = len(grads)

    def body(*refs):
        ex = _CoreExchange(refs[:nw], refs[nw:2 * nw], *refs[2 * nw:])
        ex.start()
        ex.finish()

    anyspec = pl.BlockSpec(memory_space=pl.ANY)
    return pl.pallas_call(
        body, out_shape=_core_exchange_shapes(grads), in_specs=[anyspec] * nw, out_specs=[anyspec] * nw,
        scratch_shapes=_core_exchange_sems(nw), name=name)(*grads)


def _add_core_halves(g4, recv, c_idx, rb, name):
    ns, _, rh, C = g4.shape

    def body(c_ref, g_ref, r_ref, o_ref):
        o_ref[...] = (g_ref[0] + r_ref[...]).astype(BF16)

    return pl.pallas_call(
        body,
        grid_spec=pltpu.PrefetchScalarGridSpec(
            num_scalar_prefetch=1, grid=(ns, rh // rb),
            in_specs=[pl.BlockSpec((1, 1, rb, C), lambda s, i, cr: (s, cr[0], i, 0)),
                      pl.BlockSpec((1, rb, C), lambda s, i, cr: (s, i, 0))],
            out_specs=pl.BlockSpec((1, rb, C), lambda s, i, cr: (s, i, 0))),
        out_shape=SDS((ns, rh, C), BF16), compiler_params=_arb(2), name=name)(c_idx, g4, recv)


def _add_core_halves_in(g4, recv, c_idx, name):
    n_slabs, _, rh, C = g4.shape
    cb = 256
    per_slab, per_chip, n_blocks = C // cb, DIN // N_CHIPS // cb, DIN // cb

    def stored(s, k):
        sb = (per_chip * s + k + 4 * DH // cb) % n_blocks
        return sb // per_slab, sb % per_slab

    def body(c_ref, g_ref, r_ref, o_ref):
        o_ref[...] = (g_ref[0] + r_ref[...]).astype(BF16)

    return pl.pallas_call(
        body,
        grid_spec=pltpu.PrefetchScalarGridSpec(
            num_scalar_prefetch=1, grid=(N_CHIPS, per_chip),
            in_specs=[pl.BlockSpec((1, 1, rh, cb), lambda s, k, cr: (stored(s, k)[0], cr[0], 0, stored(s, k)[1])),
                      pl.BlockSpec((1, rh, cb), lambda s, k, cr: (stored(s, k)[0], 0, stored(s, k)[1]))],
            out_specs=pl.BlockSpec((1, rh, cb), lambda s, k, cr: (s, 0, k))),
        out_shape=SDS((N_CHIPS, rh, DIN // N_CHIPS), BF16), compiler_params=_arb(2), name=name)(c_idx, g4, recv)


def _slot_shapes(sums):
    return [SDS((3,) + s.shape[1:], s.dtype) for s in sums]


def _add_chips(own, slots, order, rb, name):
    _, rh, C = slots.shape

    def body(o_ref, own_ref, a_ref, b_ref, c_ref, d_ref, out_ref):
        mine = own_ref[0].astype(F32)
        t = [jnp.where(o_ref[i] == 0, mine, r[0].astype(F32)) for i, r in enumerate((a_ref, b_ref, c_ref, d_ref))]
        out_ref[...] = ((t[0] + t[1]) + t[2]) + t[3]

    def spec(i):
        return pl.BlockSpec((1, rb, C), lambda t, o: (jnp.maximum(o[i], 1) - 1, t, 0))

    return pl.pallas_call(
        body,
        grid_spec=pltpu.PrefetchScalarGridSpec(
            num_scalar_prefetch=1, grid=(rh // rb,),
            in_specs=[pl.BlockSpec((1, rb, C), lambda t, o: (o[4], t, 0)), spec(0), spec(1), spec(2), spec(3)],
            out_specs=pl.BlockSpec((rb, C), lambda t, o: (t, 0))),
        out_shape=SDS((rh, C), F32), compiler_params=_arb(), name=name)(order, own, slots, slots, slots, slots)


def _share_halves(halves):
    nw = len(halves)

    def body(*refs):
        ins, outs = refs[:nw], refs[nw:2 * nw]
        send_sems, recv_sems = refs[2 * nw:]
        x, y, c = _position()
        started = []
        for w in range(nw):
            cp = pltpu.make_async_remote_copy(
                src_ref=ins[w], dst_ref=outs[w], send_sem=send_sems.at[w], recv_sem=recv_sems.at[w],
                device_id=(x, y, 1 - c), device_id_type=MESH)
            cp.start()
            started.append(cp)
        for cp in started:
            cp.wait()

    anyspec = pl.BlockSpec(memory_space=pl.ANY)
    return pl.pallas_call(
        body, out_shape=[SDS(h.shape, F32) for h in halves], in_specs=[anyspec] * nw, out_specs=[anyspec] * nw,
        scratch_shapes=[pltpu.SemaphoreType.DMA((nw,)), pltpu.SemaphoreType.DMA((nw,))],
        name="share_halves")(*halves)


def _small_2d(b_ada, norm1_w, norm2_w, final_norm_w, v_ln_w, v_ln_b, lower_bounds, gn_w, b_s, w_s):
    return dict(zip(SMALL_NAMES, (b_ada, norm1_w, norm2_w, final_norm_w.reshape(1, D), v_ln_w, v_ln_b, lower_bounds, gn_w,
                                  b_s.reshape(NH, BLK), w_s.reshape(NH * BLK, BLK))))


def _small_original_shapes(d):
    out = dict(d)
    out['final_norm_w'] = d['final_norm_w'].reshape(D)
    out['b_s'] = d['b_s'].reshape(1, NH, BLK)
    out['w_s'] = d['w_s'].reshape(1, NH, BLK, BLK)
    return out


def _row_block(r):
    for cand in (256, 176, 128, 64, 32, 16, 8):
        if r % cand == 0:
            return cand
    return r


def kernel(x, c, w_ada, b_ada, norm1_w, w_in, w_s, b_s, v_ln_w, v_ln_b, lower_bounds, gn_w, w_out, norm2_w, w_ffn_in, w_ffn_out, final_norm_w, loss_target, m_w_ada, m_b_ada, m_norm1_w, m_w_in, m_w_s, m_b_s, m_v_ln_w, m_v_ln_b, m_lower_bounds, m_gn_w, m_w_out, m_norm2_w, m_w_ffn_in, m_w_ffn_out, m_final_norm_w, v_w_ada, v_b_ada, v_norm1_w, v_w_in, v_w_s, v_b_s, v_v_ln_w, v_v_ln_b, v_lower_bounds, v_gn_w, v_w_out, v_norm2_w, v_w_ffn_in, v_w_ffn_out, v_final_norm_w):
    T = x.shape[1]
    tm, tp = min(TOKEN_TILE, T), min(PROJ_TILE, T)
    px, py, pc = _position()
    chip = 2 * px + py
    me = 4 * px + 2 * py + pc
    x2d = x.reshape(T, D)
    tgt = loss_target.reshape(T, D)

    chip_idx = jnp.reshape(chip, (1,)).astype(jnp.int32)
    c_idx = jnp.reshape(pc, (1,)).astype(jnp.int32)
    (w_in_b,), (c_all,) = _gather_weights(
        [_place_shard(w_in[0], 1, chip_idx, "place_in")], [1], [jnp.broadcast_to(c, (8, D))], "gather_w_in_and_c")
    placed = [_place_shard(w_out[0], 0, chip_idx, "place_out"), _place_shard(w_ffn_in[0], 1, chip_idx, "place_ffn_in"),
              _place_shard(w_ffn_out[0], 0, chip_idx, "place_ffn_out")]

    cact, ada_part = _ada_forward(c_all.reshape(N_DEV, 8, D)[:, 0, :], w_ada[0])
    n_ada = ada_part.shape[1]
    (ada_all,) = _all_gather_rows([ada_part], "gather_ada")
    ada_all = ada_all.reshape(N_CHIPS, 2, N_DEV, n_ada)[:, 0]
    ada = lax.dynamic_index_in_dim(ada_all, me, axis=1, keepdims=False).reshape(1, 6 * D) + b_ada

    rr = lax.broadcasted_iota(jnp.int32, (BLK, BLK), 0) // CH
    cc = lax.broadcasted_iota(jnp.int32, (BLK, BLK), 1) // CH
    ws_b = jnp.where((rr >= cc)[None], w_s[0], 0.0).astype(BF16)
    bst = b_s[0].T
    lnw, lnb = v_ln_w, v_ln_b
    nw1, nw2, fw = norm1_w, norm2_w, final_norm_w.reshape(1, D)

    h1, proj = _proj_in(x2d, nw1, ada, ada, w_in_b, tp)
    ycat = _gmlp_fwd(proj, ws_b, bst, lnw, lnb)
    tables = _hgrn_tables()
    (ycat, o_pre, a_all, st_all), (w_out_b, w_fi_b, w_fo_b) = _hgrn_fwd(
        proj, lower_bounds, gn_w, ycat, tables, placed, [0, 1, 0])

    dycat, dx1, h2, act, dff, dgu, dmix, acc2 = _token_local(
        x2d, ycat, tgt, ada, nw2, ada, ada, ada, fw, w_out_b, w_fi_b, w_fo_b, tm)

    tt = min(WGRAD_TOKENS, T)
    order = jnp.concatenate([chip ^ jnp.arange(N_CHIPS, dtype=jnp.int32), chip_idx]).astype(jnp.int32)

    def by_core_half(g):
        return g.reshape(g.shape[0], 2, g.shape[1] // 2, g.shape[2])

    def core_sums(g4, recv, names):
        return [_add_core_halves(a, b, c_idx, _row_block(a.shape[2]), "add_core_" + n) for a, b, n in zip(g4, recv, names)]

    def chip_sums(sums, slots, names):
        return [_add_chips(o, s, order, _row_block(s.shape[1]), "add_chips_" + n) for o, s, n in zip(sums, slots, names)]

    g_out = _wgrad(ycat, dmix, D, D, tt, "wgrad_out").reshape(N_CHIPS, D // N_CHIPS, D)
    g_fi = _wgrad(h2, dgu, D, FFB, tt, "wgrad_ffn_in")
    g_fo = _wgrad(act, dff, FFB, D, tt, "wgrad_ffn_out").reshape(N_CHIPS, DFF // N_CHIPS, D)
    late_names = ["out", "ffn_in", "ffn_out"]
    late_g4 = [by_core_half(g) for g in (g_out, g_fi, g_fo)]

    (dproj, dws, dbs, dln), late_recv = _gmlp_bwd(proj, dycat, ws_b, bst, lnw, lnb, late_g4)
    late_sums = core_sums(late_g4, late_recv, late_names)
    (dproj, dlb, dgn), late_slots = _hgrn_bwd(
        proj, o_pre, a_all, st_all, dycat, lower_bounds, gn_w, dproj, tables, late_sums)

    g_in = _wgrad(h1, dproj, D, D, tt, "wgrad_in")
    in_g4 = [by_core_half(g_in)]
    (in_recv,) = _exchange_core_halves(in_g4, "exchange_core_halves_in")
    in_sums = [_add_core_halves_in(in_g4[0], in_recv, c_idx, "add_core_in")]
    (grad_x, acc1), in_slots = _proj_in_bwd(dproj, x2d, dx1, nw1, ada, w_in_b, tp, in_sums)
    names = ["in"] + late_names
    halves = chip_sums(in_sums, in_slots, ["in"]) + chip_sums(late_sums, late_slots, late_names)
    sibling_halves = _share_halves(halves)

    big_w = [(w_in, m_w_in, v_w_in), (w_out, m_w_out, v_w_out), (w_ffn_in, m_w_ffn_in, v_w_ffn_in),
             (w_ffn_out, m_w_ffn_out, v_w_ffn_out)]
    big_out = []
    for mine, sib, (w, m, v), n in zip(halves, sibling_halves, big_w, names):
        res = _adamw_halves(w[0], mine, sib, m[0], v[0], c_idx, _row_block(mine.shape[0]), "adamw_" + n)
        big_out.append([r[None] for r in res])

    gathered = _all_gather_rows([acc1, acc2, dln, dlb, dgn, dbs, dws], "gather_small")
    small, loss, dada_all = _small_finalize(
        gathered,
        _small_2d(b_ada, norm1_w, norm2_w, final_norm_w, v_ln_w, v_ln_b, lower_bounds, gn_w, b_s, w_s),
        _small_2d(m_b_ada, m_norm1_w, m_norm2_w, m_final_norm_w, m_v_ln_w, m_v_ln_b, m_lower_bounds, m_gn_w, m_b_s, m_w_s),
        _small_2d(v_b_ada, v_norm1_w, v_norm2_w, v_final_norm_w, v_v_ln_w, v_v_ln_b, v_lower_bounds, v_gn_w, v_b_s, v_w_s))
    small = [_small_original_shapes(d) for d in small]
    loss = loss.reshape(())

    ada_out = [o[None] for o in _ada_wgrad_adam(cact.T, dada_all, w_ada[0], m_w_ada[0], v_w_ada[0], chip_idx)]

    order_names = ['w_ada', 'b_ada', 'norm1_w', 'w_in', 'w_s', 'b_s', 'v_ln_w', 'v_ln_b', 'lower_bounds', 'gn_w',
                   'w_out', 'norm2_w', 'w_ffn_in', 'w_ffn_out', 'final_norm_w']
    big_idx = {'w_in': 0, 'w_out': 1, 'w_ffn_in': 2, 'w_ffn_out': 3}
    outs = [loss, grad_x.reshape(1, T, D)]
    for kind in range(4):
        for n in order_names:
            if n == 'w_ada':
                outs.append(ada_out[kind])
            elif n in big_idx:
                outs.append(big_out[big_idx[n]][kind])
            else:
                outs.append(small[kind][n])
    return tuple(outs)
```

```python
import functools

import jax
import jax.numpy as jnp
import numpy as np
from jax import lax
from jax.experimental import pallas as pl
from jax.experimental.pallas import tpu as pltpu

F32 = jnp.float32
BF16 = jnp.bfloat16
SDS = jax.ShapeDtypeStruct
MESH = pl.DeviceIdType.MESH
HIGHEST = lax.Precision.HIGHEST

D = 1024
DG = 512
DH = 512
NH = 4
HD = 128
BLK = 128
CH = 64
DFF = 2816
DIN = 3072
FFB = 1408
LEVELS = (64, 32, 16, 8, 4, 2)
HGRN_CHUNKS_PER_STEP = 8
GMLP_ROWS_PER_STEP = 1024
TOKEN_TILE = 256
PROJ_TILE = 1024
WGRAD_TOKENS = 2048
N_CHIPS = 4
N_DEV = 8
EPS = 1e-6
LR, B1, B2, AEPS, WD, STEP = 0.001, 0.9, 0.999, 1e-08, 0.01, 10

NT = (((1,), (1,)), ((), ()))
TN = (((0,), (0,)), ((), ()))


def _full(shape):
    nd = len(shape)
    return pl.BlockSpec(shape, lambda *_: (0,) * nd)


ADA_SH1, ADA_SC1, ADA_G1, ADA_SH2, ADA_SC2, ADA_G2 = range(6)


def _ada_part(k):
    return pl.BlockSpec((1, D), lambda *_: (0, k))


def _resident(shape):
    nd = len(shape)
    return pl.BlockSpec(shape, lambda *_: (0,) * nd, pipeline_mode=pl.Buffered(1))


def _arb(n=1):
    return pltpu.CompilerParams(dimension_semantics=("arbitrary",) * n)


def _dot(a, b, dims=None, precision=None):
    if dims is None:
        return jnp.dot(a, b, preferred_element_type=F32, precision=precision)
    return lax.dot_general(a, b, dims, preferred_element_type=F32, precision=precision)


def _sigmoid(x):
    return jax.nn.sigmoid(x)


def _gelu_parts(x):
    cdf = 0.5 * (1.0 + lax.erf(x * 0.7071067811865476))
    pdf = jnp.exp(-0.5 * x * x) * 0.3989422804014327
    return x * cdf, cdf + x * pdf


def _rms(x):
    return lax.rsqrt(jnp.mean(x * x, axis=-1, keepdims=True) + EPS)


def _rms_bwd(xhat, r, gw):
    return r * (gw - xhat * jnp.mean(xhat * gw, axis=-1, keepdims=True))


def _lower_bound(lbp_ref):
    l0, l1 = lbp_ref[0:1, :], lbp_ref[1:2, :]
    m = jnp.maximum(l0, l1)
    e0, e1 = jnp.exp(l0 - m), jnp.exp(l1 - m)
    return e0 / (e0 + e1), e1 / (e0 + e1)


def _proj_in(x, nw, sc, sh, w_in_b, tm):
    T = x.shape[0]

    def body(x_ref, nw_ref, sc_ref, sh_ref, w_ref, h_ref, p_ref):
        xv = x_ref[...]
        h = ((xv * _rms(xv)) * nw_ref[...]) * (1.0 + sc_ref[...]) + sh_ref[...]
        hb = h.astype(BF16)
        h_ref[...] = hb
        p_ref[...] = _dot(hb, w_ref[...])

    row = lambda i: (i, 0)
    return pl.pallas_call(
        body, grid=(T // tm,),
        in_specs=[pl.BlockSpec((tm, D), row), _full((1, D)), _ada_part(ADA_SC1), _ada_part(ADA_SH1), _resident((D, DIN))],
        out_specs=[pl.BlockSpec((tm, D), row), pl.BlockSpec((tm, DIN), row)],
        out_shape=[SDS((T, D), BF16), SDS((T, DIN), F32)],
        compiler_params=_arb(), name="proj_in")(x, nw, sc, sh, w_in_b)


def _gmlp_common(u, v, lnw, lnb, ws_ref, bst_ref):
    ug, dug = _gelu_parts(u)
    vg, dvg = _gelu_parts(v)
    mu = jnp.mean(vg, axis=-1, keepdims=True)
    vc = vg - mu
    rstd = lax.rsqrt(jnp.mean(vc * vc, axis=-1, keepdims=True) + EPS)
    vhat = vc * rstd
    vn = vhat * lnw + lnb
    vnb = vn.astype(BF16)
    mixed = []
    for h in range(NH):
        sl = slice(h * HD, (h + 1) * HD)
        mixed.append(_dot(ws_ref[h], vnb[:, sl]) + bst_ref[:, h:h + 1])
    return ug, dug, dvg, rstd, vhat, vnb, jnp.concatenate(mixed, axis=1)


def _gmlp_fwd(proj, ws_b, bst, lnw, lnb):
    T = proj.shape[0]
    rows = min(GMLP_ROWS_PER_STEP, T)

    def body(u_ref, v_ref, ws_ref, bst_ref, lnw_ref, lnb_ref, y_ref):
        for bi in range(rows // BLK):
            rs = slice(bi * BLK, (bi + 1) * BLK)
            ug, _, _, _, _, _, mixed = _gmlp_common(u_ref[rs, :], v_ref[rs, :], lnw_ref[...], lnb_ref[...], ws_ref, bst_ref)
            y_ref[rs, :] = (ug * mixed).astype(BF16)

    return pl.pallas_call(
        body, grid=(T // rows,),
        in_specs=[pl.BlockSpec((rows, DG), lambda i: (i, 0)), pl.BlockSpec((rows, DG), lambda i: (i, 1)),
                  _full((NH, BLK, BLK)), _full((BLK, NH)), _full((1, DG)), _full((1, DG))],
        out_specs=pl.BlockSpec((rows, DG), lambda i: (i, 0)),
        out_shape=SDS((T, D), BF16),
        compiler_params=_arb(), name="gmlp_fwd")(proj, proj, ws_b, bst, lnw, lnb)


def _hgrn_tables():
    t = np.arange(CH)[:, None]
    j = np.arange(CH)[None, :]
    blocks = [j <= t, j > t]
    masks = []
    for n in LEVELS:
        mid = t - t % n + n // 2
        blocks.append(np.where(t >= mid, (j >= mid) & (j <= t), (j > t) & (j < mid)))
        masks.append((t // n == j // n) & (t % n >= n // 2) & (j % n < n // 2))
    w = np.concatenate(blocks, axis=0).astype(np.float32)
    m = np.stack(masks).astype(np.float32)
    return (jnp.asarray(w, BF16), jnp.asarray(w.T, BF16), jnp.asarray(m), jnp.asarray(m + m.transpose(0, 2, 1)))


def _split_dot(w, x, parts):
    acc = None
    for _ in range(parts):
        piece = x.astype(BF16)
        term = _dot(w, piece)
        acc = term if acc is None else acc + term
        x = x - piece.astype(F32)
    return acc


def _hgrn_decays(f, w_ref):
    b = _split_dot(w_ref[0:CH, :], jnp.log(f), 3)
    row = lax.broadcasted_iota(jnp.int32, (CH, 1), 0)
    blocks = [jnp.exp(b), jnp.exp(b[CH - 1:CH, :] - b)]
    for n in LEVELS:
        up = (row & (n // 2)) != 0
        if n >= 8:
            ref = b.reshape(CH // n, n, DH)[:, n // 2 - 1:n // 2, :]
            ref = jnp.broadcast_to(ref, (CH // n, n, DH)).reshape(CH, DH)
            blocks.append(jnp.exp(jnp.where(up, b - ref, ref - b)))
        elif n == 4:
            r4 = row & 3
            two = jnp.where(r4 == 3, pltpu.roll(f, 1, 0) * f, 1.0)
            blocks.append(jnp.where(r4 == 0, pltpu.roll(f, CH - 1, 0), jnp.where(r4 == 2, f, two)))
        else:
            blocks.append(jnp.where(up, f, 1.0))
    return blocks


def _hgrn_gates(q, fl, lb, omlb, w_ref):
    sq = _sigmoid(q)
    qf = q * sq
    sig = _sigmoid(fl)
    f = lb + omlb * sig
    k = 1.0 - f
    return sq, qf, sig, f, k, _hgrn_decays(f, w_ref)


def _level_factor(e, li, sl, row, qh, kh):
    el = e[2 + li][:, sl]
    up = (row & (LEVELS[li] // 2)) != 0
    return el, up, el * jnp.where(up, qh, kh)


def _hgrn_fwd(proj, lower_bounds, gn_w, ycat, tables, placed, axes):
    T = proj.shape[0]
    nc = T // CH
    nch = min(HGRN_CHUNKS_PER_STEP, nc)
    steps = nc // nch
    w_st, _, masks, _ = tables
    nw = len(placed)
    pass_step = (13 * steps) // 16

    def body(*refs):
        q_ref, f_ref, i_ref, g_ref, lbp_ref, gn_ref, w_ref, m_ref = refs[:8]
        y_ref, o_ref, a_ref, st_ref = refs[9 + nw:13 + nw]
        s_scr, send_sems, recv_sems = refs[13 + 2 * nw:]
        gather = _WeightGather(refs[13 + nw:13 + 2 * nw], axes, send_sems, recv_sems)
        step = pl.program_id(0)

        @pl.when(step == 0)
        def _():
            gather.start()
            s_scr[...] = jnp.zeros_like(s_scr)

        @pl.when(step == pass_step)
        def _():
            gather.forward()

        lb, omlb = _lower_bound(lbp_ref)
        row = lax.broadcasted_iota(jnp.int32, (CH, 1), 0)
        eye = lax.broadcasted_iota(jnp.int32, (CH, CH), 0) == lax.broadcasted_iota(jnp.int32, (CH, CH), 1)
        in_level = [m_ref[li] > 0.0 for li in range(len(LEVELS))]
        pre = []
        for ci in range(nch):
            rs = slice(ci * CH, (ci + 1) * CH)
            _, qf, _, _, k, e = _hgrn_gates(q_ref[rs, :], f_ref[rs, :], lb, omlb, w_ref)
            mats = []
            for h in range(NH):
                sl = slice(h * HD, (h + 1) * HD)
                qh, kh = qf[:, sl], k[:, sl]
                a = jnp.where(eye, jnp.sum(qh * kh, axis=-1, keepdims=True), 0.0)
                for li in range(len(LEVELS)):
                    _, _, y = _level_factor(e, li, sl, row, qh, kh)
                    yb = y.astype(BF16)
                    a = jnp.where(in_level[li], _dot(yb, yb, NT), a)
                a_ref[ci, h] = a
                mats.append(a.astype(BF16))
            eb = e[0]
            pre.append(((qf * eb).astype(BF16), eb[CH - 1:CH, :], (k * e[1]).astype(BF16), mats))
        for ci in range(nch):
            rs = slice(ci * CH, (ci + 1) * CH)
            qe, ebl, kd, mats = pre[ci]
            v = i_ref[rs, :]
            g = g_ref[rs, :]
            for h in range(NH):
                sl = slice(h * HD, (h + 1) * HD)
                st0 = s_scr[h]
                st_ref[ci, h] = st0
                vb = v[:, sl].astype(BF16)
                o = _dot(qe[:, sl], st0.astype(BF16), NT) + _dot(mats[h], vb)
                s_scr[h] = st0 * ebl[:, sl] + _dot(vb, kd[:, sl], TN)
                o_ref[rs, sl] = o
                gh = g[:, sl]
                y_ref[rs, sl] = (((o * _rms(o)) * gn_ref[...]) * (gh * _sigmoid(gh))).astype(BF16)

        @pl.when(step == steps - 1)
        def _():
            gather.finish()

    blk = lambda j: pl.BlockSpec((nch * CH, DH), lambda c: (c, j))
    anyspec = pl.BlockSpec(memory_space=pl.ANY)
    res = pl.pallas_call(
        body, grid=(steps,),
        in_specs=[blk(2), blk(3), blk(4), blk(5), _full((2, DH)), _full((1, HD)),
                  _full(w_st.shape), _full(masks.shape), anyspec] + [anyspec] * nw,
        out_specs=[pl.BlockSpec((nch * CH, DH), lambda c: (c, 1)),
                   pl.BlockSpec((nch * CH, DH), lambda c: (c, 0)),
                   pl.BlockSpec((nch, NH, CH, CH), lambda c: (c, 0, 0, 0)),
                   pl.BlockSpec((nch, NH, HD, HD), lambda c: (c, 0, 0, 0))] + [anyspec] * nw,
        out_shape=[SDS((T, D), BF16), SDS((T, DH), F32), SDS((nc, NH, CH, CH), F32), SDS((nc, NH, HD, HD), F32)]
        + [SDS(a.shape, a.dtype) for a in placed],
        scratch_shapes=[pltpu.VMEM((NH, HD, HD), F32)] + _gather_sems(nw),
        input_output_aliases={8: 0, **{9 + i: 4 + i for i in range(nw)}},
        compiler_params=_arb(), name="hgrn_fwd")(proj, proj, proj, proj, lower_bounds, gn_w, w_st, masks, ycat, *placed)
    return res[:4], res[4:]


def _token_local(x, ycat, tgt, g1, nw2, sc2, sh2, g2, fw, w_out_b, w_fi_b, w_fo_b, tm):
    T = x.shape[0]
    inv_d = 1.0 / D

    def body(x_ref, y_ref, t_ref, g1_ref, nw2_ref, sc2_ref, sh2_ref, g2_ref, fw_ref, wo_ref, wfi_ref, wfo_ref,
             dy_ref, dx1_ref, h2_ref, act_ref, dff_ref, dgu_ref, dmix_ref, acc_ref):
        @pl.when(pl.program_id(0) == 0)
        def _():
            acc_ref[...] = jnp.zeros_like(acc_ref)

        def acc(row, val):
            acc_ref[row:row + 1, :] += jnp.sum(val, axis=0, keepdims=True)

        g1v, g2v = g1_ref[...], g2_ref[...]
        mix = _dot(y_ref[...], wo_ref[...])
        x1 = x_ref[...] + g1v * mix
        r2 = _rms(x1)
        xh2 = x1 * r2
        n2 = xh2 * nw2_ref[...]
        osc2 = 1.0 + sc2_ref[...]
        h2b = (n2 * osc2 + sh2_ref[...]).astype(BF16)
        h2_ref[...] = h2b
        ff = jnp.zeros((tm, D), F32)
        saved = []
        for kb in range(DFF // FFB):
            gate = _dot(h2b, wfi_ref[:, kb * FFB:(kb + 1) * FFB])
            up = _dot(h2b, wfi_ref[:, DFF + kb * FFB:DFF + (kb + 1) * FFB])
            sg = _sigmoid(gate)
            actb = (gate * sg * up).astype(BF16)
            act_ref[:, kb * FFB:(kb + 1) * FFB] = actb
            ff = ff + _dot(actb, wfo_ref[kb * FFB:(kb + 1) * FFB, :])
            saved.append((gate, up, sg))
        x2 = x1 + g2v * ff
        r3 = _rms(x2)
        xh3 = x2 * r3
        err = xh3 * fw_ref[...] - t_ref[...]
        acc(6, (0.5 * inv_d) * err * err)
        dy = err * inv_d
        acc(4, dy * xh3)
        dx2 = _rms_bwd(xh3, r3, dy * fw_ref[...])
        acc(0, dx2 * ff)
        dffb = (dx2 * g2v).astype(BF16)
        dff_ref[...] = dffb
        dh2 = jnp.zeros((tm, D), F32)
        for kb in range(DFF // FFB):
            gate, up, sg = saved[kb]
            da = _dot(dffb, wfo_ref[kb * FFB:(kb + 1) * FFB, :], NT)
            dgate = (da * up * (sg * (1.0 + gate * (1.0 - sg)))).astype(BF16)
            dup = (da * gate * sg).astype(BF16)
            dgu_ref[:, kb * FFB:(kb + 1) * FFB] = dgate
            dgu_ref[:, DFF + kb * FFB:DFF + (kb + 1) * FFB] = dup
            dh2 = dh2 + _dot(dgate, wfi_ref[:, kb * FFB:(kb + 1) * FFB], NT)
            dh2 = dh2 + _dot(dup, wfi_ref[:, DFF + kb * FFB:DFF + (kb + 1) * FFB], NT)
        acc(2, dh2)
        acc(1, dh2 * n2)
        dn2 = dh2 * osc2
        acc(3, dn2 * xh2)
        dx1 = dx2 + _rms_bwd(xh2, r2, dn2 * nw2_ref[...])
        acc(5, dx1 * mix)
        dmixb = (dx1 * g1v).astype(BF16)
        dmix_ref[...] = dmixb
        dy_ref[...] = _dot(dmixb, wo_ref[...], NT)
        dx1_ref[...] = dx1

    row = lambda i: (i, 0)
    vec = _full((1, D))
    return pl.pallas_call(
        body, grid=(T // tm,),
        in_specs=[pl.BlockSpec((tm, D), row), pl.BlockSpec((tm, D), row), pl.BlockSpec((tm, D), row),
                  _ada_part(ADA_G1), vec, _ada_part(ADA_SC2), _ada_part(ADA_SH2), _ada_part(ADA_G2), vec,
                  _resident((D, D)), _resident((D, 2 * DFF)), _resident((DFF, D))],
        out_specs=[pl.BlockSpec((tm, D), row), pl.BlockSpec((tm, D), row), pl.BlockSpec((tm, D), row),
                   pl.BlockSpec((tm, DFF), row), pl.BlockSpec((tm, D), row), pl.BlockSpec((tm, 2 * DFF), row),
                   pl.BlockSpec((tm, D), row), _full((8, D))],
        out_shape=[SDS((T, D), F32), SDS((T, D), F32), SDS((T, D), BF16), SDS((T, DFF), BF16), SDS((T, D), BF16),
                   SDS((T, 2 * DFF), BF16), SDS((T, D), BF16), SDS((8, D), F32)],
        compiler_params=_arb(), name="token_local")(x, ycat, tgt, g1, nw2, sc2, sh2, g2, fw, w_out_b, w_fi_b, w_fo_b)


def _gmlp_bwd(proj, dycat, ws_b, bst, lnw, lnb, grads):
    T = proj.shape[0]
    rows = min(GMLP_ROWS_PER_STEP, T)
    nb = T // rows
    nw = len(grads)

    def body(*refs):
        u_ref, v_ref, dy_ref, ws_ref, bst_ref, lnw_ref, lnb_ref = refs[:7]
        dp_ref, dws_ref, dbs_ref, dln_ref = refs[7 + nw:11 + nw]
        dbs_acc, send_sems, recv_sems = refs[11 + 2 * nw:]
        exchange = _CoreExchange(refs[7:7 + nw], refs[11 + nw:11 + 2 * nw], send_sems, recv_sems)
        i = pl.program_id(0)

        @pl.when(i == 0)
        def _():
            exchange.start()
            dws_ref[...] = jnp.zeros_like(dws_ref)
            dln_ref[...] = jnp.zeros_like(dln_ref)
            dbs_acc[...] = jnp.zeros_like(dbs_acc)

        r = lax.broadcasted_iota(jnp.int32, (BLK, BLK), 0) // CH
        c = lax.broadcasted_iota(jnp.int32, (BLK, BLK), 1) // CH
        for bi in range(rows // BLK):
            rs = slice(bi * BLK, (bi + 1) * BLK)
            ug, dug, dvg, rstd, vhat, vnb, mixed = _gmlp_common(
                u_ref[rs, :], v_ref[rs, :], lnw_ref[...], lnb_ref[...], ws_ref, bst_ref)
            dya = dy_ref[rs, :]
            dp_ref[rs, 0:DG] = (dya * mixed * dug).astype(BF16)
            dmixed = dya * ug
            dbs_acc[...] += dmixed
            dmb = dmixed.astype(BF16)
            dvn = []
            for h in range(NH):
                sl = slice(h * HD, (h + 1) * HD)
                dws_ref[h * BLK:(h + 1) * BLK, :] += jnp.where(r >= c, _dot(dmb[:, sl], vnb[:, sl], NT), 0.0)
                dvn.append(_dot(ws_ref[h], dmb[:, sl], TN))
            dvn = jnp.concatenate(dvn, axis=1)
            dln_ref[0:1, :] += jnp.sum(dvn * vhat, axis=0, keepdims=True)
            dln_ref[1:2, :] += jnp.sum(dvn, axis=0, keepdims=True)
            dvh = dvn * lnw_ref[...]
            dvgel = rstd * (dvh - jnp.mean(dvh, axis=-1, keepdims=True) - vhat * jnp.mean(dvh * vhat, axis=-1, keepdims=True))
            dp_ref[rs, DG:2 * DG] = (dvgel * dvg).astype(BF16)

        @pl.when(i == nb - 1)
        def _():
            head = lax.broadcasted_iota(jnp.int32, (8, BLK), 0)
            ones = jnp.ones((8, HD), F32)
            out = jnp.zeros((8, BLK), F32)
            for h in range(NH):
                sums = _dot(ones, dbs_acc[:, h * HD:(h + 1) * HD], NT, precision=HIGHEST)
                out = out + jnp.where(head == h, sums, 0.0)
            dbs_ref[...] = out
            exchange.finish()

    anyspec = pl.BlockSpec(memory_space=pl.ANY)
    res = pl.pallas_call(
        body, grid=(nb,),
        in_specs=[pl.BlockSpec((rows, DG), lambda i: (i, 0)), pl.BlockSpec((rows, DG), lambda i: (i, 1)),
                  pl.BlockSpec((rows, DG), lambda i: (i, 0)),
                  _full((NH, BLK, BLK)), _full((BLK, NH)), _full((1, DG)), _full((1, DG))] + [anyspec] * nw,
        out_specs=[pl.BlockSpec((rows, 2 * DG), lambda i: (i, 2)), _full((NH * BLK, BLK)), _full((8, BLK)), _full((8, DG))]
        + [anyspec] * nw,
        out_shape=[SDS((T, DIN), BF16), SDS((NH * BLK, BLK), F32), SDS((8, BLK), F32), SDS((8, DG), F32)]
        + _core_exchange_shapes(grads),
        scratch_shapes=[pltpu.VMEM((BLK, DG), F32)] + _core_exchange_sems(nw),
        compiler_params=_arb(), name="gmlp_bwd")(proj, proj, dycat, ws_b, bst, lnw, lnb, *grads)
    return res[:4], res[4:]


def _hgrn_bwd(proj, o_pre, a_all, st_all, dycat, lower_bounds, gn_w, dproj, tables, sums):
    T = proj.shape[0]
    nc = T // CH
    nch = min(HGRN_CHUNKS_PER_STEP, nc)
    steps = nc // nch
    w_st, w_st_t, _, masks_sym = tables
    n_lev = len(LEVELS)
    nw = len(sums)

    def body(*refs):
        q_ref, f_ref, i_ref, g_ref, o_ref, a_ref, st_ref, dy_ref, lbp_ref, gn_ref, w_ref, wt_ref, ms_ref = refs[:13]
        dp_ref, dlb_ref, dgn_ref = refs[14 + nw:17 + nw]
        ds_scr, dx_scr, send_sems, recv_sems = refs[17 + 2 * nw:]
        exchange = _ChipExchange(refs[14:14 + nw], refs[17 + nw:17 + 2 * nw], send_sems, recv_sems)
        i = pl.program_id(0)

        @pl.when(i == 0)
        def _():
            exchange.start()
            ds_scr[...] = jnp.zeros_like(ds_scr)
            dlb_ref[...] = jnp.zeros_like(dlb_ref)
            dgn_ref[...] = jnp.zeros_like(dgn_ref)

        lb, omlb = _lower_bound(lbp_ref)
        row = lax.broadcasted_iota(jnp.int32, (CH, 1), 0)
        eye = lax.broadcasted_iota(jnp.int32, (CH, CH), 0) == lax.broadcasted_iota(jnp.int32, (CH, CH), 1)
        lower = lax.broadcasted_iota(jnp.int32, (CH, CH), 0) > lax.broadcasted_iota(jnp.int32, (CH, CH), 1)
        dgn = jnp.zeros((1, HD), F32)
        pre = []
        for ci in range(nch):
            rs = slice(ci * CH, (ci + 1) * CH)
            q = q_ref[rs, :]
            v = i_ref[rs, :]
            g = g_ref[rs, :]
            sq, qf, sig, f, k, e = _hgrn_gates(q, f_ref[rs, :], lb, omlb, w_ref)
            eb = e[0]
            ekd = e[1]
            kd = k * ekd
            qe = qf * eb
            dob_h, dqe_h, dqf_h, dki_h, dv_h, dg_h = [], [], [], [], [], []
            for h in range(NH):
                sl = slice(h * HD, (h + 1) * HD)
                o = o_ref[rs, sl]
                ro = _rms(o)
                oh = o * ro
                gh = g[:, sl]
                sg = _sigmoid(gh)
                dyb = dy_ref[rs, sl]
                dg_h.append(dyb * (oh * gn_ref[...]) * (sg * (1.0 + gh * (1.0 - sg))))
                don = dyb * (gh * sg)
                dgn = dgn + jnp.sum(don * oh, axis=0, keepdims=True)
                dob = _rms_bwd(oh, ro, don * gn_ref[...]).astype(BF16)
                vb = v[:, sl].astype(BF16)
                qh, kh = qf[:, sl], k[:, sl]
                dqe = _dot(dob, st_ref[ci, h].astype(BF16))
                da = _dot(dob, vb, NT)
                ddiag = jnp.sum(jnp.where(eye, da, 0.0), axis=-1, keepdims=True)
                dsym = jnp.where(lower, da, _dot(vb, dob, NT))
                upper_part = jnp.zeros((CH, HD), F32)
                both = jnp.zeros((CH, HD), F32)
                for li in range(n_lev):
                    el, up, y = _level_factor(e, li, sl, row, qh, kh)
                    dyv = _dot((ms_ref[li] * dsym).astype(BF16), y.astype(BF16))
                    dx_scr[ci, (2 + li) * CH:(3 + li) * CH, sl] = dyv * y
                    dye = dyv * el
                    upper_part = upper_part + jnp.where(up, dye, 0.0)
                    both = both + dye
                dob_h.append(dob)
                dqe_h.append(dqe)
                dqf_h.append(dqe * eb[:, sl] + ddiag * kh + upper_part)
                dki_h.append(ddiag * qh + (both - upper_part))
                dv_h.append(_dot(a_ref[ci, h].astype(BF16), dob, TN))
            dp_ref[rs, 0:DH] = (jnp.concatenate(dqf_h, axis=1) * (sq * (1.0 + q * (1.0 - sq)))).astype(BF16)
            dp_ref[rs, 3 * DH:4 * DH] = jnp.concatenate(dg_h, axis=1).astype(BF16)
            pre.append((v, sig, f, eb, ekd, kd, qe, dob_h, jnp.concatenate(dqe_h, axis=1), dki_h, dv_h))
        dgn_ref[0:1, :] += dgn
        for ci in reversed(range(nch)):
            rs = slice(ci * CH, (ci + 1) * CH)
            v, sig, f, eb, ekd, kd, qe, dob_h, dqe, dki_h, dv_h = pre[ci]
            ebl = eb[CH - 1:CH, :]
            dbl_h, dkd_h, dv2_h = [], [], []
            for h in range(NH):
                sl = slice(h * HD, (h + 1) * HD)
                dst1 = ds_scr[h]
                dst1b = dst1.astype(BF16)
                ds_scr[h] = dst1 * ebl[:, sl] + _dot(dob_h[h], qe[:, sl].astype(BF16), TN)
                dbl_h.append(ebl[:, sl] * jnp.sum(st_ref[ci, h] * dst1, axis=0, keepdims=True))
                dkd_h.append(_dot(v[:, sl].astype(BF16), dst1b))
                dv2_h.append(dv_h[h] + _dot(kd[:, sl].astype(BF16), dst1b, NT))
            dkd = jnp.concatenate(dkd_h, axis=1)
            dx_scr[ci, 0:CH, :] = dqe * qe + jnp.where(row == CH - 1, jnp.concatenate(dbl_h, axis=1), 0.0)
            dx_scr[ci, CH:2 * CH, :] = dkd * kd
            dlf = _split_dot(wt_ref[...], dx_scr[ci], 2)
            df = dlf / f - (dkd * ekd + jnp.concatenate(dki_h, axis=1))
            dlb_ref[0:1, :] += jnp.sum(df * (1.0 - sig), axis=0, keepdims=True)
            dp_ref[rs, DH:2 * DH] = (df * omlb * sig * (1.0 - sig)).astype(BF16)
            dp_ref[rs, 2 * DH:3 * DH] = jnp.concatenate(dv2_h, axis=1).astype(BF16)

        @pl.when(i == steps - 1)
        def _():
            gl = dlb_ref[0:1, :] * lb * omlb
            dlb_ref[0:1, :] = gl
            dlb_ref[1:2, :] = -gl
            exchange.finish()

    rev = lambda j: pl.BlockSpec((nch * CH, DH), lambda c: (steps - 1 - c, j))
    anyspec = pl.BlockSpec(memory_space=pl.ANY)
    res = pl.pallas_call(
        body, grid=(steps,),
        in_specs=[rev(2), rev(3), rev(4), rev(5), rev(0),
                  pl.BlockSpec((nch, NH, CH, CH), lambda c: (steps - 1 - c, 0, 0, 0)),
                  pl.BlockSpec((nch, NH, HD, HD), lambda c: (steps - 1 - c, 0, 0, 0)),
                  rev(1), _full((2, DH)), _full((1, HD)),
                  _full(w_st.shape), _full(w_st_t.shape), _full(masks_sym.shape),
                  anyspec] + [anyspec] * nw,
        out_specs=[pl.BlockSpec((nch * CH, 4 * DH), lambda c: (steps - 1 - c, 0)), _full((8, DH)), _full((8, HD))]
        + [anyspec] * nw,
        out_shape=[SDS((T, DIN), BF16), SDS((8, DH), F32), SDS((8, HD), F32)] + _slot_shapes(sums),
        scratch_shapes=[pltpu.VMEM((NH, HD, HD), F32), pltpu.VMEM((nch, (2 + n_lev) * CH, DH), F32)] + _exchange_sems(nw),
        input_output_aliases={13: 0},
        compiler_params=_arb(), name="hgrn_bwd")(proj, proj, proj, proj, o_pre, a_all, st_all, dycat, lower_bounds, gn_w,
                                                 w_st, w_st_t, masks_sym, dproj, *sums)
    return res[:3], res[3:]


def _proj_in_bwd(dproj, x, dx1, nw, sc, w_in_b, tm, sums):
    T = x.shape[0]
    ns = len(sums)
    steps = T // tm

    def body(*refs):
        dp_ref, x_ref, dx1_ref, nw_ref, sc_ref, w_ref = refs[:6]
        gx_ref, acc_ref = refs[6 + ns:8 + ns]
        exchange = _ChipExchange(refs[6:6 + ns], refs[8 + ns:8 + 2 * ns], *refs[8 + 2 * ns:])

        @pl.when(pl.program_id(0) == 0)
        def _():
            exchange.start()
            acc_ref[...] = jnp.zeros_like(acc_ref)

        dh = _dot(dp_ref[:, 0:4 * DH], w_ref[:, 2 * DG:DIN], NT) + _dot(dp_ref[:, 4 * DH:DIN], w_ref[:, 0:2 * DG], NT)
        xv = x_ref[...]
        r = _rms(xv)
        xh = xv * r
        n1 = xh * nw_ref[...]
        acc_ref[0:1, :] += jnp.sum(dh, axis=0, keepdims=True)
        acc_ref[1:2, :] += jnp.sum(dh * n1, axis=0, keepdims=True)
        dn = dh * (1.0 + sc_ref[...])
        acc_ref[2:3, :] += jnp.sum(dn * xh, axis=0, keepdims=True)
        gx_ref[...] = dx1_ref[...] + _rms_bwd(xh, r, dn * nw_ref[...])

        @pl.when(pl.program_id(0) == steps - 1)
        def _():
            exchange.finish()

    row = lambda i: (i, 0)
    anyspec = pl.BlockSpec(memory_space=pl.ANY)
    res = pl.pallas_call(
        body, grid=(steps,),
        in_specs=[pl.BlockSpec((tm, DIN), row), pl.BlockSpec((tm, D), row), pl.BlockSpec((tm, D), row),
                  _full((1, D)), _ada_part(ADA_SC1), _resident((D, DIN))] + [anyspec] * ns,
        out_specs=[pl.BlockSpec((tm, D), row), _full((8, D))] + [anyspec] * ns,
        out_shape=[SDS((T, D), F32), SDS((8, D), F32)] + _slot_shapes(sums),
        scratch_shapes=_exchange_sems(ns),
        compiler_params=_arb(), name="proj_in_bwd")(dproj, x, dx1, nw, sc, w_in_b, *sums)
    return res[:2], res[2:]


def _wgrad(a, b, bk, bn, tt, name, bf16_copy=False):
    T, K = a.shape
    N = b.shape[1]
    nn, nk, nt = N // bn, K // bk, T // tt
    bmap = lambda n, k, t: (t, n)

    def body(a_ref, b_ref, o_ref, *copy_ref):
        @pl.when(pl.program_id(2) == 0)
        def _():
            o_ref[...] = jnp.zeros_like(o_ref)

        o_ref[0] += _dot(a_ref[...], b_ref[...], TN)

        if bf16_copy:
            @pl.when(pl.program_id(2) == nt - 1)
            def _():
                copy_ref[0][...] = o_ref[...].astype(BF16)

    ospec = pl.BlockSpec((1, bk, bn), lambda n, k, t: (n, k, 0))
    return pl.pallas_call(
        body, grid=(nn, nk, nt),
        in_specs=[pl.BlockSpec((tt, bk), lambda n, k, t: (t, k)), pl.BlockSpec((tt, bn), bmap)],
        out_specs=[ospec, ospec] if bf16_copy else ospec,
        out_shape=[SDS((nn, K, bn), F32), SDS((nn, K, bn), BF16)] if bf16_copy else SDS((nn, K, bn), F32),
        compiler_params=_arb(3), name=name)(a, b)


def _adam_math(w, g, m, v):
    m = B1 * m + (1.0 - B1) * g
    v = B2 * v + (1.0 - B2) * (g * g)
    m_hat = m / (1.0 - B1 ** STEP)
    v_hat = v / (1.0 - B2 ** STEP)
    return -LR * (m_hat / (jnp.sqrt(v_hat) + AEPS) + WD * w), m, v


def _adamw_halves(w, mine, sibling, m, v, c_idx, rb, name):
    R, C = w.shape
    nb = (R // 2) // rb

    def body(c_ref, w_ref, a_ref, b_ref, m_ref, v_ref, g_out, d_out, m_out, v_out):
        g = jnp.where(pl.program_id(0) == c_ref[0], a_ref[...], b_ref[...])
        g_out[...] = g
        d_out[...], m_out[...], v_out[...] = _adam_math(w_ref[...], g, m_ref[...], v_ref[...])

    whole = pl.BlockSpec((rb, C), lambda hh, i, cr: (hh * nb + i, 0))
    half = pl.BlockSpec((rb, C), lambda hh, i, cr: (i, 0))
    return pl.pallas_call(
        body,
        grid_spec=pltpu.PrefetchScalarGridSpec(
            num_scalar_prefetch=1, grid=(2, nb), in_specs=[whole, half, half, whole, whole], out_specs=[whole] * 4),
        out_shape=[SDS((R, C), F32)] * 4, compiler_params=_arb(2), name=name)(c_idx, w, mine, sibling, m, v)


def _ada_forward(c_all, w_ada):
    n = w_ada.shape[1]

    def body(c_ref, w_ref, ca_ref, p_ref):
        cv = c_ref[...]
        ca = cv * _sigmoid(cv)
        ca_ref[...] = ca
        p_ref[...] = _dot(ca, w_ref[...], precision=HIGHEST)

    return pl.pallas_call(
        body, grid=(n // 512,),
        in_specs=[_full((N_DEV, D)), pl.BlockSpec((D, 512), lambda i: (0, i))],
        out_specs=[_full((N_DEV, D)), pl.BlockSpec((N_DEV, 512), lambda i: (0, i))],
        out_shape=[SDS((N_DEV, D), F32), SDS((N_DEV, n), F32)],
        compiler_params=_arb(), name="ada_forward")(c_all, w_ada)


def _ada_wgrad_adam(cact_t, dada_all, w, m, v, chip_idx):
    R, C = w.shape
    rb = 256

    def body(j_ref, c_ref, d_ref, w_ref, m_ref, v_ref, g_out, d_out, m_out, v_out):
        g = _dot(c_ref[...], d_ref[...], precision=HIGHEST)
        g_out[...] = g
        d_out[...], m_out[...], v_out[...] = _adam_math(w_ref[...], g, m_ref[...], v_ref[...])

    spec = pl.BlockSpec((rb, C), lambda i, j: (i, 0))
    return pl.pallas_call(
        body,
        grid_spec=pltpu.PrefetchScalarGridSpec(
            num_scalar_prefetch=1, grid=(R // rb,),
            in_specs=[pl.BlockSpec((rb, N_DEV), lambda i, j: (i, 0)), pl.BlockSpec((N_DEV, C), lambda i, j: (0, j[0])),
                      spec, spec, spec],
            out_specs=[spec] * 4),
        out_shape=[SDS((R, C), F32)] * 4,
        compiler_params=_arb(), name="ada_wgrad_adam")(chip_idx, cact_t, dada_all, w, m, v)


SMALL_NAMES = ('b_ada', 'norm1_w', 'norm2_w', 'final_norm_w', 'v_ln_w', 'v_ln_b', 'lower_bounds', 'gn_w', 'b_s', 'w_s')


def _small_finalize(gathered, params, moms, vels):
    n_in = len(gathered)

    def body(*refs):
        acc1, acc2, dln, dlb, dgn, dbs, dws = refs[:n_in]
        prm = [dict(zip(SMALL_NAMES, refs[n_in + k * 10:n_in + (k + 1) * 10])) for k in range(3)]
        outs = [dict(zip(SMALL_NAMES, refs[n_in + 30 + k * 10:n_in + 30 + (k + 1) * 10])) for k in range(4)]
        loss_ref, dada_ref = refs[n_in + 70:n_in + 72]

        def dev_sum(ref, first, n):
            per = ref.shape[0] // N_DEV
            g = ref[first:first + n, :]
            for dev in range(1, N_DEV):
                g = g + ref[dev * per + first:dev * per + first + n, :]
            return g

        def update(n, g, cols=slice(None)):
            outs[0][n][:, cols] = g
            outs[1][n][:, cols], outs[2][n][:, cols], outs[3][n][:, cols] = _adam_math(
                prm[0][n][:, cols], g, prm[1][n][:, cols], prm[2][n][:, cols])

        ada_rows = ((acc1, 0), (acc1, 1), (acc2, 5), (acc2, 2), (acc2, 1), (acc2, 0))
        for k, (ref, r) in enumerate(ada_rows):
            update('b_ada', dev_sum(ref, r, 1), slice(k * D, (k + 1) * D))
            for dev in range(N_DEV):
                dada_ref[dev:dev + 1, k * D:(k + 1) * D] = ref[8 * dev + r:8 * dev + r + 1, :]
        update('norm1_w', dev_sum(acc1, 2, 1))
        update('norm2_w', dev_sum(acc2, 3, 1))
        update('final_norm_w', dev_sum(acc2, 4, 1))
        update('v_ln_w', dev_sum(dln, 0, 1))
        update('v_ln_b', dev_sum(dln, 1, 1))
        update('lower_bounds', dev_sum(dlb, 0, 2))
        update('gn_w', dev_sum(dgn, 0, 1))
        update('b_s', dev_sum(dbs, 0, NH))
        update('w_s', dev_sum(dws, 0, NH * BLK))
        loss_ref[...] = jnp.sum(dev_sum(acc2, 6, 1), axis=-1, keepdims=True)

    shapes = [SDS(params[n].shape, F32) for n in SMALL_NAMES]
    res = pl.pallas_call(
        body, out_shape=shapes * 4 + [SDS((1, 1), F32), SDS((N_DEV, 6 * D), F32)], name="small_finalize")(
            *gathered, *[d[n] for d in (params, moms, vels) for n in SMALL_NAMES])
    return [dict(zip(SMALL_NAMES, res[k * 10:(k + 1) * 10])) for k in range(4)], res[40], res[41]


def _position():
    x, y, c = lax.axis_index("x"), lax.axis_index("y"), lax.axis_index("c")
    return x, y, c


def _chip_at(x, y, r):
    return (x ^ (r >> 1), y ^ (r & 1))


def _gather_rows(ins, outs, send_sems, recv_sems, local_sems, after_issue=None):
    nb = len(ins)
    x, y, c = _position()
    me, sibling = (x, y, c), (x, y, 1 - c)
    chips = [_chip_at(x, y, r) for r in (1, 2, 3)]

    def rows(b, px, py, pc):
        m_per = ins[b].shape[0]
        return outs[b].at[pl.ds((4 * px + 2 * py + pc) * m_per, m_per), :]

    def copy(b, k, blk, to, src=None):
        return pltpu.make_async_remote_copy(
            src_ref=rows(b, *blk) if src is None else src, dst_ref=rows(b, *blk),
            send_sem=send_sems.at[7 * b + k], recv_sem=recv_sems.at[7 * b + k], device_id=to, device_id_type=MESH)

    local, sent = [], []
    for b in range(nb):
        mine = pltpu.make_async_copy(ins[b], rows(b, *me), local_sems.at[b])
        mine.start()
        local.append(mine)
        first = [copy(b, 0, me, sibling, src=ins[b])]
        first += [copy(b, 1 + j, me, (*chip, c), src=ins[b]) for j, chip in enumerate(chips)]
        for cp in first:
            cp.start()
        sent += first
    if after_issue is not None:
        after_issue()
    for b in range(nb):
        for j, chip in enumerate(chips):
            copy(b, 1 + j, (*chip, c), me).wait_recv()
            passed = copy(b, 4 + j, (*chip, c), sibling)
            passed.start()
            sent.append(passed)
    for b in range(nb):
        copy(b, 0, sibling, me).wait_recv()
        for j, chip in enumerate(chips):
            copy(b, 4 + j, (*chip, 1 - c), me).wait_recv()
    for cp in sent:
        cp.wait_send()
    for cp in local:
        cp.wait()


def _gather_rows_shapes(blocks):
    return [SDS((N_DEV * b.shape[0], b.shape[1]), b.dtype) for b in blocks]


def _gather_rows_sems(nb):
    return [pltpu.SemaphoreType.DMA((7 * nb,)), pltpu.SemaphoreType.DMA((7 * nb,)), pltpu.SemaphoreType.DMA((nb,))]


def _all_gather_rows(blocks, name):
    nb = len(blocks)

    def body(*refs):
        _gather_rows(refs[:nb], refs[nb:2 * nb], *refs[2 * nb:])

    vmem = pl.BlockSpec(memory_space=pltpu.VMEM)
    return pl.pallas_call(
        body, out_shape=_gather_rows_shapes(blocks), in_specs=[vmem] * nb, out_specs=[vmem] * nb,
        scratch_shapes=_gather_rows_sems(nb), name=name)(*blocks)


def _place_shard(w_shard, axis, chip_idx, name):
    R, C = w_shard.shape
    rb = _row_block(R)
    nb = R // rb
    full = (R * N_CHIPS, C) if axis == 0 else (R, C * N_CHIPS)
    omap = (lambda i, j: (j[0] * nb + i, 0)) if axis == 0 else (lambda i, j: (i, j[0]))

    def body(j_ref, w_ref, o_ref):
        o_ref[...] = w_ref[...].astype(BF16)

    return pl.pallas_call(
        body,
        grid_spec=pltpu.PrefetchScalarGridSpec(
            num_scalar_prefetch=1, grid=(nb,), in_specs=[pl.BlockSpec((rb, C), lambda i, j: (i, 0))],
            out_specs=pl.BlockSpec((rb, C), omap)),
        out_shape=SDS(full, BF16), compiler_params=_arb(), name=name)(chip_idx, w_shard)


class _WeightGather:
    def __init__(self, refs, axes, send_sems, recv_sems):
        self.refs, self.axes, self.send_sems, self.recv_sems = refs, axes, send_sems, recv_sems
        self.x, self.y, self.c = _position()
        self.j = 2 * self.x + self.y
        self.n = 3 * len(refs)

    def _half(self, w, chip_idx, half):
        ref, axis = self.refs[w], self.axes[w]
        if axis == 0:
            size = ref.shape[0] // N_CHIPS
            return ref.at[pl.ds(chip_idx * size + half * (size // 2), size // 2), :]
        size = ref.shape[1] // N_CHIPS
        rows = ref.shape[0] // 2
        return ref.at[pl.ds(half * rows, rows), pl.ds(chip_idx * size, size)]

    def _ici(self, w, r, chip_idx):
        k = 3 * w + r - 1
        piece = self._half(w, chip_idx, self.c)
        return pltpu.make_async_remote_copy(
            src_ref=piece, dst_ref=piece, send_sem=self.send_sems.at[k], recv_sem=self.recv_sems.at[k],
            device_id=(*_chip_at(self.x, self.y, r), self.c), device_id_type=MESH)

    def _d2d(self, w, r, half):
        k = self.n + 3 * w + r - 1
        piece = self._half(w, self.j ^ r, half)
        return pltpu.make_async_remote_copy(
            src_ref=piece, dst_ref=piece, send_sem=self.send_sems.at[k], recv_sem=self.recv_sems.at[k],
            device_id=(self.x, self.y, 1 - self.c), device_id_type=MESH)

    def _each(self):
        return [(w, r) for w in range(len(self.refs)) for r in (1, 2, 3)]

    def start(self):
        for w, r in self._each():
            self._ici(w, r, self.j).start()

    def forward(self):
        for w, r in self._each():
            self._ici(w, r, self.j ^ r).wait_recv()
            self._d2d(w, r, self.c).start()

    def finish(self):
        for w, r in self._each():
            self._ici(w, r, self.j).wait_send()
            self._d2d(w, r, self.c).wait_send()
            self._d2d(w, r, 1 - self.c).wait_recv()


def _gather_sems(n_weights):
    return [pltpu.SemaphoreType.DMA((6 * n_weights,)), pltpu.SemaphoreType.DMA((6 * n_weights,))]


def _gather_weights(placed, axes, row_blocks, name):
    nw, nb = len(placed), len(row_blocks)

    def body(*refs):
        w_outs, b_ins, b_outs = refs[nw + nb:2 * nw + nb], refs[nw:nw + nb], refs[2 * nw + nb:2 * (nw + nb)]
        sems = refs[2 * (nw + nb):]
        g = _WeightGather(w_outs, axes, *sems[:2])
        _gather_rows(b_ins, b_outs, *sems[2:], after_issue=g.start)
        g.forward()
        g.finish()

    anyspec = pl.BlockSpec(memory_space=pl.ANY)
    vmem = pl.BlockSpec(memory_space=pltpu.VMEM)
    res = pl.pallas_call(
        body, out_shape=[SDS(a.shape, a.dtype) for a in placed] + _gather_rows_shapes(row_blocks),
        in_specs=[anyspec] * nw + [vmem] * nb, out_specs=[anyspec] * nw + [vmem] * nb,
        scratch_shapes=_gather_sems(nw) + _gather_rows_sems(nb), input_output_aliases={i: i for i in range(nw)},
        name=name)(*placed, *row_blocks)
    return res[:nw], res[nw:]


class _ChipExchange:
    def __init__(self, ins, outs, send_sems, recv_sems):
        self.ins, self.outs, self.send_sems, self.recv_sems = ins, outs, send_sems, recv_sems
        self.x, self.y, self.c = _position()
        self.j = 2 * self.x + self.y

    def _copies(self):
        for w in range(len(self.ins)):
            for r in (1, 2, 3):
                k = 3 * w + r - 1
                yield pltpu.make_async_remote_copy(
                    src_ref=self.ins[w].at[self.j ^ r], dst_ref=self.outs[w].at[r - 1],
                    send_sem=self.send_sems.at[k], recv_sem=self.recv_sems.at[k],
                    device_id=(*_chip_at(self.x, self.y, r), self.c), device_id_type=MESH)

    def start(self):
        for cp in self._copies():
            cp.start()

    def finish(self):
        for cp in self._copies():
            cp.wait()


def _exchange_sems(n_weights):
    return [pltpu.SemaphoreType.DMA((3 * n_weights,)), pltpu.SemaphoreType.DMA((3 * n_weights,))]


class _CoreExchange:
    def __init__(self, ins, outs, send_sems, recv_sems):
        self.ins, self.outs, self.send_sems, self.recv_sems = ins, outs, send_sems, recv_sems
        self.x, self.y, self.c = _position()

    def _copies(self):
        for w in range(len(self.ins)):
            yield pltpu.make_async_remote_copy(
                src_ref=self.ins[w].at[:, 1 - self.c], dst_ref=self.outs[w],
                send_sem=self.send_sems.at[w], recv_sem=self.recv_sems.at[w],
                device_id=(self.x, self.y, 1 - self.c), device_id_type=MESH)

    def start(self):
        for cp in self._copies():
            cp.start()

    def finish(self):
        for cp in self._copies():
            cp.wait()


def _core_exchange_shapes(grads):
    return [SDS((g.shape[0], g.shape[2], g.shape[3]), g.dtype) for g in grads]


def _core_exchange_sems(n):
    return [pltpu.SemaphoreType.DMA((n,)), pltpu.SemaphoreType.DMA((n,))]


def _exchange_core_halves(grads, name):
    nw = len(grads)

    def body(*refs):
        ex = _CoreExchange(refs[:nw], refs[nw:2 * nw], *refs[2 * nw:])
        ex.start()
        ex.finish()

    anyspec = pl.BlockSpec(memory_space=pl.ANY)
    return pl.pallas_call(
        body, out_shape=_core_exchange_shapes(grads), in_specs=[anyspec] * nw, out_specs=[anyspec] * nw,
        scratch_shapes=_core_exchange_sems(nw), name=name)(*grads)


def _add_core_halves(g4, recv, c_idx, rb, name):
    ns, _, rh, C = g4.shape

    def body(c_ref, g_ref, r_ref, o_ref):
        o_ref[...] = (g_ref[0] + r_ref[...]).astype(BF16)

    return pl.pallas_call(
        body,
        grid_spec=pltpu.PrefetchScalarGridSpec(
            num_scalar_prefetch=1, grid=(ns, rh // rb),
            in_specs=[pl.BlockSpec((1, 1, rb, C), lambda s, i, cr: (s, cr[0], i, 0)),
                      pl.BlockSpec((1, rb, C), lambda s, i, cr: (s, i, 0))],
            out_specs=pl.BlockSpec((1, rb, C), lambda s, i, cr: (s, i, 0))),
        out_shape=SDS((ns, rh, C), BF16), compiler_params=_arb(2), name=name)(c_idx, g4, recv)


def _add_core_halves_in(g4, recv, c_idx, name):
    n_slabs, _, rh, C = g4.shape
    cb = 256
    per_slab, per_chip, n_blocks = C // cb, DIN // N_CHIPS // cb, DIN // cb

    def stored(s, k):
        sb = (per_chip * s + k + 4 * DH // cb) % n_blocks
        return sb // per_slab, sb % per_slab

    def body(c_ref, g_ref, r_ref, o_ref):
        o_ref[...] = (g_ref[0] + r_ref[...].astype(F32)).astype(BF16)

    return pl.pallas_call(
        body,
        grid_spec=pltpu.PrefetchScalarGridSpec(
            num_scalar_prefetch=1, grid=(N_CHIPS, per_chip),
            in_specs=[pl.BlockSpec((1, 1, rh, cb), lambda s, k, cr: (stored(s, k)[0], cr[0], 0, stored(s, k)[1])),
                      pl.BlockSpec((1, rh, cb), lambda s, k, cr: (stored(s, k)[0], 0, stored(s, k)[1]))],
            out_specs=pl.BlockSpec((1, rh, cb), lambda s, k, cr: (s, 0, k))),
        out_shape=SDS((N_CHIPS, rh, DIN // N_CHIPS), BF16), compiler_params=_arb(2), name=name)(c_idx, g4, recv)


def _slot_shapes(sums):
    return [SDS((3,) + s.shape[1:], s.dtype) for s in sums]


def _add_chips(own, slots, order, rb, name):
    _, rh, C = slots.shape

    def body(o_ref, own_ref, a_ref, b_ref, c_ref, d_ref, out_ref):
        mine = own_ref[0].astype(F32)
        t = [jnp.where(o_ref[i] == 0, mine, r[0].astype(F32)) for i, r in enumerate((a_ref, b_ref, c_ref, d_ref))]
        out_ref[...] = ((t[0] + t[1]) + t[2]) + t[3]

    def spec(i):
        return pl.BlockSpec((1, rb, C), lambda t, o: (jnp.maximum(o[i], 1) - 1, t, 0))

    return pl.pallas_call(
        body,
        grid_spec=pltpu.PrefetchScalarGridSpec(
            num_scalar_prefetch=1, grid=(rh // rb,),
            in_specs=[pl.BlockSpec((1, rb, C), lambda t, o: (o[4], t, 0)), spec(0), spec(1), spec(2), spec(3)],
            out_specs=pl.BlockSpec((rb, C), lambda t, o: (t, 0))),
        out_shape=SDS((rh, C), F32), compiler_params=_arb(), name=name)(order, own, slots, slots, slots, slots)


def _share_halves(halves):
    nw = len(halves)

    def body(*refs):
        ins, outs = refs[:nw], refs[nw:2 * nw]
        send_sems, recv_sems = refs[2 * nw:]
        x, y, c = _position()
        started = []
        for w in range(nw):
            cp = pltpu.make_async_remote_copy(
                src_ref=ins[w], dst_ref=outs[w], send_sem=send_sems.at[w], recv_sem=recv_sems.at[w],
                device_id=(x, y, 1 - c), device_id_type=MESH)
            cp.start()
            started.append(cp)
        for cp in started:
            cp.wait()

    anyspec = pl.BlockSpec(memory_space=pl.ANY)
    return pl.pallas_call(
        body, out_shape=[SDS(h.shape, F32) for h in halves], in_specs=[anyspec] * nw, out_specs=[anyspec] * nw,
        scratch_shapes=[pltpu.SemaphoreType.DMA((nw,)), pltpu.SemaphoreType.DMA((nw,))],
        name="share_halves")(*halves)


def _small_2d(b_ada, norm1_w, norm2_w, final_norm_w, v_ln_w, v_ln_b, lower_bounds, gn_w, b_s, w_s):
    return dict(zip(SMALL_NAMES, (b_ada, norm1_w, norm2_w, final_norm_w.reshape(1, D), v_ln_w, v_ln_b, lower_bounds, gn_w,
                                  b_s.reshape(NH, BLK), w_s.reshape(NH * BLK, BLK))))


def _small_original_shapes(d):
    out = dict(d)
    out['final_norm_w'] = d['final_norm_w'].reshape(D)
    out['b_s'] = d['b_s'].reshape(1, NH, BLK)
    out['w_s'] = d['w_s'].reshape(1, NH, BLK, BLK)
    return out


def _row_block(r):
    for cand in (256, 176, 128, 64, 32, 16, 8):
        if r % cand == 0:
            return cand
    return r


def kernel(x, c, w_ada, b_ada, norm1_w, w_in, w_s, b_s, v_ln_w, v_ln_b, lower_bounds, gn_w, w_out, norm2_w, w_ffn_in, w_ffn_out, final_norm_w, loss_target, m_w_ada, m_b_ada, m_norm1_w, m_w_in, m_w_s, m_b_s, m_v_ln_w, m_v_ln_b, m_lower_bounds, m_gn_w, m_w_out, m_norm2_w, m_w_ffn_in, m_w_ffn_out, m_final_norm_w, v_w_ada, v_b_ada, v_norm1_w, v_w_in, v_w_s, v_b_s, v_v_ln_w, v_v_ln_b, v_lower_bounds, v_gn_w, v_w_out, v_norm2_w, v_w_ffn_in, v_w_ffn_out, v_final_norm_w):
    T = x.shape[1]
    tm, tp = min(TOKEN_TILE, T), min(PROJ_TILE, T)
    px, py, pc = _position()
    chip = 2 * px + py
    me = 4 * px + 2 * py + pc
    x2d = x.reshape(T, D)
    tgt = loss_target.reshape(T, D)

    chip_idx = jnp.reshape(chip, (1,)).astype(jnp.int32)
    c_idx = jnp.reshape(pc, (1,)).astype(jnp.int32)
    (w_in_b,), (c_all,) = _gather_weights(
        [_place_shard(w_in[0], 1, chip_idx, "place_in")], [1], [jnp.broadcast_to(c, (8, D))], "gather_w_in_and_c")
    placed = [_place_shard(w_out[0], 0, chip_idx, "place_out"), _place_shard(w_ffn_in[0], 1, chip_idx, "place_ffn_in"),
              _place_shard(w_ffn_out[0], 0, chip_idx, "place_ffn_out")]

    cact, ada_part = _ada_forward(c_all.reshape(N_DEV, 8, D)[:, 0, :], w_ada[0])
    n_ada = ada_part.shape[1]
    (ada_all,) = _all_gather_rows([ada_part], "gather_ada")
    ada_all = ada_all.reshape(N_CHIPS, 2, N_DEV, n_ada)[:, 0]
    ada = lax.dynamic_index_in_dim(ada_all, me, axis=1, keepdims=False).reshape(1, 6 * D) + b_ada

    rr = lax.broadcasted_iota(jnp.int32, (BLK, BLK), 0) // CH
    cc = lax.broadcasted_iota(jnp.int32, (BLK, BLK), 1) // CH
    ws_b = jnp.where((rr >= cc)[None], w_s[0], 0.0).astype(BF16)
    bst = b_s[0].T
    lnw, lnb = v_ln_w, v_ln_b
    nw1, nw2, fw = norm1_w, norm2_w, final_norm_w.reshape(1, D)

    h1, proj = _proj_in(x2d, nw1, ada, ada, w_in_b, tp)
    ycat = _gmlp_fwd(proj, ws_b, bst, lnw, lnb)
    tables = _hgrn_tables()
    (ycat, o_pre, a_all, st_all), (w_out_b, w_fi_b, w_fo_b) = _hgrn_fwd(
        proj, lower_bounds, gn_w, ycat, tables, placed, [0, 1, 0])

    dycat, dx1, h2, act, dff, dgu, dmix, acc2 = _token_local(
        x2d, ycat, tgt, ada, nw2, ada, ada, ada, fw, w_out_b, w_fi_b, w_fo_b, tm)

    tt = min(WGRAD_TOKENS, T)
    order = jnp.concatenate([chip ^ jnp.arange(N_CHIPS, dtype=jnp.int32), chip_idx]).astype(jnp.int32)

    def by_core_half(g):
        return g.reshape(g.shape[0], 2, g.shape[1] // 2, g.shape[2])

    def core_sums(g4, recv, names):
        return [_add_core_halves(a, b, c_idx, _row_block(a.shape[2]), "add_core_" + n) for a, b, n in zip(g4, recv, names)]

    def chip_sums(sums, slots, names):
        return [_add_chips(o, s, order, _row_block(s.shape[1]), "add_chips_" + n) for o, s, n in zip(sums, slots, names)]

    g_out = _wgrad(ycat, dmix, D, D, tt, "wgrad_out").reshape(N_CHIPS, D // N_CHIPS, D)
    g_fi = _wgrad(h2, dgu, D, FFB, tt, "wgrad_ffn_in")
    g_fo = _wgrad(act, dff, FFB, D, tt, "wgrad_ffn_out").reshape(N_CHIPS, DFF // N_CHIPS, D)
    late_names = ["out", "ffn_in", "ffn_out"]
    late_g4 = [by_core_half(g) for g in (g_out, g_fi, g_fo)]

    (dproj, dws, dbs, dln), late_recv = _gmlp_bwd(proj, dycat, ws_b, bst, lnw, lnb, late_g4)
    late_sums = core_sums(late_g4, late_recv, late_names)
    (dproj, dlb, dgn), late_slots = _hgrn_bwd(
        proj, o_pre, a_all, st_all, dycat, lower_bounds, gn_w, dproj, tables, late_sums)

    g_in, g_in_wire = _wgrad(h1, dproj, D, D, tt, "wgrad_in", bf16_copy=True)
    (in_recv,) = _exchange_core_halves([by_core_half(g_in_wire)], "exchange_core_halves_in")
    in_sums = [_add_core_halves_in(by_core_half(g_in), in_recv, c_idx, "add_core_in")]
    (grad_x, acc1), in_slots = _proj_in_bwd(dproj, x2d, dx1, nw1, ada, w_in_b, tp, in_sums)
    names = ["in"] + late_names
    halves = chip_sums(in_sums, in_slots, ["in"]) + chip_sums(late_sums, late_slots, late_names)
    sibling_halves = _share_halves(halves)

    big_w = [(w_in, m_w_in, v_w_in), (w_out, m_w_out, v_w_out), (w_ffn_in, m_w_ffn_in, v_w_ffn_in),
             (w_ffn_out, m_w_ffn_out, v_w_ffn_out)]
    big_out = []
    for mine, sib, (w, m, v), n in zip(halves, sibling_halves, big_w, names):
        res = _adamw_halves(w[0], mine, sib, m[0], v[0], c_idx, _row_block(mine.shape[0]), "adamw_" + n)
        big_out.append([r[None] for r in res])

    gathered = _all_gather_rows([acc1, acc2, dln, dlb, dgn, dbs, dws], "gather_small")
    small, loss, dada_all = _small_finalize(
        gathered,
        _small_2d(b_ada, norm1_w, norm2_w, final_norm_w, v_ln_w, v_ln_b, lower_bounds, gn_w, b_s, w_s),
        _small_2d(m_b_ada, m_norm1_w, m_norm2_w, m_final_norm_w, m_v_ln_w, m_v_ln_b, m_lower_bounds, m_gn_w, m_b_s, m_w_s),
        _small_2d(v_b_ada, v_norm1_w, v_norm2_w, v_final_norm_w, v_v_ln_w, v_v_ln_b, v_lower_bounds, v_gn_w, v_b_s, v_w_s))
    small = [_small_original_shapes(d) for d in small]
    loss = loss.reshape(())

    ada_out = [o[None] for o in _ada_wgrad_adam(cact.T, dada_all, w_ada[0], m_w_ada[0], v_w_ada[0], chip_idx)]

    order_names = ['w_ada', 'b_ada', 'norm1_w', 'w_in', 'w_s', 'b_s', 'v_ln_w', 'v_ln_b', 'lower_bounds', 'gn_w',
                   'w_out', 'norm2_w', 'w_ffn_in', 'w_ffn_out', 'final_norm_w']
    big_idx = {'w_in': 0, 'w_out': 1, 'w_ffn_in': 2, 'w_ffn_out': 3}
    outs = [loss, grad_x.reshape(1, T, D)]
    for kind in range(4):
        for n in order_names:
            if n == 'w_ada':
                outs.append(ada_out[kind])
            elif n in big_idx:
                outs.append(big_out[big_idx[n]][kind])
            else:
                outs.append(small[kind][n])
    return tuple(outs)
```

```python
import functools

import jax
import jax.numpy as jnp
import numpy as np
from jax import lax
from jax.experimental import pallas as pl
from jax.experimental.pallas import tpu as pltpu

F32 = jnp.float32
BF16 = jnp.bfloat16
SDS = jax.ShapeDtypeStruct
MESH = pl.DeviceIdType.MESH
HIGHEST = lax.Precision.HIGHEST

D = 1024
DG = 512
DH = 512
NH = 4
HD = 128
BLK = 128
CH = 64
DFF = 2816
DIN = 3072
FFB = 1408
LEVELS = (64, 32, 16, 8, 4, 2)
HGRN_CHUNKS_PER_STEP = 8
GMLP_ROWS_PER_STEP = 1024
TOKEN_TILE = 256
PROJ_TILE = 1024
WGRAD_TOKENS = 2048
N_CHIPS = 4
N_DEV = 8
EPS = 1e-6
LR, B1, B2, AEPS, WD, STEP = 0.001, 0.9, 0.999, 1e-08, 0.01, 10

NT = (((1,), (1,)), ((), ()))
TN = (((0,), (0,)), ((), ()))


def _full(shape):
    nd = len(shape)
    return pl.BlockSpec(shape, lambda *_: (0,) * nd)


ADA_SH1, ADA_SC1, ADA_G1, ADA_SH2, ADA_SC2, ADA_G2 = range(6)


def _ada_part(k):
    return pl.BlockSpec((1, D), lambda *_: (0, k))


def _resident(shape):
    nd = len(shape)
    return pl.BlockSpec(shape, lambda *_: (0,) * nd, pipeline_mode=pl.Buffered(1))


def _arb(n=1):
    return pltpu.CompilerParams(dimension_semantics=("arbitrary",) * n)


def _dot(a, b, dims=None, precision=None):
    if dims is None:
        return jnp.dot(a, b, preferred_element_type=F32, precision=precision)
    return lax.dot_general(a, b, dims, preferred_element_type=F32, precision=precision)


def _sigmoid(x):
    return jax.nn.sigmoid(x)


def _gelu_parts(x):
    cdf = 0.5 * (1.0 + lax.erf(x * 0.7071067811865476))
    pdf = jnp.exp(-0.5 * x * x) * 0.3989422804014327
    return x * cdf, cdf + x * pdf


def _rms(x):
    return lax.rsqrt(jnp.mean(x * x, axis=-1, keepdims=True) + EPS)


def _rms_bwd(xhat, r, gw):
    return r * (gw - xhat * jnp.mean(xhat * gw, axis=-1, keepdims=True))


def _lower_bound(lbp_ref):
    l0, l1 = lbp_ref[0:1, :], lbp_ref[1:2, :]
    m = jnp.maximum(l0, l1)
    e0, e1 = jnp.exp(l0 - m), jnp.exp(l1 - m)
    return e0 / (e0 + e1), e1 / (e0 + e1)


def _gmlp_common(u, v, lnw, lnb, ws_ref, bst_ref):
    ug, dug = _gelu_parts(u)
    vg, dvg = _gelu_parts(v)
    mu = jnp.mean(vg, axis=-1, keepdims=True)
    vc = vg - mu
    rstd = lax.rsqrt(jnp.mean(vc * vc, axis=-1, keepdims=True) + EPS)
    vhat = vc * rstd
    vn = vhat * lnw + lnb
    vnb = vn.astype(BF16)
    mixed = []
    for h in range(NH):
        sl = slice(h * HD, (h + 1) * HD)
        mixed.append(_dot(ws_ref[h], vnb[:, sl]) + bst_ref[:, h:h + 1])
    return ug, dug, dvg, rstd, vhat, vnb, jnp.concatenate(mixed, axis=1)


def _gmlp_fwd(proj, ws_b, bst, lnw, lnb, ycat):
    T = proj.shape[0]
    rows = min(GMLP_ROWS_PER_STEP, T)

    def body(u_ref, v_ref, ws_ref, bst_ref, lnw_ref, lnb_ref, ycat_any, y_ref):
        for bi in range(rows // BLK):
            rs = slice(bi * BLK, (bi + 1) * BLK)
            ug, _, _, _, _, _, mixed = _gmlp_common(u_ref[rs, :], v_ref[rs, :], lnw_ref[...], lnb_ref[...], ws_ref, bst_ref)
            y_ref[rs, :] = (ug * mixed).astype(BF16)

    return pl.pallas_call(
        body, grid=(T // rows,),
        in_specs=[pl.BlockSpec((rows, DG), lambda i: (i, 0)), pl.BlockSpec((rows, DG), lambda i: (i, 1)),
                  _full((NH, BLK, BLK)), _full((BLK, NH)), _full((1, DG)), _full((1, DG)),
                  pl.BlockSpec(memory_space=pl.ANY)],
        out_specs=pl.BlockSpec((rows, DG), lambda i: (i, 0)),
        out_shape=SDS((T, D), BF16), input_output_aliases={6: 0},
        compiler_params=_arb(), name="gmlp_fwd")(proj, proj, ws_b, bst, lnw, lnb, ycat)


def _hgrn_tables():
    t = np.arange(CH)[:, None]
    j = np.arange(CH)[None, :]
    blocks = [j <= t, j > t]
    masks = []
    for n in LEVELS:
        mid = t - t % n + n // 2
        blocks.append(np.where(t >= mid, (j >= mid) & (j <= t), (j > t) & (j < mid)))
        masks.append((t // n == j // n) & (t % n >= n // 2) & (j % n < n // 2))
    w = np.concatenate(blocks, axis=0).astype(np.float32)
    m = np.stack(masks).astype(np.float32)
    return (jnp.asarray(w, BF16), jnp.asarray(w.T, BF16), jnp.asarray(m), jnp.asarray(m + m.transpose(0, 2, 1)))


def _split_dot(w, x, parts):
    acc = None
    for _ in range(parts):
        piece = x.astype(BF16)
        term = _dot(w, piece)
        acc = term if acc is None else acc + term
        x = x - piece.astype(F32)
    return acc


def _hgrn_decays(f, w_ref):
    b = _split_dot(w_ref[0:CH, :], jnp.log(f), 3)
    row = lax.broadcasted_iota(jnp.int32, (CH, 1), 0)
    blocks = [jnp.exp(b), jnp.exp(b[CH - 1:CH, :] - b)]
    for n in LEVELS:
        up = (row & (n // 2)) != 0
        if n >= 8:
            ref = b.reshape(CH // n, n, DH)[:, n // 2 - 1:n // 2, :]
            ref = jnp.broadcast_to(ref, (CH // n, n, DH)).reshape(CH, DH)
            blocks.append(jnp.exp(jnp.where(up, b - ref, ref - b)))
        elif n == 4:
            r4 = row & 3
            two = jnp.where(r4 == 3, pltpu.roll(f, 1, 0) * f, 1.0)
            blocks.append(jnp.where(r4 == 0, pltpu.roll(f, CH - 1, 0), jnp.where(r4 == 2, f, two)))
        else:
            blocks.append(jnp.where(up, f, 1.0))
    return blocks


def _hgrn_gates(q, fl, lb, omlb, w_ref):
    sq = _sigmoid(q)
    qf = q * sq
    sig = _sigmoid(fl)
    f = lb + omlb * sig
    k = 1.0 - f
    return sq, qf, sig, f, k, _hgrn_decays(f, w_ref)


def _level_factor(e, li, sl, row, qh, kh):
    el = e[2 + li][:, sl]
    up = (row & (LEVELS[li] // 2)) != 0
    return el, up, el * jnp.where(up, qh, kh)


def _proj_hgrn_fwd(x, nw1, ada, w_in_b, lower_bounds, gn_w, tables, placed, axes):
    T = x.shape[0]
    nc = T // CH
    nch = min(HGRN_CHUNKS_PER_STEP, nc)
    steps = nc // nch
    w_st, _, masks, _ = tables
    nw = len(placed)
    pass_step = (13 * steps) // 16
    q0 = 2 * DG

    def body(*refs):
        x_ref, nw_ref, sc_ref, sh_ref, win_ref, lbp_ref, gn_ref, w_ref, m_ref = refs[:9]
        h_ref, p_ref, y_ref, o_ref, a_ref, st_ref = refs[9 + nw:15 + nw]
        s_scr, send_sems, recv_sems = refs[15 + 2 * nw:]
        gather = _WeightGather(refs[15 + nw:15 + 2 * nw], axes, send_sems, recv_sems)
        step = pl.program_id(0)
        xv = x_ref[...]
        hb = (((xv * _rms(xv)) * nw_ref[...]) * (1.0 + sc_ref[...]) + sh_ref[...]).astype(BF16)
        h_ref[...] = hb
        p_ref[...] = _dot(hb, win_ref[...])

        @pl.when(step == 0)
        def _():
            gather.start()
            s_scr[...] = jnp.zeros_like(s_scr)

        @pl.when(step == pass_step)
        def _():
            gather.forward()

        lb, omlb = _lower_bound(lbp_ref)
        row = lax.broadcasted_iota(jnp.int32, (CH, 1), 0)
        eye = lax.broadcasted_iota(jnp.int32, (CH, CH), 0) == lax.broadcasted_iota(jnp.int32, (CH, CH), 1)
        in_level = [m_ref[li] > 0.0 for li in range(len(LEVELS))]
        pre = []
        for ci in range(nch):
            rs = slice(ci * CH, (ci + 1) * CH)
            _, qf, _, _, k, e = _hgrn_gates(p_ref[rs, q0:q0 + DH], p_ref[rs, q0 + DH:q0 + 2 * DH], lb, omlb, w_ref)
            mats = []
            for h in range(NH):
                sl = slice(h * HD, (h + 1) * HD)
                qh, kh = qf[:, sl], k[:, sl]
                a = jnp.where(eye, jnp.sum(qh * kh, axis=-1, keepdims=True), 0.0)
                for li in range(len(LEVELS)):
                    _, _, y = _level_factor(e, li, sl, row, qh, kh)
                    yb = y.astype(BF16)
                    a = jnp.where(in_level[li], _dot(yb, yb, NT), a)
                a_ref[ci, h] = a
                mats.append(a.astype(BF16))
            eb = e[0]
            pre.append(((qf * eb).astype(BF16), eb[CH - 1:CH, :], (k * e[1]).astype(BF16), mats))
        for ci in range(nch):
            rs = slice(ci * CH, (ci + 1) * CH)
            qe, ebl, kd, mats = pre[ci]
            v = p_ref[rs, q0 + 2 * DH:q0 + 3 * DH]
            g = p_ref[rs, q0 + 3 * DH:q0 + 4 * DH]
            for h in range(NH):
                sl = slice(h * HD, (h + 1) * HD)
                st0 = s_scr[h]
                st_ref[ci, h] = st0
                vb = v[:, sl].astype(BF16)
                o = _dot(qe[:, sl], st0.astype(BF16), NT) + _dot(mats[h], vb)
                s_scr[h] = st0 * ebl[:, sl] + _dot(vb, kd[:, sl], TN)
                o_ref[rs, sl] = o
                gh = g[:, sl]
                y_ref[rs, sl] = (((o * _rms(o)) * gn_ref[...]) * (gh * _sigmoid(gh))).astype(BF16)

        @pl.when(step == steps - 1)
        def _():
            gather.finish()

    rows = nch * CH
    row = lambda c: (c, 0)
    anyspec = pl.BlockSpec(memory_space=pl.ANY)
    res = pl.pallas_call(
        body, grid=(steps,),
        in_specs=[pl.BlockSpec((rows, D), row), _full((1, D)), _ada_part(ADA_SC1), _ada_part(ADA_SH1), _resident((D, DIN)),
                  _full((2, DH)), _full((1, HD)), _full(w_st.shape), _full(masks.shape)] + [anyspec] * nw,
        out_specs=[pl.BlockSpec((rows, D), row), pl.BlockSpec((rows, DIN), row),
                   pl.BlockSpec((rows, DH), lambda c: (c, 1)),
                   pl.BlockSpec((rows, DH), row),
                   pl.BlockSpec((nch, NH, CH, CH), lambda c: (c, 0, 0, 0)),
                   pl.BlockSpec((nch, NH, HD, HD), lambda c: (c, 0, 0, 0))] + [anyspec] * nw,
        out_shape=[SDS((T, D), BF16), SDS((T, DIN), F32), SDS((T, D), BF16), SDS((T, DH), F32),
                   SDS((nc, NH, CH, CH), F32), SDS((nc, NH, HD, HD), F32)] + [SDS(a.shape, a.dtype) for a in placed],
        scratch_shapes=[pltpu.VMEM((NH, HD, HD), F32)] + _gather_sems(nw),
        input_output_aliases={9 + i: 6 + i for i in range(nw)},
        compiler_params=_arb(), name="proj_hgrn_fwd")(x, nw1, ada, ada, w_in_b, lower_bounds, gn_w, w_st, masks, *placed)
    return res[:6], res[6:]


def _token_local(x, ycat, tgt, g1, nw2, sc2, sh2, g2, fw, w_out_b, w_fi_b, w_fo_b, tm):
    T = x.shape[0]
    inv_d = 1.0 / D

    def body(x_ref, y_ref, t_ref, g1_ref, nw2_ref, sc2_ref, sh2_ref, g2_ref, fw_ref, wo_ref, wfi_ref, wfo_ref,
             dy_ref, dx1_ref, h2_ref, act_ref, dff_ref, dgu_ref, dmix_ref, acc_ref):
        @pl.when(pl.program_id(0) == 0)
        def _():
            acc_ref[...] = jnp.zeros_like(acc_ref)

        def acc(row, val):
            acc_ref[row:row + 1, :] += jnp.sum(val, axis=0, keepdims=True)

        g1v, g2v = g1_ref[...], g2_ref[...]
        mix = _dot(y_ref[...], wo_ref[...])
        x1 = x_ref[...] + g1v * mix
        r2 = _rms(x1)
        xh2 = x1 * r2
        n2 = xh2 * nw2_ref[...]
        osc2 = 1.0 + sc2_ref[...]
        h2b = (n2 * osc2 + sh2_ref[...]).astype(BF16)
        h2_ref[...] = h2b
        ff = jnp.zeros((tm, D), F32)
        saved = []
        for kb in range(DFF // FFB):
            gate = _dot(h2b, wfi_ref[:, kb * FFB:(kb + 1) * FFB])
            up = _dot(h2b, wfi_ref[:, DFF + kb * FFB:DFF + (kb + 1) * FFB])
            sg = _sigmoid(gate)
            actb = (gate * sg * up).astype(BF16)
            act_ref[:, kb * FFB:(kb + 1) * FFB] = actb
            ff = ff + _dot(actb, wfo_ref[kb * FFB:(kb + 1) * FFB, :])
            saved.append((gate, up, sg))
        x2 = x1 + g2v * ff
        r3 = _rms(x2)
        xh3 = x2 * r3
        err = xh3 * fw_ref[...] - t_ref[...]
        acc(6, (0.5 * inv_d) * err * err)
        dy = err * inv_d
        acc(4, dy * xh3)
        dx2 = _rms_bwd(xh3, r3, dy * fw_ref[...])
        acc(0, dx2 * ff)
        dffb = (dx2 * g2v).astype(BF16)
        dff_ref[...] = dffb
        dh2 = jnp.zeros((tm, D), F32)
        for kb in range(DFF // FFB):
            gate, up, sg = saved[kb]
            da = _dot(dffb, wfo_ref[kb * FFB:(kb + 1) * FFB, :], NT)
            dgate = (da * up * (sg * (1.0 + gate * (1.0 - sg)))).astype(BF16)
            dup = (da * gate * sg).astype(BF16)
            dgu_ref[:, kb * FFB:(kb + 1) * FFB] = dgate
            dgu_ref[:, DFF + kb * FFB:DFF + (kb + 1) * FFB] = dup
            dh2 = dh2 + _dot(dgate, wfi_ref[:, kb * FFB:(kb + 1) * FFB], NT)
            dh2 = dh2 + _dot(dup, wfi_ref[:, DFF + kb * FFB:DFF + (kb + 1) * FFB], NT)
        acc(2, dh2)
        acc(1, dh2 * n2)
        dn2 = dh2 * osc2
        acc(3, dn2 * xh2)
        dx1 = dx2 + _rms_bwd(xh2, r2, dn2 * nw2_ref[...])
        acc(5, dx1 * mix)
        dmixb = (dx1 * g1v).astype(BF16)
        dmix_ref[...] = dmixb
        dy_ref[...] = _dot(dmixb, wo_ref[...], NT)
        dx1_ref[...] = dx1

    row = lambda i: (i, 0)
    vec = _full((1, D))
    return pl.pallas_call(
        body, grid=(T // tm,),
        in_specs=[pl.BlockSpec((tm, D), row), pl.BlockSpec((tm, D), row), pl.BlockSpec((tm, D), row),
                  _ada_part(ADA_G1), vec, _ada_part(ADA_SC2), _ada_part(ADA_SH2), _ada_part(ADA_G2), vec,
                  _resident((D, D)), _resident((D, 2 * DFF)), _resident((DFF, D))],
        out_specs=[pl.BlockSpec((tm, D), row), pl.BlockSpec((tm, D), row), pl.BlockSpec((tm, D), row),
                   pl.BlockSpec((tm, DFF), row), pl.BlockSpec((tm, D), row), pl.BlockSpec((tm, 2 * DFF), row),
                   pl.BlockSpec((tm, D), row), _full((8, D))],
        out_shape=[SDS((T, D), F32), SDS((T, D), F32), SDS((T, D), BF16), SDS((T, DFF), BF16), SDS((T, D), BF16),
                   SDS((T, 2 * DFF), BF16), SDS((T, D), BF16), SDS((8, D), F32)],
        compiler_params=_arb(), name="token_local")(x, ycat, tgt, g1, nw2, sc2, sh2, g2, fw, w_out_b, w_fi_b, w_fo_b)


def _gmlp_bwd(proj, dycat, ws_b, bst, lnw, lnb, grads):
    T = proj.shape[0]
    rows = min(GMLP_ROWS_PER_STEP, T)
    nb = T // rows
    nw = len(grads)

    def body(*refs):
        u_ref, v_ref, dy_ref, ws_ref, bst_ref, lnw_ref, lnb_ref = refs[:7]
        dp_ref, dws_ref, dbs_ref, dln_ref = refs[7 + nw:11 + nw]
        dbs_acc, send_sems, recv_sems = refs[11 + 2 * nw:]
        exchange = _CoreExchange(refs[7:7 + nw], refs[11 + nw:11 + 2 * nw], send_sems, recv_sems)
        i = pl.program_id(0)

        @pl.when(i == 0)
        def _():
            exchange.start()
            dws_ref[...] = jnp.zeros_like(dws_ref)
            dln_ref[...] = jnp.zeros_like(dln_ref)
            dbs_acc[...] = jnp.zeros_like(dbs_acc)

        r = lax.broadcasted_iota(jnp.int32, (BLK, BLK), 0) // CH
        c = lax.broadcasted_iota(jnp.int32, (BLK, BLK), 1) // CH
        for bi in range(rows // BLK):
            rs = slice(bi * BLK, (bi + 1) * BLK)
            ug, dug, dvg, rstd, vhat, vnb, mixed = _gmlp_common(
                u_ref[rs, :], v_ref[rs, :], lnw_ref[...], lnb_ref[...], ws_ref, bst_ref)
            dya = dy_ref[rs, :]
            dp_ref[rs, 0:DG] = (dya * mixed * dug).astype(BF16)
            dmixed = dya * ug
            dbs_acc[...] += dmixed
            dmb = dmixed.astype(BF16)
            dvn = []
            for h in range(NH):
                sl = slice(h * HD, (h + 1) * HD)
                dws_ref[h * BLK:(h + 1) * BLK, :] += jnp.where(r >= c, _dot(dmb[:, sl], vnb[:, sl], NT), 0.0)
                dvn.append(_dot(ws_ref[h], dmb[:, sl], TN))
            dvn = jnp.concatenate(dvn, axis=1)
            dln_ref[0:1, :] += jnp.sum(dvn * vhat, axis=0, keepdims=True)
            dln_ref[1:2, :] += jnp.sum(dvn, axis=0, keepdims=True)
            dvh = dvn * lnw_ref[...]
            dvgel = rstd * (dvh - jnp.mean(dvh, axis=-1, keepdims=True) - vhat * jnp.mean(dvh * vhat, axis=-1, keepdims=True))
            dp_ref[rs, DG:2 * DG] = (dvgel * dvg).astype(BF16)

        @pl.when(i == nb - 1)
        def _():
            head = lax.broadcasted_iota(jnp.int32, (8, BLK), 0)
            ones = jnp.ones((8, HD), F32)
            out = jnp.zeros((8, BLK), F32)
            for h in range(NH):
                sums = _dot(ones, dbs_acc[:, h * HD:(h + 1) * HD], NT, precision=HIGHEST)
                out = out + jnp.where(head == h, sums, 0.0)
            dbs_ref[...] = out
            exchange.finish()

    anyspec = pl.BlockSpec(memory_space=pl.ANY)
    res = pl.pallas_call(
        body, grid=(nb,),
        in_specs=[pl.BlockSpec((rows, DG), lambda i: (i, 0)), pl.BlockSpec((rows, DG), lambda i: (i, 1)),
                  pl.BlockSpec((rows, DG), lambda i: (i, 0)),
                  _full((NH, BLK, BLK)), _full((BLK, NH)), _full((1, DG)), _full((1, DG))] + [anyspec] * nw,
        out_specs=[pl.BlockSpec((rows, 2 * DG), lambda i: (i, 2)), _full((NH * BLK, BLK)), _full((8, BLK)), _full((8, DG))]
        + [anyspec] * nw,
        out_shape=[SDS((T, DIN), BF16), SDS((NH * BLK, BLK), F32), SDS((8, BLK), F32), SDS((8, DG), F32)]
        + _core_exchange_shapes(grads),
        scratch_shapes=[pltpu.VMEM((BLK, DG), F32)] + _core_exchange_sems(nw),
        compiler_params=_arb(), name="gmlp_bwd")(proj, proj, dycat, ws_b, bst, lnw, lnb, *grads)
    return res[:4], res[4:]


def _hgrn_bwd(proj, o_pre, a_all, st_all, dycat, lower_bounds, gn_w, dproj, tables, sums):
    T = proj.shape[0]
    nc = T // CH
    nch = min(HGRN_CHUNKS_PER_STEP, nc)
    steps = nc // nch
    w_st, w_st_t, _, masks_sym = tables
    n_lev = len(LEVELS)
    nw = len(sums)

    def body(*refs):
        q_ref, f_ref, i_ref, g_ref, o_ref, a_ref, st_ref, dy_ref, lbp_ref, gn_ref, w_ref, wt_ref, ms_ref = refs[:13]
        dp_ref, dlb_ref, dgn_ref = refs[14 + nw:17 + nw]
        ds_scr, dx_scr, send_sems, recv_sems = refs[17 + 2 * nw:]
        exchange = _ChipExchange(refs[14:14 + nw], refs[17 + nw:17 + 2 * nw], send_sems, recv_sems)
        i = pl.program_id(0)

        @pl.when(i == 0)
        def _():
            exchange.start()
            ds_scr[...] = jnp.zeros_like(ds_scr)
            dlb_ref[...] = jnp.zeros_like(dlb_ref)
            dgn_ref[...] = jnp.zeros_like(dgn_ref)

        lb, omlb = _lower_bound(lbp_ref)
        row = lax.broadcasted_iota(jnp.int32, (CH, 1), 0)
        eye = lax.broadcasted_iota(jnp.int32, (CH, CH), 0) == lax.broadcasted_iota(jnp.int32, (CH, CH), 1)
        lower = lax.broadcasted_iota(jnp.int32, (CH, CH), 0) > lax.broadcasted_iota(jnp.int32, (CH, CH), 1)
        dgn = jnp.zeros((1, HD), F32)
        pre = []
        for ci in range(nch):
            rs = slice(ci * CH, (ci + 1) * CH)
            q = q_ref[rs, :]
            v = i_ref[rs, :]
            g = g_ref[rs, :]
            sq, qf, sig, f, k, e = _hgrn_gates(q, f_ref[rs, :], lb, omlb, w_ref)
            eb = e[0]
            ekd = e[1]
            kd = k * ekd
            qe = qf * eb
            dob_h, dqe_h, dqf_h, dki_h, dv_h, dg_h = [], [], [], [], [], []
            for h in range(NH):
                sl = slice(h * HD, (h + 1) * HD)
                o = o_ref[rs, sl]
                ro = _rms(o)
                oh = o * ro
                gh = g[:, sl]
                sg = _sigmoid(gh)
                dyb = dy_ref[rs, sl]
                dg_h.append(dyb * (oh * gn_ref[...]) * (sg * (1.0 + gh * (1.0 - sg))))
                don = dyb * (gh * sg)
                dgn = dgn + jnp.sum(don * oh, axis=0, keepdims=True)
                dob = _rms_bwd(oh, ro, don * gn_ref[...]).astype(BF16)
                vb = v[:, sl].astype(BF16)
                qh, kh = qf[:, sl], k[:, sl]
                dqe = _dot(dob, st_ref[ci, h].astype(BF16))
                da = _dot(dob, vb, NT)
                ddiag = jnp.sum(jnp.where(eye, da, 0.0), axis=-1, keepdims=True)
                dsym = jnp.where(lower, da, _dot(vb, dob, NT))
                upper_part = jnp.zeros((CH, HD), F32)
                both = jnp.zeros((CH, HD), F32)
                for li in range(n_lev):
                    el, up, y = _level_factor(e, li, sl, row, qh, kh)
                    dyv = _dot((ms_ref[li] * dsym).astype(BF16), y.astype(BF16))
                    dx_scr[ci, (2 + li) * CH:(3 + li) * CH, sl] = dyv * y
                    dye = dyv * el
                    upper_part = upper_part + jnp.where(up, dye, 0.0)
                    both = both + dye
                dob_h.append(dob)
                dqe_h.append(dqe)
                dqf_h.append(dqe * eb[:, sl] + ddiag * kh + upper_part)
                dki_h.append(ddiag * qh + (both - upper_part))
                dv_h.append(_dot(a_ref[ci, h].astype(BF16), dob, TN))
            dp_ref[rs, 0:DH] = (jnp.concatenate(dqf_h, axis=1) * (sq * (1.0 + q * (1.0 - sq)))).astype(BF16)
            dp_ref[rs, 3 * DH:4 * DH] = jnp.concatenate(dg_h, axis=1).astype(BF16)
            pre.append((v, sig, f, eb, ekd, kd, qe, dob_h, jnp.concatenate(dqe_h, axis=1), dki_h, dv_h))
        dgn_ref[0:1, :] += dgn
        for ci in reversed(range(nch)):
            rs = slice(ci * CH, (ci + 1) * CH)
            v, sig, f, eb, ekd, kd, qe, dob_h, dqe, dki_h, dv_h = pre[ci]
            ebl = eb[CH - 1:CH, :]
            dbl_h, dkd_h, dv2_h = [], [], []
            for h in range(NH):
                sl = slice(h * HD, (h + 1) * HD)
                dst1 = ds_scr[h]
                dst1b = dst1.astype(BF16)
                ds_scr[h] = dst1 * ebl[:, sl] + _dot(dob_h[h], qe[:, sl].astype(BF16), TN)
                dbl_h.append(ebl[:, sl] * jnp.sum(st_ref[ci, h] * dst1, axis=0, keepdims=True))
                dkd_h.append(_dot(v[:, sl].astype(BF16), dst1b))
                dv2_h.append(dv_h[h] + _dot(kd[:, sl].astype(BF16), dst1b, NT))
            dkd = jnp.concatenate(dkd_h, axis=1)
            dx_scr[ci, 0:CH, :] = dqe * qe + jnp.where(row == CH - 1, jnp.concatenate(dbl_h, axis=1), 0.0)
            dx_scr[ci, CH:2 * CH, :] = dkd * kd
            dlf = _split_dot(wt_ref[...], dx_scr[ci], 2)
            df = dlf / f - (dkd * ekd + jnp.concatenate(dki_h, axis=1))
            dlb_ref[0:1, :] += jnp.sum(df * (1.0 - sig), axis=0, keepdims=True)
            dp_ref[rs, DH:2 * DH] = (df * omlb * sig * (1.0 - sig)).astype(BF16)
            dp_ref[rs, 2 * DH:3 * DH] = jnp.concatenate(dv2_h, axis=1).astype(BF16)

        @pl.when(i == steps - 1)
        def _():
            gl = dlb_ref[0:1, :] * lb * omlb
            dlb_ref[0:1, :] = gl
            dlb_ref[1:2, :] = -gl
            exchange.finish()

    rev = lambda j: pl.BlockSpec((nch * CH, DH), lambda c: (steps - 1 - c, j))
    anyspec = pl.BlockSpec(memory_space=pl.ANY)
    res = pl.pallas_call(
        body, grid=(steps,),
        in_specs=[rev(2), rev(3), rev(4), rev(5), rev(0),
                  pl.BlockSpec((nch, NH, CH, CH), lambda c: (steps - 1 - c, 0, 0, 0)),
                  pl.BlockSpec((nch, NH, HD, HD), lambda c: (steps - 1 - c, 0, 0, 0)),
                  rev(1), _full((2, DH)), _full((1, HD)),
                  _full(w_st.shape), _full(w_st_t.shape), _full(masks_sym.shape),
                  anyspec] + [anyspec] * nw,
        out_specs=[pl.BlockSpec((nch * CH, 4 * DH), lambda c: (steps - 1 - c, 0)), _full((8, DH)), _full((8, HD))]
        + [anyspec] * nw,
        out_shape=[SDS((T, DIN), BF16), SDS((8, DH), F32), SDS((8, HD), F32)] + _slot_shapes(sums),
        scratch_shapes=[pltpu.VMEM((NH, HD, HD), F32), pltpu.VMEM((nch, (2 + n_lev) * CH, DH), F32)] + _exchange_sems(nw),
        input_output_aliases={13: 0},
        compiler_params=_arb(), name="hgrn_bwd")(proj, proj, proj, proj, o_pre, a_all, st_all, dycat, lower_bounds, gn_w,
                                                 w_st, w_st_t, masks_sym, dproj, *sums)
    return res[:3], res[3:]


def _proj_in_bwd(dproj, x, dx1, nw, sc, w_in_b, tm, sums):
    T = x.shape[0]
    ns = len(sums)
    steps = T // tm

    def body(*refs):
        dp_ref, x_ref, dx1_ref, nw_ref, sc_ref, w_ref = refs[:6]
        gx_ref, acc_ref = refs[6 + ns:8 + ns]
        exchange = _ChipExchange(refs[6:6 + ns], refs[8 + ns:8 + 2 * ns], *refs[8 + 2 * ns:])

        @pl.when(pl.program_id(0) == 0)
        def _():
            exchange.start()
            acc_ref[...] = jnp.zeros_like(acc_ref)

        dh = _dot(dp_ref[:, 0:4 * DH], w_ref[:, 2 * DG:DIN], NT) + _dot(dp_ref[:, 4 * DH:DIN], w_ref[:, 0:2 * DG], NT)
        xv = x_ref[...]
        r = _rms(xv)
        xh = xv * r
        n1 = xh * nw_ref[...]
        acc_ref[0:1, :] += jnp.sum(dh, axis=0, keepdims=True)
        acc_ref[1:2, :] += jnp.sum(dh * n1, axis=0, keepdims=True)
        dn = dh * (1.0 + sc_ref[...])
        acc_ref[2:3, :] += jnp.sum(dn * xh, axis=0, keepdims=True)
        gx_ref[...] = dx1_ref[...] + _rms_bwd(xh, r, dn * nw_ref[...])

        @pl.when(pl.program_id(0) == steps - 1)
        def _():
            exchange.finish()

    row = lambda i: (i, 0)
    anyspec = pl.BlockSpec(memory_space=pl.ANY)
    res = pl.pallas_call(
        body, grid=(steps,),
        in_specs=[pl.BlockSpec((tm, DIN), row), pl.BlockSpec((tm, D), row), pl.BlockSpec((tm, D), row),
                  _full((1, D)), _ada_part(ADA_SC1), _resident((D, DIN))] + [anyspec] * ns,
        out_specs=[pl.BlockSpec((tm, D), row), _full((8, D))] + [anyspec] * ns,
        out_shape=[SDS((T, D), F32), SDS((8, D), F32)] + _slot_shapes(sums),
        scratch_shapes=_exchange_sems(ns),
        compiler_params=_arb(), name="proj_in_bwd")(dproj, x, dx1, nw, sc, w_in_b, *sums)
    return res[:2], res[2:]


def _wgrad(a, b, bk, bn, tt, name, bf16_copy=False):
    T, K = a.shape
    N = b.shape[1]
    nn, nk, nt = N // bn, K // bk, T // tt
    bmap = lambda n, k, t: (t, n)

    def body(a_ref, b_ref, o_ref, *copy_ref):
        @pl.when(pl.program_id(2) == 0)
        def _():
            o_ref[...] = jnp.zeros_like(o_ref)

        o_ref[0] += _dot(a_ref[...], b_ref[...], TN)

        if bf16_copy:
            @pl.when(pl.program_id(2) == nt - 1)
            def _():
                copy_ref[0][...] = o_ref[...].astype(BF16)

    ospec = pl.BlockSpec((1, bk, bn), lambda n, k, t: (n, k, 0))
    return pl.pallas_call(
        body, grid=(nn, nk, nt),
        in_specs=[pl.BlockSpec((tt, bk), lambda n, k, t: (t, k)), pl.BlockSpec((tt, bn), bmap)],
        out_specs=[ospec, ospec] if bf16_copy else ospec,
        out_shape=[SDS((nn, K, bn), F32), SDS((nn, K, bn), BF16)] if bf16_copy else SDS((nn, K, bn), F32),
        compiler_params=_arb(3), name=name)(a, b)


def _adam_math(w, g, m, v):
    m = B1 * m + (1.0 - B1) * g
    v = B2 * v + (1.0 - B2) * (g * g)
    m_hat = m / (1.0 - B1 ** STEP)
    v_hat = v / (1.0 - B2 ** STEP)
    return -LR * (m_hat / (jnp.sqrt(v_hat) + AEPS) + WD * w), m, v


def _adamw_halves(w, mine, sibling, m, v, c_idx, rb, name):
    R, C = w.shape
    nb = (R // 2) // rb

    def body(c_ref, w_ref, a_ref, b_ref, m_ref, v_ref, g_out, d_out, m_out, v_out):
        g = jnp.where(pl.program_id(0) == c_ref[0], a_ref[...], b_ref[...])
        g_out[...] = g
        d_out[...], m_out[...], v_out[...] = _adam_math(w_ref[...], g, m_ref[...], v_ref[...])

    whole = pl.BlockSpec((rb, C), lambda hh, i, cr: (hh * nb + i, 0))
    half = pl.BlockSpec((rb, C), lambda hh, i, cr: (i, 0))
    return pl.pallas_call(
        body,
        grid_spec=pltpu.PrefetchScalarGridSpec(
            num_scalar_prefetch=1, grid=(2, nb), in_specs=[whole, half, half, whole, whole], out_specs=[whole] * 4),
        out_shape=[SDS((R, C), F32)] * 4, compiler_params=_arb(2), name=name)(c_idx, w, mine, sibling, m, v)


def _ada_forward(c_all, w_ada):
    n = w_ada.shape[1]

    def body(c_ref, w_ref, ca_ref, p_ref):
        cv = c_ref[...]
        ca = cv * _sigmoid(cv)
        ca_ref[...] = ca
        p_ref[...] = _dot(ca, w_ref[...], precision=HIGHEST)

    return pl.pallas_call(
        body, grid=(n // 512,),
        in_specs=[_full((N_DEV, D)), pl.BlockSpec((D, 512), lambda i: (0, i))],
        out_specs=[_full((N_DEV, D)), pl.BlockSpec((N_DEV, 512), lambda i: (0, i))],
        out_shape=[SDS((N_DEV, D), F32), SDS((N_DEV, n), F32)],
        compiler_params=_arb(), name="ada_forward")(c_all, w_ada)


def _ada_wgrad_adam(cact_t, dada_all, w, m, v, chip_idx):
    R, C = w.shape
    rb = 256

    def body(j_ref, c_ref, d_ref, w_ref, m_ref, v_ref, g_out, d_out, m_out, v_out):
        g = _dot(c_ref[...], d_ref[...], precision=HIGHEST)
        g_out[...] = g
        d_out[...], m_out[...], v_out[...] = _adam_math(w_ref[...], g, m_ref[...], v_ref[...])

    spec = pl.BlockSpec((rb, C), lambda i, j: (i, 0))
    return pl.pallas_call(
        body,
        grid_spec=pltpu.PrefetchScalarGridSpec(
            num_scalar_prefetch=1, grid=(R // rb,),
            in_specs=[pl.BlockSpec((rb, N_DEV), lambda i, j: (i, 0)), pl.BlockSpec((N_DEV, C), lambda i, j: (0, j[0])),
                      spec, spec, spec],
            out_specs=[spec] * 4),
        out_shape=[SDS((R, C), F32)] * 4,
        compiler_params=_arb(), name="ada_wgrad_adam")(chip_idx, cact_t, dada_all, w, m, v)


SMALL_NAMES = ('b_ada', 'norm1_w', 'norm2_w', 'final_norm_w', 'v_ln_w', 'v_ln_b', 'lower_bounds', 'gn_w', 'b_s', 'w_s')


def _small_finalize(gathered, params, moms, vels):
    n_in = len(gathered)

    def body(*refs):
        acc1, acc2, dln, dlb, dgn, dbs, dws = refs[:n_in]
        prm = [dict(zip(SMALL_NAMES, refs[n_in + k * 10:n_in + (k + 1) * 10])) for k in range(3)]
        outs = [dict(zip(SMALL_NAMES, refs[n_in + 30 + k * 10:n_in + 30 + (k + 1) * 10])) for k in range(4)]
        loss_ref, dada_ref = refs[n_in + 70:n_in + 72]

        def dev_sum(ref, first, n):
            per = ref.shape[0] // N_DEV
            g = ref[first:first + n, :]
            for dev in range(1, N_DEV):
                g = g + ref[dev * per + first:dev * per + first + n, :]
            return g

        def update(n, g, cols=slice(None)):
            outs[0][n][:, cols] = g
            outs[1][n][:, cols], outs[2][n][:, cols], outs[3][n][:, cols] = _adam_math(
                prm[0][n][:, cols], g, prm[1][n][:, cols], prm[2][n][:, cols])

        ada_rows = ((acc1, 0), (acc1, 1), (acc2, 5), (acc2, 2), (acc2, 1), (acc2, 0))
        for k, (ref, r) in enumerate(ada_rows):
            update('b_ada', dev_sum(ref, r, 1), slice(k * D, (k + 1) * D))
            for dev in range(N_DEV):
                dada_ref[dev:dev + 1, k * D:(k + 1) * D] = ref[8 * dev + r:8 * dev + r + 1, :]
        update('norm1_w', dev_sum(acc1, 2, 1))
        update('norm2_w', dev_sum(acc2, 3, 1))
        update('final_norm_w', dev_sum(acc2, 4, 1))
        update('v_ln_w', dev_sum(dln, 0, 1))
        update('v_ln_b', dev_sum(dln, 1, 1))
        update('lower_bounds', dev_sum(dlb, 0, 2))
        update('gn_w', dev_sum(dgn, 0, 1))
        update('b_s', dev_sum(dbs, 0, NH))
        update('w_s', dev_sum(dws, 0, NH * BLK))
        loss_ref[...] = jnp.sum(dev_sum(acc2, 6, 1), axis=-1, keepdims=True)

    shapes = [SDS(params[n].shape, F32) for n in SMALL_NAMES]
    res = pl.pallas_call(
        body, out_shape=shapes * 4 + [SDS((1, 1), F32), SDS((N_DEV, 6 * D), F32)], name="small_finalize")(
            *gathered, *[d[n] for d in (params, moms, vels) for n in SMALL_NAMES])
    return [dict(zip(SMALL_NAMES, res[k * 10:(k + 1) * 10])) for k in range(4)], res[40], res[41]


def _position():
    x, y, c = lax.axis_index("x"), lax.axis_index("y"), lax.axis_index("c")
    return x, y, c


def _chip_at(x, y, r):
    return (x ^ (r >> 1), y ^ (r & 1))


def _gather_rows(ins, outs, send_sems, recv_sems, local_sems, after_issue=None):
    nb = len(ins)
    x, y, c = _position()
    me, sibling = (x, y, c), (x, y, 1 - c)
    chips = [_chip_at(x, y, r) for r in (1, 2, 3)]

    def rows(b, px, py, pc):
        m_per = ins[b].shape[0]
        return outs[b].at[pl.ds((4 * px + 2 * py + pc) * m_per, m_per), :]

    def copy(b, k, blk, to, src=None):
        return pltpu.make_async_remote_copy(
            src_ref=rows(b, *blk) if src is None else src, dst_ref=rows(b, *blk),
            send_sem=send_sems.at[7 * b + k], recv_sem=recv_sems.at[7 * b + k], device_id=to, device_id_type=MESH)

    local, sent = [], []
    for b in range(nb):
        mine = pltpu.make_async_copy(ins[b], rows(b, *me), local_sems.at[b])
        mine.start()
        local.append(mine)
        first = [copy(b, 0, me, sibling, src=ins[b])]
        first += [copy(b, 1 + j, me, (*chip, c), src=ins[b]) for j, chip in enumerate(chips)]
        for cp in first:
            cp.start()
        sent += first
    if after_issue is not None:
        after_issue()
    for b in range(nb):
        for j, chip in enumerate(chips):
            copy(b, 1 + j, (*chip, c), me).wait_recv()
            passed = copy(b, 4 + j, (*chip, c), sibling)
            passed.start()
            sent.append(passed)
    for b in range(nb):
        copy(b, 0, sibling, me).wait_recv()
        for j, chip in enumerate(chips):
            copy(b, 4 + j, (*chip, 1 - c), me).wait_recv()
    for cp in sent:
        cp.wait_send()
    for cp in local:
        cp.wait()


def _gather_rows_shapes(blocks):
    return [SDS((N_DEV * b.shape[0], b.shape[1]), b.dtype) for b in blocks]


def _gather_rows_sems(nb):
    return [pltpu.SemaphoreType.DMA((7 * nb,)), pltpu.SemaphoreType.DMA((7 * nb,)), pltpu.SemaphoreType.DMA((nb,))]


def _all_gather_rows(blocks, name):
    nb = len(blocks)

    def body(*refs):
        _gather_rows(refs[:nb], refs[nb:2 * nb], *refs[2 * nb:])

    vmem = pl.BlockSpec(memory_space=pltpu.VMEM)
    return pl.pallas_call(
        body, out_shape=_gather_rows_shapes(blocks), in_specs=[vmem] * nb, out_specs=[vmem] * nb,
        scratch_shapes=_gather_rows_sems(nb), name=name)(*blocks)


def _place_shard(w_shard, axis, chip_idx, name):
    R, C = w_shard.shape
    rb = _row_block(R)
    nb = R // rb
    full = (R * N_CHIPS, C) if axis == 0 else (R, C * N_CHIPS)
    omap = (lambda i, j: (j[0] * nb + i, 0)) if axis == 0 else (lambda i, j: (i, j[0]))

    def body(j_ref, w_ref, o_ref):
        o_ref[...] = w_ref[...].astype(BF16)

    return pl.pallas_call(
        body,
        grid_spec=pltpu.PrefetchScalarGridSpec(
            num_scalar_prefetch=1, grid=(nb,), in_specs=[pl.BlockSpec((rb, C), lambda i, j: (i, 0))],
            out_specs=pl.BlockSpec((rb, C), omap)),
        out_shape=SDS(full, BF16), compiler_params=_arb(), name=name)(chip_idx, w_shard)


class _WeightGather:
    def __init__(self, refs, axes, send_sems, recv_sems):
        self.refs, self.axes, self.send_sems, self.recv_sems = refs, axes, send_sems, recv_sems
        self.x, self.y, self.c = _position()
        self.j = 2 * self.x + self.y
        self.n = 3 * len(refs)

    def _half(self, w, chip_idx, half):
        ref, axis = self.refs[w], self.axes[w]
        if axis == 0:
            size = ref.shape[0] // N_CHIPS
            return ref.at[pl.ds(chip_idx * size + half * (size // 2), size // 2), :]
        size = ref.shape[1] // N_CHIPS
        rows = ref.shape[0] // 2
        return ref.at[pl.ds(half * rows, rows), pl.ds(chip_idx * size, size)]

    def _ici(self, w, r, chip_idx):
        k = 3 * w + r - 1
        piece = self._half(w, chip_idx, self.c)
        return pltpu.make_async_remote_copy(
            src_ref=piece, dst_ref=piece, send_sem=self.send_sems.at[k], recv_sem=self.recv_sems.at[k],
            device_id=(*_chip_at(self.x, self.y, r), self.c), device_id_type=MESH)

    def _d2d(self, w, r, half):
        k = self.n + 3 * w + r - 1
        piece = self._half(w, self.j ^ r, half)
        return pltpu.make_async_remote_copy(
            src_ref=piece, dst_ref=piece, send_sem=self.send_sems.at[k], recv_sem=self.recv_sems.at[k],
            device_id=(self.x, self.y, 1 - self.c), device_id_type=MESH)

    def _each(self):
        return [(w, r) for w in range(len(self.refs)) for r in (1, 2, 3)]

    def start(self):
        for w, r in self._each():
            self._ici(w, r, self.j).start()

    def forward(self):
        for w, r in self._each():
            self._ici(w, r, self.j ^ r).wait_recv()
            self._d2d(w, r, self.c).start()

    def finish(self):
        for w, r in self._each():
            self._ici(w, r, self.j).wait_send()
            self._d2d(w, r, self.c).wait_send()
            self._d2d(w, r, 1 - self.c).wait_recv()


def _gather_sems(n_weights):
    return [pltpu.SemaphoreType.DMA((6 * n_weights,)), pltpu.SemaphoreType.DMA((6 * n_weights,))]


def _gather_weights(placed, axes, row_blocks, name):
    nw, nb = len(placed), len(row_blocks)

    def body(*refs):
        w_outs, b_ins, b_outs = refs[nw + nb:2 * nw + nb], refs[nw:nw + nb], refs[2 * nw + nb:2 * (nw + nb)]
        sems = refs[2 * (nw + nb):]
        g = _WeightGather(w_outs, axes, *sems[:2])
        _gather_rows(b_ins, b_outs, *sems[2:], after_issue=g.start)
        g.forward()
        g.finish()

    anyspec = pl.BlockSpec(memory_space=pl.ANY)
    vmem = pl.BlockSpec(memory_space=pltpu.VMEM)
    res = pl.pallas_call(
        body, out_shape=[SDS(a.shape, a.dtype) for a in placed] + _gather_rows_shapes(row_blocks),
        in_specs=[anyspec] * nw + [vmem] * nb, out_specs=[anyspec] * nw + [vmem] * nb,
        scratch_shapes=_gather_sems(nw) + _gather_rows_sems(nb), input_output_aliases={i: i for i in range(nw)},
        name=name)(*placed, *row_blocks)
    return res[:nw], res[nw:]


class _ChipExchange:
    def __init__(self, ins, outs, send_sems, recv_sems):
        self.ins, self.outs, self.send_sems, self.recv_sems = ins, outs, send_sems, recv_sems
        self.x, self.y, self.c = _position()
        self.j = 2 * self.x + self.y

    def _copies(self):
        for w in range(len(self.ins)):
            for r in (1, 2, 3):
                k = 3 * w + r - 1
                yield pltpu.make_async_remote_copy(
                    src_ref=self.ins[w].at[self.j ^ r], dst_ref=self.outs[w].at[r - 1],
                    send_sem=self.send_sems.at[k], recv_sem=self.recv_sems.at[k],
                    device_id=(*_chip_at(self.x, self.y, r), self.c), device_id_type=MESH)

    def start(self):
        for cp in self._copies():
            cp.start()

    def finish(self):
        for cp in self._copies():
            cp.wait()


def _exchange_sems(n_weights):
    return [pltpu.SemaphoreType.DMA((3 * n_weights,)), pltpu.SemaphoreType.DMA((3 * n_weights,))]


class _CoreExchange:
    def __init__(self, ins, outs, send_sems, recv_sems):
        self.ins, self.outs, self.send_sems, self.recv_sems = ins, outs, send_sems, recv_sems
        self.x, self.y, self.c = _position()

    def _copies(self):
        for w in range(len(self.ins)):
            yield pltpu.make_async_remote_copy(
                src_ref=self.ins[w].at[:, 1 - self.c], dst_ref=self.outs[w],
                send_sem=self.send_sems.at[w], recv_sem=self.recv_sems.at[w],
                device_id=(self.x, self.y, 1 - self.c), device_id_type=MESH)

    def start(self):
        for cp in self._copies():
            cp.start()

    def finish(self):
        for cp in self._copies():
            cp.wait()


def _core_exchange_shapes(grads):
    return [SDS((g.shape[0], g.shape[2], g.shape[3]), g.dtype) for g in grads]


def _core_exchange_sems(n):
    return [pltpu.SemaphoreType.DMA((n,)), pltpu.SemaphoreType.DMA((n,))]


def _exchange_core_halves(grads, name):
    nw = len(grads)

    def body(*refs):
        ex = _CoreExchange(refs[:nw], refs[nw:2 * nw], *refs[2 * nw:])
        ex.start()
        ex.finish()

    anyspec = pl.BlockSpec(memory_space=pl.ANY)
    return pl.pallas_call(
        body, out_shape=_core_exchange_shapes(grads), in_specs=[anyspec] * nw, out_specs=[anyspec] * nw,
        scratch_shapes=_core_exchange_sems(nw), name=name)(*grads)


def _add_core_halves(g4, recv, c_idx, rb, name):
    ns, _, rh, C = g4.shape

    def body(c_ref, g_ref, r_ref, o_ref):
        o_ref[...] = (g_ref[0] + r_ref[...]).astype(BF16)

    return pl.pallas_call(
        body,
        grid_spec=pltpu.PrefetchScalarGridSpec(
            num_scalar_prefetch=1, grid=(ns, rh // rb),
            in_specs=[pl.BlockSpec((1, 1, rb, C), lambda s, i, cr: (s, cr[0], i, 0)),
                      pl.BlockSpec((1, rb, C), lambda s, i, cr: (s, i, 0))],
            out_specs=pl.BlockSpec((1, rb, C), lambda s, i, cr: (s, i, 0))),
        out_shape=SDS((ns, rh, C), BF16), compiler_params=_arb(2), name=name)(c_idx, g4, recv)


def _add_core_halves_in(g4, recv, c_idx, name):
    n_slabs, _, rh, C = g4.shape
    cb = 256
    per_slab, per_chip, n_blocks = C // cb, DIN // N_CHIPS // cb, DIN // cb

    def stored(s, k):
        sb = (per_chip * s + k + 4 * DH // cb) % n_blocks
        return sb // per_slab, sb % per_slab

    def body(c_ref, g_ref, r_ref, o_ref):
        o_ref[...] = (g_ref[0] + r_ref[...].astype(F32)).astype(BF16)

    return pl.pallas_call(
        body,
        grid_spec=pltpu.PrefetchScalarGridSpec(
            num_scalar_prefetch=1, grid=(N_CHIPS, per_chip),
            in_specs=[pl.BlockSpec((1, 1, rh, cb), lambda s, k, cr: (stored(s, k)[0], cr[0], 0, stored(s, k)[1])),
                      pl.BlockSpec((1, rh, cb), lambda s, k, cr: (stored(s, k)[0], 0, stored(s, k)[1]))],
            out_specs=pl.BlockSpec((1, rh, cb), lambda s, k, cr: (s, 0, k))),
        out_shape=SDS((N_CHIPS, rh, DIN // N_CHIPS), BF16), compiler_params=_arb(2), name=name)(c_idx, g4, recv)


def _slot_shapes(sums):
    return [SDS((3,) + s.shape[1:], s.dtype) for s in sums]


def _add_chips(own, slots, order, rb, name):
    _, rh, C = slots.shape

    def body(o_ref, own_ref, a_ref, b_ref, c_ref, d_ref, out_ref):
        mine = own_ref[0].astype(F32)
        t = [jnp.where(o_ref[i] == 0, mine, r[0].astype(F32)) for i, r in enumerate((a_ref, b_ref, c_ref, d_ref))]
        out_ref[...] = ((t[0] + t[1]) + t[2]) + t[3]

    def spec(i):
        return pl.BlockSpec((1, rb, C), lambda t, o: (jnp.maximum(o[i], 1) - 1, t, 0))

    return pl.pallas_call(
        body,
        grid_spec=pltpu.PrefetchScalarGridSpec(
            num_scalar_prefetch=1, grid=(rh // rb,),
            in_specs=[pl.BlockSpec((1, rb, C), lambda t, o: (o[4], t, 0)), spec(0), spec(1), spec(2), spec(3)],
            out_specs=pl.BlockSpec((rb, C), lambda t, o: (t, 0))),
        out_shape=SDS((rh, C), F32), compiler_params=_arb(), name=name)(order, own, slots, slots, slots, slots)


def _share_halves(halves):
    nw = len(halves)

    def body(*refs):
        ins, outs = refs[:nw], refs[nw:2 * nw]
        send_sems, recv_sems = refs[2 * nw:]
        x, y, c = _position()
        started = []
        for w in range(nw):
            cp = pltpu.make_async_remote_copy(
                src_ref=ins[w], dst_ref=outs[w], send_sem=send_sems.at[w], recv_sem=recv_sems.at[w],
                device_id=(x, y, 1 - c), device_id_type=MESH)
            cp.start()
            started.append(cp)
        for cp in started:
            cp.wait()

    anyspec = pl.BlockSpec(memory_space=pl.ANY)
    return pl.pallas_call(
        body, out_shape=[SDS(h.shape, F32) for h in halves], in_specs=[anyspec] * nw, out_specs=[anyspec] * nw,
        scratch_shapes=[pltpu.SemaphoreType.DMA((nw,)), pltpu.SemaphoreType.DMA((nw,))],
        name="share_halves")(*halves)


def _small_2d(b_ada, norm1_w, norm2_w, final_norm_w, v_ln_w, v_ln_b, lower_bounds, gn_w, b_s, w_s):
    return dict(zip(SMALL_NAMES, (b_ada, norm1_w, norm2_w, final_norm_w.reshape(1, D), v_ln_w, v_ln_b, lower_bounds, gn_w,
                                  b_s.reshape(NH, BLK), w_s.reshape(NH * BLK, BLK))))


def _small_original_shapes(d):
    out = dict(d)
    out['final_norm_w'] = d['final_norm_w'].reshape(D)
    out['b_s'] = d['b_s'].reshape(1, NH, BLK)
    out['w_s'] = d['w_s'].reshape(1, NH, BLK, BLK)
    return out


def _row_block(r):
    for cand in (256, 176, 128, 64, 32, 16, 8):
        if r % cand == 0:
            return cand
    return r


def kernel(x, c, w_ada, b_ada, norm1_w, w_in, w_s, b_s, v_ln_w, v_ln_b, lower_bounds, gn_w, w_out, norm2_w, w_ffn_in, w_ffn_out, final_norm_w, loss_target, m_w_ada, m_b_ada, m_norm1_w, m_w_in, m_w_s, m_b_s, m_v_ln_w, m_v_ln_b, m_lower_bounds, m_gn_w, m_w_out, m_norm2_w, m_w_ffn_in, m_w_ffn_out, m_final_norm_w, v_w_ada, v_b_ada, v_norm1_w, v_w_in, v_w_s, v_b_s, v_v_ln_w, v_v_ln_b, v_lower_bounds, v_gn_w, v_w_out, v_norm2_w, v_w_ffn_in, v_w_ffn_out, v_final_norm_w):
    T = x.shape[1]
    tm, tp = min(TOKEN_TILE, T), min(PROJ_TILE, T)
    px, py, pc = _position()
    chip = 2 * px + py
    me = 4 * px + 2 * py + pc
    x2d = x.reshape(T, D)
    tgt = loss_target.reshape(T, D)

    chip_idx = jnp.reshape(chip, (1,)).astype(jnp.int32)
    c_idx = jnp.reshape(pc, (1,)).astype(jnp.int32)
    (w_in_b,), (c_all,) = _gather_weights(
        [_place_shard(w_in[0], 1, chip_idx, "place_in")], [1], [jnp.broadcast_to(c, (8, D))], "gather_w_in_and_c")
    placed = [_place_shard(w_out[0], 0, chip_idx, "place_out"), _place_shard(w_ffn_in[0], 1, chip_idx, "place_ffn_in"),
              _place_shard(w_ffn_out[0], 0, chip_idx, "place_ffn_out")]

    cact, ada_part = _ada_forward(c_all.reshape(N_DEV, 8, D)[:, 0, :], w_ada[0])
    n_ada = ada_part.shape[1]
    (ada_all,) = _all_gather_rows([ada_part], "gather_ada")
    ada_all = ada_all.reshape(N_CHIPS, 2, N_DEV, n_ada)[:, 0]
    ada = lax.dynamic_index_in_dim(ada_all, me, axis=1, keepdims=False).reshape(1, 6 * D) + b_ada

    rr = lax.broadcasted_iota(jnp.int32, (BLK, BLK), 0) // CH
    cc = lax.broadcasted_iota(jnp.int32, (BLK, BLK), 1) // CH
    ws_b = jnp.where((rr >= cc)[None], w_s[0], 0.0).astype(BF16)
    bst = b_s[0].T
    lnw, lnb = v_ln_w, v_ln_b
    nw1, nw2, fw = norm1_w, norm2_w, final_norm_w.reshape(1, D)

    tables = _hgrn_tables()
    (h1, proj, ycat, o_pre, a_all, st_all), (w_out_b, w_fi_b, w_fo_b) = _proj_hgrn_fwd(
        x2d, nw1, ada, w_in_b, lower_bounds, gn_w, tables, placed, [0, 1, 0])
    ycat = _gmlp_fwd(proj, ws_b, bst, lnw, lnb, ycat)

    dycat, dx1, h2, act, dff, dgu, dmix, acc2 = _token_local(
        x2d, ycat, tgt, ada, nw2, ada, ada, ada, fw, w_out_b, w_fi_b, w_fo_b, tm)

    tt = min(WGRAD_TOKENS, T)
    order = jnp.concatenate([chip ^ jnp.arange(N_CHIPS, dtype=jnp.int32), chip_idx]).astype(jnp.int32)

    def by_core_half(g):
        return g.reshape(g.shape[0], 2, g.shape[1] // 2, g.shape[2])

    def core_sums(g4, recv, names):
        return [_add_core_halves(a, b, c_idx, _row_block(a.shape[2]), "add_core_" + n) for a, b, n in zip(g4, recv, names)]

    def chip_sums(sums, slots, names):
        return [_add_chips(o, s, order, _row_block(s.shape[1]), "add_chips_" + n) for o, s, n in zip(sums, slots, names)]

    g_out = _wgrad(ycat, dmix, D, D, tt, "wgrad_out").reshape(N_CHIPS, D // N_CHIPS, D)
    g_fi = _wgrad(h2, dgu, D, FFB, tt, "wgrad_ffn_in")
    g_fo = _wgrad(act, dff, FFB, D, tt, "wgrad_ffn_out").reshape(N_CHIPS, DFF // N_CHIPS, D)
    late_names = ["out", "ffn_in", "ffn_out"]
    late_g4 = [by_core_half(g) for g in (g_out, g_fi, g_fo)]

    (dproj, dws, dbs, dln), late_recv = _gmlp_bwd(proj, dycat, ws_b, bst, lnw, lnb, late_g4)
    late_sums = core_sums(late_g4, late_recv, late_names)
    (dproj, dlb, dgn), late_slots = _hgrn_bwd(
        proj, o_pre, a_all, st_all, dycat, lower_bounds, gn_w, dproj, tables, late_sums)

    g_in, g_in_wire = _wgrad(h1, dproj, D, D, tt, "wgrad_in", bf16_copy=True)
    (in_recv,) = _exchange_core_halves([by_core_half(g_in_wire)], "exchange_core_halves_in")
    in_sums = [_add_core_halves_in(by_core_half(g_in), in_recv, c_idx, "add_core_in")]
    (grad_x, acc1), in_slots = _proj_in_bwd(dproj, x2d, dx1, nw1, ada, w_in_b, tp, in_sums)
    names = ["in"] + late_names
    halves = chip_sums(in_sums, in_slots, ["in"]) + chip_sums(late_sums, late_slots, late_names)
    sibling_halves = _share_halves(halves)

    big_w = [(w_in, m_w_in, v_w_in), (w_out, m_w_out, v_w_out), (w_ffn_in, m_w_ffn_in, v_w_ffn_in),
             (w_ffn_out, m_w_ffn_out, v_w_ffn_out)]
    big_out = []
    for mine, sib, (w, m, v), n in zip(halves, sibling_halves, big_w, names):
        res = _adamw_halves(w[0], mine, sib, m[0], v[0], c_idx, _row_block(mine.shape[0]), "adamw_" + n)
        big_out.append([r[None] for r in res])

    gathered = _all_gather_rows([acc1, acc2, dln, dlb, dgn, dbs, dws], "gather_small")
    small, loss, dada_all = _small_finalize(
        gathered,
        _small_2d(b_ada, norm1_w, norm2_w, final_norm_w, v_ln_w, v_ln_b, lower_bounds, gn_w, b_s, w_s),
        _small_2d(m_b_ada, m_norm1_w, m_norm2_w, m_final_norm_w, m_v_ln_w, m_v_ln_b, m_lower_bounds, m_gn_w, m_b_s, m_w_s),
        _small_2d(v_b_ada, v_norm1_w, v_norm2_w, v_final_norm_w, v_v_ln_w, v_v_ln_b, v_lower_bounds, v_gn_w, v_b_s, v_w_s))
    small = [_small_original_shapes(d) for d in small]
    loss = loss.reshape(())

    ada_out = [o[None] for o in _ada_wgrad_adam(cact.T, dada_all, w_ada[0], m_w_ada[0], v_w_ada[0], chip_idx)]

    order_names = ['w_ada', 'b_ada', 'norm1_w', 'w_in', 'w_s', 'b_s', 'v_ln_w', 'v_ln_b', 'lower_bounds', 'gn_w',
                   'w_out', 'norm2_w', 'w_ffn_in', 'w_ffn_out', 'final_norm_w']
    big_idx = {'w_in': 0, 'w_out': 1, 'w_ffn_in': 2, 'w_ffn_out': 3}
    outs = [loss, grad_x.reshape(1, T, D)]
    for kind in range(4):
        for n in order_names:
            if n == 'w_ada':
                outs.append(ada_out[kind])
            elif n in big_idx:
                outs.append(big_out[big_idx[n]][kind])
            else:
                outs.append(small[kind][n])
    return tuple(outs)
```

```python
import functools

import jax
import jax.numpy as jnp
import numpy as np
from jax import lax
from jax.experimental import pallas as pl
from jax.experimental.pallas import tpu as pltpu

F32 = jnp.float32
BF16 = jnp.bfloat16
SDS = jax.ShapeDtypeStruct
MESH = pl.DeviceIdType.MESH
HIGHEST = lax.Precision.HIGHEST

D = 1024
DG = 512
DH = 512
NH = 4
HD = 128
BLK = 128
CH = 64
DFF = 2816
DIN = 3072
FFB = 1408
LEVELS = (64, 32, 16, 8, 4, 2)
HGRN_CHUNKS_PER_STEP = 8
GMLP_ROWS_PER_STEP = 1024
TOKEN_TILE = 256
PROJ_TILE = 1024
WGRAD_TOKENS = 2048
N_CHIPS = 4
N_DEV = 8
EPS = 1e-6
LR, B1, B2, AEPS, WD, STEP = 0.001, 0.9, 0.999, 1e-08, 0.01, 10

NT = (((1,), (1,)), ((), ()))
TN = (((0,), (0,)), ((), ()))


def _full(shape):
    nd = len(shape)
    return pl.BlockSpec(shape, lambda *_: (0,) * nd)


ADA_SH1, ADA_SC1, ADA_G1, ADA_SH2, ADA_SC2, ADA_G2 = range(6)


def _ada_part(k):
    return pl.BlockSpec((1, D), lambda *_: (0, k))


def _resident(shape):
    nd = len(shape)
    return pl.BlockSpec(shape, lambda *_: (0,) * nd, pipeline_mode=pl.Buffered(1))


def _arb(n=1):
    return pltpu.CompilerParams(dimension_semantics=("arbitrary",) * n)


def _dot(a, b, dims=None, precision=None):
    if dims is None:
        return jnp.dot(a, b, preferred_element_type=F32, precision=precision)
    return lax.dot_general(a, b, dims, preferred_element_type=F32, precision=precision)


def _sigmoid(x):
    return jax.nn.sigmoid(x)


def _gelu_parts(x):
    cdf = 0.5 * (1.0 + lax.erf(x * 0.7071067811865476))
    pdf = jnp.exp(-0.5 * x * x) * 0.3989422804014327
    return x * cdf, cdf + x * pdf


def _rms(x):
    return lax.rsqrt(jnp.mean(x * x, axis=-1, keepdims=True) + EPS)


def _rms_bwd(xhat, r, gw):
    return r * (gw - xhat * jnp.mean(xhat * gw, axis=-1, keepdims=True))


def _lower_bound(lbp_ref):
    l0, l1 = lbp_ref[0:1, :], lbp_ref[1:2, :]
    m = jnp.maximum(l0, l1)
    e0, e1 = jnp.exp(l0 - m), jnp.exp(l1 - m)
    return e0 / (e0 + e1), e1 / (e0 + e1)


def _gmlp_common(u, v, lnw, lnb, ws_ref, bst_ref):
    ug, dug = _gelu_parts(u)
    vg, dvg = _gelu_parts(v)
    mu = jnp.mean(vg, axis=-1, keepdims=True)
    vc = vg - mu
    rstd = lax.rsqrt(jnp.mean(vc * vc, axis=-1, keepdims=True) + EPS)
    vhat = vc * rstd
    vn = vhat * lnw + lnb
    vnb = vn.astype(BF16)
    mixed = []
    for h in range(NH):
        sl = slice(h * HD, (h + 1) * HD)
        mixed.append(_dot(ws_ref[h], vnb[:, sl]) + bst_ref[:, h:h + 1])
    return ug, dug, dvg, rstd, vhat, vnb, jnp.concatenate(mixed, axis=1)


def _hgrn_tables():
    t = np.arange(CH)[:, None]
    j = np.arange(CH)[None, :]
    blocks = [j <= t, j > t]
    masks = []
    for n in LEVELS:
        mid = t - t % n + n // 2
        blocks.append(np.where(t >= mid, (j >= mid) & (j <= t), (j > t) & (j < mid)))
        masks.append((t // n == j // n) & (t % n >= n // 2) & (j % n < n // 2))
    w = np.concatenate(blocks, axis=0).astype(np.float32)
    m = np.stack(masks).astype(np.float32)
    return (jnp.asarray(w, BF16), jnp.asarray(w.T, BF16), jnp.asarray(m), jnp.asarray(m + m.transpose(0, 2, 1)))


def _split_dot(w, x, parts):
    acc = None
    for _ in range(parts):
        piece = x.astype(BF16)
        term = _dot(w, piece)
        acc = term if acc is None else acc + term
        x = x - piece.astype(F32)
    return acc


def _hgrn_decays(f, w_ref):
    b = _split_dot(w_ref[0:CH, :], jnp.log(f), 3)
    row = lax.broadcasted_iota(jnp.int32, (CH, 1), 0)
    blocks = [jnp.exp(b), jnp.exp(b[CH - 1:CH, :] - b)]
    for n in LEVELS:
        up = (row & (n // 2)) != 0
        if n >= 8:
            ref = b.reshape(CH // n, n, DH)[:, n // 2 - 1:n // 2, :]
            ref = jnp.broadcast_to(ref, (CH // n, n, DH)).reshape(CH, DH)
            blocks.append(jnp.exp(jnp.where(up, b - ref, ref - b)))
        elif n == 4:
            r4 = row & 3
            two = jnp.where(r4 == 3, pltpu.roll(f, 1, 0) * f, 1.0)
            blocks.append(jnp.where(r4 == 0, pltpu.roll(f, CH - 1, 0), jnp.where(r4 == 2, f, two)))
        else:
            blocks.append(jnp.where(up, f, 1.0))
    return blocks


def _hgrn_gates(q, fl, lb, omlb, w_ref):
    sq = _sigmoid(q)
    qf = q * sq
    sig = _sigmoid(fl)
    f = lb + omlb * sig
    k = 1.0 - f
    return sq, qf, sig, f, k, _hgrn_decays(f, w_ref)


def _level_factor(e, li, sl, row, qh, kh):
    el = e[2 + li][:, sl]
    up = (row & (LEVELS[li] // 2)) != 0
    return el, up, el * jnp.where(up, qh, kh)


def _proj_hgrn_fwd(x, nw1, ada, w_in_b, lower_bounds, gn_w, tables, placed, axes):
    T = x.shape[0]
    nc = T // CH
    nch = min(HGRN_CHUNKS_PER_STEP, nc)
    steps = nc // nch
    w_st, _, masks, _ = tables
    nw = len(placed)
    pass_step = (13 * steps) // 16
    q0 = 2 * DG

    def body(*refs):
        x_ref, nw_ref, sc_ref, sh_ref, win_ref, lbp_ref, gn_ref, w_ref, m_ref = refs[:9]
        h_ref, p_ref, y_ref, o_ref, a_ref, st_ref = refs[9 + nw:15 + nw]
        s_scr, send_sems, recv_sems = refs[15 + 2 * nw:]
        gather = _WeightGather(refs[15 + nw:15 + 2 * nw], axes, send_sems, recv_sems)
        step = pl.program_id(0)
        xv = x_ref[...]
        hb = (((xv * _rms(xv)) * nw_ref[...]) * (1.0 + sc_ref[...]) + sh_ref[...]).astype(BF16)
        h_ref[...] = hb
        p_ref[...] = _dot(hb, win_ref[...])

        @pl.when(step == 0)
        def _():
            gather.start()
            s_scr[...] = jnp.zeros_like(s_scr)

        @pl.when(step == pass_step)
        def _():
            gather.forward()

        lb, omlb = _lower_bound(lbp_ref)
        row = lax.broadcasted_iota(jnp.int32, (CH, 1), 0)
        eye = lax.broadcasted_iota(jnp.int32, (CH, CH), 0) == lax.broadcasted_iota(jnp.int32, (CH, CH), 1)
        in_level = [m_ref[li] > 0.0 for li in range(len(LEVELS))]
        pre = []
        for ci in range(nch):
            rs = slice(ci * CH, (ci + 1) * CH)
            _, qf, _, _, k, e = _hgrn_gates(p_ref[rs, q0:q0 + DH], p_ref[rs, q0 + DH:q0 + 2 * DH], lb, omlb, w_ref)
            mats = []
            for h in range(NH):
                sl = slice(h * HD, (h + 1) * HD)
                qh, kh = qf[:, sl], k[:, sl]
                a = jnp.where(eye, jnp.sum(qh * kh, axis=-1, keepdims=True), 0.0)
                for li in range(len(LEVELS)):
                    _, _, y = _level_factor(e, li, sl, row, qh, kh)
                    yb = y.astype(BF16)
                    a = jnp.where(in_level[li], _dot(yb, yb, NT), a)
                a_ref[ci, h] = a
                mats.append(a.astype(BF16))
            eb = e[0]
            pre.append(((qf * eb).astype(BF16), eb[CH - 1:CH, :], (k * e[1]).astype(BF16), mats))
        for ci in range(nch):
            rs = slice(ci * CH, (ci + 1) * CH)
            qe, ebl, kd, mats = pre[ci]
            v = p_ref[rs, q0 + 2 * DH:q0 + 3 * DH]
            g = p_ref[rs, q0 + 3 * DH:q0 + 4 * DH]
            for h in range(NH):
                sl = slice(h * HD, (h + 1) * HD)
                st0 = s_scr[h]
                st_ref[ci, h] = st0
                vb = v[:, sl].astype(BF16)
                o = _dot(qe[:, sl], st0.astype(BF16), NT) + _dot(mats[h], vb)
                s_scr[h] = st0 * ebl[:, sl] + _dot(vb, kd[:, sl], TN)
                o_ref[rs, sl] = o
                gh = g[:, sl]
                y_ref[rs, sl] = (((o * _rms(o)) * gn_ref[...]) * (gh * _sigmoid(gh))).astype(BF16)

        @pl.when(step == steps - 1)
        def _():
            gather.finish()

    rows = nch * CH
    row = lambda c: (c, 0)
    anyspec = pl.BlockSpec(memory_space=pl.ANY)
    res = pl.pallas_call(
        body, grid=(steps,),
        in_specs=[pl.BlockSpec((rows, D), row), _full((1, D)), _ada_part(ADA_SC1), _ada_part(ADA_SH1), _resident((D, DIN)),
                  _full((2, DH)), _full((1, HD)), _full(w_st.shape), _full(masks.shape)] + [anyspec] * nw,
        out_specs=[pl.BlockSpec((rows, D), row), pl.BlockSpec((rows, DIN), row),
                   pl.BlockSpec((rows, DH), lambda c: (c, 1)),
                   pl.BlockSpec((rows, DH), row),
                   pl.BlockSpec((nch, NH, CH, CH), lambda c: (c, 0, 0, 0)),
                   pl.BlockSpec((nch, NH, HD, HD), lambda c: (c, 0, 0, 0))] + [anyspec] * nw,
        out_shape=[SDS((T, D), BF16), SDS((T, DIN), F32), SDS((T, D), BF16), SDS((T, DH), F32),
                   SDS((nc, NH, CH, CH), F32), SDS((nc, NH, HD, HD), F32)] + [SDS(a.shape, a.dtype) for a in placed],
        scratch_shapes=[pltpu.VMEM((NH, HD, HD), F32)] + _gather_sems(nw),
        input_output_aliases={9 + i: 6 + i for i in range(nw)},
        compiler_params=_arb(), name="proj_hgrn_fwd")(x, nw1, ada, ada, w_in_b, lower_bounds, gn_w, w_st, masks, *placed)
    return res[:6], res[6:]


def _token_local(x, ycat, tgt, g1, nw2, sc2, sh2, g2, fw, w_out_b, w_fi_b, w_fo_b, proj, ws_b, bst, lnw, lnb, tm):
    T = x.shape[0]
    inv_d = 1.0 / D

    def body(x_ref, yb_ref, t_ref, g1_ref, nw2_ref, sc2_ref, sh2_ref, g2_ref, fw_ref, wo_ref, wfi_ref, wfo_ref,
             u_ref, v_ref, ws_ref, bst_ref, lnw_ref, lnb_ref,
             dy_ref, dx1_ref, h2_ref, act_ref, dff_ref, dgu_ref, dmix_ref, acc_ref, ya_ref):
        @pl.when(pl.program_id(0) == 0)
        def _():
            acc_ref[...] = jnp.zeros_like(acc_ref)

        def acc(row, val):
            acc_ref[row:row + 1, :] += jnp.sum(val, axis=0, keepdims=True)

        for bi in range(tm // BLK):
            rs = slice(bi * BLK, (bi + 1) * BLK)
            ug, _, _, _, _, _, mixed = _gmlp_common(u_ref[rs, :], v_ref[rs, :], lnw_ref[...], lnb_ref[...], ws_ref, bst_ref)
            ya_ref[rs, :] = (ug * mixed).astype(BF16)
        g1v, g2v = g1_ref[...], g2_ref[...]
        mix = _dot(ya_ref[...], wo_ref[0:DG, :]) + _dot(yb_ref[...], wo_ref[DG:D, :])
        x1 = x_ref[...] + g1v * mix
        r2 = _rms(x1)
        xh2 = x1 * r2
        n2 = xh2 * nw2_ref[...]
        osc2 = 1.0 + sc2_ref[...]
        h2b = (n2 * osc2 + sh2_ref[...]).astype(BF16)
        h2_ref[...] = h2b
        ff = jnp.zeros((tm, D), F32)
        saved = []
        for kb in range(DFF // FFB):
            gate = _dot(h2b, wfi_ref[:, kb * FFB:(kb + 1) * FFB])
            up = _dot(h2b, wfi_ref[:, DFF + kb * FFB:DFF + (kb + 1) * FFB])
            sg = _sigmoid(gate)
            actb = (gate * sg * up).astype(BF16)
            act_ref[:, kb * FFB:(kb + 1) * FFB] = actb
            ff = ff + _dot(actb, wfo_ref[kb * FFB:(kb + 1) * FFB, :])
            saved.append((gate, up, sg))
        x2 = x1 + g2v * ff
        r3 = _rms(x2)
        xh3 = x2 * r3
        err = xh3 * fw_ref[...] - t_ref[...]
        acc(6, (0.5 * inv_d) * err * err)
        dy = err * inv_d
        acc(4, dy * xh3)
        dx2 = _rms_bwd(xh3, r3, dy * fw_ref[...])
        acc(0, dx2 * ff)
        dffb = (dx2 * g2v).astype(BF16)
        dff_ref[...] = dffb
        dh2 = jnp.zeros((tm, D), F32)
        for kb in range(DFF // FFB):
            gate, up, sg = saved[kb]
            da = _dot(dffb, wfo_ref[kb * FFB:(kb + 1) * FFB, :], NT)
            dgate = (da * up * (sg * (1.0 + gate * (1.0 - sg)))).astype(BF16)
            dup = (da * gate * sg).astype(BF16)
            dgu_ref[:, kb * FFB:(kb + 1) * FFB] = dgate
            dgu_ref[:, DFF + kb * FFB:DFF + (kb + 1) * FFB] = dup
            dh2 = dh2 + _dot(dgate, wfi_ref[:, kb * FFB:(kb + 1) * FFB], NT)
            dh2 = dh2 + _dot(dup, wfi_ref[:, DFF + kb * FFB:DFF + (kb + 1) * FFB], NT)
        acc(2, dh2)
        acc(1, dh2 * n2)
        dn2 = dh2 * osc2
        acc(3, dn2 * xh2)
        dx1 = dx2 + _rms_bwd(xh2, r2, dn2 * nw2_ref[...])
        acc(5, dx1 * mix)
        dmixb = (dx1 * g1v).astype(BF16)
        dmix_ref[...] = dmixb
        dy_ref[...] = _dot(dmixb, wo_ref[...], NT)
        dx1_ref[...] = dx1

    row = lambda i: (i, 0)
    vec = _full((1, D))
    half = lambda j: pl.BlockSpec((tm, DG), lambda i: (i, j))
    return pl.pallas_call(
        body, grid=(T // tm,),
        in_specs=[pl.BlockSpec((tm, D), row), half(1), pl.BlockSpec((tm, D), row),
                  _ada_part(ADA_G1), vec, _ada_part(ADA_SC2), _ada_part(ADA_SH2), _ada_part(ADA_G2), vec,
                  _resident((D, D)), _resident((D, 2 * DFF)), _resident((DFF, D)),
                  half(0), half(1), _full((NH, BLK, BLK)), _full((BLK, NH)), _full((1, DG)), _full((1, DG))],
        out_specs=[pl.BlockSpec((tm, D), row), pl.BlockSpec((tm, D), row), pl.BlockSpec((tm, D), row),
                   pl.BlockSpec((tm, DFF), row), pl.BlockSpec((tm, D), row), pl.BlockSpec((tm, 2 * DFF), row),
                   pl.BlockSpec((tm, D), row), _full((8, D)), half(0)],
        out_shape=[SDS((T, D), F32), SDS((T, D), F32), SDS((T, D), BF16), SDS((T, DFF), BF16), SDS((T, D), BF16),
                   SDS((T, 2 * DFF), BF16), SDS((T, D), BF16), SDS((8, D), F32), SDS((T, D), BF16)],
        input_output_aliases={1: 8},
        compiler_params=_arb(), name="token_local")(x, ycat, tgt, g1, nw2, sc2, sh2, g2, fw, w_out_b, w_fi_b, w_fo_b,
                                                    proj, proj, ws_b, bst, lnw, lnb)


def _gmlp_bwd(proj, dycat, ws_b, bst, lnw, lnb, grads):
    T = proj.shape[0]
    rows = min(GMLP_ROWS_PER_STEP, T)
    nb = T // rows
    nw = len(grads)

    def body(*refs):
        u_ref, v_ref, dy_ref, ws_ref, bst_ref, lnw_ref, lnb_ref = refs[:7]
        dp_ref, dws_ref, dbs_ref, dln_ref = refs[7 + nw:11 + nw]
        dbs_acc, send_sems, recv_sems = refs[11 + 2 * nw:]
        exchange = _CoreExchange(refs[7:7 + nw], refs[11 + nw:11 + 2 * nw], send_sems, recv_sems)
        i = pl.program_id(0)

        @pl.when(i == 0)
        def _():
            exchange.start()
            dws_ref[...] = jnp.zeros_like(dws_ref)
            dln_ref[...] = jnp.zeros_like(dln_ref)
            dbs_acc[...] = jnp.zeros_like(dbs_acc)

        r = lax.broadcasted_iota(jnp.int32, (BLK, BLK), 0) // CH
        c = lax.broadcasted_iota(jnp.int32, (BLK, BLK), 1) // CH
        for bi in range(rows // BLK):
            rs = slice(bi * BLK, (bi + 1) * BLK)
            ug, dug, dvg, rstd, vhat, vnb, mixed = _gmlp_common(
                u_ref[rs, :], v_ref[rs, :], lnw_ref[...], lnb_ref[...], ws_ref, bst_ref)
            dya = dy_ref[rs, :]
            dp_ref[rs, 0:DG] = (dya * mixed * dug).astype(BF16)
            dmixed = dya * ug
            dbs_acc[...] += dmixed
            dmb = dmixed.astype(BF16)
            dvn = []
            for h in range(NH):
                sl = slice(h * HD, (h + 1) * HD)
                dws_ref[h * BLK:(h + 1) * BLK, :] += jnp.where(r >= c, _dot(dmb[:, sl], vnb[:, sl], NT), 0.0)
                dvn.append(_dot(ws_ref[h], dmb[:, sl], TN))
            dvn = jnp.concatenate(dvn, axis=1)
            dln_ref[0:1, :] += jnp.sum(dvn * vhat, axis=0, keepdims=True)
            dln_ref[1:2, :] += jnp.sum(dvn, axis=0, keepdims=True)
            dvh = dvn * lnw_ref[...]
            dvgel = rstd * (dvh - jnp.mean(dvh, axis=-1, keepdims=True) - vhat * jnp.mean(dvh * vhat, axis=-1, keepdims=True))
            dp_ref[rs, DG:2 * DG] = (dvgel * dvg).astype(BF16)

        @pl.when(i == nb - 1)
        def _():
            head = lax.broadcasted_iota(jnp.int32, (8, BLK), 0)
            ones = jnp.ones((8, HD), F32)
            out = jnp.zeros((8, BLK), F32)
            for h in range(NH):
                sums = _dot(ones, dbs_acc[:, h * HD:(h + 1) * HD], NT, precision=HIGHEST)
                out = out + jnp.where(head == h, sums, 0.0)
            dbs_ref[...] = out
            exchange.finish()

    anyspec = pl.BlockSpec(memory_space=pl.ANY)
    res = pl.pallas_call(
        body, grid=(nb,),
        in_specs=[pl.BlockSpec((rows, DG), lambda i: (i, 0)), pl.BlockSpec((rows, DG), lambda i: (i, 1)),
                  pl.BlockSpec((rows, DG), lambda i: (i, 0)),
                  _full((NH, BLK, BLK)), _full((BLK, NH)), _full((1, DG)), _full((1, DG))] + [anyspec] * nw,
        out_specs=[pl.BlockSpec((rows, 2 * DG), lambda i: (i, 2)), _full((NH * BLK, BLK)), _full((8, BLK)), _full((8, DG))]
        + [anyspec] * nw,
        out_shape=[SDS((T, DIN), BF16), SDS((NH * BLK, BLK), F32), SDS((8, BLK), F32), SDS((8, DG), F32)]
        + _core_exchange_shapes(grads),
        scratch_shapes=[pltpu.VMEM((BLK, DG), F32)] + _core_exchange_sems(nw),
        compiler_params=_arb(), name="gmlp_bwd")(proj, proj, dycat, ws_b, bst, lnw, lnb, *grads)
    return res[:4], res[4:]


def _hgrn_bwd(proj, o_pre, a_all, st_all, dycat, lower_bounds, gn_w, dproj, tables, sums):
    T = proj.shape[0]
    nc = T // CH
    nch = min(HGRN_CHUNKS_PER_STEP, nc)
    steps = nc // nch
    w_st, w_st_t, _, masks_sym = tables
    n_lev = len(LEVELS)
    nw = len(sums)

    def body(*refs):
        q_ref, f_ref, i_ref, g_ref, o_ref, a_ref, st_ref, dy_ref, lbp_ref, gn_ref, w_ref, wt_ref, ms_ref = refs[:13]
        dp_ref, dlb_ref, dgn_ref = refs[14 + nw:17 + nw]
        ds_scr, dx_scr, send_sems, recv_sems = refs[17 + 2 * nw:]
        exchange = _ChipExchange(refs[14:14 + nw], refs[17 + nw:17 + 2 * nw], send_sems, recv_sems)
        i = pl.program_id(0)

        @pl.when(i == 0)
        def _():
            exchange.start()
            ds_scr[...] = jnp.zeros_like(ds_scr)
            dlb_ref[...] = jnp.zeros_like(dlb_ref)
            dgn_ref[...] = jnp.zeros_like(dgn_ref)

        lb, omlb = _lower_bound(lbp_ref)
        row = lax.broadcasted_iota(jnp.int32, (CH, 1), 0)
        eye = lax.broadcasted_iota(jnp.int32, (CH, CH), 0) == lax.broadcasted_iota(jnp.int32, (CH, CH), 1)
        lower = lax.broadcasted_iota(jnp.int32, (CH, CH), 0) > lax.broadcasted_iota(jnp.int32, (CH, CH), 1)
        dgn = jnp.zeros((1, HD), F32)
        pre = []
        for ci in range(nch):
            rs = slice(ci * CH, (ci + 1) * CH)
            q = q_ref[rs, :]
            v = i_ref[rs, :]
            g = g_ref[rs, :]
            sq, qf, sig, f, k, e = _hgrn_gates(q, f_ref[rs, :], lb, omlb, w_ref)
            eb = e[0]
            ekd = e[1]
            kd = k * ekd
            qe = qf * eb
            dob_h, dqe_h, dqf_h, dki_h, dv_h, dg_h = [], [], [], [], [], []
            for h in range(NH):
                sl = slice(h * HD, (h + 1) * HD)
                o = o_ref[rs, sl]
                ro = _rms(o)
                oh = o * ro
                gh = g[:, sl]
                sg = _sigmoid(gh)
                dyb = dy_ref[rs, sl]
                dg_h.append(dyb * (oh * gn_ref[...]) * (sg * (1.0 + gh * (1.0 - sg))))
                don = dyb * (gh * sg)
                dgn = dgn + jnp.sum(don * oh, axis=0, keepdims=True)
                dob = _rms_bwd(oh, ro, don * gn_ref[...]).astype(BF16)
                vb = v[:, sl].astype(BF16)
                qh, kh = qf[:, sl], k[:, sl]
                dqe = _dot(dob, st_ref[ci, h].astype(BF16))
                da = _dot(dob, vb, NT)
                ddiag = jnp.sum(jnp.where(eye, da, 0.0), axis=-1, keepdims=True)
                dsym = jnp.where(lower, da, _dot(vb, dob, NT))
                upper_part = jnp.zeros((CH, HD), F32)
                both = jnp.zeros((CH, HD), F32)
                for li in range(n_lev):
                    el, up, y = _level_factor(e, li, sl, row, qh, kh)
                    dyv = _dot((ms_ref[li] * dsym).astype(BF16), y.astype(BF16))
                    dx_scr[ci, (2 + li) * CH:(3 + li) * CH, sl] = dyv * y
                    dye = dyv * el
                    upper_part = upper_part + jnp.where(up, dye, 0.0)
                    both = both + dye
                dob_h.append(dob)
                dqe_h.append(dqe)
                dqf_h.append(dqe * eb[:, sl] + ddiag * kh + upper_part)
                dki_h.append(ddiag * qh + (both - upper_part))
                dv_h.append(_dot(a_ref[ci, h].astype(BF16), dob, TN))
            dp_ref[rs, 0:DH] = (jnp.concatenate(dqf_h, axis=1) * (sq * (1.0 + q * (1.0 - sq)))).astype(BF16)
            dp_ref[rs, 3 * DH:4 * DH] = jnp.concatenate(dg_h, axis=1).astype(BF16)
            pre.append((v, sig, f, eb, ekd, kd, qe, dob_h, jnp.concatenate(dqe_h, axis=1), dki_h, dv_h))
        dgn_ref[0:1, :] += dgn
        for ci in reversed(range(nch)):
            rs = slice(ci * CH, (ci + 1) * CH)
            v, sig, f, eb, ekd, kd, qe, dob_h, dqe, dki_h, dv_h = pre[ci]
            ebl = eb[CH - 1:CH, :]
            dbl_h, dkd_h, dv2_h = [], [], []
            for h in range(NH):
                sl = slice(h * HD, (h + 1) * HD)
                dst1 = ds_scr[h]
                dst1b = dst1.astype(BF16)
                ds_scr[h] = dst1 * ebl[:, sl] + _dot(dob_h[h], qe[:, sl].astype(BF16), TN)
                dbl_h.append(ebl[:, sl] * jnp.sum(st_ref[ci, h] * dst1, axis=0, keepdims=True))
                dkd_h.append(_dot(v[:, sl].astype(BF16), dst1b))
                dv2_h.append(dv_h[h] + _dot(kd[:, sl].astype(BF16), dst1b, NT))
            dkd = jnp.concatenate(dkd_h, axis=1)
            dx_scr[ci, 0:CH, :] = dqe * qe + jnp.where(row == CH - 1, jnp.concatenate(dbl_h, axis=1), 0.0)
            dx_scr[ci, CH:2 * CH, :] = dkd * kd
            dlf = _split_dot(wt_ref[...], dx_scr[ci], 2)
            df = dlf / f - (dkd * ekd + jnp.concatenate(dki_h, axis=1))
            dlb_ref[0:1, :] += jnp.sum(df * (1.0 - sig), axis=0, keepdims=True)
            dp_ref[rs, DH:2 * DH] = (df * omlb * sig * (1.0 - sig)).astype(BF16)
            dp_ref[rs, 2 * DH:3 * DH] = jnp.concatenate(dv2_h, axis=1).astype(BF16)

        @pl.when(i == steps - 1)
        def _():
            gl = dlb_ref[0:1, :] * lb * omlb
            dlb_ref[0:1, :] = gl
            dlb_ref[1:2, :] = -gl
            exchange.finish()

    rev = lambda j: pl.BlockSpec((nch * CH, DH), lambda c: (steps - 1 - c, j))
    anyspec = pl.BlockSpec(memory_space=pl.ANY)
    res = pl.pallas_call(
        body, grid=(steps,),
        in_specs=[rev(2), rev(3), rev(4), rev(5), rev(0),
                  pl.BlockSpec((nch, NH, CH, CH), lambda c: (steps - 1 - c, 0, 0, 0)),
                  pl.BlockSpec((nch, NH, HD, HD), lambda c: (steps - 1 - c, 0, 0, 0)),
                  rev(1), _full((2, DH)), _full((1, HD)),
                  _full(w_st.shape), _full(w_st_t.shape), _full(masks_sym.shape),
                  anyspec] + [anyspec] * nw,
        out_specs=[pl.BlockSpec((nch * CH, 4 * DH), lambda c: (steps - 1 - c, 0)), _full((8, DH)), _full((8, HD))]
        + [anyspec] * nw,
        out_shape=[SDS((T, DIN), BF16), SDS((8, DH), F32), SDS((8, HD), F32)] + _slot_shapes(sums),
        scratch_shapes=[pltpu.VMEM((NH, HD, HD), F32), pltpu.VMEM((nch, (2 + n_lev) * CH, DH), F32)] + _exchange_sems(nw),
        input_output_aliases={13: 0},
        compiler_params=_arb(), name="hgrn_bwd")(proj, proj, proj, proj, o_pre, a_all, st_all, dycat, lower_bounds, gn_w,
                                                 w_st, w_st_t, masks_sym, dproj, *sums)
    return res[:3], res[3:]


def _proj_in_bwd(dproj, x, dx1, nw, sc, w_in_b, tm, sums):
    T = x.shape[0]
    ns = len(sums)
    steps = T // tm

    def body(*refs):
        dp_ref, x_ref, dx1_ref, nw_ref, sc_ref, w_ref = refs[:6]
        gx_ref, acc_ref = refs[6 + ns:8 + ns]
        exchange = _ChipExchange(refs[6:6 + ns], refs[8 + ns:8 + 2 * ns], *refs[8 + 2 * ns:])

        @pl.when(pl.program_id(0) == 0)
        def _():
            exchange.start()
            acc_ref[...] = jnp.zeros_like(acc_ref)

        dh = _dot(dp_ref[:, 0:4 * DH], w_ref[:, 2 * DG:DIN], NT) + _dot(dp_ref[:, 4 * DH:DIN], w_ref[:, 0:2 * DG], NT)
        xv = x_ref[...]
        r = _rms(xv)
        xh = xv * r
        n1 = xh * nw_ref[...]
        acc_ref[0:1, :] += jnp.sum(dh, axis=0, keepdims=True)
        acc_ref[1:2, :] += jnp.sum(dh * n1, axis=0, keepdims=True)
        dn = dh * (1.0 + sc_ref[...])
        acc_ref[2:3, :] += jnp.sum(dn * xh, axis=0, keepdims=True)
        gx_ref[...] = dx1_ref[...] + _rms_bwd(xh, r, dn * nw_ref[...])

        @pl.when(pl.program_id(0) == steps - 1)
        def _():
            exchange.finish()

    row = lambda i: (i, 0)
    anyspec = pl.BlockSpec(memory_space=pl.ANY)
    res = pl.pallas_call(
        body, grid=(steps,),
        in_specs=[pl.BlockSpec((tm, DIN), row), pl.BlockSpec((tm, D), row), pl.BlockSpec((tm, D), row),
                  _full((1, D)), _ada_part(ADA_SC1), _resident((D, DIN))] + [anyspec] * ns,
        out_specs=[pl.BlockSpec((tm, D), row), _full((8, D))] + [anyspec] * ns,
        out_shape=[SDS((T, D), F32), SDS((8, D), F32)] + _slot_shapes(sums),
        scratch_shapes=_exchange_sems(ns),
        compiler_params=_arb(), name="proj_in_bwd")(dproj, x, dx1, nw, sc, w_in_b, *sums)
    return res[:2], res[2:]


def _wgrad(a, b, bk, bn, tt, name, bf16_copy=False):
    T, K = a.shape
    N = b.shape[1]
    nn, nk, nt = N // bn, K // bk, T // tt
    bmap = lambda n, k, t: (t, n)

    def body(a_ref, b_ref, o_ref, *copy_ref):
        @pl.when(pl.program_id(2) == 0)
        def _():
            o_ref[...] = jnp.zeros_like(o_ref)

        o_ref[0] += _dot(a_ref[...], b_ref[...], TN)

        if bf16_copy:
            @pl.when(pl.program_id(2) == nt - 1)
            def _():
                copy_ref[0][...] = o_ref[...].astype(BF16)

    ospec = pl.BlockSpec((1, bk, bn), lambda n, k, t: (n, k, 0))
    return pl.pallas_call(
        body, grid=(nn, nk, nt),
        in_specs=[pl.BlockSpec((tt, bk), lambda n, k, t: (t, k)), pl.BlockSpec((tt, bn), bmap)],
        out_specs=[ospec, ospec] if bf16_copy else ospec,
        out_shape=[SDS((nn, K, bn), F32), SDS((nn, K, bn), BF16)] if bf16_copy else SDS((nn, K, bn), F32),
        compiler_params=_arb(3), name=name)(a, b)


def _adam_math(w, g, m, v):
    m = B1 * m + (1.0 - B1) * g
    v = B2 * v + (1.0 - B2) * (g * g)
    m_hat = m / (1.0 - B1 ** STEP)
    v_hat = v / (1.0 - B2 ** STEP)
    return -LR * (m_hat / (jnp.sqrt(v_hat) + AEPS) + WD * w), m, v


def _adamw_halves(w, mine, sibling, m, v, c_idx, rb, name):
    R, C = w.shape
    nb = (R // 2) // rb

    def body(c_ref, w_ref, a_ref, b_ref, m_ref, v_ref, g_out, d_out, m_out, v_out):
        g = jnp.where(pl.program_id(0) == c_ref[0], a_ref[...], b_ref[...])
        g_out[...] = g
        d_out[...], m_out[...], v_out[...] = _adam_math(w_ref[...], g, m_ref[...], v_ref[...])

    whole = pl.BlockSpec((rb, C), lambda hh, i, cr: (hh * nb + i, 0))
    half = pl.BlockSpec((rb, C), lambda hh, i, cr: (i, 0))
    return pl.pallas_call(
        body,
        grid_spec=pltpu.PrefetchScalarGridSpec(
            num_scalar_prefetch=1, grid=(2, nb), in_specs=[whole, half, half, whole, whole], out_specs=[whole] * 4),
        out_shape=[SDS((R, C), F32)] * 4, compiler_params=_arb(2), name=name)(c_idx, w, mine, sibling, m, v)


def _ada_forward(c_all, w_ada):
    n = w_ada.shape[1]

    def body(c_ref, w_ref, ca_ref, p_ref):
        cv = c_ref[...]
        ca = cv * _sigmoid(cv)
        ca_ref[...] = ca
        p_ref[...] = _dot(ca, w_ref[...], precision=HIGHEST)

    return pl.pallas_call(
        body, grid=(n // 512,),
        in_specs=[_full((N_DEV, D)), pl.BlockSpec((D, 512), lambda i: (0, i))],
        out_specs=[_full((N_DEV, D)), pl.BlockSpec((N_DEV, 512), lambda i: (0, i))],
        out_shape=[SDS((N_DEV, D), F32), SDS((N_DEV, n), F32)],
        compiler_params=_arb(), name="ada_forward")(c_all, w_ada)


def _ada_wgrad_adam(cact_t, dada_all, w, m, v, chip_idx):
    R, C = w.shape
    rb = 256

    def body(j_ref, c_ref, d_ref, w_ref, m_ref, v_ref, g_out, d_out, m_out, v_out):
        g = _dot(c_ref[...], d_ref[...], precision=HIGHEST)
        g_out[...] = g
        d_out[...], m_out[...], v_out[...] = _adam_math(w_ref[...], g, m_ref[...], v_ref[...])

    spec = pl.BlockSpec((rb, C), lambda i, j: (i, 0))
    return pl.pallas_call(
        body,
        grid_spec=pltpu.PrefetchScalarGridSpec(
            num_scalar_prefetch=1, grid=(R // rb,),
            in_specs=[pl.BlockSpec((rb, N_DEV), lambda i, j: (i, 0)), pl.BlockSpec((N_DEV, C), lambda i, j: (0, j[0])),
                      spec, spec, spec],
            out_specs=[spec] * 4),
        out_shape=[SDS((R, C), F32)] * 4,
        compiler_params=_arb(), name="ada_wgrad_adam")(chip_idx, cact_t, dada_all, w, m, v)


SMALL_NAMES = ('b_ada', 'norm1_w', 'norm2_w', 'final_norm_w', 'v_ln_w', 'v_ln_b', 'lower_bounds', 'gn_w', 'b_s', 'w_s')


def _small_finalize(gathered, params, moms, vels):
    n_in = len(gathered)

    def body(*refs):
        acc1, acc2, dln, dlb, dgn, dbs, dws = refs[:n_in]
        prm = [dict(zip(SMALL_NAMES, refs[n_in + k * 10:n_in + (k + 1) * 10])) for k in range(3)]
        outs = [dict(zip(SMALL_NAMES, refs[n_in + 30 + k * 10:n_in + 30 + (k + 1) * 10])) for k in range(4)]
        loss_ref, dada_ref = refs[n_in + 70:n_in + 72]

        def dev_sum(ref, first, n):
            per = ref.shape[0] // N_DEV
            g = ref[first:first + n, :]
            for dev in range(1, N_DEV):
                g = g + ref[dev * per + first:dev * per + first + n, :]
            return g

        def update(n, g, cols=slice(None)):
            outs[0][n][:, cols] = g
            outs[1][n][:, cols], outs[2][n][:, cols], outs[3][n][:, cols] = _adam_math(
                prm[0][n][:, cols], g, prm[1][n][:, cols], prm[2][n][:, cols])

        ada_rows = ((acc1, 0), (acc1, 1), (acc2, 5), (acc2, 2), (acc2, 1), (acc2, 0))
        for k, (ref, r) in enumerate(ada_rows):
            update('b_ada', dev_sum(ref, r, 1), slice(k * D, (k + 1) * D))
            for dev in range(N_DEV):
                dada_ref[dev:dev + 1, k * D:(k + 1) * D] = ref[8 * dev + r:8 * dev + r + 1, :]
        update('norm1_w', dev_sum(acc1, 2, 1))
        update('norm2_w', dev_sum(acc2, 3, 1))
        update('final_norm_w', dev_sum(acc2, 4, 1))
        update('v_ln_w', dev_sum(dln, 0, 1))
        update('v_ln_b', dev_sum(dln, 1, 1))
        update('lower_bounds', dev_sum(dlb, 0, 2))
        update('gn_w', dev_sum(dgn, 0, 1))
        update('b_s', dev_sum(dbs, 0, NH))
        update('w_s', dev_sum(dws, 0, NH * BLK))
        loss_ref[...] = jnp.sum(dev_sum(acc2, 6, 1), axis=-1, keepdims=True)

    shapes = [SDS(params[n].shape, F32) for n in SMALL_NAMES]
    res = pl.pallas_call(
        body, out_shape=shapes * 4 + [SDS((1, 1), F32), SDS((N_DEV, 6 * D), F32)], name="small_finalize")(
            *gathered, *[d[n] for d in (params, moms, vels) for n in SMALL_NAMES])
    return [dict(zip(SMALL_NAMES, res[k * 10:(k + 1) * 10])) for k in range(4)], res[40], res[41]


def _position():
    x, y, c = lax.axis_index("x"), lax.axis_index("y"), lax.axis_index("c")
    return x, y, c


def _chip_at(x, y, r):
    return (x ^ (r >> 1), y ^ (r & 1))


def _gather_rows(ins, outs, send_sems, recv_sems, local_sems, after_issue=None):
    nb = len(ins)
    x, y, c = _position()
    me, sibling = (x, y, c), (x, y, 1 - c)
    chips = [_chip_at(x, y, r) for r in (1, 2, 3)]

    def rows(b, px, py, pc):
        m_per = ins[b].shape[0]
        return outs[b].at[pl.ds((4 * px + 2 * py + pc) * m_per, m_per), :]

    def copy(b, k, blk, to, src=None):
        return pltpu.make_async_remote_copy(
            src_ref=rows(b, *blk) if src is None else src, dst_ref=rows(b, *blk),
            send_sem=send_sems.at[7 * b + k], recv_sem=recv_sems.at[7 * b + k], device_id=to, device_id_type=MESH)

    local, sent = [], []
    for b in range(nb):
        mine = pltpu.make_async_copy(ins[b], rows(b, *me), local_sems.at[b])
        mine.start()
        local.append(mine)
        first = [copy(b, 0, me, sibling, src=ins[b])]
        first += [copy(b, 1 + j, me, (*chip, c), src=ins[b]) for j, chip in enumerate(chips)]
        for cp in first:
            cp.start()
        sent += first
    if after_issue is not None:
        after_issue()
    for b in range(nb):
        for j, chip in enumerate(chips):
            copy(b, 1 + j, (*chip, c), me).wait_recv()
            passed = copy(b, 4 + j, (*chip, c), sibling)
            passed.start()
            sent.append(passed)
    for b in range(nb):
        copy(b, 0, sibling, me).wait_recv()
        for j, chip in enumerate(chips):
            copy(b, 4 + j, (*chip, 1 - c), me).wait_recv()
    for cp in sent:
        cp.wait_send()
    for cp in local:
        cp.wait()


def _gather_rows_shapes(blocks):
    return [SDS((N_DEV * b.shape[0], b.shape[1]), b.dtype) for b in blocks]


def _gather_rows_sems(nb):
    return [pltpu.SemaphoreType.DMA((7 * nb,)), pltpu.SemaphoreType.DMA((7 * nb,)), pltpu.SemaphoreType.DMA((nb,))]


def _all_gather_rows(blocks, name):
    nb = len(blocks)

    def body(*refs):
        _gather_rows(refs[:nb], refs[nb:2 * nb], *refs[2 * nb:])

    vmem = pl.BlockSpec(memory_space=pltpu.VMEM)
    return pl.pallas_call(
        body, out_shape=_gather_rows_shapes(blocks), in_specs=[vmem] * nb, out_specs=[vmem] * nb,
        scratch_shapes=_gather_rows_sems(nb), name=name)(*blocks)


def _place_shard(w_shard, axis, chip_idx, name):
    R, C = w_shard.shape
    rb = _row_block(R)
    nb = R // rb
    full = (R * N_CHIPS, C) if axis == 0 else (R, C * N_CHIPS)
    omap = (lambda i, j: (j[0] * nb + i, 0)) if axis == 0 else (lambda i, j: (i, j[0]))

    def body(j_ref, w_ref, o_ref):
        o_ref[...] = w_ref[...].astype(BF16)

    return pl.pallas_call(
        body,
        grid_spec=pltpu.PrefetchScalarGridSpec(
            num_scalar_prefetch=1, grid=(nb,), in_specs=[pl.BlockSpec((rb, C), lambda i, j: (i, 0))],
            out_specs=pl.BlockSpec((rb, C), omap)),
        out_shape=SDS(full, BF16), compiler_params=_arb(), name=name)(chip_idx, w_shard)


class _WeightGather:
    def __init__(self, refs, axes, send_sems, recv_sems):
        self.refs, self.axes, self.send_sems, self.recv_sems = refs, axes, send_sems, recv_sems
        self.x, self.y, self.c = _position()
        self.j = 2 * self.x + self.y
        self.n = 3 * len(refs)

    def _half(self, w, chip_idx, half):
        ref, axis = self.refs[w], self.axes[w]
        if axis == 0:
            size = ref.shape[0] // N_CHIPS
            return ref.at[pl.ds(chip_idx * size + half * (size // 2), size // 2), :]
        size = ref.shape[1] // N_CHIPS
        rows = ref.shape[0] // 2
        return ref.at[pl.ds(half * rows, rows), pl.ds(chip_idx * size, size)]

    def _ici(self, w, r, chip_idx):
        k = 3 * w + r - 1
        piece = self._half(w, chip_idx, self.c)
        return pltpu.make_async_remote_copy(
            src_ref=piece, dst_ref=piece, send_sem=self.send_sems.at[k], recv_sem=self.recv_sems.at[k],
            device_id=(*_chip_at(self.x, self.y, r), self.c), device_id_type=MESH)

    def _d2d(self, w, r, half):
        k = self.n + 3 * w + r - 1
        piece = self._half(w, self.j ^ r, half)
        return pltpu.make_async_remote_copy(
            src_ref=piece, dst_ref=piece, send_sem=self.send_sems.at[k], recv_sem=self.recv_sems.at[k],
            device_id=(self.x, self.y, 1 - self.c), device_id_type=MESH)

    def _each(self):
        return [(w, r) for w in range(len(self.refs)) for r in (1, 2, 3)]

    def start(self):
        for w, r in self._each():
            self._ici(w, r, self.j).start()

    def forward(self):
        for w, r in self._each():
            self._ici(w, r, self.j ^ r).wait_recv()
            self._d2d(w, r, self.c).start()

    def finish(self):
        for w, r in self._each():
            self._ici(w, r, self.j).wait_send()
            self._d2d(w, r, self.c).wait_send()
            self._d2d(w, r, 1 - self.c).wait_recv()


def _gather_sems(n_weights):
    return [pltpu.SemaphoreType.DMA((6 * n_weights,)), pltpu.SemaphoreType.DMA((6 * n_weights,))]


def _gather_weights(placed, axes, row_blocks, name):
    nw, nb = len(placed), len(row_blocks)

    def body(*refs):
        w_outs, b_ins, b_outs = refs[nw + nb:2 * nw + nb], refs[nw:nw + nb], refs[2 * nw + nb:2 * (nw + nb)]
        sems = refs[2 * (nw + nb):]
        g = _WeightGather(w_outs, axes, *sems[:2])
        _gather_rows(b_ins, b_outs, *sems[2:], after_issue=g.start)
        g.forward()
        g.finish()

    anyspec = pl.BlockSpec(memory_space=pl.ANY)
    vmem = pl.BlockSpec(memory_space=pltpu.VMEM)
    res = pl.pallas_call(
        body, out_shape=[SDS(a.shape, a.dtype) for a in placed] + _gather_rows_shapes(row_blocks),
        in_specs=[anyspec] * nw + [vmem] * nb, out_specs=[anyspec] * nw + [vmem] * nb,
        scratch_shapes=_gather_sems(nw) + _gather_rows_sems(nb), input_output_aliases={i: i for i in range(nw)},
        name=name)(*placed, *row_blocks)
    return res[:nw], res[nw:]


class _ChipExchange:
    def __init__(self, ins, outs, send_sems, recv_sems):
        self.ins, self.outs, self.send_sems, self.recv_sems = ins, outs, send_sems, recv_sems
        self.x, self.y, self.c = _position()
        self.j = 2 * self.x + self.y

    def _copies(self):
        for w in range(len(self.ins)):
            for r in (1, 2, 3):
                k = 3 * w + r - 1
                yield pltpu.make_async_remote_copy(
                    src_ref=self.ins[w].at[self.j ^ r], dst_ref=self.outs[w].at[r - 1],
                    send_sem=self.send_sems.at[k], recv_sem=self.recv_sems.at[k],
                    device_id=(*_chip_at(self.x, self.y, r), self.c), device_id_type=MESH)

    def start(self):
        for cp in self._copies():
            cp.start()

    def finish(self):
        for cp in self._copies():
            cp.wait()


def _exchange_sems(n_weights):
    return [pltpu.SemaphoreType.DMA((3 * n_weights,)), pltpu.SemaphoreType.DMA((3 * n_weights,))]


class _CoreExchange:
    def __init__(self, ins, outs, send_sems, recv_sems):
        self.ins, self.outs, self.send_sems, self.recv_sems = ins, outs, send_sems, recv_sems
        self.x, self.y, self.c = _position()

    def _copies(self):
        for w in range(len(self.ins)):
            yield pltpu.make_async_remote_copy(
                src_ref=self.ins[w].at[:, 1 - self.c], dst_ref=self.outs[w],
                send_sem=self.send_sems.at[w], recv_sem=self.recv_sems.at[w],
                device_id=(self.x, self.y, 1 - self.c), device_id_type=MESH)

    def start(self):
        for cp in self._copies():
            cp.start()

    def finish(self):
        for cp in self._copies():
            cp.wait()


def _core_exchange_shapes(grads):
    return [SDS((g.shape[0], g.shape[2], g.shape[3]), g.dtype) for g in grads]


def _core_exchange_sems(n):
    return [pltpu.SemaphoreType.DMA((n,)), pltpu.SemaphoreType.DMA((n,))]


def _exchange_core_halves(grads, name):
    nw = len(grads)

    def body(*refs):
        ex = _CoreExchange(refs[:nw], refs[nw:2 * nw], *refs[2 * nw:])
        ex.start()
        ex.finish()

    anyspec = pl.BlockSpec(memory_space=pl.ANY)
    return pl.pallas_call(
        body, out_shape=_core_exchange_shapes(grads), in_specs=[anyspec] * nw, out_specs=[anyspec] * nw,
        scratch_shapes=_core_exchange_sems(nw), name=name)(*grads)


def _add_core_halves(g4, recv, c_idx, rb, name):
    ns, _, rh, C = g4.shape

    def body(c_ref, g_ref, r_ref, o_ref):
        o_ref[...] = (g_ref[0] + r_ref[...]).astype(BF16)

    return pl.pallas_call(
        body,
        grid_spec=pltpu.PrefetchScalarGridSpec(
            num_scalar_prefetch=1, grid=(ns, rh // rb),
            in_specs=[pl.BlockSpec((1, 1, rb, C), lambda s, i, cr: (s, cr[0], i, 0)),
                      pl.BlockSpec((1, rb, C), lambda s, i, cr: (s, i, 0))],
            out_specs=pl.BlockSpec((1, rb, C), lambda s, i, cr: (s, i, 0))),
        out_shape=SDS((ns, rh, C), BF16), compiler_params=_arb(2), name=name)(c_idx, g4, recv)


def _add_core_halves_in(g4, recv, c_idx, name):
    n_slabs, _, rh, C = g4.shape
    cb = 256
    per_slab, per_chip, n_blocks = C // cb, DIN // N_CHIPS // cb, DIN // cb

    def stored(s, k):
        sb = (per_chip * s + k + 4 * DH // cb) % n_blocks
        return sb // per_slab, sb % per_slab

    def body(c_ref, g_ref, r_ref, o_ref):
        o_ref[...] = (g_ref[0] + r_ref[...].astype(F32)).astype(BF16)

    return pl.pallas_call(
        body,
        grid_spec=pltpu.PrefetchScalarGridSpec(
            num_scalar_prefetch=1, grid=(N_CHIPS, per_chip),
            in_specs=[pl.BlockSpec((1, 1, rh, cb), lambda s, k, cr: (stored(s, k)[0], cr[0], 0, stored(s, k)[1])),
                      pl.BlockSpec((1, rh, cb), lambda s, k, cr: (stored(s, k)[0], 0, stored(s, k)[1]))],
            out_specs=pl.BlockSpec((1, rh, cb), lambda s, k, cr: (s, 0, k))),
        out_shape=SDS((N_CHIPS, rh, DIN // N_CHIPS), BF16), compiler_params=_arb(2), name=name)(c_idx, g4, recv)


def _slot_shapes(sums):
    return [SDS((3,) + s.shape[1:], s.dtype) for s in sums]


def _add_chips(own, slots, order, rb, name):
    _, rh, C = slots.shape

    def body(o_ref, own_ref, a_ref, b_ref, c_ref, d_ref, out_ref):
        mine = own_ref[0].astype(F32)
        t = [jnp.where(o_ref[i] == 0, mine, r[0].astype(F32)) for i, r in enumerate((a_ref, b_ref, c_ref, d_ref))]
        out_ref[...] = ((t[0] + t[1]) + t[2]) + t[3]

    def spec(i):
        return pl.BlockSpec((1, rb, C), lambda t, o: (jnp.maximum(o[i], 1) - 1, t, 0))

    return pl.pallas_call(
        body,
        grid_spec=pltpu.PrefetchScalarGridSpec(
            num_scalar_prefetch=1, grid=(rh // rb,),
            in_specs=[pl.BlockSpec((1, rb, C), lambda t, o: (o[4], t, 0)), spec(0), spec(1), spec(2), spec(3)],
            out_specs=pl.BlockSpec((rb, C), lambda t, o: (t, 0))),
        out_shape=SDS((rh, C), F32), compiler_params=_arb(), name=name)(order, own, slots, slots, slots, slots)


def _share_halves(halves):
    nw = len(halves)

    def body(*refs):
        ins, outs = refs[:nw], refs[nw:2 * nw]
        send_sems, recv_sems = refs[2 * nw:]
        x, y, c = _position()
        started = []
        for w in range(nw):
            cp = pltpu.make_async_remote_copy(
                src_ref=ins[w], dst_ref=outs[w], send_sem=send_sems.at[w], recv_sem=recv_sems.at[w],
                device_id=(x, y, 1 - c), device_id_type=MESH)
            cp.start()
            started.append(cp)
        for cp in started:
            cp.wait()

    anyspec = pl.BlockSpec(memory_space=pl.ANY)
    return pl.pallas_call(
        body, out_shape=[SDS(h.shape, F32) for h in halves], in_specs=[anyspec] * nw, out_specs=[anyspec] * nw,
        scratch_shapes=[pltpu.SemaphoreType.DMA((nw,)), pltpu.SemaphoreType.DMA((nw,))],
        name="share_halves")(*halves)


def _small_2d(b_ada, norm1_w, norm2_w, final_norm_w, v_ln_w, v_ln_b, lower_bounds, gn_w, b_s, w_s):
    return dict(zip(SMALL_NAMES, (b_ada, norm1_w, norm2_w, final_norm_w.reshape(1, D), v_ln_w, v_ln_b, lower_bounds, gn_w,
                                  b_s.reshape(NH, BLK), w_s.reshape(NH * BLK, BLK))))


def _small_original_shapes(d):
    out = dict(d)
    out['final_norm_w'] = d['final_norm_w'].reshape(D)
    out['b_s'] = d['b_s'].reshape(1, NH, BLK)
    out['w_s'] = d['w_s'].reshape(1, NH, BLK, BLK)
    return out


def _row_block(r):
    for cand in (256, 176, 128, 64, 32, 16, 8):
        if r % cand == 0:
            return cand
    return r


def kernel(x, c, w_ada, b_ada, norm1_w, w_in, w_s, b_s, v_ln_w, v_ln_b, lower_bounds, gn_w, w_out, norm2_w, w_ffn_in, w_ffn_out, final_norm_w, loss_target, m_w_ada, m_b_ada, m_norm1_w, m_w_in, m_w_s, m_b_s, m_v_ln_w, m_v_ln_b, m_lower_bounds, m_gn_w, m_w_out, m_norm2_w, m_w_ffn_in, m_w_ffn_out, m_final_norm_w, v_w_ada, v_b_ada, v_norm1_w, v_w_in, v_w_s, v_b_s, v_v_ln_w, v_v_ln_b, v_lower_bounds, v_gn_w, v_w_out, v_norm2_w, v_w_ffn_in, v_w_ffn_out, v_final_norm_w):
    T = x.shape[1]
    tm, tp = min(TOKEN_TILE, T), min(PROJ_TILE, T)
    px, py, pc = _position()
    chip = 2 * px + py
    me = 4 * px + 2 * py + pc
    x2d = x.reshape(T, D)
    tgt = loss_target.reshape(T, D)

    chip_idx = jnp.reshape(chip, (1,)).astype(jnp.int32)
    c_idx = jnp.reshape(pc, (1,)).astype(jnp.int32)
    (w_in_b,), (c_all,) = _gather_weights(
        [_place_shard(w_in[0], 1, chip_idx, "place_in")], [1], [jnp.broadcast_to(c, (8, D))], "gather_w_in_and_c")
    placed = [_place_shard(w_out[0], 0, chip_idx, "place_out"), _place_shard(w_ffn_in[0], 1, chip_idx, "place_ffn_in"),
              _place_shard(w_ffn_out[0], 0, chip_idx, "place_ffn_out")]

    cact, ada_part = _ada_forward(c_all.reshape(N_DEV, 8, D)[:, 0, :], w_ada[0])
    n_ada = ada_part.shape[1]
    (ada_all,) = _all_gather_rows([ada_part], "gather_ada")
    ada_all = ada_all.reshape(N_CHIPS, 2, N_DEV, n_ada)[:, 0]
    ada = lax.dynamic_index_in_dim(ada_all, me, axis=1, keepdims=False).reshape(1, 6 * D) + b_ada

    rr = lax.broadcasted_iota(jnp.int32, (BLK, BLK), 0) // CH
    cc = lax.broadcasted_iota(jnp.int32, (BLK, BLK), 1) // CH
    ws_b = jnp.where((rr >= cc)[None], w_s[0], 0.0).astype(BF16)
    bst = b_s[0].T
    lnw, lnb = v_ln_w, v_ln_b
    nw1, nw2, fw = norm1_w, norm2_w, final_norm_w.reshape(1, D)

    tables = _hgrn_tables()
    (h1, proj, ycat, o_pre, a_all, st_all), (w_out_b, w_fi_b, w_fo_b) = _proj_hgrn_fwd(
        x2d, nw1, ada, w_in_b, lower_bounds, gn_w, tables, placed, [0, 1, 0])

    dycat, dx1, h2, act, dff, dgu, dmix, acc2, ycat = _token_local(
        x2d, ycat, tgt, ada, nw2, ada, ada, ada, fw, w_out_b, w_fi_b, w_fo_b, proj, ws_b, bst, lnw, lnb, tm)

    tt = min(WGRAD_TOKENS, T)
    order = jnp.concatenate([chip ^ jnp.arange(N_CHIPS, dtype=jnp.int32), chip_idx]).astype(jnp.int32)

    def by_core_half(g):
        return g.reshape(g.shape[0], 2, g.shape[1] // 2, g.shape[2])

    def core_sums(g4, recv, names):
        return [_add_core_halves(a, b, c_idx, _row_block(a.shape[2]), "add_core_" + n) for a, b, n in zip(g4, recv, names)]

    def chip_sums(sums, slots, names):
        return [_add_chips(o, s, order, _row_block(s.shape[1]), "add_chips_" + n) for o, s, n in zip(sums, slots, names)]

    g_out = _wgrad(ycat, dmix, D, D, tt, "wgrad_out").reshape(N_CHIPS, D // N_CHIPS, D)
    g_fi = _wgrad(h2, dgu, D, FFB, tt, "wgrad_ffn_in")
    g_fo = _wgrad(act, dff, FFB, D, tt, "wgrad_ffn_out").reshape(N_CHIPS, DFF // N_CHIPS, D)
    late_names = ["out", "ffn_in", "ffn_out"]
    late_g4 = [by_core_half(g) for g in (g_out, g_fi, g_fo)]

    (dproj, dws, dbs, dln), late_recv = _gmlp_bwd(proj, dycat, ws_b, bst, lnw, lnb, late_g4)
    late_sums = core_sums(late_g4, late_recv, late_names)
    (dproj, dlb, dgn), late_slots = _hgrn_bwd(
        proj, o_pre, a_all, st_all, dycat, lower_bounds, gn_w, dproj, tables, late_sums)

    g_in, g_in_wire = _wgrad(h1, dproj, D, D, tt, "wgrad_in", bf16_copy=True)
    (in_recv,) = _exchange_core_halves([by_core_half(g_in_wire)], "exchange_core_halves_in")
    in_sums = [_add_core_halves_in(by_core_half(g_in), in_recv, c_idx, "add_core_in")]
    (grad_x, acc1), in_slots = _proj_in_bwd(dproj, x2d, dx1, nw1, ada, w_in_b, tp, in_sums)
    names = ["in"] + late_names
    halves = chip_sums(in_sums, in_slots, ["in"]) + chip_sums(late_sums, late_slots, late_names)
    sibling_halves = _share_halves(halves)

    big_w = [(w_in, m_w_in, v_w_in), (w_out, m_w_out, v_w_out), (w_ffn_in, m_w_ffn_in, v_w_ffn_in),
             (w_ffn_out, m_w_ffn_out, v_w_ffn_out)]
    big_out = []
    for mine, sib, (w, m, v), n in zip(halves, sibling_halves, big_w, names):
        res = _adamw_halves(w[0], mine, sib, m[0], v[0], c_idx, _row_block(mine.shape[0]), "adamw_" + n)
        big_out.append([r[None] for r in res])

    gathered = _all_gather_rows([acc1, acc2, dln, dlb, dgn, dbs, dws], "gather_small")
    small, loss, dada_all = _small_finalize(
        gathered,
        _small_2d(b_ada, norm1_w, norm2_w, final_norm_w, v_ln_w, v_ln_b, lower_bounds, gn_w, b_s, w_s),
        _small_2d(m_b_ada, m_norm1_w, m_norm2_w, m_final_norm_w, m_v_ln_w, m_v_ln_b, m_lower_bounds, m_gn_w, m_b_s, m_w_s),
        _small_2d(v_b_ada, v_norm1_w, v_norm2_w, v_final_norm_w, v_v_ln_w, v_v_ln_b, v_lower_bounds, v_gn_w, v_b_s, v_w_s))
    small = [_small_original_shapes(d) for d in small]
    loss = loss.reshape(())

    ada_out = [o[None] for o in _ada_wgrad_adam(cact.T, dada_all, w_ada[0], m_w_ada[0], v_w_ada[0], chip_idx)]

    order_names = ['w_ada', 'b_ada', 'norm1_w', 'w_in', 'w_s', 'b_s', 'v_ln_w', 'v_ln_b', 'lower_bounds', 'gn_w',
                   'w_out', 'norm2_w', 'w_ffn_in', 'w_ffn_out', 'final_norm_w']
    big_idx = {'w_in': 0, 'w_out': 1, 'w_ffn_in': 2, 'w_ffn_out': 3}
    outs = [loss, grad_x.reshape(1, T, D)]
    for kind in range(4):
        for n in order_names:
            if n == 'w_ada':
                outs.append(ada_out[kind])
            elif n in big_idx:
                outs.append(big_out[big_idx[n]][kind])
            else:
                outs.append(small[kind][n])
    return tuple(outs)
```

```python
import functools

import jax
import jax.numpy as jnp
import numpy as np
from jax import lax
from jax.experimental import pallas as pl
from jax.experimental.pallas import tpu as pltpu

F32 = jnp.float32
BF16 = jnp.bfloat16
SDS = jax.ShapeDtypeStruct
MESH = pl.DeviceIdType.MESH
HIGHEST = lax.Precision.HIGHEST

D = 1024
DG = 512
DH = 512
NH = 4
HD = 128
BLK = 128
CH = 64
DFF = 2816
DIN = 3072
FFB = 1408
LEVELS = (64, 32, 16, 8, 4, 2)
HGRN_CHUNKS_PER_STEP = 8
TOKEN_TILE = 256
PROJ_TILE = 1024
WGRAD_TOKENS = 2048
N_CHIPS = 4
N_DEV = 8
EPS = 1e-6
LR, B1, B2, AEPS, WD, STEP = 0.001, 0.9, 0.999, 1e-08, 0.01, 10

NT = (((1,), (1,)), ((), ()))
TN = (((0,), (0,)), ((), ()))


def _full(shape):
    nd = len(shape)
    return pl.BlockSpec(shape, lambda *_: (0,) * nd)


ADA_SH1, ADA_SC1, ADA_G1, ADA_SH2, ADA_SC2, ADA_G2 = range(6)


def _ada_part(k):
    return pl.BlockSpec((1, D), lambda *_: (0, k))


def _resident(shape):
    nd = len(shape)
    return pl.BlockSpec(shape, lambda *_: (0,) * nd, pipeline_mode=pl.Buffered(1))


def _arb(n=1):
    return pltpu.CompilerParams(dimension_semantics=("arbitrary",) * n)


def _dot(a, b, dims=None, precision=None):
    if dims is None:
        return jnp.dot(a, b, preferred_element_type=F32, precision=precision)
    return lax.dot_general(a, b, dims, preferred_element_type=F32, precision=precision)


def _sigmoid(x):
    return jax.nn.sigmoid(x)


def _gelu_parts(x):
    cdf = 0.5 * (1.0 + lax.erf(x * 0.7071067811865476))
    pdf = jnp.exp(-0.5 * x * x) * 0.3989422804014327
    return x * cdf, cdf + x * pdf


def _rms(x):
    return lax.rsqrt(jnp.mean(x * x, axis=-1, keepdims=True) + EPS)


def _rms_bwd(xhat, r, gw):
    return r * (gw - xhat * jnp.mean(xhat * gw, axis=-1, keepdims=True))


def _lower_bound(lbp_ref):
    l0, l1 = lbp_ref[0:1, :], lbp_ref[1:2, :]
    m = jnp.maximum(l0, l1)
    e0, e1 = jnp.exp(l0 - m), jnp.exp(l1 - m)
    return e0 / (e0 + e1), e1 / (e0 + e1)


def _gmlp_common(u, v, lnw, lnb, ws_ref, bst_ref):
    ug, dug = _gelu_parts(u)
    vg, dvg = _gelu_parts(v)
    mu = jnp.mean(vg, axis=-1, keepdims=True)
    vc = vg - mu
    rstd = lax.rsqrt(jnp.mean(vc * vc, axis=-1, keepdims=True) + EPS)
    vhat = vc * rstd
    vn = vhat * lnw + lnb
    vnb = vn.astype(BF16)
    mixed = []
    for h in range(NH):
        sl = slice(h * HD, (h + 1) * HD)
        mixed.append(_dot(ws_ref[h], vnb[:, sl]) + bst_ref[:, h:h + 1])
    return ug, dug, dvg, rstd, vhat, vnb, jnp.concatenate(mixed, axis=1)


def _hgrn_tables():
    t = np.arange(CH)[:, None]
    j = np.arange(CH)[None, :]
    blocks = [j <= t, j > t]
    masks = []
    for n in LEVELS:
        mid = t - t % n + n // 2
        blocks.append(np.where(t >= mid, (j >= mid) & (j <= t), (j > t) & (j < mid)))
        masks.append((t // n == j // n) & (t % n >= n // 2) & (j % n < n // 2))
    w = np.concatenate(blocks, axis=0).astype(np.float32)
    m = np.stack(masks).astype(np.float32)
    return (jnp.asarray(w, BF16), jnp.asarray(w.T, BF16), jnp.asarray(m), jnp.asarray(m + m.transpose(0, 2, 1)))


def _split_dot(w, x, parts):
    acc = None
    for _ in range(parts):
        piece = x.astype(BF16)
        term = _dot(w, piece)
        acc = term if acc is None else acc + term
        x = x - piece.astype(F32)
    return acc


def _hgrn_decays(f, w_ref):
    b = _split_dot(w_ref[0:CH, :], jnp.log(f), 3)
    row = lax.broadcasted_iota(jnp.int32, (CH, 1), 0)
    blocks = [jnp.exp(b), jnp.exp(b[CH - 1:CH, :] - b)]
    for n in LEVELS:
        up = (row & (n // 2)) != 0
        if n >= 8:
            ref = b.reshape(CH // n, n, DH)[:, n // 2 - 1:n // 2, :]
            ref = jnp.broadcast_to(ref, (CH // n, n, DH)).reshape(CH, DH)
            blocks.append(jnp.exp(jnp.where(up, b - ref, ref - b)))
        elif n == 4:
            r4 = row & 3
            two = jnp.where(r4 == 3, pltpu.roll(f, 1, 0) * f, 1.0)
            blocks.append(jnp.where(r4 == 0, pltpu.roll(f, CH - 1, 0), jnp.where(r4 == 2, f, two)))
        else:
            blocks.append(jnp.where(up, f, 1.0))
    return blocks


def _hgrn_gates(q, fl, lb, omlb, w_ref):
    sq = _sigmoid(q)
    qf = q * sq
    sig = _sigmoid(fl)
    f = lb + omlb * sig
    k = 1.0 - f
    return sq, qf, sig, f, k, _hgrn_decays(f, w_ref)


def _level_factor(e, li, sl, row, qh, kh):
    el = e[2 + li][:, sl]
    up = (row & (LEVELS[li] // 2)) != 0
    return el, up, el * jnp.where(up, qh, kh)


def _proj_hgrn_fwd(x, nw1, ada, w_in_b, lower_bounds, gn_w, tables, placed, axes):
    T = x.shape[0]
    nc = T // CH
    nch = min(HGRN_CHUNKS_PER_STEP, nc)
    steps = nc // nch
    w_st, _, masks, _ = tables
    nw = len(placed)
    pass_step = (13 * steps) // 16
    q0 = 2 * DG

    def body(*refs):
        x_ref, nw_ref, sc_ref, sh_ref, win_ref, lbp_ref, gn_ref, w_ref, m_ref = refs[:9]
        h_ref, p_ref, y_ref, o_ref, a_ref, st_ref = refs[9 + nw:15 + nw]
        s_scr, send_sems, recv_sems = refs[15 + 2 * nw:]
        gather = _WeightGather(refs[15 + nw:15 + 2 * nw], axes, send_sems, recv_sems)
        step = pl.program_id(0)
        xv = x_ref[...]
        hb = (((xv * _rms(xv)) * nw_ref[...]) * (1.0 + sc_ref[...]) + sh_ref[...]).astype(BF16)
        h_ref[...] = hb
        p_ref[...] = _dot(hb, win_ref[...])

        @pl.when(step == 0)
        def _():
            gather.start()
            s_scr[...] = jnp.zeros_like(s_scr)

        @pl.when(step == pass_step)
        def _():
            gather.forward()

        lb, omlb = _lower_bound(lbp_ref)
        row = lax.broadcasted_iota(jnp.int32, (CH, 1), 0)
        eye = lax.broadcasted_iota(jnp.int32, (CH, CH), 0) == lax.broadcasted_iota(jnp.int32, (CH, CH), 1)
        in_level = [m_ref[li] > 0.0 for li in range(len(LEVELS))]
        pre = []
        for ci in range(nch):
            rs = slice(ci * CH, (ci + 1) * CH)
            _, qf, _, _, k, e = _hgrn_gates(p_ref[rs, q0:q0 + DH], p_ref[rs, q0 + DH:q0 + 2 * DH], lb, omlb, w_ref)
            mats = []
            for h in range(NH):
                sl = slice(h * HD, (h + 1) * HD)
                qh, kh = qf[:, sl], k[:, sl]
                a = jnp.where(eye, jnp.sum(qh * kh, axis=-1, keepdims=True), 0.0)
                for li in range(len(LEVELS)):
                    _, _, y = _level_factor(e, li, sl, row, qh, kh)
                    yb = y.astype(BF16)
                    a = jnp.where(in_level[li], _dot(yb, yb, NT), a)
                a_ref[ci, h] = a
                mats.append(a.astype(BF16))
            eb = e[0]
            pre.append(((qf * eb).astype(BF16), eb[CH - 1:CH, :], (k * e[1]).astype(BF16), mats))
        for ci in range(nch):
            rs = slice(ci * CH, (ci + 1) * CH)
            qe, ebl, kd, mats = pre[ci]
            v = p_ref[rs, q0 + 2 * DH:q0 + 3 * DH]
            g = p_ref[rs, q0 + 3 * DH:q0 + 4 * DH]
            for h in range(NH):
                sl = slice(h * HD, (h + 1) * HD)
                st0 = s_scr[h]
                st_ref[ci, h] = st0
                vb = v[:, sl].astype(BF16)
                o = _dot(qe[:, sl], st0.astype(BF16), NT) + _dot(mats[h], vb)
                s_scr[h] = st0 * ebl[:, sl] + _dot(vb, kd[:, sl], TN)
                o_ref[rs, sl] = o
                gh = g[:, sl]
                y_ref[rs, sl] = (((o * _rms(o)) * gn_ref[...]) * (gh * _sigmoid(gh))).astype(BF16)

        @pl.when(step == steps - 1)
        def _():
            gather.finish()

    rows = nch * CH
    row = lambda c: (c, 0)
    anyspec = pl.BlockSpec(memory_space=pl.ANY)
    res = pl.pallas_call(
        body, grid=(steps,),
        in_specs=[pl.BlockSpec((rows, D), row), _full((1, D)), _ada_part(ADA_SC1), _ada_part(ADA_SH1), _resident((D, DIN)),
                  _full((2, DH)), _full((1, HD)), _full(w_st.shape), _full(masks.shape)] + [anyspec] * nw,
        out_specs=[pl.BlockSpec((rows, D), row), pl.BlockSpec((rows, DIN), row),
                   pl.BlockSpec((rows, DH), lambda c: (c, 1)),
                   pl.BlockSpec((rows, DH), row),
                   pl.BlockSpec((nch, NH, CH, CH), lambda c: (c, 0, 0, 0)),
                   pl.BlockSpec((nch, NH, HD, HD), lambda c: (c, 0, 0, 0))] + [anyspec] * nw,
        out_shape=[SDS((T, D), BF16), SDS((T, DIN), F32), SDS((T, D), BF16), SDS((T, DH), F32),
                   SDS((nc, NH, CH, CH), F32), SDS((nc, NH, HD, HD), F32)] + [SDS(a.shape, a.dtype) for a in placed],
        scratch_shapes=[pltpu.VMEM((NH, HD, HD), F32)] + _gather_sems(nw),
        input_output_aliases={9 + i: 6 + i for i in range(nw)},
        compiler_params=_arb(), name="proj_hgrn_fwd")(x, nw1, ada, ada, w_in_b, lower_bounds, gn_w, w_st, masks, *placed)
    return res[:6], res[6:]


def _token_local(x, ycat, tgt, g1, nw2, sc2, sh2, g2, fw, w_out_b, w_fi_b, w_fo_b, proj, ws_b, bst, lnw, lnb, tm):
    T = x.shape[0]
    inv_d = 1.0 / D

    def body(x_ref, yb_ref, t_ref, g1_ref, nw2_ref, sc2_ref, sh2_ref, g2_ref, fw_ref, wo_ref, wfi_ref, wfo_ref,
             u_ref, v_ref, ws_ref, bst_ref, lnw_ref, lnb_ref,
             dy_ref, dx1_ref, h2_ref, act_ref, dff_ref, dgu_ref, dmix_ref, acc_ref, ya_ref):
        @pl.when(pl.program_id(0) == 0)
        def _():
            acc_ref[...] = jnp.zeros_like(acc_ref)

        def acc(row, val):
            acc_ref[row:row + 1, :] += jnp.sum(val, axis=0, keepdims=True)

        for bi in range(tm // BLK):
            rs = slice(bi * BLK, (bi + 1) * BLK)
            ug, _, _, _, _, _, mixed = _gmlp_common(u_ref[rs, :], v_ref[rs, :], lnw_ref[...], lnb_ref[...], ws_ref, bst_ref)
            ya_ref[rs, :] = (ug * mixed).astype(BF16)
        g1v, g2v = g1_ref[...], g2_ref[...]
        mix = _dot(ya_ref[...], wo_ref[0:DG, :]) + _dot(yb_ref[...], wo_ref[DG:D, :])
        x1 = x_ref[...] + g1v * mix
        r2 = _rms(x1)
        xh2 = x1 * r2
        n2 = xh2 * nw2_ref[...]
        osc2 = 1.0 + sc2_ref[...]
        h2b = (n2 * osc2 + sh2_ref[...]).astype(BF16)
        h2_ref[...] = h2b
        ff = jnp.zeros((tm, D), F32)
        saved = []
        for kb in range(DFF // FFB):
            gate = _dot(h2b, wfi_ref[:, kb * FFB:(kb + 1) * FFB])
            up = _dot(h2b, wfi_ref[:, DFF + kb * FFB:DFF + (kb + 1) * FFB])
            sg = _sigmoid(gate)
            actb = (gate * sg * up).astype(BF16)
            act_ref[:, kb * FFB:(kb + 1) * FFB] = actb
            ff = ff + _dot(actb, wfo_ref[kb * FFB:(kb + 1) * FFB, :])
            saved.append((gate, up, sg))
        x2 = x1 + g2v * ff
        r3 = _rms(x2)
        xh3 = x2 * r3
        err = xh3 * fw_ref[...] - t_ref[...]
        acc(6, (0.5 * inv_d) * err * err)
        dy = err * inv_d
        acc(4, dy * xh3)
        dx2 = _rms_bwd(xh3, r3, dy * fw_ref[...])
        acc(0, dx2 * ff)
        dffb = (dx2 * g2v).astype(BF16)
        dff_ref[...] = dffb
        dh2 = jnp.zeros((tm, D), F32)
        for kb in range(DFF // FFB):
            gate, up, sg = saved[kb]
            da = _dot(dffb, wfo_ref[kb * FFB:(kb + 1) * FFB, :], NT)
            dgate = (da * up * (sg * (1.0 + gate * (1.0 - sg)))).astype(BF16)
            dup = (da * gate * sg).astype(BF16)
            dgu_ref[:, kb * FFB:(kb + 1) * FFB] = dgate
            dgu_ref[:, DFF + kb * FFB:DFF + (kb + 1) * FFB] = dup
            dh2 = dh2 + _dot(dgate, wfi_ref[:, kb * FFB:(kb + 1) * FFB], NT)
            dh2 = dh2 + _dot(dup, wfi_ref[:, DFF + kb * FFB:DFF + (kb + 1) * FFB], NT)
        acc(2, dh2)
        acc(1, dh2 * n2)
        dn2 = dh2 * osc2
        acc(3, dn2 * xh2)
        dx1 = dx2 + _rms_bwd(xh2, r2, dn2 * nw2_ref[...])
        acc(5, dx1 * mix)
        dmixb = (dx1 * g1v).astype(BF16)
        dmix_ref[...] = dmixb
        dy_ref[...] = _dot(dmixb, wo_ref[...], NT)
        dx1_ref[...] = dx1

    row = lambda i: (i, 0)
    vec = _full((1, D))
    half = lambda j: pl.BlockSpec((tm, DG), lambda i: (i, j))
    return pl.pallas_call(
        body, grid=(T // tm,),
        in_specs=[pl.BlockSpec((tm, D), row), half(1), pl.BlockSpec((tm, D), row),
                  _ada_part(ADA_G1), vec, _ada_part(ADA_SC2), _ada_part(ADA_SH2), _ada_part(ADA_G2), vec,
                  _resident((D, D)), _resident((D, 2 * DFF)), _resident((DFF, D)),
                  half(0), half(1), _full((NH, BLK, BLK)), _full((BLK, NH)), _full((1, DG)), _full((1, DG))],
        out_specs=[pl.BlockSpec((tm, D), row), pl.BlockSpec((tm, D), row), pl.BlockSpec((tm, D), row),
                   pl.BlockSpec((tm, DFF), row), pl.BlockSpec((tm, D), row), pl.BlockSpec((tm, 2 * DFF), row),
                   pl.BlockSpec((tm, D), row), _full((8, D)), half(0)],
        out_shape=[SDS((T, D), F32), SDS((T, D), F32), SDS((T, D), BF16), SDS((T, DFF), BF16), SDS((T, D), BF16),
                   SDS((T, 2 * DFF), BF16), SDS((T, D), BF16), SDS((8, D), F32), SDS((T, D), BF16)],
        input_output_aliases={1: 8},
        compiler_params=_arb(), name="token_local")(x, ycat, tgt, g1, nw2, sc2, sh2, g2, fw, w_out_b, w_fi_b, w_fo_b,
                                                    proj, proj, ws_b, bst, lnw, lnb)


def _gmlp_bwd_wgrad(proj, dycat, ws_b, bst, lnw, lnb, grads, a, b, bn, tt):
    T, K = a.shape
    nn, nt = b.shape[1] // bn, T // tt
    nb = nn * nt
    rows = T // nb
    assert rows % BLK == 0, (T, nb)
    nw = len(grads)

    def body(*refs):
        u_ref, v_ref, dy_ref, ws_ref, bst_ref, lnw_ref, lnb_ref = refs[:7]
        a_ref, b_ref = refs[7 + nw:9 + nw]
        dp_ref, dws_ref, dbs_ref, dln_ref = refs[9 + nw:13 + nw]
        g_ref, g16_ref = refs[13 + 2 * nw:15 + 2 * nw]
        dbs_acc, send_sems, recv_sems = refs[15 + 2 * nw:]
        exchange = _CoreExchange(refs[7:7 + nw], refs[13 + nw:13 + 2 * nw], send_sems, recv_sems)
        t = pl.program_id(1)
        i = pl.program_id(0) * nt + t

        @pl.when(i == 0)
        def _():
            exchange.start()
            dws_ref[...] = jnp.zeros_like(dws_ref)
            dln_ref[...] = jnp.zeros_like(dln_ref)
            dbs_acc[...] = jnp.zeros_like(dbs_acc)

        @pl.when(t == 0)
        def _():
            g_ref[...] = jnp.zeros_like(g_ref)

        g_ref[0] += _dot(a_ref[...], b_ref[...], TN)

        @pl.when(t == nt - 1)
        def _():
            g16_ref[...] = g_ref[...].astype(BF16)

        r = lax.broadcasted_iota(jnp.int32, (BLK, BLK), 0) // CH
        c = lax.broadcasted_iota(jnp.int32, (BLK, BLK), 1) // CH
        for bi in range(rows // BLK):
            rs = slice(bi * BLK, (bi + 1) * BLK)
            ug, dug, dvg, rstd, vhat, vnb, mixed = _gmlp_common(
                u_ref[rs, :], v_ref[rs, :], lnw_ref[...], lnb_ref[...], ws_ref, bst_ref)
            dya = dy_ref[rs, :]
            dp_ref[rs, 0:DG] = (dya * mixed * dug).astype(BF16)
            dmixed = dya * ug
            dbs_acc[...] += dmixed
            dmb = dmixed.astype(BF16)
            dvn = []
            for h in range(NH):
                sl = slice(h * HD, (h + 1) * HD)
                dws_ref[h * BLK:(h + 1) * BLK, :] += jnp.where(r >= c, _dot(dmb[:, sl], vnb[:, sl], NT), 0.0)
                dvn.append(_dot(ws_ref[h], dmb[:, sl], TN))
            dvn = jnp.concatenate(dvn, axis=1)
            dln_ref[0:1, :] += jnp.sum(dvn * vhat, axis=0, keepdims=True)
            dln_ref[1:2, :] += jnp.sum(dvn, axis=0, keepdims=True)
            dvh = dvn * lnw_ref[...]
            dvgel = rstd * (dvh - jnp.mean(dvh, axis=-1, keepdims=True) - vhat * jnp.mean(dvh * vhat, axis=-1, keepdims=True))
            dp_ref[rs, DG:2 * DG] = (dvgel * dvg).astype(BF16)

        @pl.when(i == nb - 1)
        def _():
            head = lax.broadcasted_iota(jnp.int32, (8, BLK), 0)
            ones = jnp.ones((8, HD), F32)
            out = jnp.zeros((8, BLK), F32)
            for h in range(NH):
                sums = _dot(ones, dbs_acc[:, h * HD:(h + 1) * HD], NT, precision=HIGHEST)
                out = out + jnp.where(head == h, sums, 0.0)
            dbs_ref[...] = out
            exchange.finish()

    anyspec = pl.BlockSpec(memory_space=pl.ANY)
    gspec = pl.BlockSpec((1, K, bn), lambda n, t: (n, 0, 0))
    res = pl.pallas_call(
        body, grid=(nn, nt),
        in_specs=[pl.BlockSpec((rows, DG), lambda n, t: (n * nt + t, 0)), pl.BlockSpec((rows, DG), lambda n, t: (n * nt + t, 1)),
                  pl.BlockSpec((rows, DG), lambda n, t: (n * nt + t, 0)),
                  _full((NH, BLK, BLK)), _full((BLK, NH)), _full((1, DG)), _full((1, DG))] + [anyspec] * nw
        + [pl.BlockSpec((tt, K), lambda n, t: (t, 0)), pl.BlockSpec((tt, bn), lambda n, t: (t, n))],
        out_specs=[pl.BlockSpec((rows, 2 * DG), lambda n, t: (n * nt + t, 2)), _full((NH * BLK, BLK)), _full((8, BLK)),
                   _full((8, DG))] + [anyspec] * nw + [gspec, gspec],
        out_shape=[SDS((T, DIN), BF16), SDS((NH * BLK, BLK), F32), SDS((8, BLK), F32), SDS((8, DG), F32)]
        + _core_exchange_shapes(grads) + [SDS((nn, K, bn), F32), SDS((nn, K, bn), BF16)],
        scratch_shapes=[pltpu.VMEM((BLK, DG), F32)] + _core_exchange_sems(nw),
        compiler_params=_arb(2), name="gmlp_bwd_wgrad_ffn_in")(proj, proj, dycat, ws_b, bst, lnw, lnb, *grads, a, b)
    return res[:4], res[4:4 + nw], res[4 + nw:]


def _hgrn_bwd(proj, o_pre, a_all, st_all, dycat, lower_bounds, gn_w, dproj, tables, sums):
    T = proj.shape[0]
    nc = T // CH
    nch = min(HGRN_CHUNKS_PER_STEP, nc)
    steps = nc // nch
    w_st, w_st_t, _, masks_sym = tables
    n_lev = len(LEVELS)
    nw = len(sums)

    def body(*refs):
        q_ref, f_ref, i_ref, g_ref, o_ref, a_ref, st_ref, dy_ref, lbp_ref, gn_ref, w_ref, wt_ref, ms_ref = refs[:13]
        dp_ref, dlb_ref, dgn_ref = refs[14 + nw:17 + nw]
        ds_scr, dx_scr, send_sems, recv_sems = refs[17 + 2 * nw:]
        exchange = _ChipExchange(refs[14:14 + nw], refs[17 + nw:17 + 2 * nw], send_sems, recv_sems)
        i = pl.program_id(0)

        @pl.when(i == 0)
        def _():
            exchange.start()
            ds_scr[...] = jnp.zeros_like(ds_scr)
            dlb_ref[...] = jnp.zeros_like(dlb_ref)
            dgn_ref[...] = jnp.zeros_like(dgn_ref)

        lb, omlb = _lower_bound(lbp_ref)
        row = lax.broadcasted_iota(jnp.int32, (CH, 1), 0)
        eye = lax.broadcasted_iota(jnp.int32, (CH, CH), 0) == lax.broadcasted_iota(jnp.int32, (CH, CH), 1)
        lower = lax.broadcasted_iota(jnp.int32, (CH, CH), 0) > lax.broadcasted_iota(jnp.int32, (CH, CH), 1)
        dgn = jnp.zeros((1, HD), F32)
        pre = []
        for ci in range(nch):
            rs = slice(ci * CH, (ci + 1) * CH)
            q = q_ref[rs, :]
            v = i_ref[rs, :]
            g = g_ref[rs, :]
            sq, qf, sig, f, k, e = _hgrn_gates(q, f_ref[rs, :], lb, omlb, w_ref)
            eb = e[0]
            ekd = e[1]
            kd = k * ekd
            qe = qf * eb
            dob_h, dqe_h, dqf_h, dki_h, dv_h, dg_h = [], [], [], [], [], []
            for h in range(NH):
                sl = slice(h * HD, (h + 1) * HD)
                o = o_ref[rs, sl]
                ro = _rms(o)
                oh = o * ro
                gh = g[:, sl]
                sg = _sigmoid(gh)
                dyb = dy_ref[rs, sl]
                dg_h.append(dyb * (oh * gn_ref[...]) * (sg * (1.0 + gh * (1.0 - sg))))
                don = dyb * (gh * sg)
                dgn = dgn + jnp.sum(don * oh, axis=0, keepdims=True)
                dob = _rms_bwd(oh, ro, don * gn_ref[...]).astype(BF16)
                vb = v[:, sl].astype(BF16)
                qh, kh = qf[:, sl], k[:, sl]
                dqe = _dot(dob, st_ref[ci, h].astype(BF16))
                da = _dot(dob, vb, NT)
                ddiag = jnp.sum(jnp.where(eye, da, 0.0), axis=-1, keepdims=True)
                dsym = jnp.where(lower, da, _dot(vb, dob, NT))
                upper_part = jnp.zeros((CH, HD), F32)
                both = jnp.zeros((CH, HD), F32)
                for li in range(n_lev):
                    el, up, y = _level_factor(e, li, sl, row, qh, kh)
                    dyv = _dot((ms_ref[li] * dsym).astype(BF16), y.astype(BF16))
                    dx_scr[ci, (2 + li) * CH:(3 + li) * CH, sl] = dyv * y
                    dye = dyv * el
                    upper_part = upper_part + jnp.where(up, dye, 0.0)
                    both = both + dye
                dob_h.append(dob)
                dqe_h.append(dqe)
                dqf_h.append(dqe * eb[:, sl] + ddiag * kh + upper_part)
                dki_h.append(ddiag * qh + (both - upper_part))
                dv_h.append(_dot(a_ref[ci, h].astype(BF16), dob, TN))
            dp_ref[rs, 0:DH] = (jnp.concatenate(dqf_h, axis=1) * (sq * (1.0 + q * (1.0 - sq)))).astype(BF16)
            dp_ref[rs, 3 * DH:4 * DH] = jnp.concatenate(dg_h, axis=1).astype(BF16)
            pre.append((v, sig, f, eb, ekd, kd, qe, dob_h, jnp.concatenate(dqe_h, axis=1), dki_h, dv_h))
        dgn_ref[0:1, :] += dgn
        for ci in reversed(range(nch)):
            rs = slice(ci * CH, (ci + 1) * CH)
            v, sig, f, eb, ekd, kd, qe, dob_h, dqe, dki_h, dv_h = pre[ci]
            ebl = eb[CH - 1:CH, :]
            dbl_h, dkd_h, dv2_h = [], [], []
            for h in range(NH):
                sl = slice(h * HD, (h + 1) * HD)
                dst1 = ds_scr[h]
                dst1b = dst1.astype(BF16)
                ds_scr[h] = dst1 * ebl[:, sl] + _dot(dob_h[h], qe[:, sl].astype(BF16), TN)
                dbl_h.append(ebl[:, sl] * jnp.sum(st_ref[ci, h] * dst1, axis=0, keepdims=True))
                dkd_h.append(_dot(v[:, sl].astype(BF16), dst1b))
                dv2_h.append(dv_h[h] + _dot(kd[:, sl].astype(BF16), dst1b, NT))
            dkd = jnp.concatenate(dkd_h, axis=1)
            dx_scr[ci, 0:CH, :] = dqe * qe + jnp.where(row == CH - 1, jnp.concatenate(dbl_h, axis=1), 0.0)
            dx_scr[ci, CH:2 * CH, :] = dkd * kd
            dlf = _split_dot(wt_ref[...], dx_scr[ci], 2)
            df = dlf / f - (dkd * ekd + jnp.concatenate(dki_h, axis=1))
            dlb_ref[0:1, :] += jnp.sum(df * (1.0 - sig), axis=0, keepdims=True)
            dp_ref[rs, DH:2 * DH] = (df * omlb * sig * (1.0 - sig)).astype(BF16)
            dp_ref[rs, 2 * DH:3 * DH] = jnp.concatenate(dv2_h, axis=1).astype(BF16)

        @pl.when(i == steps - 1)
        def _():
            gl = dlb_ref[0:1, :] * lb * omlb
            dlb_ref[0:1, :] = gl
            dlb_ref[1:2, :] = -gl
            exchange.finish()

    rev = lambda j: pl.BlockSpec((nch * CH, DH), lambda c: (steps - 1 - c, j))
    anyspec = pl.BlockSpec(memory_space=pl.ANY)
    res = pl.pallas_call(
        body, grid=(steps,),
        in_specs=[rev(2), rev(3), rev(4), rev(5), rev(0),
                  pl.BlockSpec((nch, NH, CH, CH), lambda c: (steps - 1 - c, 0, 0, 0)),
                  pl.BlockSpec((nch, NH, HD, HD), lambda c: (steps - 1 - c, 0, 0, 0)),
                  rev(1), _full((2, DH)), _full((1, HD)),
                  _full(w_st.shape), _full(w_st_t.shape), _full(masks_sym.shape),
                  anyspec] + [anyspec] * nw,
        out_specs=[pl.BlockSpec((nch * CH, 4 * DH), lambda c: (steps - 1 - c, 0)), _full((8, DH)), _full((8, HD))]
        + [anyspec] * nw,
        out_shape=[SDS((T, DIN), BF16), SDS((8, DH), F32), SDS((8, HD), F32)] + _slot_shapes(sums),
        scratch_shapes=[pltpu.VMEM((NH, HD, HD), F32), pltpu.VMEM((nch, (2 + n_lev) * CH, DH), F32)] + _exchange_sems(nw),
        input_output_aliases={13: 0},
        compiler_params=_arb(), name="hgrn_bwd")(proj, proj, proj, proj, o_pre, a_all, st_all, dycat, lower_bounds, gn_w,
                                                 w_st, w_st_t, masks_sym, dproj, *sums)
    return res[:3], res[3:]


def _proj_in_bwd(dproj, x, dx1, nw, sc, w_in_b, tm, sums):
    T = x.shape[0]
    ns = len(sums)
    steps = T // tm

    def body(*refs):
        dp_ref, x_ref, dx1_ref, nw_ref, sc_ref, w_ref = refs[:6]
        gx_ref, acc_ref = refs[6 + ns:8 + ns]
        exchange = _ChipExchange(refs[6:6 + ns], refs[8 + ns:8 + 2 * ns], *refs[8 + 2 * ns:])

        @pl.when(pl.program_id(0) == 0)
        def _():
            exchange.start()
            acc_ref[...] = jnp.zeros_like(acc_ref)

        dh = _dot(dp_ref[:, 0:4 * DH], w_ref[:, 2 * DG:DIN], NT) + _dot(dp_ref[:, 4 * DH:DIN], w_ref[:, 0:2 * DG], NT)
        xv = x_ref[...]
        r = _rms(xv)
        xh = xv * r
        n1 = xh * nw_ref[...]
        acc_ref[0:1, :] += jnp.sum(dh, axis=0, keepdims=True)
        acc_ref[1:2, :] += jnp.sum(dh * n1, axis=0, keepdims=True)
        dn = dh * (1.0 + sc_ref[...])
        acc_ref[2:3, :] += jnp.sum(dn * xh, axis=0, keepdims=True)
        gx_ref[...] = dx1_ref[...] + _rms_bwd(xh, r, dn * nw_ref[...])

        @pl.when(pl.program_id(0) == steps - 1)
        def _():
            exchange.finish()

    row = lambda i: (i, 0)
    anyspec = pl.BlockSpec(memory_space=pl.ANY)
    res = pl.pallas_call(
        body, grid=(steps,),
        in_specs=[pl.BlockSpec((tm, DIN), row), pl.BlockSpec((tm, D), row), pl.BlockSpec((tm, D), row),
                  _full((1, D)), _ada_part(ADA_SC1), _resident((D, DIN))] + [anyspec] * ns,
        out_specs=[pl.BlockSpec((tm, D), row), _full((8, D))] + [anyspec] * ns,
        out_shape=[SDS((T, D), F32), SDS((8, D), F32)] + _slot_shapes(sums),
        scratch_shapes=_exchange_sems(ns),
        compiler_params=_arb(), name="proj_in_bwd")(dproj, x, dx1, nw, sc, w_in_b, *sums)
    return res[:2], res[2:]


def _wgrad(a, b, bk, bn, tt, name, bf16_copy=False):
    T, K = a.shape
    N = b.shape[1]
    nn, nk, nt = N // bn, K // bk, T // tt
    bmap = lambda n, k, t: (t, n)

    def body(a_ref, b_ref, o_ref, *copy_ref):
        @pl.when(pl.program_id(2) == 0)
        def _():
            o_ref[...] = jnp.zeros_like(o_ref)

        o_ref[0] += _dot(a_ref[...], b_ref[...], TN)

        if bf16_copy:
            @pl.when(pl.program_id(2) == nt - 1)
            def _():
                copy_ref[0][...] = o_ref[...].astype(BF16)

    ospec = pl.BlockSpec((1, bk, bn), lambda n, k, t: (n, k, 0))
    return pl.pallas_call(
        body, grid=(nn, nk, nt),
        in_specs=[pl.BlockSpec((tt, bk), lambda n, k, t: (t, k)), pl.BlockSpec((tt, bn), bmap)],
        out_specs=[ospec, ospec] if bf16_copy else ospec,
        out_shape=[SDS((nn, K, bn), F32), SDS((nn, K, bn), BF16)] if bf16_copy else SDS((nn, K, bn), F32),
        compiler_params=_arb(3), name=name)(a, b)


def _adam_math(w, g, m, v):
    m = B1 * m + (1.0 - B1) * g
    v = B2 * v + (1.0 - B2) * (g * g)
    m_hat = m / (1.0 - B1 ** STEP)
    v_hat = v / (1.0 - B2 ** STEP)
    return -LR * (m_hat / (jnp.sqrt(v_hat) + AEPS) + WD * w), m, v


def _adamw_halves(w, mine, sibling, m, v, c_idx, rb, name):
    R, C = w.shape
    nb = (R // 2) // rb

    def body(c_ref, w_ref, a_ref, b_ref, m_ref, v_ref, g_out, d_out, m_out, v_out):
        g = jnp.where(pl.program_id(0) == c_ref[0], a_ref[...], b_ref[...])
        g_out[...] = g
        d_out[...], m_out[...], v_out[...] = _adam_math(w_ref[...], g, m_ref[...], v_ref[...])

    whole = pl.BlockSpec((rb, C), lambda hh, i, cr: (hh * nb + i, 0))
    half = pl.BlockSpec((rb, C), lambda hh, i, cr: (i, 0))
    return pl.pallas_call(
        body,
        grid_spec=pltpu.PrefetchScalarGridSpec(
            num_scalar_prefetch=1, grid=(2, nb), in_specs=[whole, half, half, whole, whole], out_specs=[whole] * 4),
        out_shape=[SDS((R, C), F32)] * 4, compiler_params=_arb(2), name=name)(c_idx, w, mine, sibling, m, v)


def _ada_forward(c_all, w_ada):
    n = w_ada.shape[1]

    def body(c_ref, w_ref, ca_ref, p_ref):
        cv = c_ref[...]
        ca = cv * _sigmoid(cv)
        ca_ref[...] = ca
        p_ref[...] = _dot(ca, w_ref[...], precision=HIGHEST)

    return pl.pallas_call(
        body, grid=(n // 512,),
        in_specs=[_full((N_DEV, D)), pl.BlockSpec((D, 512), lambda i: (0, i))],
        out_specs=[_full((N_DEV, D)), pl.BlockSpec((N_DEV, 512), lambda i: (0, i))],
        out_shape=[SDS((N_DEV, D), F32), SDS((N_DEV, n), F32)],
        compiler_params=_arb(), name="ada_forward")(c_all, w_ada)


def _ada_wgrad_adam(cact_t, dada_all, w, m, v, chip_idx):
    R, C = w.shape
    rb = 256

    def body(j_ref, c_ref, d_ref, w_ref, m_ref, v_ref, g_out, d_out, m_out, v_out):
        g = _dot(c_ref[...], d_ref[...], precision=HIGHEST)
        g_out[...] = g
        d_out[...], m_out[...], v_out[...] = _adam_math(w_ref[...], g, m_ref[...], v_ref[...])

    spec = pl.BlockSpec((rb, C), lambda i, j: (i, 0))
    return pl.pallas_call(
        body,
        grid_spec=pltpu.PrefetchScalarGridSpec(
            num_scalar_prefetch=1, grid=(R // rb,),
            in_specs=[pl.BlockSpec((rb, N_DEV), lambda i, j: (i, 0)), pl.BlockSpec((N_DEV, C), lambda i, j: (0, j[0])),
                      spec, spec, spec],
            out_specs=[spec] * 4),
        out_shape=[SDS((R, C), F32)] * 4,
        compiler_params=_arb(), name="ada_wgrad_adam")(chip_idx, cact_t, dada_all, w, m, v)


SMALL_NAMES = ('b_ada', 'norm1_w', 'norm2_w', 'final_norm_w', 'v_ln_w', 'v_ln_b', 'lower_bounds', 'gn_w', 'b_s', 'w_s')


def _small_finalize(gathered, params, moms, vels):
    n_in = len(gathered)

    def body(*refs):
        acc1, acc2, dln, dlb, dgn, dbs, dws = refs[:n_in]
        prm = [dict(zip(SMALL_NAMES, refs[n_in + k * 10:n_in + (k + 1) * 10])) for k in range(3)]
        outs = [dict(zip(SMALL_NAMES, refs[n_in + 30 + k * 10:n_in + 30 + (k + 1) * 10])) for k in range(4)]
        loss_ref, dada_ref = refs[n_in + 70:n_in + 72]

        def dev_sum(ref, first, n):
            per = ref.shape[0] // N_DEV
            g = ref[first:first + n, :]
            for dev in range(1, N_DEV):
                g = g + ref[dev * per + first:dev * per + first + n, :]
            return g

        def update(n, g, cols=slice(None)):
            outs[0][n][:, cols] = g
            outs[1][n][:, cols], outs[2][n][:, cols], outs[3][n][:, cols] = _adam_math(
                prm[0][n][:, cols], g, prm[1][n][:, cols], prm[2][n][:, cols])

        ada_rows = ((acc1, 0), (acc1, 1), (acc2, 5), (acc2, 2), (acc2, 1), (acc2, 0))
        for k, (ref, r) in enumerate(ada_rows):
            update('b_ada', dev_sum(ref, r, 1), slice(k * D, (k + 1) * D))
            for dev in range(N_DEV):
                dada_ref[dev:dev + 1, k * D:(k + 1) * D] = ref[8 * dev + r:8 * dev + r + 1, :]
        update('norm1_w', dev_sum(acc1, 2, 1))
        update('norm2_w', dev_sum(acc2, 3, 1))
        update('final_norm_w', dev_sum(acc2, 4, 1))
        update('v_ln_w', dev_sum(dln, 0, 1))
        update('v_ln_b', dev_sum(dln, 1, 1))
        update('lower_bounds', dev_sum(dlb, 0, 2))
        update('gn_w', dev_sum(dgn, 0, 1))
        update('b_s', dev_sum(dbs, 0, NH))
        update('w_s', dev_sum(dws, 0, NH * BLK))
        loss_ref[...] = jnp.sum(dev_sum(acc2, 6, 1), axis=-1, keepdims=True)

    shapes = [SDS(params[n].shape, F32) for n in SMALL_NAMES]
    res = pl.pallas_call(
        body, out_shape=shapes * 4 + [SDS((1, 1), F32), SDS((N_DEV, 6 * D), F32)], name="small_finalize")(
            *gathered, *[d[n] for d in (params, moms, vels) for n in SMALL_NAMES])
    return [dict(zip(SMALL_NAMES, res[k * 10:(k + 1) * 10])) for k in range(4)], res[40], res[41]


def _position():
    x, y, c = lax.axis_index("x"), lax.axis_index("y"), lax.axis_index("c")
    return x, y, c


def _chip_at(x, y, r):
    return (x ^ (r >> 1), y ^ (r & 1))


def _gather_rows(ins, outs, send_sems, recv_sems, local_sems, after_issue=None):
    nb = len(ins)
    x, y, c = _position()
    me, sibling = (x, y, c), (x, y, 1 - c)
    chips = [_chip_at(x, y, r) for r in (1, 2, 3)]

    def rows(b, px, py, pc):
        m_per = ins[b].shape[0]
        return outs[b].at[pl.ds((4 * px + 2 * py + pc) * m_per, m_per), :]

    def copy(b, k, blk, to, src=None):
        return pltpu.make_async_remote_copy(
            src_ref=rows(b, *blk) if src is None else src, dst_ref=rows(b, *blk),
            send_sem=send_sems.at[7 * b + k], recv_sem=recv_sems.at[7 * b + k], device_id=to, device_id_type=MESH)

    local, sent = [], []
    for b in range(nb):
        mine = pltpu.make_async_copy(ins[b], rows(b, *me), local_sems.at[b])
        mine.start()
        local.append(mine)
        first = [copy(b, 0, me, sibling, src=ins[b])]
        first += [copy(b, 1 + j, me, (*chip, c), src=ins[b]) for j, chip in enumerate(chips)]
        for cp in first:
            cp.start()
        sent += first
    if after_issue is not None:
        after_issue()
    for b in range(nb):
        for j, chip in enumerate(chips):
            copy(b, 1 + j, (*chip, c), me).wait_recv()
            passed = copy(b, 4 + j, (*chip, c), sibling)
            passed.start()
            sent.append(passed)
    for b in range(nb):
        copy(b, 0, sibling, me).wait_recv()
        for j, chip in enumerate(chips):
            copy(b, 4 + j, (*chip, 1 - c), me).wait_recv()
    for cp in sent:
        cp.wait_send()
    for cp in local:
        cp.wait()


def _gather_rows_shapes(blocks):
    return [SDS((N_DEV * b.shape[0], b.shape[1]), b.dtype) for b in blocks]


def _gather_rows_sems(nb):
    return [pltpu.SemaphoreType.DMA((7 * nb,)), pltpu.SemaphoreType.DMA((7 * nb,)), pltpu.SemaphoreType.DMA((nb,))]


def _all_gather_rows(blocks, name):
    nb = len(blocks)

    def body(*refs):
        _gather_rows(refs[:nb], refs[nb:2 * nb], *refs[2 * nb:])

    vmem = pl.BlockSpec(memory_space=pltpu.VMEM)
    return pl.pallas_call(
        body, out_shape=_gather_rows_shapes(blocks), in_specs=[vmem] * nb, out_specs=[vmem] * nb,
        scratch_shapes=_gather_rows_sems(nb), name=name)(*blocks)


def _place_shard(w_shard, axis, chip_idx, name):
    R, C = w_shard.shape
    rb = _row_block(R)
    nb = R // rb
    full = (R * N_CHIPS, C) if axis == 0 else (R, C * N_CHIPS)
    omap = (lambda i, j: (j[0] * nb + i, 0)) if axis == 0 else (lambda i, j: (i, j[0]))

    def body(j_ref, w_ref, o_ref):
        o_ref[...] = w_ref[...].astype(BF16)

    return pl.pallas_call(
        body,
        grid_spec=pltpu.PrefetchScalarGridSpec(
            num_scalar_prefetch=1, grid=(nb,), in_specs=[pl.BlockSpec((rb, C), lambda i, j: (i, 0))],
            out_specs=pl.BlockSpec((rb, C), omap)),
        out_shape=SDS(full, BF16), compiler_params=_arb(), name=name)(chip_idx, w_shard)


class _WeightGather:
    def __init__(self, refs, axes, send_sems, recv_sems):
        self.refs, self.axes, self.send_sems, self.recv_sems = refs, axes, send_sems, recv_sems
        self.x, self.y, self.c = _position()
        self.j = 2 * self.x + self.y
        self.n = 3 * len(refs)

    def _half(self, w, chip_idx, half):
        ref, axis = self.refs[w], self.axes[w]
        if axis == 0:
            size = ref.shape[0] // N_CHIPS
            return ref.at[pl.ds(chip_idx * size + half * (size // 2), size // 2), :]
        size = ref.shape[1] // N_CHIPS
        rows = ref.shape[0] // 2
        return ref.at[pl.ds(half * rows, rows), pl.ds(chip_idx * size, size)]

    def _ici(self, w, r, chip_idx):
        k = 3 * w + r - 1
        piece = self._half(w, chip_idx, self.c)
        return pltpu.make_async_remote_copy(
            src_ref=piece, dst_ref=piece, send_sem=self.send_sems.at[k], recv_sem=self.recv_sems.at[k],
            device_id=(*_chip_at(self.x, self.y, r), self.c), device_id_type=MESH)

    def _d2d(self, w, r, half):
        k = self.n + 3 * w + r - 1
        piece = self._half(w, self.j ^ r, half)
        return pltpu.make_async_remote_copy(
            src_ref=piece, dst_ref=piece, send_sem=self.send_sems.at[k], recv_sem=self.recv_sems.at[k],
            device_id=(self.x, self.y, 1 - self.c), device_id_type=MESH)

    def _each(self):
        return [(w, r) for w in range(len(self.refs)) for r in (1, 2, 3)]

    def start(self):
        for w, r in self._each():
            self._ici(w, r, self.j).start()

    def forward(self):
        for w, r in self._each():
            self._ici(w, r, self.j ^ r).wait_recv()
            self._d2d(w, r, self.c).start()

    def finish(self):
        for w, r in self._each():
            self._ici(w, r, self.j).wait_send()
            self._d2d(w, r, self.c).wait_send()
            self._d2d(w, r, 1 - self.c).wait_recv()


def _gather_sems(n_weights):
    return [pltpu.SemaphoreType.DMA((6 * n_weights,)), pltpu.SemaphoreType.DMA((6 * n_weights,))]


def _gather_weights(placed, axes, row_blocks, name):
    nw, nb = len(placed), len(row_blocks)

    def body(*refs):
        w_outs, b_ins, b_outs = refs[nw + nb:2 * nw + nb], refs[nw:nw + nb], refs[2 * nw + nb:2 * (nw + nb)]
        sems = refs[2 * (nw + nb):]
        g = _WeightGather(w_outs, axes, *sems[:2])
        _gather_rows(b_ins, b_outs, *sems[2:], after_issue=g.start)
        g.forward()
        g.finish()

    anyspec = pl.BlockSpec(memory_space=pl.ANY)
    vmem = pl.BlockSpec(memory_space=pltpu.VMEM)
    res = pl.pallas_call(
        body, out_shape=[SDS(a.shape, a.dtype) for a in placed] + _gather_rows_shapes(row_blocks),
        in_specs=[anyspec] * nw + [vmem] * nb, out_specs=[anyspec] * nw + [vmem] * nb,
        scratch_shapes=_gather_sems(nw) + _gather_rows_sems(nb), input_output_aliases={i: i for i in range(nw)},
        name=name)(*placed, *row_blocks)
    return res[:nw], res[nw:]


class _ChipExchange:
    def __init__(self, ins, outs, send_sems, recv_sems):
        self.ins, self.outs, self.send_sems, self.recv_sems = ins, outs, send_sems, recv_sems
        self.x, self.y, self.c = _position()
        self.j = 2 * self.x + self.y

    def _copies(self):
        for w in range(len(self.ins)):
            for r in (1, 2, 3):
                k = 3 * w + r - 1
                yield pltpu.make_async_remote_copy(
                    src_ref=self.ins[w].at[self.j ^ r], dst_ref=self.outs[w].at[r - 1],
                    send_sem=self.send_sems.at[k], recv_sem=self.recv_sems.at[k],
                    device_id=(*_chip_at(self.x, self.y, r), self.c), device_id_type=MESH)

    def start(self):
        for cp in self._copies():
            cp.start()

    def finish(self):
        for cp in self._copies():
            cp.wait()


def _exchange_sems(n_weights):
    return [pltpu.SemaphoreType.DMA((3 * n_weights,)), pltpu.SemaphoreType.DMA((3 * n_weights,))]


class _CoreExchange:
    def __init__(self, ins, outs, send_sems, recv_sems):
        self.ins, self.outs, self.send_sems, self.recv_sems = ins, outs, send_sems, recv_sems
        self.x, self.y, self.c = _position()

    def _copies(self):
        for w in range(len(self.ins)):
            yield pltpu.make_async_remote_copy(
                src_ref=self.ins[w].at[:, 1 - self.c], dst_ref=self.outs[w],
                send_sem=self.send_sems.at[w], recv_sem=self.recv_sems.at[w],
                device_id=(self.x, self.y, 1 - self.c), device_id_type=MESH)

    def start(self):
        for cp in self._copies():
            cp.start()

    def finish(self):
        for cp in self._copies():
            cp.wait()


def _core_exchange_shapes(grads):
    return [SDS((g.shape[0], g.shape[2], g.shape[3]), g.dtype) for g in grads]


def _core_exchange_sems(n):
    return [pltpu.SemaphoreType.DMA((n,)), pltpu.SemaphoreType.DMA((n,))]


def _exchange_core_halves(grads, name):
    nw = len(grads)

    def body(*refs):
        ex = _CoreExchange(refs[:nw], refs[nw:2 * nw], *refs[2 * nw:])
        ex.start()
        ex.finish()

    anyspec = pl.BlockSpec(memory_space=pl.ANY)
    return pl.pallas_call(
        body, out_shape=_core_exchange_shapes(grads), in_specs=[anyspec] * nw, out_specs=[anyspec] * nw,
        scratch_shapes=_core_exchange_sems(nw), name=name)(*grads)


def _add_core_halves(g4, recv, c_idx, rb, name):
    ns, _, rh, C = g4.shape

    def body(c_ref, g_ref, r_ref, o_ref):
        o_ref[...] = (g_ref[0] + r_ref[...].astype(F32)).astype(BF16)

    return pl.pallas_call(
        body,
        grid_spec=pltpu.PrefetchScalarGridSpec(
            num_scalar_prefetch=1, grid=(ns, rh // rb),
            in_specs=[pl.BlockSpec((1, 1, rb, C), lambda s, i, cr: (s, cr[0], i, 0)),
                      pl.BlockSpec((1, rb, C), lambda s, i, cr: (s, i, 0))],
            out_specs=pl.BlockSpec((1, rb, C), lambda s, i, cr: (s, i, 0))),
        out_shape=SDS((ns, rh, C), BF16), compiler_params=_arb(2), name=name)(c_idx, g4, recv)


def _add_core_halves_in(g4, recv, c_idx, name):
    n_slabs, _, rh, C = g4.shape
    cb = 256
    per_slab, per_chip, n_blocks = C // cb, DIN // N_CHIPS // cb, DIN // cb

    def stored(s, k):
        sb = (per_chip * s + k + 4 * DH // cb) % n_blocks
        return sb // per_slab, sb % per_slab

    def body(c_ref, g_ref, r_ref, o_ref):
        o_ref[...] = (g_ref[0] + r_ref[...].astype(F32)).astype(BF16)

    return pl.pallas_call(
        body,
        grid_spec=pltpu.PrefetchScalarGridSpec(
            num_scalar_prefetch=1, grid=(N_CHIPS, per_chip),
            in_specs=[pl.BlockSpec((1, 1, rh, cb), lambda s, k, cr: (stored(s, k)[0], cr[0], 0, stored(s, k)[1])),
                      pl.BlockSpec((1, rh, cb), lambda s, k, cr: (stored(s, k)[0], 0, stored(s, k)[1]))],
            out_specs=pl.BlockSpec((1, rh, cb), lambda s, k, cr: (s, 0, k))),
        out_shape=SDS((N_CHIPS, rh, DIN // N_CHIPS), BF16), compiler_params=_arb(2), name=name)(c_idx, g4, recv)


def _slot_shapes(sums):
    return [SDS((3,) + s.shape[1:], s.dtype) for s in sums]


def _add_chips(own, slots, order, rb, name):
    _, rh, C = slots.shape

    def body(o_ref, own_ref, a_ref, b_ref, c_ref, d_ref, out_ref):
        mine = own_ref[0].astype(F32)
        t = [jnp.where(o_ref[i] == 0, mine, r[0].astype(F32)) for i, r in enumerate((a_ref, b_ref, c_ref, d_ref))]
        out_ref[...] = ((t[0] + t[1]) + t[2]) + t[3]

    def spec(i):
        return pl.BlockSpec((1, rb, C), lambda t, o: (jnp.maximum(o[i], 1) - 1, t, 0))

    return pl.pallas_call(
        body,
        grid_spec=pltpu.PrefetchScalarGridSpec(
            num_scalar_prefetch=1, grid=(rh // rb,),
            in_specs=[pl.BlockSpec((1, rb, C), lambda t, o: (o[4], t, 0)), spec(0), spec(1), spec(2), spec(3)],
            out_specs=pl.BlockSpec((rb, C), lambda t, o: (t, 0))),
        out_shape=SDS((rh, C), F32), compiler_params=_arb(), name=name)(order, own, slots, slots, slots, slots)


def _share_halves(halves):
    nw = len(halves)

    def body(*refs):
        ins, outs = refs[:nw], refs[nw:2 * nw]
        send_sems, recv_sems = refs[2 * nw:]
        x, y, c = _position()
        started = []
        for w in range(nw):
            cp = pltpu.make_async_remote_copy(
                src_ref=ins[w], dst_ref=outs[w], send_sem=send_sems.at[w], recv_sem=recv_sems.at[w],
                device_id=(x, y, 1 - c), device_id_type=MESH)
            cp.start()
            started.append(cp)
        for cp in started:
            cp.wait()

    anyspec = pl.BlockSpec(memory_space=pl.ANY)
    return pl.pallas_call(
        body, out_shape=[SDS(h.shape, F32) for h in halves], in_specs=[anyspec] * nw, out_specs=[anyspec] * nw,
        scratch_shapes=[pltpu.SemaphoreType.DMA((nw,)), pltpu.SemaphoreType.DMA((nw,))],
        name="share_halves")(*halves)


def _small_2d(b_ada, norm1_w, norm2_w, final_norm_w, v_ln_w, v_ln_b, lower_bounds, gn_w, b_s, w_s):
    return dict(zip(SMALL_NAMES, (b_ada, norm1_w, norm2_w, final_norm_w.reshape(1, D), v_ln_w, v_ln_b, lower_bounds, gn_w,
                                  b_s.reshape(NH, BLK), w_s.reshape(NH * BLK, BLK))))


def _small_original_shapes(d):
    out = dict(d)
    out['final_norm_w'] = d['final_norm_w'].reshape(D)
    out['b_s'] = d['b_s'].reshape(1, NH, BLK)
    out['w_s'] = d['w_s'].reshape(1, NH, BLK, BLK)
    return out


def _row_block(r):
    for cand in (256, 176, 128, 64, 32, 16, 8):
        if r % cand == 0:
            return cand
    return r


def kernel(x, c, w_ada, b_ada, norm1_w, w_in, w_s, b_s, v_ln_w, v_ln_b, lower_bounds, gn_w, w_out, norm2_w, w_ffn_in, w_ffn_out, final_norm_w, loss_target, m_w_ada, m_b_ada, m_norm1_w, m_w_in, m_w_s, m_b_s, m_v_ln_w, m_v_ln_b, m_lower_bounds, m_gn_w, m_w_out, m_norm2_w, m_w_ffn_in, m_w_ffn_out, m_final_norm_w, v_w_ada, v_b_ada, v_norm1_w, v_w_in, v_w_s, v_b_s, v_v_ln_w, v_v_ln_b, v_lower_bounds, v_gn_w, v_w_out, v_norm2_w, v_w_ffn_in, v_w_ffn_out, v_final_norm_w):
    T = x.shape[1]
    tm, tp = min(TOKEN_TILE, T), min(PROJ_TILE, T)
    px, py, pc = _position()
    chip = 2 * px + py
    me = 4 * px + 2 * py + pc
    x2d = x.reshape(T, D)
    tgt = loss_target.reshape(T, D)

    chip_idx = jnp.reshape(chip, (1,)).astype(jnp.int32)
    c_idx = jnp.reshape(pc, (1,)).astype(jnp.int32)
    (w_in_b,), (c_all,) = _gather_weights(
        [_place_shard(w_in[0], 1, chip_idx, "place_in")], [1], [jnp.broadcast_to(c, (8, D))], "gather_w_in_and_c")
    placed = [_place_shard(w_out[0], 0, chip_idx, "place_out"), _place_shard(w_ffn_in[0], 1, chip_idx, "place_ffn_in"),
              _place_shard(w_ffn_out[0], 0, chip_idx, "place_ffn_out")]

    cact, ada_part = _ada_forward(c_all.reshape(N_DEV, 8, D)[:, 0, :], w_ada[0])
    n_ada = ada_part.shape[1]
    (ada_all,) = _all_gather_rows([ada_part], "gather_ada")
    ada_all = ada_all.reshape(N_CHIPS, 2, N_DEV, n_ada)[:, 0]
    ada = lax.dynamic_index_in_dim(ada_all, me, axis=1, keepdims=False).reshape(1, 6 * D) + b_ada

    rr = lax.broadcasted_iota(jnp.int32, (BLK, BLK), 0) // CH
    cc = lax.broadcasted_iota(jnp.int32, (BLK, BLK), 1) // CH
    ws_b = jnp.where((rr >= cc)[None], w_s[0], 0.0).astype(BF16)
    bst = b_s[0].T
    lnw, lnb = v_ln_w, v_ln_b
    nw1, nw2, fw = norm1_w, norm2_w, final_norm_w.reshape(1, D)

    tables = _hgrn_tables()
    (h1, proj, ycat, o_pre, a_all, st_all), (w_out_b, w_fi_b, w_fo_b) = _proj_hgrn_fwd(
        x2d, nw1, ada, w_in_b, lower_bounds, gn_w, tables, placed, [0, 1, 0])

    dycat, dx1, h2, act, dff, dgu, dmix, acc2, ycat = _token_local(
        x2d, ycat, tgt, ada, nw2, ada, ada, ada, fw, w_out_b, w_fi_b, w_fo_b, proj, ws_b, bst, lnw, lnb, tm)

    tt = min(WGRAD_TOKENS, T)
    order = jnp.concatenate([chip ^ jnp.arange(N_CHIPS, dtype=jnp.int32), chip_idx]).astype(jnp.int32)

    def by_core_half(g):
        return g.reshape(g.shape[0], 2, g.shape[1] // 2, g.shape[2])

    def core_sums(g4, recv, names):
        return [_add_core_halves(a, b, c_idx, _row_block(a.shape[2]), "add_core_" + n) for a, b, n in zip(g4, recv, names)]

    def chip_sums(sums, slots, names):
        return [_add_chips(o, s, order, _row_block(s.shape[1]), "add_chips_" + n) for o, s, n in zip(sums, slots, names)]

    g_out = _wgrad(ycat, dmix, D, D, tt, "wgrad_out").reshape(N_CHIPS, D // N_CHIPS, D)
    g_fo = _wgrad(act, dff, FFB, D, tt, "wgrad_ffn_out").reshape(N_CHIPS, DFF // N_CHIPS, D)
    late_names = ["out", "ffn_in", "ffn_out"]

    (dproj, dws, dbs, dln), (out_recv, fo_recv), (g_fi, g_fi_wire) = _gmlp_bwd_wgrad(
        proj, dycat, ws_b, bst, lnw, lnb, [by_core_half(g_out), by_core_half(g_fo)], h2, dgu, FFB, tt)
    (fi_recv,) = _exchange_core_halves([by_core_half(g_fi_wire)], "exchange_core_halves_ffn_in")
    late_g4 = [by_core_half(g) for g in (g_out, g_fi, g_fo)]
    late_sums = core_sums(late_g4, [out_recv, fi_recv, fo_recv], late_names)
    (dproj, dlb, dgn), late_slots = _hgrn_bwd(
        proj, o_pre, a_all, st_all, dycat, lower_bounds, gn_w, dproj, tables, late_sums)

    g_in, g_in_wire = _wgrad(h1, dproj, D, D, tt, "wgrad_in", bf16_copy=True)
    (in_recv,) = _exchange_core_halves([by_core_half(g_in_wire)], "exchange_core_halves_in")
    in_sums = [_add_core_halves_in(by_core_half(g_in), in_recv, c_idx, "add_core_in")]
    (grad_x, acc1), in_slots = _proj_in_bwd(dproj, x2d, dx1, nw1, ada, w_in_b, tp, in_sums)
    names = ["in"] + late_names
    halves = chip_sums(in_sums, in_slots, ["in"]) + chip_sums(late_sums, late_slots, late_names)
    sibling_halves = _share_halves(halves)

    big_w = [(w_in, m_w_in, v_w_in), (w_out, m_w_out, v_w_out), (w_ffn_in, m_w_ffn_in, v_w_ffn_in),
             (w_ffn_out, m_w_ffn_out, v_w_ffn_out)]
    big_out = []
    for mine, sib, (w, m, v), n in zip(halves, sibling_halves, big_w, names):
        res = _adamw_halves(w[0], mine, sib, m[0], v[0], c_idx, _row_block(mine.shape[0]), "adamw_" + n)
        big_out.append([r[None] for r in res])

    gathered = _all_gather_rows([acc1, acc2, dln, dlb, dgn, dbs, dws], "gather_small")
    small, loss, dada_all = _small_finalize(
        gathered,
        _small_2d(b_ada, norm1_w, norm2_w, final_norm_w, v_ln_w, v_ln_b, lower_bounds, gn_w, b_s, w_s),
        _small_2d(m_b_ada, m_norm1_w, m_norm2_w, m_final_norm_w, m_v_ln_w, m_v_ln_b, m_lower_bounds, m_gn_w, m_b_s, m_w_s),
        _small_2d(v_b_ada, v_norm1_w, v_norm2_w, v_final_norm_w, v_v_ln_w, v_v_ln_b, v_lower_bounds, v_gn_w, v_b_s, v_w_s))
    small = [_small_original_shapes(d) for d in small]
    loss = loss.reshape(())

    ada_out = [o[None] for o in _ada_wgrad_adam(cact.T, dada_all, w_ada[0], m_w_ada[0], v_w_ada[0], chip_idx)]

    order_names = ['w_ada', 'b_ada', 'norm1_w', 'w_in', 'w_s', 'b_s', 'v_ln_w', 'v_ln_b', 'lower_bounds', 'gn_w',
                   'w_out', 'norm2_w', 'w_ffn_in', 'w_ffn_out', 'final_norm_w']
    big_idx = {'w_in': 0, 'w_out': 1, 'w_ffn_in': 2, 'w_ffn_out': 3}
    outs = [loss, grad_x.reshape(1, T, D)]
    for kind in range(4):
        for n in order_names:
            if n == 'w_ada':
                outs.append(ada_out[kind])
            elif n in big_idx:
                outs.append(big_out[big_idx[n]][kind])
            else:
                outs.append(small[kind][n])
    return tuple(outs)
```

```python
import jax
import jax.numpy as jnp
import numpy as np
from jax import lax
from jax.experimental import pallas as pl
from jax.experimental.pallas import tpu as pltpu

F32 = jnp.float32
BF16 = jnp.bfloat16
SDS = jax.ShapeDtypeStruct
MESH = pl.DeviceIdType.MESH
HIGHEST = lax.Precision.HIGHEST

D = 1024
DG = 512
DH = 512
NH = 4
HD = 128
BLK = 128
CH = 64
DFF = 2816
DIN = 3072
FFB = 1408
LEVELS = (64, 32, 16, 8, 4, 2)
HGRN_CHUNKS_PER_STEP = 8
GMLP_ROWS_PER_STEP = 1024
TOKEN_TILE = 256
PROJ_TILE = 1024
WGRAD_TOKENS = 2048
N_CHIPS = 4
N_DEV = 8
EPS = 1e-6
LR, B1, B2, AEPS, WD, STEP = 0.001, 0.9, 0.999, 1e-08, 0.01, 10

NT = (((1,), (1,)), ((), ()))
TN = (((0,), (0,)), ((), ()))


def _full(shape):
    nd = len(shape)
    return pl.BlockSpec(shape, lambda *_: (0,) * nd)


ADA_SH1, ADA_SC1, ADA_G1, ADA_SH2, ADA_SC2, ADA_G2 = range(6)


def _ada_part(k):
    return pl.BlockSpec((1, D), lambda *_: (0, k))


def _resident(shape):
    nd = len(shape)
    return pl.BlockSpec(shape, lambda *_: (0,) * nd, pipeline_mode=pl.Buffered(1))


def _arb(n=1):
    return pltpu.CompilerParams(dimension_semantics=("arbitrary",) * n)


def _dot(a, b, dims=None, precision=None):
    if dims is None:
        return jnp.dot(a, b, preferred_element_type=F32, precision=precision)
    return lax.dot_general(a, b, dims, preferred_element_type=F32, precision=precision)


def _sigmoid(x):
    return jax.nn.sigmoid(x)


def _gelu_parts(x):
    cdf = 0.5 * (1.0 + lax.erf(x * 0.7071067811865476))
    pdf = jnp.exp(-0.5 * x * x) * 0.3989422804014327
    return x * cdf, cdf + x * pdf


def _rms(x):
    return lax.rsqrt(jnp.mean(x * x, axis=-1, keepdims=True) + EPS)


def _rms_bwd(xhat, r, gw):
    return r * (gw - xhat * jnp.mean(xhat * gw, axis=-1, keepdims=True))


def _lower_bound(lbp_ref):
    l0, l1 = lbp_ref[0:1, :], lbp_ref[1:2, :]
    m = jnp.maximum(l0, l1)
    e0, e1 = jnp.exp(l0 - m), jnp.exp(l1 - m)
    return e0 / (e0 + e1), e1 / (e0 + e1)


def _gmlp_common(u, v, lnw, lnb, ws_ref, bst_ref):
    ug, dug = _gelu_parts(u)
    vg, dvg = _gelu_parts(v)
    mu = jnp.mean(vg, axis=-1, keepdims=True)
    vc = vg - mu
    rstd = lax.rsqrt(jnp.mean(vc * vc, axis=-1, keepdims=True) + EPS)
    vhat = vc * rstd
    vn = vhat * lnw + lnb
    vnb = vn.astype(BF16)
    mixed = []
    for h in range(NH):
        sl = slice(h * HD, (h + 1) * HD)
        mixed.append(_dot(ws_ref[h], vnb[:, sl]) + bst_ref[:, h:h + 1])
    return ug, dug, dvg, rstd, vhat, vnb, jnp.concatenate(mixed, axis=1)


def _hgrn_tables():
    t = np.arange(CH)[:, None]
    j = np.arange(CH)[None, :]
    blocks = [j <= t, j > t]
    masks = []
    for n in LEVELS:
        mid = t - t % n + n // 2
        blocks.append(np.where(t >= mid, (j >= mid) & (j <= t), (j > t) & (j < mid)))
        masks.append((t // n == j // n) & (t % n >= n // 2) & (j % n < n // 2))
    w = np.concatenate(blocks, axis=0).astype(np.float32)
    m = np.stack(masks).astype(np.float32)
    return (jnp.asarray(w, BF16), jnp.asarray(w.T, BF16), jnp.asarray(m), jnp.asarray(m + m.transpose(0, 2, 1)))


def _split_dot(w, x, parts):
    acc = None
    for _ in range(parts):
        piece = x.astype(BF16)
        term = _dot(w, piece)
        acc = term if acc is None else acc + term
        x = x - piece.astype(F32)
    return acc


def _hgrn_decays(f, w_ref):
    b = _split_dot(w_ref[0:CH, :], jnp.log(f), 3)
    row = lax.broadcasted_iota(jnp.int32, (CH, 1), 0)
    blocks = [jnp.exp(b), jnp.exp(b[CH - 1:CH, :] - b)]
    for n in LEVELS:
        up = (row & (n // 2)) != 0
        if n >= 8:
            ref = b.reshape(CH // n, n, DH)[:, n // 2 - 1:n // 2, :]
            ref = jnp.broadcast_to(ref, (CH // n, n, DH)).reshape(CH, DH)
            blocks.append(jnp.exp(jnp.where(up, b - ref, ref - b)))
        elif n == 4:
            r4 = row & 3
            two = jnp.where(r4 == 3, pltpu.roll(f, 1, 0) * f, 1.0)
            blocks.append(jnp.where(r4 == 0, pltpu.roll(f, CH - 1, 0), jnp.where(r4 == 2, f, two)))
        else:
            blocks.append(jnp.where(up, f, 1.0))
    return blocks


def _hgrn_gates(q, fl, lb, omlb, w_ref):
    sq = _sigmoid(q)
    qf = q * sq
    sig = _sigmoid(fl)
    f = lb + omlb * sig
    k = 1.0 - f
    return sq, qf, sig, f, k, _hgrn_decays(f, w_ref)


def _level_factor(e, li, sl, row, qh, kh):
    el = e[2 + li][:, sl]
    up = (row & (LEVELS[li] // 2)) != 0
    return el, up, el * jnp.where(up, qh, kh)


def _proj_hgrn_fwd(x, nw1, ada, w_in_b, lower_bounds, gn_w, tables, placed, axes):
    T = x.shape[0]
    nc = T // CH
    nch = min(HGRN_CHUNKS_PER_STEP, nc)
    steps = nc // nch
    w_st, _, masks, _ = tables
    nw = len(placed)
    pass_step = (13 * steps) // 16
    q0 = 2 * DG

    def body(*refs):
        x_ref, nw_ref, sc_ref, sh_ref, win_ref, lbp_ref, gn_ref, w_ref, m_ref = refs[:9]
        h_ref, p_ref, y_ref, o_ref, a_ref, st_ref = refs[9 + nw:15 + nw]
        s_scr, send_sems, recv_sems = refs[15 + 2 * nw:]
        gather = _WeightGather(refs[15 + nw:15 + 2 * nw], axes, send_sems, recv_sems)
        step = pl.program_id(0)
        xv = x_ref[...]
        hb = (((xv * _rms(xv)) * nw_ref[...]) * (1.0 + sc_ref[...]) + sh_ref[...]).astype(BF16)
        h_ref[...] = hb
        p_ref[...] = _dot(hb, win_ref[...])

        @pl.when(step == 0)
        def _():
            gather.start()
            s_scr[...] = jnp.zeros_like(s_scr)

        @pl.when(step == pass_step)
        def _():
            gather.forward()

        lb, omlb = _lower_bound(lbp_ref)
        row = lax.broadcasted_iota(jnp.int32, (CH, 1), 0)
        eye = lax.broadcasted_iota(jnp.int32, (CH, CH), 0) == lax.broadcasted_iota(jnp.int32, (CH, CH), 1)
        in_level = [m_ref[li] > 0.0 for li in range(len(LEVELS))]
        pre = []
        for ci in range(nch):
            rs = slice(ci * CH, (ci + 1) * CH)
            _, qf, _, _, k, e = _hgrn_gates(p_ref[rs, q0:q0 + DH], p_ref[rs, q0 + DH:q0 + 2 * DH], lb, omlb, w_ref)
            mats = []
            for h in range(NH):
                sl = slice(h * HD, (h + 1) * HD)
                qh, kh = qf[:, sl], k[:, sl]
                a = jnp.where(eye, jnp.sum(qh * kh, axis=-1, keepdims=True), 0.0)
                for li in range(len(LEVELS)):
                    _, _, y = _level_factor(e, li, sl, row, qh, kh)
                    yb = y.astype(BF16)
                    a = jnp.where(in_level[li], _dot(yb, yb, NT), a)
                a_ref[ci, h] = a
                mats.append(a.astype(BF16))
            eb = e[0]
            pre.append(((qf * eb).astype(BF16), eb[CH - 1:CH, :], (k * e[1]).astype(BF16), mats))
        for ci in range(nch):
            rs = slice(ci * CH, (ci + 1) * CH)
            qe, ebl, kd, mats = pre[ci]
            v = p_ref[rs, q0 + 2 * DH:q0 + 3 * DH]
            g = p_ref[rs, q0 + 3 * DH:q0 + 4 * DH]
            for h in range(NH):
                sl = slice(h * HD, (h + 1) * HD)
                st0 = s_scr[h]
                st_ref[ci, h] = st0
                vb = v[:, sl].astype(BF16)
                o = _dot(qe[:, sl], st0.astype(BF16), NT) + _dot(mats[h], vb)
                s_scr[h] = st0 * ebl[:, sl] + _dot(vb, kd[:, sl], TN)
                o_ref[rs, sl] = o
                gh = g[:, sl]
                y_ref[rs, sl] = (((o * _rms(o)) * gn_ref[...]) * (gh * _sigmoid(gh))).astype(BF16)

        @pl.when(step == steps - 1)
        def _():
            gather.finish()

    rows = nch * CH
    row = lambda c: (c, 0)
    anyspec = pl.BlockSpec(memory_space=pl.ANY)
    res = pl.pallas_call(
        body, grid=(steps,),
        in_specs=[pl.BlockSpec((rows, D), row), _full((1, D)), _ada_part(ADA_SC1), _ada_part(ADA_SH1), _resident((D, DIN)),
                  _full((2, DH)), _full((1, HD)), _full(w_st.shape), _full(masks.shape)] + [anyspec] * nw,
        out_specs=[pl.BlockSpec((rows, D), row), pl.BlockSpec((rows, DIN), row),
                   pl.BlockSpec((rows, DH), lambda c: (c, 1)),
                   pl.BlockSpec((rows, DH), row),
                   pl.BlockSpec((nch, NH, CH, CH), lambda c: (c, 0, 0, 0)),
                   pl.BlockSpec((nch, NH, HD, HD), lambda c: (c, 0, 0, 0))] + [anyspec] * nw,
        out_shape=[SDS((T, D), BF16), SDS((T, DIN), F32), SDS((T, D), BF16), SDS((T, DH), F32),
                   SDS((nc, NH, CH, CH), F32), SDS((nc, NH, HD, HD), F32)] + [SDS(a.shape, a.dtype) for a in placed],
        scratch_shapes=[pltpu.VMEM((NH, HD, HD), F32)] + _gather_sems(nw),
        input_output_aliases={9 + i: 6 + i for i in range(nw)},
        compiler_params=_arb(), name="proj_hgrn_fwd")(x, nw1, ada, ada, w_in_b, lower_bounds, gn_w, w_st, masks, *placed)
    return res[:6], res[6:]


def _token_local(x, ycat, tgt, g1, nw2, sc2, sh2, g2, fw, w_out_b, w_fi_b, w_fo_b, proj, ws_b, bst, lnw, lnb, tm):
    T = x.shape[0]
    inv_d = 1.0 / D

    def body(x_ref, yb_ref, t_ref, g1_ref, nw2_ref, sc2_ref, sh2_ref, g2_ref, fw_ref, wo_ref, wfi_ref, wfo_ref,
             u_ref, v_ref, ws_ref, bst_ref, lnw_ref, lnb_ref,
             dy_ref, dx1_ref, h2_ref, act_ref, dff_ref, dgu_ref, dmix_ref, acc_ref, ya_ref):
        @pl.when(pl.program_id(0) == 0)
        def _():
            acc_ref[...] = jnp.zeros_like(acc_ref)

        def acc(row, val):
            acc_ref[row:row + 1, :] += jnp.sum(val, axis=0, keepdims=True)

        for bi in range(tm // BLK):
            rs = slice(bi * BLK, (bi + 1) * BLK)
            ug, _, _, _, _, _, mixed = _gmlp_common(u_ref[rs, :], v_ref[rs, :], lnw_ref[...], lnb_ref[...], ws_ref, bst_ref)
            ya_ref[rs, :] = (ug * mixed).astype(BF16)
        g1v, g2v = g1_ref[...], g2_ref[...]
        mix = _dot(ya_ref[...], wo_ref[0:DG, :]) + _dot(yb_ref[...], wo_ref[DG:D, :])
        x1 = x_ref[...] + g1v * mix
        r2 = _rms(x1)
        xh2 = x1 * r2
        n2 = xh2 * nw2_ref[...]
        osc2 = 1.0 + sc2_ref[...]
        h2b = (n2 * osc2 + sh2_ref[...]).astype(BF16)
        h2_ref[...] = h2b
        ff = jnp.zeros((tm, D), F32)
        saved = []
        for kb in range(DFF // FFB):
            gate = _dot(h2b, wfi_ref[:, kb * FFB:(kb + 1) * FFB])
            up = _dot(h2b, wfi_ref[:, DFF + kb * FFB:DFF + (kb + 1) * FFB])
            sg = _sigmoid(gate)
            actb = (gate * sg * up).astype(BF16)
            act_ref[:, kb * FFB:(kb + 1) * FFB] = actb
            ff = ff + _dot(actb, wfo_ref[kb * FFB:(kb + 1) * FFB, :])
            saved.append((gate, up, sg))
        x2 = x1 + g2v * ff
        r3 = _rms(x2)
        xh3 = x2 * r3
        err = xh3 * fw_ref[...] - t_ref[...]
        acc(6, (0.5 * inv_d) * err * err)
        dy = err * inv_d
        acc(4, dy * xh3)
        dx2 = _rms_bwd(xh3, r3, dy * fw_ref[...])
        acc(0, dx2 * ff)
        dffb = (dx2 * g2v).astype(BF16)
        dff_ref[...] = dffb
        dh2 = jnp.zeros((tm, D), F32)
        for kb in range(DFF // FFB):
            gate, up, sg = saved[kb]
            da = _dot(dffb, wfo_ref[kb * FFB:(kb + 1) * FFB, :], NT)
            dgate = (da * up * (sg * (1.0 + gate * (1.0 - sg)))).astype(BF16)
            dup = (da * gate * sg).astype(BF16)
            dgu_ref[:, kb * FFB:(kb + 1) * FFB] = dgate
            dgu_ref[:, DFF + kb * FFB:DFF + (kb + 1) * FFB] = dup
            dh2 = dh2 + _dot(dgate, wfi_ref[:, kb * FFB:(kb + 1) * FFB], NT)
            dh2 = dh2 + _dot(dup, wfi_ref[:, DFF + kb * FFB:DFF + (kb + 1) * FFB], NT)
        acc(2, dh2)
        acc(1, dh2 * n2)
        dn2 = dh2 * osc2
        acc(3, dn2 * xh2)
        dx1 = dx2 + _rms_bwd(xh2, r2, dn2 * nw2_ref[...])
        acc(5, dx1 * mix)
        dmixb = (dx1 * g1v).astype(BF16)
        dmix_ref[...] = dmixb
        dy_ref[...] = _dot(dmixb, wo_ref[...], NT)
        dx1_ref[...] = dx1

    row = lambda i: (i, 0)
    vec = _full((1, D))
    half = lambda j: pl.BlockSpec((tm, DG), lambda i: (i, j))
    return pl.pallas_call(
        body, grid=(T // tm,),
        in_specs=[pl.BlockSpec((tm, D), row), half(1), pl.BlockSpec((tm, D), row),
                  _ada_part(ADA_G1), vec, _ada_part(ADA_SC2), _ada_part(ADA_SH2), _ada_part(ADA_G2), vec,
                  _resident((D, D)), _resident((D, 2 * DFF)), _resident((DFF, D)),
                  half(0), half(1), _full((NH, BLK, BLK)), _full((BLK, NH)), _full((1, DG)), _full((1, DG))],
        out_specs=[pl.BlockSpec((tm, D), row), pl.BlockSpec((tm, D), row), pl.BlockSpec((tm, D), row),
                   pl.BlockSpec((tm, DFF), row), pl.BlockSpec((tm, D), row), pl.BlockSpec((tm, 2 * DFF), row),
                   pl.BlockSpec((tm, D), row), _full((8, D)), half(0)],
        out_shape=[SDS((T, D), F32), SDS((T, D), F32), SDS((T, D), BF16), SDS((T, DFF), BF16), SDS((T, D), BF16),
                   SDS((T, 2 * DFF), BF16), SDS((T, D), BF16), SDS((8, D), F32), SDS((T, D), BF16)],
        input_output_aliases={1: 8},
        compiler_params=_arb(), name="token_local")(x, ycat, tgt, g1, nw2, sc2, sh2, g2, fw, w_out_b, w_fi_b, w_fo_b,
                                                    proj, proj, ws_b, bst, lnw, lnb)


def _gmlp_bwd(proj, dycat, ws_b, bst, lnw, lnb, grads):
    T = proj.shape[0]
    rows = min(GMLP_ROWS_PER_STEP, T)
    nb = T // rows
    nw = len(grads)

    def body(*refs):
        u_ref, v_ref, dy_ref, ws_ref, bst_ref, lnw_ref, lnb_ref = refs[:7]
        dp_ref, dws_ref, dbs_ref, dln_ref = refs[7 + nw:11 + nw]
        dbs_acc, send_sems, recv_sems = refs[11 + 2 * nw:]
        exchange = _CoreExchange(refs[7:7 + nw], refs[11 + nw:11 + 2 * nw], send_sems, recv_sems)
        i = pl.program_id(0)

        @pl.when(i == 0)
        def _():
            exchange.start()
            dws_ref[...] = jnp.zeros_like(dws_ref)
            dln_ref[...] = jnp.zeros_like(dln_ref)
            dbs_acc[...] = jnp.zeros_like(dbs_acc)

        r = lax.broadcasted_iota(jnp.int32, (BLK, BLK), 0) // CH
        c = lax.broadcasted_iota(jnp.int32, (BLK, BLK), 1) // CH
        for bi in range(rows // BLK):
            rs = slice(bi * BLK, (bi + 1) * BLK)
            ug, dug, dvg, rstd, vhat, vnb, mixed = _gmlp_common(
                u_ref[rs, :], v_ref[rs, :], lnw_ref[...], lnb_ref[...], ws_ref, bst_ref)
            dya = dy_ref[rs, :]
            dp_ref[rs, 0:DG] = (dya * mixed * dug).astype(BF16)
            dmixed = dya * ug
            dbs_acc[...] += dmixed
            dmb = dmixed.astype(BF16)
            dvn = []
            for h in range(NH):
                sl = slice(h * HD, (h + 1) * HD)
                dws_ref[h * BLK:(h + 1) * BLK, :] += jnp.where(r >= c, _dot(dmb[:, sl], vnb[:, sl], NT), 0.0)
                dvn.append(_dot(ws_ref[h], dmb[:, sl], TN))
            dvn = jnp.concatenate(dvn, axis=1)
            dln_ref[0:1, :] += jnp.sum(dvn * vhat, axis=0, keepdims=True)
            dln_ref[1:2, :] += jnp.sum(dvn, axis=0, keepdims=True)
            dvh = dvn * lnw_ref[...]
            dvgel = rstd * (dvh - jnp.mean(dvh, axis=-1, keepdims=True) - vhat * jnp.mean(dvh * vhat, axis=-1, keepdims=True))
            dp_ref[rs, DG:2 * DG] = (dvgel * dvg).astype(BF16)

        @pl.when(i == nb - 1)
        def _():
            head = lax.broadcasted_iota(jnp.int32, (8, BLK), 0)
            ones = jnp.ones((8, HD), F32)
            out = jnp.zeros((8, BLK), F32)
            for h in range(NH):
                sums = _dot(ones, dbs_acc[:, h * HD:(h + 1) * HD], NT, precision=HIGHEST)
                out = out + jnp.where(head == h, sums, 0.0)
            dbs_ref[...] = out
            exchange.finish()

    anyspec = pl.BlockSpec(memory_space=pl.ANY)
    res = pl.pallas_call(
        body, grid=(nb,),
        in_specs=[pl.BlockSpec((rows, DG), lambda i: (i, 0)), pl.BlockSpec((rows, DG), lambda i: (i, 1)),
                  pl.BlockSpec((rows, DG), lambda i: (i, 0)),
                  _full((NH, BLK, BLK)), _full((BLK, NH)), _full((1, DG)), _full((1, DG))] + [anyspec] * nw,
        out_specs=[pl.BlockSpec((rows, 2 * DG), lambda i: (i, 2)), _full((NH * BLK, BLK)), _full((8, BLK)), _full((8, DG))]
        + [anyspec] * nw,
        out_shape=[SDS((T, DIN), BF16), SDS((NH * BLK, BLK), F32), SDS((8, BLK), F32), SDS((8, DG), F32)]
        + _core_exchange_shapes(grads),
        scratch_shapes=[pltpu.VMEM((BLK, DG), F32)] + _core_exchange_sems(nw),
        compiler_params=_arb(), name="gmlp_bwd")(proj, proj, dycat, ws_b, bst, lnw, lnb, *grads)
    return res[:4], res[4:]


def _hgrn_bwd(proj, o_pre, a_all, st_all, dycat, lower_bounds, gn_w, dproj, tables, sums, row_blocks):
    T = proj.shape[0]
    nc = T // CH
    nch = min(HGRN_CHUNKS_PER_STEP, nc)
    steps = nc // nch
    w_st, w_st_t, _, masks_sym = tables
    n_lev = len(LEVELS)
    nw, nr = len(sums), len(row_blocks)

    def body(*refs):
        q_ref, f_ref, i_ref, g_ref, o_ref, a_ref, st_ref, dy_ref, lbp_ref, gn_ref, w_ref, wt_ref, ms_ref = refs[:13]
        n_in = 14 + nw + nr
        dp_ref, dlb_ref, dgn_ref = refs[n_in:n_in + 3]
        ds_scr, dx_scr = refs[n_in + 3 + nw + nr:n_in + 5 + nw + nr]
        sems = refs[n_in + 5 + nw + nr:]
        exchange = _ChipExchange(refs[14:14 + nw], refs[n_in + 3:n_in + 3 + nw], *sems[:2])
        rows_gather = _RowGather(refs[14 + nw:n_in], refs[n_in + 3 + nw:n_in + 3 + nw + nr], *sems[2:])
        i = pl.program_id(0)

        @pl.when(i == 0)
        def _():
            rows_gather.start()
            exchange.start()
            ds_scr[...] = jnp.zeros_like(ds_scr)
            dlb_ref[...] = jnp.zeros_like(dlb_ref)
            dgn_ref[...] = jnp.zeros_like(dgn_ref)

        @pl.when(i == steps // 2)
        def _():
            rows_gather.forward()

        lb, omlb = _lower_bound(lbp_ref)
        row = lax.broadcasted_iota(jnp.int32, (CH, 1), 0)
        eye = lax.broadcasted_iota(jnp.int32, (CH, CH), 0) == lax.broadcasted_iota(jnp.int32, (CH, CH), 1)
        lower = lax.broadcasted_iota(jnp.int32, (CH, CH), 0) > lax.broadcasted_iota(jnp.int32, (CH, CH), 1)
        dgn = jnp.zeros((1, HD), F32)
        pre = []
        for ci in range(nch):
            rs = slice(ci * CH, (ci + 1) * CH)
            q = q_ref[rs, :]
            v = i_ref[rs, :]
            g = g_ref[rs, :]
            sq, qf, sig, f, k, e = _hgrn_gates(q, f_ref[rs, :], lb, omlb, w_ref)
            eb = e[0]
            ekd = e[1]
            kd = k * ekd
            qe = qf * eb
            dob_h, dqe_h, dqf_h, dki_h, dv_h, dg_h = [], [], [], [], [], []
            for h in range(NH):
                sl = slice(h * HD, (h + 1) * HD)
                o = o_ref[rs, sl]
                ro = _rms(o)
                oh = o * ro
                gh = g[:, sl]
                sg = _sigmoid(gh)
                dyb = dy_ref[rs, sl]
                dg_h.append(dyb * (oh * gn_ref[...]) * (sg * (1.0 + gh * (1.0 - sg))))
                don = dyb * (gh * sg)
                dgn = dgn + jnp.sum(don * oh, axis=0, keepdims=True)
                dob = _rms_bwd(oh, ro, don * gn_ref[...]).astype(BF16)
                vb = v[:, sl].astype(BF16)
                qh, kh = qf[:, sl], k[:, sl]
                dqe = _dot(dob, st_ref[ci, h].astype(BF16))
                da = _dot(dob, vb, NT)
                ddiag = jnp.sum(jnp.where(eye, da, 0.0), axis=-1, keepdims=True)
                dsym = jnp.where(lower, da, _dot(vb, dob, NT))
                upper_part = jnp.zeros((CH, HD), F32)
                both = jnp.zeros((CH, HD), F32)
                for li in range(n_lev):
                    el, up, y = _level_factor(e, li, sl, row, qh, kh)
                    dyv = _dot((ms_ref[li] * dsym).astype(BF16), y.astype(BF16))
                    dx_scr[ci, (2 + li) * CH:(3 + li) * CH, sl] = dyv * y
                    dye = dyv * el
                    upper_part = upper_part + jnp.where(up, dye, 0.0)
                    both = both + dye
                dob_h.append(dob)
                dqe_h.append(dqe)
                dqf_h.append(dqe * eb[:, sl] + ddiag * kh + upper_part)
                dki_h.append(ddiag * qh + (both - upper_part))
                dv_h.append(_dot(a_ref[ci, h].astype(BF16), dob, TN))
            dp_ref[rs, 0:DH] = (jnp.concatenate(dqf_h, axis=1) * (sq * (1.0 + q * (1.0 - sq)))).astype(BF16)
            dp_ref[rs, 3 * DH:4 * DH] = jnp.concatenate(dg_h, axis=1).astype(BF16)
            pre.append((v, sig, f, eb, ekd, kd, qe, dob_h, jnp.concatenate(dqe_h, axis=1), dki_h, dv_h))
        dgn_ref[0:1, :] += dgn
        for ci in reversed(range(nch)):
            rs = slice(ci * CH, (ci + 1) * CH)
            v, sig, f, eb, ekd, kd, qe, dob_h, dqe, dki_h, dv_h = pre[ci]
            ebl = eb[CH - 1:CH, :]
            dbl_h, dkd_h, dv2_h = [], [], []
            for h in range(NH):
                sl = slice(h * HD, (h + 1) * HD)
                dst1 = ds_scr[h]
                dst1b = dst1.astype(BF16)
                ds_scr[h] = dst1 * ebl[:, sl] + _dot(dob_h[h], qe[:, sl].astype(BF16), TN)
                dbl_h.append(ebl[:, sl] * jnp.sum(st_ref[ci, h] * dst1, axis=0, keepdims=True))
                dkd_h.append(_dot(v[:, sl].astype(BF16), dst1b))
                dv2_h.append(dv_h[h] + _dot(kd[:, sl].astype(BF16), dst1b, NT))
            dkd = jnp.concatenate(dkd_h, axis=1)
            dx_scr[ci, 0:CH, :] = dqe * qe + jnp.where(row == CH - 1, jnp.concatenate(dbl_h, axis=1), 0.0)
            dx_scr[ci, CH:2 * CH, :] = dkd * kd
            dlf = _split_dot(wt_ref[...], dx_scr[ci], 2)
            df = dlf / f - (dkd * ekd + jnp.concatenate(dki_h, axis=1))
            dlb_ref[0:1, :] += jnp.sum(df * (1.0 - sig), axis=0, keepdims=True)
            dp_ref[rs, DH:2 * DH] = (df * omlb * sig * (1.0 - sig)).astype(BF16)
            dp_ref[rs, 2 * DH:3 * DH] = jnp.concatenate(dv2_h, axis=1).astype(BF16)

        @pl.when(i == steps - 1)
        def _():
            gl = dlb_ref[0:1, :] * lb * omlb
            dlb_ref[0:1, :] = gl
            dlb_ref[1:2, :] = -gl
            exchange.finish()
            rows_gather.finish()

    rev = lambda j: pl.BlockSpec((nch * CH, DH), lambda c: (steps - 1 - c, j))
    anyspec = pl.BlockSpec(memory_space=pl.ANY)
    res = pl.pallas_call(
        body, grid=(steps,),
        in_specs=[rev(2), rev(3), rev(4), rev(5), rev(0),
                  pl.BlockSpec((nch, NH, CH, CH), lambda c: (steps - 1 - c, 0, 0, 0)),
                  pl.BlockSpec((nch, NH, HD, HD), lambda c: (steps - 1 - c, 0, 0, 0)),
                  rev(1), _full((2, DH)), _full((1, HD)),
                  _full(w_st.shape), _full(w_st_t.shape), _full(masks_sym.shape),
                  anyspec] + [anyspec] * (nw + nr),
        out_specs=[pl.BlockSpec((nch * CH, 4 * DH), lambda c: (steps - 1 - c, 0)), _full((8, DH)), _full((8, HD))]
        + [anyspec] * (nw + nr),
        out_shape=[SDS((T, DIN), BF16), SDS((8, DH), F32), SDS((8, HD), F32)] + _slot_shapes(sums)
        + _gather_rows_shapes(row_blocks),
        scratch_shapes=[pltpu.VMEM((NH, HD, HD), F32), pltpu.VMEM((nch, (2 + n_lev) * CH, DH), F32)]
        + _exchange_sems(nw) + _gather_rows_sems(nr),
        input_output_aliases={13: 0},
        compiler_params=_arb(), name="hgrn_bwd")(proj, proj, proj, proj, o_pre, a_all, st_all, dycat, lower_bounds, gn_w,
                                                 w_st, w_st_t, masks_sym, dproj, *sums, *row_blocks)
    return res[:3], res[3:3 + nw], res[3 + nw:]


def _proj_in_bwd(dproj, x, dx1, nw, sc, w_in_b, tm, sums):
    T = x.shape[0]
    ns = len(sums)
    steps = T // tm

    def body(*refs):
        dp_ref, x_ref, dx1_ref, nw_ref, sc_ref, w_ref = refs[:6]
        gx_ref, acc_ref = refs[6 + ns:8 + ns]
        exchange = _ChipExchange(refs[6:6 + ns], refs[8 + ns:8 + 2 * ns], *refs[8 + 2 * ns:])

        @pl.when(pl.program_id(0) == 0)
        def _():
            exchange.start()
            acc_ref[...] = jnp.zeros_like(acc_ref)

        dh = _dot(dp_ref[:, 0:4 * DH], w_ref[:, 2 * DG:DIN], NT) + _dot(dp_ref[:, 4 * DH:DIN], w_ref[:, 0:2 * DG], NT)
        xv = x_ref[...]
        r = _rms(xv)
        xh = xv * r
        n1 = xh * nw_ref[...]
        acc_ref[0:1, :] += jnp.sum(dh, axis=0, keepdims=True)
        acc_ref[1:2, :] += jnp.sum(dh * n1, axis=0, keepdims=True)
        dn = dh * (1.0 + sc_ref[...])
        acc_ref[2:3, :] += jnp.sum(dn * xh, axis=0, keepdims=True)
        gx_ref[...] = dx1_ref[...] + _rms_bwd(xh, r, dn * nw_ref[...])

        @pl.when(pl.program_id(0) == steps - 1)
        def _():
            exchange.finish()

    row = lambda i: (i, 0)
    anyspec = pl.BlockSpec(memory_space=pl.ANY)
    res = pl.pallas_call(
        body, grid=(steps,),
        in_specs=[pl.BlockSpec((tm, DIN), row), pl.BlockSpec((tm, D), row), pl.BlockSpec((tm, D), row),
                  _full((1, D)), _ada_part(ADA_SC1), _resident((D, DIN))] + [anyspec] * ns,
        out_specs=[pl.BlockSpec((tm, D), row), _full((8, D))] + [anyspec] * ns,
        out_shape=[SDS((T, D), F32), SDS((8, D), F32)] + _slot_shapes(sums),
        scratch_shapes=_exchange_sems(ns),
        compiler_params=_arb(), name="proj_in_bwd")(dproj, x, dx1, nw, sc, w_in_b, *sums)
    return res[:2], res[2:]


def _wgrad(a, b, bk, bn, tt, name, bf16_copy=False):
    T, K = a.shape
    N = b.shape[1]
    nn, nk, nt = N // bn, K // bk, T // tt
    bmap = lambda n, k, t: (t, n)

    def body(a_ref, b_ref, o_ref, *copy_ref):
        @pl.when(pl.program_id(2) == 0)
        def _():
            o_ref[...] = jnp.zeros_like(o_ref)

        o_ref[0] += _dot(a_ref[...], b_ref[...], TN)

        if bf16_copy:
            @pl.when(pl.program_id(2) == nt - 1)
            def _():
                copy_ref[0][...] = o_ref[...].astype(BF16)

    ospec = pl.BlockSpec((1, bk, bn), lambda n, k, t: (n, k, 0))
    return pl.pallas_call(
        body, grid=(nn, nk, nt),
        in_specs=[pl.BlockSpec((tt, bk), lambda n, k, t: (t, k)), pl.BlockSpec((tt, bn), bmap)],
        out_specs=[ospec, ospec] if bf16_copy else ospec,
        out_shape=[SDS((nn, K, bn), F32), SDS((nn, K, bn), BF16)] if bf16_copy else SDS((nn, K, bn), F32),
        compiler_params=_arb(3), name=name)(a, b)


def _adam_math(w, g, m, v):
    m = B1 * m + (1.0 - B1) * g
    v = B2 * v + (1.0 - B2) * (g * g)
    m_hat = m / (1.0 - B1 ** STEP)
    v_hat = v / (1.0 - B2 ** STEP)
    return -LR * (m_hat / (jnp.sqrt(v_hat) + AEPS) + WD * w), m, v


def _adamw_halves(w, mine, sibling, m, v, c_idx, rb, name):
    R, C = w.shape
    nb = (R // 2) // rb

    def body(c_ref, w_ref, a_ref, b_ref, m_ref, v_ref, g_out, d_out, m_out, v_out):
        g = jnp.where(pl.program_id(0) == c_ref[0], a_ref[...], b_ref[...])
        g_out[...] = g
        d_out[...], m_out[...], v_out[...] = _adam_math(w_ref[...], g, m_ref[...], v_ref[...])

    whole = pl.BlockSpec((rb, C), lambda hh, i, cr: (hh * nb + i, 0))
    half = pl.BlockSpec((rb, C), lambda hh, i, cr: (i, 0))
    return pl.pallas_call(
        body,
        grid_spec=pltpu.PrefetchScalarGridSpec(
            num_scalar_prefetch=1, grid=(2, nb), in_specs=[whole, half, half, whole, whole], out_specs=[whole] * 4),
        out_shape=[SDS((R, C), F32)] * 4, compiler_params=_arb(2), name=name)(c_idx, w, mine, sibling, m, v)


def _ada_forward(c_all, w_ada):
    n = w_ada.shape[1]

    def body(c_ref, w_ref, ca_ref, p_ref):
        cv = c_ref[...]
        ca = cv * _sigmoid(cv)
        ca_ref[...] = ca
        p_ref[...] = _dot(ca, w_ref[...], precision=HIGHEST)

    return pl.pallas_call(
        body, grid=(n // 512,),
        in_specs=[_full((N_DEV, D)), pl.BlockSpec((D, 512), lambda i: (0, i))],
        out_specs=[_full((N_DEV, D)), pl.BlockSpec((N_DEV, 512), lambda i: (0, i))],
        out_shape=[SDS((N_DEV, D), F32), SDS((N_DEV, n), F32)],
        compiler_params=_arb(), name="ada_forward")(c_all, w_ada)


def _ada_wgrad_adam(cact_t, dada_all, w, m, v, chip_idx):
    R, C = w.shape
    rb = 256

    def body(j_ref, c_ref, d_ref, w_ref, m_ref, v_ref, g_out, d_out, m_out, v_out):
        g = _dot(c_ref[...], d_ref[...], precision=HIGHEST)
        g_out[...] = g
        d_out[...], m_out[...], v_out[...] = _adam_math(w_ref[...], g, m_ref[...], v_ref[...])

    spec = pl.BlockSpec((rb, C), lambda i, j: (i, 0))
    return pl.pallas_call(
        body,
        grid_spec=pltpu.PrefetchScalarGridSpec(
            num_scalar_prefetch=1, grid=(R // rb,),
            in_specs=[pl.BlockSpec((rb, N_DEV), lambda i, j: (i, 0)), pl.BlockSpec((N_DEV, C), lambda i, j: (0, j[0])),
                      spec, spec, spec],
            out_specs=[spec] * 4),
        out_shape=[SDS((R, C), F32)] * 4,
        compiler_params=_arb(), name="ada_wgrad_adam")(chip_idx, cact_t, dada_all, w, m, v)


SMALL_NAMES = ('b_ada', 'norm1_w', 'norm2_w', 'final_norm_w', 'v_ln_w', 'v_ln_b', 'lower_bounds', 'gn_w', 'b_s', 'w_s')


def _small_finalize(gathered, params, moms, vels):
    n_in = len(gathered)

    def body(*refs):
        acc1, acc2, dln, dlb, dgn, dbs, dws = refs[:n_in]
        prm = [dict(zip(SMALL_NAMES, refs[n_in + k * 10:n_in + (k + 1) * 10])) for k in range(3)]
        outs = [dict(zip(SMALL_NAMES, refs[n_in + 30 + k * 10:n_in + 30 + (k + 1) * 10])) for k in range(4)]
        loss_ref, dada_ref = refs[n_in + 70:n_in + 72]

        def dev_sum(ref, first, n):
            per = ref.shape[0] // N_DEV
            g = ref[first:first + n, :]
            for dev in range(1, N_DEV):
                g = g + ref[dev * per + first:dev * per + first + n, :]
            return g

        def update(n, g, cols=slice(None)):
            outs[0][n][:, cols] = g
            outs[1][n][:, cols], outs[2][n][:, cols], outs[3][n][:, cols] = _adam_math(
                prm[0][n][:, cols], g, prm[1][n][:, cols], prm[2][n][:, cols])

        ada_rows = ((acc1, 0), (acc1, 1), (acc2, 5), (acc2, 2), (acc2, 1), (acc2, 0))
        for k, (ref, r) in enumerate(ada_rows):
            update('b_ada', dev_sum(ref, r, 1), slice(k * D, (k + 1) * D))
            for dev in range(N_DEV):
                dada_ref[dev:dev + 1, k * D:(k + 1) * D] = ref[8 * dev + r:8 * dev + r + 1, :]
        update('norm1_w', dev_sum(acc1, 2, 1))
        update('norm2_w', dev_sum(acc2, 3, 1))
        update('final_norm_w', dev_sum(acc2, 4, 1))
        update('v_ln_w', dev_sum(dln, 0, 1))
        update('v_ln_b', dev_sum(dln, 1, 1))
        update('lower_bounds', dev_sum(dlb, 0, 2))
        update('gn_w', dev_sum(dgn, 0, 1))
        update('b_s', dev_sum(dbs, 0, NH))
        update('w_s', dev_sum(dws, 0, NH * BLK))
        loss_ref[...] = jnp.sum(dev_sum(acc2, 6, 1), axis=-1, keepdims=True)

    shapes = [SDS(params[n].shape, F32) for n in SMALL_NAMES]
    res = pl.pallas_call(
        body, out_shape=shapes * 4 + [SDS((1, 1), F32), SDS((N_DEV, 6 * D), F32)], name="small_finalize")(
            *gathered, *[d[n] for d in (params, moms, vels) for n in SMALL_NAMES])
    return [dict(zip(SMALL_NAMES, res[k * 10:(k + 1) * 10])) for k in range(4)], res[40], res[41]


def _position():
    x, y, c = lax.axis_index("x"), lax.axis_index("y"), lax.axis_index("c")
    return x, y, c


def _chip_at(x, y, r):
    return (x ^ (r >> 1), y ^ (r & 1))


class _RowGather:
    def __init__(self, ins, outs, send_sems, recv_sems, local_sems):
        self.ins, self.outs = ins, outs
        self.send_sems, self.recv_sems, self.local_sems = send_sems, recv_sems, local_sems
        self.x, self.y, self.c = _position()
        self.me, self.sibling = (self.x, self.y, self.c), (self.x, self.y, 1 - self.c)
        self.chips = [_chip_at(self.x, self.y, r) for r in (1, 2, 3)]

    def _rows(self, b, px, py, pc):
        m_per = self.ins[b].shape[0]
        return self.outs[b].at[pl.ds((4 * px + 2 * py + pc) * m_per, m_per), :]

    def _copy(self, b, k, blk, to, from_input=False):
        return pltpu.make_async_remote_copy(
            src_ref=self.ins[b] if from_input else self._rows(b, *blk), dst_ref=self._rows(b, *blk),
            send_sem=self.send_sems.at[7 * b + k], recv_sem=self.recv_sems.at[7 * b + k],
            device_id=to, device_id_type=MESH)

    def _local(self, b):
        return pltpu.make_async_copy(self.ins[b], self._rows(b, *self.me), self.local_sems.at[b])

    def _first(self, b):
        c = self.c
        return [self._copy(b, 0, self.me, self.sibling, from_input=True)] + [
            self._copy(b, 1 + j, self.me, (*chip, c), from_input=True) for j, chip in enumerate(self.chips)]

    def start(self):
        for b in range(len(self.ins)):
            self._local(b).start()
            for cp in self._first(b):
                cp.start()

    def forward(self):
        for b in range(len(self.ins)):
            for j, chip in enumerate(self.chips):
                self._copy(b, 1 + j, (*chip, self.c), self.me).wait_recv()
                self._copy(b, 4 + j, (*chip, self.c), self.sibling).start()

    def finish(self):
        for b in range(len(self.ins)):
            self._copy(b, 0, self.sibling, self.me).wait_recv()
            for j, chip in enumerate(self.chips):
                self._copy(b, 4 + j, (*chip, 1 - self.c), self.me).wait_recv()
        for b in range(len(self.ins)):
            for cp in self._first(b):
                cp.wait_send()
            for j, chip in enumerate(self.chips):
                self._copy(b, 4 + j, (*chip, self.c), self.sibling).wait_send()
            self._local(b).wait()


def _gather_rows(ins, outs, send_sems, recv_sems, local_sems, after_issue=None):
    g = _RowGather(ins, outs, send_sems, recv_sems, local_sems)
    g.start()
    if after_issue is not None:
        after_issue()
    g.forward()
    g.finish()


def _gather_rows_shapes(blocks):
    return [SDS((N_DEV * b.shape[0], b.shape[1]), b.dtype) for b in blocks]


def _gather_rows_sems(nb):
    return [pltpu.SemaphoreType.DMA((7 * nb,)), pltpu.SemaphoreType.DMA((7 * nb,)), pltpu.SemaphoreType.DMA((nb,))]


def _all_gather_rows(blocks, name):
    nb = len(blocks)

    def body(*refs):
        _gather_rows(refs[:nb], refs[nb:2 * nb], *refs[2 * nb:])

    vmem = pl.BlockSpec(memory_space=pltpu.VMEM)
    return pl.pallas_call(
        body, out_shape=_gather_rows_shapes(blocks), in_specs=[vmem] * nb, out_specs=[vmem] * nb,
        scratch_shapes=_gather_rows_sems(nb), name=name)(*blocks)


def _place_shard(w_shard, axis, chip_idx, name):
    R, C = w_shard.shape
    rb = _row_block(R)
    nb = R // rb
    full = (R * N_CHIPS, C) if axis == 0 else (R, C * N_CHIPS)
    omap = (lambda i, j: (j[0] * nb + i, 0)) if axis == 0 else (lambda i, j: (i, j[0]))

    def body(j_ref, w_ref, o_ref):
        o_ref[...] = w_ref[...].astype(BF16)

    return pl.pallas_call(
        body,
        grid_spec=pltpu.PrefetchScalarGridSpec(
            num_scalar_prefetch=1, grid=(nb,), in_specs=[pl.BlockSpec((rb, C), lambda i, j: (i, 0))],
            out_specs=pl.BlockSpec((rb, C), omap)),
        out_shape=SDS(full, BF16), compiler_params=_arb(), name=name)(chip_idx, w_shard)


class _WeightGather:
    def __init__(self, refs, axes, send_sems, recv_sems):
        self.refs, self.axes, self.send_sems, self.recv_sems = refs, axes, send_sems, recv_sems
        self.x, self.y, self.c = _position()
        self.j = 2 * self.x + self.y
        self.n = 3 * len(refs)

    def _half(self, w, chip_idx, half):
        ref, axis = self.refs[w], self.axes[w]
        if axis == 0:
            size = ref.shape[0] // N_CHIPS
            return ref.at[pl.ds(chip_idx * size + half * (size // 2), size // 2), :]
        size = ref.shape[1] // N_CHIPS
        rows = ref.shape[0] // 2
        return ref.at[pl.ds(half * rows, rows), pl.ds(chip_idx * size, size)]

    def _ici(self, w, r, chip_idx):
        k = 3 * w + r - 1
        piece = self._half(w, chip_idx, self.c)
        return pltpu.make_async_remote_copy(
            src_ref=piece, dst_ref=piece, send_sem=self.send_sems.at[k], recv_sem=self.recv_sems.at[k],
            device_id=(*_chip_at(self.x, self.y, r), self.c), device_id_type=MESH)

    def _d2d(self, w, r, half):
        k = self.n + 3 * w + r - 1
        piece = self._half(w, self.j ^ r, half)
        return pltpu.make_async_remote_copy(
            src_ref=piece, dst_ref=piece, send_sem=self.send_sems.at[k], recv_sem=self.recv_sems.at[k],
            device_id=(self.x, self.y, 1 - self.c), device_id_type=MESH)

    def _each(self):
        return [(w, r) for w in range(len(self.refs)) for r in (1, 2, 3)]

    def start(self):
        for w, r in self._each():
            self._ici(w, r, self.j).start()

    def forward(self):
        for w, r in self._each():
            self._ici(w, r, self.j ^ r).wait_recv()
            self._d2d(w, r, self.c).start()

    def finish(self):
        for w, r in self._each():
            self._ici(w, r, self.j).wait_send()
            self._d2d(w, r, self.c).wait_send()
            self._d2d(w, r, 1 - self.c).wait_recv()


def _gather_sems(n_weights):
    return [pltpu.SemaphoreType.DMA((6 * n_weights,)), pltpu.SemaphoreType.DMA((6 * n_weights,))]


def _gather_weights(placed, axes, row_blocks, name):
    nw, nb = len(placed), len(row_blocks)

    def body(*refs):
        w_outs, b_ins, b_outs = refs[nw + nb:2 * nw + nb], refs[nw:nw + nb], refs[2 * nw + nb:2 * (nw + nb)]
        sems = refs[2 * (nw + nb):]
        g = _WeightGather(w_outs, axes, *sems[:2])
        _gather_rows(b_ins, b_outs, *sems[2:], after_issue=g.start)
        g.forward()
        g.finish()

    anyspec = pl.BlockSpec(memory_space=pl.ANY)
    vmem = pl.BlockSpec(memory_space=pltpu.VMEM)
    res = pl.pallas_call(
        body, out_shape=[SDS(a.shape, a.dtype) for a in placed] + _gather_rows_shapes(row_blocks),
        in_specs=[anyspec] * nw + [vmem] * nb, out_specs=[anyspec] * nw + [vmem] * nb,
        scratch_shapes=_gather_sems(nw) + _gather_rows_sems(nb), input_output_aliases={i: i for i in range(nw)},
        name=name)(*placed, *row_blocks)
    return res[:nw], res[nw:]


class _ChipExchange:
    def __init__(self, ins, outs, send_sems, recv_sems):
        self.ins, self.outs, self.send_sems, self.recv_sems = ins, outs, send_sems, recv_sems
        self.x, self.y, self.c = _position()
        self.j = 2 * self.x + self.y

    def _copies(self):
        for w in range(len(self.ins)):
            for r in (1, 2, 3):
                k = 3 * w + r - 1
                yield pltpu.make_async_remote_copy(
                    src_ref=self.ins[w].at[self.j ^ r], dst_ref=self.outs[w].at[r - 1],
                    send_sem=self.send_sems.at[k], recv_sem=self.recv_sems.at[k],
                    device_id=(*_chip_at(self.x, self.y, r), self.c), device_id_type=MESH)

    def start(self):
        for cp in self._copies():
            cp.start()

    def finish(self):
        for cp in self._copies():
            cp.wait()


def _exchange_sems(n_weights):
    return [pltpu.SemaphoreType.DMA((3 * n_weights,)), pltpu.SemaphoreType.DMA((3 * n_weights,))]


class _CoreExchange:
    def __init__(self, ins, outs, send_sems, recv_sems):
        self.ins, self.outs, self.send_sems, self.recv_sems = ins, outs, send_sems, recv_sems
        self.x, self.y, self.c = _position()

    def _copies(self):
        for w in range(len(self.ins)):
            yield pltpu.make_async_remote_copy(
                src_ref=self.ins[w].at[:, 1 - self.c], dst_ref=self.outs[w],
                send_sem=self.send_sems.at[w], recv_sem=self.recv_sems.at[w],
                device_id=(self.x, self.y, 1 - self.c), device_id_type=MESH)

    def start(self):
        for cp in self._copies():
            cp.start()

    def finish(self):
        for cp in self._copies():
            cp.wait()


def _core_exchange_shapes(grads):
    return [SDS((g.shape[0], g.shape[2], g.shape[3]), g.dtype) for g in grads]


def _core_exchange_sems(n):
    return [pltpu.SemaphoreType.DMA((n,)), pltpu.SemaphoreType.DMA((n,))]


def _exchange_core_halves(grads, name):
    nw = len(grads)

    def body(*refs):
        ex = _CoreExchange(refs[:nw], refs[nw:2 * nw], *refs[2 * nw:])
        ex.start()
        ex.finish()

    anyspec = pl.BlockSpec(memory_space=pl.ANY)
    return pl.pallas_call(
        body, out_shape=_core_exchange_shapes(grads), in_specs=[anyspec] * nw, out_specs=[anyspec] * nw,
        scratch_shapes=_core_exchange_sems(nw), name=name)(*grads)


def _add_core_halves(g4, recv, c_idx, rb, name):
    ns, _, rh, C = g4.shape

    def body(c_ref, g_ref, r_ref, o_ref):
        o_ref[...] = (g_ref[0] + r_ref[...]).astype(BF16)

    return pl.pallas_call(
        body,
        grid_spec=pltpu.PrefetchScalarGridSpec(
            num_scalar_prefetch=1, grid=(ns, rh // rb),
            in_specs=[pl.BlockSpec((1, 1, rb, C), lambda s, i, cr: (s, cr[0], i, 0)),
                      pl.BlockSpec((1, rb, C), lambda s, i, cr: (s, i, 0))],
            out_specs=pl.BlockSpec((1, rb, C), lambda s, i, cr: (s, i, 0))),
        out_shape=SDS((ns, rh, C), BF16), compiler_params=_arb(2), name=name)(c_idx, g4, recv)


def _add_core_halves_in(g4, recv, c_idx, name):
    n_slabs, _, rh, C = g4.shape
    cb = 256
    per_slab, per_chip, n_blocks = C // cb, DIN // N_CHIPS // cb, DIN // cb

    def stored(s, k):
        sb = (per_chip * s + k + 4 * DH // cb) % n_blocks
        return sb // per_slab, sb % per_slab

    def body(c_ref, g_ref, r_ref, o_ref):
        o_ref[...] = (g_ref[0] + r_ref[...].astype(F32)).astype(BF16)

    return pl.pallas_call(
        body,
        grid_spec=pltpu.PrefetchScalarGridSpec(
            num_scalar_prefetch=1, grid=(N_CHIPS, per_chip),
            in_specs=[pl.BlockSpec((1, 1, rh, cb), lambda s, k, cr: (stored(s, k)[0], cr[0], 0, stored(s, k)[1])),
                      pl.BlockSpec((1, rh, cb), lambda s, k, cr: (stored(s, k)[0], 0, stored(s, k)[1]))],
            out_specs=pl.BlockSpec((1, rh, cb), lambda s, k, cr: (s, 0, k))),
        out_shape=SDS((N_CHIPS, rh, DIN // N_CHIPS), BF16), compiler_params=_arb(2), name=name)(c_idx, g4, recv)


def _slot_shapes(sums):
    return [SDS((3,) + s.shape[1:], s.dtype) for s in sums]


def _add_chips(own, slots, order, rb, name):
    _, rh, C = slots.shape

    def body(o_ref, own_ref, a_ref, b_ref, c_ref, d_ref, out_ref):
        mine = own_ref[0].astype(F32)
        t = [jnp.where(o_ref[i] == 0, mine, r[0].astype(F32)) for i, r in enumerate((a_ref, b_ref, c_ref, d_ref))]
        out_ref[...] = ((t[0] + t[1]) + t[2]) + t[3]

    def spec(i):
        return pl.BlockSpec((1, rb, C), lambda t, o: (jnp.maximum(o[i], 1) - 1, t, 0))

    return pl.pallas_call(
        body,
        grid_spec=pltpu.PrefetchScalarGridSpec(
            num_scalar_prefetch=1, grid=(rh // rb,),
            in_specs=[pl.BlockSpec((1, rb, C), lambda t, o: (o[4], t, 0)), spec(0), spec(1), spec(2), spec(3)],
            out_specs=pl.BlockSpec((rb, C), lambda t, o: (t, 0))),
        out_shape=SDS((rh, C), F32), compiler_params=_arb(), name=name)(order, own, slots, slots, slots, slots)


def _share_halves(halves):
    nw = len(halves)

    def body(*refs):
        ins, outs = refs[:nw], refs[nw:2 * nw]
        send_sems, recv_sems = refs[2 * nw:]
        x, y, c = _position()
        started = []
        for w in range(nw):
            cp = pltpu.make_async_remote_copy(
                src_ref=ins[w], dst_ref=outs[w], send_sem=send_sems.at[w], recv_sem=recv_sems.at[w],
                device_id=(x, y, 1 - c), device_id_type=MESH)
            cp.start()
            started.append(cp)
        for cp in started:
            cp.wait()

    anyspec = pl.BlockSpec(memory_space=pl.ANY)
    return pl.pallas_call(
        body, out_shape=[SDS(h.shape, F32) for h in halves], in_specs=[anyspec] * nw, out_specs=[anyspec] * nw,
        scratch_shapes=[pltpu.SemaphoreType.DMA((nw,)), pltpu.SemaphoreType.DMA((nw,))],
        name="share_halves")(*halves)


def _small_2d(b_ada, norm1_w, norm2_w, final_norm_w, v_ln_w, v_ln_b, lower_bounds, gn_w, b_s, w_s):
    return dict(zip(SMALL_NAMES, (b_ada, norm1_w, norm2_w, final_norm_w.reshape(1, D), v_ln_w, v_ln_b, lower_bounds, gn_w,
                                  b_s.reshape(NH, BLK), w_s.reshape(NH * BLK, BLK))))


def _small_original_shapes(d):
    out = dict(d)
    out['final_norm_w'] = d['final_norm_w'].reshape(D)
    out['b_s'] = d['b_s'].reshape(1, NH, BLK)
    out['w_s'] = d['w_s'].reshape(1, NH, BLK, BLK)
    return out


def _row_block(r):
    for cand in (256, 176, 128, 64, 32, 16, 8):
        if r % cand == 0:
            return cand
    return r


def kernel(x, c, w_ada, b_ada, norm1_w, w_in, w_s, b_s, v_ln_w, v_ln_b, lower_bounds, gn_w, w_out, norm2_w, w_ffn_in, w_ffn_out, final_norm_w, loss_target, m_w_ada, m_b_ada, m_norm1_w, m_w_in, m_w_s, m_b_s, m_v_ln_w, m_v_ln_b, m_lower_bounds, m_gn_w, m_w_out, m_norm2_w, m_w_ffn_in, m_w_ffn_out, m_final_norm_w, v_w_ada, v_b_ada, v_norm1_w, v_w_in, v_w_s, v_b_s, v_v_ln_w, v_v_ln_b, v_lower_bounds, v_gn_w, v_w_out, v_norm2_w, v_w_ffn_in, v_w_ffn_out, v_final_norm_w):
    T = x.shape[1]
    tm, tp = min(TOKEN_TILE, T), min(PROJ_TILE, T)
    px, py, pc = _position()
    chip = 2 * px + py
    me = 4 * px + 2 * py + pc
    x2d = x.reshape(T, D)
    tgt = loss_target.reshape(T, D)

    chip_idx = jnp.reshape(chip, (1,)).astype(jnp.int32)
    c_idx = jnp.reshape(pc, (1,)).astype(jnp.int32)
    (w_in_b,), (c_all,) = _gather_weights(
        [_place_shard(w_in[0], 1, chip_idx, "place_in")], [1], [jnp.broadcast_to(c, (8, D))], "gather_w_in_and_c")
    placed = [_place_shard(w_out[0], 0, chip_idx, "place_out"), _place_shard(w_ffn_in[0], 1, chip_idx, "place_ffn_in"),
              _place_shard(w_ffn_out[0], 0, chip_idx, "place_ffn_out")]

    cact, ada_part = _ada_forward(c_all.reshape(N_DEV, 8, D)[:, 0, :], w_ada[0])
    n_ada = ada_part.shape[1]
    (ada_all,) = _all_gather_rows([ada_part], "gather_ada")
    ada_all = ada_all.reshape(N_CHIPS, 2, N_DEV, n_ada)[:, 0]
    ada = lax.dynamic_index_in_dim(ada_all, me, axis=1, keepdims=False).reshape(1, 6 * D) + b_ada

    rr = lax.broadcasted_iota(jnp.int32, (BLK, BLK), 0) // CH
    cc = lax.broadcasted_iota(jnp.int32, (BLK, BLK), 1) // CH
    ws_b = jnp.where((rr >= cc)[None], w_s[0], 0.0).astype(BF16)
    bst = b_s[0].T
    lnw, lnb = v_ln_w, v_ln_b
    nw1, nw2, fw = norm1_w, norm2_w, final_norm_w.reshape(1, D)

    tables = _hgrn_tables()
    (h1, proj, ycat, o_pre, a_all, st_all), (w_out_b, w_fi_b, w_fo_b) = _proj_hgrn_fwd(
        x2d, nw1, ada, w_in_b, lower_bounds, gn_w, tables, placed, [0, 1, 0])

    dycat, dx1, h2, act, dff, dgu, dmix, acc2, ycat = _token_local(
        x2d, ycat, tgt, ada, nw2, ada, ada, ada, fw, w_out_b, w_fi_b, w_fo_b, proj, ws_b, bst, lnw, lnb, tm)

    tt = min(WGRAD_TOKENS, T)
    order = jnp.concatenate([chip ^ jnp.arange(N_CHIPS, dtype=jnp.int32), chip_idx]).astype(jnp.int32)

    def by_core_half(g):
        return g.reshape(g.shape[0], 2, g.shape[1] // 2, g.shape[2])

    def core_sums(g4, recv, names):
        return [_add_core_halves(a, b, c_idx, _row_block(a.shape[2]), "add_core_" + n) for a, b, n in zip(g4, recv, names)]

    def chip_sums(sums, slots, names):
        return [_add_chips(o, s, order, _row_block(s.shape[1]), "add_chips_" + n) for o, s, n in zip(sums, slots, names)]

    g_out = _wgrad(ycat, dmix, D, D, tt, "wgrad_out").reshape(N_CHIPS, D // N_CHIPS, D)
    g_fi = _wgrad(h2, dgu, D, FFB, tt, "wgrad_ffn_in")
    g_fo = _wgrad(act, dff, FFB, D, tt, "wgrad_ffn_out").reshape(N_CHIPS, DFF // N_CHIPS, D)
    late_names = ["out", "ffn_in", "ffn_out"]
    late_g4 = [by_core_half(g) for g in (g_out, g_fi, g_fo)]

    (dproj, dws, dbs, dln), late_recv = _gmlp_bwd(proj, dycat, ws_b, bst, lnw, lnb, late_g4)
    late_sums = core_sums(late_g4, late_recv, late_names)
    (dproj, dlb, dgn), late_slots, (acc2_all, dln_all, dbs_all, dws_all) = _hgrn_bwd(
        proj, o_pre, a_all, st_all, dycat, lower_bounds, gn_w, dproj, tables, late_sums, [acc2, dln, dbs, dws])

    g_in, g_in_wire = _wgrad(h1, dproj, D, D, tt, "wgrad_in", bf16_copy=True)
    (in_recv,) = _exchange_core_halves([by_core_half(g_in_wire)], "exchange_core_halves_in")
    in_sums = [_add_core_halves_in(by_core_half(g_in), in_recv, c_idx, "add_core_in")]
    (grad_x, acc1), in_slots = _proj_in_bwd(dproj, x2d, dx1, nw1, ada, w_in_b, tp, in_sums)
    names = ["in"] + late_names
    halves = chip_sums(in_sums, in_slots, ["in"]) + chip_sums(late_sums, late_slots, late_names)
    sibling_halves = _share_halves(halves)

    big_w = [(w_in, m_w_in, v_w_in), (w_out, m_w_out, v_w_out), (w_ffn_in, m_w_ffn_in, v_w_ffn_in),
             (w_ffn_out, m_w_ffn_out, v_w_ffn_out)]
    big_out = []
    for mine, sib, (w, m, v), n in zip(halves, sibling_halves, big_w, names):
        res = _adamw_halves(w[0], mine, sib, m[0], v[0], c_idx, _row_block(mine.shape[0]), "adamw_" + n)
        big_out.append([r[None] for r in res])

    acc1_all, dlb_all, dgn_all = _all_gather_rows([acc1, dlb, dgn], "gather_small")
    gathered = [acc1_all, acc2_all, dln_all, dlb_all, dgn_all, dbs_all, dws_all]
    small, loss, dada_all = _small_finalize(
        gathered,
        _small_2d(b_ada, norm1_w, norm2_w, final_norm_w, v_ln_w, v_ln_b, lower_bounds, gn_w, b_s, w_s),
        _small_2d(m_b_ada, m_norm1_w, m_norm2_w, m_final_norm_w, m_v_ln_w, m_v_ln_b, m_lower_bounds, m_gn_w, m_b_s, m_w_s),
        _small_2d(v_b_ada, v_norm1_w, v_norm2_w, v_final_norm_w, v_v_ln_w, v_v_ln_b, v_lower_bounds, v_gn_w, v_b_s, v_w_s))
    small = [_small_original_shapes(d) for d in small]
    loss = loss.reshape(())

    ada_out = [o[None] for o in _ada_wgrad_adam(cact.T, dada_all, w_ada[0], m_w_ada[0], v_w_ada[0], chip_idx)]

    order_names = ['w_ada', 'b_ada', 'norm1_w', 'w_in', 'w_s', 'b_s', 'v_ln_w', 'v_ln_b', 'lower_bounds', 'gn_w',
                   'w_out', 'norm2_w', 'w_ffn_in', 'w_ffn_out', 'final_norm_w']
    big_idx = {'w_in': 0, 'w_out': 1, 'w_ffn_in': 2, 'w_ffn_out': 3}
    outs = [loss, grad_x.reshape(1, T, D)]
    for kind in range(4):
        for n in order_names:
            if n == 'w_ada':
                outs.append(ada_out[kind])
            elif n in big_idx:
                outs.append(big_out[big_idx[n]][kind])
            else:
                outs.append(small[kind][n])
    return tuple(outs)
```

```python
import jax
import jax.numpy as jnp
import numpy as np
from jax import lax
from jax.experimental import pallas as pl
from jax.experimental.pallas import tpu as pltpu

F32 = jnp.float32
BF16 = jnp.bfloat16
SDS = jax.ShapeDtypeStruct
MESH = pl.DeviceIdType.MESH
HIGHEST = lax.Precision.HIGHEST

D = 1024
DG = 512
DH = 512
NH = 4
HD = 128
BLK = 128
CH = 64
DFF = 2816
DIN = 3072
FFB = 1408
LEVELS = (64, 32, 16, 8, 4, 2)
HGRN_CHUNKS_PER_STEP = 8
GMLP_ROWS_PER_STEP = 1024
TOKEN_TILE = 256
PROJ_TILE = 1024
WGRAD_TOKENS = 2048
N_CHIPS = 4
N_DEV = 8
EPS = 1e-6
LR, B1, B2, AEPS, WD, STEP = 0.001, 0.9, 0.999, 1e-08, 0.01, 10

NT = (((1,), (1,)), ((), ()))
TN = (((0,), (0,)), ((), ()))


def _full(shape):
    nd = len(shape)
    return pl.BlockSpec(shape, lambda *_: (0,) * nd)


ADA_SH1, ADA_SC1, ADA_G1, ADA_SH2, ADA_SC2, ADA_G2 = range(6)


def _ada_part(k):
    return pl.BlockSpec((1, D), lambda *_: (0, k))


def _resident(shape):
    nd = len(shape)
    return pl.BlockSpec(shape, lambda *_: (0,) * nd, pipeline_mode=pl.Buffered(1))


def _arb(n=1):
    return pltpu.CompilerParams(dimension_semantics=("arbitrary",) * n)


def _dot(a, b, dims=None, precision=None):
    if dims is None:
        return jnp.dot(a, b, preferred_element_type=F32, precision=precision)
    return lax.dot_general(a, b, dims, preferred_element_type=F32, precision=precision)


def _sigmoid(x):
    return jax.nn.sigmoid(x)


def _gelu_parts(x):
    cdf = 0.5 * (1.0 + lax.erf(x * 0.7071067811865476))
    pdf = jnp.exp(-0.5 * x * x) * 0.3989422804014327
    return x * cdf, cdf + x * pdf


def _rms(x):
    return lax.rsqrt(jnp.mean(x * x, axis=-1, keepdims=True) + EPS)


def _rms_bwd(xhat, r, gw):
    return r * (gw - xhat * jnp.mean(xhat * gw, axis=-1, keepdims=True))


def _lower_bound(lbp_ref):
    l0, l1 = lbp_ref[0:1, :], lbp_ref[1:2, :]
    m = jnp.maximum(l0, l1)
    e0, e1 = jnp.exp(l0 - m), jnp.exp(l1 - m)
    return e0 / (e0 + e1), e1 / (e0 + e1)


def _gmlp_common(u, v, lnw, lnb, ws_ref, bst_ref):
    ug, dug = _gelu_parts(u)
    vg, dvg = _gelu_parts(v)
    mu = jnp.mean(vg, axis=-1, keepdims=True)
    vc = vg - mu
    rstd = lax.rsqrt(jnp.mean(vc * vc, axis=-1, keepdims=True) + EPS)
    vhat = vc * rstd
    vn = vhat * lnw + lnb
    vnb = vn.astype(BF16)
    mixed = []
    for h in range(NH):
        sl = slice(h * HD, (h + 1) * HD)
        mixed.append(_dot(ws_ref[h], vnb[:, sl]) + bst_ref[:, h:h + 1])
    return ug, dug, dvg, rstd, vhat, vnb, jnp.concatenate(mixed, axis=1)


def _hgrn_tables():
    t = np.arange(CH)[:, None]
    j = np.arange(CH)[None, :]
    blocks = [j <= t, j > t]
    masks = []
    for n in LEVELS:
        mid = t - t % n + n // 2
        blocks.append(np.where(t >= mid, (j >= mid) & (j <= t), (j > t) & (j < mid)))
        masks.append((t // n == j // n) & (t % n >= n // 2) & (j % n < n // 2))
    w = np.concatenate(blocks, axis=0).astype(np.float32)
    m = np.stack(masks).astype(np.float32)
    return (jnp.asarray(w, BF16), jnp.asarray(w.T, BF16), jnp.asarray(m), jnp.asarray(m + m.transpose(0, 2, 1)))


def _split_dot(w, x, parts):
    acc = None
    for _ in range(parts):
        piece = x.astype(BF16)
        term = _dot(w, piece)
        acc = term if acc is None else acc + term
        x = x - piece.astype(F32)
    return acc


def _hgrn_decays(f, w_ref):
    b = _split_dot(w_ref[0:CH, :], jnp.log(f), 3)
    row = lax.broadcasted_iota(jnp.int32, (CH, 1), 0)
    blocks = [jnp.exp(b), jnp.exp(b[CH - 1:CH, :] - b)]
    for n in LEVELS:
        up = (row & (n // 2)) != 0
        if n >= 8:
            ref = b.reshape(CH // n, n, DH)[:, n // 2 - 1:n // 2, :]
            ref = jnp.broadcast_to(ref, (CH // n, n, DH)).reshape(CH, DH)
            blocks.append(jnp.exp(jnp.where(up, b - ref, ref - b)))
        elif n == 4:
            r4 = row & 3
            two = jnp.where(r4 == 3, pltpu.roll(f, 1, 0) * f, 1.0)
            blocks.append(jnp.where(r4 == 0, pltpu.roll(f, CH - 1, 0), jnp.where(r4 == 2, f, two)))
        else:
            blocks.append(jnp.where(up, f, 1.0))
    return blocks


def _hgrn_gates(q, fl, lb, omlb, w_ref):
    sq = _sigmoid(q)
    qf = q * sq
    sig = _sigmoid(fl)
    f = lb + omlb * sig
    k = 1.0 - f
    return sq, qf, sig, f, k, _hgrn_decays(f, w_ref)


def _level_factor(e, li, sl, row, qh, kh):
    el = e[2 + li][:, sl]
    up = (row & (LEVELS[li] // 2)) != 0
    return el, up, el * jnp.where(up, qh, kh)


def _proj_hgrn_fwd(x, nw1, ada, w_in_b, lower_bounds, gn_w, tables, placed, axes):
    T = x.shape[0]
    nc = T // CH
    nch = min(HGRN_CHUNKS_PER_STEP, nc)
    steps = nc // nch
    w_st, _, masks, _ = tables
    nw = len(placed)
    pass_step = (13 * steps) // 16
    q0 = 2 * DG

    def body(*refs):
        x_ref, nw_ref, sc_ref, sh_ref, win_ref, lbp_ref, gn_ref, w_ref, m_ref = refs[:9]
        h_ref, p_ref, y_ref, o_ref, a_ref, st_ref = refs[9 + nw:15 + nw]
        s_scr, send_sems, recv_sems = refs[15 + 2 * nw:]
        gather = _WeightGather(refs[15 + nw:15 + 2 * nw], axes, send_sems, recv_sems)
        step = pl.program_id(0)
        xv = x_ref[...]
        hb = (((xv * _rms(xv)) * nw_ref[...]) * (1.0 + sc_ref[...]) + sh_ref[...]).astype(BF16)
        h_ref[...] = hb
        p_ref[...] = _dot(hb, win_ref[...])

        @pl.when(step == 0)
        def _():
            gather.start()
            s_scr[...] = jnp.zeros_like(s_scr)

        @pl.when(step == pass_step)
        def _():
            gather.forward()

        lb, omlb = _lower_bound(lbp_ref)
        row = lax.broadcasted_iota(jnp.int32, (CH, 1), 0)
        eye = lax.broadcasted_iota(jnp.int32, (CH, CH), 0) == lax.broadcasted_iota(jnp.int32, (CH, CH), 1)
        in_level = [m_ref[li] > 0.0 for li in range(len(LEVELS))]
        pre = []
        for ci in range(nch):
            rs = slice(ci * CH, (ci + 1) * CH)
            _, qf, _, _, k, e = _hgrn_gates(p_ref[rs, q0:q0 + DH], p_ref[rs, q0 + DH:q0 + 2 * DH], lb, omlb, w_ref)
            mats = []
            for h in range(NH):
                sl = slice(h * HD, (h + 1) * HD)
                qh, kh = qf[:, sl], k[:, sl]
                a = jnp.where(eye, jnp.sum(qh * kh, axis=-1, keepdims=True), 0.0)
                for li in range(len(LEVELS)):
                    _, _, y = _level_factor(e, li, sl, row, qh, kh)
                    yb = y.astype(BF16)
                    a = jnp.where(in_level[li], _dot(yb, yb, NT), a)
                a_ref[ci, h] = a
                mats.append(a.astype(BF16))
            eb = e[0]
            pre.append(((qf * eb).astype(BF16), eb[CH - 1:CH, :], (k * e[1]).astype(BF16), mats))
        for ci in range(nch):
            rs = slice(ci * CH, (ci + 1) * CH)
            qe, ebl, kd, mats = pre[ci]
            v = p_ref[rs, q0 + 2 * DH:q0 + 3 * DH]
            g = p_ref[rs, q0 + 3 * DH:q0 + 4 * DH]
            for h in range(NH):
                sl = slice(h * HD, (h + 1) * HD)
                st0 = s_scr[h]
                st_ref[ci, h] = st0
                vb = v[:, sl].astype(BF16)
                o = _dot(qe[:, sl], st0.astype(BF16), NT) + _dot(mats[h], vb)
                s_scr[h] = st0 * ebl[:, sl] + _dot(vb, kd[:, sl], TN)
                o_ref[rs, sl] = o
                gh = g[:, sl]
                y_ref[rs, sl] = (((o * _rms(o)) * gn_ref[...]) * (gh * _sigmoid(gh))).astype(BF16)

        @pl.when(step == steps - 1)
        def _():
            gather.finish()

    rows = nch * CH
    row = lambda c: (c, 0)
    anyspec = pl.BlockSpec(memory_space=pl.ANY)
    res = pl.pallas_call(
        body, grid=(steps,),
        in_specs=[pl.BlockSpec((rows, D), row), _full((1, D)), _ada_part(ADA_SC1), _ada_part(ADA_SH1), _resident((D, DIN)),
                  _full((2, DH)), _full((1, HD)), _full(w_st.shape), _full(masks.shape)] + [anyspec] * nw,
        out_specs=[pl.BlockSpec((rows, D), row), pl.BlockSpec((rows, DIN), row),
                   pl.BlockSpec((rows, DH), lambda c: (c, 1)),
                   pl.BlockSpec((rows, DH), row),
                   pl.BlockSpec((nch, NH, CH, CH), lambda c: (c, 0, 0, 0)),
                   pl.BlockSpec((nch, NH, HD, HD), lambda c: (c, 0, 0, 0))] + [anyspec] * nw,
        out_shape=[SDS((T, D), BF16), SDS((T, DIN), F32), SDS((T, D), BF16), SDS((T, DH), F32),
                   SDS((nc, NH, CH, CH), F32), SDS((nc, NH, HD, HD), F32)] + [SDS(a.shape, a.dtype) for a in placed],
        scratch_shapes=[pltpu.VMEM((NH, HD, HD), F32)] + _gather_sems(nw),
        input_output_aliases={9 + i: 6 + i for i in range(nw)},
        compiler_params=_arb(), name="proj_hgrn_fwd")(x, nw1, ada, ada, w_in_b, lower_bounds, gn_w, w_st, masks, *placed)
    return res[:6], res[6:]


def _token_local(x, ycat, tgt, g1, nw2, sc2, sh2, g2, fw, w_out_b, w_fi_b, w_fo_b, proj, ws_b, bst, lnw, lnb, tm):
    T = x.shape[0]
    inv_d = 1.0 / D

    def body(x_ref, yb_ref, t_ref, g1_ref, nw2_ref, sc2_ref, sh2_ref, g2_ref, fw_ref, wo_ref, wfi_ref, wfo_ref,
             u_ref, v_ref, ws_ref, bst_ref, lnw_ref, lnb_ref,
             dy_ref, dx1_ref, h2_ref, act_ref, dff_ref, dgu_ref, dmix_ref, acc_ref, ya_ref):
        @pl.when(pl.program_id(0) == 0)
        def _():
            acc_ref[...] = jnp.zeros_like(acc_ref)

        def acc(row, val):
            acc_ref[row:row + 1, :] += jnp.sum(val, axis=0, keepdims=True)

        for bi in range(tm // BLK):
            rs = slice(bi * BLK, (bi + 1) * BLK)
            ug, _, _, _, _, _, mixed = _gmlp_common(u_ref[rs, :], v_ref[rs, :], lnw_ref[...], lnb_ref[...], ws_ref, bst_ref)
            ya_ref[rs, :] = (ug * mixed).astype(BF16)
        g1v, g2v = g1_ref[...], g2_ref[...]
        mix = _dot(ya_ref[...], wo_ref[0:DG, :]) + _dot(yb_ref[...], wo_ref[DG:D, :])
        x1 = x_ref[...] + g1v * mix
        r2 = _rms(x1)
        xh2 = x1 * r2
        n2 = xh2 * nw2_ref[...]
        osc2 = 1.0 + sc2_ref[...]
        h2b = (n2 * osc2 + sh2_ref[...]).astype(BF16)
        h2_ref[...] = h2b
        ff = jnp.zeros((tm, D), F32)
        saved = []
        for kb in range(DFF // FFB):
            gate = _dot(h2b, wfi_ref[:, kb * FFB:(kb + 1) * FFB])
            up = _dot(h2b, wfi_ref[:, DFF + kb * FFB:DFF + (kb + 1) * FFB])
            sg = _sigmoid(gate)
            actb = (gate * sg * up).astype(BF16)
            act_ref[:, kb * FFB:(kb + 1) * FFB] = actb
            ff = ff + _dot(actb, wfo_ref[kb * FFB:(kb + 1) * FFB, :])
            saved.append((gate, up, sg))
        x2 = x1 + g2v * ff
        r3 = _rms(x2)
        xh3 = x2 * r3
        err = xh3 * fw_ref[...] - t_ref[...]
        acc(6, (0.5 * inv_d) * err * err)
        dy = err * inv_d
        acc(4, dy * xh3)
        dx2 = _rms_bwd(xh3, r3, dy * fw_ref[...])
        acc(0, dx2 * ff)
        dffb = (dx2 * g2v).astype(BF16)
        dff_ref[...] = dffb
        dh2 = jnp.zeros((tm, D), F32)
        for kb in range(DFF // FFB):
            gate, up, sg = saved[kb]
            da = _dot(dffb, wfo_ref[kb * FFB:(kb + 1) * FFB, :], NT)
            dgate = (da * up * (sg * (1.0 + gate * (1.0 - sg)))).astype(BF16)
            dup = (da * gate * sg).astype(BF16)
            dgu_ref[:, kb * FFB:(kb + 1) * FFB] = dgate
            dgu_ref[:, DFF + kb * FFB:DFF + (kb + 1) * FFB] = dup
            dh2 = dh2 + _dot(dgate, wfi_ref[:, kb * FFB:(kb + 1) * FFB], NT)
            dh2 = dh2 + _dot(dup, wfi_ref[:, DFF + kb * FFB:DFF + (kb + 1) * FFB], NT)
        acc(2, dh2)
        acc(1, dh2 * n2)
        dn2 = dh2 * osc2
        acc(3, dn2 * xh2)
        dx1 = dx2 + _rms_bwd(xh2, r2, dn2 * nw2_ref[...])
        acc(5, dx1 * mix)
        dmixb = (dx1 * g1v).astype(BF16)
        dmix_ref[...] = dmixb
        dy_ref[...] = _dot(dmixb, wo_ref[...], NT)
        dx1_ref[...] = dx1

    row = lambda i: (i, 0)
    vec = _full((1, D))
    half = lambda j: pl.BlockSpec((tm, DG), lambda i: (i, j))
    return pl.pallas_call(
        body, grid=(T // tm,),
        in_specs=[pl.BlockSpec((tm, D), row), half(1), pl.BlockSpec((tm, D), row),
                  _ada_part(ADA_G1), vec, _ada_part(ADA_SC2), _ada_part(ADA_SH2), _ada_part(ADA_G2), vec,
                  _resident((D, D)), _resident((D, 2 * DFF)), _resident((DFF, D)),
                  half(0), half(1), _full((NH, BLK, BLK)), _full((BLK, NH)), _full((1, DG)), _full((1, DG))],
        out_specs=[pl.BlockSpec((tm, D), row), pl.BlockSpec((tm, D), row), pl.BlockSpec((tm, D), row),
                   pl.BlockSpec((tm, DFF), row), pl.BlockSpec((tm, D), row), pl.BlockSpec((tm, 2 * DFF), row),
                   pl.BlockSpec((tm, D), row), _full((8, D)), half(0)],
        out_shape=[SDS((T, D), F32), SDS((T, D), F32), SDS((T, D), BF16), SDS((T, DFF), BF16), SDS((T, D), BF16),
                   SDS((T, 2 * DFF), BF16), SDS((T, D), BF16), SDS((8, D), F32), SDS((T, D), BF16)],
        input_output_aliases={1: 8},
        compiler_params=_arb(), name="token_local")(x, ycat, tgt, g1, nw2, sc2, sh2, g2, fw, w_out_b, w_fi_b, w_fo_b,
                                                    proj, proj, ws_b, bst, lnw, lnb)


def _gmlp_bwd(proj, dycat, ws_b, bst, lnw, lnb, grads):
    T = proj.shape[0]
    rows = min(GMLP_ROWS_PER_STEP, T)
    nb = T // rows
    nw = len(grads)

    def body(*refs):
        u_ref, v_ref, dy_ref, ws_ref, bst_ref, lnw_ref, lnb_ref = refs[:7]
        dp_ref, dws_ref, dbs_ref, dln_ref = refs[7 + nw:11 + nw]
        dbs_acc, send_sems, recv_sems = refs[11 + 2 * nw:]
        exchange = _CoreExchange(refs[7:7 + nw], refs[11 + nw:11 + 2 * nw], send_sems, recv_sems)
        i = pl.program_id(0)

        @pl.when(i == 0)
        def _():
            exchange.start()
            dws_ref[...] = jnp.zeros_like(dws_ref)
            dln_ref[...] = jnp.zeros_like(dln_ref)
            dbs_acc[...] = jnp.zeros_like(dbs_acc)

        r = lax.broadcasted_iota(jnp.int32, (BLK, BLK), 0) // CH
        c = lax.broadcasted_iota(jnp.int32, (BLK, BLK), 1) // CH
        for bi in range(rows // BLK):
            rs = slice(bi * BLK, (bi + 1) * BLK)
            ug, dug, dvg, rstd, vhat, vnb, mixed = _gmlp_common(
                u_ref[rs, :], v_ref[rs, :], lnw_ref[...], lnb_ref[...], ws_ref, bst_ref)
            dya = dy_ref[rs, :]
            dp_ref[rs, 0:DG] = (dya * mixed * dug).astype(BF16)
            dmixed = dya * ug
            dbs_acc[...] += dmixed
            dmb = dmixed.astype(BF16)
            dvn = []
            for h in range(NH):
                sl = slice(h * HD, (h + 1) * HD)
                dws_ref[h * BLK:(h + 1) * BLK, :] += jnp.where(r >= c, _dot(dmb[:, sl], vnb[:, sl], NT), 0.0)
                dvn.append(_dot(ws_ref[h], dmb[:, sl], TN))
            dvn = jnp.concatenate(dvn, axis=1)
            dln_ref[0:1, :] += jnp.sum(dvn * vhat, axis=0, keepdims=True)
            dln_ref[1:2, :] += jnp.sum(dvn, axis=0, keepdims=True)
            dvh = dvn * lnw_ref[...]
            dvgel = rstd * (dvh - jnp.mean(dvh, axis=-1, keepdims=True) - vhat * jnp.mean(dvh * vhat, axis=-1, keepdims=True))
            dp_ref[rs, DG:2 * DG] = (dvgel * dvg).astype(BF16)

        @pl.when(i == nb - 1)
        def _():
            head = lax.broadcasted_iota(jnp.int32, (8, BLK), 0)
            ones = jnp.ones((8, HD), F32)
            out = jnp.zeros((8, BLK), F32)
            for h in range(NH):
                sums = _dot(ones, dbs_acc[:, h * HD:(h + 1) * HD], NT, precision=HIGHEST)
                out = out + jnp.where(head == h, sums, 0.0)
            dbs_ref[...] = out
            exchange.finish()

    anyspec = pl.BlockSpec(memory_space=pl.ANY)
    res = pl.pallas_call(
        body, grid=(nb,),
        in_specs=[pl.BlockSpec((rows, DG), lambda i: (i, 0)), pl.BlockSpec((rows, DG), lambda i: (i, 1)),
                  pl.BlockSpec((rows, DG), lambda i: (i, 0)),
                  _full((NH, BLK, BLK)), _full((BLK, NH)), _full((1, DG)), _full((1, DG))] + [anyspec] * nw,
        out_specs=[pl.BlockSpec((rows, 2 * DG), lambda i: (i, 2)), _full((NH * BLK, BLK)), _full((8, BLK)), _full((8, DG))]
        + [anyspec] * nw,
        out_shape=[SDS((T, DIN), BF16), SDS((NH * BLK, BLK), F32), SDS((8, BLK), F32), SDS((8, DG), F32)]
        + _core_exchange_shapes(grads),
        scratch_shapes=[pltpu.VMEM((BLK, DG), F32)] + _core_exchange_sems(nw),
        compiler_params=_arb(), name="gmlp_bwd")(proj, proj, dycat, ws_b, bst, lnw, lnb, *grads)
    return res[:4], res[4:]


def _hgrn_bwd(proj, o_pre, a_all, st_all, dycat, lower_bounds, gn_w, dproj, tables, sums, row_blocks):
    T = proj.shape[0]
    nc = T // CH
    nch = min(HGRN_CHUNKS_PER_STEP, nc)
    steps = nc // nch
    w_st, w_st_t, _, masks_sym = tables
    n_lev = len(LEVELS)
    nw, nr = len(sums), len(row_blocks)

    def body(*refs):
        q_ref, f_ref, i_ref, g_ref, o_ref, a_ref, st_ref, dy_ref, lbp_ref, gn_ref, w_ref, wt_ref, ms_ref = refs[:13]
        n_in = 14 + nw + nr
        dp_ref, dlb_ref, dgn_ref = refs[n_in:n_in + 3]
        ds_scr, dx_scr = refs[n_in + 3 + nw + nr:n_in + 5 + nw + nr]
        sems = refs[n_in + 5 + nw + nr:]
        exchange = _ChipExchange(refs[14:14 + nw], refs[n_in + 3:n_in + 3 + nw], *sems[:2])
        rows_gather = _RowGather(refs[14 + nw:n_in], refs[n_in + 3 + nw:n_in + 3 + nw + nr], *sems[2:])
        i = pl.program_id(0)

        @pl.when(i == 0)
        def _():
            rows_gather.start()
            exchange.start()
            ds_scr[...] = jnp.zeros_like(ds_scr)
            dlb_ref[...] = jnp.zeros_like(dlb_ref)
            dgn_ref[...] = jnp.zeros_like(dgn_ref)

        @pl.when(i == steps // 2)
        def _():
            rows_gather.forward()

        lb, omlb = _lower_bound(lbp_ref)
        row = lax.broadcasted_iota(jnp.int32, (CH, 1), 0)
        eye = lax.broadcasted_iota(jnp.int32, (CH, CH), 0) == lax.broadcasted_iota(jnp.int32, (CH, CH), 1)
        lower = lax.broadcasted_iota(jnp.int32, (CH, CH), 0) > lax.broadcasted_iota(jnp.int32, (CH, CH), 1)
        dgn = jnp.zeros((1, HD), F32)
        pre = []
        for ci in range(nch):
            rs = slice(ci * CH, (ci + 1) * CH)
            q = q_ref[rs, :]
            v = i_ref[rs, :]
            g = g_ref[rs, :]
            sq, qf, sig, f, k, e = _hgrn_gates(q, f_ref[rs, :], lb, omlb, w_ref)
            eb = e[0]
            ekd = e[1]
            kd = k * ekd
            qe = qf * eb
            dob_h, dqe_h, dqf_h, dki_h, dv_h, dg_h = [], [], [], [], [], []
            for h in range(NH):
                sl = slice(h * HD, (h + 1) * HD)
                o = o_ref[rs, sl]
                ro = _rms(o)
                oh = o * ro
                gh = g[:, sl]
                sg = _sigmoid(gh)
                dyb = dy_ref[rs, sl]
                dg_h.append(dyb * (oh * gn_ref[...]) * (sg * (1.0 + gh * (1.0 - sg))))
                don = dyb * (gh * sg)
                dgn = dgn + jnp.sum(don * oh, axis=0, keepdims=True)
                dob = _rms_bwd(oh, ro, don * gn_ref[...]).astype(BF16)
                vb = v[:, sl].astype(BF16)
                qh, kh = qf[:, sl], k[:, sl]
                dqe = _dot(dob, st_ref[ci, h].astype(BF16))
                da = _dot(dob, vb, NT)
                ddiag = jnp.sum(jnp.where(eye, da, 0.0), axis=-1, keepdims=True)
                dsym = jnp.where(lower, da, _dot(vb, dob, NT))
                upper_part = jnp.zeros((CH, HD), F32)
                both = jnp.zeros((CH, HD), F32)
                for li in range(n_lev):
                    el, up, y = _level_factor(e, li, sl, row, qh, kh)
                    dyv = _dot((ms_ref[li] * dsym).astype(BF16), y.astype(BF16))
                    dx_scr[ci, (2 + li) * CH:(3 + li) * CH, sl] = dyv * y
                    dye = dyv * el
                    upper_part = upper_part + jnp.where(up, dye, 0.0)
                    both = both + dye
                dob_h.append(dob)
                dqe_h.append(dqe)
                dqf_h.append(dqe * eb[:, sl] + ddiag * kh + upper_part)
                dki_h.append(ddiag * qh + (both - upper_part))
                dv_h.append(_dot(a_ref[ci, h].astype(BF16), dob, TN))
            dp_ref[rs, 0:DH] = (jnp.concatenate(dqf_h, axis=1) * (sq * (1.0 + q * (1.0 - sq)))).astype(BF16)
            dp_ref[rs, 3 * DH:4 * DH] = jnp.concatenate(dg_h, axis=1).astype(BF16)
            pre.append((v, sig, f, eb, ekd, kd, qe, dob_h, jnp.concatenate(dqe_h, axis=1), dki_h, dv_h))
        dgn_ref[0:1, :] += dgn
        for ci in reversed(range(nch)):
            rs = slice(ci * CH, (ci + 1) * CH)
            v, sig, f, eb, ekd, kd, qe, dob_h, dqe, dki_h, dv_h = pre[ci]
            ebl = eb[CH - 1:CH, :]
            dbl_h, dkd_h, dv2_h = [], [], []
            for h in range(NH):
                sl = slice(h * HD, (h + 1) * HD)
                dst1 = ds_scr[h]
                dst1b = dst1.astype(BF16)
                ds_scr[h] = dst1 * ebl[:, sl] + _dot(dob_h[h], qe[:, sl].astype(BF16), TN)
                dbl_h.append(ebl[:, sl] * jnp.sum(st_ref[ci, h] * dst1, axis=0, keepdims=True))
                dkd_h.append(_dot(v[:, sl].astype(BF16), dst1b))
                dv2_h.append(dv_h[h] + _dot(kd[:, sl].astype(BF16), dst1b, NT))
            dkd = jnp.concatenate(dkd_h, axis=1)
            dx_scr[ci, 0:CH, :] = dqe * qe + jnp.where(row == CH - 1, jnp.concatenate(dbl_h, axis=1), 0.0)
            dx_scr[ci, CH:2 * CH, :] = dkd * kd
            dlf = _split_dot(wt_ref[...], dx_scr[ci], 2)
            df = dlf / f - (dkd * ekd + jnp.concatenate(dki_h, axis=1))
            dlb_ref[0:1, :] += jnp.sum(df * (1.0 - sig), axis=0, keepdims=True)
            dp_ref[rs, DH:2 * DH] = (df * omlb * sig * (1.0 - sig)).astype(BF16)
            dp_ref[rs, 2 * DH:3 * DH] = jnp.concatenate(dv2_h, axis=1).astype(BF16)

        @pl.when(i == steps - 1)
        def _():
            gl = dlb_ref[0:1, :] * lb * omlb
            dlb_ref[0:1, :] = gl
            dlb_ref[1:2, :] = -gl
            exchange.finish()
            rows_gather.finish()

    rev = lambda j: pl.BlockSpec((nch * CH, DH), lambda c: (steps - 1 - c, j))
    anyspec = pl.BlockSpec(memory_space=pl.ANY)
    res = pl.pallas_call(
        body, grid=(steps,),
        in_specs=[rev(2), rev(3), rev(4), rev(5), rev(0),
                  pl.BlockSpec((nch, NH, CH, CH), lambda c: (steps - 1 - c, 0, 0, 0)),
                  pl.BlockSpec((nch, NH, HD, HD), lambda c: (steps - 1 - c, 0, 0, 0)),
                  rev(1), _full((2, DH)), _full((1, HD)),
                  _full(w_st.shape), _full(w_st_t.shape), _full(masks_sym.shape),
                  anyspec] + [anyspec] * (nw + nr),
        out_specs=[pl.BlockSpec((nch * CH, 4 * DH), lambda c: (steps - 1 - c, 0)), _full((8, DH)), _full((8, HD))]
        + [anyspec] * (nw + nr),
        out_shape=[SDS((T, DIN), BF16), SDS((8, DH), F32), SDS((8, HD), F32)] + _slot_shapes(sums)
        + _gather_rows_shapes(row_blocks),
        scratch_shapes=[pltpu.VMEM((NH, HD, HD), F32), pltpu.VMEM((nch, (2 + n_lev) * CH, DH), F32)]
        + _exchange_sems(nw) + _gather_rows_sems(nr),
        input_output_aliases={13: 0},
        compiler_params=_arb(), name="hgrn_bwd")(proj, proj, proj, proj, o_pre, a_all, st_all, dycat, lower_bounds, gn_w,
                                                 w_st, w_st_t, masks_sym, dproj, *sums, *row_blocks)
    return res[:3], res[3:3 + nw], res[3 + nw:]


def _proj_in_bwd(dproj, x, dx1, nw, sc, w_in_b, tm, sums):
    T = x.shape[0]
    ns = len(sums)
    steps = T // tm

    def body(*refs):
        dp_ref, x_ref, dx1_ref, nw_ref, sc_ref, w_ref = refs[:6]
        gx_ref, acc_ref = refs[6 + ns:8 + ns]
        exchange = _ChipExchange(refs[6:6 + ns], refs[8 + ns:8 + 2 * ns], *refs[8 + 2 * ns:])

        @pl.when(pl.program_id(0) == 0)
        def _():
            exchange.start()
            acc_ref[...] = jnp.zeros_like(acc_ref)

        dh = _dot(dp_ref[:, 0:4 * DH], w_ref[:, 2 * DG:DIN], NT) + _dot(dp_ref[:, 4 * DH:DIN], w_ref[:, 0:2 * DG], NT)
        xv = x_ref[...]
        r = _rms(xv)
        xh = xv * r
        n1 = xh * nw_ref[...]
        acc_ref[0:1, :] += jnp.sum(dh, axis=0, keepdims=True)
        acc_ref[1:2, :] += jnp.sum(dh * n1, axis=0, keepdims=True)
        dn = dh * (1.0 + sc_ref[...])
        acc_ref[2:3, :] += jnp.sum(dn * xh, axis=0, keepdims=True)
        gx_ref[...] = dx1_ref[...] + _rms_bwd(xh, r, dn * nw_ref[...])

        @pl.when(pl.program_id(0) == steps - 1)
        def _():
            exchange.finish()

    row = lambda i: (i, 0)
    anyspec = pl.BlockSpec(memory_space=pl.ANY)
    res = pl.pallas_call(
        body, grid=(steps,),
        in_specs=[pl.BlockSpec((tm, DIN), row), pl.BlockSpec((tm, D), row), pl.BlockSpec((tm, D), row),
                  _full((1, D)), _ada_part(ADA_SC1), _resident((D, DIN))] + [anyspec] * ns,
        out_specs=[pl.BlockSpec((tm, D), row), _full((8, D))] + [anyspec] * ns,
        out_shape=[SDS((T, D), F32), SDS((8, D), F32)] + _slot_shapes(sums),
        scratch_shapes=_exchange_sems(ns),
        compiler_params=_arb(), name="proj_in_bwd")(dproj, x, dx1, nw, sc, w_in_b, *sums)
    return res[:2], res[2:]


def _wgrad(a, b, bk, bn, tt, name, bf16_copy=False):
    T, K = a.shape
    N = b.shape[1]
    nn, nk, nt = N // bn, K // bk, T // tt
    bmap = lambda n, k, t: (t, n)

    def body(a_ref, b_ref, o_ref, *copy_ref):
        @pl.when(pl.program_id(2) == 0)
        def _():
            o_ref[...] = jnp.zeros_like(o_ref)

        o_ref[0] += _dot(a_ref[...], b_ref[...], TN)

        if bf16_copy:
            @pl.when(pl.program_id(2) == nt - 1)
            def _():
                copy_ref[0][...] = o_ref[...].astype(BF16)

    ospec = pl.BlockSpec((1, bk, bn), lambda n, k, t: (n, k, 0))
    return pl.pallas_call(
        body, grid=(nn, nk, nt),
        in_specs=[pl.BlockSpec((tt, bk), lambda n, k, t: (t, k)), pl.BlockSpec((tt, bn), bmap)],
        out_specs=[ospec, ospec] if bf16_copy else ospec,
        out_shape=[SDS((nn, K, bn), F32), SDS((nn, K, bn), BF16)] if bf16_copy else SDS((nn, K, bn), F32),
        compiler_params=_arb(3), name=name)(a, b)


def _adam_math(w, g, m, v):
    m = B1 * m + (1.0 - B1) * g
    v = B2 * v + (1.0 - B2) * (g * g)
    m_hat = m / (1.0 - B1 ** STEP)
    v_hat = v / (1.0 - B2 ** STEP)
    return -LR * (m_hat / (jnp.sqrt(v_hat) + AEPS) + WD * w), m, v


def _adamw_halves(w, mine, sibling, m, v, c_idx, rb, name):
    R, C = w.shape
    nb = (R // 2) // rb

    def body(c_ref, w_ref, a_ref, b_ref, m_ref, v_ref, g_out, d_out, m_out, v_out):
        g = jnp.where(pl.program_id(0) == c_ref[0], a_ref[...], b_ref[...])
        g_out[...] = g
        d_out[...], m_out[...], v_out[...] = _adam_math(w_ref[...], g, m_ref[...], v_ref[...])

    whole = pl.BlockSpec((rb, C), lambda hh, i, cr: (hh * nb + i, 0))
    half = pl.BlockSpec((rb, C), lambda hh, i, cr: (i, 0))
    return pl.pallas_call(
        body,
        grid_spec=pltpu.PrefetchScalarGridSpec(
            num_scalar_prefetch=1, grid=(2, nb), in_specs=[whole, half, half, whole, whole], out_specs=[whole] * 4),
        out_shape=[SDS((R, C), F32)] * 4, compiler_params=_arb(2), name=name)(c_idx, w, mine, sibling, m, v)


def _ada_wgrad_adam(cact_t, dada_all, w, m, v, chip_idx):
    R, C = w.shape
    rb = 256

    def body(j_ref, c_ref, d_ref, w_ref, m_ref, v_ref, g_out, d_out, m_out, v_out):
        g = _dot(c_ref[...], d_ref[...], precision=HIGHEST)
        g_out[...] = g
        d_out[...], m_out[...], v_out[...] = _adam_math(w_ref[...], g, m_ref[...], v_ref[...])

    spec = pl.BlockSpec((rb, C), lambda i, j: (i, 0))
    return pl.pallas_call(
        body,
        grid_spec=pltpu.PrefetchScalarGridSpec(
            num_scalar_prefetch=1, grid=(R // rb,),
            in_specs=[pl.BlockSpec((rb, N_DEV), lambda i, j: (i, 0)), pl.BlockSpec((N_DEV, C), lambda i, j: (0, j[0])),
                      spec, spec, spec],
            out_specs=[spec] * 4),
        out_shape=[SDS((R, C), F32)] * 4,
        compiler_params=_arb(), name="ada_wgrad_adam")(chip_idx, cact_t, dada_all, w, m, v)


SMALL_NAMES = ('b_ada', 'norm1_w', 'norm2_w', 'final_norm_w', 'v_ln_w', 'v_ln_b', 'lower_bounds', 'gn_w', 'b_s', 'w_s')


def _small_finalize(gathered, params, moms, vels):
    n_in = len(gathered)

    def body(*refs):
        acc1, acc2, dln, dlb, dgn, dbs, dws = refs[:n_in]
        prm = [dict(zip(SMALL_NAMES, refs[n_in + k * 10:n_in + (k + 1) * 10])) for k in range(3)]
        outs = [dict(zip(SMALL_NAMES, refs[n_in + 30 + k * 10:n_in + 30 + (k + 1) * 10])) for k in range(4)]
        loss_ref, dada_ref = refs[n_in + 70:n_in + 72]

        def dev_sum(ref, first, n):
            per = ref.shape[0] // N_DEV
            g = ref[first:first + n, :]
            for dev in range(1, N_DEV):
                g = g + ref[dev * per + first:dev * per + first + n, :]
            return g

        def update(n, g, cols=slice(None)):
            outs[0][n][:, cols] = g
            outs[1][n][:, cols], outs[2][n][:, cols], outs[3][n][:, cols] = _adam_math(
                prm[0][n][:, cols], g, prm[1][n][:, cols], prm[2][n][:, cols])

        ada_rows = ((acc1, 0), (acc1, 1), (acc2, 5), (acc2, 2), (acc2, 1), (acc2, 0))
        for k, (ref, r) in enumerate(ada_rows):
            update('b_ada', dev_sum(ref, r, 1), slice(k * D, (k + 1) * D))
            for dev in range(N_DEV):
                dada_ref[dev:dev + 1, k * D:(k + 1) * D] = ref[8 * dev + r:8 * dev + r + 1, :]
        update('norm1_w', dev_sum(acc1, 2, 1))
        update('norm2_w', dev_sum(acc2, 3, 1))
        update('final_norm_w', dev_sum(acc2, 4, 1))
        update('v_ln_w', dev_sum(dln, 0, 1))
        update('v_ln_b', dev_sum(dln, 1, 1))
        update('lower_bounds', dev_sum(dlb, 0, 2))
        update('gn_w', dev_sum(dgn, 0, 1))
        update('b_s', dev_sum(dbs, 0, NH))
        update('w_s', dev_sum(dws, 0, NH * BLK))
        loss_ref[...] = jnp.sum(dev_sum(acc2, 6, 1), axis=-1, keepdims=True)

    shapes = [SDS(params[n].shape, F32) for n in SMALL_NAMES]
    res = pl.pallas_call(
        body, out_shape=shapes * 4 + [SDS((1, 1), F32), SDS((N_DEV, 6 * D), F32)], name="small_finalize")(
            *gathered, *[d[n] for d in (params, moms, vels) for n in SMALL_NAMES])
    return [dict(zip(SMALL_NAMES, res[k * 10:(k + 1) * 10])) for k in range(4)], res[40], res[41]


def _position():
    x, y, c = lax.axis_index("x"), lax.axis_index("y"), lax.axis_index("c")
    return x, y, c


def _chip_at(x, y, r):
    return (x ^ (r >> 1), y ^ (r & 1))


class _RowGather:
    def __init__(self, ins, outs, send_sems, recv_sems, local_sems):
        self.ins, self.outs = ins, outs
        self.send_sems, self.recv_sems, self.local_sems = send_sems, recv_sems, local_sems
        self.x, self.y, self.c = _position()
        self.me, self.sibling = (self.x, self.y, self.c), (self.x, self.y, 1 - self.c)
        self.chips = [_chip_at(self.x, self.y, r) for r in (1, 2, 3)]

    def _rows(self, b, px, py, pc):
        m_per = self.ins[b].shape[0]
        return self.outs[b].at[pl.ds((4 * px + 2 * py + pc) * m_per, m_per), :]

    def _copy(self, b, k, blk, to, from_input=False):
        return pltpu.make_async_remote_copy(
            src_ref=self.ins[b] if from_input else self._rows(b, *blk), dst_ref=self._rows(b, *blk),
            send_sem=self.send_sems.at[7 * b + k], recv_sem=self.recv_sems.at[7 * b + k],
            device_id=to, device_id_type=MESH)

    def _local(self, b):
        return pltpu.make_async_copy(self.ins[b], self._rows(b, *self.me), self.local_sems.at[b])

    def _first(self, b):
        c = self.c
        return [self._copy(b, 0, self.me, self.sibling, from_input=True)] + [
            self._copy(b, 1 + j, self.me, (*chip, c), from_input=True) for j, chip in enumerate(self.chips)]

    def start(self):
        for b in range(len(self.ins)):
            self._local(b).start()
            for cp in self._first(b):
                cp.start()

    def forward(self):
        for b in range(len(self.ins)):
            for j, chip in enumerate(self.chips):
                self._copy(b, 1 + j, (*chip, self.c), self.me).wait_recv()
                self._copy(b, 4 + j, (*chip, self.c), self.sibling).start()

    def finish(self):
        for b in range(len(self.ins)):
            self._copy(b, 0, self.sibling, self.me).wait_recv()
            for j, chip in enumerate(self.chips):
                self._copy(b, 4 + j, (*chip, 1 - self.c), self.me).wait_recv()
        for b in range(len(self.ins)):
            for cp in self._first(b):
                cp.wait_send()
            for j, chip in enumerate(self.chips):
                self._copy(b, 4 + j, (*chip, self.c), self.sibling).wait_send()
            self._local(b).wait()


def _gather_rows(ins, outs, send_sems, recv_sems, local_sems, after_issue=None):
    g = _RowGather(ins, outs, send_sems, recv_sems, local_sems)
    g.start()
    if after_issue is not None:
        after_issue()
    g.forward()
    g.finish()


def _gather_rows_shapes(blocks):
    return [SDS((N_DEV * b.shape[0], b.shape[1]), b.dtype) for b in blocks]


def _gather_rows_sems(nb):
    return [pltpu.SemaphoreType.DMA((7 * nb,)), pltpu.SemaphoreType.DMA((7 * nb,)), pltpu.SemaphoreType.DMA((nb,))]


def _all_gather_rows(blocks, name):
    nb = len(blocks)

    def body(*refs):
        _gather_rows(refs[:nb], refs[nb:2 * nb], *refs[2 * nb:])

    vmem = pl.BlockSpec(memory_space=pltpu.VMEM)
    return pl.pallas_call(
        body, out_shape=_gather_rows_shapes(blocks), in_specs=[vmem] * nb, out_specs=[vmem] * nb,
        scratch_shapes=_gather_rows_sems(nb), name=name)(*blocks)


def _place_shard(w_shard, axis, chip_idx, name):
    R, C = w_shard.shape
    rb = _row_block(R)
    nb = R // rb
    full = (R * N_CHIPS, C) if axis == 0 else (R, C * N_CHIPS)
    omap = (lambda i, j: (j[0] * nb + i, 0)) if axis == 0 else (lambda i, j: (i, j[0]))

    def body(j_ref, w_ref, o_ref):
        o_ref[...] = w_ref[...].astype(BF16)

    return pl.pallas_call(
        body,
        grid_spec=pltpu.PrefetchScalarGridSpec(
            num_scalar_prefetch=1, grid=(nb,), in_specs=[pl.BlockSpec((rb, C), lambda i, j: (i, 0))],
            out_specs=pl.BlockSpec((rb, C), omap)),
        out_shape=SDS(full, BF16), compiler_params=_arb(), name=name)(chip_idx, w_shard)


class _WeightGather:
    def __init__(self, refs, axes, send_sems, recv_sems):
        self.refs, self.axes, self.send_sems, self.recv_sems = refs, axes, send_sems, recv_sems
        self.x, self.y, self.c = _position()
        self.j = 2 * self.x + self.y
        self.n = 3 * len(refs)

    def _half(self, w, chip_idx, half):
        ref, axis = self.refs[w], self.axes[w]
        if axis == 0:
            size = ref.shape[0] // N_CHIPS
            return ref.at[pl.ds(chip_idx * size + half * (size // 2), size // 2), :]
        size = ref.shape[1] // N_CHIPS
        rows = ref.shape[0] // 2
        return ref.at[pl.ds(half * rows, rows), pl.ds(chip_idx * size, size)]

    def _ici(self, w, r, chip_idx):
        k = 3 * w + r - 1
        piece = self._half(w, chip_idx, self.c)
        return pltpu.make_async_remote_copy(
            src_ref=piece, dst_ref=piece, send_sem=self.send_sems.at[k], recv_sem=self.recv_sems.at[k],
            device_id=(*_chip_at(self.x, self.y, r), self.c), device_id_type=MESH)

    def _d2d(self, w, r, half):
        k = self.n + 3 * w + r - 1
        piece = self._half(w, self.j ^ r, half)
        return pltpu.make_async_remote_copy(
            src_ref=piece, dst_ref=piece, send_sem=self.send_sems.at[k], recv_sem=self.recv_sems.at[k],
            device_id=(self.x, self.y, 1 - self.c), device_id_type=MESH)

    def _each(self):
        return [(w, r) for w in range(len(self.refs)) for r in (1, 2, 3)]

    def start(self):
        for w, r in self._each():
            self._ici(w, r, self.j).start()

    def forward(self):
        for w, r in self._each():
            self._ici(w, r, self.j ^ r).wait_recv()
            self._d2d(w, r, self.c).start()

    def finish(self):
        for w, r in self._each():
            self._ici(w, r, self.j).wait_send()
            self._d2d(w, r, self.c).wait_send()
            self._d2d(w, r, 1 - self.c).wait_recv()


def _gather_sems(n_weights):
    return [pltpu.SemaphoreType.DMA((6 * n_weights,)), pltpu.SemaphoreType.DMA((6 * n_weights,))]


def _gather_w_in_and_ada(placed, axis, c_block, w_ada):
    n = w_ada.shape[1]

    def body(w_any, c_ref, wada_ref, w_out, call_ref, cact_ref, p_ref, *sems):
        g = _WeightGather([w_out], [axis], *sems[:2])
        _gather_rows([c_ref], [call_ref], *sems[2:], after_issue=g.start)
        cv = call_ref[...]
        ca = cv * _sigmoid(cv)
        cact_ref[...] = ca
        p_ref[...] = _dot(ca, wada_ref[...], precision=HIGHEST)
        g.forward()
        g.finish()

    anyspec = pl.BlockSpec(memory_space=pl.ANY)
    vmem = pl.BlockSpec(memory_space=pltpu.VMEM)
    rows = N_DEV * c_block.shape[0]
    res = pl.pallas_call(
        body, out_shape=[SDS(placed.shape, placed.dtype), SDS((rows, D), F32), SDS((rows, D), F32), SDS((rows, n), F32)],
        in_specs=[anyspec, vmem, vmem], out_specs=[anyspec, vmem, vmem, vmem],
        scratch_shapes=_gather_sems(1) + _gather_rows_sems(1), input_output_aliases={0: 0},
        name="gather_w_in_and_ada")(placed, c_block, w_ada)
    return res[0], res[2], res[3]


class _ChipExchange:
    def __init__(self, ins, outs, send_sems, recv_sems):
        self.ins, self.outs, self.send_sems, self.recv_sems = ins, outs, send_sems, recv_sems
        self.x, self.y, self.c = _position()
        self.j = 2 * self.x + self.y

    def _copies(self):
        for w in range(len(self.ins)):
            for r in (1, 2, 3):
                k = 3 * w + r - 1
                yield pltpu.make_async_remote_copy(
                    src_ref=self.ins[w].at[self.j ^ r], dst_ref=self.outs[w].at[r - 1],
                    send_sem=self.send_sems.at[k], recv_sem=self.recv_sems.at[k],
                    device_id=(*_chip_at(self.x, self.y, r), self.c), device_id_type=MESH)

    def start(self):
        for cp in self._copies():
            cp.start()

    def finish(self):
        for cp in self._copies():
            cp.wait()


def _exchange_sems(n_weights):
    return [pltpu.SemaphoreType.DMA((3 * n_weights,)), pltpu.SemaphoreType.DMA((3 * n_weights,))]


class _CoreExchange:
    def __init__(self, ins, outs, send_sems, recv_sems):
        self.ins, self.outs, self.send_sems, self.recv_sems = ins, outs, send_sems, recv_sems
        self.x, self.y, self.c = _position()

    def _copies(self):
        for w in range(len(self.ins)):
            yield pltpu.make_async_remote_copy(
                src_ref=self.ins[w].at[:, 1 - self.c], dst_ref=self.outs[w],
                send_sem=self.send_sems.at[w], recv_sem=self.recv_sems.at[w],
                device_id=(self.x, self.y, 1 - self.c), device_id_type=MESH)

    def start(self):
        for cp in self._copies():
            cp.start()

    def finish(self):
        for cp in self._copies():
            cp.wait()


def _core_exchange_shapes(grads):
    return [SDS((g.shape[0], g.shape[2], g.shape[3]), g.dtype) for g in grads]


def _core_exchange_sems(n):
    return [pltpu.SemaphoreType.DMA((n,)), pltpu.SemaphoreType.DMA((n,))]


def _exchange_core_halves(grads, name):
    nw = len(grads)

    def body(*refs):
        ex = _CoreExchange(refs[:nw], refs[nw:2 * nw], *refs[2 * nw:])
        ex.start()
        ex.finish()

    anyspec = pl.BlockSpec(memory_space=pl.ANY)
    return pl.pallas_call(
        body, out_shape=_core_exchange_shapes(grads), in_specs=[anyspec] * nw, out_specs=[anyspec] * nw,
        scratch_shapes=_core_exchange_sems(nw), name=name)(*grads)


def _add_core_halves(g4, recv, c_idx, rb, name):
    ns, _, rh, C = g4.shape

    def body(c_ref, g_ref, r_ref, o_ref):
        o_ref[...] = (g_ref[0] + r_ref[...]).astype(BF16)

    return pl.pallas_call(
        body,
        grid_spec=pltpu.PrefetchScalarGridSpec(
            num_scalar_prefetch=1, grid=(ns, rh // rb),
            in_specs=[pl.BlockSpec((1, 1, rb, C), lambda s, i, cr: (s, cr[0], i, 0)),
                      pl.BlockSpec((1, rb, C), lambda s, i, cr: (s, i, 0))],
            out_specs=pl.BlockSpec((1, rb, C), lambda s, i, cr: (s, i, 0))),
        out_shape=SDS((ns, rh, C), BF16), compiler_params=_arb(2), name=name)(c_idx, g4, recv)


def _add_core_halves_in(g4, recv, c_idx, name):
    n_slabs, _, rh, C = g4.shape
    cb = 256
    per_slab, per_chip, n_blocks = C // cb, DIN // N_CHIPS // cb, DIN // cb

    def stored(s, k):
        sb = (per_chip * s + k + 4 * DH // cb) % n_blocks
        return sb // per_slab, sb % per_slab

    def body(c_ref, g_ref, r_ref, o_ref):
        o_ref[...] = (g_ref[0] + r_ref[...].astype(F32)).astype(BF16)

    return pl.pallas_call(
        body,
        grid_spec=pltpu.PrefetchScalarGridSpec(
            num_scalar_prefetch=1, grid=(N_CHIPS, per_chip),
            in_specs=[pl.BlockSpec((1, 1, rh, cb), lambda s, k, cr: (stored(s, k)[0], cr[0], 0, stored(s, k)[1])),
                      pl.BlockSpec((1, rh, cb), lambda s, k, cr: (stored(s, k)[0], 0, stored(s, k)[1]))],
            out_specs=pl.BlockSpec((1, rh, cb), lambda s, k, cr: (s, 0, k))),
        out_shape=SDS((N_CHIPS, rh, DIN // N_CHIPS), BF16), compiler_params=_arb(2), name=name)(c_idx, g4, recv)


def _slot_shapes(sums):
    return [SDS((3,) + s.shape[1:], s.dtype) for s in sums]


def _add_chips(own, slots, order, rb, name):
    _, rh, C = slots.shape

    def body(o_ref, own_ref, a_ref, b_ref, c_ref, d_ref, out_ref):
        mine = own_ref[0].astype(F32)
        t = [jnp.where(o_ref[i] == 0, mine, r[0].astype(F32)) for i, r in enumerate((a_ref, b_ref, c_ref, d_ref))]
        out_ref[...] = ((t[0] + t[1]) + t[2]) + t[3]

    def spec(i):
        return pl.BlockSpec((1, rb, C), lambda t, o: (jnp.maximum(o[i], 1) - 1, t, 0))

    return pl.pallas_call(
        body,
        grid_spec=pltpu.PrefetchScalarGridSpec(
            num_scalar_prefetch=1, grid=(rh // rb,),
            in_specs=[pl.BlockSpec((1, rb, C), lambda t, o: (o[4], t, 0)), spec(0), spec(1), spec(2), spec(3)],
            out_specs=pl.BlockSpec((rb, C), lambda t, o: (t, 0))),
        out_shape=SDS((rh, C), F32), compiler_params=_arb(), name=name)(order, own, slots, slots, slots, slots)


def _share_halves(halves):
    nw = len(halves)

    def body(*refs):
        ins, outs = refs[:nw], refs[nw:2 * nw]
        send_sems, recv_sems = refs[2 * nw:]
        x, y, c = _position()
        started = []
        for w in range(nw):
            cp = pltpu.make_async_remote_copy(
                src_ref=ins[w], dst_ref=outs[w], send_sem=send_sems.at[w], recv_sem=recv_sems.at[w],
                device_id=(x, y, 1 - c), device_id_type=MESH)
            cp.start()
            started.append(cp)
        for cp in started:
            cp.wait()

    anyspec = pl.BlockSpec(memory_space=pl.ANY)
    return pl.pallas_call(
        body, out_shape=[SDS(h.shape, F32) for h in halves], in_specs=[anyspec] * nw, out_specs=[anyspec] * nw,
        scratch_shapes=[pltpu.SemaphoreType.DMA((nw,)), pltpu.SemaphoreType.DMA((nw,))],
        name="share_halves")(*halves)


def _small_2d(b_ada, norm1_w, norm2_w, final_norm_w, v_ln_w, v_ln_b, lower_bounds, gn_w, b_s, w_s):
    return dict(zip(SMALL_NAMES, (b_ada, norm1_w, norm2_w, final_norm_w.reshape(1, D), v_ln_w, v_ln_b, lower_bounds, gn_w,
                                  b_s.reshape(NH, BLK), w_s.reshape(NH * BLK, BLK))))


def _small_original_shapes(d):
    out = dict(d)
    out['final_norm_w'] = d['final_norm_w'].reshape(D)
    out['b_s'] = d['b_s'].reshape(1, NH, BLK)
    out['w_s'] = d['w_s'].reshape(1, NH, BLK, BLK)
    return out


def _row_block(r):
    for cand in (256, 176, 128, 64, 32, 16, 8):
        if r % cand == 0:
            return cand
    return r


def kernel(x, c, w_ada, b_ada, norm1_w, w_in, w_s, b_s, v_ln_w, v_ln_b, lower_bounds, gn_w, w_out, norm2_w, w_ffn_in, w_ffn_out, final_norm_w, loss_target, m_w_ada, m_b_ada, m_norm1_w, m_w_in, m_w_s, m_b_s, m_v_ln_w, m_v_ln_b, m_lower_bounds, m_gn_w, m_w_out, m_norm2_w, m_w_ffn_in, m_w_ffn_out, m_final_norm_w, v_w_ada, v_b_ada, v_norm1_w, v_w_in, v_w_s, v_b_s, v_v_ln_w, v_v_ln_b, v_lower_bounds, v_gn_w, v_w_out, v_norm2_w, v_w_ffn_in, v_w_ffn_out, v_final_norm_w):
    T = x.shape[1]
    tm, tp = min(TOKEN_TILE, T), min(PROJ_TILE, T)
    px, py, pc = _position()
    chip = 2 * px + py
    me = 4 * px + 2 * py + pc
    x2d = x.reshape(T, D)
    tgt = loss_target.reshape(T, D)

    chip_idx = jnp.reshape(chip, (1,)).astype(jnp.int32)
    c_idx = jnp.reshape(pc, (1,)).astype(jnp.int32)
    w_in_b, cact, ada_part = _gather_w_in_and_ada(
        _place_shard(w_in[0], 1, chip_idx, "place_in"), 1, jnp.broadcast_to(c, (8, D)), w_ada[0])
    cact = cact.reshape(N_DEV, 8, D)[:, 0, :]
    n_ada = ada_part.shape[1]
    ada_part = ada_part.reshape(N_DEV, 8, n_ada)[:, 0, :]
    placed = [_place_shard(w_out[0], 0, chip_idx, "place_out"), _place_shard(w_ffn_in[0], 1, chip_idx, "place_ffn_in"),
              _place_shard(w_ffn_out[0], 0, chip_idx, "place_ffn_out")]

    (ada_all,) = _all_gather_rows([ada_part], "gather_ada")
    ada_all = ada_all.reshape(N_CHIPS, 2, N_DEV, n_ada)[:, 0]
    ada = lax.dynamic_index_in_dim(ada_all, me, axis=1, keepdims=False).reshape(1, 6 * D) + b_ada

    rr = lax.broadcasted_iota(jnp.int32, (BLK, BLK), 0) // CH
    cc = lax.broadcasted_iota(jnp.int32, (BLK, BLK), 1) // CH
    ws_b = jnp.where((rr >= cc)[None], w_s[0], 0.0).astype(BF16)
    bst = b_s[0].T
    lnw, lnb = v_ln_w, v_ln_b
    nw1, nw2, fw = norm1_w, norm2_w, final_norm_w.reshape(1, D)

    tables = _hgrn_tables()
    (h1, proj, ycat, o_pre, a_all, st_all), (w_out_b, w_fi_b, w_fo_b) = _proj_hgrn_fwd(
        x2d, nw1, ada, w_in_b, lower_bounds, gn_w, tables, placed, [0, 1, 0])

    dycat, dx1, h2, act, dff, dgu, dmix, acc2, ycat = _token_local(
        x2d, ycat, tgt, ada, nw2, ada, ada, ada, fw, w_out_b, w_fi_b, w_fo_b, proj, ws_b, bst, lnw, lnb, tm)

    tt = min(WGRAD_TOKENS, T)
    order = jnp.concatenate([chip ^ jnp.arange(N_CHIPS, dtype=jnp.int32), chip_idx]).astype(jnp.int32)

    def by_core_half(g):
        return g.reshape(g.shape[0], 2, g.shape[1] // 2, g.shape[2])

    def core_sums(g4, recv, names):
        return [_add_core_halves(a, b, c_idx, _row_block(a.shape[2]), "add_core_" + n) for a, b, n in zip(g4, recv, names)]

    def chip_sums(sums, slots, names):
        return [_add_chips(o, s, order, _row_block(s.shape[1]), "add_chips_" + n) for o, s, n in zip(sums, slots, names)]

    g_out = _wgrad(ycat, dmix, D, D, tt, "wgrad_out").reshape(N_CHIPS, D // N_CHIPS, D)
    g_fi = _wgrad(h2, dgu, D, FFB, tt, "wgrad_ffn_in")
    g_fo = _wgrad(act, dff, FFB, D, tt, "wgrad_ffn_out").reshape(N_CHIPS, DFF // N_CHIPS, D)
    late_names = ["out", "ffn_in", "ffn_out"]
    late_g4 = [by_core_half(g) for g in (g_out, g_fi, g_fo)]

    (dproj, dws, dbs, dln), late_recv = _gmlp_bwd(proj, dycat, ws_b, bst, lnw, lnb, late_g4)
    late_sums = core_sums(late_g4, late_recv, late_names)
    (dproj, dlb, dgn), late_slots, (acc2_all, dln_all, dbs_all, dws_all) = _hgrn_bwd(
        proj, o_pre, a_all, st_all, dycat, lower_bounds, gn_w, dproj, tables, late_sums, [acc2, dln, dbs, dws])

    g_in, g_in_wire = _wgrad(h1, dproj, D, D, tt, "wgrad_in", bf16_copy=True)
    (in_recv,) = _exchange_core_halves([by_core_half(g_in_wire)], "exchange_core_halves_in")
    in_sums = [_add_core_halves_in(by_core_half(g_in), in_recv, c_idx, "add_core_in")]
    (grad_x, acc1), in_slots = _proj_in_bwd(dproj, x2d, dx1, nw1, ada, w_in_b, tp, in_sums)
    names = ["in"] + late_names
    halves = chip_sums(in_sums, in_slots, ["in"]) + chip_sums(late_sums, late_slots, late_names)
    sibling_halves = _share_halves(halves)

    big_w = [(w_in, m_w_in, v_w_in), (w_out, m_w_out, v_w_out), (w_ffn_in, m_w_ffn_in, v_w_ffn_in),
             (w_ffn_out, m_w_ffn_out, v_w_ffn_out)]
    big_out = []
    for mine, sib, (w, m, v), n in zip(halves, sibling_halves, big_w, names):
        res = _adamw_halves(w[0], mine, sib, m[0], v[0], c_idx, _row_block(mine.shape[0]), "adamw_" + n)
        big_out.append([r[None] for r in res])

    acc1_all, dlb_all, dgn_all = _all_gather_rows([acc1, dlb, dgn], "gather_small")
    gathered = [acc1_all, acc2_all, dln_all, dlb_all, dgn_all, dbs_all, dws_all]
    small, loss, dada_all = _small_finalize(
        gathered,
        _small_2d(b_ada, norm1_w, norm2_w, final_norm_w, v_ln_w, v_ln_b, lower_bounds, gn_w, b_s, w_s),
        _small_2d(m_b_ada, m_norm1_w, m_norm2_w, m_final_norm_w, m_v_ln_w, m_v_ln_b, m_lower_bounds, m_gn_w, m_b_s, m_w_s),
        _small_2d(v_b_ada, v_norm1_w, v_norm2_w, v_final_norm_w, v_v_ln_w, v_v_ln_b, v_lower_bounds, v_gn_w, v_b_s, v_w_s))
    small = [_small_original_shapes(d) for d in small]
    loss = loss.reshape(())

    ada_out = [o[None] for o in _ada_wgrad_adam(cact.T, dada_all, w_ada[0], m_w_ada[0], v_w_ada[0], chip_idx)]

    order_names = ['w_ada', 'b_ada', 'norm1_w', 'w_in', 'w_s', 'b_s', 'v_ln_w', 'v_ln_b', 'lower_bounds', 'gn_w',
                   'w_out', 'norm2_w', 'w_ffn_in', 'w_ffn_out', 'final_norm_w']
    big_idx = {'w_in': 0, 'w_out': 1, 'w_ffn_in': 2, 'w_ffn_out': 3}
    outs = [loss, grad_x.reshape(1, T, D)]
    for kind in range(4):
        for n in order_names:
            if n == 'w_ada':
                outs.append(ada_out[kind])
            elif n in big_idx:
                outs.append(big_out[big_idx[n]][kind])
            else:
                outs.append(small[kind][n])
    return tuple(outs)
```

```python
import jax
import jax.numpy as jnp
import numpy as np
from jax import lax
from jax.experimental import pallas as pl
from jax.experimental.pallas import tpu as pltpu

F32 = jnp.float32
BF16 = jnp.bfloat16
SDS = jax.ShapeDtypeStruct
MESH = pl.DeviceIdType.MESH
HIGHEST = lax.Precision.HIGHEST

D = 1024
DG = 512
DH = 512
NH = 4
HD = 128
BLK = 128
CH = 64
DFF = 2816
DIN = 3072
FFB = 1408
LEVELS = (64, 32, 16, 8, 4, 2)
HGRN_CHUNKS_PER_STEP = 8
GMLP_ROWS_PER_STEP = 1024
TOKEN_TILE = 256
PROJ_TILE = 1024
WGRAD_TOKENS = 2048
N_CHIPS = 4
N_DEV = 8
EPS = 1e-6
LR, B1, B2, AEPS, WD, STEP = 0.001, 0.9, 0.999, 1e-08, 0.01, 10

NT = (((1,), (1,)), ((), ()))
TN = (((0,), (0,)), ((), ()))


def _full(shape):
    nd = len(shape)
    return pl.BlockSpec(shape, lambda *_: (0,) * nd)


ADA_SH1, ADA_SC1, ADA_G1, ADA_SH2, ADA_SC2, ADA_G2 = range(6)


def _ada_part(k):
    return pl.BlockSpec((1, D), lambda *_: (0, k))


def _resident(shape):
    nd = len(shape)
    return pl.BlockSpec(shape, lambda *_: (0,) * nd, pipeline_mode=pl.Buffered(1))


def _arb(n=1):
    return pltpu.CompilerParams(dimension_semantics=("arbitrary",) * n)


def _dot(a, b, dims=None, precision=None):
    if dims is None:
        return jnp.dot(a, b, preferred_element_type=F32, precision=precision)
    return lax.dot_general(a, b, dims, preferred_element_type=F32, precision=precision)


def _sigmoid(x):
    return jax.nn.sigmoid(x)


def _gelu_parts(x):
    cdf = 0.5 * (1.0 + lax.erf(x * 0.7071067811865476))
    pdf = jnp.exp(-0.5 * x * x) * 0.3989422804014327
    return x * cdf, cdf + x * pdf


def _rms(x):
    return lax.rsqrt(jnp.mean(x * x, axis=-1, keepdims=True) + EPS)


def _rms_bwd(xhat, r, gw):
    return r * (gw - xhat * jnp.mean(xhat * gw, axis=-1, keepdims=True))


def _lower_bound(lbp_ref):
    l0, l1 = lbp_ref[0:1, :], lbp_ref[1:2, :]
    m = jnp.maximum(l0, l1)
    e0, e1 = jnp.exp(l0 - m), jnp.exp(l1 - m)
    return e0 / (e0 + e1), e1 / (e0 + e1)


def _gmlp_common(u, v, lnw, lnb, ws_ref, bst_ref):
    ug, dug = _gelu_parts(u)
    vg, dvg = _gelu_parts(v)
    mu = jnp.mean(vg, axis=-1, keepdims=True)
    vc = vg - mu
    rstd = lax.rsqrt(jnp.mean(vc * vc, axis=-1, keepdims=True) + EPS)
    vhat = vc * rstd
    vn = vhat * lnw + lnb
    vnb = vn.astype(BF16)
    mixed = []
    for h in range(NH):
        sl = slice(h * HD, (h + 1) * HD)
        mixed.append(_dot(ws_ref[h], vnb[:, sl]) + bst_ref[:, h:h + 1])
    return ug, dug, dvg, rstd, vhat, vnb, jnp.concatenate(mixed, axis=1)


def _hgrn_tables():
    t = np.arange(CH)[:, None]
    j = np.arange(CH)[None, :]
    blocks = [j <= t, j > t]
    masks = []
    for n in LEVELS:
        mid = t - t % n + n // 2
        blocks.append(np.where(t >= mid, (j >= mid) & (j <= t), (j > t) & (j < mid)))
        masks.append((t // n == j // n) & (t % n >= n // 2) & (j % n < n // 2))
    w = np.concatenate(blocks, axis=0).astype(np.float32)
    m = np.stack(masks).astype(np.float32)
    return (jnp.asarray(w, BF16), jnp.asarray(w.T, BF16), jnp.asarray(m), jnp.asarray(m + m.transpose(0, 2, 1)))


def _split_dot(w, x, parts):
    acc = None
    for _ in range(parts):
        piece = x.astype(BF16)
        term = _dot(w, piece)
        acc = term if acc is None else acc + term
        x = x - piece.astype(F32)
    return acc


def _hgrn_decays(f, w_ref):
    b = _split_dot(w_ref[0:CH, :], jnp.log(f), 3)
    row = lax.broadcasted_iota(jnp.int32, (CH, 1), 0)
    blocks = [jnp.exp(b), jnp.exp(b[CH - 1:CH, :] - b)]
    for n in LEVELS:
        up = (row & (n // 2)) != 0
        if n >= 8:
            ref = b.reshape(CH // n, n, DH)[:, n // 2 - 1:n // 2, :]
            ref = jnp.broadcast_to(ref, (CH // n, n, DH)).reshape(CH, DH)
            blocks.append(jnp.exp(jnp.where(up, b - ref, ref - b)))
        elif n == 4:
            r4 = row & 3
            two = jnp.where(r4 == 3, pltpu.roll(f, 1, 0) * f, 1.0)
            blocks.append(jnp.where(r4 == 0, pltpu.roll(f, CH - 1, 0), jnp.where(r4 == 2, f, two)))
        else:
            blocks.append(jnp.where(up, f, 1.0))
    return blocks


def _hgrn_gates(q, fl, lb, omlb, w_ref):
    sq = _sigmoid(q)
    qf = q * sq
    sig = _sigmoid(fl)
    f = lb + omlb * sig
    k = 1.0 - f
    return sq, qf, sig, f, k, _hgrn_decays(f, w_ref)


def _level_factor(e, li, sl, row, qh, kh):
    el = e[2 + li][:, sl]
    up = (row & (LEVELS[li] // 2)) != 0
    return el, up, el * jnp.where(up, qh, kh)


def _proj_hgrn_fwd(x, nw1, ada, w_in_b, lower_bounds, gn_w, tables, placed, axes):
    T = x.shape[0]
    nc = T // CH
    nch = min(HGRN_CHUNKS_PER_STEP, nc)
    steps = nc // nch
    w_st, _, masks, _ = tables
    nw = len(placed)
    pass_step = (13 * steps) // 16
    q0 = 2 * DG

    def body(*refs):
        x_ref, nw_ref, sc_ref, sh_ref, win_ref, lbp_ref, gn_ref, w_ref, m_ref = refs[:9]
        h_ref, p_ref, y_ref, o_ref, a_ref, st_ref = refs[9 + nw:15 + nw]
        s_scr, send_sems, recv_sems = refs[15 + 2 * nw:]
        gather = _WeightGather(refs[15 + nw:15 + 2 * nw], axes, send_sems, recv_sems)
        step = pl.program_id(0)
        xv = x_ref[...]
        hb = (((xv * _rms(xv)) * nw_ref[...]) * (1.0 + sc_ref[...]) + sh_ref[...]).astype(BF16)
        h_ref[...] = hb
        p_ref[...] = _dot(hb, win_ref[...])

        @pl.when(step == 0)
        def _():
            gather.start()
            s_scr[...] = jnp.zeros_like(s_scr)

        @pl.when(step == pass_step)
        def _():
            gather.forward()

        lb, omlb = _lower_bound(lbp_ref)
        row = lax.broadcasted_iota(jnp.int32, (CH, 1), 0)
        eye = lax.broadcasted_iota(jnp.int32, (CH, CH), 0) == lax.broadcasted_iota(jnp.int32, (CH, CH), 1)
        in_level = [m_ref[li] > 0.0 for li in range(len(LEVELS))]
        pre = []
        for ci in range(nch):
            rs = slice(ci * CH, (ci + 1) * CH)
            _, qf, _, _, k, e = _hgrn_gates(p_ref[rs, q0:q0 + DH], p_ref[rs, q0 + DH:q0 + 2 * DH], lb, omlb, w_ref)
            mats = []
            for h in range(NH):
                sl = slice(h * HD, (h + 1) * HD)
                qh, kh = qf[:, sl], k[:, sl]
                a = jnp.where(eye, jnp.sum(qh * kh, axis=-1, keepdims=True), 0.0)
                for li in range(len(LEVELS)):
                    _, _, y = _level_factor(e, li, sl, row, qh, kh)
                    yb = y.astype(BF16)
                    a = jnp.where(in_level[li], _dot(yb, yb, NT), a)
                a_ref[ci, h] = a
                mats.append(a.astype(BF16))
            eb = e[0]
            pre.append(((qf * eb).astype(BF16), eb[CH - 1:CH, :], (k * e[1]).astype(BF16), mats))
        for ci in range(nch):
            rs = slice(ci * CH, (ci + 1) * CH)
            qe, ebl, kd, mats = pre[ci]
            v = p_ref[rs, q0 + 2 * DH:q0 + 3 * DH]
            g = p_ref[rs, q0 + 3 * DH:q0 + 4 * DH]
            for h in range(NH):
                sl = slice(h * HD, (h + 1) * HD)
                st0 = s_scr[h]
                st_ref[ci, h] = st0
                vb = v[:, sl].astype(BF16)
                o = _dot(qe[:, sl], st0.astype(BF16), NT) + _dot(mats[h], vb)
                s_scr[h] = st0 * ebl[:, sl] + _dot(vb, kd[:, sl], TN)
                o_ref[rs, sl] = o
                gh = g[:, sl]
                y_ref[rs, sl] = (((o * _rms(o)) * gn_ref[...]) * (gh * _sigmoid(gh))).astype(BF16)

        @pl.when(step == steps - 1)
        def _():
            gather.finish()

    rows = nch * CH
    row = lambda c: (c, 0)
    anyspec = pl.BlockSpec(memory_space=pl.ANY)
    res = pl.pallas_call(
        body, grid=(steps,),
        in_specs=[pl.BlockSpec((rows, D), row), _full((1, D)), _ada_part(ADA_SC1), _ada_part(ADA_SH1), _resident((D, DIN)),
                  _full((2, DH)), _full((1, HD)), _full(w_st.shape), _full(masks.shape)] + [anyspec] * nw,
        out_specs=[pl.BlockSpec((rows, D), row), pl.BlockSpec((rows, DIN), row),
                   pl.BlockSpec((rows, DH), lambda c: (c, 1)),
                   pl.BlockSpec((rows, DH), row),
                   pl.BlockSpec((nch, NH, CH, CH), lambda c: (c, 0, 0, 0)),
                   pl.BlockSpec((nch, NH, HD, HD), lambda c: (c, 0, 0, 0))] + [anyspec] * nw,
        out_shape=[SDS((T, D), BF16), SDS((T, DIN), F32), SDS((T, D), BF16), SDS((T, DH), F32),
                   SDS((nc, NH, CH, CH), F32), SDS((nc, NH, HD, HD), F32)] + [SDS(a.shape, a.dtype) for a in placed],
        scratch_shapes=[pltpu.VMEM((NH, HD, HD), F32)] + _gather_sems(nw),
        input_output_aliases={9 + i: 6 + i for i in range(nw)},
        compiler_params=_arb(), name="proj_hgrn_fwd")(x, nw1, ada, ada, w_in_b, lower_bounds, gn_w, w_st, masks, *placed)
    return res[:6], res[6:]


def _token_local(x, ycat, tgt, g1, nw2, sc2, sh2, g2, fw, w_out_b, w_fi_b, w_fo_b, proj, ws_b, bst, lnw, lnb, tm):
    T = x.shape[0]
    inv_d = 1.0 / D

    def body(x_ref, yb_ref, t_ref, g1_ref, nw2_ref, sc2_ref, sh2_ref, g2_ref, fw_ref, wo_ref, wfi_ref, wfo_ref,
             u_ref, v_ref, ws_ref, bst_ref, lnw_ref, lnb_ref,
             dy_ref, dx1_ref, h2_ref, act_ref, dff_ref, dgu_ref, dmix_ref, acc_ref, ya_ref):
        @pl.when(pl.program_id(0) == 0)
        def _():
            acc_ref[...] = jnp.zeros_like(acc_ref)

        def acc(row, val):
            acc_ref[row:row + 1, :] += jnp.sum(val, axis=0, keepdims=True)

        for bi in range(tm // BLK):
            rs = slice(bi * BLK, (bi + 1) * BLK)
            ug, _, _, _, _, _, mixed = _gmlp_common(u_ref[rs, :], v_ref[rs, :], lnw_ref[...], lnb_ref[...], ws_ref, bst_ref)
            ya_ref[rs, :] = (ug * mixed).astype(BF16)
        g1v, g2v = g1_ref[...], g2_ref[...]
        mix = _dot(ya_ref[...], wo_ref[0:DG, :]) + _dot(yb_ref[...], wo_ref[DG:D, :])
        x1 = x_ref[...] + g1v * mix
        r2 = _rms(x1)
        xh2 = x1 * r2
        n2 = xh2 * nw2_ref[...]
        osc2 = 1.0 + sc2_ref[...]
        h2b = (n2 * osc2 + sh2_ref[...]).astype(BF16)
        h2_ref[...] = h2b
        ff = jnp.zeros((tm, D), F32)
        saved = []
        for kb in range(DFF // FFB):
            gate = _dot(h2b, wfi_ref[:, kb * FFB:(kb + 1) * FFB])
            up = _dot(h2b, wfi_ref[:, DFF + kb * FFB:DFF + (kb + 1) * FFB])
            sg = _sigmoid(gate)
            actb = (gate * sg * up).astype(BF16)
            act_ref[:, kb * FFB:(kb + 1) * FFB] = actb
            ff = ff + _dot(actb, wfo_ref[kb * FFB:(kb + 1) * FFB, :])
            saved.append((gate, up, sg))
        x2 = x1 + g2v * ff
        r3 = _rms(x2)
        xh3 = x2 * r3
        err = xh3 * fw_ref[...] - t_ref[...]
        acc(6, (0.5 * inv_d) * err * err)
        dy = err * inv_d
        acc(4, dy * xh3)
        dx2 = _rms_bwd(xh3, r3, dy * fw_ref[...])
        acc(0, dx2 * ff)
        dffb = (dx2 * g2v).astype(BF16)
        dff_ref[...] = dffb
        dh2 = jnp.zeros((tm, D), F32)
        for kb in range(DFF // FFB):
            gate, up, sg = saved[kb]
            da = _dot(dffb, wfo_ref[kb * FFB:(kb + 1) * FFB, :], NT)
            dgate = (da * up * (sg * (1.0 + gate * (1.0 - sg)))).astype(BF16)
            dup = (da * gate * sg).astype(BF16)
            dgu_ref[:, kb * FFB:(kb + 1) * FFB] = dgate
            dgu_ref[:, DFF + kb * FFB:DFF + (kb + 1) * FFB] = dup
            dh2 = dh2 + _dot(dgate, wfi_ref[:, kb * FFB:(kb + 1) * FFB], NT)
            dh2 = dh2 + _dot(dup, wfi_ref[:, DFF + kb * FFB:DFF + (kb + 1) * FFB], NT)
        acc(2, dh2)
        acc(1, dh2 * n2)
        dn2 = dh2 * osc2
        acc(3, dn2 * xh2)
        dx1 = dx2 + _rms_bwd(xh2, r2, dn2 * nw2_ref[...])
        acc(5, dx1 * mix)
        dmixb = (dx1 * g1v).astype(BF16)
        dmix_ref[...] = dmixb
        dy_ref[...] = _dot(dmixb, wo_ref[...], NT)
        dx1_ref[...] = dx1

    row = lambda i: (i, 0)
    vec = _full((1, D))
    half = lambda j: pl.BlockSpec((tm, DG), lambda i: (i, j))
    return pl.pallas_call(
        body, grid=(T // tm,),
        in_specs=[pl.BlockSpec((tm, D), row), half(1), pl.BlockSpec((tm, D), row),
                  _ada_part(ADA_G1), vec, _ada_part(ADA_SC2), _ada_part(ADA_SH2), _ada_part(ADA_G2), vec,
                  _resident((D, D)), _resident((D, 2 * DFF)), _resident((DFF, D)),
                  half(0), half(1), _full((NH, BLK, BLK)), _full((BLK, NH)), _full((1, DG)), _full((1, DG))],
        out_specs=[pl.BlockSpec((tm, D), row), pl.BlockSpec((tm, D), row), pl.BlockSpec((tm, D), row),
                   pl.BlockSpec((tm, DFF), row), pl.BlockSpec((tm, D), row), pl.BlockSpec((tm, 2 * DFF), row),
                   pl.BlockSpec((tm, D), row), _full((8, D)), half(0)],
        out_shape=[SDS((T, D), F32), SDS((T, D), F32), SDS((T, D), BF16), SDS((T, DFF), BF16), SDS((T, D), BF16),
                   SDS((T, 2 * DFF), BF16), SDS((T, D), BF16), SDS((8, D), F32), SDS((T, D), BF16)],
        input_output_aliases={1: 8},
        compiler_params=_arb(), name="token_local")(x, ycat, tgt, g1, nw2, sc2, sh2, g2, fw, w_out_b, w_fi_b, w_fo_b,
                                                    proj, proj, ws_b, bst, lnw, lnb)


def _gmlp_bwd(proj, dycat, ws_b, bst, lnw, lnb, grads):
    T = proj.shape[0]
    rows = min(GMLP_ROWS_PER_STEP, T)
    nb = T // rows
    nw = len(grads)

    def body(*refs):
        u_ref, v_ref, dy_ref, ws_ref, bst_ref, lnw_ref, lnb_ref = refs[:7]
        dp_ref, dws_ref, dbs_ref, dln_ref = refs[7 + nw:11 + nw]
        dbs_acc, send_sems, recv_sems = refs[11 + 2 * nw:]
        exchange = _CoreExchange(refs[7:7 + nw], refs[11 + nw:11 + 2 * nw], send_sems, recv_sems)
        i = pl.program_id(0)

        @pl.when(i == 0)
        def _():
            exchange.start()
            dws_ref[...] = jnp.zeros_like(dws_ref)
            dln_ref[...] = jnp.zeros_like(dln_ref)
            dbs_acc[...] = jnp.zeros_like(dbs_acc)

        r = lax.broadcasted_iota(jnp.int32, (BLK, BLK), 0) // CH
        c = lax.broadcasted_iota(jnp.int32, (BLK, BLK), 1) // CH
        for bi in range(rows // BLK):
            rs = slice(bi * BLK, (bi + 1) * BLK)
            ug, dug, dvg, rstd, vhat, vnb, mixed = _gmlp_common(
                u_ref[rs, :], v_ref[rs, :], lnw_ref[...], lnb_ref[...], ws_ref, bst_ref)
            dya = dy_ref[rs, :]
            dp_ref[rs, 0:DG] = (dya * mixed * dug).astype(BF16)
            dmixed = dya * ug
            dbs_acc[...] += dmixed
            dmb = dmixed.astype(BF16)
            dvn = []
            for h in range(NH):
                sl = slice(h * HD, (h + 1) * HD)
                dws_ref[h * BLK:(h + 1) * BLK, :] += jnp.where(r >= c, _dot(dmb[:, sl], vnb[:, sl], NT), 0.0)
                dvn.append(_dot(ws_ref[h], dmb[:, sl], TN))
            dvn = jnp.concatenate(dvn, axis=1)
            dln_ref[0:1, :] += jnp.sum(dvn * vhat, axis=0, keepdims=True)
            dln_ref[1:2, :] += jnp.sum(dvn, axis=0, keepdims=True)
            dvh = dvn * lnw_ref[...]
            dvgel = rstd * (dvh - jnp.mean(dvh, axis=-1, keepdims=True) - vhat * jnp.mean(dvh * vhat, axis=-1, keepdims=True))
            dp_ref[rs, DG:2 * DG] = (dvgel * dvg).astype(BF16)

        @pl.when(i == nb - 1)
        def _():
            head = lax.broadcasted_iota(jnp.int32, (8, BLK), 0)
            ones = jnp.ones((8, HD), F32)
            out = jnp.zeros((8, BLK), F32)
            for h in range(NH):
                sums = _dot(ones, dbs_acc[:, h * HD:(h + 1) * HD], NT, precision=HIGHEST)
                out = out + jnp.where(head == h, sums, 0.0)
            dbs_ref[...] = out
            exchange.finish()

    anyspec = pl.BlockSpec(memory_space=pl.ANY)
    res = pl.pallas_call(
        body, grid=(nb,),
        in_specs=[pl.BlockSpec((rows, DG), lambda i: (i, 0)), pl.BlockSpec((rows, DG), lambda i: (i, 1)),
                  pl.BlockSpec((rows, DG), lambda i: (i, 0)),
                  _full((NH, BLK, BLK)), _full((BLK, NH)), _full((1, DG)), _full((1, DG))] + [anyspec] * nw,
        out_specs=[pl.BlockSpec((rows, 2 * DG), lambda i: (i, 2)), _full((NH * BLK, BLK)), _full((8, BLK)), _full((8, DG))]
        + [anyspec] * nw,
        out_shape=[SDS((T, DIN), BF16), SDS((NH * BLK, BLK), F32), SDS((8, BLK), F32), SDS((8, DG), F32)]
        + _core_exchange_shapes(grads),
        scratch_shapes=[pltpu.VMEM((BLK, DG), F32)] + _core_exchange_sems(nw),
        compiler_params=_arb(), name="gmlp_bwd")(proj, proj, dycat, ws_b, bst, lnw, lnb, *grads)
    return res[:4], res[4:]


def _hgrn_bwd(proj, o_pre, a_all, st_all, dycat, lower_bounds, gn_w, dproj, tables, sums, row_blocks):
    T = proj.shape[0]
    nc = T // CH
    nch = min(HGRN_CHUNKS_PER_STEP, nc)
    steps = nc // nch
    w_st, w_st_t, _, masks_sym = tables
    n_lev = len(LEVELS)
    nw, nr = len(sums), len(row_blocks)

    def body(*refs):
        q_ref, f_ref, i_ref, g_ref, o_ref, a_ref, st_ref, dy_ref, lbp_ref, gn_ref, w_ref, wt_ref, ms_ref = refs[:13]
        n_in = 14 + nw + nr
        dp_ref, dlb_ref, dgn_ref = refs[n_in:n_in + 3]
        ds_scr, dx_scr = refs[n_in + 3 + nw + nr:n_in + 5 + nw + nr]
        sems = refs[n_in + 5 + nw + nr:]
        exchange = _ChipExchange(refs[14:14 + nw], refs[n_in + 3:n_in + 3 + nw], *sems[:2])
        rows_gather = _RowGather(refs[14 + nw:n_in], refs[n_in + 3 + nw:n_in + 3 + nw + nr], *sems[2:])
        i = pl.program_id(0)

        @pl.when(i == 0)
        def _():
            rows_gather.start()
            exchange.start()
            ds_scr[...] = jnp.zeros_like(ds_scr)
            dlb_ref[...] = jnp.zeros_like(dlb_ref)
            dgn_ref[...] = jnp.zeros_like(dgn_ref)

        @pl.when(i == steps // 2)
        def _():
            rows_gather.forward()

        lb, omlb = _lower_bound(lbp_ref)
        row = lax.broadcasted_iota(jnp.int32, (CH, 1), 0)
        eye = lax.broadcasted_iota(jnp.int32, (CH, CH), 0) == lax.broadcasted_iota(jnp.int32, (CH, CH), 1)
        lower = lax.broadcasted_iota(jnp.int32, (CH, CH), 0) > lax.broadcasted_iota(jnp.int32, (CH, CH), 1)
        dgn = jnp.zeros((1, HD), F32)
        pre = []
        for ci in range(nch):
            rs = slice(ci * CH, (ci + 1) * CH)
            q = q_ref[rs, :]
            v = i_ref[rs, :]
            g = g_ref[rs, :]
            sq, qf, sig, f, k, e = _hgrn_gates(q, f_ref[rs, :], lb, omlb, w_ref)
            eb = e[0]
            ekd = e[1]
            kd = k * ekd
            qe = qf * eb
            dob_h, dqe_h, dqf_h, dki_h, dv_h, dg_h = [], [], [], [], [], []
            for h in range(NH):
                sl = slice(h * HD, (h + 1) * HD)
                o = o_ref[rs, sl]
                ro = _rms(o)
                oh = o * ro
                gh = g[:, sl]
                sg = _sigmoid(gh)
                dyb = dy_ref[rs, sl]
                dg_h.append(dyb * (oh * gn_ref[...]) * (sg * (1.0 + gh * (1.0 - sg))))
                don = dyb * (gh * sg)
                dgn = dgn + jnp.sum(don * oh, axis=0, keepdims=True)
                dob = _rms_bwd(oh, ro, don * gn_ref[...]).astype(BF16)
                vb = v[:, sl].astype(BF16)
                qh, kh = qf[:, sl], k[:, sl]
                dqe = _dot(dob, st_ref[ci, h].astype(BF16))
                da = _dot(dob, vb, NT)
                ddiag = jnp.sum(jnp.where(eye, da, 0.0), axis=-1, keepdims=True)
                dsym = jnp.where(lower, da, _dot(vb, dob, NT))
                upper_part = jnp.zeros((CH, HD), F32)
                both = jnp.zeros((CH, HD), F32)
                for li in range(n_lev):
                    el, up, y = _level_factor(e, li, sl, row, qh, kh)
                    dyv = _dot((ms_ref[li] * dsym).astype(BF16), y.astype(BF16))
                    dx_scr[ci, (2 + li) * CH:(3 + li) * CH, sl] = dyv * y
                    dye = dyv * el
                    upper_part = upper_part + jnp.where(up, dye, 0.0)
                    both = both + dye
                dob_h.append(dob)
                dqe_h.append(dqe)
                dqf_h.append(dqe * eb[:, sl] + ddiag * kh + upper_part)
                dki_h.append(ddiag * qh + (both - upper_part))
                dv_h.append(_dot(a_ref[ci, h].astype(BF16), dob, TN))
            dp_ref[rs, 0:DH] = (jnp.concatenate(dqf_h, axis=1) * (sq * (1.0 + q * (1.0 - sq)))).astype(BF16)
            dp_ref[rs, 3 * DH:4 * DH] = jnp.concatenate(dg_h, axis=1).astype(BF16)
            pre.append((v, sig, f, eb, ekd, kd, qe, dob_h, jnp.concatenate(dqe_h, axis=1), dki_h, dv_h))
        dgn_ref[0:1, :] += dgn
        for ci in reversed(range(nch)):
            rs = slice(ci * CH, (ci + 1) * CH)
            v, sig, f, eb, ekd, kd, qe, dob_h, dqe, dki_h, dv_h = pre[ci]
            ebl = eb[CH - 1:CH, :]
            dbl_h, dkd_h, dv2_h = [], [], []
            for h in range(NH):
                sl = slice(h * HD, (h + 1) * HD)
                dst1 = ds_scr[h]
                dst1b = dst1.astype(BF16)
                ds_scr[h] = dst1 * ebl[:, sl] + _dot(dob_h[h], qe[:, sl].astype(BF16), TN)
                dbl_h.append(ebl[:, sl] * jnp.sum(st_ref[ci, h] * dst1, axis=0, keepdims=True))
                dkd_h.append(_dot(v[:, sl].astype(BF16), dst1b))
                dv2_h.append(dv_h[h] + _dot(kd[:, sl].astype(BF16), dst1b, NT))
            dkd = jnp.concatenate(dkd_h, axis=1)
            dx_scr[ci, 0:CH, :] = dqe * qe + jnp.where(row == CH - 1, jnp.concatenate(dbl_h, axis=1), 0.0)
            dx_scr[ci, CH:2 * CH, :] = dkd * kd
            dlf = _split_dot(wt_ref[...], dx_scr[ci], 2)
            df = dlf / f - (dkd * ekd + jnp.concatenate(dki_h, axis=1))
            dlb_ref[0:1, :] += jnp.sum(df * (1.0 - sig), axis=0, keepdims=True)
            dp_ref[rs, DH:2 * DH] = (df * omlb * sig * (1.0 - sig)).astype(BF16)
            dp_ref[rs, 2 * DH:3 * DH] = jnp.concatenate(dv2_h, axis=1).astype(BF16)

        @pl.when(i == steps - 1)
        def _():
            gl = dlb_ref[0:1, :] * lb * omlb
            dlb_ref[0:1, :] = gl
            dlb_ref[1:2, :] = -gl
            exchange.finish()
            rows_gather.finish()

    rev = lambda j: pl.BlockSpec((nch * CH, DH), lambda c: (steps - 1 - c, j))
    anyspec = pl.BlockSpec(memory_space=pl.ANY)
    res = pl.pallas_call(
        body, grid=(steps,),
        in_specs=[rev(2), rev(3), rev(4), rev(5), rev(0),
                  pl.BlockSpec((nch, NH, CH, CH), lambda c: (steps - 1 - c, 0, 0, 0)),
                  pl.BlockSpec((nch, NH, HD, HD), lambda c: (steps - 1 - c, 0, 0, 0)),
                  rev(1), _full((2, DH)), _full((1, HD)),
                  _full(w_st.shape), _full(w_st_t.shape), _full(masks_sym.shape),
                  anyspec] + [anyspec] * (nw + nr),
        out_specs=[pl.BlockSpec((nch * CH, 4 * DH), lambda c: (steps - 1 - c, 0)), _full((8, DH)), _full((8, HD))]
        + [anyspec] * (nw + nr),
        out_shape=[SDS((T, DIN), BF16), SDS((8, DH), F32), SDS((8, HD), F32)] + _slot_shapes(sums)
        + _gather_rows_shapes(row_blocks),
        scratch_shapes=[pltpu.VMEM((NH, HD, HD), F32), pltpu.VMEM((nch, (2 + n_lev) * CH, DH), F32)]
        + _exchange_sems(nw) + _gather_rows_sems(nr),
        input_output_aliases={13: 0},
        compiler_params=_arb(), name="hgrn_bwd")(proj, proj, proj, proj, o_pre, a_all, st_all, dycat, lower_bounds, gn_w,
                                                 w_st, w_st_t, masks_sym, dproj, *sums, *row_blocks)
    return res[:3], res[3:3 + nw], res[3 + nw:]


def _proj_in_bwd(dproj, x, dx1, nw, sc, w_in_b, tm, sums):
    T = x.shape[0]
    ns = len(sums)
    steps = T // tm

    def body(*refs):
        dp_ref, x_ref, dx1_ref, nw_ref, sc_ref, w_ref = refs[:6]
        gx_ref, acc_ref = refs[6 + ns:8 + ns]
        exchange = _ChipExchange(refs[6:6 + ns], refs[8 + ns:8 + 2 * ns], *refs[8 + 2 * ns:])

        @pl.when(pl.program_id(0) == 0)
        def _():
            exchange.start()
            acc_ref[...] = jnp.zeros_like(acc_ref)

        dh = _dot(dp_ref[:, 0:4 * DH], w_ref[:, 2 * DG:DIN], NT) + _dot(dp_ref[:, 4 * DH:DIN], w_ref[:, 0:2 * DG], NT)
        xv = x_ref[...]
        r = _rms(xv)
        xh = xv * r
        n1 = xh * nw_ref[...]
        acc_ref[0:1, :] += jnp.sum(dh, axis=0, keepdims=True)
        acc_ref[1:2, :] += jnp.sum(dh * n1, axis=0, keepdims=True)
        dn = dh * (1.0 + sc_ref[...])
        acc_ref[2:3, :] += jnp.sum(dn * xh, axis=0, keepdims=True)
        gx_ref[...] = dx1_ref[...] + _rms_bwd(xh, r, dn * nw_ref[...])

        @pl.when(pl.program_id(0) == steps - 1)
        def _():
            exchange.finish()

    row = lambda i: (i, 0)
    anyspec = pl.BlockSpec(memory_space=pl.ANY)
    res = pl.pallas_call(
        body, grid=(steps,),
        in_specs=[pl.BlockSpec((tm, DIN), row), pl.BlockSpec((tm, D), row), pl.BlockSpec((tm, D), row),
                  _full((1, D)), _ada_part(ADA_SC1), _resident((D, DIN))] + [anyspec] * ns,
        out_specs=[pl.BlockSpec((tm, D), row), _full((8, D))] + [anyspec] * ns,
        out_shape=[SDS((T, D), F32), SDS((8, D), F32)] + _slot_shapes(sums),
        scratch_shapes=_exchange_sems(ns),
        compiler_params=_arb(), name="proj_in_bwd")(dproj, x, dx1, nw, sc, w_in_b, *sums)
    return res[:2], res[2:]


def _wgrad(a, b, bk, bn, tt, name, bf16_copy=False):
    T, K = a.shape
    N = b.shape[1]
    nn, nk, nt = N // bn, K // bk, T // tt
    bmap = lambda n, k, t: (t, n)

    def body(a_ref, b_ref, o_ref, *copy_ref):
        @pl.when(pl.program_id(2) == 0)
        def _():
            o_ref[...] = jnp.zeros_like(o_ref)

        o_ref[0] += _dot(a_ref[...], b_ref[...], TN)

        if bf16_copy:
            @pl.when(pl.program_id(2) == nt - 1)
            def _():
                copy_ref[0][...] = o_ref[...].astype(BF16)

    ospec = pl.BlockSpec((1, bk, bn), lambda n, k, t: (n, k, 0))
    return pl.pallas_call(
        body, grid=(nn, nk, nt),
        in_specs=[pl.BlockSpec((tt, bk), lambda n, k, t: (t, k)), pl.BlockSpec((tt, bn), bmap)],
        out_specs=[ospec, ospec] if bf16_copy else ospec,
        out_shape=[SDS((nn, K, bn), F32), SDS((nn, K, bn), BF16)] if bf16_copy else SDS((nn, K, bn), F32),
        compiler_params=_arb(3), name=name)(a, b)


def _adam_math(w, g, m, v):
    m = B1 * m + (1.0 - B1) * g
    v = B2 * v + (1.0 - B2) * (g * g)
    m_hat = m / (1.0 - B1 ** STEP)
    v_hat = v / (1.0 - B2 ** STEP)
    return -LR * (m_hat / (jnp.sqrt(v_hat) + AEPS) + WD * w), m, v


def _adamw_halves(w, mine, sibling, m, v, c_idx, rb, name):
    R, C = w.shape
    nb = (R // 2) // rb

    def body(c_ref, w_ref, a_ref, b_ref, m_ref, v_ref, g_out, d_out, m_out, v_out):
        g = jnp.where(pl.program_id(0) == c_ref[0], a_ref[...], b_ref[...])
        g_out[...] = g
        d_out[...], m_out[...], v_out[...] = _adam_math(w_ref[...], g, m_ref[...], v_ref[...])

    whole = pl.BlockSpec((rb, C), lambda hh, i, cr: (hh * nb + i, 0))
    half = pl.BlockSpec((rb, C), lambda hh, i, cr: (i, 0))
    return pl.pallas_call(
        body,
        grid_spec=pltpu.PrefetchScalarGridSpec(
            num_scalar_prefetch=1, grid=(2, nb), in_specs=[whole, half, half, whole, whole], out_specs=[whole] * 4),
        out_shape=[SDS((R, C), F32)] * 4, compiler_params=_arb(2), name=name)(c_idx, w, mine, sibling, m, v)


def _ada_wgrad_adam(cact_t, dada_all, w, m, v, chip_idx):
    R, C = w.shape
    rb = 256

    def body(j_ref, c_ref, d_ref, w_ref, m_ref, v_ref, g_out, d_out, m_out, v_out):
        g = _dot(c_ref[...], d_ref[...], precision=HIGHEST)
        g_out[...] = g
        d_out[...], m_out[...], v_out[...] = _adam_math(w_ref[...], g, m_ref[...], v_ref[...])

    spec = pl.BlockSpec((rb, C), lambda i, j: (i, 0))
    return pl.pallas_call(
        body,
        grid_spec=pltpu.PrefetchScalarGridSpec(
            num_scalar_prefetch=1, grid=(R // rb,),
            in_specs=[pl.BlockSpec((rb, N_DEV), lambda i, j: (i, 0)), pl.BlockSpec((N_DEV, C), lambda i, j: (0, j[0])),
                      spec, spec, spec],
            out_specs=[spec] * 4),
        out_shape=[SDS((R, C), F32)] * 4,
        compiler_params=_arb(), name="ada_wgrad_adam")(chip_idx, cact_t, dada_all, w, m, v)


SMALL_NAMES = ('b_ada', 'norm1_w', 'norm2_w', 'final_norm_w', 'v_ln_w', 'v_ln_b', 'lower_bounds', 'gn_w', 'b_s', 'w_s')


def _small_finalize(gathered, params, moms, vels):
    n_in = len(gathered)

    def body(*refs):
        acc1, acc2, dln, dlb, dgn, dbs, dws = refs[:n_in]
        prm = [dict(zip(SMALL_NAMES, refs[n_in + k * 10:n_in + (k + 1) * 10])) for k in range(3)]
        outs = [dict(zip(SMALL_NAMES, refs[n_in + 30 + k * 10:n_in + 30 + (k + 1) * 10])) for k in range(4)]
        loss_ref, dada_ref = refs[n_in + 70:n_in + 72]

        def dev_sum(ref, first, n):
            per = ref.shape[0] // N_DEV
            g = ref[first:first + n, :]
            for dev in range(1, N_DEV):
                g = g + ref[dev * per + first:dev * per + first + n, :]
            return g

        def update(n, g, cols=slice(None)):
            outs[0][n][:, cols] = g
            outs[1][n][:, cols], outs[2][n][:, cols], outs[3][n][:, cols] = _adam_math(
                prm[0][n][:, cols], g, prm[1][n][:, cols], prm[2][n][:, cols])

        ada_rows = ((acc1, 0), (acc1, 1), (acc2, 5), (acc2, 2), (acc2, 1), (acc2, 0))
        for k, (ref, r) in enumerate(ada_rows):
            update('b_ada', dev_sum(ref, r, 1), slice(k * D, (k + 1) * D))
            for dev in range(N_DEV):
                dada_ref[dev:dev + 1, k * D:(k + 1) * D] = ref[8 * dev + r:8 * dev + r + 1, :]
        update('norm1_w', dev_sum(acc1, 2, 1))
        update('norm2_w', dev_sum(acc2, 3, 1))
        update('final_norm_w', dev_sum(acc2, 4, 1))
        update('v_ln_w', dev_sum(dln, 0, 1))
        update('v_ln_b', dev_sum(dln, 1, 1))
        update('lower_bounds', dev_sum(dlb, 0, 2))
        update('gn_w', dev_sum(dgn, 0, 1))
        update('b_s', dev_sum(dbs, 0, NH))
        update('w_s', dev_sum(dws, 0, NH * BLK))
        loss_ref[...] = jnp.sum(dev_sum(acc2, 6, 1), axis=-1, keepdims=True)

    shapes = [SDS(params[n].shape, F32) for n in SMALL_NAMES]
    res = pl.pallas_call(
        body, out_shape=shapes * 4 + [SDS((1, 1), F32), SDS((N_DEV, 6 * D), F32)], name="small_finalize")(
            *gathered, *[d[n] for d in (params, moms, vels) for n in SMALL_NAMES])
    return [dict(zip(SMALL_NAMES, res[k * 10:(k + 1) * 10])) for k in range(4)], res[40], res[41]


def _position():
    x, y, c = lax.axis_index("x"), lax.axis_index("y"), lax.axis_index("c")
    return x, y, c


def _chip_at(x, y, r):
    return (x ^ (r >> 1), y ^ (r & 1))


class _RowGather:
    def __init__(self, ins, outs, send_sems, recv_sems, local_sems):
        self.ins, self.outs = ins, outs
        self.send_sems, self.recv_sems, self.local_sems = send_sems, recv_sems, local_sems
        self.x, self.y, self.c = _position()
        self.me, self.sibling = (self.x, self.y, self.c), (self.x, self.y, 1 - self.c)
        self.chips = [_chip_at(self.x, self.y, r) for r in (1, 2, 3)]

    def _rows(self, b, px, py, pc):
        m_per = self.ins[b].shape[0]
        return self.outs[b].at[pl.ds((4 * px + 2 * py + pc) * m_per, m_per), :]

    def _copy(self, b, k, blk, to, from_input=False):
        return pltpu.make_async_remote_copy(
            src_ref=self.ins[b] if from_input else self._rows(b, *blk), dst_ref=self._rows(b, *blk),
            send_sem=self.send_sems.at[7 * b + k], recv_sem=self.recv_sems.at[7 * b + k],
            device_id=to, device_id_type=MESH)

    def _local(self, b):
        return pltpu.make_async_copy(self.ins[b], self._rows(b, *self.me), self.local_sems.at[b])

    def _first(self, b):
        c = self.c
        return [self._copy(b, 0, self.me, self.sibling, from_input=True)] + [
            self._copy(b, 1 + j, self.me, (*chip, c), from_input=True) for j, chip in enumerate(self.chips)]

    def start(self):
        for b in range(len(self.ins)):
            self._local(b).start()
            for cp in self._first(b):
                cp.start()

    def forward(self):
        for b in range(len(self.ins)):
            for j, chip in enumerate(self.chips):
                self._copy(b, 1 + j, (*chip, self.c), self.me).wait_recv()
                self._copy(b, 4 + j, (*chip, self.c), self.sibling).start()

    def finish(self):
        for b in range(len(self.ins)):
            self._copy(b, 0, self.sibling, self.me).wait_recv()
            for j, chip in enumerate(self.chips):
                self._copy(b, 4 + j, (*chip, 1 - self.c), self.me).wait_recv()
        for b in range(len(self.ins)):
            for cp in self._first(b):
                cp.wait_send()
            for j, chip in enumerate(self.chips):
                self._copy(b, 4 + j, (*chip, self.c), self.sibling).wait_send()
            self._local(b).wait()


def _gather_rows(ins, outs, send_sems, recv_sems, local_sems, after_issue=None):
    g = _RowGather(ins, outs, send_sems, recv_sems, local_sems)
    g.start()
    if after_issue is not None:
        after_issue()
    g.forward()
    g.finish()


def _gather_rows_shapes(blocks):
    return [SDS((N_DEV * b.shape[0], b.shape[1]), b.dtype) for b in blocks]


def _gather_rows_sems(nb):
    return [pltpu.SemaphoreType.DMA((7 * nb,)), pltpu.SemaphoreType.DMA((7 * nb,)), pltpu.SemaphoreType.DMA((nb,))]


def _all_gather_rows(blocks, name):
    nb = len(blocks)

    def body(*refs):
        _gather_rows(refs[:nb], refs[nb:2 * nb], *refs[2 * nb:])

    vmem = pl.BlockSpec(memory_space=pltpu.VMEM)
    return pl.pallas_call(
        body, out_shape=_gather_rows_shapes(blocks), in_specs=[vmem] * nb, out_specs=[vmem] * nb,
        scratch_shapes=_gather_rows_sems(nb), name=name)(*blocks)


def _place_shard(w_shard, axis, chip_idx, name):
    R, C = w_shard.shape
    rb = _row_block(R)
    nb = R // rb
    full = (R * N_CHIPS, C) if axis == 0 else (R, C * N_CHIPS)
    omap = (lambda i, j: (j[0] * nb + i, 0)) if axis == 0 else (lambda i, j: (i, j[0]))

    def body(j_ref, w_ref, o_ref):
        o_ref[...] = w_ref[...].astype(BF16)

    return pl.pallas_call(
        body,
        grid_spec=pltpu.PrefetchScalarGridSpec(
            num_scalar_prefetch=1, grid=(nb,), in_specs=[pl.BlockSpec((rb, C), lambda i, j: (i, 0))],
            out_specs=pl.BlockSpec((rb, C), omap)),
        out_shape=SDS(full, BF16), compiler_params=_arb(), name=name)(chip_idx, w_shard)


class _WeightGather:
    def __init__(self, refs, axes, send_sems, recv_sems):
        self.refs, self.axes, self.send_sems, self.recv_sems = refs, axes, send_sems, recv_sems
        self.x, self.y, self.c = _position()
        self.j = 2 * self.x + self.y
        self.n = 3 * len(refs)

    def _half(self, w, chip_idx, half):
        ref, axis = self.refs[w], self.axes[w]
        if axis == 0:
            size = ref.shape[0] // N_CHIPS
            return ref.at[pl.ds(chip_idx * size + half * (size // 2), size // 2), :]
        size = ref.shape[1] // N_CHIPS
        rows = ref.shape[0] // 2
        return ref.at[pl.ds(half * rows, rows), pl.ds(chip_idx * size, size)]

    def _ici(self, w, r, chip_idx):
        k = 3 * w + r - 1
        piece = self._half(w, chip_idx, self.c)
        return pltpu.make_async_remote_copy(
            src_ref=piece, dst_ref=piece, send_sem=self.send_sems.at[k], recv_sem=self.recv_sems.at[k],
            device_id=(*_chip_at(self.x, self.y, r), self.c), device_id_type=MESH)

    def _d2d(self, w, r, half):
        k = self.n + 3 * w + r - 1
        piece = self._half(w, self.j ^ r, half)
        return pltpu.make_async_remote_copy(
            src_ref=piece, dst_ref=piece, send_sem=self.send_sems.at[k], recv_sem=self.recv_sems.at[k],
            device_id=(self.x, self.y, 1 - self.c), device_id_type=MESH)

    def _each(self):
        return [(w, r) for w in range(len(self.refs)) for r in (1, 2, 3)]

    def start(self):
        for w, r in self._each():
            self._ici(w, r, self.j).start()

    def forward(self):
        for w, r in self._each():
            self._ici(w, r, self.j ^ r).wait_recv()
            self._d2d(w, r, self.c).start()

    def finish(self):
        for w, r in self._each():
            self._ici(w, r, self.j).wait_send()
            self._d2d(w, r, self.c).wait_send()
            self._d2d(w, r, 1 - self.c).wait_recv()


def _gather_sems(n_weights):
    return [pltpu.SemaphoreType.DMA((6 * n_weights,)), pltpu.SemaphoreType.DMA((6 * n_weights,))]


def _gather_w_in_and_ada(placed, axis, c_block, w_ada):
    n = w_ada.shape[1]

    def body(w_any, c_ref, wada_ref, w_out, call_ref, cact_ref, pall_ref, p_scr, *sems):
        g = _WeightGather([w_out], [axis], *sems[:2])
        _gather_rows([c_ref], [call_ref], *sems[2:5], after_issue=g.start)
        pick = (lax.broadcasted_iota(jnp.int32, (N_DEV, N_DEV * 8), 1)
                == 8 * lax.broadcasted_iota(jnp.int32, (N_DEV, N_DEV * 8), 0)).astype(F32)
        cv = _dot(pick, call_ref[...], precision=HIGHEST)
        ca = cv * _sigmoid(cv)
        cact_ref[...] = ca
        p_scr[...] = _dot(ca, wada_ref[...], precision=HIGHEST)
        products = _RowGather([p_scr], [pall_ref], *sems[5:])
        products.start()
        g.forward()
        products.forward()
        products.finish()
        g.finish()

    anyspec = pl.BlockSpec(memory_space=pl.ANY)
    vmem = pl.BlockSpec(memory_space=pltpu.VMEM)
    rows = N_DEV * c_block.shape[0]
    res = pl.pallas_call(
        body, out_shape=[SDS(placed.shape, placed.dtype), SDS((rows, D), F32), SDS((N_DEV, D), F32), SDS((rows, n), F32)],
        in_specs=[anyspec, vmem, vmem], out_specs=[anyspec, vmem, vmem, vmem],
        scratch_shapes=[pltpu.VMEM((N_DEV, n), F32)] + _gather_sems(1) + _gather_rows_sems(1) + _gather_rows_sems(1),
        input_output_aliases={0: 0}, name="gather_w_in_and_ada")(placed, c_block, w_ada)
    return res[0], res[2], res[3]


class _ChipExchange:
    def __init__(self, ins, outs, send_sems, recv_sems):
        self.ins, self.outs, self.send_sems, self.recv_sems = ins, outs, send_sems, recv_sems
        self.x, self.y, self.c = _position()
        self.j = 2 * self.x + self.y

    def _copies(self):
        for w in range(len(self.ins)):
            for r in (1, 2, 3):
                k = 3 * w + r - 1
                yield pltpu.make_async_remote_copy(
                    src_ref=self.ins[w].at[self.j ^ r], dst_ref=self.outs[w].at[r - 1],
                    send_sem=self.send_sems.at[k], recv_sem=self.recv_sems.at[k],
                    device_id=(*_chip_at(self.x, self.y, r), self.c), device_id_type=MESH)

    def start(self):
        for cp in self._copies():
            cp.start()

    def finish(self):
        for cp in self._copies():
            cp.wait()


def _exchange_sems(n_weights):
    return [pltpu.SemaphoreType.DMA((3 * n_weights,)), pltpu.SemaphoreType.DMA((3 * n_weights,))]


class _CoreExchange:
    def __init__(self, ins, outs, send_sems, recv_sems):
        self.ins, self.outs, self.send_sems, self.recv_sems = ins, outs, send_sems, recv_sems
        self.x, self.y, self.c = _position()

    def _copies(self):
        for w in range(len(self.ins)):
            yield pltpu.make_async_remote_copy(
                src_ref=self.ins[w].at[:, 1 - self.c], dst_ref=self.outs[w],
                send_sem=self.send_sems.at[w], recv_sem=self.recv_sems.at[w],
                device_id=(self.x, self.y, 1 - self.c), device_id_type=MESH)

    def start(self):
        for cp in self._copies():
            cp.start()

    def finish(self):
        for cp in self._copies():
            cp.wait()


def _core_exchange_shapes(grads):
    return [SDS((g.shape[0], g.shape[2], g.shape[3]), g.dtype) for g in grads]


def _core_exchange_sems(n):
    return [pltpu.SemaphoreType.DMA((n,)), pltpu.SemaphoreType.DMA((n,))]


def _exchange_core_halves(grads, name):
    nw = len(grads)

    def body(*refs):
        ex = _CoreExchange(refs[:nw], refs[nw:2 * nw], *refs[2 * nw:])
        ex.start()
        ex.finish()

    anyspec = pl.BlockSpec(memory_space=pl.ANY)
    return pl.pallas_call(
        body, out_shape=_core_exchange_shapes(grads), in_specs=[anyspec] * nw, out_specs=[anyspec] * nw,
        scratch_shapes=_core_exchange_sems(nw), name=name)(*grads)


def _add_core_halves(g4, recv, c_idx, rb, name):
    ns, _, rh, C = g4.shape

    def body(c_ref, g_ref, r_ref, o_ref):
        o_ref[...] = (g_ref[0] + r_ref[...]).astype(BF16)

    return pl.pallas_call(
        body,
        grid_spec=pltpu.PrefetchScalarGridSpec(
            num_scalar_prefetch=1, grid=(ns, rh // rb),
            in_specs=[pl.BlockSpec((1, 1, rb, C), lambda s, i, cr: (s, cr[0], i, 0)),
                      pl.BlockSpec((1, rb, C), lambda s, i, cr: (s, i, 0))],
            out_specs=pl.BlockSpec((1, rb, C), lambda s, i, cr: (s, i, 0))),
        out_shape=SDS((ns, rh, C), BF16), compiler_params=_arb(2), name=name)(c_idx, g4, recv)


def _add_core_halves_in(g4, recv, c_idx, name):
    n_slabs, _, rh, C = g4.shape
    cb = 256
    per_slab, per_chip, n_blocks = C // cb, DIN // N_CHIPS // cb, DIN // cb

    def stored(s, k):
        sb = (per_chip * s + k + 4 * DH // cb) % n_blocks
        return sb // per_slab, sb % per_slab

    def body(c_ref, g_ref, r_ref, o_ref):
        o_ref[...] = (g_ref[0] + r_ref[...].astype(F32)).astype(BF16)

    return pl.pallas_call(
        body,
        grid_spec=pltpu.PrefetchScalarGridSpec(
            num_scalar_prefetch=1, grid=(N_CHIPS, per_chip),
            in_specs=[pl.BlockSpec((1, 1, rh, cb), lambda s, k, cr: (stored(s, k)[0], cr[0], 0, stored(s, k)[1])),
                      pl.BlockSpec((1, rh, cb), lambda s, k, cr: (stored(s, k)[0], 0, stored(s, k)[1]))],
            out_specs=pl.BlockSpec((1, rh, cb), lambda s, k, cr: (s, 0, k))),
        out_shape=SDS((N_CHIPS, rh, DIN // N_CHIPS), BF16), compiler_params=_arb(2), name=name)(c_idx, g4, recv)


def _slot_shapes(sums):
    return [SDS((3,) + s.shape[1:], s.dtype) for s in sums]


def _add_chips(own, slots, order, rb, name):
    _, rh, C = slots.shape

    def body(o_ref, own_ref, a_ref, b_ref, c_ref, d_ref, out_ref):
        mine = own_ref[0].astype(F32)
        t = [jnp.where(o_ref[i] == 0, mine, r[0].astype(F32)) for i, r in enumerate((a_ref, b_ref, c_ref, d_ref))]
        out_ref[...] = ((t[0] + t[1]) + t[2]) + t[3]

    def spec(i):
        return pl.BlockSpec((1, rb, C), lambda t, o: (jnp.maximum(o[i], 1) - 1, t, 0))

    return pl.pallas_call(
        body,
        grid_spec=pltpu.PrefetchScalarGridSpec(
            num_scalar_prefetch=1, grid=(rh // rb,),
            in_specs=[pl.BlockSpec((1, rb, C), lambda t, o: (o[4], t, 0)), spec(0), spec(1), spec(2), spec(3)],
            out_specs=pl.BlockSpec((rb, C), lambda t, o: (t, 0))),
        out_shape=SDS((rh, C), F32), compiler_params=_arb(), name=name)(order, own, slots, slots, slots, slots)


def _share_halves(halves):
    nw = len(halves)

    def body(*refs):
        ins, outs = refs[:nw], refs[nw:2 * nw]
        send_sems, recv_sems = refs[2 * nw:]
        x, y, c = _position()
        started = []
        for w in range(nw):
            cp = pltpu.make_async_remote_copy(
                src_ref=ins[w], dst_ref=outs[w], send_sem=send_sems.at[w], recv_sem=recv_sems.at[w],
                device_id=(x, y, 1 - c), device_id_type=MESH)
            cp.start()
            started.append(cp)
        for cp in started:
            cp.wait()

    anyspec = pl.BlockSpec(memory_space=pl.ANY)
    return pl.pallas_call(
        body, out_shape=[SDS(h.shape, F32) for h in halves], in_specs=[anyspec] * nw, out_specs=[anyspec] * nw,
        scratch_shapes=[pltpu.SemaphoreType.DMA((nw,)), pltpu.SemaphoreType.DMA((nw,))],
        name="share_halves")(*halves)


def _small_2d(b_ada, norm1_w, norm2_w, final_norm_w, v_ln_w, v_ln_b, lower_bounds, gn_w, b_s, w_s):
    return dict(zip(SMALL_NAMES, (b_ada, norm1_w, norm2_w, final_norm_w.reshape(1, D), v_ln_w, v_ln_b, lower_bounds, gn_w,
                                  b_s.reshape(NH, BLK), w_s.reshape(NH * BLK, BLK))))


def _small_original_shapes(d):
    out = dict(d)
    out['final_norm_w'] = d['final_norm_w'].reshape(D)
    out['b_s'] = d['b_s'].reshape(1, NH, BLK)
    out['w_s'] = d['w_s'].reshape(1, NH, BLK, BLK)
    return out


def _row_block(r):
    for cand in (256, 176, 128, 64, 32, 16, 8):
        if r % cand == 0:
            return cand
    return r


def kernel(x, c, w_ada, b_ada, norm1_w, w_in, w_s, b_s, v_ln_w, v_ln_b, lower_bounds, gn_w, w_out, norm2_w, w_ffn_in, w_ffn_out, final_norm_w, loss_target, m_w_ada, m_b_ada, m_norm1_w, m_w_in, m_w_s, m_b_s, m_v_ln_w, m_v_ln_b, m_lower_bounds, m_gn_w, m_w_out, m_norm2_w, m_w_ffn_in, m_w_ffn_out, m_final_norm_w, v_w_ada, v_b_ada, v_norm1_w, v_w_in, v_w_s, v_b_s, v_v_ln_w, v_v_ln_b, v_lower_bounds, v_gn_w, v_w_out, v_norm2_w, v_w_ffn_in, v_w_ffn_out, v_final_norm_w):
    T = x.shape[1]
    tm, tp = min(TOKEN_TILE, T), min(PROJ_TILE, T)
    px, py, pc = _position()
    chip = 2 * px + py
    me = 4 * px + 2 * py + pc
    x2d = x.reshape(T, D)
    tgt = loss_target.reshape(T, D)

    chip_idx = jnp.reshape(chip, (1,)).astype(jnp.int32)
    c_idx = jnp.reshape(pc, (1,)).astype(jnp.int32)
    w_in_b, cact, ada_all = _gather_w_in_and_ada(
        _place_shard(w_in[0], 1, chip_idx, "place_in"), 1, jnp.broadcast_to(c, (8, D)), w_ada[0])
    placed = [_place_shard(w_out[0], 0, chip_idx, "place_out"), _place_shard(w_ffn_in[0], 1, chip_idx, "place_ffn_in"),
              _place_shard(w_ffn_out[0], 0, chip_idx, "place_ffn_out")]

    n_ada = ada_all.shape[1]
    ada_all = ada_all.reshape(N_CHIPS, 2, N_DEV, n_ada)[:, 0]
    ada = lax.dynamic_index_in_dim(ada_all, me, axis=1, keepdims=False).reshape(1, 6 * D) + b_ada

    rr = lax.broadcasted_iota(jnp.int32, (BLK, BLK), 0) // CH
    cc = lax.broadcasted_iota(jnp.int32, (BLK, BLK), 1) // CH
    ws_b = jnp.where((rr >= cc)[None], w_s[0], 0.0).astype(BF16)
    bst = b_s[0].T
    lnw, lnb = v_ln_w, v_ln_b
    nw1, nw2, fw = norm1_w, norm2_w, final_norm_w.reshape(1, D)

    tables = _hgrn_tables()
    (h1, proj, ycat, o_pre, a_all, st_all), (w_out_b, w_fi_b, w_fo_b) = _proj_hgrn_fwd(
        x2d, nw1, ada, w_in_b, lower_bounds, gn_w, tables, placed, [0, 1, 0])

    dycat, dx1, h2, act, dff, dgu, dmix, acc2, ycat = _token_local(
        x2d, ycat, tgt, ada, nw2, ada, ada, ada, fw, w_out_b, w_fi_b, w_fo_b, proj, ws_b, bst, lnw, lnb, tm)

    tt = min(WGRAD_TOKENS, T)
    order = jnp.concatenate([chip ^ jnp.arange(N_CHIPS, dtype=jnp.int32), chip_idx]).astype(jnp.int32)

    def by_core_half(g):
        return g.reshape(g.shape[0], 2, g.shape[1] // 2, g.shape[2])

    def core_sums(g4, recv, names):
        return [_add_core_halves(a, b, c_idx, _row_block(a.shape[2]), "add_core_" + n) for a, b, n in zip(g4, recv, names)]

    def chip_sums(sums, slots, names):
        return [_add_chips(o, s, order, _row_block(s.shape[1]), "add_chips_" + n) for o, s, n in zip(sums, slots, names)]

    g_out = _wgrad(ycat, dmix, D, D, tt, "wgrad_out").reshape(N_CHIPS, D // N_CHIPS, D)
    g_fi = _wgrad(h2, dgu, D, FFB, tt, "wgrad_ffn_in")
    g_fo = _wgrad(act, dff, FFB, D, tt, "wgrad_ffn_out").reshape(N_CHIPS, DFF // N_CHIPS, D)
    late_names = ["out", "ffn_in", "ffn_out"]
    late_g4 = [by_core_half(g) for g in (g_out, g_fi, g_fo)]

    (dproj, dws, dbs, dln), late_recv = _gmlp_bwd(proj, dycat, ws_b, bst, lnw, lnb, late_g4)
    late_sums = core_sums(late_g4, late_recv, late_names)
    (dproj, dlb, dgn), late_slots, (acc2_all, dln_all, dbs_all, dws_all) = _hgrn_bwd(
        proj, o_pre, a_all, st_all, dycat, lower_bounds, gn_w, dproj, tables, late_sums, [acc2, dln, dbs, dws])

    g_in, g_in_wire = _wgrad(h1, dproj, D, D, tt, "wgrad_in", bf16_copy=True)
    (in_recv,) = _exchange_core_halves([by_core_half(g_in_wire)], "exchange_core_halves_in")
    in_sums = [_add_core_halves_in(by_core_half(g_in), in_recv, c_idx, "add_core_in")]
    (grad_x, acc1), in_slots = _proj_in_bwd(dproj, x2d, dx1, nw1, ada, w_in_b, tp, in_sums)
    names = ["in"] + late_names
    halves = chip_sums(in_sums, in_slots, ["in"]) + chip_sums(late_sums, late_slots, late_names)
    sibling_halves = _share_halves(halves)

    big_w = [(w_in, m_w_in, v_w_in), (w_out, m_w_out, v_w_out), (w_ffn_in, m_w_ffn_in, v_w_ffn_in),
             (w_ffn_out, m_w_ffn_out, v_w_ffn_out)]
    big_out = []
    for mine, sib, (w, m, v), n in zip(halves, sibling_halves, big_w, names):
        res = _adamw_halves(w[0], mine, sib, m[0], v[0], c_idx, _row_block(mine.shape[0]), "adamw_" + n)
        big_out.append([r[None] for r in res])

    acc1_all, dlb_all, dgn_all = _all_gather_rows([acc1, dlb, dgn], "gather_small")
    gathered = [acc1_all, acc2_all, dln_all, dlb_all, dgn_all, dbs_all, dws_all]
    small, loss, dada_all = _small_finalize(
        gathered,
        _small_2d(b_ada, norm1_w, norm2_w, final_norm_w, v_ln_w, v_ln_b, lower_bounds, gn_w, b_s, w_s),
        _small_2d(m_b_ada, m_norm1_w, m_norm2_w, m_final_norm_w, m_v_ln_w, m_v_ln_b, m_lower_bounds, m_gn_w, m_b_s, m_w_s),
        _small_2d(v_b_ada, v_norm1_w, v_norm2_w, v_final_norm_w, v_v_ln_w, v_v_ln_b, v_lower_bounds, v_gn_w, v_b_s, v_w_s))
    small = [_small_original_shapes(d) for d in small]
    loss = loss.reshape(())

    ada_out = [o[None] for o in _ada_wgrad_adam(cact.T, dada_all, w_ada[0], m_w_ada[0], v_w_ada[0], chip_idx)]

    order_names = ['w_ada', 'b_ada', 'norm1_w', 'w_in', 'w_s', 'b_s', 'v_ln_w', 'v_ln_b', 'lower_bounds', 'gn_w',
                   'w_out', 'norm2_w', 'w_ffn_in', 'w_ffn_out', 'final_norm_w']
    big_idx = {'w_in': 0, 'w_out': 1, 'w_ffn_in': 2, 'w_ffn_out': 3}
    outs = [loss, grad_x.reshape(1, T, D)]
    for kind in range(4):
        for n in order_names:
            if n == 'w_ada':
                outs.append(ada_out[kind])
            elif n in big_idx:
                outs.append(big_out[big_idx[n]][kind])
            else:
                outs.append(small[kind][n])
    return tuple(outs)
```

```python
import jax
import jax.numpy as jnp
import numpy as np
from jax import lax
from jax.experimental import pallas as pl
from jax.experimental.pallas import tpu as pltpu

F32 = jnp.float32
BF16 = jnp.bfloat16
SDS = jax.ShapeDtypeStruct
MESH = pl.DeviceIdType.MESH
HIGHEST = lax.Precision.HIGHEST

D = 1024
DG = 512
DH = 512
NH = 4
HD = 128
BLK = 128
CH = 64
DFF = 2816
DIN = 3072
FFB = 1408
LEVELS = (64, 32, 16, 8, 4, 2)
HGRN_CHUNKS_PER_STEP = 8
GMLP_ROWS_PER_STEP = 1024
TOKEN_TILE = 256
PROJ_TILE = 1024
WGRAD_TOKENS = 2048
N_CHIPS = 4
N_DEV = 8
EPS = 1e-6
LR, B1, B2, AEPS, WD, STEP = 0.001, 0.9, 0.999, 1e-08, 0.01, 10

NT = (((1,), (1,)), ((), ()))
TN = (((0,), (0,)), ((), ()))


def _full(shape):
    nd = len(shape)
    return pl.BlockSpec(shape, lambda *_: (0,) * nd)


ADA_SH1, ADA_SC1, ADA_G1, ADA_SH2, ADA_SC2, ADA_G2 = range(6)


def _ada_part(k):
    return pl.BlockSpec((1, D), lambda *_: (0, k))


def _resident(shape):
    nd = len(shape)
    return pl.BlockSpec(shape, lambda *_: (0,) * nd, pipeline_mode=pl.Buffered(1))


def _arb(n=1):
    return pltpu.CompilerParams(dimension_semantics=("arbitrary",) * n)


def _dot(a, b, dims=None, precision=None):
    if dims is None:
        return jnp.dot(a, b, preferred_element_type=F32, precision=precision)
    return lax.dot_general(a, b, dims, preferred_element_type=F32, precision=precision)


def _sigmoid(x):
    return jax.nn.sigmoid(x)


def _gelu_parts(x):
    cdf = 0.5 * (1.0 + lax.erf(x * 0.7071067811865476))
    pdf = jnp.exp(-0.5 * x * x) * 0.3989422804014327
    return x * cdf, cdf + x * pdf


def _rms(x):
    return lax.rsqrt(jnp.mean(x * x, axis=-1, keepdims=True) + EPS)


def _rms_bwd(xhat, r, gw):
    return r * (gw - xhat * jnp.mean(xhat * gw, axis=-1, keepdims=True))


def _lower_bound(lbp_ref):
    l0, l1 = lbp_ref[0:1, :], lbp_ref[1:2, :]
    m = jnp.maximum(l0, l1)
    e0, e1 = jnp.exp(l0 - m), jnp.exp(l1 - m)
    return e0 / (e0 + e1), e1 / (e0 + e1)


def _gmlp_common(u, v, lnw, lnb, ws_ref, bst_ref):
    ug, dug = _gelu_parts(u)
    vg, dvg = _gelu_parts(v)
    mu = jnp.mean(vg, axis=-1, keepdims=True)
    vc = vg - mu
    rstd = lax.rsqrt(jnp.mean(vc * vc, axis=-1, keepdims=True) + EPS)
    vhat = vc * rstd
    vn = vhat * lnw + lnb
    vnb = vn.astype(BF16)
    mixed = []
    for h in range(NH):
        sl = slice(h * HD, (h + 1) * HD)
        mixed.append(_dot(ws_ref[h], vnb[:, sl]) + bst_ref[:, h:h + 1])
    return ug, dug, dvg, rstd, vhat, vnb, jnp.concatenate(mixed, axis=1)


def _hgrn_tables():
    t = np.arange(CH)[:, None]
    j = np.arange(CH)[None, :]
    blocks = [j <= t, j > t]
    masks = []
    for n in LEVELS:
        mid = t - t % n + n // 2
        blocks.append(np.where(t >= mid, (j >= mid) & (j <= t), (j > t) & (j < mid)))
        masks.append((t // n == j // n) & (t % n >= n // 2) & (j % n < n // 2))
    w = np.concatenate(blocks, axis=0).astype(np.float32)
    m = np.stack(masks).astype(np.float32)
    return (jnp.asarray(w, BF16), jnp.asarray(w.T, BF16), jnp.asarray(m), jnp.asarray(m + m.transpose(0, 2, 1)))


def _split_dot(w, x, parts):
    acc = None
    for _ in range(parts):
        piece = x.astype(BF16)
        term = _dot(w, piece)
        acc = term if acc is None else acc + term
        x = x - piece.astype(F32)
    return acc


def _hgrn_decays(f, w_ref):
    b = _split_dot(w_ref[0:CH, :], jnp.log(f), 3)
    row = lax.broadcasted_iota(jnp.int32, (CH, 1), 0)
    blocks = [jnp.exp(b), jnp.exp(b[CH - 1:CH, :] - b)]
    for n in LEVELS:
        up = (row & (n // 2)) != 0
        if n >= 8:
            ref = b.reshape(CH // n, n, DH)[:, n // 2 - 1:n // 2, :]
            ref = jnp.broadcast_to(ref, (CH // n, n, DH)).reshape(CH, DH)
            blocks.append(jnp.exp(jnp.where(up, b - ref, ref - b)))
        elif n == 4:
            r4 = row & 3
            two = jnp.where(r4 == 3, pltpu.roll(f, 1, 0) * f, 1.0)
            blocks.append(jnp.where(r4 == 0, pltpu.roll(f, CH - 1, 0), jnp.where(r4 == 2, f, two)))
        else:
            blocks.append(jnp.where(up, f, 1.0))
    return blocks


def _hgrn_gates(q, fl, lb, omlb, w_ref):
    sq = _sigmoid(q)
    qf = q * sq
    sig = _sigmoid(fl)
    f = lb + omlb * sig
    k = 1.0 - f
    return sq, qf, sig, f, k, _hgrn_decays(f, w_ref)


def _level_factor(e, li, sl, row, qh, kh):
    el = e[2 + li][:, sl]
    up = (row & (LEVELS[li] // 2)) != 0
    return el, up, el * jnp.where(up, qh, kh)


def _proj_hgrn_fwd(x, nw1, ada, w_in_b, lower_bounds, gn_w, tables, placed, axes):
    T = x.shape[0]
    nc = T // CH
    nch = min(HGRN_CHUNKS_PER_STEP, nc)
    steps = nc // nch
    w_st, _, masks, _ = tables
    nw = len(placed)
    pass_step = (13 * steps) // 16
    q0 = 2 * DG

    def body(*refs):
        x_ref, nw_ref, sc_ref, sh_ref, win_ref, lbp_ref, gn_ref, w_ref, m_ref = refs[:9]
        h_ref, p_ref, y_ref, o_ref, a_ref, st_ref = refs[9 + nw:15 + nw]
        s_scr, send_sems, recv_sems = refs[15 + 2 * nw:]
        gather = _WeightGather(refs[15 + nw:15 + 2 * nw], axes, send_sems, recv_sems)
        step = pl.program_id(0)
        xv = x_ref[...]
        hb = (((xv * _rms(xv)) * nw_ref[...]) * (1.0 + sc_ref[...]) + sh_ref[...]).astype(BF16)
        h_ref[...] = hb
        p_ref[...] = _dot(hb, win_ref[...])

        @pl.when(step == 0)
        def _():
            gather.start()
            s_scr[...] = jnp.zeros_like(s_scr)

        @pl.when(step == pass_step)
        def _():
            gather.forward()

        lb, omlb = _lower_bound(lbp_ref)
        row = lax.broadcasted_iota(jnp.int32, (CH, 1), 0)
        eye = lax.broadcasted_iota(jnp.int32, (CH, CH), 0) == lax.broadcasted_iota(jnp.int32, (CH, CH), 1)
        in_level = [m_ref[li] > 0.0 for li in range(len(LEVELS))]
        pre = []
        for ci in range(nch):
            rs = slice(ci * CH, (ci + 1) * CH)
            _, qf, _, _, k, e = _hgrn_gates(p_ref[rs, q0:q0 + DH], p_ref[rs, q0 + DH:q0 + 2 * DH], lb, omlb, w_ref)
            mats = []
            for h in range(NH):
                sl = slice(h * HD, (h + 1) * HD)
                qh, kh = qf[:, sl], k[:, sl]
                a = jnp.where(eye, jnp.sum(qh * kh, axis=-1, keepdims=True), 0.0)
                for li in range(len(LEVELS)):
                    _, _, y = _level_factor(e, li, sl, row, qh, kh)
                    yb = y.astype(BF16)
                    a = jnp.where(in_level[li], _dot(yb, yb, NT), a)
                a_ref[ci, h] = a
                mats.append(a.astype(BF16))
            eb = e[0]
            pre.append(((qf * eb).astype(BF16), eb[CH - 1:CH, :], (k * e[1]).astype(BF16), mats))
        for ci in range(nch):
            rs = slice(ci * CH, (ci + 1) * CH)
            qe, ebl, kd, mats = pre[ci]
            v = p_ref[rs, q0 + 2 * DH:q0 + 3 * DH]
            g = p_ref[rs, q0 + 3 * DH:q0 + 4 * DH]
            for h in range(NH):
                sl = slice(h * HD, (h + 1) * HD)
                st0 = s_scr[h]
                st_ref[ci, h] = st0
                vb = v[:, sl].astype(BF16)
                o = _dot(qe[:, sl], st0.astype(BF16), NT) + _dot(mats[h], vb)
                s_scr[h] = st0 * ebl[:, sl] + _dot(vb, kd[:, sl], TN)
                o_ref[rs, sl] = o
                gh = g[:, sl]
                y_ref[rs, sl] = (((o * _rms(o)) * gn_ref[...]) * (gh * _sigmoid(gh))).astype(BF16)

        @pl.when(step == steps - 1)
        def _():
            gather.finish()

    rows = nch * CH
    row = lambda c: (c, 0)
    anyspec = pl.BlockSpec(memory_space=pl.ANY)
    res = pl.pallas_call(
        body, grid=(steps,),
        in_specs=[pl.BlockSpec((rows, D), row), _full((1, D)), _ada_part(ADA_SC1), _ada_part(ADA_SH1), _resident((D, DIN)),
                  _full((2, DH)), _full((1, HD)), _full(w_st.shape), _full(masks.shape)] + [anyspec] * nw,
        out_specs=[pl.BlockSpec((rows, D), row), pl.BlockSpec((rows, DIN), row),
                   pl.BlockSpec((rows, DH), lambda c: (c, 1)),
                   pl.BlockSpec((rows, DH), row),
                   pl.BlockSpec((nch, NH, CH, CH), lambda c: (c, 0, 0, 0)),
                   pl.BlockSpec((nch, NH, HD, HD), lambda c: (c, 0, 0, 0))] + [anyspec] * nw,
        out_shape=[SDS((T, D), BF16), SDS((T, DIN), F32), SDS((T, D), BF16), SDS((T, DH), F32),
                   SDS((nc, NH, CH, CH), F32), SDS((nc, NH, HD, HD), F32)] + [SDS(a.shape, a.dtype) for a in placed],
        scratch_shapes=[pltpu.VMEM((NH, HD, HD), F32)] + _gather_sems(nw),
        input_output_aliases={9 + i: 6 + i for i in range(nw)},
        compiler_params=_arb(), name="proj_hgrn_fwd")(x, nw1, ada, ada, w_in_b, lower_bounds, gn_w, w_st, masks, *placed)
    return res[:6], res[6:]


def _token_local(x, ycat, tgt, g1, nw2, sc2, sh2, g2, fw, w_out_b, w_fi_b, w_fo_b, proj, ws_b, bst, lnw, lnb, tm):
    T = x.shape[0]
    inv_d = 1.0 / D

    def body(x_ref, yb_ref, t_ref, g1_ref, nw2_ref, sc2_ref, sh2_ref, g2_ref, fw_ref, wo_ref, wfi_ref, wfo_ref,
             u_ref, v_ref, ws_ref, bst_ref, lnw_ref, lnb_ref,
             dy_ref, dx1_ref, h2_ref, act_ref, dff_ref, dgu_ref, dmix_ref, acc_ref, ya_ref):
        @pl.when(pl.program_id(0) == 0)
        def _():
            acc_ref[...] = jnp.zeros_like(acc_ref)

        def acc(row, val):
            acc_ref[row:row + 1, :] += jnp.sum(val, axis=0, keepdims=True)

        for bi in range(tm // BLK):
            rs = slice(bi * BLK, (bi + 1) * BLK)
            ug, _, _, _, _, _, mixed = _gmlp_common(u_ref[rs, :], v_ref[rs, :], lnw_ref[...], lnb_ref[...], ws_ref, bst_ref)
            ya_ref[rs, :] = (ug * mixed).astype(BF16)
        g1v, g2v = g1_ref[...], g2_ref[...]
        mix = _dot(ya_ref[...], wo_ref[0:DG, :]) + _dot(yb_ref[...], wo_ref[DG:D, :])
        x1 = x_ref[...] + g1v * mix
        r2 = _rms(x1)
        xh2 = x1 * r2
        n2 = xh2 * nw2_ref[...]
        osc2 = 1.0 + sc2_ref[...]
        h2b = (n2 * osc2 + sh2_ref[...]).astype(BF16)
        h2_ref[...] = h2b
        ff = jnp.zeros((tm, D), F32)
        saved = []
        for kb in range(DFF // FFB):
            gate = _dot(h2b, wfi_ref[:, kb * FFB:(kb + 1) * FFB])
            up = _dot(h2b, wfi_ref[:, DFF + kb * FFB:DFF + (kb + 1) * FFB])
            sg = _sigmoid(gate)
            actb = (gate * sg * up).astype(BF16)
            act_ref[:, kb * FFB:(kb + 1) * FFB] = actb
            ff = ff + _dot(actb, wfo_ref[kb * FFB:(kb + 1) * FFB, :])
            saved.append((gate, up, sg))
        x2 = x1 + g2v * ff
        r3 = _rms(x2)
        xh3 = x2 * r3
        err = xh3 * fw_ref[...] - t_ref[...]
        acc(6, (0.5 * inv_d) * err * err)
        dy = err * inv_d
        acc(4, dy * xh3)
        dx2 = _rms_bwd(xh3, r3, dy * fw_ref[...])
        acc(0, dx2 * ff)
        dffb = (dx2 * g2v).astype(BF16)
        dff_ref[...] = dffb
        dh2 = jnp.zeros((tm, D), F32)
        for kb in range(DFF // FFB):
            gate, up, sg = saved[kb]
            da = _dot(dffb, wfo_ref[kb * FFB:(kb + 1) * FFB, :], NT)
            dgate = (da * up * (sg * (1.0 + gate * (1.0 - sg)))).astype(BF16)
            dup = (da * gate * sg).astype(BF16)
            dgu_ref[:, kb * FFB:(kb + 1) * FFB] = dgate
            dgu_ref[:, DFF + kb * FFB:DFF + (kb + 1) * FFB] = dup
            dh2 = dh2 + _dot(dgate, wfi_ref[:, kb * FFB:(kb + 1) * FFB], NT)
            dh2 = dh2 + _dot(dup, wfi_ref[:, DFF + kb * FFB:DFF + (kb + 1) * FFB], NT)
        acc(2, dh2)
        acc(1, dh2 * n2)
        dn2 = dh2 * osc2
        acc(3, dn2 * xh2)
        dx1 = dx2 + _rms_bwd(xh2, r2, dn2 * nw2_ref[...])
        acc(5, dx1 * mix)
        dmixb = (dx1 * g1v).astype(BF16)
        dmix_ref[...] = dmixb
        dy_ref[...] = _dot(dmixb, wo_ref[...], NT)
        dx1_ref[...] = dx1

    row = lambda i: (i, 0)
    vec = _full((1, D))
    half = lambda j: pl.BlockSpec((tm, DG), lambda i: (i, j))
    return pl.pallas_call(
        body, grid=(T // tm,),
        in_specs=[pl.BlockSpec((tm, D), row), half(1), pl.BlockSpec((tm, D), row),
                  _ada_part(ADA_G1), vec, _ada_part(ADA_SC2), _ada_part(ADA_SH2), _ada_part(ADA_G2), vec,
                  _resident((D, D)), _resident((D, 2 * DFF)), _resident((DFF, D)),
                  half(0), half(1), _full((NH, BLK, BLK)), _full((BLK, NH)), _full((1, DG)), _full((1, DG))],
        out_specs=[pl.BlockSpec((tm, D), row), pl.BlockSpec((tm, D), row), pl.BlockSpec((tm, D), row),
                   pl.BlockSpec((tm, DFF), row), pl.BlockSpec((tm, D), row), pl.BlockSpec((tm, 2 * DFF), row),
                   pl.BlockSpec((tm, D), row), _full((8, D)), half(0)],
        out_shape=[SDS((T, D), F32), SDS((T, D), F32), SDS((T, D), BF16), SDS((T, DFF), BF16), SDS((T, D), BF16),
                   SDS((T, 2 * DFF), BF16), SDS((T, D), BF16), SDS((8, D), F32), SDS((T, D), BF16)],
        input_output_aliases={1: 8},
        compiler_params=_arb(), name="token_local")(x, ycat, tgt, g1, nw2, sc2, sh2, g2, fw, w_out_b, w_fi_b, w_fo_b,
                                                    proj, proj, ws_b, bst, lnw, lnb)


def _gmlp_bwd(proj, dycat, ws_b, bst, lnw, lnb, grads):
    T = proj.shape[0]
    rows = min(GMLP_ROWS_PER_STEP, T)
    nb = T // rows
    nw = len(grads)

    def body(*refs):
        u_ref, v_ref, dy_ref, ws_ref, bst_ref, lnw_ref, lnb_ref = refs[:7]
        dp_ref, dws_ref, dbs_ref, dln_ref = refs[7 + nw:11 + nw]
        dbs_acc, send_sems, recv_sems = refs[11 + 2 * nw:]
        exchange = _CoreExchange(refs[7:7 + nw], refs[11 + nw:11 + 2 * nw], send_sems, recv_sems)
        i = pl.program_id(0)

        @pl.when(i == 0)
        def _():
            exchange.start()
            dws_ref[...] = jnp.zeros_like(dws_ref)
            dln_ref[...] = jnp.zeros_like(dln_ref)
            dbs_acc[...] = jnp.zeros_like(dbs_acc)

        r = lax.broadcasted_iota(jnp.int32, (BLK, BLK), 0) // CH
        c = lax.broadcasted_iota(jnp.int32, (BLK, BLK), 1) // CH
        for bi in range(rows // BLK):
            rs = slice(bi * BLK, (bi + 1) * BLK)
            ug, dug, dvg, rstd, vhat, vnb, mixed = _gmlp_common(
                u_ref[rs, :], v_ref[rs, :], lnw_ref[...], lnb_ref[...], ws_ref, bst_ref)
            dya = dy_ref[rs, :]
            dp_ref[rs, 0:DG] = (dya * mixed * dug).astype(BF16)
            dmixed = dya * ug
            dbs_acc[...] += dmixed
            dmb = dmixed.astype(BF16)
            dvn = []
            for h in range(NH):
                sl = slice(h * HD, (h + 1) * HD)
                dws_ref[h * BLK:(h + 1) * BLK, :] += jnp.where(r >= c, _dot(dmb[:, sl], vnb[:, sl], NT), 0.0)
                dvn.append(_dot(ws_ref[h], dmb[:, sl], TN))
            dvn = jnp.concatenate(dvn, axis=1)
            dln_ref[0:1, :] += jnp.sum(dvn * vhat, axis=0, keepdims=True)
            dln_ref[1:2, :] += jnp.sum(dvn, axis=0, keepdims=True)
            dvh = dvn * lnw_ref[...]
            dvgel = rstd * (dvh - jnp.mean(dvh, axis=-1, keepdims=True) - vhat * jnp.mean(dvh * vhat, axis=-1, keepdims=True))
            dp_ref[rs, DG:2 * DG] = (dvgel * dvg).astype(BF16)

        @pl.when(i == nb - 1)
        def _():
            head = lax.broadcasted_iota(jnp.int32, (8, BLK), 0)
            ones = jnp.ones((8, HD), F32)
            out = jnp.zeros((8, BLK), F32)
            for h in range(NH):
                sums = _dot(ones, dbs_acc[:, h * HD:(h + 1) * HD], NT, precision=HIGHEST)
                out = out + jnp.where(head == h, sums, 0.0)
            dbs_ref[...] = out
            exchange.finish()

    anyspec = pl.BlockSpec(memory_space=pl.ANY)
    res = pl.pallas_call(
        body, grid=(nb,),
        in_specs=[pl.BlockSpec((rows, DG), lambda i: (i, 0)), pl.BlockSpec((rows, DG), lambda i: (i, 1)),
                  pl.BlockSpec((rows, DG), lambda i: (i, 0)),
                  _full((NH, BLK, BLK)), _full((BLK, NH)), _full((1, DG)), _full((1, DG))] + [anyspec] * nw,
        out_specs=[pl.BlockSpec((rows, 2 * DG), lambda i: (i, 2)), _full((NH * BLK, BLK)), _full((8, BLK)), _full((8, DG))]
        + [anyspec] * nw,
        out_shape=[SDS((T, DIN), BF16), SDS((NH * BLK, BLK), F32), SDS((8, BLK), F32), SDS((8, DG), F32)]
        + _core_exchange_shapes(grads),
        scratch_shapes=[pltpu.VMEM((BLK, DG), F32)] + _core_exchange_sems(nw),
        compiler_params=_arb(), name="gmlp_bwd")(proj, proj, dycat, ws_b, bst, lnw, lnb, *grads)
    return res[:4], res[4:]


def _hgrn_bwd(proj, o_pre, a_all, st_all, dycat, lower_bounds, gn_w, dproj, tables, sums, row_blocks):
    T = proj.shape[0]
    nc = T // CH
    nch = min(HGRN_CHUNKS_PER_STEP, nc)
    steps = nc // nch
    w_st, w_st_t, _, masks_sym = tables
    n_lev = len(LEVELS)
    nw, nr = len(sums), len(row_blocks)

    def body(*refs):
        q_ref, f_ref, i_ref, g_ref, o_ref, a_ref, st_ref, dy_ref, lbp_ref, gn_ref, w_ref, wt_ref, ms_ref = refs[:13]
        n_in = 14 + nw + nr
        dp_ref, dlb_ref, dgn_ref = refs[n_in:n_in + 3]
        ds_scr, dx_scr = refs[n_in + 3 + nw + nr:n_in + 5 + nw + nr]
        sems = refs[n_in + 5 + nw + nr:]
        exchange = _ChipExchange(refs[14:14 + nw], refs[n_in + 3:n_in + 3 + nw], *sems[:2])
        rows_gather = _RowGather(refs[14 + nw:n_in], refs[n_in + 3 + nw:n_in + 3 + nw + nr], *sems[2:])
        i = pl.program_id(0)

        @pl.when(i == 0)
        def _():
            rows_gather.start()
            exchange.start()
            ds_scr[...] = jnp.zeros_like(ds_scr)
            dlb_ref[...] = jnp.zeros_like(dlb_ref)
            dgn_ref[...] = jnp.zeros_like(dgn_ref)

        @pl.when(i == steps // 2)
        def _():
            rows_gather.forward()

        lb, omlb = _lower_bound(lbp_ref)
        row = lax.broadcasted_iota(jnp.int32, (CH, 1), 0)
        eye = lax.broadcasted_iota(jnp.int32, (CH, CH), 0) == lax.broadcasted_iota(jnp.int32, (CH, CH), 1)
        lower = lax.broadcasted_iota(jnp.int32, (CH, CH), 0) > lax.broadcasted_iota(jnp.int32, (CH, CH), 1)
        dgn = jnp.zeros((1, HD), F32)
        pre = []
        for ci in range(nch):
            rs = slice(ci * CH, (ci + 1) * CH)
            q = q_ref[rs, :]
            v = i_ref[rs, :]
            g = g_ref[rs, :]
            sq, qf, sig, f, k, e = _hgrn_gates(q, f_ref[rs, :], lb, omlb, w_ref)
            eb = e[0]
            ekd = e[1]
            kd = k * ekd
            qe = qf * eb
            dob_h, dqe_h, dqf_h, dki_h, dv_h, dg_h = [], [], [], [], [], []
            for h in range(NH):
                sl = slice(h * HD, (h + 1) * HD)
                o = o_ref[rs, sl]
                ro = _rms(o)
                oh = o * ro
                gh = g[:, sl]
                sg = _sigmoid(gh)
                dyb = dy_ref[rs, sl]
                dg_h.append(dyb * (oh * gn_ref[...]) * (sg * (1.0 + gh * (1.0 - sg))))
                don = dyb * (gh * sg)
                dgn = dgn + jnp.sum(don * oh, axis=0, keepdims=True)
                dob = _rms_bwd(oh, ro, don * gn_ref[...]).astype(BF16)
                vb = v[:, sl].astype(BF16)
                qh, kh = qf[:, sl], k[:, sl]
                dqe = _dot(dob, st_ref[ci, h].astype(BF16))
                da = _dot(dob, vb, NT)
                ddiag = jnp.sum(jnp.where(eye, da, 0.0), axis=-1, keepdims=True)
                dsym = jnp.where(lower, da, _dot(vb, dob, NT))
                upper_part = jnp.zeros((CH, HD), F32)
                both = jnp.zeros((CH, HD), F32)
                for li in range(n_lev):
                    el, up, y = _level_factor(e, li, sl, row, qh, kh)
                    dyv = _dot((ms_ref[li] * dsym).astype(BF16), y.astype(BF16))
                    dx_scr[ci, (2 + li) * CH:(3 + li) * CH, sl] = dyv * y
                    dye = dyv * el
                    upper_part = upper_part + jnp.where(up, dye, 0.0)
                    both = both + dye
                dob_h.append(dob)
                dqe_h.append(dqe)
                dqf_h.append(dqe * eb[:, sl] + ddiag * kh + upper_part)
                dki_h.append(ddiag * qh + (both - upper_part))
                dv_h.append(_dot(a_ref[ci, h].astype(BF16), dob, TN))
            dp_ref[rs, 0:DH] = (jnp.concatenate(dqf_h, axis=1) * (sq * (1.0 + q * (1.0 - sq)))).astype(BF16)
            dp_ref[rs, 3 * DH:4 * DH] = jnp.concatenate(dg_h, axis=1).astype(BF16)
            pre.append((v, sig, f, eb, ekd, kd, qe, dob_h, jnp.concatenate(dqe_h, axis=1), dki_h, dv_h))
        dgn_ref[0:1, :] += dgn
        for ci in reversed(range(nch)):
            rs = slice(ci * CH, (ci + 1) * CH)
            v, sig, f, eb, ekd, kd, qe, dob_h, dqe, dki_h, dv_h = pre[ci]
            ebl = eb[CH - 1:CH, :]
            dbl_h, dkd_h, dv2_h = [], [], []
            for h in range(NH):
                sl = slice(h * HD, (h + 1) * HD)
                dst1 = ds_scr[h]
                dst1b = dst1.astype(BF16)
                ds_scr[h] = dst1 * ebl[:, sl] + _dot(dob_h[h], qe[:, sl].astype(BF16), TN)
                dbl_h.append(ebl[:, sl] * jnp.sum(st_ref[ci, h] * dst1, axis=0, keepdims=True))
                dkd_h.append(_dot(v[:, sl].astype(BF16), dst1b))
                dv2_h.append(dv_h[h] + _dot(kd[:, sl].astype(BF16), dst1b, NT))
            dkd = jnp.concatenate(dkd_h, axis=1)
            dx_scr[ci, 0:CH, :] = dqe * qe + jnp.where(row == CH - 1, jnp.concatenate(dbl_h, axis=1), 0.0)
            dx_scr[ci, CH:2 * CH, :] = dkd * kd
            dlf = _split_dot(wt_ref[...], dx_scr[ci], 2)
            df = dlf / f - (dkd * ekd + jnp.concatenate(dki_h, axis=1))
            dlb_ref[0:1, :] += jnp.sum(df * (1.0 - sig), axis=0, keepdims=True)
            dp_ref[rs, DH:2 * DH] = (df * omlb * sig * (1.0 - sig)).astype(BF16)
            dp_ref[rs, 2 * DH:3 * DH] = jnp.concatenate(dv2_h, axis=1).astype(BF16)

        @pl.when(i == steps - 1)
        def _():
            gl = dlb_ref[0:1, :] * lb * omlb
            dlb_ref[0:1, :] = gl
            dlb_ref[1:2, :] = -gl
            exchange.finish()
            rows_gather.finish()

    rev = lambda j: pl.BlockSpec((nch * CH, DH), lambda c: (steps - 1 - c, j))
    anyspec = pl.BlockSpec(memory_space=pl.ANY)
    res = pl.pallas_call(
        body, grid=(steps,),
        in_specs=[rev(2), rev(3), rev(4), rev(5), rev(0),
                  pl.BlockSpec((nch, NH, CH, CH), lambda c: (steps - 1 - c, 0, 0, 0)),
                  pl.BlockSpec((nch, NH, HD, HD), lambda c: (steps - 1 - c, 0, 0, 0)),
                  rev(1), _full((2, DH)), _full((1, HD)),
                  _full(w_st.shape), _full(w_st_t.shape), _full(masks_sym.shape),
                  anyspec] + [anyspec] * (nw + nr),
        out_specs=[pl.BlockSpec((nch * CH, 4 * DH), lambda c: (steps - 1 - c, 0)), _full((8, DH)), _full((8, HD))]
        + [anyspec] * (nw + nr),
        out_shape=[SDS((T, DIN), BF16), SDS((8, DH), F32), SDS((8, HD), F32)] + _slot_shapes(sums)
        + _gather_rows_shapes(row_blocks),
        scratch_shapes=[pltpu.VMEM((NH, HD, HD), F32), pltpu.VMEM((nch, (2 + n_lev) * CH, DH), F32)]
        + _exchange_sems(nw) + _gather_rows_sems(nr),
        input_output_aliases={13: 0},
        compiler_params=_arb(), name="hgrn_bwd")(proj, proj, proj, proj, o_pre, a_all, st_all, dycat, lower_bounds, gn_w,
                                                 w_st, w_st_t, masks_sym, dproj, *sums, *row_blocks)
    return res[:3], res[3:3 + nw], res[3 + nw:]


def _proj_in_bwd(dproj, x, dx1, nw, sc, w_in_b, tm, sums):
    T = x.shape[0]
    ns = len(sums)
    steps = T // tm

    def body(*refs):
        dp_ref, x_ref, dx1_ref, nw_ref, sc_ref, w_ref = refs[:6]
        gx_ref, acc_ref = refs[6 + ns:8 + ns]
        exchange = _ChipExchange(refs[6:6 + ns], refs[8 + ns:8 + 2 * ns], *refs[8 + 2 * ns:])

        @pl.when(pl.program_id(0) == 0)
        def _():
            exchange.start()
            acc_ref[...] = jnp.zeros_like(acc_ref)

        dh = _dot(dp_ref[:, 0:4 * DH], w_ref[:, 2 * DG:DIN], NT) + _dot(dp_ref[:, 4 * DH:DIN], w_ref[:, 0:2 * DG], NT)
        xv = x_ref[...]
        r = _rms(xv)
        xh = xv * r
        n1 = xh * nw_ref[...]
        acc_ref[0:1, :] += jnp.sum(dh, axis=0, keepdims=True)
        acc_ref[1:2, :] += jnp.sum(dh * n1, axis=0, keepdims=True)
        dn = dh * (1.0 + sc_ref[...])
        acc_ref[2:3, :] += jnp.sum(dn * xh, axis=0, keepdims=True)
        gx_ref[...] = dx1_ref[...] + _rms_bwd(xh, r, dn * nw_ref[...])

        @pl.when(pl.program_id(0) == steps - 1)
        def _():
            exchange.finish()

    row = lambda i: (i, 0)
    anyspec = pl.BlockSpec(memory_space=pl.ANY)
    res = pl.pallas_call(
        body, grid=(steps,),
        in_specs=[pl.BlockSpec((tm, DIN), row), pl.BlockSpec((tm, D), row), pl.BlockSpec((tm, D), row),
                  _full((1, D)), _ada_part(ADA_SC1), _resident((D, DIN))] + [anyspec] * ns,
        out_specs=[pl.BlockSpec((tm, D), row), _full((8, D))] + [anyspec] * ns,
        out_shape=[SDS((T, D), F32), SDS((8, D), F32)] + _slot_shapes(sums),
        scratch_shapes=_exchange_sems(ns),
        compiler_params=_arb(), name="proj_in_bwd")(dproj, x, dx1, nw, sc, w_in_b, *sums)
    return res[:2], res[2:]


def _wgrad(a, b, bk, bn, tt, name, bf16_copy=False):
    T, K = a.shape
    N = b.shape[1]
    nn, nk, nt = N // bn, K // bk, T // tt
    bmap = lambda n, k, t: (t, n)

    def body(a_ref, b_ref, o_ref, *copy_ref):
        @pl.when(pl.program_id(2) == 0)
        def _():
            o_ref[...] = jnp.zeros_like(o_ref)

        o_ref[0] += _dot(a_ref[...], b_ref[...], TN)

        if bf16_copy:
            @pl.when(pl.program_id(2) == nt - 1)
            def _():
                copy_ref[0][...] = o_ref[...].astype(BF16)

    ospec = pl.BlockSpec((1, bk, bn), lambda n, k, t: (n, k, 0))
    return pl.pallas_call(
        body, grid=(nn, nk, nt),
        in_specs=[pl.BlockSpec((tt, bk), lambda n, k, t: (t, k)), pl.BlockSpec((tt, bn), bmap)],
        out_specs=[ospec, ospec] if bf16_copy else ospec,
        out_shape=[SDS((nn, K, bn), F32), SDS((nn, K, bn), BF16)] if bf16_copy else SDS((nn, K, bn), F32),
        compiler_params=_arb(3), name=name)(a, b)


def _adam_math(w, g, m, v):
    m = B1 * m + (1.0 - B1) * g
    v = B2 * v + (1.0 - B2) * (g * g)
    m_hat = m / (1.0 - B1 ** STEP)
    v_hat = v / (1.0 - B2 ** STEP)
    return -LR * (m_hat / (jnp.sqrt(v_hat) + AEPS) + WD * w), m, v


def _adamw_halves(w, mine, sibling, m, v, c_idx, rb, name):
    R, C = w.shape
    nb = (R // 2) // rb

    def body(c_ref, w_ref, a_ref, b_ref, m_ref, v_ref, g_out, d_out, m_out, v_out):
        g = jnp.where(pl.program_id(0) == c_ref[0], a_ref[...], b_ref[...])
        g_out[...] = g
        d_out[...], m_out[...], v_out[...] = _adam_math(w_ref[...], g, m_ref[...], v_ref[...])

    whole = pl.BlockSpec((rb, C), lambda hh, i, cr: (hh * nb + i, 0))
    half = pl.BlockSpec((rb, C), lambda hh, i, cr: (i, 0))
    return pl.pallas_call(
        body,
        grid_spec=pltpu.PrefetchScalarGridSpec(
            num_scalar_prefetch=1, grid=(2, nb), in_specs=[whole, half, half, whole, whole], out_specs=[whole] * 4),
        out_shape=[SDS((R, C), F32)] * 4, compiler_params=_arb(2), name=name)(c_idx, w, mine, sibling, m, v)


def _ada_wgrad_adam(cact_t, dada_all, w, m, v, chip_idx):
    R, C = w.shape
    rb = 256

    def body(j_ref, c_ref, d_ref, w_ref, m_ref, v_ref, g_out, d_out, m_out, v_out):
        g = _dot(c_ref[...], d_ref[...], precision=HIGHEST)
        g_out[...] = g
        d_out[...], m_out[...], v_out[...] = _adam_math(w_ref[...], g, m_ref[...], v_ref[...])

    spec = pl.BlockSpec((rb, C), lambda i, j: (i, 0))
    return pl.pallas_call(
        body,
        grid_spec=pltpu.PrefetchScalarGridSpec(
            num_scalar_prefetch=1, grid=(R // rb,),
            in_specs=[pl.BlockSpec((rb, N_DEV), lambda i, j: (i, 0)), pl.BlockSpec((N_DEV, C), lambda i, j: (0, j[0])),
                      spec, spec, spec],
            out_specs=[spec] * 4),
        out_shape=[SDS((R, C), F32)] * 4,
        compiler_params=_arb(), name="ada_wgrad_adam")(chip_idx, cact_t, dada_all, w, m, v)


SMALL_NAMES = ('b_ada', 'norm1_w', 'norm2_w', 'final_norm_w', 'v_ln_w', 'v_ln_b', 'lower_bounds', 'gn_w', 'b_s', 'w_s')


def _small_finalize(gathered, params, moms, vels):
    n_in = len(gathered)

    def body(*refs):
        acc1, acc2, dln, dlb, dgn, dbs, dws = refs[:n_in]
        prm = [dict(zip(SMALL_NAMES, refs[n_in + k * 10:n_in + (k + 1) * 10])) for k in range(3)]
        outs = [dict(zip(SMALL_NAMES, refs[n_in + 30 + k * 10:n_in + 30 + (k + 1) * 10])) for k in range(4)]
        loss_ref, dada_ref = refs[n_in + 70:n_in + 72]

        def dev_sum(ref, first, n):
            per = ref.shape[0] // N_DEV
            g = ref[first:first + n, :]
            for dev in range(1, N_DEV):
                g = g + ref[dev * per + first:dev * per + first + n, :]
            return g

        def update(n, g, cols=slice(None)):
            outs[0][n][:, cols] = g
            outs[1][n][:, cols], outs[2][n][:, cols], outs[3][n][:, cols] = _adam_math(
                prm[0][n][:, cols], g, prm[1][n][:, cols], prm[2][n][:, cols])

        ada_rows = ((acc1, 0), (acc1, 1), (acc2, 5), (acc2, 2), (acc2, 1), (acc2, 0))
        for k, (ref, r) in enumerate(ada_rows):
            update('b_ada', dev_sum(ref, r, 1), slice(k * D, (k + 1) * D))
            for dev in range(N_DEV):
                dada_ref[dev:dev + 1, k * D:(k + 1) * D] = ref[8 * dev + r:8 * dev + r + 1, :]
        update('norm1_w', dev_sum(acc1, 2, 1))
        update('norm2_w', dev_sum(acc2, 3, 1))
        update('final_norm_w', dev_sum(acc2, 4, 1))
        update('v_ln_w', dev_sum(dln, 0, 1))
        update('v_ln_b', dev_sum(dln, 1, 1))
        update('lower_bounds', dev_sum(dlb, 0, 2))
        update('gn_w', dev_sum(dgn, 0, 1))
        update('b_s', dev_sum(dbs, 0, NH))
        update('w_s', dev_sum(dws, 0, NH * BLK))
        loss_ref[...] = jnp.sum(dev_sum(acc2, 6, 1), axis=-1, keepdims=True)

    shapes = [SDS(params[n].shape, F32) for n in SMALL_NAMES]
    res = pl.pallas_call(
        body, out_shape=shapes * 4 + [SDS((1, 1), F32), SDS((N_DEV, 6 * D), F32)], name="small_finalize")(
            *gathered, *[d[n] for d in (params, moms, vels) for n in SMALL_NAMES])
    return [dict(zip(SMALL_NAMES, res[k * 10:(k + 1) * 10])) for k in range(4)], res[40], res[41]


def _position():
    x, y, c = lax.axis_index("x"), lax.axis_index("y"), lax.axis_index("c")
    return x, y, c


def _chip_at(x, y, r):
    return (x ^ (r >> 1), y ^ (r & 1))


class _RowGather:
    def __init__(self, ins, outs, send_sems, recv_sems, local_sems):
        self.ins, self.outs = ins, outs
        self.send_sems, self.recv_sems, self.local_sems = send_sems, recv_sems, local_sems
        self.x, self.y, self.c = _position()
        self.me, self.sibling = (self.x, self.y, self.c), (self.x, self.y, 1 - self.c)
        self.chips = [_chip_at(self.x, self.y, r) for r in (1, 2, 3)]

    def _rows(self, b, px, py, pc):
        m_per = self.ins[b].shape[0]
        return self.outs[b].at[pl.ds((4 * px + 2 * py + pc) * m_per, m_per), :]

    def _copy(self, b, k, blk, to, from_input=False):
        return pltpu.make_async_remote_copy(
            src_ref=self.ins[b] if from_input else self._rows(b, *blk), dst_ref=self._rows(b, *blk),
            send_sem=self.send_sems.at[7 * b + k], recv_sem=self.recv_sems.at[7 * b + k],
            device_id=to, device_id_type=MESH)

    def _local(self, b):
        return pltpu.make_async_copy(self.ins[b], self._rows(b, *self.me), self.local_sems.at[b])

    def _first(self, b):
        c = self.c
        return [self._copy(b, 0, self.me, self.sibling, from_input=True)] + [
            self._copy(b, 1 + j, self.me, (*chip, c), from_input=True) for j, chip in enumerate(self.chips)]

    def start(self):
        for b in range(len(self.ins)):
            self._local(b).start()
            for cp in self._first(b):
                cp.start()

    def forward(self):
        for b in range(len(self.ins)):
            for j, chip in enumerate(self.chips):
                self._copy(b, 1 + j, (*chip, self.c), self.me).wait_recv()
                self._copy(b, 4 + j, (*chip, self.c), self.sibling).start()

    def finish(self):
        for b in range(len(self.ins)):
            self._copy(b, 0, self.sibling, self.me).wait_recv()
            for j, chip in enumerate(self.chips):
                self._copy(b, 4 + j, (*chip, 1 - self.c), self.me).wait_recv()
        for b in range(len(self.ins)):
            for cp in self._first(b):
                cp.wait_send()
            for j, chip in enumerate(self.chips):
                self._copy(b, 4 + j, (*chip, self.c), self.sibling).wait_send()
            self._local(b).wait()


def _gather_rows(ins, outs, send_sems, recv_sems, local_sems, after_issue=None):
    g = _RowGather(ins, outs, send_sems, recv_sems, local_sems)
    g.start()
    if after_issue is not None:
        after_issue()
    g.forward()
    g.finish()


def _gather_rows_shapes(blocks):
    return [SDS((N_DEV * b.shape[0], b.shape[1]), b.dtype) for b in blocks]


def _gather_rows_sems(nb):
    return [pltpu.SemaphoreType.DMA((7 * nb,)), pltpu.SemaphoreType.DMA((7 * nb,)), pltpu.SemaphoreType.DMA((nb,))]


def _place_shard(w_shard, axis, chip_idx, name):
    R, C = w_shard.shape
    rb = _row_block(R)
    nb = R // rb
    full = (R * N_CHIPS, C) if axis == 0 else (R, C * N_CHIPS)
    omap = (lambda i, j: (j[0] * nb + i, 0)) if axis == 0 else (lambda i, j: (i, j[0]))

    def body(j_ref, w_ref, o_ref):
        o_ref[...] = w_ref[...].astype(BF16)

    return pl.pallas_call(
        body,
        grid_spec=pltpu.PrefetchScalarGridSpec(
            num_scalar_prefetch=1, grid=(nb,), in_specs=[pl.BlockSpec((rb, C), lambda i, j: (i, 0))],
            out_specs=pl.BlockSpec((rb, C), omap)),
        out_shape=SDS(full, BF16), compiler_params=_arb(), name=name)(chip_idx, w_shard)


class _WeightGather:
    def __init__(self, refs, axes, send_sems, recv_sems):
        self.refs, self.axes, self.send_sems, self.recv_sems = refs, axes, send_sems, recv_sems
        self.x, self.y, self.c = _position()
        self.j = 2 * self.x + self.y
        self.n = 3 * len(refs)

    def _half(self, w, chip_idx, half):
        ref, axis = self.refs[w], self.axes[w]
        if axis == 0:
            size = ref.shape[0] // N_CHIPS
            return ref.at[pl.ds(chip_idx * size + half * (size // 2), size // 2), :]
        size = ref.shape[1] // N_CHIPS
        rows = ref.shape[0] // 2
        return ref.at[pl.ds(half * rows, rows), pl.ds(chip_idx * size, size)]

    def _ici(self, w, r, chip_idx):
        k = 3 * w + r - 1
        piece = self._half(w, chip_idx, self.c)
        return pltpu.make_async_remote_copy(
            src_ref=piece, dst_ref=piece, send_sem=self.send_sems.at[k], recv_sem=self.recv_sems.at[k],
            device_id=(*_chip_at(self.x, self.y, r), self.c), device_id_type=MESH)

    def _d2d(self, w, r, half):
        k = self.n + 3 * w + r - 1
        piece = self._half(w, self.j ^ r, half)
        return pltpu.make_async_remote_copy(
            src_ref=piece, dst_ref=piece, send_sem=self.send_sems.at[k], recv_sem=self.recv_sems.at[k],
            device_id=(self.x, self.y, 1 - self.c), device_id_type=MESH)

    def _each(self):
        return [(w, r) for w in range(len(self.refs)) for r in (1, 2, 3)]

    def start(self):
        for w, r in self._each():
            self._ici(w, r, self.j).start()

    def forward(self):
        for w, r in self._each():
            self._ici(w, r, self.j ^ r).wait_recv()
            self._d2d(w, r, self.c).start()

    def finish(self):
        for w, r in self._each():
            self._ici(w, r, self.j).wait_send()
            self._d2d(w, r, self.c).wait_send()
            self._d2d(w, r, 1 - self.c).wait_recv()


def _gather_sems(n_weights):
    return [pltpu.SemaphoreType.DMA((6 * n_weights,)), pltpu.SemaphoreType.DMA((6 * n_weights,))]


def _gather_w_in_and_ada(placed, axis, c_block, w_ada):
    n = w_ada.shape[1]

    def body(w_any, c_ref, wada_ref, w_out, call_ref, cact_ref, pall_ref, p_scr, *sems):
        g = _WeightGather([w_out], [axis], *sems[:2])
        _gather_rows([c_ref], [call_ref], *sems[2:5], after_issue=g.start)
        pick = (lax.broadcasted_iota(jnp.int32, (N_DEV, N_DEV * 8), 1)
                == 8 * lax.broadcasted_iota(jnp.int32, (N_DEV, N_DEV * 8), 0)).astype(F32)
        cv = _dot(pick, call_ref[...], precision=HIGHEST)
        ca = cv * _sigmoid(cv)
        cact_ref[...] = ca
        p_scr[...] = _dot(ca, wada_ref[...], precision=HIGHEST)
        products = _RowGather([p_scr], [pall_ref], *sems[5:])
        products.start()
        g.forward()
        products.forward()
        products.finish()
        g.finish()

    anyspec = pl.BlockSpec(memory_space=pl.ANY)
    vmem = pl.BlockSpec(memory_space=pltpu.VMEM)
    rows = N_DEV * c_block.shape[0]
    res = pl.pallas_call(
        body, out_shape=[SDS(placed.shape, placed.dtype), SDS((rows, D), F32), SDS((N_DEV, D), F32), SDS((rows, n), F32)],
        in_specs=[anyspec, vmem, vmem], out_specs=[anyspec, vmem, vmem, vmem],
        scratch_shapes=[pltpu.VMEM((N_DEV, n), F32)] + _gather_sems(1) + _gather_rows_sems(1) + _gather_rows_sems(1),
        input_output_aliases={0: 0}, name="gather_w_in_and_ada")(placed, c_block, w_ada)
    return res[0], res[2], res[3]


class _ChipExchange:
    def __init__(self, ins, outs, send_sems, recv_sems):
        self.ins, self.outs, self.send_sems, self.recv_sems = ins, outs, send_sems, recv_sems
        self.x, self.y, self.c = _position()
        self.j = 2 * self.x + self.y

    def _copies(self):
        for w in range(len(self.ins)):
            for r in (1, 2, 3):
                k = 3 * w + r - 1
                yield pltpu.make_async_remote_copy(
                    src_ref=self.ins[w].at[self.j ^ r], dst_ref=self.outs[w].at[r - 1],
                    send_sem=self.send_sems.at[k], recv_sem=self.recv_sems.at[k],
                    device_id=(*_chip_at(self.x, self.y, r), self.c), device_id_type=MESH)

    def start(self):
        for cp in self._copies():
            cp.start()

    def finish(self):
        for cp in self._copies():
            cp.wait()


def _exchange_sems(n_weights):
    return [pltpu.SemaphoreType.DMA((3 * n_weights,)), pltpu.SemaphoreType.DMA((3 * n_weights,))]


class _CoreExchange:
    def __init__(self, ins, outs, send_sems, recv_sems):
        self.ins, self.outs, self.send_sems, self.recv_sems = ins, outs, send_sems, recv_sems
        self.x, self.y, self.c = _position()

    def _copies(self):
        for w in range(len(self.ins)):
            yield pltpu.make_async_remote_copy(
                src_ref=self.ins[w].at[:, 1 - self.c], dst_ref=self.outs[w],
                send_sem=self.send_sems.at[w], recv_sem=self.recv_sems.at[w],
                device_id=(self.x, self.y, 1 - self.c), device_id_type=MESH)

    def start(self):
        for cp in self._copies():
            cp.start()

    def finish(self):
        for cp in self._copies():
            cp.wait()


def _core_exchange_shapes(grads):
    return [SDS((g.shape[0], g.shape[2], g.shape[3]), g.dtype) for g in grads]


def _core_exchange_sems(n):
    return [pltpu.SemaphoreType.DMA((n,)), pltpu.SemaphoreType.DMA((n,))]


def _exchange_core_halves(grads, name):
    nw = len(grads)

    def body(*refs):
        ex = _CoreExchange(refs[:nw], refs[nw:2 * nw], *refs[2 * nw:])
        ex.start()
        ex.finish()

    anyspec = pl.BlockSpec(memory_space=pl.ANY)
    return pl.pallas_call(
        body, out_shape=_core_exchange_shapes(grads), in_specs=[anyspec] * nw, out_specs=[anyspec] * nw,
        scratch_shapes=_core_exchange_sems(nw), name=name)(*grads)


def _add_core_halves(g4, recv, c_idx, rb, name):
    ns, _, rh, C = g4.shape

    def body(c_ref, g_ref, r_ref, o_ref):
        o_ref[...] = (g_ref[0] + r_ref[...]).astype(BF16)

    return pl.pallas_call(
        body,
        grid_spec=pltpu.PrefetchScalarGridSpec(
            num_scalar_prefetch=1, grid=(ns, rh // rb),
            in_specs=[pl.BlockSpec((1, 1, rb, C), lambda s, i, cr: (s, cr[0], i, 0)),
                      pl.BlockSpec((1, rb, C), lambda s, i, cr: (s, i, 0))],
            out_specs=pl.BlockSpec((1, rb, C), lambda s, i, cr: (s, i, 0))),
        out_shape=SDS((ns, rh, C), BF16), compiler_params=_arb(2), name=name)(c_idx, g4, recv)


def _add_core_halves_in(g4, recv, c_idx, name):
    n_slabs, _, rh, C = g4.shape
    cb = 256
    per_slab, per_chip, n_blocks = C // cb, DIN // N_CHIPS // cb, DIN // cb

    def stored(s, k):
        sb = (per_chip * s + k + 4 * DH // cb) % n_blocks
        return sb // per_slab, sb % per_slab

    def body(c_ref, g_ref, r_ref, o_ref):
        o_ref[...] = (g_ref[0] + r_ref[...].astype(F32)).astype(BF16)

    return pl.pallas_call(
        body,
        grid_spec=pltpu.PrefetchScalarGridSpec(
            num_scalar_prefetch=1, grid=(N_CHIPS, per_chip),
            in_specs=[pl.BlockSpec((1, 1, rh, cb), lambda s, k, cr: (stored(s, k)[0], cr[0], 0, stored(s, k)[1])),
                      pl.BlockSpec((1, rh, cb), lambda s, k, cr: (stored(s, k)[0], 0, stored(s, k)[1]))],
            out_specs=pl.BlockSpec((1, rh, cb), lambda s, k, cr: (s, 0, k))),
        out_shape=SDS((N_CHIPS, rh, DIN // N_CHIPS), BF16), compiler_params=_arb(2), name=name)(c_idx, g4, recv)


def _slot_shapes(sums):
    return [SDS((3,) + s.shape[1:], s.dtype) for s in sums]


def _add_chips(own, slots, order, rb, name):
    _, rh, C = slots.shape

    def body(o_ref, own_ref, a_ref, b_ref, c_ref, d_ref, out_ref):
        mine = own_ref[0].astype(F32)
        t = [jnp.where(o_ref[i] == 0, mine, r[0].astype(F32)) for i, r in enumerate((a_ref, b_ref, c_ref, d_ref))]
        out_ref[...] = ((t[0] + t[1]) + t[2]) + t[3]

    def spec(i):
        return pl.BlockSpec((1, rb, C), lambda t, o: (jnp.maximum(o[i], 1) - 1, t, 0))

    return pl.pallas_call(
        body,
        grid_spec=pltpu.PrefetchScalarGridSpec(
            num_scalar_prefetch=1, grid=(rh // rb,),
            in_specs=[pl.BlockSpec((1, rb, C), lambda t, o: (o[4], t, 0)), spec(0), spec(1), spec(2), spec(3)],
            out_specs=pl.BlockSpec((rb, C), lambda t, o: (t, 0))),
        out_shape=SDS((rh, C), F32), compiler_params=_arb(), name=name)(order, own, slots, slots, slots, slots)


def _share_halves_and_gather(halves, row_blocks):
    nw, nr = len(halves), len(row_blocks)

    def body(*refs):
        ins, outs = refs[:nw], refs[nw + nr:2 * nw + nr]
        sems = refs[2 * (nw + nr):]
        send_sems, recv_sems = sems[:2]
        rows_gather = _RowGather(refs[nw:nw + nr], refs[2 * nw + nr:2 * (nw + nr)], *sems[2:])
        x, y, c = _position()
        rows_gather.start()
        started = []
        for w in range(nw):
            cp = pltpu.make_async_remote_copy(
                src_ref=ins[w], dst_ref=outs[w], send_sem=send_sems.at[w], recv_sem=recv_sems.at[w],
                device_id=(x, y, 1 - c), device_id_type=MESH)
            cp.start()
            started.append(cp)
        rows_gather.forward()
        rows_gather.finish()
        for cp in started:
            cp.wait()

    anyspec = pl.BlockSpec(memory_space=pl.ANY)
    vmem = pl.BlockSpec(memory_space=pltpu.VMEM)
    res = pl.pallas_call(
        body, out_shape=[SDS(h.shape, F32) for h in halves] + _gather_rows_shapes(row_blocks),
        in_specs=[anyspec] * nw + [vmem] * nr, out_specs=[anyspec] * nw + [vmem] * nr,
        scratch_shapes=[pltpu.SemaphoreType.DMA((nw,)), pltpu.SemaphoreType.DMA((nw,))] + _gather_rows_sems(nr),
        name="share_halves_and_gather")(*halves, *row_blocks)
    return res[:nw], res[nw:]


def _small_2d(b_ada, norm1_w, norm2_w, final_norm_w, v_ln_w, v_ln_b, lower_bounds, gn_w, b_s, w_s):
    return dict(zip(SMALL_NAMES, (b_ada, norm1_w, norm2_w, final_norm_w.reshape(1, D), v_ln_w, v_ln_b, lower_bounds, gn_w,
                                  b_s.reshape(NH, BLK), w_s.reshape(NH * BLK, BLK))))


def _small_original_shapes(d):
    out = dict(d)
    out['final_norm_w'] = d['final_norm_w'].reshape(D)
    out['b_s'] = d['b_s'].reshape(1, NH, BLK)
    out['w_s'] = d['w_s'].reshape(1, NH, BLK, BLK)
    return out


def _row_block(r):
    for cand in (256, 176, 128, 64, 32, 16, 8):
        if r % cand == 0:
            return cand
    return r


def kernel(x, c, w_ada, b_ada, norm1_w, w_in, w_s, b_s, v_ln_w, v_ln_b, lower_bounds, gn_w, w_out, norm2_w, w_ffn_in, w_ffn_out, final_norm_w, loss_target, m_w_ada, m_b_ada, m_norm1_w, m_w_in, m_w_s, m_b_s, m_v_ln_w, m_v_ln_b, m_lower_bounds, m_gn_w, m_w_out, m_norm2_w, m_w_ffn_in, m_w_ffn_out, m_final_norm_w, v_w_ada, v_b_ada, v_norm1_w, v_w_in, v_w_s, v_b_s, v_v_ln_w, v_v_ln_b, v_lower_bounds, v_gn_w, v_w_out, v_norm2_w, v_w_ffn_in, v_w_ffn_out, v_final_norm_w):
    T = x.shape[1]
    tm, tp = min(TOKEN_TILE, T), min(PROJ_TILE, T)
    px, py, pc = _position()
    chip = 2 * px + py
    me = 4 * px + 2 * py + pc
    x2d = x.reshape(T, D)
    tgt = loss_target.reshape(T, D)

    chip_idx = jnp.reshape(chip, (1,)).astype(jnp.int32)
    c_idx = jnp.reshape(pc, (1,)).astype(jnp.int32)
    w_in_b, cact, ada_all = _gather_w_in_and_ada(
        _place_shard(w_in[0], 1, chip_idx, "place_in"), 1, jnp.broadcast_to(c, (8, D)), w_ada[0])
    placed = [_place_shard(w_out[0], 0, chip_idx, "place_out"), _place_shard(w_ffn_in[0], 1, chip_idx, "place_ffn_in"),
              _place_shard(w_ffn_out[0], 0, chip_idx, "place_ffn_out")]

    n_ada = ada_all.shape[1]
    ada_all = ada_all.reshape(N_CHIPS, 2, N_DEV, n_ada)[:, 0]
    ada = lax.dynamic_index_in_dim(ada_all, me, axis=1, keepdims=False).reshape(1, 6 * D) + b_ada

    rr = lax.broadcasted_iota(jnp.int32, (BLK, BLK), 0) // CH
    cc = lax.broadcasted_iota(jnp.int32, (BLK, BLK), 1) // CH
    ws_b = jnp.where((rr >= cc)[None], w_s[0], 0.0).astype(BF16)
    bst = b_s[0].T
    lnw, lnb = v_ln_w, v_ln_b
    nw1, nw2, fw = norm1_w, norm2_w, final_norm_w.reshape(1, D)

    tables = _hgrn_tables()
    (h1, proj, ycat, o_pre, a_all, st_all), (w_out_b, w_fi_b, w_fo_b) = _proj_hgrn_fwd(
        x2d, nw1, ada, w_in_b, lower_bounds, gn_w, tables, placed, [0, 1, 0])

    dycat, dx1, h2, act, dff, dgu, dmix, acc2, ycat = _token_local(
        x2d, ycat, tgt, ada, nw2, ada, ada, ada, fw, w_out_b, w_fi_b, w_fo_b, proj, ws_b, bst, lnw, lnb, tm)

    tt = min(WGRAD_TOKENS, T)
    order = jnp.concatenate([chip ^ jnp.arange(N_CHIPS, dtype=jnp.int32), chip_idx]).astype(jnp.int32)

    def by_core_half(g):
        return g.reshape(g.shape[0], 2, g.shape[1] // 2, g.shape[2])

    def core_sums(g4, recv, names):
        return [_add_core_halves(a, b, c_idx, _row_block(a.shape[2]), "add_core_" + n) for a, b, n in zip(g4, recv, names)]

    def chip_sums(sums, slots, names):
        return [_add_chips(o, s, order, _row_block(s.shape[1]), "add_chips_" + n) for o, s, n in zip(sums, slots, names)]

    g_out = _wgrad(ycat, dmix, D, D, tt, "wgrad_out").reshape(N_CHIPS, D // N_CHIPS, D)
    g_fi = _wgrad(h2, dgu, D, FFB, tt, "wgrad_ffn_in")
    g_fo = _wgrad(act, dff, FFB, D, tt, "wgrad_ffn_out").reshape(N_CHIPS, DFF // N_CHIPS, D)
    late_names = ["out", "ffn_in", "ffn_out"]
    late_g4 = [by_core_half(g) for g in (g_out, g_fi, g_fo)]

    (dproj, dws, dbs, dln), late_recv = _gmlp_bwd(proj, dycat, ws_b, bst, lnw, lnb, late_g4)
    late_sums = core_sums(late_g4, late_recv, late_names)
    (dproj, dlb, dgn), late_slots, (acc2_all, dln_all, dbs_all, dws_all) = _hgrn_bwd(
        proj, o_pre, a_all, st_all, dycat, lower_bounds, gn_w, dproj, tables, late_sums, [acc2, dln, dbs, dws])

    g_in, g_in_wire = _wgrad(h1, dproj, D, D, tt, "wgrad_in", bf16_copy=True)
    (in_recv,) = _exchange_core_halves([by_core_half(g_in_wire)], "exchange_core_halves_in")
    in_sums = [_add_core_halves_in(by_core_half(g_in), in_recv, c_idx, "add_core_in")]
    (grad_x, acc1), in_slots = _proj_in_bwd(dproj, x2d, dx1, nw1, ada, w_in_b, tp, in_sums)
    names = ["in"] + late_names
    halves = chip_sums(in_sums, in_slots, ["in"]) + chip_sums(late_sums, late_slots, late_names)
    sibling_halves, (acc1_all, dlb_all, dgn_all) = _share_halves_and_gather(halves, [acc1, dlb, dgn])

    big_w = [(w_in, m_w_in, v_w_in), (w_out, m_w_out, v_w_out), (w_ffn_in, m_w_ffn_in, v_w_ffn_in),
             (w_ffn_out, m_w_ffn_out, v_w_ffn_out)]
    big_out = []
    for mine, sib, (w, m, v), n in zip(halves, sibling_halves, big_w, names):
        res = _adamw_halves(w[0], mine, sib, m[0], v[0], c_idx, _row_block(mine.shape[0]), "adamw_" + n)
        big_out.append([r[None] for r in res])

    gathered = [acc1_all, acc2_all, dln_all, dlb_all, dgn_all, dbs_all, dws_all]
    small, loss, dada_all = _small_finalize(
        gathered,
        _small_2d(b_ada, norm1_w, norm2_w, final_norm_w, v_ln_w, v_ln_b, lower_bounds, gn_w, b_s, w_s),
        _small_2d(m_b_ada, m_norm1_w, m_norm2_w, m_final_norm_w, m_v_ln_w, m_v_ln_b, m_lower_bounds, m_gn_w, m_b_s, m_w_s),
        _small_2d(v_b_ada, v_norm1_w, v_norm2_w, v_final_norm_w, v_v_ln_w, v_v_ln_b, v_lower_bounds, v_gn_w, v_b_s, v_w_s))
    small = [_small_original_shapes(d) for d in small]
    loss = loss.reshape(())

    ada_out = [o[None] for o in _ada_wgrad_adam(cact.T, dada_all, w_ada[0], m_w_ada[0], v_w_ada[0], chip_idx)]

    order_names = ['w_ada', 'b_ada', 'norm1_w', 'w_in', 'w_s', 'b_s', 'v_ln_w', 'v_ln_b', 'lower_bounds', 'gn_w',
                   'w_out', 'norm2_w', 'w_ffn_in', 'w_ffn_out', 'final_norm_w']
    big_idx = {'w_in': 0, 'w_out': 1, 'w_ffn_in': 2, 'w_ffn_out': 3}
    outs = [loss, grad_x.reshape(1, T, D)]
    for kind in range(4):
        for n in order_names:
            if n == 'w_ada':
                outs.append(ada_out[kind])
            elif n in big_idx:
                outs.append(big_out[big_idx[n]][kind])
            else:
                outs.append(small[kind][n])
    return tuple(outs)
```

```python
import jax
import jax.numpy as jnp
import numpy as np
from jax import lax
from jax.experimental import pallas as pl
from jax.experimental.pallas import tpu as pltpu

F32 = jnp.float32
BF16 = jnp.bfloat16
SDS = jax.ShapeDtypeStruct
MESH = pl.DeviceIdType.MESH
HIGHEST = lax.Precision.HIGHEST

D = 1024
DG = 512
DH = 512
NH = 4
HD = 128
BLK = 128
CH = 64
DFF = 2816
DIN = 3072
FFB = 1408
LEVELS = (64, 32, 16, 8, 4, 2)
HGRN_CHUNKS_PER_STEP = 8
GMLP_ROWS_PER_STEP = 1024
TOKEN_TILE = 256
PROJ_TILE = 1024
WGRAD_TOKENS = 2048
N_CHIPS = 4
N_DEV = 8
EPS = 1e-6
LR, B1, B2, AEPS, WD, STEP = 0.001, 0.9, 0.999, 1e-08, 0.01, 10

NT = (((1,), (1,)), ((), ()))
TN = (((0,), (0,)), ((), ()))


def _full(shape):
    nd = len(shape)
    return pl.BlockSpec(shape, lambda *_: (0,) * nd)


ADA_SH1, ADA_SC1, ADA_G1, ADA_SH2, ADA_SC2, ADA_G2 = range(6)


def _ada_part(k):
    return pl.BlockSpec((1, D), lambda *_: (0, k))


def _resident(shape):
    nd = len(shape)
    return pl.BlockSpec(shape, lambda *_: (0,) * nd, pipeline_mode=pl.Buffered(1))


def _arb(n=1):
    return pltpu.CompilerParams(dimension_semantics=("arbitrary",) * n)


def _dot(a, b, dims=None, precision=None):
    if dims is None:
        return jnp.dot(a, b, preferred_element_type=F32, precision=precision)
    return lax.dot_general(a, b, dims, preferred_element_type=F32, precision=precision)


def _sigmoid(x):
    return jax.nn.sigmoid(x)


def _gelu_parts(x):
    cdf = 0.5 * (1.0 + lax.erf(x * 0.7071067811865476))
    pdf = jnp.exp(-0.5 * x * x) * 0.3989422804014327
    return x * cdf, cdf + x * pdf


def _rms(x):
    return lax.rsqrt(jnp.mean(x * x, axis=-1, keepdims=True) + EPS)


def _rms_bwd(xhat, r, gw):
    return r * (gw - xhat * jnp.mean(xhat * gw, axis=-1, keepdims=True))


def _lower_bound(lbp_ref):
    l0, l1 = lbp_ref[0:1, :], lbp_ref[1:2, :]
    m = jnp.maximum(l0, l1)
    e0, e1 = jnp.exp(l0 - m), jnp.exp(l1 - m)
    return e0 / (e0 + e1), e1 / (e0 + e1)


def _gmlp_common(u, v, lnw, lnb, ws_ref, bst_ref):
    ug, dug = _gelu_parts(u)
    vg, dvg = _gelu_parts(v)
    mu = jnp.mean(vg, axis=-1, keepdims=True)
    vc = vg - mu
    rstd = lax.rsqrt(jnp.mean(vc * vc, axis=-1, keepdims=True) + EPS)
    vhat = vc * rstd
    vn = vhat * lnw + lnb
    vnb = vn.astype(BF16)
    mixed = []
    for h in range(NH):
        sl = slice(h * HD, (h + 1) * HD)
        mixed.append(_dot(ws_ref[h], vnb[:, sl]) + bst_ref[:, h:h + 1])
    return ug, dug, dvg, rstd, vhat, vnb, jnp.concatenate(mixed, axis=1)


def _hgrn_tables():
    t = np.arange(CH)[:, None]
    j = np.arange(CH)[None, :]
    blocks = [j <= t, j > t]
    masks = []
    for n in LEVELS:
        mid = t - t % n + n // 2
        blocks.append(np.where(t >= mid, (j >= mid) & (j <= t), (j > t) & (j < mid)))
        masks.append((t // n == j // n) & (t % n >= n // 2) & (j % n < n // 2))
    w = np.concatenate(blocks, axis=0).astype(np.float32)
    m = np.stack(masks).astype(np.float32)
    return (jnp.asarray(w, BF16), jnp.asarray(w.T, BF16), jnp.asarray(m), jnp.asarray(m + m.transpose(0, 2, 1)))


def _split_dot(w, x, parts):
    acc = None
    for _ in range(parts):
        piece = x.astype(BF16)
        term = _dot(w, piece)
        acc = term if acc is None else acc + term
        x = x - piece.astype(F32)
    return acc


def _hgrn_decays(f, w_ref):
    b = _split_dot(w_ref[0:CH, :], jnp.log(f), 3)
    row = lax.broadcasted_iota(jnp.int32, (CH, 1), 0)
    blocks = [jnp.exp(b), jnp.exp(b[CH - 1:CH, :] - b)]
    for n in LEVELS:
        up = (row & (n // 2)) != 0
        if n >= 8:
            ref = b.reshape(CH // n, n, DH)[:, n // 2 - 1:n // 2, :]
            ref = jnp.broadcast_to(ref, (CH // n, n, DH)).reshape(CH, DH)
            blocks.append(jnp.exp(jnp.where(up, b - ref, ref - b)))
        elif n == 4:
            r4 = row & 3
            two = jnp.where(r4 == 3, pltpu.roll(f, 1, 0) * f, 1.0)
            blocks.append(jnp.where(r4 == 0, pltpu.roll(f, CH - 1, 0), jnp.where(r4 == 2, f, two)))
        else:
            blocks.append(jnp.where(up, f, 1.0))
    return blocks


def _hgrn_gates(q, fl, lb, omlb, w_ref):
    sq = _sigmoid(q)
    qf = q * sq
    sig = _sigmoid(fl)
    f = lb + omlb * sig
    k = 1.0 - f
    return sq, qf, sig, f, k, _hgrn_decays(f, w_ref)


def _level_factor(e, li, sl, row, qh, kh):
    el = e[2 + li][:, sl]
    up = (row & (LEVELS[li] // 2)) != 0
    return el, up, el * jnp.where(up, qh, kh)


def _proj_hgrn_fwd(x, nw1, ada, w_in_b, lower_bounds, gn_w, tables, placed, axes):
    T = x.shape[0]
    nc = T // CH
    nch = min(HGRN_CHUNKS_PER_STEP, nc)
    steps = nc // nch
    w_st, _, masks, _ = tables
    nw = len(placed)
    pass_step = (13 * steps) // 16
    q0 = 2 * DG

    def body(*refs):
        x_ref, nw_ref, sc_ref, sh_ref, win_ref, lbp_ref, gn_ref, w_ref, m_ref = refs[:9]
        h_ref, p_ref, y_ref, o_ref, a_ref, st_ref = refs[9 + nw:15 + nw]
        s_scr, send_sems, recv_sems = refs[15 + 2 * nw:]
        gather = _WeightGather(refs[15 + nw:15 + 2 * nw], axes, send_sems, recv_sems)
        step = pl.program_id(0)
        xv = x_ref[...]
        hb = (((xv * _rms(xv)) * nw_ref[...]) * (1.0 + sc_ref[...]) + sh_ref[...]).astype(BF16)
        h_ref[...] = hb
        p_ref[...] = _dot(hb, win_ref[...])

        @pl.when(step == 0)
        def _():
            gather.start()
            s_scr[...] = jnp.zeros_like(s_scr)

        @pl.when(step == pass_step)
        def _():
            gather.forward()

        lb, omlb = _lower_bound(lbp_ref)
        row = lax.broadcasted_iota(jnp.int32, (CH, 1), 0)
        eye = lax.broadcasted_iota(jnp.int32, (CH, CH), 0) == lax.broadcasted_iota(jnp.int32, (CH, CH), 1)
        in_level = [m_ref[li] > 0.0 for li in range(len(LEVELS))]
        pre = []
        for ci in range(nch):
            rs = slice(ci * CH, (ci + 1) * CH)
            _, qf, _, _, k, e = _hgrn_gates(p_ref[rs, q0:q0 + DH], p_ref[rs, q0 + DH:q0 + 2 * DH], lb, omlb, w_ref)
            mats = []
            for h in range(NH):
                sl = slice(h * HD, (h + 1) * HD)
                qh, kh = qf[:, sl], k[:, sl]
                a = jnp.where(eye, jnp.sum(qh * kh, axis=-1, keepdims=True), 0.0)
                for li in range(len(LEVELS)):
                    _, _, y = _level_factor(e, li, sl, row, qh, kh)
                    yb = y.astype(BF16)
                    a = jnp.where(in_level[li], _dot(yb, yb, NT), a)
                a_ref[ci, h] = a
                mats.append(a.astype(BF16))
            eb = e[0]
            pre.append(((qf * eb).astype(BF16), eb[CH - 1:CH, :], (k * e[1]).astype(BF16), mats))
        for ci in range(nch):
            rs = slice(ci * CH, (ci + 1) * CH)
            qe, ebl, kd, mats = pre[ci]
            v = p_ref[rs, q0 + 2 * DH:q0 + 3 * DH]
            g = p_ref[rs, q0 + 3 * DH:q0 + 4 * DH]
            for h in range(NH):
                sl = slice(h * HD, (h + 1) * HD)
                st0 = s_scr[h]
                st_ref[ci, h] = st0
                vb = v[:, sl].astype(BF16)
                o = _dot(qe[:, sl], st0.astype(BF16), NT) + _dot(mats[h], vb)
                s_scr[h] = st0 * ebl[:, sl] + _dot(vb, kd[:, sl], TN)
                o_ref[rs, sl] = o
                gh = g[:, sl]
                y_ref[rs, sl] = (((o * _rms(o)) * gn_ref[...]) * (gh * _sigmoid(gh))).astype(BF16)

        @pl.when(step == steps - 1)
        def _():
            gather.finish()

    rows = nch * CH
    row = lambda c: (c, 0)
    anyspec = pl.BlockSpec(memory_space=pl.ANY)
    res = pl.pallas_call(
        body, grid=(steps,),
        in_specs=[pl.BlockSpec((rows, D), row), _full((1, D)), _ada_part(ADA_SC1), _ada_part(ADA_SH1), _resident((D, DIN)),
                  _full((2, DH)), _full((1, HD)), _full(w_st.shape), _full(masks.shape)] + [anyspec] * nw,
        out_specs=[pl.BlockSpec((rows, D), row), pl.BlockSpec((rows, DIN), row),
                   pl.BlockSpec((rows, DH), lambda c: (c, 1)),
                   pl.BlockSpec((rows, DH), row),
                   pl.BlockSpec((nch, NH, CH, CH), lambda c: (c, 0, 0, 0)),
                   pl.BlockSpec((nch, NH, HD, HD), lambda c: (c, 0, 0, 0))] + [anyspec] * nw,
        out_shape=[SDS((T, D), BF16), SDS((T, DIN), F32), SDS((T, D), BF16), SDS((T, DH), F32),
                   SDS((nc, NH, CH, CH), F32), SDS((nc, NH, HD, HD), F32)] + [SDS(a.shape, a.dtype) for a in placed],
        scratch_shapes=[pltpu.VMEM((NH, HD, HD), F32)] + _gather_sems(nw),
        input_output_aliases={9 + i: 6 + i for i in range(nw)},
        compiler_params=_arb(), name="proj_hgrn_fwd")(x, nw1, ada, ada, w_in_b, lower_bounds, gn_w, w_st, masks, *placed)
    return res[:6], res[6:]


def _token_local(x, ycat, tgt, g1, nw2, sc2, sh2, g2, fw, w_out_b, w_fi_b, w_fo_b, proj, ws_b, bst, lnw, lnb, tm):
    T = x.shape[0]
    inv_d = 1.0 / D

    def body(x_ref, yb_ref, t_ref, g1_ref, nw2_ref, sc2_ref, sh2_ref, g2_ref, fw_ref, wo_ref, wfi_ref, wfo_ref,
             u_ref, v_ref, ws_ref, bst_ref, lnw_ref, lnb_ref,
             dy_ref, dx1_ref, h2_ref, act_ref, dff_ref, dgu_ref, dmix_ref, acc_ref, ya_ref):
        @pl.when(pl.program_id(0) == 0)
        def _():
            acc_ref[...] = jnp.zeros_like(acc_ref)

        def acc(row, val):
            acc_ref[row:row + 1, :] += jnp.sum(val, axis=0, keepdims=True)

        for bi in range(tm // BLK):
            rs = slice(bi * BLK, (bi + 1) * BLK)
            ug, _, _, _, _, _, mixed = _gmlp_common(u_ref[rs, :], v_ref[rs, :], lnw_ref[...], lnb_ref[...], ws_ref, bst_ref)
            ya_ref[rs, :] = (ug * mixed).astype(BF16)
        g1v, g2v = g1_ref[...], g2_ref[...]
        mix = _dot(ya_ref[...], wo_ref[0:DG, :]) + _dot(yb_ref[...], wo_ref[DG:D, :])
        x1 = x_ref[...] + g1v * mix
        r2 = _rms(x1)
        xh2 = x1 * r2
        n2 = xh2 * nw2_ref[...]
        osc2 = 1.0 + sc2_ref[...]
        h2b = (n2 * osc2 + sh2_ref[...]).astype(BF16)
        h2_ref[...] = h2b
        ff = jnp.zeros((tm, D), F32)
        saved = []
        for kb in range(DFF // FFB):
            gate = _dot(h2b, wfi_ref[:, kb * FFB:(kb + 1) * FFB])
            up = _dot(h2b, wfi_ref[:, DFF + kb * FFB:DFF + (kb + 1) * FFB])
            sg = _sigmoid(gate)
            actb = (gate * sg * up).astype(BF16)
            act_ref[:, kb * FFB:(kb + 1) * FFB] = actb
            ff = ff + _dot(actb, wfo_ref[kb * FFB:(kb + 1) * FFB, :])
            saved.append((gate, up, sg))
        x2 = x1 + g2v * ff
        r3 = _rms(x2)
        xh3 = x2 * r3
        err = xh3 * fw_ref[...] - t_ref[...]
        acc(6, (0.5 * inv_d) * err * err)
        dy = err * inv_d
        acc(4, dy * xh3)
        dx2 = _rms_bwd(xh3, r3, dy * fw_ref[...])
        acc(0, dx2 * ff)
        dffb = (dx2 * g2v).astype(BF16)
        dff_ref[...] = dffb
        dh2 = jnp.zeros((tm, D), F32)
        for kb in range(DFF // FFB):
            gate, up, sg = saved[kb]
            da = _dot(dffb, wfo_ref[kb * FFB:(kb + 1) * FFB, :], NT)
            dgate = (da * up * (sg * (1.0 + gate * (1.0 - sg)))).astype(BF16)
            dup = (da * gate * sg).astype(BF16)
            dgu_ref[:, kb * FFB:(kb + 1) * FFB] = dgate
            dgu_ref[:, DFF + kb * FFB:DFF + (kb + 1) * FFB] = dup
            dh2 = dh2 + _dot(dgate, wfi_ref[:, kb * FFB:(kb + 1) * FFB], NT)
            dh2 = dh2 + _dot(dup, wfi_ref[:, DFF + kb * FFB:DFF + (kb + 1) * FFB], NT)
        acc(2, dh2)
        acc(1, dh2 * n2)
        dn2 = dh2 * osc2
        acc(3, dn2 * xh2)
        dx1 = dx2 + _rms_bwd(xh2, r2, dn2 * nw2_ref[...])
        acc(5, dx1 * mix)
        dmixb = (dx1 * g1v).astype(BF16)
        dmix_ref[...] = dmixb
        dy_ref[...] = _dot(dmixb, wo_ref[...], NT)
        dx1_ref[...] = dx1

    row = lambda i: (i, 0)
    vec = _full((1, D))
    half = lambda j: pl.BlockSpec((tm, DG), lambda i: (i, j))
    return pl.pallas_call(
        body, grid=(T // tm,),
        in_specs=[pl.BlockSpec((tm, D), row), half(1), pl.BlockSpec((tm, D), row),
                  _ada_part(ADA_G1), vec, _ada_part(ADA_SC2), _ada_part(ADA_SH2), _ada_part(ADA_G2), vec,
                  _resident((D, D)), _resident((D, 2 * DFF)), _resident((DFF, D)),
                  half(0), half(1), _full((NH, BLK, BLK)), _full((BLK, NH)), _full((1, DG)), _full((1, DG))],
        out_specs=[pl.BlockSpec((tm, D), row), pl.BlockSpec((tm, D), row), pl.BlockSpec((tm, D), row),
                   pl.BlockSpec((tm, DFF), row), pl.BlockSpec((tm, D), row), pl.BlockSpec((tm, 2 * DFF), row),
                   pl.BlockSpec((tm, D), row), _full((8, D)), half(0)],
        out_shape=[SDS((T, D), F32), SDS((T, D), F32), SDS((T, D), BF16), SDS((T, DFF), BF16), SDS((T, D), BF16),
                   SDS((T, 2 * DFF), BF16), SDS((T, D), BF16), SDS((8, D), F32), SDS((T, D), BF16)],
        input_output_aliases={1: 8},
        compiler_params=_arb(), name="token_local")(x, ycat, tgt, g1, nw2, sc2, sh2, g2, fw, w_out_b, w_fi_b, w_fo_b,
                                                    proj, proj, ws_b, bst, lnw, lnb)


def _gmlp_bwd(proj, dycat, ws_b, bst, lnw, lnb, grads):
    T = proj.shape[0]
    rows = min(GMLP_ROWS_PER_STEP, T)
    nb = T // rows
    nw = len(grads)

    def body(*refs):
        u_ref, v_ref, dy_ref, ws_ref, bst_ref, lnw_ref, lnb_ref = refs[:7]
        dp_ref, dws_ref, dbs_ref, dln_ref = refs[7 + nw:11 + nw]
        dbs_acc, send_sems, recv_sems = refs[11 + 2 * nw:]
        exchange = _CoreExchange(refs[7:7 + nw], refs[11 + nw:11 + 2 * nw], send_sems, recv_sems)
        i = pl.program_id(0)

        @pl.when(i == 0)
        def _():
            exchange.start()
            dws_ref[...] = jnp.zeros_like(dws_ref)
            dln_ref[...] = jnp.zeros_like(dln_ref)
            dbs_acc[...] = jnp.zeros_like(dbs_acc)

        r = lax.broadcasted_iota(jnp.int32, (BLK, BLK), 0) // CH
        c = lax.broadcasted_iota(jnp.int32, (BLK, BLK), 1) // CH
        for bi in range(rows // BLK):
            rs = slice(bi * BLK, (bi + 1) * BLK)
            ug, dug, dvg, rstd, vhat, vnb, mixed = _gmlp_common(
                u_ref[rs, :], v_ref[rs, :], lnw_ref[...], lnb_ref[...], ws_ref, bst_ref)
            dya = dy_ref[rs, :]
            dp_ref[rs, 0:DG] = (dya * mixed * dug).astype(BF16)
            dmixed = dya * ug
            dbs_acc[...] += dmixed
            dmb = dmixed.astype(BF16)
            dvn = []
            for h in range(NH):
                sl = slice(h * HD, (h + 1) * HD)
                dws_ref[h * BLK:(h + 1) * BLK, :] += jnp.where(r >= c, _dot(dmb[:, sl], vnb[:, sl], NT), 0.0)
                dvn.append(_dot(ws_ref[h], dmb[:, sl], TN))
            dvn = jnp.concatenate(dvn, axis=1)
            dln_ref[0:1, :] += jnp.sum(dvn * vhat, axis=0, keepdims=True)
            dln_ref[1:2, :] += jnp.sum(dvn, axis=0, keepdims=True)
            dvh = dvn * lnw_ref[...]
            dvgel = rstd * (dvh - jnp.mean(dvh, axis=-1, keepdims=True) - vhat * jnp.mean(dvh * vhat, axis=-1, keepdims=True))
            dp_ref[rs, DG:2 * DG] = (dvgel * dvg).astype(BF16)

        @pl.when(i == nb - 1)
        def _():
            head = lax.broadcasted_iota(jnp.int32, (8, BLK), 0)
            ones = jnp.ones((8, HD), F32)
            out = jnp.zeros((8, BLK), F32)
            for h in range(NH):
                sums = _dot(ones, dbs_acc[:, h * HD:(h + 1) * HD], NT, precision=HIGHEST)
                out = out + jnp.where(head == h, sums, 0.0)
            dbs_ref[...] = out
            exchange.finish()

    anyspec = pl.BlockSpec(memory_space=pl.ANY)
    res = pl.pallas_call(
        body, grid=(nb,),
        in_specs=[pl.BlockSpec((rows, DG), lambda i: (i, 0)), pl.BlockSpec((rows, DG), lambda i: (i, 1)),
                  pl.BlockSpec((rows, DG), lambda i: (i, 0)),
                  _full((NH, BLK, BLK)), _full((BLK, NH)), _full((1, DG)), _full((1, DG))] + [anyspec] * nw,
        out_specs=[pl.BlockSpec((rows, 2 * DG), lambda i: (i, 2)), _full((NH * BLK, BLK)), _full((8, BLK)), _full((8, DG))]
        + [anyspec] * nw,
        out_shape=[SDS((T, DIN), BF16), SDS((NH * BLK, BLK), F32), SDS((8, BLK), F32), SDS((8, DG), F32)]
        + _core_exchange_shapes(grads),
        scratch_shapes=[pltpu.VMEM((BLK, DG), F32)] + _core_exchange_sems(nw),
        compiler_params=_arb(), name="gmlp_bwd")(proj, proj, dycat, ws_b, bst, lnw, lnb, *grads)
    return res[:4], res[4:]


def _hgrn_bwd(proj, o_pre, a_all, st_all, dycat, lower_bounds, gn_w, dproj, tables, sums, row_blocks):
    T = proj.shape[0]
    nc = T // CH
    nch = min(HGRN_CHUNKS_PER_STEP, nc)
    steps = nc // nch
    w_st, w_st_t, _, masks_sym = tables
    n_lev = len(LEVELS)
    nw, nr = len(sums), len(row_blocks)

    def body(*refs):
        q_ref, f_ref, i_ref, g_ref, o_ref, a_ref, st_ref, dy_ref, lbp_ref, gn_ref, w_ref, wt_ref, ms_ref = refs[:13]
        n_in = 14 + nw + nr
        dp_ref, dlb_ref, dgn_ref = refs[n_in:n_in + 3]
        ds_scr, dx_scr = refs[n_in + 3 + nw + nr:n_in + 5 + nw + nr]
        sems = refs[n_in + 5 + nw + nr:]
        exchange = _ChipExchange(refs[14:14 + nw], refs[n_in + 3:n_in + 3 + nw], *sems[:2])
        rows_gather = _RowGather(refs[14 + nw:n_in], refs[n_in + 3 + nw:n_in + 3 + nw + nr], *sems[2:])
        i = pl.program_id(0)

        @pl.when(i == 0)
        def _():
            rows_gather.start()
            exchange.start()
            ds_scr[...] = jnp.zeros_like(ds_scr)
            dlb_ref[...] = jnp.zeros_like(dlb_ref)
            dgn_ref[...] = jnp.zeros_like(dgn_ref)

        @pl.when(i == steps // 2)
        def _():
            rows_gather.forward()

        lb, omlb = _lower_bound(lbp_ref)
        row = lax.broadcasted_iota(jnp.int32, (CH, 1), 0)
        eye = lax.broadcasted_iota(jnp.int32, (CH, CH), 0) == lax.broadcasted_iota(jnp.int32, (CH, CH), 1)
        lower = lax.broadcasted_iota(jnp.int32, (CH, CH), 0) > lax.broadcasted_iota(jnp.int32, (CH, CH), 1)
        def independent(ci, dgn):
            rs = slice(ci * CH, (ci + 1) * CH)
            q = q_ref[rs, :]
            v = i_ref[rs, :]
            g = g_ref[rs, :]
            sq, qf, sig, f, k, e = _hgrn_gates(q, f_ref[rs, :], lb, omlb, w_ref)
            eb = e[0]
            ekd = e[1]
            kd = k * ekd
            qe = qf * eb
            dob_h, dqe_h, dqf_h, dki_h, dv_h, dg_h = [], [], [], [], [], []
            for h in range(NH):
                sl = slice(h * HD, (h + 1) * HD)
                o = o_ref[rs, sl]
                ro = _rms(o)
                oh = o * ro
                gh = g[:, sl]
                sg = _sigmoid(gh)
                dyb = dy_ref[rs, sl]
                dg_h.append(dyb * (oh * gn_ref[...]) * (sg * (1.0 + gh * (1.0 - sg))))
                don = dyb * (gh * sg)
                dgn = dgn + jnp.sum(don * oh, axis=0, keepdims=True)
                dob = _rms_bwd(oh, ro, don * gn_ref[...]).astype(BF16)
                vb = v[:, sl].astype(BF16)
                qh, kh = qf[:, sl], k[:, sl]
                dqe = _dot(dob, st_ref[ci, h].astype(BF16))
                da = _dot(dob, vb, NT)
                ddiag = jnp.sum(jnp.where(eye, da, 0.0), axis=-1, keepdims=True)
                dsym = jnp.where(lower, da, _dot(vb, dob, NT))
                upper_part = jnp.zeros((CH, HD), F32)
                both = jnp.zeros((CH, HD), F32)
                for li in range(n_lev):
                    el, up, y = _level_factor(e, li, sl, row, qh, kh)
                    dyv = _dot((ms_ref[li] * dsym).astype(BF16), y.astype(BF16))
                    dx_scr[ci, (2 + li) * CH:(3 + li) * CH, sl] = dyv * y
                    dye = dyv * el
                    upper_part = upper_part + jnp.where(up, dye, 0.0)
                    both = both + dye
                dob_h.append(dob)
                dqe_h.append(dqe)
                dqf_h.append(dqe * eb[:, sl] + ddiag * kh + upper_part)
                dki_h.append(ddiag * qh + (both - upper_part))
                dv_h.append(_dot(a_ref[ci, h].astype(BF16), dob, TN))
            dp_ref[rs, 0:DH] = (jnp.concatenate(dqf_h, axis=1) * (sq * (1.0 + q * (1.0 - sq)))).astype(BF16)
            dp_ref[rs, 3 * DH:4 * DH] = jnp.concatenate(dg_h, axis=1).astype(BF16)
            return (v, sig, f, eb, ekd, kd, qe, dob_h, jnp.concatenate(dqe_h, axis=1), dki_h, dv_h), dgn

        dgn = jnp.zeros((1, HD), F32)
        for ci in reversed(range(nch)):
            rs = slice(ci * CH, (ci + 1) * CH)
            (v, sig, f, eb, ekd, kd, qe, dob_h, dqe, dki_h, dv_h), dgn = independent(ci, dgn)
            ebl = eb[CH - 1:CH, :]
            dbl_h, dkd_h, dv2_h = [], [], []
            for h in range(NH):
                sl = slice(h * HD, (h + 1) * HD)
                dst1 = ds_scr[h]
                dst1b = dst1.astype(BF16)
                ds_scr[h] = dst1 * ebl[:, sl] + _dot(dob_h[h], qe[:, sl].astype(BF16), TN)
                dbl_h.append(ebl[:, sl] * jnp.sum(st_ref[ci, h] * dst1, axis=0, keepdims=True))
                dkd_h.append(_dot(v[:, sl].astype(BF16), dst1b))
                dv2_h.append(dv_h[h] + _dot(kd[:, sl].astype(BF16), dst1b, NT))
            dkd = jnp.concatenate(dkd_h, axis=1)
            dx_scr[ci, 0:CH, :] = dqe * qe + jnp.where(row == CH - 1, jnp.concatenate(dbl_h, axis=1), 0.0)
            dx_scr[ci, CH:2 * CH, :] = dkd * kd
            dlf = _split_dot(wt_ref[...], dx_scr[ci], 2)
            df = dlf / f - (dkd * ekd + jnp.concatenate(dki_h, axis=1))
            dlb_ref[0:1, :] += jnp.sum(df * (1.0 - sig), axis=0, keepdims=True)
            dp_ref[rs, DH:2 * DH] = (df * omlb * sig * (1.0 - sig)).astype(BF16)
            dp_ref[rs, 2 * DH:3 * DH] = jnp.concatenate(dv2_h, axis=1).astype(BF16)
        dgn_ref[0:1, :] += dgn

        @pl.when(i == steps - 1)
        def _():
            gl = dlb_ref[0:1, :] * lb * omlb
            dlb_ref[0:1, :] = gl
            dlb_ref[1:2, :] = -gl
            exchange.finish()
            rows_gather.finish()

    rev = lambda j: pl.BlockSpec((nch * CH, DH), lambda c: (steps - 1 - c, j))
    anyspec = pl.BlockSpec(memory_space=pl.ANY)
    res = pl.pallas_call(
        body, grid=(steps,),
        in_specs=[rev(2), rev(3), rev(4), rev(5), rev(0),
                  pl.BlockSpec((nch, NH, CH, CH), lambda c: (steps - 1 - c, 0, 0, 0)),
                  pl.BlockSpec((nch, NH, HD, HD), lambda c: (steps - 1 - c, 0, 0, 0)),
                  rev(1), _full((2, DH)), _full((1, HD)),
                  _full(w_st.shape), _full(w_st_t.shape), _full(masks_sym.shape),
                  anyspec] + [anyspec] * (nw + nr),
        out_specs=[pl.BlockSpec((nch * CH, 4 * DH), lambda c: (steps - 1 - c, 0)), _full((8, DH)), _full((8, HD))]
        + [anyspec] * (nw + nr),
        out_shape=[SDS((T, DIN), BF16), SDS((8, DH), F32), SDS((8, HD), F32)] + _slot_shapes(sums)
        + _gather_rows_shapes(row_blocks),
        scratch_shapes=[pltpu.VMEM((NH, HD, HD), F32), pltpu.VMEM((nch, (2 + n_lev) * CH, DH), F32)]
        + _exchange_sems(nw) + _gather_rows_sems(nr),
        input_output_aliases={13: 0},
        compiler_params=_arb(), name="hgrn_bwd")(proj, proj, proj, proj, o_pre, a_all, st_all, dycat, lower_bounds, gn_w,
                                                 w_st, w_st_t, masks_sym, dproj, *sums, *row_blocks)
    return res[:3], res[3:3 + nw], res[3 + nw:]


def _proj_in_bwd(dproj, x, dx1, nw, sc, w_in_b, tm, sums):
    T = x.shape[0]
    ns = len(sums)
    steps = T // tm

    def body(*refs):
        dp_ref, x_ref, dx1_ref, nw_ref, sc_ref, w_ref = refs[:6]
        gx_ref, acc_ref = refs[6 + ns:8 + ns]
        exchange = _ChipExchange(refs[6:6 + ns], refs[8 + ns:8 + 2 * ns], *refs[8 + 2 * ns:])

        @pl.when(pl.program_id(0) == 0)
        def _():
            exchange.start()
            acc_ref[...] = jnp.zeros_like(acc_ref)

        dh = _dot(dp_ref[:, 0:4 * DH], w_ref[:, 2 * DG:DIN], NT) + _dot(dp_ref[:, 4 * DH:DIN], w_ref[:, 0:2 * DG], NT)
        xv = x_ref[...]
        r = _rms(xv)
        xh = xv * r
        n1 = xh * nw_ref[...]
        acc_ref[0:1, :] += jnp.sum(dh, axis=0, keepdims=True)
        acc_ref[1:2, :] += jnp.sum(dh * n1, axis=0, keepdims=True)
        dn = dh * (1.0 + sc_ref[...])
        acc_ref[2:3, :] += jnp.sum(dn * xh, axis=0, keepdims=True)
        gx_ref[...] = dx1_ref[...] + _rms_bwd(xh, r, dn * nw_ref[...])

        @pl.when(pl.program_id(0) == steps - 1)
        def _():
            exchange.finish()

    row = lambda i: (i, 0)
    anyspec = pl.BlockSpec(memory_space=pl.ANY)
    res = pl.pallas_call(
        body, grid=(steps,),
        in_specs=[pl.BlockSpec((tm, DIN), row), pl.BlockSpec((tm, D), row), pl.BlockSpec((tm, D), row),
                  _full((1, D)), _ada_part(ADA_SC1), _resident((D, DIN))] + [anyspec] * ns,
        out_specs=[pl.BlockSpec((tm, D), row), _full((8, D))] + [anyspec] * ns,
        out_shape=[SDS((T, D), F32), SDS((8, D), F32)] + _slot_shapes(sums),
        scratch_shapes=_exchange_sems(ns),
        compiler_params=_arb(), name="proj_in_bwd")(dproj, x, dx1, nw, sc, w_in_b, *sums)
    return res[:2], res[2:]


def _wgrad(a, b, bk, bn, tt, name, bf16_copy=False):
    T, K = a.shape
    N = b.shape[1]
    nn, nk, nt = N // bn, K // bk, T // tt
    bmap = lambda n, k, t: (t, n)

    def body(a_ref, b_ref, o_ref, *copy_ref):
        @pl.when(pl.program_id(2) == 0)
        def _():
            o_ref[...] = jnp.zeros_like(o_ref)

        o_ref[0] += _dot(a_ref[...], b_ref[...], TN)

        if bf16_copy:
            @pl.when(pl.program_id(2) == nt - 1)
            def _():
                copy_ref[0][...] = o_ref[...].astype(BF16)

    ospec = pl.BlockSpec((1, bk, bn), lambda n, k, t: (n, k, 0))
    return pl.pallas_call(
        body, grid=(nn, nk, nt),
        in_specs=[pl.BlockSpec((tt, bk), lambda n, k, t: (t, k)), pl.BlockSpec((tt, bn), bmap)],
        out_specs=[ospec, ospec] if bf16_copy else ospec,
        out_shape=[SDS((nn, K, bn), F32), SDS((nn, K, bn), BF16)] if bf16_copy else SDS((nn, K, bn), F32),
        compiler_params=_arb(3), name=name)(a, b)


def _adam_math(w, g, m, v):
    m = B1 * m + (1.0 - B1) * g
    v = B2 * v + (1.0 - B2) * (g * g)
    m_hat = m / (1.0 - B1 ** STEP)
    v_hat = v / (1.0 - B2 ** STEP)
    return -LR * (m_hat / (jnp.sqrt(v_hat) + AEPS) + WD * w), m, v


def _adamw_halves(w, mine, sibling, m, v, c_idx, rb, name):
    R, C = w.shape
    nb = (R // 2) // rb

    def body(c_ref, w_ref, a_ref, b_ref, m_ref, v_ref, g_out, d_out, m_out, v_out):
        g = jnp.where(pl.program_id(0) == c_ref[0], a_ref[...], b_ref[...])
        g_out[...] = g
        d_out[...], m_out[...], v_out[...] = _adam_math(w_ref[...], g, m_ref[...], v_ref[...])

    whole = pl.BlockSpec((rb, C), lambda hh, i, cr: (hh * nb + i, 0))
    half = pl.BlockSpec((rb, C), lambda hh, i, cr: (i, 0))
    return pl.pallas_call(
        body,
        grid_spec=pltpu.PrefetchScalarGridSpec(
            num_scalar_prefetch=1, grid=(2, nb), in_specs=[whole, half, half, whole, whole], out_specs=[whole] * 4),
        out_shape=[SDS((R, C), F32)] * 4, compiler_params=_arb(2), name=name)(c_idx, w, mine, sibling, m, v)


def _ada_wgrad_adam(cact_t, dada_all, w, m, v, chip_idx):
    R, C = w.shape
    rb = 256

    def body(j_ref, c_ref, d_ref, w_ref, m_ref, v_ref, g_out, d_out, m_out, v_out):
        g = _dot(c_ref[...], d_ref[...], precision=HIGHEST)
        g_out[...] = g
        d_out[...], m_out[...], v_out[...] = _adam_math(w_ref[...], g, m_ref[...], v_ref[...])

    spec = pl.BlockSpec((rb, C), lambda i, j: (i, 0))
    return pl.pallas_call(
        body,
        grid_spec=pltpu.PrefetchScalarGridSpec(
            num_scalar_prefetch=1, grid=(R // rb,),
            in_specs=[pl.BlockSpec((rb, N_DEV), lambda i, j: (i, 0)), pl.BlockSpec((N_DEV, C), lambda i, j: (0, j[0])),
                      spec, spec, spec],
            out_specs=[spec] * 4),
        out_shape=[SDS((R, C), F32)] * 4,
        compiler_params=_arb(), name="ada_wgrad_adam")(chip_idx, cact_t, dada_all, w, m, v)


SMALL_NAMES = ('b_ada', 'norm1_w', 'norm2_w', 'final_norm_w', 'v_ln_w', 'v_ln_b', 'lower_bounds', 'gn_w', 'b_s', 'w_s')


def _small_finalize(gathered, params, moms, vels):
    n_in = len(gathered)

    def body(*refs):
        acc1, acc2, dln, dlb, dgn, dbs, dws = refs[:n_in]
        prm = [dict(zip(SMALL_NAMES, refs[n_in + k * 10:n_in + (k + 1) * 10])) for k in range(3)]
        outs = [dict(zip(SMALL_NAMES, refs[n_in + 30 + k * 10:n_in + 30 + (k + 1) * 10])) for k in range(4)]
        loss_ref, dada_ref = refs[n_in + 70:n_in + 72]

        def dev_sum(ref, first, n):
            per = ref.shape[0] // N_DEV
            g = ref[first:first + n, :]
            for dev in range(1, N_DEV):
                g = g + ref[dev * per + first:dev * per + first + n, :]
            return g

        def update(n, g, cols=slice(None)):
            outs[0][n][:, cols] = g
            outs[1][n][:, cols], outs[2][n][:, cols], outs[3][n][:, cols] = _adam_math(
                prm[0][n][:, cols], g, prm[1][n][:, cols], prm[2][n][:, cols])

        ada_rows = ((acc1, 0), (acc1, 1), (acc2, 5), (acc2, 2), (acc2, 1), (acc2, 0))
        for k, (ref, r) in enumerate(ada_rows):
            update('b_ada', dev_sum(ref, r, 1), slice(k * D, (k + 1) * D))
            for dev in range(N_DEV):
                dada_ref[dev:dev + 1, k * D:(k + 1) * D] = ref[8 * dev + r:8 * dev + r + 1, :]
        update('norm1_w', dev_sum(acc1, 2, 1))
        update('norm2_w', dev_sum(acc2, 3, 1))
        update('final_norm_w', dev_sum(acc2, 4, 1))
        update('v_ln_w', dev_sum(dln, 0, 1))
        update('v_ln_b', dev_sum(dln, 1, 1))
        update('lower_bounds', dev_sum(dlb, 0, 2))
        update('gn_w', dev_sum(dgn, 0, 1))
        update('b_s', dev_sum(dbs, 0, NH))
        update('w_s', dev_sum(dws, 0, NH * BLK))
        loss_ref[...] = jnp.sum(dev_sum(acc2, 6, 1), axis=-1, keepdims=True)

    shapes = [SDS(params[n].shape, F32) for n in SMALL_NAMES]
    res = pl.pallas_call(
        body, out_shape=shapes * 4 + [SDS((1, 1), F32), SDS((N_DEV, 6 * D), F32)], name="small_finalize")(
            *gathered, *[d[n] for d in (params, moms, vels) for n in SMALL_NAMES])
    return [dict(zip(SMALL_NAMES, res[k * 10:(k + 1) * 10])) for k in range(4)], res[40], res[41]


def _position():
    x, y, c = lax.axis_index("x"), lax.axis_index("y"), lax.axis_index("c")
    return x, y, c


def _chip_at(x, y, r):
    return (x ^ (r >> 1), y ^ (r & 1))


class _RowGather:
    def __init__(self, ins, outs, send_sems, recv_sems, local_sems):
        self.ins, self.outs = ins, outs
        self.send_sems, self.recv_sems, self.local_sems = send_sems, recv_sems, local_sems
        self.x, self.y, self.c = _position()
        self.me, self.sibling = (self.x, self.y, self.c), (self.x, self.y, 1 - self.c)
        self.chips = [_chip_at(self.x, self.y, r) for r in (1, 2, 3)]

    def _rows(self, b, px, py, pc):
        m_per = self.ins[b].shape[0]
        return self.outs[b].at[pl.ds((4 * px + 2 * py + pc) * m_per, m_per), :]

    def _copy(self, b, k, blk, to, from_input=False):
        return pltpu.make_async_remote_copy(
            src_ref=self.ins[b] if from_input else self._rows(b, *blk), dst_ref=self._rows(b, *blk),
            send_sem=self.send_sems.at[7 * b + k], recv_sem=self.recv_sems.at[7 * b + k],
            device_id=to, device_id_type=MESH)

    def _local(self, b):
        return pltpu.make_async_copy(self.ins[b], self._rows(b, *self.me), self.local_sems.at[b])

    def _first(self, b):
        c = self.c
        return [self._copy(b, 0, self.me, self.sibling, from_input=True)] + [
            self._copy(b, 1 + j, self.me, (*chip, c), from_input=True) for j, chip in enumerate(self.chips)]

    def start(self):
        for b in range(len(self.ins)):
            self._local(b).start()
            for cp in self._first(b):
                cp.start()

    def forward(self):
        for b in range(len(self.ins)):
            for j, chip in enumerate(self.chips):
                self._copy(b, 1 + j, (*chip, self.c), self.me).wait_recv()
                self._copy(b, 4 + j, (*chip, self.c), self.sibling).start()

    def finish(self):
        for b in range(len(self.ins)):
            self._copy(b, 0, self.sibling, self.me).wait_recv()
            for j, chip in enumerate(self.chips):
                self._copy(b, 4 + j, (*chip, 1 - self.c), self.me).wait_recv()
        for b in range(len(self.ins)):
            for cp in self._first(b):
                cp.wait_send()
            for j, chip in enumerate(self.chips):
                self._copy(b, 4 + j, (*chip, self.c), self.sibling).wait_send()
            self._local(b).wait()


def _gather_rows(ins, outs, send_sems, recv_sems, local_sems, after_issue=None):
    g = _RowGather(ins, outs, send_sems, recv_sems, local_sems)
    g.start()
    if after_issue is not None:
        after_issue()
    g.forward()
    g.finish()


def _gather_rows_shapes(blocks):
    return [SDS((N_DEV * b.shape[0], b.shape[1]), b.dtype) for b in blocks]


def _gather_rows_sems(nb):
    return [pltpu.SemaphoreType.DMA((7 * nb,)), pltpu.SemaphoreType.DMA((7 * nb,)), pltpu.SemaphoreType.DMA((nb,))]


def _place_shard(w_shard, axis, chip_idx, name):
    R, C = w_shard.shape
    rb = _row_block(R)
    nb = R // rb
    full = (R * N_CHIPS, C) if axis == 0 else (R, C * N_CHIPS)
    omap = (lambda i, j: (j[0] * nb + i, 0)) if axis == 0 else (lambda i, j: (i, j[0]))

    def body(j_ref, w_ref, o_ref):
        o_ref[...] = w_ref[...].astype(BF16)

    return pl.pallas_call(
        body,
        grid_spec=pltpu.PrefetchScalarGridSpec(
            num_scalar_prefetch=1, grid=(nb,), in_specs=[pl.BlockSpec((rb, C), lambda i, j: (i, 0))],
            out_specs=pl.BlockSpec((rb, C), omap)),
        out_shape=SDS(full, BF16), compiler_params=_arb(), name=name)(chip_idx, w_shard)


class _WeightGather:
    def __init__(self, refs, axes, send_sems, recv_sems):
        self.refs, self.axes, self.send_sems, self.recv_sems = refs, axes, send_sems, recv_sems
        self.x, self.y, self.c = _position()
        self.j = 2 * self.x + self.y
        self.n = 3 * len(refs)

    def _half(self, w, chip_idx, half):
        ref, axis = self.refs[w], self.axes[w]
        if axis == 0:
            size = ref.shape[0] // N_CHIPS
            return ref.at[pl.ds(chip_idx * size + half * (size // 2), size // 2), :]
        size = ref.shape[1] // N_CHIPS
        rows = ref.shape[0] // 2
        return ref.at[pl.ds(half * rows, rows), pl.ds(chip_idx * size, size)]

    def _ici(self, w, r, chip_idx):
        k = 3 * w + r - 1
        piece = self._half(w, chip_idx, self.c)
        return pltpu.make_async_remote_copy(
            src_ref=piece, dst_ref=piece, send_sem=self.send_sems.at[k], recv_sem=self.recv_sems.at[k],
            device_id=(*_chip_at(self.x, self.y, r), self.c), device_id_type=MESH)

    def _d2d(self, w, r, half):
        k = self.n + 3 * w + r - 1
        piece = self._half(w, self.j ^ r, half)
        return pltpu.make_async_remote_copy(
            src_ref=piece, dst_ref=piece, send_sem=self.send_sems.at[k], recv_sem=self.recv_sems.at[k],
            device_id=(self.x, self.y, 1 - self.c), device_id_type=MESH)

    def _each(self):
        return [(w, r) for w in range(len(self.refs)) for r in (1, 2, 3)]

    def start(self):
        for w, r in self._each():
            self._ici(w, r, self.j).start()

    def forward(self):
        for w, r in self._each():
            self._ici(w, r, self.j ^ r).wait_recv()
            self._d2d(w, r, self.c).start()

    def finish(self):
        for w, r in self._each():
            self._ici(w, r, self.j).wait_send()
            self._d2d(w, r, self.c).wait_send()
            self._d2d(w, r, 1 - self.c).wait_recv()


def _gather_sems(n_weights):
    return [pltpu.SemaphoreType.DMA((6 * n_weights,)), pltpu.SemaphoreType.DMA((6 * n_weights,))]


def _gather_w_in_and_ada(placed, axis, c_block, w_ada):
    n = w_ada.shape[1]

    def body(w_any, c_ref, wada_ref, w_out, call_ref, cact_ref, pall_ref, p_scr, *sems):
        g = _WeightGather([w_out], [axis], *sems[:2])
        _gather_rows([c_ref], [call_ref], *sems[2:5], after_issue=g.start)
        pick = (lax.broadcasted_iota(jnp.int32, (N_DEV, N_DEV * 8), 1)
                == 8 * lax.broadcasted_iota(jnp.int32, (N_DEV, N_DEV * 8), 0)).astype(F32)
        cv = _dot(pick, call_ref[...], precision=HIGHEST)
        ca = cv * _sigmoid(cv)
        cact_ref[...] = ca
        p_scr[...] = _dot(ca, wada_ref[...], precision=HIGHEST)
        products = _RowGather([p_scr], [pall_ref], *sems[5:])
        products.start()
        g.forward()
        products.forward()
        products.finish()
        g.finish()

    anyspec = pl.BlockSpec(memory_space=pl.ANY)
    vmem = pl.BlockSpec(memory_space=pltpu.VMEM)
    rows = N_DEV * c_block.shape[0]
    res = pl.pallas_call(
        body, out_shape=[SDS(placed.shape, placed.dtype), SDS((rows, D), F32), SDS((N_DEV, D), F32), SDS((rows, n), F32)],
        in_specs=[anyspec, vmem, vmem], out_specs=[anyspec, vmem, vmem, vmem],
        scratch_shapes=[pltpu.VMEM((N_DEV, n), F32)] + _gather_sems(1) + _gather_rows_sems(1) + _gather_rows_sems(1),
        input_output_aliases={0: 0}, name="gather_w_in_and_ada")(placed, c_block, w_ada)
    return res[0], res[2], res[3]


class _ChipExchange:
    def __init__(self, ins, outs, send_sems, recv_sems):
        self.ins, self.outs, self.send_sems, self.recv_sems = ins, outs, send_sems, recv_sems
        self.x, self.y, self.c = _position()
        self.j = 2 * self.x + self.y

    def _copies(self):
        for w in range(len(self.ins)):
            for r in (1, 2, 3):
                k = 3 * w + r - 1
                yield pltpu.make_async_remote_copy(
                    src_ref=self.ins[w].at[self.j ^ r], dst_ref=self.outs[w].at[r - 1],
                    send_sem=self.send_sems.at[k], recv_sem=self.recv_sems.at[k],
                    device_id=(*_chip_at(self.x, self.y, r), self.c), device_id_type=MESH)

    def start(self):
        for cp in self._copies():
            cp.start()

    def finish(self):
        for cp in self._copies():
            cp.wait()


def _exchange_sems(n_weights):
    return [pltpu.SemaphoreType.DMA((3 * n_weights,)), pltpu.SemaphoreType.DMA((3 * n_weights,))]


class _CoreExchange:
    def __init__(self, ins, outs, send_sems, recv_sems):
        self.ins, self.outs, self.send_sems, self.recv_sems = ins, outs, send_sems, recv_sems
        self.x, self.y, self.c = _position()

    def _copies(self):
        for w in range(len(self.ins)):
            yield pltpu.make_async_remote_copy(
                src_ref=self.ins[w].at[:, 1 - self.c], dst_ref=self.outs[w],
                send_sem=self.send_sems.at[w], recv_sem=self.recv_sems.at[w],
                device_id=(self.x, self.y, 1 - self.c), device_id_type=MESH)

    def start(self):
        for cp in self._copies():
            cp.start()

    def finish(self):
        for cp in self._copies():
            cp.wait()


def _core_exchange_shapes(grads):
    return [SDS((g.shape[0], g.shape[2], g.shape[3]), g.dtype) for g in grads]


def _core_exchange_sems(n):
    return [pltpu.SemaphoreType.DMA((n,)), pltpu.SemaphoreType.DMA((n,))]


def _exchange_core_halves(grads, name):
    nw = len(grads)

    def body(*refs):
        ex = _CoreExchange(refs[:nw], refs[nw:2 * nw], *refs[2 * nw:])
        ex.start()
        ex.finish()

    anyspec = pl.BlockSpec(memory_space=pl.ANY)
    return pl.pallas_call(
        body, out_shape=_core_exchange_shapes(grads), in_specs=[anyspec] * nw, out_specs=[anyspec] * nw,
        scratch_shapes=_core_exchange_sems(nw), name=name)(*grads)


def _add_core_halves(g4, recv, c_idx, rb, name):
    ns, _, rh, C = g4.shape

    def body(c_ref, g_ref, r_ref, o_ref):
        o_ref[...] = (g_ref[0] + r_ref[...]).astype(BF16)

    return pl.pallas_call(
        body,
        grid_spec=pltpu.PrefetchScalarGridSpec(
            num_scalar_prefetch=1, grid=(ns, rh // rb),
            in_specs=[pl.BlockSpec((1, 1, rb, C), lambda s, i, cr: (s, cr[0], i, 0)),
                      pl.BlockSpec((1, rb, C), lambda s, i, cr: (s, i, 0))],
            out_specs=pl.BlockSpec((1, rb, C), lambda s, i, cr: (s, i, 0))),
        out_shape=SDS((ns, rh, C), BF16), compiler_params=_arb(2), name=name)(c_idx, g4, recv)


def _add_core_halves_in(g4, recv, c_idx, name):
    n_slabs, _, rh, C = g4.shape
    cb = 256
    per_slab, per_chip, n_blocks = C // cb, DIN // N_CHIPS // cb, DIN // cb

    def stored(s, k):
        sb = (per_chip * s + k + 4 * DH // cb) % n_blocks
        return sb // per_slab, sb % per_slab

    def body(c_ref, g_ref, r_ref, o_ref):
        o_ref[...] = (g_ref[0] + r_ref[...].astype(F32)).astype(BF16)

    return pl.pallas_call(
        body,
        grid_spec=pltpu.PrefetchScalarGridSpec(
            num_scalar_prefetch=1, grid=(N_CHIPS, per_chip),
            in_specs=[pl.BlockSpec((1, 1, rh, cb), lambda s, k, cr: (stored(s, k)[0], cr[0], 0, stored(s, k)[1])),
                      pl.BlockSpec((1, rh, cb), lambda s, k, cr: (stored(s, k)[0], 0, stored(s, k)[1]))],
            out_specs=pl.BlockSpec((1, rh, cb), lambda s, k, cr: (s, 0, k))),
        out_shape=SDS((N_CHIPS, rh, DIN // N_CHIPS), BF16), compiler_params=_arb(2), name=name)(c_idx, g4, recv)


def _slot_shapes(sums):
    return [SDS((3,) + s.shape[1:], s.dtype) for s in sums]


def _add_chips(own, slots, order, rb, name):
    _, rh, C = slots.shape

    def body(o_ref, own_ref, a_ref, b_ref, c_ref, d_ref, out_ref):
        mine = own_ref[0].astype(F32)
        t = [jnp.where(o_ref[i] == 0, mine, r[0].astype(F32)) for i, r in enumerate((a_ref, b_ref, c_ref, d_ref))]
        out_ref[...] = ((t[0] + t[1]) + t[2]) + t[3]

    def spec(i):
        return pl.BlockSpec((1, rb, C), lambda t, o: (jnp.maximum(o[i], 1) - 1, t, 0))

    return pl.pallas_call(
        body,
        grid_spec=pltpu.PrefetchScalarGridSpec(
            num_scalar_prefetch=1, grid=(rh // rb,),
            in_specs=[pl.BlockSpec((1, rb, C), lambda t, o: (o[4], t, 0)), spec(0), spec(1), spec(2), spec(3)],
            out_specs=pl.BlockSpec((rb, C), lambda t, o: (t, 0))),
        out_shape=SDS((rh, C), F32), compiler_params=_arb(), name=name)(order, own, slots, slots, slots, slots)


def _share_halves_and_gather(halves, row_blocks):
    nw, nr = len(halves), len(row_blocks)

    def body(*refs):
        ins, outs = refs[:nw], refs[nw + nr:2 * nw + nr]
        sems = refs[2 * (nw + nr):]
        send_sems, recv_sems = sems[:2]
        rows_gather = _RowGather(refs[nw:nw + nr], refs[2 * nw + nr:2 * (nw + nr)], *sems[2:])
        x, y, c = _position()
        rows_gather.start()
        started = []
        for w in range(nw):
            cp = pltpu.make_async_remote_copy(
                src_ref=ins[w], dst_ref=outs[w], send_sem=send_sems.at[w], recv_sem=recv_sems.at[w],
                device_id=(x, y, 1 - c), device_id_type=MESH)
            cp.start()
            started.append(cp)
        rows_gather.forward()
        rows_gather.finish()
        for cp in started:
            cp.wait()

    anyspec = pl.BlockSpec(memory_space=pl.ANY)
    vmem = pl.BlockSpec(memory_space=pltpu.VMEM)
    res = pl.pallas_call(
        body, out_shape=[SDS(h.shape, F32) for h in halves] + _gather_rows_shapes(row_blocks),
        in_specs=[anyspec] * nw + [vmem] * nr, out_specs=[anyspec] * nw + [vmem] * nr,
        scratch_shapes=[pltpu.SemaphoreType.DMA((nw,)), pltpu.SemaphoreType.DMA((nw,))] + _gather_rows_sems(nr),
        name="share_halves_and_gather")(*halves, *row_blocks)
    return res[:nw], res[nw:]


def _small_2d(b_ada, norm1_w, norm2_w, final_norm_w, v_ln_w, v_ln_b, lower_bounds, gn_w, b_s, w_s):
    return dict(zip(SMALL_NAMES, (b_ada, norm1_w, norm2_w, final_norm_w.reshape(1, D), v_ln_w, v_ln_b, lower_bounds, gn_w,
                                  b_s.reshape(NH, BLK), w_s.reshape(NH * BLK, BLK))))


def _small_original_shapes(d):
    out = dict(d)
    out['final_norm_w'] = d['final_norm_w'].reshape(D)
    out['b_s'] = d['b_s'].reshape(1, NH, BLK)
    out['w_s'] = d['w_s'].reshape(1, NH, BLK, BLK)
    return out


def _row_block(r):
    for cand in (256, 176, 128, 64, 32, 16, 8):
        if r % cand == 0:
            return cand
    return r


def kernel(x, c, w_ada, b_ada, norm1_w, w_in, w_s, b_s, v_ln_w, v_ln_b, lower_bounds, gn_w, w_out, norm2_w, w_ffn_in, w_ffn_out, final_norm_w, loss_target, m_w_ada, m_b_ada, m_norm1_w, m_w_in, m_w_s, m_b_s, m_v_ln_w, m_v_ln_b, m_lower_bounds, m_gn_w, m_w_out, m_norm2_w, m_w_ffn_in, m_w_ffn_out, m_final_norm_w, v_w_ada, v_b_ada, v_norm1_w, v_w_in, v_w_s, v_b_s, v_v_ln_w, v_v_ln_b, v_lower_bounds, v_gn_w, v_w_out, v_norm2_w, v_w_ffn_in, v_w_ffn_out, v_final_norm_w):
    T = x.shape[1]
    tm, tp = min(TOKEN_TILE, T), min(PROJ_TILE, T)
    px, py, pc = _position()
    chip = 2 * px + py
    me = 4 * px + 2 * py + pc
    x2d = x.reshape(T, D)
    tgt = loss_target.reshape(T, D)

    chip_idx = jnp.reshape(chip, (1,)).astype(jnp.int32)
    c_idx = jnp.reshape(pc, (1,)).astype(jnp.int32)
    w_in_b, cact, ada_all = _gather_w_in_and_ada(
        _place_shard(w_in[0], 1, chip_idx, "place_in"), 1, jnp.broadcast_to(c, (8, D)), w_ada[0])
    placed = [_place_shard(w_out[0], 0, chip_idx, "place_out"), _place_shard(w_ffn_in[0], 1, chip_idx, "place_ffn_in"),
              _place_shard(w_ffn_out[0], 0, chip_idx, "place_ffn_out")]

    n_ada = ada_all.shape[1]
    ada_all = ada_all.reshape(N_CHIPS, 2, N_DEV, n_ada)[:, 0]
    ada = lax.dynamic_index_in_dim(ada_all, me, axis=1, keepdims=False).reshape(1, 6 * D) + b_ada

    rr = lax.broadcasted_iota(jnp.int32, (BLK, BLK), 0) // CH
    cc = lax.broadcasted_iota(jnp.int32, (BLK, BLK), 1) // CH
    ws_b = jnp.where((rr >= cc)[None], w_s[0], 0.0).astype(BF16)
    bst = b_s[0].T
    lnw, lnb = v_ln_w, v_ln_b
    nw1, nw2, fw = norm1_w, norm2_w, final_norm_w.reshape(1, D)

    tables = _hgrn_tables()
    (h1, proj, ycat, o_pre, a_all, st_all), (w_out_b, w_fi_b, w_fo_b) = _proj_hgrn_fwd(
        x2d, nw1, ada, w_in_b, lower_bounds, gn_w, tables, placed, [0, 1, 0])

    dycat, dx1, h2, act, dff, dgu, dmix, acc2, ycat = _token_local(
        x2d, ycat, tgt, ada, nw2, ada, ada, ada, fw, w_out_b, w_fi_b, w_fo_b, proj, ws_b, bst, lnw, lnb, tm)

    tt = min(WGRAD_TOKENS, T)
    order = jnp.concatenate([chip ^ jnp.arange(N_CHIPS, dtype=jnp.int32), chip_idx]).astype(jnp.int32)

    def by_core_half(g):
        return g.reshape(g.shape[0], 2, g.shape[1] // 2, g.shape[2])

    def core_sums(g4, recv, names):
        return [_add_core_halves(a, b, c_idx, _row_block(a.shape[2]), "add_core_" + n) for a, b, n in zip(g4, recv, names)]

    def chip_sums(sums, slots, names):
        return [_add_chips(o, s, order, _row_block(s.shape[1]), "add_chips_" + n) for o, s, n in zip(sums, slots, names)]

    g_out = _wgrad(ycat, dmix, D, D, tt, "wgrad_out").reshape(N_CHIPS, D // N_CHIPS, D)
    g_fi = _wgrad(h2, dgu, D, FFB, tt, "wgrad_ffn_in")
    g_fo = _wgrad(act, dff, FFB, D, tt, "wgrad_ffn_out").reshape(N_CHIPS, DFF // N_CHIPS, D)
    late_names = ["out", "ffn_in", "ffn_out"]
    late_g4 = [by_core_half(g) for g in (g_out, g_fi, g_fo)]

    (dproj, dws, dbs, dln), late_recv = _gmlp_bwd(proj, dycat, ws_b, bst, lnw, lnb, late_g4)
    late_sums = core_sums(late_g4, late_recv, late_names)
    (dproj, dlb, dgn), late_slots, (acc2_all, dln_all, dbs_all, dws_all) = _hgrn_bwd(
        proj, o_pre, a_all, st_all, dycat, lower_bounds, gn_w, dproj, tables, late_sums, [acc2, dln, dbs, dws])

    g_in, g_in_wire = _wgrad(h1, dproj, D, D, tt, "wgrad_in", bf16_copy=True)
    (in_recv,) = _exchange_core_halves([by_core_half(g_in_wire)], "exchange_core_halves_in")
    in_sums = [_add_core_halves_in(by_core_half(g_in), in_recv, c_idx, "add_core_in")]
    (grad_x, acc1), in_slots = _proj_in_bwd(dproj, x2d, dx1, nw1, ada, w_in_b, tp, in_sums)
    names = ["in"] + late_names
    halves = chip_sums(in_sums, in_slots, ["in"]) + chip_sums(late_sums, late_slots, late_names)
    sibling_halves, (acc1_all, dlb_all, dgn_all) = _share_halves_and_gather(halves, [acc1, dlb, dgn])

    big_w = [(w_in, m_w_in, v_w_in), (w_out, m_w_out, v_w_out), (w_ffn_in, m_w_ffn_in, v_w_ffn_in),
             (w_ffn_out, m_w_ffn_out, v_w_ffn_out)]
    big_out = []
    for mine, sib, (w, m, v), n in zip(halves, sibling_halves, big_w, names):
        res = _adamw_halves(w[0], mine, sib, m[0], v[0], c_idx, _row_block(mine.shape[0]), "adamw_" + n)
        big_out.append([r[None] for r in res])

    gathered = [acc1_all, acc2_all, dln_all, dlb_all, dgn_all, dbs_all, dws_all]
    small, loss, dada_all = _small_finalize(
        gathered,
        _small_2d(b_ada, norm1_w, norm2_w, final_norm_w, v_ln_w, v_ln_b, lower_bounds, gn_w, b_s, w_s),
        _small_2d(m_b_ada, m_norm1_w, m_norm2_w, m_final_norm_w, m_v_ln_w, m_v_ln_b, m_lower_bounds, m_gn_w, m_b_s, m_w_s),
        _small_2d(v_b_ada, v_norm1_w, v_norm2_w, v_final_norm_w, v_v_ln_w, v_v_ln_b, v_lower_bounds, v_gn_w, v_b_s, v_w_s))
    small = [_small_original_shapes(d) for d in small]
    loss = loss.reshape(())

    ada_out = [o[None] for o in _ada_wgrad_adam(cact.T, dada_all, w_ada[0], m_w_ada[0], v_w_ada[0], chip_idx)]

    order_names = ['w_ada', 'b_ada', 'norm1_w', 'w_in', 'w_s', 'b_s', 'v_ln_w', 'v_ln_b', 'lower_bounds', 'gn_w',
                   'w_out', 'norm2_w', 'w_ffn_in', 'w_ffn_out', 'final_norm_w']
    big_idx = {'w_in': 0, 'w_out': 1, 'w_ffn_in': 2, 'w_ffn_out': 3}
    outs = [loss, grad_x.reshape(1, T, D)]
    for kind in range(4):
        for n in order_names:
            if n == 'w_ada':
                outs.append(ada_out[kind])
            elif n in big_idx:
                outs.append(big_out[big_idx[n]][kind])
            else:
                outs.append(small[kind][n])
    return tuple(outs)
```

```python
import jax
import jax.numpy as jnp
import numpy as np
from jax import lax
from jax.experimental import pallas as pl
from jax.experimental.pallas import tpu as pltpu

F32 = jnp.float32
BF16 = jnp.bfloat16
SDS = jax.ShapeDtypeStruct
MESH = pl.DeviceIdType.MESH
HIGHEST = lax.Precision.HIGHEST

D = 1024
DG = 512
DH = 512
NH = 4
HD = 128
BLK = 128
CH = 64
DFF = 2816
DIN = 3072
FFB = 1408
LEVELS = (64, 32, 16, 8, 4, 2)
HGRN_CHUNKS_PER_STEP = 8
GMLP_ROWS_PER_STEP = 1024
TOKEN_TILE = 256
PROJ_TILE = 1024
WGRAD_TOKENS = 2048
WGRAD_TOKENS_SQUARE = 4096
N_CHIPS = 4
N_DEV = 8
EPS = 1e-6
LR, B1, B2, AEPS, WD, STEP = 0.001, 0.9, 0.999, 1e-08, 0.01, 10

NT = (((1,), (1,)), ((), ()))
TN = (((0,), (0,)), ((), ()))


def _full(shape):
    nd = len(shape)
    return pl.BlockSpec(shape, lambda *_: (0,) * nd)


ADA_SH1, ADA_SC1, ADA_G1, ADA_SH2, ADA_SC2, ADA_G2 = range(6)


def _ada_part(k):
    return pl.BlockSpec((1, D), lambda *_: (0, k))


def _resident(shape):
    nd = len(shape)
    return pl.BlockSpec(shape, lambda *_: (0,) * nd, pipeline_mode=pl.Buffered(1))


def _arb(n=1):
    return pltpu.CompilerParams(dimension_semantics=("arbitrary",) * n)


def _dot(a, b, dims=None, precision=None):
    if dims is None:
        return jnp.dot(a, b, preferred_element_type=F32, precision=precision)
    return lax.dot_general(a, b, dims, preferred_element_type=F32, precision=precision)


def _sigmoid(x):
    return jax.nn.sigmoid(x)


def _gelu_parts(x):
    cdf = 0.5 * (1.0 + lax.erf(x * 0.7071067811865476))
    pdf = jnp.exp(-0.5 * x * x) * 0.3989422804014327
    return x * cdf, cdf + x * pdf


def _rms(x):
    return lax.rsqrt(jnp.mean(x * x, axis=-1, keepdims=True) + EPS)


def _rms_bwd(xhat, r, gw):
    return r * (gw - xhat * jnp.mean(xhat * gw, axis=-1, keepdims=True))


def _lower_bound(lbp_ref):
    l0, l1 = lbp_ref[0:1, :], lbp_ref[1:2, :]
    m = jnp.maximum(l0, l1)
    e0, e1 = jnp.exp(l0 - m), jnp.exp(l1 - m)
    return e0 / (e0 + e1), e1 / (e0 + e1)


def _gmlp_common(u, v, lnw, lnb, ws_ref, bst_ref):
    ug, dug = _gelu_parts(u)
    vg, dvg = _gelu_parts(v)
    mu = jnp.mean(vg, axis=-1, keepdims=True)
    vc = vg - mu
    rstd = lax.rsqrt(jnp.mean(vc * vc, axis=-1, keepdims=True) + EPS)
    vhat = vc * rstd
    vn = vhat * lnw + lnb
    vnb = vn.astype(BF16)
    mixed = []
    for h in range(NH):
        sl = slice(h * HD, (h + 1) * HD)
        mixed.append(_dot(ws_ref[h], vnb[:, sl]) + bst_ref[:, h:h + 1])
    return ug, dug, dvg, rstd, vhat, vnb, jnp.concatenate(mixed, axis=1)


def _hgrn_tables():
    t = np.arange(CH)[:, None]
    j = np.arange(CH)[None, :]
    blocks = [j <= t, j > t]
    masks = []
    for n in LEVELS:
        mid = t - t % n + n // 2
        blocks.append(np.where(t >= mid, (j >= mid) & (j <= t), (j > t) & (j < mid)))
        masks.append((t // n == j // n) & (t % n >= n // 2) & (j % n < n // 2))
    w = np.concatenate(blocks, axis=0).astype(np.float32)
    m = np.stack(masks).astype(np.float32)
    return (jnp.asarray(w, BF16), jnp.asarray(w.T, BF16), jnp.asarray(m), jnp.asarray(m + m.transpose(0, 2, 1)))


def _split_dot(w, x, parts):
    acc = None
    for _ in range(parts):
        piece = x.astype(BF16)
        term = _dot(w, piece)
        acc = term if acc is None else acc + term
        x = x - piece.astype(F32)
    return acc


def _hgrn_decays(f, w_ref):
    b = _split_dot(w_ref[0:CH, :], jnp.log(f), 3)
    row = lax.broadcasted_iota(jnp.int32, (CH, 1), 0)
    blocks = [jnp.exp(b), jnp.exp(b[CH - 1:CH, :] - b)]
    for n in LEVELS:
        up = (row & (n // 2)) != 0
        if n >= 8:
            ref = b.reshape(CH // n, n, DH)[:, n // 2 - 1:n // 2, :]
            ref = jnp.broadcast_to(ref, (CH // n, n, DH)).reshape(CH, DH)
            blocks.append(jnp.exp(jnp.where(up, b - ref, ref - b)))
        elif n == 4:
            r4 = row & 3
            two = jnp.where(r4 == 3, pltpu.roll(f, 1, 0) * f, 1.0)
            blocks.append(jnp.where(r4 == 0, pltpu.roll(f, CH - 1, 0), jnp.where(r4 == 2, f, two)))
        else:
            blocks.append(jnp.where(up, f, 1.0))
    return blocks


def _hgrn_gates(q, fl, lb, omlb, w_ref):
    sq = _sigmoid(q)
    qf = q * sq
    sig = _sigmoid(fl)
    f = lb + omlb * sig
    k = 1.0 - f
    return sq, qf, sig, f, k, _hgrn_decays(f, w_ref)


def _level_factor(e, li, sl, row, qh, kh):
    el = e[2 + li][:, sl]
    up = (row & (LEVELS[li] // 2)) != 0
    return el, up, el * jnp.where(up, qh, kh)


def _proj_hgrn_fwd(x, nw1, ada, w_in_b, lower_bounds, gn_w, tables, placed, axes):
    T = x.shape[0]
    nc = T // CH
    nch = min(HGRN_CHUNKS_PER_STEP, nc)
    steps = nc // nch
    w_st, _, masks, _ = tables
    nw = len(placed)
    pass_step = (13 * steps) // 16
    q0 = 2 * DG

    def body(*refs):
        x_ref, nw_ref, sc_ref, sh_ref, win_ref, lbp_ref, gn_ref, w_ref, m_ref = refs[:9]
        h_ref, p_ref, y_ref, o_ref, a_ref, st_ref = refs[9 + nw:15 + nw]
        s_scr, send_sems, recv_sems = refs[15 + 2 * nw:]
        gather = _WeightGather(refs[15 + nw:15 + 2 * nw], axes, send_sems, recv_sems)
        step = pl.program_id(0)
        xv = x_ref[...]
        hb = (((xv * _rms(xv)) * nw_ref[...]) * (1.0 + sc_ref[...]) + sh_ref[...]).astype(BF16)
        h_ref[...] = hb
        p_ref[...] = _dot(hb, win_ref[...])

        @pl.when(step == 0)
        def _():
            gather.start()
            s_scr[...] = jnp.zeros_like(s_scr)

        @pl.when(step == pass_step)
        def _():
            gather.forward()

        lb, omlb = _lower_bound(lbp_ref)
        row = lax.broadcasted_iota(jnp.int32, (CH, 1), 0)
        eye = lax.broadcasted_iota(jnp.int32, (CH, CH), 0) == lax.broadcasted_iota(jnp.int32, (CH, CH), 1)
        in_level = [m_ref[li] > 0.0 for li in range(len(LEVELS))]
        pre = []
        for ci in range(nch):
            rs = slice(ci * CH, (ci + 1) * CH)
            _, qf, _, _, k, e = _hgrn_gates(p_ref[rs, q0:q0 + DH], p_ref[rs, q0 + DH:q0 + 2 * DH], lb, omlb, w_ref)
            mats = []
            for h in range(NH):
                sl = slice(h * HD, (h + 1) * HD)
                qh, kh = qf[:, sl], k[:, sl]
                a = jnp.where(eye, jnp.sum(qh * kh, axis=-1, keepdims=True), 0.0)
                for li in range(len(LEVELS)):
                    _, _, y = _level_factor(e, li, sl, row, qh, kh)
                    yb = y.astype(BF16)
                    a = jnp.where(in_level[li], _dot(yb, yb, NT), a)
                a_ref[ci, h] = a
                mats.append(a.astype(BF16))
            eb = e[0]
            pre.append(((qf * eb).astype(BF16), eb[CH - 1:CH, :], (k * e[1]).astype(BF16), mats))
        for ci in range(nch):
            rs = slice(ci * CH, (ci + 1) * CH)
            qe, ebl, kd, mats = pre[ci]
            v = p_ref[rs, q0 + 2 * DH:q0 + 3 * DH]
            g = p_ref[rs, q0 + 3 * DH:q0 + 4 * DH]
            for h in range(NH):
                sl = slice(h * HD, (h + 1) * HD)
                st0 = s_scr[h]
                st_ref[ci, h] = st0
                vb = v[:, sl].astype(BF16)
                o = _dot(qe[:, sl], st0.astype(BF16), NT) + _dot(mats[h], vb)
                s_scr[h] = st0 * ebl[:, sl] + _dot(vb, kd[:, sl], TN)
                o_ref[rs, sl] = o
                gh = g[:, sl]
                y_ref[rs, sl] = (((o * _rms(o)) * gn_ref[...]) * (gh * _sigmoid(gh))).astype(BF16)

        @pl.when(step == steps - 1)
        def _():
            gather.finish()

    rows = nch * CH
    row = lambda c: (c, 0)
    anyspec = pl.BlockSpec(memory_space=pl.ANY)
    res = pl.pallas_call(
        body, grid=(steps,),
        in_specs=[pl.BlockSpec((rows, D), row), _full((1, D)), _ada_part(ADA_SC1), _ada_part(ADA_SH1), _resident((D, DIN)),
                  _full((2, DH)), _full((1, HD)), _full(w_st.shape), _full(masks.shape)] + [anyspec] * nw,
        out_specs=[pl.BlockSpec((rows, D), row), pl.BlockSpec((rows, DIN), row),
                   pl.BlockSpec((rows, DH), lambda c: (c, 1)),
                   pl.BlockSpec((rows, DH), row),
                   pl.BlockSpec((nch, NH, CH, CH), lambda c: (c, 0, 0, 0)),
                   pl.BlockSpec((nch, NH, HD, HD), lambda c: (c, 0, 0, 0))] + [anyspec] * nw,
        out_shape=[SDS((T, D), BF16), SDS((T, DIN), F32), SDS((T, D), BF16), SDS((T, DH), F32),
                   SDS((nc, NH, CH, CH), F32), SDS((nc, NH, HD, HD), F32)] + [SDS(a.shape, a.dtype) for a in placed],
        scratch_shapes=[pltpu.VMEM((NH, HD, HD), F32)] + _gather_sems(nw),
        input_output_aliases={9 + i: 6 + i for i in range(nw)},
        compiler_params=_arb(), name="proj_hgrn_fwd")(x, nw1, ada, ada, w_in_b, lower_bounds, gn_w, w_st, masks, *placed)
    return res[:6], res[6:]


def _token_local(x, ycat, tgt, g1, nw2, sc2, sh2, g2, fw, w_out_b, w_fi_b, w_fo_b, proj, ws_b, bst, lnw, lnb, tm):
    T = x.shape[0]
    inv_d = 1.0 / D

    def body(x_ref, yb_ref, t_ref, g1_ref, nw2_ref, sc2_ref, sh2_ref, g2_ref, fw_ref, wo_ref, wfi_ref, wfo_ref,
             u_ref, v_ref, ws_ref, bst_ref, lnw_ref, lnb_ref,
             dy_ref, dx1_ref, h2_ref, act_ref, dff_ref, dgu_ref, dmix_ref, acc_ref, ya_ref):
        @pl.when(pl.program_id(0) == 0)
        def _():
            acc_ref[...] = jnp.zeros_like(acc_ref)

        def acc(row, val):
            acc_ref[row:row + 1, :] += jnp.sum(val, axis=0, keepdims=True)

        for bi in range(tm // BLK):
            rs = slice(bi * BLK, (bi + 1) * BLK)
            ug, _, _, _, _, _, mixed = _gmlp_common(u_ref[rs, :], v_ref[rs, :], lnw_ref[...], lnb_ref[...], ws_ref, bst_ref)
            ya_ref[rs, :] = (ug * mixed).astype(BF16)
        g1v, g2v = g1_ref[...], g2_ref[...]
        mix = _dot(ya_ref[...], wo_ref[0:DG, :]) + _dot(yb_ref[...], wo_ref[DG:D, :])
        x1 = x_ref[...] + g1v * mix
        r2 = _rms(x1)
        xh2 = x1 * r2
        n2 = xh2 * nw2_ref[...]
        osc2 = 1.0 + sc2_ref[...]
        h2b = (n2 * osc2 + sh2_ref[...]).astype(BF16)
        h2_ref[...] = h2b
        ff = jnp.zeros((tm, D), F32)
        saved = []
        for kb in range(DFF // FFB):
            gate = _dot(h2b, wfi_ref[:, kb * FFB:(kb + 1) * FFB])
            up = _dot(h2b, wfi_ref[:, DFF + kb * FFB:DFF + (kb + 1) * FFB])
            sg = _sigmoid(gate)
            actb = (gate * sg * up).astype(BF16)
            act_ref[:, kb * FFB:(kb + 1) * FFB] = actb
            ff = ff + _dot(actb, wfo_ref[kb * FFB:(kb + 1) * FFB, :])
            saved.append((gate, up, sg))
        x2 = x1 + g2v * ff
        r3 = _rms(x2)
        xh3 = x2 * r3
        err = xh3 * fw_ref[...] - t_ref[...]
        acc(6, (0.5 * inv_d) * err * err)
        dy = err * inv_d
        acc(4, dy * xh3)
        dx2 = _rms_bwd(xh3, r3, dy * fw_ref[...])
        acc(0, dx2 * ff)
        dffb = (dx2 * g2v).astype(BF16)
        dff_ref[...] = dffb
        dh2 = jnp.zeros((tm, D), F32)
        for kb in range(DFF // FFB):
            gate, up, sg = saved[kb]
            da = _dot(dffb, wfo_ref[kb * FFB:(kb + 1) * FFB, :], NT)
            dgate = (da * up * (sg * (1.0 + gate * (1.0 - sg)))).astype(BF16)
            dup = (da * gate * sg).astype(BF16)
            dgu_ref[:, kb * FFB:(kb + 1) * FFB] = dgate
            dgu_ref[:, DFF + kb * FFB:DFF + (kb + 1) * FFB] = dup
            dh2 = dh2 + _dot(dgate, wfi_ref[:, kb * FFB:(kb + 1) * FFB], NT)
            dh2 = dh2 + _dot(dup, wfi_ref[:, DFF + kb * FFB:DFF + (kb + 1) * FFB], NT)
        acc(2, dh2)
        acc(1, dh2 * n2)
        dn2 = dh2 * osc2
        acc(3, dn2 * xh2)
        dx1 = dx2 + _rms_bwd(xh2, r2, dn2 * nw2_ref[...])
        acc(5, dx1 * mix)
        dmixb = (dx1 * g1v).astype(BF16)
        dmix_ref[...] = dmixb
        dy_ref[...] = _dot(dmixb, wo_ref[...], NT)
        dx1_ref[...] = dx1

    row = lambda i: (i, 0)
    vec = _full((1, D))
    half = lambda j: pl.BlockSpec((tm, DG), lambda i: (i, j))
    return pl.pallas_call(
        body, grid=(T // tm,),
        in_specs=[pl.BlockSpec((tm, D), row), half(1), pl.BlockSpec((tm, D), row),
                  _ada_part(ADA_G1), vec, _ada_part(ADA_SC2), _ada_part(ADA_SH2), _ada_part(ADA_G2), vec,
                  _resident((D, D)), _resident((D, 2 * DFF)), _resident((DFF, D)),
                  half(0), half(1), _full((NH, BLK, BLK)), _full((BLK, NH)), _full((1, DG)), _full((1, DG))],
        out_specs=[pl.BlockSpec((tm, D), row), pl.BlockSpec((tm, D), row), pl.BlockSpec((tm, D), row),
                   pl.BlockSpec((tm, DFF), row), pl.BlockSpec((tm, D), row), pl.BlockSpec((tm, 2 * DFF), row),
                   pl.BlockSpec((tm, D), row), _full((8, D)), half(0)],
        out_shape=[SDS((T, D), F32), SDS((T, D), F32), SDS((T, D), BF16), SDS((T, DFF), BF16), SDS((T, D), BF16),
                   SDS((T, 2 * DFF), BF16), SDS((T, D), BF16), SDS((8, D), F32), SDS((T, D), BF16)],
        input_output_aliases={1: 8},
        compiler_params=_arb(), name="token_local")(x, ycat, tgt, g1, nw2, sc2, sh2, g2, fw, w_out_b, w_fi_b, w_fo_b,
                                                    proj, proj, ws_b, bst, lnw, lnb)


def _gmlp_bwd(proj, dycat, ws_b, bst, lnw, lnb, grads):
    T = proj.shape[0]
    rows = min(GMLP_ROWS_PER_STEP, T)
    nb = T // rows
    nw = len(grads)

    def body(*refs):
        u_ref, v_ref, dy_ref, ws_ref, bst_ref, lnw_ref, lnb_ref = refs[:7]
        dp_ref, dws_ref, dbs_ref, dln_ref = refs[7 + nw:11 + nw]
        dbs_acc, send_sems, recv_sems = refs[11 + 2 * nw:]
        exchange = _CoreExchange(refs[7:7 + nw], refs[11 + nw:11 + 2 * nw], send_sems, recv_sems)
        i = pl.program_id(0)

        @pl.when(i == 0)
        def _():
            exchange.start()
            dws_ref[...] = jnp.zeros_like(dws_ref)
            dln_ref[...] = jnp.zeros_like(dln_ref)
            dbs_acc[...] = jnp.zeros_like(dbs_acc)

        r = lax.broadcasted_iota(jnp.int32, (BLK, BLK), 0) // CH
        c = lax.broadcasted_iota(jnp.int32, (BLK, BLK), 1) // CH
        for bi in range(rows // BLK):
            rs = slice(bi * BLK, (bi + 1) * BLK)
            ug, dug, dvg, rstd, vhat, vnb, mixed = _gmlp_common(
                u_ref[rs, :], v_ref[rs, :], lnw_ref[...], lnb_ref[...], ws_ref, bst_ref)
            dya = dy_ref[rs, :]
            dp_ref[rs, 0:DG] = (dya * mixed * dug).astype(BF16)
            dmixed = dya * ug
            dbs_acc[...] += dmixed
            dmb = dmixed.astype(BF16)
            dvn = []
            for h in range(NH):
                sl = slice(h * HD, (h + 1) * HD)
                dws_ref[h * BLK:(h + 1) * BLK, :] += jnp.where(r >= c, _dot(dmb[:, sl], vnb[:, sl], NT), 0.0)
                dvn.append(_dot(ws_ref[h], dmb[:, sl], TN))
            dvn = jnp.concatenate(dvn, axis=1)
            dln_ref[0:1, :] += jnp.sum(dvn * vhat, axis=0, keepdims=True)
            dln_ref[1:2, :] += jnp.sum(dvn, axis=0, keepdims=True)
            dvh = dvn * lnw_ref[...]
            dvgel = rstd * (dvh - jnp.mean(dvh, axis=-1, keepdims=True) - vhat * jnp.mean(dvh * vhat, axis=-1, keepdims=True))
            dp_ref[rs, DG:2 * DG] = (dvgel * dvg).astype(BF16)

        @pl.when(i == nb - 1)
        def _():
            head = lax.broadcasted_iota(jnp.int32, (8, BLK), 0)
            ones = jnp.ones((8, HD), F32)
            out = jnp.zeros((8, BLK), F32)
            for h in range(NH):
                sums = _dot(ones, dbs_acc[:, h * HD:(h + 1) * HD], NT, precision=HIGHEST)
                out = out + jnp.where(head == h, sums, 0.0)
            dbs_ref[...] = out
            exchange.finish()

    anyspec = pl.BlockSpec(memory_space=pl.ANY)
    res = pl.pallas_call(
        body, grid=(nb,),
        in_specs=[pl.BlockSpec((rows, DG), lambda i: (i, 0)), pl.BlockSpec((rows, DG), lambda i: (i, 1)),
                  pl.BlockSpec((rows, DG), lambda i: (i, 0)),
                  _full((NH, BLK, BLK)), _full((BLK, NH)), _full((1, DG)), _full((1, DG))] + [anyspec] * nw,
        out_specs=[pl.BlockSpec((rows, 2 * DG), lambda i: (i, 2)), _full((NH * BLK, BLK)), _full((8, BLK)), _full((8, DG))]
        + [anyspec] * nw,
        out_shape=[SDS((T, DIN), BF16), SDS((NH * BLK, BLK), F32), SDS((8, BLK), F32), SDS((8, DG), F32)]
        + _core_exchange_shapes(grads),
        scratch_shapes=[pltpu.VMEM((BLK, DG), F32)] + _core_exchange_sems(nw),
        compiler_params=_arb(), name="gmlp_bwd")(proj, proj, dycat, ws_b, bst, lnw, lnb, *grads)
    return res[:4], res[4:]


def _hgrn_bwd(proj, o_pre, a_all, st_all, dycat, lower_bounds, gn_w, dproj, tables, sums, row_blocks):
    T = proj.shape[0]
    nc = T // CH
    nch = min(HGRN_CHUNKS_PER_STEP, nc)
    steps = nc // nch
    w_st, w_st_t, _, masks_sym = tables
    n_lev = len(LEVELS)
    nw, nr = len(sums), len(row_blocks)

    def body(*refs):
        q_ref, f_ref, i_ref, g_ref, o_ref, a_ref, st_ref, dy_ref, lbp_ref, gn_ref, w_ref, wt_ref, ms_ref = refs[:13]
        n_in = 14 + nw + nr
        dp_ref, dlb_ref, dgn_ref = refs[n_in:n_in + 3]
        ds_scr, dx_scr = refs[n_in + 3 + nw + nr:n_in + 5 + nw + nr]
        sems = refs[n_in + 5 + nw + nr:]
        exchange = _ChipExchange(refs[14:14 + nw], refs[n_in + 3:n_in + 3 + nw], *sems[:2])
        rows_gather = _RowGather(refs[14 + nw:n_in], refs[n_in + 3 + nw:n_in + 3 + nw + nr], *sems[2:])
        i = pl.program_id(0)

        @pl.when(i == 0)
        def _():
            rows_gather.start()
            exchange.start()
            ds_scr[...] = jnp.zeros_like(ds_scr)
            dlb_ref[...] = jnp.zeros_like(dlb_ref)
            dgn_ref[...] = jnp.zeros_like(dgn_ref)

        @pl.when(i == steps // 2)
        def _():
            rows_gather.forward()

        lb, omlb = _lower_bound(lbp_ref)
        row = lax.broadcasted_iota(jnp.int32, (CH, 1), 0)
        eye = lax.broadcasted_iota(jnp.int32, (CH, CH), 0) == lax.broadcasted_iota(jnp.int32, (CH, CH), 1)
        lower = lax.broadcasted_iota(jnp.int32, (CH, CH), 0) > lax.broadcasted_iota(jnp.int32, (CH, CH), 1)
        dgn = jnp.zeros((1, HD), F32)
        pre = []
        for ci in range(nch):
            rs = slice(ci * CH, (ci + 1) * CH)
            q = q_ref[rs, :]
            v = i_ref[rs, :]
            g = g_ref[rs, :]
            sq, qf, sig, f, k, e = _hgrn_gates(q, f_ref[rs, :], lb, omlb, w_ref)
            eb = e[0]
            ekd = e[1]
            kd = k * ekd
            qe = qf * eb
            dob_h, dqe_h, dqf_h, dki_h, dv_h, dg_h = [], [], [], [], [], []
            for h in range(NH):
                sl = slice(h * HD, (h + 1) * HD)
                o = o_ref[rs, sl]
                ro = _rms(o)
                oh = o * ro
                gh = g[:, sl]
                sg = _sigmoid(gh)
                dyb = dy_ref[rs, sl]
                dg_h.append(dyb * (oh * gn_ref[...]) * (sg * (1.0 + gh * (1.0 - sg))))
                don = dyb * (gh * sg)
                dgn = dgn + jnp.sum(don * oh, axis=0, keepdims=True)
                dob = _rms_bwd(oh, ro, don * gn_ref[...]).astype(BF16)
                vb = v[:, sl].astype(BF16)
                qh, kh = qf[:, sl], k[:, sl]
                dqe = _dot(dob, st_ref[ci, h].astype(BF16))
                da = _dot(dob, vb, NT)
                ddiag = jnp.sum(jnp.where(eye, da, 0.0), axis=-1, keepdims=True)
                dsym = jnp.where(lower, da, _dot(vb, dob, NT))
                upper_part = jnp.zeros((CH, HD), F32)
                both = jnp.zeros((CH, HD), F32)
                for li in range(n_lev):
                    el, up, y = _level_factor(e, li, sl, row, qh, kh)
                    dyv = _dot((ms_ref[li] * dsym).astype(BF16), y.astype(BF16))
                    dx_scr[ci, (2 + li) * CH:(3 + li) * CH, sl] = dyv * y
                    dye = dyv * el
                    upper_part = upper_part + jnp.where(up, dye, 0.0)
                    both = both + dye
                dob_h.append(dob)
                dqe_h.append(dqe)
                dqf_h.append(dqe * eb[:, sl] + ddiag * kh + upper_part)
                dki_h.append(ddiag * qh + (both - upper_part))
                dv_h.append(_dot(a_ref[ci, h].astype(BF16), dob, TN))
            dp_ref[rs, 0:DH] = (jnp.concatenate(dqf_h, axis=1) * (sq * (1.0 + q * (1.0 - sq)))).astype(BF16)
            dp_ref[rs, 3 * DH:4 * DH] = jnp.concatenate(dg_h, axis=1).astype(BF16)
            pre.append((v, sig, f, eb, ekd, kd, qe, dob_h, jnp.concatenate(dqe_h, axis=1), dki_h, dv_h))
        dgn_ref[0:1, :] += dgn
        for ci in reversed(range(nch)):
            rs = slice(ci * CH, (ci + 1) * CH)
            v, sig, f, eb, ekd, kd, qe, dob_h, dqe, dki_h, dv_h = pre[ci]
            ebl = eb[CH - 1:CH, :]
            dbl_h, dkd_h, dv2_h = [], [], []
            for h in range(NH):
                sl = slice(h * HD, (h + 1) * HD)
                dst1 = ds_scr[h]
                dst1b = dst1.astype(BF16)
                ds_scr[h] = dst1 * ebl[:, sl] + _dot(dob_h[h], qe[:, sl].astype(BF16), TN)
                dbl_h.append(ebl[:, sl] * jnp.sum(st_ref[ci, h] * dst1, axis=0, keepdims=True))
                dkd_h.append(_dot(v[:, sl].astype(BF16), dst1b))
                dv2_h.append(dv_h[h] + _dot(kd[:, sl].astype(BF16), dst1b, NT))
            dkd = jnp.concatenate(dkd_h, axis=1)
            dx_scr[ci, 0:CH, :] = dqe * qe + jnp.where(row == CH - 1, jnp.concatenate(dbl_h, axis=1), 0.0)
            dx_scr[ci, CH:2 * CH, :] = dkd * kd
            dlf = _split_dot(wt_ref[...], dx_scr[ci], 2)
            df = dlf / f - (dkd * ekd + jnp.concatenate(dki_h, axis=1))
            dlb_ref[0:1, :] += jnp.sum(df * (1.0 - sig), axis=0, keepdims=True)
            dp_ref[rs, DH:2 * DH] = (df * omlb * sig * (1.0 - sig)).astype(BF16)
            dp_ref[rs, 2 * DH:3 * DH] = jnp.concatenate(dv2_h, axis=1).astype(BF16)

        @pl.when(i == steps - 1)
        def _():
            gl = dlb_ref[0:1, :] * lb * omlb
            dlb_ref[0:1, :] = gl
            dlb_ref[1:2, :] = -gl
            exchange.finish()
            rows_gather.finish()

    rev = lambda j: pl.BlockSpec((nch * CH, DH), lambda c: (steps - 1 - c, j))
    anyspec = pl.BlockSpec(memory_space=pl.ANY)
    res = pl.pallas_call(
        body, grid=(steps,),
        in_specs=[rev(2), rev(3), rev(4), rev(5), rev(0),
                  pl.BlockSpec((nch, NH, CH, CH), lambda c: (steps - 1 - c, 0, 0, 0)),
                  pl.BlockSpec((nch, NH, HD, HD), lambda c: (steps - 1 - c, 0, 0, 0)),
                  rev(1), _full((2, DH)), _full((1, HD)),
                  _full(w_st.shape), _full(w_st_t.shape), _full(masks_sym.shape),
                  anyspec] + [anyspec] * (nw + nr),
        out_specs=[pl.BlockSpec((nch * CH, 4 * DH), lambda c: (steps - 1 - c, 0)), _full((8, DH)), _full((8, HD))]
        + [anyspec] * (nw + nr),
        out_shape=[SDS((T, DIN), BF16), SDS((8, DH), F32), SDS((8, HD), F32)] + _slot_shapes(sums)
        + _gather_rows_shapes(row_blocks),
        scratch_shapes=[pltpu.VMEM((NH, HD, HD), F32), pltpu.VMEM((nch, (2 + n_lev) * CH, DH), F32)]
        + _exchange_sems(nw) + _gather_rows_sems(nr),
        input_output_aliases={13: 0},
        compiler_params=_arb(), name="hgrn_bwd")(proj, proj, proj, proj, o_pre, a_all, st_all, dycat, lower_bounds, gn_w,
                                                 w_st, w_st_t, masks_sym, dproj, *sums, *row_blocks)
    return res[:3], res[3:3 + nw], res[3 + nw:]


def _proj_in_bwd(dproj, x, dx1, nw, sc, w_in_b, tm, sums):
    T = x.shape[0]
    ns = len(sums)
    steps = T // tm

    def body(*refs):
        dp_ref, x_ref, dx1_ref, nw_ref, sc_ref, w_ref = refs[:6]
        gx_ref, acc_ref = refs[6 + ns:8 + ns]
        exchange = _ChipExchange(refs[6:6 + ns], refs[8 + ns:8 + 2 * ns], *refs[8 + 2 * ns:])

        @pl.when(pl.program_id(0) == 0)
        def _():
            exchange.start()
            acc_ref[...] = jnp.zeros_like(acc_ref)

        dh = _dot(dp_ref[:, 0:4 * DH], w_ref[:, 2 * DG:DIN], NT) + _dot(dp_ref[:, 4 * DH:DIN], w_ref[:, 0:2 * DG], NT)
        xv = x_ref[...]
        r = _rms(xv)
        xh = xv * r
        n1 = xh * nw_ref[...]
        acc_ref[0:1, :] += jnp.sum(dh, axis=0, keepdims=True)
        acc_ref[1:2, :] += jnp.sum(dh * n1, axis=0, keepdims=True)
        dn = dh * (1.0 + sc_ref[...])
        acc_ref[2:3, :] += jnp.sum(dn * xh, axis=0, keepdims=True)
        gx_ref[...] = dx1_ref[...] + _rms_bwd(xh, r, dn * nw_ref[...])

        @pl.when(pl.program_id(0) == steps - 1)
        def _():
            exchange.finish()

    row = lambda i: (i, 0)
    anyspec = pl.BlockSpec(memory_space=pl.ANY)
    res = pl.pallas_call(
        body, grid=(steps,),
        in_specs=[pl.BlockSpec((tm, DIN), row), pl.BlockSpec((tm, D), row), pl.BlockSpec((tm, D), row),
                  _full((1, D)), _ada_part(ADA_SC1), _resident((D, DIN))] + [anyspec] * ns,
        out_specs=[pl.BlockSpec((tm, D), row), _full((8, D))] + [anyspec] * ns,
        out_shape=[SDS((T, D), F32), SDS((8, D), F32)] + _slot_shapes(sums),
        scratch_shapes=_exchange_sems(ns),
        compiler_params=_arb(), name="proj_in_bwd")(dproj, x, dx1, nw, sc, w_in_b, *sums)
    return res[:2], res[2:]


def _wgrad(a, b, bk, bn, tt, name, bf16_copy=False):
    T, K = a.shape
    N = b.shape[1]
    nn, nk, nt = N // bn, K // bk, T // tt
    bmap = lambda n, k, t: (t, n)

    def body(a_ref, b_ref, o_ref, *copy_ref):
        @pl.when(pl.program_id(2) == 0)
        def _():
            o_ref[...] = jnp.zeros_like(o_ref)

        o_ref[0] += _dot(a_ref[...], b_ref[...], TN)

        if bf16_copy:
            @pl.when(pl.program_id(2) == nt - 1)
            def _():
                copy_ref[0][...] = o_ref[...].astype(BF16)

    ospec = pl.BlockSpec((1, bk, bn), lambda n, k, t: (n, k, 0))
    return pl.pallas_call(
        body, grid=(nn, nk, nt),
        in_specs=[pl.BlockSpec((tt, bk), lambda n, k, t: (t, k)), pl.BlockSpec((tt, bn), bmap)],
        out_specs=[ospec, ospec] if bf16_copy else ospec,
        out_shape=[SDS((nn, K, bn), F32), SDS((nn, K, bn), BF16)] if bf16_copy else SDS((nn, K, bn), F32),
        compiler_params=_arb(3), name=name)(a, b)


def _adam_math(w, g, m, v):
    m = B1 * m + (1.0 - B1) * g
    v = B2 * v + (1.0 - B2) * (g * g)
    m_hat = m / (1.0 - B1 ** STEP)
    v_hat = v / (1.0 - B2 ** STEP)
    return -LR * (m_hat / (jnp.sqrt(v_hat) + AEPS) + WD * w), m, v


def _adamw_halves(w, mine, sibling, m, v, c_idx, rb, name):
    R, C = w.shape
    nb = (R // 2) // rb

    def body(c_ref, w_ref, a_ref, b_ref, m_ref, v_ref, g_out, d_out, m_out, v_out):
        g = jnp.where(pl.program_id(0) == c_ref[0], a_ref[...], b_ref[...])
        g_out[...] = g
        d_out[...], m_out[...], v_out[...] = _adam_math(w_ref[...], g, m_ref[...], v_ref[...])

    whole = pl.BlockSpec((rb, C), lambda hh, i, cr: (hh * nb + i, 0))
    half = pl.BlockSpec((rb, C), lambda hh, i, cr: (i, 0))
    return pl.pallas_call(
        body,
        grid_spec=pltpu.PrefetchScalarGridSpec(
            num_scalar_prefetch=1, grid=(2, nb), in_specs=[whole, half, half, whole, whole], out_specs=[whole] * 4),
        out_shape=[SDS((R, C), F32)] * 4, compiler_params=_arb(2), name=name)(c_idx, w, mine, sibling, m, v)


def _ada_wgrad_adam(cact_t, dada_all, w, m, v, chip_idx):
    R, C = w.shape
    rb = 256

    def body(j_ref, c_ref, d_ref, w_ref, m_ref, v_ref, g_out, d_out, m_out, v_out):
        g = _dot(c_ref[...], d_ref[...], precision=HIGHEST)
        g_out[...] = g
        d_out[...], m_out[...], v_out[...] = _adam_math(w_ref[...], g, m_ref[...], v_ref[...])

    spec = pl.BlockSpec((rb, C), lambda i, j: (i, 0))
    return pl.pallas_call(
        body,
        grid_spec=pltpu.PrefetchScalarGridSpec(
            num_scalar_prefetch=1, grid=(R // rb,),
            in_specs=[pl.BlockSpec((rb, N_DEV), lambda i, j: (i, 0)), pl.BlockSpec((N_DEV, C), lambda i, j: (0, j[0])),
                      spec, spec, spec],
            out_specs=[spec] * 4),
        out_shape=[SDS((R, C), F32)] * 4,
        compiler_params=_arb(), name="ada_wgrad_adam")(chip_idx, cact_t, dada_all, w, m, v)


SMALL_NAMES = ('b_ada', 'norm1_w', 'norm2_w', 'final_norm_w', 'v_ln_w', 'v_ln_b', 'lower_bounds', 'gn_w', 'b_s', 'w_s')


def _small_finalize(gathered, params, moms, vels):
    n_in = len(gathered)

    def body(*refs):
        acc1, acc2, dln, dlb, dgn, dbs, dws = refs[:n_in]
        prm = [dict(zip(SMALL_NAMES, refs[n_in + k * 10:n_in + (k + 1) * 10])) for k in range(3)]
        outs = [dict(zip(SMALL_NAMES, refs[n_in + 30 + k * 10:n_in + 30 + (k + 1) * 10])) for k in range(4)]
        loss_ref, dada_ref = refs[n_in + 70:n_in + 72]

        def dev_sum(ref, first, n):
            per = ref.shape[0] // N_DEV
            g = ref[first:first + n, :]
            for dev in range(1, N_DEV):
                g = g + ref[dev * per + first:dev * per + first + n, :]
            return g

        def update(n, g, cols=slice(None)):
            outs[0][n][:, cols] = g
            outs[1][n][:, cols], outs[2][n][:, cols], outs[3][n][:, cols] = _adam_math(
                prm[0][n][:, cols], g, prm[1][n][:, cols], prm[2][n][:, cols])

        ada_rows = ((acc1, 0), (acc1, 1), (acc2, 5), (acc2, 2), (acc2, 1), (acc2, 0))
        for k, (ref, r) in enumerate(ada_rows):
            update('b_ada', dev_sum(ref, r, 1), slice(k * D, (k + 1) * D))
            for dev in range(N_DEV):
                dada_ref[dev:dev + 1, k * D:(k + 1) * D] = ref[8 * dev + r:8 * dev + r + 1, :]
        update('norm1_w', dev_sum(acc1, 2, 1))
        update('norm2_w', dev_sum(acc2, 3, 1))
        update('final_norm_w', dev_sum(acc2, 4, 1))
        update('v_ln_w', dev_sum(dln, 0, 1))
        update('v_ln_b', dev_sum(dln, 1, 1))
        update('lower_bounds', dev_sum(dlb, 0, 2))
        update('gn_w', dev_sum(dgn, 0, 1))
        update('b_s', dev_sum(dbs, 0, NH))
        update('w_s', dev_sum(dws, 0, NH * BLK))
        loss_ref[...] = jnp.sum(dev_sum(acc2, 6, 1), axis=-1, keepdims=True)

    shapes = [SDS(params[n].shape, F32) for n in SMALL_NAMES]
    res = pl.pallas_call(
        body, out_shape=shapes * 4 + [SDS((1, 1), F32), SDS((N_DEV, 6 * D), F32)], name="small_finalize")(
            *gathered, *[d[n] for d in (params, moms, vels) for n in SMALL_NAMES])
    return [dict(zip(SMALL_NAMES, res[k * 10:(k + 1) * 10])) for k in range(4)], res[40], res[41]


def _position():
    x, y, c = lax.axis_index("x"), lax.axis_index("y"), lax.axis_index("c")
    return x, y, c


def _chip_at(x, y, r):
    return (x ^ (r >> 1), y ^ (r & 1))


class _RowGather:
    def __init__(self, ins, outs, send_sems, recv_sems, local_sems):
        self.ins, self.outs = ins, outs
        self.send_sems, self.recv_sems, self.local_sems = send_sems, recv_sems, local_sems
        self.x, self.y, self.c = _position()
        self.me, self.sibling = (self.x, self.y, self.c), (self.x, self.y, 1 - self.c)
        self.chips = [_chip_at(self.x, self.y, r) for r in (1, 2, 3)]

    def _rows(self, b, px, py, pc):
        m_per = self.ins[b].shape[0]
        return self.outs[b].at[pl.ds((4 * px + 2 * py + pc) * m_per, m_per), :]

    def _copy(self, b, k, blk, to, from_input=False):
        return pltpu.make_async_remote_copy(
            src_ref=self.ins[b] if from_input else self._rows(b, *blk), dst_ref=self._rows(b, *blk),
            send_sem=self.send_sems.at[7 * b + k], recv_sem=self.recv_sems.at[7 * b + k],
            device_id=to, device_id_type=MESH)

    def _local(self, b):
        return pltpu.make_async_copy(self.ins[b], self._rows(b, *self.me), self.local_sems.at[b])

    def _first(self, b):
        c = self.c
        return [self._copy(b, 0, self.me, self.sibling, from_input=True)] + [
            self._copy(b, 1 + j, self.me, (*chip, c), from_input=True) for j, chip in enumerate(self.chips)]

    def start(self):
        for b in range(len(self.ins)):
            self._local(b).start()
            for cp in self._first(b):
                cp.start()

    def forward(self):
        for b in range(len(self.ins)):
            for j, chip in enumerate(self.chips):
                self._copy(b, 1 + j, (*chip, self.c), self.me).wait_recv()
                self._copy(b, 4 + j, (*chip, self.c), self.sibling).start()

    def finish(self):
        for b in range(len(self.ins)):
            self._copy(b, 0, self.sibling, self.me).wait_recv()
            for j, chip in enumerate(self.chips):
                self._copy(b, 4 + j, (*chip, 1 - self.c), self.me).wait_recv()
        for b in range(len(self.ins)):
            for cp in self._first(b):
                cp.wait_send()
            for j, chip in enumerate(self.chips):
                self._copy(b, 4 + j, (*chip, self.c), self.sibling).wait_send()
            self._local(b).wait()


def _gather_rows(ins, outs, send_sems, recv_sems, local_sems, after_issue=None):
    g = _RowGather(ins, outs, send_sems, recv_sems, local_sems)
    g.start()
    if after_issue is not None:
        after_issue()
    g.forward()
    g.finish()


def _gather_rows_shapes(blocks):
    return [SDS((N_DEV * b.shape[0], b.shape[1]), b.dtype) for b in blocks]


def _gather_rows_sems(nb):
    return [pltpu.SemaphoreType.DMA((7 * nb,)), pltpu.SemaphoreType.DMA((7 * nb,)), pltpu.SemaphoreType.DMA((nb,))]


def _place_shard(w_shard, axis, chip_idx, name):
    R, C = w_shard.shape
    rb = _row_block(R)
    nb = R // rb
    full = (R * N_CHIPS, C) if axis == 0 else (R, C * N_CHIPS)
    omap = (lambda i, j: (j[0] * nb + i, 0)) if axis == 0 else (lambda i, j: (i, j[0]))

    def body(j_ref, w_ref, o_ref):
        o_ref[...] = w_ref[...].astype(BF16)

    return pl.pallas_call(
        body,
        grid_spec=pltpu.PrefetchScalarGridSpec(
            num_scalar_prefetch=1, grid=(nb,), in_specs=[pl.BlockSpec((rb, C), lambda i, j: (i, 0))],
            out_specs=pl.BlockSpec((rb, C), omap)),
        out_shape=SDS(full, BF16), compiler_params=_arb(), name=name)(chip_idx, w_shard)


class _WeightGather:
    def __init__(self, refs, axes, send_sems, recv_sems):
        self.refs, self.axes, self.send_sems, self.recv_sems = refs, axes, send_sems, recv_sems
        self.x, self.y, self.c = _position()
        self.j = 2 * self.x + self.y
        self.n = 3 * len(refs)

    def _half(self, w, chip_idx, half):
        ref, axis = self.refs[w], self.axes[w]
        if axis == 0:
            size = ref.shape[0] // N_CHIPS
            return ref.at[pl.ds(chip_idx * size + half * (size // 2), size // 2), :]
        size = ref.shape[1] // N_CHIPS
        rows = ref.shape[0] // 2
        return ref.at[pl.ds(half * rows, rows), pl.ds(chip_idx * size, size)]

    def _ici(self, w, r, chip_idx):
        k = 3 * w + r - 1
        piece = self._half(w, chip_idx, self.c)
        return pltpu.make_async_remote_copy(
            src_ref=piece, dst_ref=piece, send_sem=self.send_sems.at[k], recv_sem=self.recv_sems.at[k],
            device_id=(*_chip_at(self.x, self.y, r), self.c), device_id_type=MESH)

    def _d2d(self, w, r, half):
        k = self.n + 3 * w + r - 1
        piece = self._half(w, self.j ^ r, half)
        return pltpu.make_async_remote_copy(
            src_ref=piece, dst_ref=piece, send_sem=self.send_sems.at[k], recv_sem=self.recv_sems.at[k],
            device_id=(self.x, self.y, 1 - self.c), device_id_type=MESH)

    def _each(self):
        return [(w, r) for w in range(len(self.refs)) for r in (1, 2, 3)]

    def start(self):
        for w, r in self._each():
            self._ici(w, r, self.j).start()

    def forward(self):
        for w, r in self._each():
            self._ici(w, r, self.j ^ r).wait_recv()
            self._d2d(w, r, self.c).start()

    def finish(self):
        for w, r in self._each():
            self._ici(w, r, self.j).wait_send()
            self._d2d(w, r, self.c).wait_send()
            self._d2d(w, r, 1 - self.c).wait_recv()


def _gather_sems(n_weights):
    return [pltpu.SemaphoreType.DMA((6 * n_weights,)), pltpu.SemaphoreType.DMA((6 * n_weights,))]


def _gather_w_in_and_ada(placed, axis, c_block, w_ada):
    n = w_ada.shape[1]

    def body(w_any, c_ref, wada_ref, w_out, call_ref, cact_ref, pall_ref, p_scr, *sems):
        g = _WeightGather([w_out], [axis], *sems[:2])
        _gather_rows([c_ref], [call_ref], *sems[2:5], after_issue=g.start)
        pick = (lax.broadcasted_iota(jnp.int32, (N_DEV, N_DEV * 8), 1)
                == 8 * lax.broadcasted_iota(jnp.int32, (N_DEV, N_DEV * 8), 0)).astype(F32)
        cv = _dot(pick, call_ref[...], precision=HIGHEST)
        ca = cv * _sigmoid(cv)
        cact_ref[...] = ca
        p_scr[...] = _dot(ca, wada_ref[...], precision=HIGHEST)
        products = _RowGather([p_scr], [pall_ref], *sems[5:])
        products.start()
        g.forward()
        products.forward()
        products.finish()
        g.finish()

    anyspec = pl.BlockSpec(memory_space=pl.ANY)
    vmem = pl.BlockSpec(memory_space=pltpu.VMEM)
    rows = N_DEV * c_block.shape[0]
    res = pl.pallas_call(
        body, out_shape=[SDS(placed.shape, placed.dtype), SDS((rows, D), F32), SDS((N_DEV, D), F32), SDS((rows, n), F32)],
        in_specs=[anyspec, vmem, vmem], out_specs=[anyspec, vmem, vmem, vmem],
        scratch_shapes=[pltpu.VMEM((N_DEV, n), F32)] + _gather_sems(1) + _gather_rows_sems(1) + _gather_rows_sems(1),
        input_output_aliases={0: 0}, name="gather_w_in_and_ada")(placed, c_block, w_ada)
    return res[0], res[2], res[3]


class _ChipExchange:
    def __init__(self, ins, outs, send_sems, recv_sems):
        self.ins, self.outs, self.send_sems, self.recv_sems = ins, outs, send_sems, recv_sems
        self.x, self.y, self.c = _position()
        self.j = 2 * self.x + self.y

    def _copies(self):
        for w in range(len(self.ins)):
            for r in (1, 2, 3):
                k = 3 * w + r - 1
                yield pltpu.make_async_remote_copy(
                    src_ref=self.ins[w].at[self.j ^ r], dst_ref=self.outs[w].at[r - 1],
                    send_sem=self.send_sems.at[k], recv_sem=self.recv_sems.at[k],
                    device_id=(*_chip_at(self.x, self.y, r), self.c), device_id_type=MESH)

    def start(self):
        for cp in self._copies():
            cp.start()

    def finish(self):
        for cp in self._copies():
            cp.wait()


def _exchange_sems(n_weights):
    return [pltpu.SemaphoreType.DMA((3 * n_weights,)), pltpu.SemaphoreType.DMA((3 * n_weights,))]


class _CoreExchange:
    def __init__(self, ins, outs, send_sems, recv_sems):
        self.ins, self.outs, self.send_sems, self.recv_sems = ins, outs, send_sems, recv_sems
        self.x, self.y, self.c = _position()

    def _copies(self):
        for w in range(len(self.ins)):
            yield pltpu.make_async_remote_copy(
                src_ref=self.ins[w].at[:, 1 - self.c], dst_ref=self.outs[w],
                send_sem=self.send_sems.at[w], recv_sem=self.recv_sems.at[w],
                device_id=(self.x, self.y, 1 - self.c), device_id_type=MESH)

    def start(self):
        for cp in self._copies():
            cp.start()

    def finish(self):
        for cp in self._copies():
            cp.wait()


def _core_exchange_shapes(grads):
    return [SDS((g.shape[0], g.shape[2], g.shape[3]), g.dtype) for g in grads]


def _core_exchange_sems(n):
    return [pltpu.SemaphoreType.DMA((n,)), pltpu.SemaphoreType.DMA((n,))]


def _exchange_core_halves(grads, name):
    nw = len(grads)

    def body(*refs):
        ex = _CoreExchange(refs[:nw], refs[nw:2 * nw], *refs[2 * nw:])
        ex.start()
        ex.finish()

    anyspec = pl.BlockSpec(memory_space=pl.ANY)
    return pl.pallas_call(
        body, out_shape=_core_exchange_shapes(grads), in_specs=[anyspec] * nw, out_specs=[anyspec] * nw,
        scratch_shapes=_core_exchange_sems(nw), name=name)(*grads)


def _add_core_halves(g4, recv, c_idx, rb, name):
    ns, _, rh, C = g4.shape

    def body(c_ref, g_ref, r_ref, o_ref):
        o_ref[...] = (g_ref[0] + r_ref[...]).astype(BF16)

    return pl.pallas_call(
        body,
        grid_spec=pltpu.PrefetchScalarGridSpec(
            num_scalar_prefetch=1, grid=(ns, rh // rb),
            in_specs=[pl.BlockSpec((1, 1, rb, C), lambda s, i, cr: (s, cr[0], i, 0)),
                      pl.BlockSpec((1, rb, C), lambda s, i, cr: (s, i, 0))],
            out_specs=pl.BlockSpec((1, rb, C), lambda s, i, cr: (s, i, 0))),
        out_shape=SDS((ns, rh, C), BF16), compiler_params=_arb(2), name=name)(c_idx, g4, recv)


def _add_core_halves_in(g4, recv, c_idx, name):
    n_slabs, _, rh, C = g4.shape
    cb = 256
    per_slab, per_chip, n_blocks = C // cb, DIN // N_CHIPS // cb, DIN // cb

    def stored(s, k):
        sb = (per_chip * s + k + 4 * DH // cb) % n_blocks
        return sb // per_slab, sb % per_slab

    def body(c_ref, g_ref, r_ref, o_ref):
        o_ref[...] = (g_ref[0] + r_ref[...].astype(F32)).astype(BF16)

    return pl.pallas_call(
        body,
        grid_spec=pltpu.PrefetchScalarGridSpec(
            num_scalar_prefetch=1, grid=(N_CHIPS, per_chip),
            in_specs=[pl.BlockSpec((1, 1, rh, cb), lambda s, k, cr: (stored(s, k)[0], cr[0], 0, stored(s, k)[1])),
                      pl.BlockSpec((1, rh, cb), lambda s, k, cr: (stored(s, k)[0], 0, stored(s, k)[1]))],
            out_specs=pl.BlockSpec((1, rh, cb), lambda s, k, cr: (s, 0, k))),
        out_shape=SDS((N_CHIPS, rh, DIN // N_CHIPS), BF16), compiler_params=_arb(2), name=name)(c_idx, g4, recv)


def _slot_shapes(sums):
    return [SDS((3,) + s.shape[1:], s.dtype) for s in sums]


def _add_chips(own, slots, order, rb, name):
    _, rh, C = slots.shape

    def body(o_ref, own_ref, a_ref, b_ref, c_ref, d_ref, out_ref):
        mine = own_ref[0].astype(F32)
        t = [jnp.where(o_ref[i] == 0, mine, r[0].astype(F32)) for i, r in enumerate((a_ref, b_ref, c_ref, d_ref))]
        out_ref[...] = ((t[0] + t[1]) + t[2]) + t[3]

    def spec(i):
        return pl.BlockSpec((1, rb, C), lambda t, o: (jnp.maximum(o[i], 1) - 1, t, 0))

    return pl.pallas_call(
        body,
        grid_spec=pltpu.PrefetchScalarGridSpec(
            num_scalar_prefetch=1, grid=(rh // rb,),
            in_specs=[pl.BlockSpec((1, rb, C), lambda t, o: (o[4], t, 0)), spec(0), spec(1), spec(2), spec(3)],
            out_specs=pl.BlockSpec((rb, C), lambda t, o: (t, 0))),
        out_shape=SDS((rh, C), F32), compiler_params=_arb(), name=name)(order, own, slots, slots, slots, slots)


def _share_halves_and_gather(halves, row_blocks):
    nw, nr = len(halves), len(row_blocks)

    def body(*refs):
        ins, outs = refs[:nw], refs[nw + nr:2 * nw + nr]
        sems = refs[2 * (nw + nr):]
        send_sems, recv_sems = sems[:2]
        rows_gather = _RowGather(refs[nw:nw + nr], refs[2 * nw + nr:2 * (nw + nr)], *sems[2:])
        x, y, c = _position()
        rows_gather.start()
        started = []
        for w in range(nw):
            cp = pltpu.make_async_remote_copy(
                src_ref=ins[w], dst_ref=outs[w], send_sem=send_sems.at[w], recv_sem=recv_sems.at[w],
                device_id=(x, y, 1 - c), device_id_type=MESH)
            cp.start()
            started.append(cp)
        rows_gather.forward()
        rows_gather.finish()
        for cp in started:
            cp.wait()

    anyspec = pl.BlockSpec(memory_space=pl.ANY)
    vmem = pl.BlockSpec(memory_space=pltpu.VMEM)
    res = pl.pallas_call(
        body, out_shape=[SDS(h.shape, F32) for h in halves] + _gather_rows_shapes(row_blocks),
        in_specs=[anyspec] * nw + [vmem] * nr, out_specs=[anyspec] * nw + [vmem] * nr,
        scratch_shapes=[pltpu.SemaphoreType.DMA((nw,)), pltpu.SemaphoreType.DMA((nw,))] + _gather_rows_sems(nr),
        name="share_halves_and_gather")(*halves, *row_blocks)
    return res[:nw], res[nw:]


def _small_2d(b_ada, norm1_w, norm2_w, final_norm_w, v_ln_w, v_ln_b, lower_bounds, gn_w, b_s, w_s):
    return dict(zip(SMALL_NAMES, (b_ada, norm1_w, norm2_w, final_norm_w.reshape(1, D), v_ln_w, v_ln_b, lower_bounds, gn_w,
                                  b_s.reshape(NH, BLK), w_s.reshape(NH * BLK, BLK))))


def _small_original_shapes(d):
    out = dict(d)
    out['final_norm_w'] = d['final_norm_w'].reshape(D)
    out['b_s'] = d['b_s'].reshape(1, NH, BLK)
    out['w_s'] = d['w_s'].reshape(1, NH, BLK, BLK)
    return out


def _row_block(r):
    for cand in (256, 176, 128, 64, 32, 16, 8):
        if r % cand == 0:
            return cand
    return r


def kernel(x, c, w_ada, b_ada, norm1_w, w_in, w_s, b_s, v_ln_w, v_ln_b, lower_bounds, gn_w, w_out, norm2_w, w_ffn_in, w_ffn_out, final_norm_w, loss_target, m_w_ada, m_b_ada, m_norm1_w, m_w_in, m_w_s, m_b_s, m_v_ln_w, m_v_ln_b, m_lower_bounds, m_gn_w, m_w_out, m_norm2_w, m_w_ffn_in, m_w_ffn_out, m_final_norm_w, v_w_ada, v_b_ada, v_norm1_w, v_w_in, v_w_s, v_b_s, v_v_ln_w, v_v_ln_b, v_lower_bounds, v_gn_w, v_w_out, v_norm2_w, v_w_ffn_in, v_w_ffn_out, v_final_norm_w):
    T = x.shape[1]
    tm, tp = min(TOKEN_TILE, T), min(PROJ_TILE, T)
    px, py, pc = _position()
    chip = 2 * px + py
    me = 4 * px + 2 * py + pc
    x2d = x.reshape(T, D)
    tgt = loss_target.reshape(T, D)

    chip_idx = jnp.reshape(chip, (1,)).astype(jnp.int32)
    c_idx = jnp.reshape(pc, (1,)).astype(jnp.int32)
    w_in_b, cact, ada_all = _gather_w_in_and_ada(
        _place_shard(w_in[0], 1, chip_idx, "place_in"), 1, jnp.broadcast_to(c, (8, D)), w_ada[0])
    placed = [_place_shard(w_out[0], 0, chip_idx, "place_out"), _place_shard(w_ffn_in[0], 1, chip_idx, "place_ffn_in"),
              _place_shard(w_ffn_out[0], 0, chip_idx, "place_ffn_out")]

    n_ada = ada_all.shape[1]
    ada_all = ada_all.reshape(N_CHIPS, 2, N_DEV, n_ada)[:, 0]
    ada = lax.dynamic_index_in_dim(ada_all, me, axis=1, keepdims=False).reshape(1, 6 * D) + b_ada

    rr = lax.broadcasted_iota(jnp.int32, (BLK, BLK), 0) // CH
    cc = lax.broadcasted_iota(jnp.int32, (BLK, BLK), 1) // CH
    ws_b = jnp.where((rr >= cc)[None], w_s[0], 0.0).astype(BF16)
    bst = b_s[0].T
    lnw, lnb = v_ln_w, v_ln_b
    nw1, nw2, fw = norm1_w, norm2_w, final_norm_w.reshape(1, D)

    tables = _hgrn_tables()
    (h1, proj, ycat, o_pre, a_all, st_all), (w_out_b, w_fi_b, w_fo_b) = _proj_hgrn_fwd(
        x2d, nw1, ada, w_in_b, lower_bounds, gn_w, tables, placed, [0, 1, 0])

    dycat, dx1, h2, act, dff, dgu, dmix, acc2, ycat = _token_local(
        x2d, ycat, tgt, ada, nw2, ada, ada, ada, fw, w_out_b, w_fi_b, w_fo_b, proj, ws_b, bst, lnw, lnb, tm)

    tt, tt_sq = min(WGRAD_TOKENS, T), min(WGRAD_TOKENS_SQUARE, T)
    order = jnp.concatenate([chip ^ jnp.arange(N_CHIPS, dtype=jnp.int32), chip_idx]).astype(jnp.int32)

    def by_core_half(g):
        return g.reshape(g.shape[0], 2, g.shape[1] // 2, g.shape[2])

    def core_sums(g4, recv, names):
        return [_add_core_halves(a, b, c_idx, _row_block(a.shape[2]), "add_core_" + n) for a, b, n in zip(g4, recv, names)]

    def chip_sums(sums, slots, names):
        return [_add_chips(o, s, order, _row_block(s.shape[1]), "add_chips_" + n) for o, s, n in zip(sums, slots, names)]

    g_out = _wgrad(ycat, dmix, D, D, tt_sq, "wgrad_out").reshape(N_CHIPS, D // N_CHIPS, D)
    g_fi = _wgrad(h2, dgu, D, FFB, tt, "wgrad_ffn_in")
    g_fo = _wgrad(act, dff, FFB, D, tt, "wgrad_ffn_out").reshape(N_CHIPS, DFF // N_CHIPS, D)
    late_names = ["out", "ffn_in", "ffn_out"]
    late_g4 = [by_core_half(g) for g in (g_out, g_fi, g_fo)]

    (dproj, dws, dbs, dln), late_recv = _gmlp_bwd(proj, dycat, ws_b, bst, lnw, lnb, late_g4)
    late_sums = core_sums(late_g4, late_recv, late_names)
    (dproj, dlb, dgn), late_slots, (acc2_all, dln_all, dbs_all, dws_all) = _hgrn_bwd(
        proj, o_pre, a_all, st_all, dycat, lower_bounds, gn_w, dproj, tables, late_sums, [acc2, dln, dbs, dws])

    g_in, g_in_wire = _wgrad(h1, dproj, D, D, tt_sq, "wgrad_in", bf16_copy=True)
    (in_recv,) = _exchange_core_halves([by_core_half(g_in_wire)], "exchange_core_halves_in")
    in_sums = [_add_core_halves_in(by_core_half(g_in), in_recv, c_idx, "add_core_in")]
    (grad_x, acc1), in_slots = _proj_in_bwd(dproj, x2d, dx1, nw1, ada, w_in_b, tp, in_sums)
    names = ["in"] + late_names
    halves = chip_sums(in_sums, in_slots, ["in"]) + chip_sums(late_sums, late_slots, late_names)
    sibling_halves, (acc1_all, dlb_all, dgn_all) = _share_halves_and_gather(halves, [acc1, dlb, dgn])

    big_w = [(w_in, m_w_in, v_w_in), (w_out, m_w_out, v_w_out), (w_ffn_in, m_w_ffn_in, v_w_ffn_in),
             (w_ffn_out, m_w_ffn_out, v_w_ffn_out)]
    big_out = []
    for mine, sib, (w, m, v), n in zip(halves, sibling_halves, big_w, names):
        res = _adamw_halves(w[0], mine, sib, m[0], v[0], c_idx, _row_block(mine.shape[0]), "adamw_" + n)
        big_out.append([r[None] for r in res])

    gathered = [acc1_all, acc2_all, dln_all, dlb_all, dgn_all, dbs_all, dws_all]
    small, loss, dada_all = _small_finalize(
        gathered,
        _small_2d(b_ada, norm1_w, norm2_w, final_norm_w, v_ln_w, v_ln_b, lower_bounds, gn_w, b_s, w_s),
        _small_2d(m_b_ada, m_norm1_w, m_norm2_w, m_final_norm_w, m_v_ln_w, m_v_ln_b, m_lower_bounds, m_gn_w, m_b_s, m_w_s),
        _small_2d(v_b_ada, v_norm1_w, v_norm2_w, v_final_norm_w, v_v_ln_w, v_v_ln_b, v_lower_bounds, v_gn_w, v_b_s, v_w_s))
    small = [_small_original_shapes(d) for d in small]
    loss = loss.reshape(())

    ada_out = [o[None] for o in _ada_wgrad_adam(cact.T, dada_all, w_ada[0], m_w_ada[0], v_w_ada[0], chip_idx)]

    order_names = ['w_ada', 'b_ada', 'norm1_w', 'w_in', 'w_s', 'b_s', 'v_ln_w', 'v_ln_b', 'lower_bounds', 'gn_w',
                   'w_out', 'norm2_w', 'w_ffn_in', 'w_ffn_out', 'final_norm_w']
    big_idx = {'w_in': 0, 'w_out': 1, 'w_ffn_in': 2, 'w_ffn_out': 3}
    outs = [loss, grad_x.reshape(1, T, D)]
    for kind in range(4):
        for n in order_names:
            if n == 'w_ada':
                outs.append(ada_out[kind])
            elif n in big_idx:
                outs.append(big_out[big_idx[n]][kind])
            else:
                outs.append(small[kind][n])
    return tuple(outs)
```

```python
import jax
import jax.numpy as jnp
import numpy as np
from jax import lax
from jax.experimental import pallas as pl
from jax.experimental.pallas import tpu as pltpu

F32 = jnp.float32
BF16 = jnp.bfloat16
SDS = jax.ShapeDtypeStruct
MESH = pl.DeviceIdType.MESH
HIGHEST = lax.Precision.HIGHEST

D = 1024
DG = 512
DH = 512
NH = 4
HD = 128
BLK = 128
CH = 64
DFF = 2816
DIN = 3072
FFB = 1408
LEVELS = (64, 32, 16, 8, 4, 2)
HGRN_CHUNKS_PER_STEP = 8
GMLP_ROWS_PER_STEP = 1024
TOKEN_TILE = 256
PROJ_TILE = 1024
WGRAD_TOKENS = 2048
WGRAD_TOKENS_SQUARE = 4096
N_CHIPS = 4
N_DEV = 8
EPS = 1e-6
LR, B1, B2, AEPS, WD, STEP = 0.001, 0.9, 0.999, 1e-08, 0.01, 10

NT = (((1,), (1,)), ((), ()))
TN = (((0,), (0,)), ((), ()))


def _full(shape):
    nd = len(shape)
    return pl.BlockSpec(shape, lambda *_: (0,) * nd)


ADA_SH1, ADA_SC1, ADA_G1, ADA_SH2, ADA_SC2, ADA_G2 = range(6)


def _ada_part(k):
    return pl.BlockSpec((1, D), lambda *_: (0, k))


def _resident(shape):
    nd = len(shape)
    return pl.BlockSpec(shape, lambda *_: (0,) * nd, pipeline_mode=pl.Buffered(1))


def _arb(n=1):
    return pltpu.CompilerParams(dimension_semantics=("arbitrary",) * n)


def _dot(a, b, dims=None, precision=None):
    if dims is None:
        return jnp.dot(a, b, preferred_element_type=F32, precision=precision)
    return lax.dot_general(a, b, dims, preferred_element_type=F32, precision=precision)


def _sigmoid(x):
    return jax.nn.sigmoid(x)


def _gelu_parts(x):
    cdf = 0.5 * (1.0 + lax.erf(x * 0.7071067811865476))
    pdf = jnp.exp(-0.5 * x * x) * 0.3989422804014327
    return x * cdf, cdf + x * pdf


def _rms(x):
    return lax.rsqrt(jnp.mean(x * x, axis=-1, keepdims=True) + EPS)


def _rms_bwd(xhat, r, gw):
    return r * (gw - xhat * jnp.mean(xhat * gw, axis=-1, keepdims=True))


def _lower_bound(lbp_ref):
    l0, l1 = lbp_ref[0:1, :], lbp_ref[1:2, :]
    m = jnp.maximum(l0, l1)
    e0, e1 = jnp.exp(l0 - m), jnp.exp(l1 - m)
    return e0 / (e0 + e1), e1 / (e0 + e1)


def _gmlp_common(u, v, lnw, lnb, ws_ref, bst_ref):
    ug, dug = _gelu_parts(u)
    vg, dvg = _gelu_parts(v)
    mu = jnp.mean(vg, axis=-1, keepdims=True)
    vc = vg - mu
    rstd = lax.rsqrt(jnp.mean(vc * vc, axis=-1, keepdims=True) + EPS)
    vhat = vc * rstd
    vn = vhat * lnw + lnb
    vnb = vn.astype(BF16)
    mixed = []
    for h in range(NH):
        sl = slice(h * HD, (h + 1) * HD)
        mixed.append(_dot(ws_ref[h], vnb[:, sl]) + bst_ref[:, h:h + 1])
    return ug, dug, dvg, rstd, vhat, vnb, jnp.concatenate(mixed, axis=1)


def _hgrn_tables():
    t = np.arange(CH)[:, None]
    j = np.arange(CH)[None, :]
    blocks = [j <= t, j > t]
    masks = []
    for n in LEVELS:
        mid = t - t % n + n // 2
        blocks.append(np.where(t >= mid, (j >= mid) & (j <= t), (j > t) & (j < mid)))
        masks.append((t // n == j // n) & (t % n >= n // 2) & (j % n < n // 2))
    w = np.concatenate(blocks, axis=0).astype(np.float32)
    m = np.stack(masks).astype(np.float32)
    return (jnp.asarray(w, BF16), jnp.asarray(w.T, BF16), jnp.asarray(m), jnp.asarray(m + m.transpose(0, 2, 1)))


def _split_dot(w, x, parts):
    acc = None
    for _ in range(parts):
        piece = x.astype(BF16)
        term = _dot(w, piece)
        acc = term if acc is None else acc + term
        x = x - piece.astype(F32)
    return acc


def _hgrn_decays(f, w_ref):
    b = _split_dot(w_ref[0:CH, :], jnp.log(f), 3)
    row = lax.broadcasted_iota(jnp.int32, (CH, 1), 0)
    blocks = [jnp.exp(b), jnp.exp(b[CH - 1:CH, :] - b)]
    for n in LEVELS:
        up = (row & (n // 2)) != 0
        if n >= 8:
            ref = b.reshape(CH // n, n, DH)[:, n // 2 - 1:n // 2, :]
            ref = jnp.broadcast_to(ref, (CH // n, n, DH)).reshape(CH, DH)
            blocks.append(jnp.exp(jnp.where(up, b - ref, ref - b)))
        elif n == 4:
            r4 = row & 3
            two = jnp.where(r4 == 3, pltpu.roll(f, 1, 0) * f, 1.0)
            blocks.append(jnp.where(r4 == 0, pltpu.roll(f, CH - 1, 0), jnp.where(r4 == 2, f, two)))
        else:
            blocks.append(jnp.where(up, f, 1.0))
    return blocks


def _hgrn_gates(q, fl, lb, omlb, w_ref):
    sq = _sigmoid(q)
    qf = q * sq
    sig = _sigmoid(fl)
    f = lb + omlb * sig
    k = 1.0 - f
    return sq, qf, sig, f, k, _hgrn_decays(f, w_ref)


def _level_factor(e, li, sl, row, qh, kh):
    el = e[2 + li][:, sl]
    up = (row & (LEVELS[li] // 2)) != 0
    return el, up, el * jnp.where(up, qh, kh)


def _proj_hgrn_fwd(x, nw1, ada, w_in_b, lower_bounds, gn_w, tables, placed, axes):
    T = x.shape[0]
    nc = T // CH
    nch = min(HGRN_CHUNKS_PER_STEP, nc)
    steps = nc // nch
    w_st, _, masks, _ = tables
    nw = len(placed)
    pass_step = (13 * steps) // 16
    q0 = 2 * DG

    def body(*refs):
        x_ref, nw_ref, sc_ref, sh_ref, win_ref, lbp_ref, gn_ref, w_ref, m_ref = refs[:9]
        h_ref, p_ref, y_ref, o_ref, a_ref, st_ref = refs[9 + nw:15 + nw]
        s_scr, send_sems, recv_sems = refs[15 + 2 * nw:]
        gather = _WeightGather(refs[15 + nw:15 + 2 * nw], axes, send_sems, recv_sems)
        step = pl.program_id(0)
        xv = x_ref[...]
        hb = (((xv * _rms(xv)) * nw_ref[...]) * (1.0 + sc_ref[...]) + sh_ref[...]).astype(BF16)
        h_ref[...] = hb
        p_ref[...] = _dot(hb, win_ref[...])

        @pl.when(step == 0)
        def _():
            gather.start()
            s_scr[...] = jnp.zeros_like(s_scr)

        @pl.when(step == pass_step)
        def _():
            gather.forward()

        lb, omlb = _lower_bound(lbp_ref)
        row = lax.broadcasted_iota(jnp.int32, (CH, 1), 0)
        eye = lax.broadcasted_iota(jnp.int32, (CH, CH), 0) == lax.broadcasted_iota(jnp.int32, (CH, CH), 1)
        in_level = [m_ref[li] > 0.0 for li in range(len(LEVELS))]
        pre = []
        for ci in range(nch):
            rs = slice(ci * CH, (ci + 1) * CH)
            _, qf, _, _, k, e = _hgrn_gates(p_ref[rs, q0:q0 + DH], p_ref[rs, q0 + DH:q0 + 2 * DH], lb, omlb, w_ref)
            mats = []
            for h in range(NH):
                sl = slice(h * HD, (h + 1) * HD)
                qh, kh = qf[:, sl], k[:, sl]
                a = jnp.where(eye, jnp.sum(qh * kh, axis=-1, keepdims=True), 0.0)
                for li in range(len(LEVELS)):
                    _, _, y = _level_factor(e, li, sl, row, qh, kh)
                    yb = y.astype(BF16)
                    a = jnp.where(in_level[li], _dot(yb, yb, NT), a)
                a_ref[ci, h] = a
                mats.append(a.astype(BF16))
            eb = e[0]
            pre.append(((qf * eb).astype(BF16), eb[CH - 1:CH, :], (k * e[1]).astype(BF16), mats))
        for ci in range(nch):
            rs = slice(ci * CH, (ci + 1) * CH)
            qe, ebl, kd, mats = pre[ci]
            v = p_ref[rs, q0 + 2 * DH:q0 + 3 * DH]
            g = p_ref[rs, q0 + 3 * DH:q0 + 4 * DH]
            for h in range(NH):
                sl = slice(h * HD, (h + 1) * HD)
                st0 = s_scr[h]
                st_ref[ci, h] = st0
                vb = v[:, sl].astype(BF16)
                o = _dot(qe[:, sl], st0.astype(BF16), NT) + _dot(mats[h], vb)
                s_scr[h] = st0 * ebl[:, sl] + _dot(vb, kd[:, sl], TN)
                o_ref[rs, sl] = o
                gh = g[:, sl]
                y_ref[rs, sl] = (((o * _rms(o)) * gn_ref[...]) * (gh * _sigmoid(gh))).astype(BF16)

        @pl.when(step == steps - 1)
        def _():
            gather.finish()

    rows = nch * CH
    row = lambda c: (c, 0)
    anyspec = pl.BlockSpec(memory_space=pl.ANY)
    res = pl.pallas_call(
        body, grid=(steps,),
        in_specs=[pl.BlockSpec((rows, D), row), _full((1, D)), _ada_part(ADA_SC1), _ada_part(ADA_SH1), _resident((D, DIN)),
                  _full((2, DH)), _full((1, HD)), _full(w_st.shape), _full(masks.shape)] + [anyspec] * nw,
        out_specs=[pl.BlockSpec((rows, D), row), pl.BlockSpec((rows, DIN), row),
                   pl.BlockSpec((rows, DH), lambda c: (c, 1)),
                   pl.BlockSpec((rows, DH), row),
                   pl.BlockSpec((nch, NH, CH, CH), lambda c: (c, 0, 0, 0)),
                   pl.BlockSpec((nch, NH, HD, HD), lambda c: (c, 0, 0, 0))] + [anyspec] * nw,
        out_shape=[SDS((T, D), BF16), SDS((T, DIN), F32), SDS((T, D), BF16), SDS((T, DH), F32),
                   SDS((nc, NH, CH, CH), F32), SDS((nc, NH, HD, HD), F32)] + [SDS(a.shape, a.dtype) for a in placed],
        scratch_shapes=[pltpu.VMEM((NH, HD, HD), F32)] + _gather_sems(nw),
        input_output_aliases={9 + i: 6 + i for i in range(nw)},
        compiler_params=_arb(), name="proj_hgrn_fwd")(x, nw1, ada, ada, w_in_b, lower_bounds, gn_w, w_st, masks, *placed)
    return res[:6], res[6:]


def _token_local(x, ycat, tgt, g1, nw2, sc2, sh2, g2, fw, w_out_b, w_fi_b, w_fo_b, proj, ws_b, bst, lnw, lnb, tm):
    T = x.shape[0]
    inv_d = 1.0 / D

    def body(x_ref, yb_ref, t_ref, g1_ref, nw2_ref, sc2_ref, sh2_ref, g2_ref, fw_ref, wo_ref, wfi_ref, wfo_ref,
             u_ref, v_ref, ws_ref, bst_ref, lnw_ref, lnb_ref,
             dy_ref, dx1_ref, h2_ref, act_ref, dff_ref, dgu_ref, dmix_ref, acc_ref, ya_ref):
        @pl.when(pl.program_id(0) == 0)
        def _():
            acc_ref[...] = jnp.zeros_like(acc_ref)

        def acc(row, val):
            acc_ref[row:row + 1, :] += jnp.sum(val, axis=0, keepdims=True)

        for bi in range(tm // BLK):
            rs = slice(bi * BLK, (bi + 1) * BLK)
            ug, _, _, _, _, _, mixed = _gmlp_common(u_ref[rs, :], v_ref[rs, :], lnw_ref[...], lnb_ref[...], ws_ref, bst_ref)
            ya_ref[rs, :] = (ug * mixed).astype(BF16)
        g1v, g2v = g1_ref[...], g2_ref[...]
        mix = _dot(ya_ref[...], wo_ref[0:DG, :]) + _dot(yb_ref[...], wo_ref[DG:D, :])
        x1 = x_ref[...] + g1v * mix
        r2 = _rms(x1)
        xh2 = x1 * r2
        n2 = xh2 * nw2_ref[...]
        osc2 = 1.0 + sc2_ref[...]
        h2b = (n2 * osc2 + sh2_ref[...]).astype(BF16)
        h2_ref[...] = h2b
        ff = jnp.zeros((tm, D), F32)
        saved = []
        for kb in range(DFF // FFB):
            gate = _dot(h2b, wfi_ref[:, kb * FFB:(kb + 1) * FFB])
            up = _dot(h2b, wfi_ref[:, DFF + kb * FFB:DFF + (kb + 1) * FFB])
            sg = _sigmoid(gate)
            actb = (gate * sg * up).astype(BF16)
            act_ref[:, kb * FFB:(kb + 1) * FFB] = actb
            ff = ff + _dot(actb, wfo_ref[kb * FFB:(kb + 1) * FFB, :])
            saved.append((gate, up, sg))
        x2 = x1 + g2v * ff
        r3 = _rms(x2)
        xh3 = x2 * r3
        err = xh3 * fw_ref[...] - t_ref[...]
        acc(6, (0.5 * inv_d) * err * err)
        dy = err * inv_d
        acc(4, dy * xh3)
        dx2 = _rms_bwd(xh3, r3, dy * fw_ref[...])
        acc(0, dx2 * ff)
        dffb = (dx2 * g2v).astype(BF16)
        dff_ref[...] = dffb
        dh2 = jnp.zeros((tm, D), F32)
        for kb in range(DFF // FFB):
            gate, up, sg = saved[kb]
            da = _dot(dffb, wfo_ref[kb * FFB:(kb + 1) * FFB, :], NT)
            dgate = (da * up * (sg * (1.0 + gate * (1.0 - sg)))).astype(BF16)
            dup = (da * gate * sg).astype(BF16)
            dgu_ref[:, kb * FFB:(kb + 1) * FFB] = dgate
            dgu_ref[:, DFF + kb * FFB:DFF + (kb + 1) * FFB] = dup
            dh2 = dh2 + _dot(dgate, wfi_ref[:, kb * FFB:(kb + 1) * FFB], NT)
            dh2 = dh2 + _dot(dup, wfi_ref[:, DFF + kb * FFB:DFF + (kb + 1) * FFB], NT)
        acc(2, dh2)
        acc(1, dh2 * n2)
        dn2 = dh2 * osc2
        acc(3, dn2 * xh2)
        dx1 = dx2 + _rms_bwd(xh2, r2, dn2 * nw2_ref[...])
        acc(5, dx1 * mix)
        dmixb = (dx1 * g1v).astype(BF16)
        dmix_ref[...] = dmixb
        dy_ref[...] = _dot(dmixb, wo_ref[...], NT)
        dx1_ref[...] = dx1

    row = lambda i: (i, 0)
    vec = _full((1, D))
    half = lambda j: pl.BlockSpec((tm, DG), lambda i: (i, j))
    return pl.pallas_call(
        body, grid=(T // tm,),
        in_specs=[pl.BlockSpec((tm, D), row), half(1), pl.BlockSpec((tm, D), row),
                  _ada_part(ADA_G1), vec, _ada_part(ADA_SC2), _ada_part(ADA_SH2), _ada_part(ADA_G2), vec,
                  _resident((D, D)), _resident((D, 2 * DFF)), _resident((DFF, D)),
                  half(0), half(1), _full((NH, BLK, BLK)), _full((BLK, NH)), _full((1, DG)), _full((1, DG))],
        out_specs=[pl.BlockSpec((tm, D), row), pl.BlockSpec((tm, D), row), pl.BlockSpec((tm, D), row),
                   pl.BlockSpec((tm, DFF), row), pl.BlockSpec((tm, D), row), pl.BlockSpec((tm, 2 * DFF), row),
                   pl.BlockSpec((tm, D), row), _full((8, D)), half(0)],
        out_shape=[SDS((T, D), F32), SDS((T, D), F32), SDS((T, D), BF16), SDS((T, DFF), BF16), SDS((T, D), BF16),
                   SDS((T, 2 * DFF), BF16), SDS((T, D), BF16), SDS((8, D), F32), SDS((T, D), BF16)],
        input_output_aliases={1: 8},
        compiler_params=_arb(), name="token_local")(x, ycat, tgt, g1, nw2, sc2, sh2, g2, fw, w_out_b, w_fi_b, w_fo_b,
                                                    proj, proj, ws_b, bst, lnw, lnb)


def _gmlp_bwd(proj, dycat, ws_b, bst, lnw, lnb, grads):
    T = proj.shape[0]
    rows = min(GMLP_ROWS_PER_STEP, T)
    nb = T // rows
    nw = len(grads)

    def body(*refs):
        u_ref, v_ref, dy_ref, ws_ref, bst_ref, lnw_ref, lnb_ref = refs[:7]
        dp_ref, dws_ref, dbs_ref, dln_ref = refs[7 + nw:11 + nw]
        dbs_acc, send_sems, recv_sems = refs[11 + 2 * nw:]
        exchange = _CoreExchange(refs[7:7 + nw], refs[11 + nw:11 + 2 * nw], send_sems, recv_sems)
        i = pl.program_id(0)

        @pl.when(i == 0)
        def _():
            exchange.start()
            dws_ref[...] = jnp.zeros_like(dws_ref)
            dln_ref[...] = jnp.zeros_like(dln_ref)
            dbs_acc[...] = jnp.zeros_like(dbs_acc)

        r = lax.broadcasted_iota(jnp.int32, (BLK, BLK), 0) // CH
        c = lax.broadcasted_iota(jnp.int32, (BLK, BLK), 1) // CH
        for bi in range(rows // BLK):
            rs = slice(bi * BLK, (bi + 1) * BLK)
            ug, dug, dvg, rstd, vhat, vnb, mixed = _gmlp_common(
                u_ref[rs, :], v_ref[rs, :], lnw_ref[...], lnb_ref[...], ws_ref, bst_ref)
            dya = dy_ref[rs, :]
            dp_ref[rs, 0:DG] = (dya * mixed * dug).astype(BF16)
            dmixed = dya * ug
            dbs_acc[...] += dmixed
            dmb = dmixed.astype(BF16)
            dvn = []
            for h in range(NH):
                sl = slice(h * HD, (h + 1) * HD)
                dws_ref[h * BLK:(h + 1) * BLK, :] += jnp.where(r >= c, _dot(dmb[:, sl], vnb[:, sl], NT), 0.0)
                dvn.append(_dot(ws_ref[h], dmb[:, sl], TN))
            dvn = jnp.concatenate(dvn, axis=1)
            dln_ref[0:1, :] += jnp.sum(dvn * vhat, axis=0, keepdims=True)
            dln_ref[1:2, :] += jnp.sum(dvn, axis=0, keepdims=True)
            dvh = dvn * lnw_ref[...]
            dvgel = rstd * (dvh - jnp.mean(dvh, axis=-1, keepdims=True) - vhat * jnp.mean(dvh * vhat, axis=-1, keepdims=True))
            dp_ref[rs, DG:2 * DG] = (dvgel * dvg).astype(BF16)

        @pl.when(i == nb - 1)
        def _():
            head = lax.broadcasted_iota(jnp.int32, (8, BLK), 0)
            ones = jnp.ones((8, HD), F32)
            out = jnp.zeros((8, BLK), F32)
            for h in range(NH):
                sums = _dot(ones, dbs_acc[:, h * HD:(h + 1) * HD], NT, precision=HIGHEST)
                out = out + jnp.where(head == h, sums, 0.0)
            dbs_ref[...] = out
            exchange.finish()

    anyspec = pl.BlockSpec(memory_space=pl.ANY)
    res = pl.pallas_call(
        body, grid=(nb,),
        in_specs=[pl.BlockSpec((rows, DG), lambda i: (i, 0)), pl.BlockSpec((rows, DG), lambda i: (i, 1)),
                  pl.BlockSpec((rows, DG), lambda i: (i, 0)),
                  _full((NH, BLK, BLK)), _full((BLK, NH)), _full((1, DG)), _full((1, DG))] + [anyspec] * nw,
        out_specs=[pl.BlockSpec((rows, 2 * DG), lambda i: (i, 2)), _full((NH * BLK, BLK)), _full((8, BLK)), _full((8, DG))]
        + [anyspec] * nw,
        out_shape=[SDS((T, DIN), BF16), SDS((NH * BLK, BLK), F32), SDS((8, BLK), F32), SDS((8, DG), F32)]
        + _core_exchange_shapes(grads),
        scratch_shapes=[pltpu.VMEM((BLK, DG), F32)] + _core_exchange_sems(nw),
        compiler_params=_arb(), name="gmlp_bwd")(proj, proj, dycat, ws_b, bst, lnw, lnb, *grads)
    return res[:4], res[4:]


def _hgrn_bwd(proj, o_pre, a_all, st_all, dycat, lower_bounds, gn_w, dproj, tables, sums, row_blocks):
    T = proj.shape[0]
    nc = T // CH
    nch = min(HGRN_CHUNKS_PER_STEP, nc)
    steps = nc // nch
    w_st, w_st_t, _, masks_sym = tables
    n_lev = len(LEVELS)
    nw, nr = len(sums), len(row_blocks)

    def body(*refs):
        q_ref, f_ref, i_ref, g_ref, o_ref, a_ref, st_ref, dy_ref, lbp_ref, gn_ref, w_ref, wt_ref, ms_ref = refs[:13]
        n_in = 14 + nw + nr
        dp_ref, dlb_ref, dgn_ref = refs[n_in:n_in + 3]
        ds_scr, dx_scr = refs[n_in + 3 + nw + nr:n_in + 5 + nw + nr]
        sems = refs[n_in + 5 + nw + nr:]
        exchange = _ChipExchange(refs[14:14 + nw], refs[n_in + 3:n_in + 3 + nw], *sems[:2])
        rows_gather = _RowGather(refs[14 + nw:n_in], refs[n_in + 3 + nw:n_in + 3 + nw + nr], *sems[2:])
        i = pl.program_id(0)

        @pl.when(i == 0)
        def _():
            rows_gather.start()
            exchange.start()
            ds_scr[...] = jnp.zeros_like(ds_scr)
            dlb_ref[...] = jnp.zeros_like(dlb_ref)
            dgn_ref[...] = jnp.zeros_like(dgn_ref)

        @pl.when(i == steps // 2)
        def _():
            rows_gather.forward()

        lb, omlb = _lower_bound(lbp_ref)
        row = lax.broadcasted_iota(jnp.int32, (CH, 1), 0)
        eye = lax.broadcasted_iota(jnp.int32, (CH, CH), 0) == lax.broadcasted_iota(jnp.int32, (CH, CH), 1)
        lower = lax.broadcasted_iota(jnp.int32, (CH, CH), 0) > lax.broadcasted_iota(jnp.int32, (CH, CH), 1)
        dgn = jnp.zeros((1, HD), F32)
        pre = []
        for ci in range(nch):
            rs = slice(ci * CH, (ci + 1) * CH)
            q = q_ref[rs, :]
            v = i_ref[rs, :]
            g = g_ref[rs, :]
            sq, qf, sig, f, k, e = _hgrn_gates(q, f_ref[rs, :], lb, omlb, w_ref)
            eb = e[0]
            ekd = e[1]
            kd = k * ekd
            qe = qf * eb
            dob_h, dqe_h, dqf_h, dki_h, dv_h, dg_h = [], [], [], [], [], []
            for h in range(NH):
                sl = slice(h * HD, (h + 1) * HD)
                o = o_ref[rs, sl]
                ro = _rms(o)
                oh = o * ro
                gh = g[:, sl]
                sg = _sigmoid(gh)
                dyb = dy_ref[rs, sl]
                dg_h.append(dyb * (oh * gn_ref[...]) * (sg * (1.0 + gh * (1.0 - sg))))
                don = dyb * (gh * sg)
                dgn = dgn + jnp.sum(don * oh, axis=0, keepdims=True)
                dob = _rms_bwd(oh, ro, don * gn_ref[...]).astype(BF16)
                vb = v[:, sl].astype(BF16)
                qh, kh = qf[:, sl], k[:, sl]
                dqe = _dot(dob, st_ref[ci, h].astype(BF16))
                da = _dot(dob, vb, NT)
                ddiag = jnp.sum(jnp.where(eye, da, 0.0), axis=-1, keepdims=True)
                dsym = jnp.where(lower, da, _dot(vb, dob, NT))
                upper_part = jnp.zeros((CH, HD), F32)
                both = jnp.zeros((CH, HD), F32)
                for li in range(n_lev):
                    el, up, y = _level_factor(e, li, sl, row, qh, kh)
                    dyv = _dot((ms_ref[li] * dsym).astype(BF16), y.astype(BF16))
                    dx_scr[ci, (2 + li) * CH:(3 + li) * CH, sl] = dyv * y
                    dye = dyv * el
                    upper_part = upper_part + jnp.where(up, dye, 0.0)
                    both = both + dye
                dob_h.append(dob)
                dqe_h.append(dqe)
                dqf_h.append(dqe * eb[:, sl] + ddiag * kh + upper_part)
                dki_h.append(ddiag * qh + (both - upper_part))
                dv_h.append(_dot(a_ref[ci, h].astype(BF16), dob, TN))
            dp_ref[rs, 0:DH] = (jnp.concatenate(dqf_h, axis=1) * (sq * (1.0 + q * (1.0 - sq)))).astype(BF16)
            dp_ref[rs, 3 * DH:4 * DH] = jnp.concatenate(dg_h, axis=1).astype(BF16)
            pre.append((v, sig, f, eb, ekd, kd, qe, dob_h, jnp.concatenate(dqe_h, axis=1), dki_h, dv_h))
        dgn_ref[0:1, :] += dgn
        for ci in reversed(range(nch)):
            rs = slice(ci * CH, (ci + 1) * CH)
            v, sig, f, eb, ekd, kd, qe, dob_h, dqe, dki_h, dv_h = pre[ci]
            ebl = eb[CH - 1:CH, :]
            dbl_h, dkd_h, dv2_h = [], [], []
            for h in range(NH):
                sl = slice(h * HD, (h + 1) * HD)
                dst1 = ds_scr[h]
                dst1b = dst1.astype(BF16)
                ds_scr[h] = dst1 * ebl[:, sl] + _dot(dob_h[h], qe[:, sl].astype(BF16), TN)
                dbl_h.append(ebl[:, sl] * jnp.sum(st_ref[ci, h] * dst1, axis=0, keepdims=True))
                dkd_h.append(_dot(v[:, sl].astype(BF16), dst1b))
                dv2_h.append(dv_h[h] + _dot(kd[:, sl].astype(BF16), dst1b, NT))
            dkd = jnp.concatenate(dkd_h, axis=1)
            dx_scr[ci, 0:CH, :] = dqe * qe + jnp.where(row == CH - 1, jnp.concatenate(dbl_h, axis=1), 0.0)
            dx_scr[ci, CH:2 * CH, :] = dkd * kd
            dlf = _split_dot(wt_ref[...], dx_scr[ci], 2)
            df = dlf / f - (dkd * ekd + jnp.concatenate(dki_h, axis=1))
            dlb_ref[0:1, :] += jnp.sum(df * (1.0 - sig), axis=0, keepdims=True)
            dp_ref[rs, DH:2 * DH] = (df * omlb * sig * (1.0 - sig)).astype(BF16)
            dp_ref[rs, 2 * DH:3 * DH] = jnp.concatenate(dv2_h, axis=1).astype(BF16)

        @pl.when(i == steps - 1)
        def _():
            gl = dlb_ref[0:1, :] * lb * omlb
            dlb_ref[0:1, :] = gl
            dlb_ref[1:2, :] = -gl
            exchange.finish()
            rows_gather.finish()

    rev = lambda j: pl.BlockSpec((nch * CH, DH), lambda c: (steps - 1 - c, j))
    anyspec = pl.BlockSpec(memory_space=pl.ANY)
    res = pl.pallas_call(
        body, grid=(steps,),
        in_specs=[rev(2), rev(3), rev(4), rev(5), rev(0),
                  pl.BlockSpec((nch, NH, CH, CH), lambda c: (steps - 1 - c, 0, 0, 0)),
                  pl.BlockSpec((nch, NH, HD, HD), lambda c: (steps - 1 - c, 0, 0, 0)),
                  rev(1), _full((2, DH)), _full((1, HD)),
                  _full(w_st.shape), _full(w_st_t.shape), _full(masks_sym.shape),
                  anyspec] + [anyspec] * (nw + nr),
        out_specs=[pl.BlockSpec((nch * CH, 4 * DH), lambda c: (steps - 1 - c, 0)), _full((8, DH)), _full((8, HD))]
        + [anyspec] * (nw + nr),
        out_shape=[SDS((T, DIN), BF16), SDS((8, DH), F32), SDS((8, HD), F32)] + _slot_shapes(sums)
        + _gather_rows_shapes(row_blocks),
        scratch_shapes=[pltpu.VMEM((NH, HD, HD), F32), pltpu.VMEM((nch, (2 + n_lev) * CH, DH), F32)]
        + _exchange_sems(nw) + _gather_rows_sems(nr),
        input_output_aliases={13: 0},
        compiler_params=_arb(), name="hgrn_bwd")(proj, proj, proj, proj, o_pre, a_all, st_all, dycat, lower_bounds, gn_w,
                                                 w_st, w_st_t, masks_sym, dproj, *sums, *row_blocks)
    return res[:3], res[3:3 + nw], res[3 + nw:]


def _proj_in_bwd(dproj, x, dx1, nw, sc, w_in_b, tm, sums):
    T = x.shape[0]
    ns = len(sums)
    steps = T // tm

    def body(*refs):
        dp_ref, x_ref, dx1_ref, nw_ref, sc_ref, w_ref = refs[:6]
        gx_ref, acc_ref = refs[6 + ns:8 + ns]
        exchange = _ChipExchange(refs[6:6 + ns], refs[8 + ns:8 + 2 * ns], *refs[8 + 2 * ns:])

        @pl.when(pl.program_id(0) == 0)
        def _():
            exchange.start()
            acc_ref[...] = jnp.zeros_like(acc_ref)

        dh = _dot(dp_ref[:, 0:4 * DH], w_ref[:, 2 * DG:DIN], NT) + _dot(dp_ref[:, 4 * DH:DIN], w_ref[:, 0:2 * DG], NT)
        xv = x_ref[...]
        r = _rms(xv)
        xh = xv * r
        n1 = xh * nw_ref[...]
        acc_ref[0:1, :] += jnp.sum(dh, axis=0, keepdims=True)
        acc_ref[1:2, :] += jnp.sum(dh * n1, axis=0, keepdims=True)
        dn = dh * (1.0 + sc_ref[...])
        acc_ref[2:3, :] += jnp.sum(dn * xh, axis=0, keepdims=True)
        gx_ref[...] = dx1_ref[...] + _rms_bwd(xh, r, dn * nw_ref[...])

        @pl.when(pl.program_id(0) == steps - 1)
        def _():
            exchange.finish()

    row = lambda i: (i, 0)
    anyspec = pl.BlockSpec(memory_space=pl.ANY)
    res = pl.pallas_call(
        body, grid=(steps,),
        in_specs=[pl.BlockSpec((tm, DIN), row), pl.BlockSpec((tm, D), row), pl.BlockSpec((tm, D), row),
                  _full((1, D)), _ada_part(ADA_SC1), _resident((D, DIN))] + [anyspec] * ns,
        out_specs=[pl.BlockSpec((tm, D), row), _full((8, D))] + [anyspec] * ns,
        out_shape=[SDS((T, D), F32), SDS((8, D), F32)] + _slot_shapes(sums),
        scratch_shapes=_exchange_sems(ns),
        compiler_params=_arb(), name="proj_in_bwd")(dproj, x, dx1, nw, sc, w_in_b, *sums)
    return res[:2], res[2:]


def _wgrad(a, b, bk, bn, tt, name, bf16_copy=False):
    T, K = a.shape
    N = b.shape[1]
    nn, nk, nt = N // bn, K // bk, T // tt
    bmap = lambda n, k, t: (t, n)

    def body(a_ref, b_ref, o_ref, *copy_ref):
        @pl.when(pl.program_id(2) == 0)
        def _():
            o_ref[...] = jnp.zeros_like(o_ref)

        o_ref[0] += _dot(a_ref[...], b_ref[...], TN)

        if bf16_copy:
            @pl.when(pl.program_id(2) == nt - 1)
            def _():
                copy_ref[0][...] = o_ref[...].astype(BF16)

    ospec = pl.BlockSpec((1, bk, bn), lambda n, k, t: (n, k, 0))
    return pl.pallas_call(
        body, grid=(nn, nk, nt),
        in_specs=[pl.BlockSpec((tt, bk), lambda n, k, t: (t, k)), pl.BlockSpec((tt, bn), bmap)],
        out_specs=[ospec, ospec] if bf16_copy else ospec,
        out_shape=[SDS((nn, K, bn), F32), SDS((nn, K, bn), BF16)] if bf16_copy else SDS((nn, K, bn), F32),
        compiler_params=_arb(3), name=name)(a, b)


def _adam_math(w, g, m, v):
    m = B1 * m + (1.0 - B1) * g
    v = B2 * v + (1.0 - B2) * (g * g)
    m_hat = m / (1.0 - B1 ** STEP)
    v_hat = v / (1.0 - B2 ** STEP)
    return -LR * (m_hat / (jnp.sqrt(v_hat) + AEPS) + WD * w), m, v


def _adamw_halves(w, mine, sibling, m, v, c_idx, rb, name):
    R, C = w.shape
    nb = (R // 2) // rb

    def body(c_ref, w_ref, a_ref, b_ref, m_ref, v_ref, g_out, d_out, m_out, v_out):
        g = jnp.where(pl.program_id(0) == c_ref[0], a_ref[...], b_ref[...])
        g_out[...] = g
        d_out[...], m_out[...], v_out[...] = _adam_math(w_ref[...], g, m_ref[...], v_ref[...])

    whole = pl.BlockSpec((rb, C), lambda hh, i, cr: (hh * nb + i, 0))
    half = pl.BlockSpec((rb, C), lambda hh, i, cr: (i, 0))
    return pl.pallas_call(
        body,
        grid_spec=pltpu.PrefetchScalarGridSpec(
            num_scalar_prefetch=1, grid=(2, nb), in_specs=[whole, half, half, whole, whole], out_specs=[whole] * 4),
        out_shape=[SDS((R, C), F32)] * 4, compiler_params=_arb(2), name=name)(c_idx, w, mine, sibling, m, v)


def _ada_wgrad_adam(cact_t, dada_all, w, m, v, chip_idx):
    R, C = w.shape
    rb = 256

    def body(j_ref, c_ref, d_ref, w_ref, m_ref, v_ref, g_out, d_out, m_out, v_out):
        g = _dot(c_ref[...], d_ref[...], precision=HIGHEST)
        g_out[...] = g
        d_out[...], m_out[...], v_out[...] = _adam_math(w_ref[...], g, m_ref[...], v_ref[...])

    spec = pl.BlockSpec((rb, C), lambda i, j: (i, 0))
    return pl.pallas_call(
        body,
        grid_spec=pltpu.PrefetchScalarGridSpec(
            num_scalar_prefetch=1, grid=(R // rb,),
            in_specs=[pl.BlockSpec((rb, N_DEV), lambda i, j: (i, 0)), pl.BlockSpec((N_DEV, C), lambda i, j: (0, j[0])),
                      spec, spec, spec],
            out_specs=[spec] * 4),
        out_shape=[SDS((R, C), F32)] * 4,
        compiler_params=_arb(), name="ada_wgrad_adam")(chip_idx, cact_t, dada_all, w, m, v)


SMALL_NAMES = ('b_ada', 'norm1_w', 'norm2_w', 'final_norm_w', 'v_ln_w', 'v_ln_b', 'lower_bounds', 'gn_w', 'b_s', 'w_s')


def _small_finalize(gathered, params, moms, vels):
    n_in = len(gathered)

    def body(*refs):
        acc1, acc2, dln, dlb, dgn, dbs, dws = refs[:n_in]
        prm = [dict(zip(SMALL_NAMES, refs[n_in + k * 10:n_in + (k + 1) * 10])) for k in range(3)]
        outs = [dict(zip(SMALL_NAMES, refs[n_in + 30 + k * 10:n_in + 30 + (k + 1) * 10])) for k in range(4)]
        loss_ref, dada_ref = refs[n_in + 70:n_in + 72]

        def dev_sum(ref, first, n):
            per = ref.shape[0] // N_DEV
            g = ref[first:first + n, :]
            for dev in range(1, N_DEV):
                g = g + ref[dev * per + first:dev * per + first + n, :]
            return g

        def update(n, g, cols=slice(None)):
            outs[0][n][:, cols] = g
            outs[1][n][:, cols], outs[2][n][:, cols], outs[3][n][:, cols] = _adam_math(
                prm[0][n][:, cols], g, prm[1][n][:, cols], prm[2][n][:, cols])

        ada_rows = ((acc1, 0), (acc1, 1), (acc2, 5), (acc2, 2), (acc2, 1), (acc2, 0))
        for k, (ref, r) in enumerate(ada_rows):
            update('b_ada', dev_sum(ref, r, 1), slice(k * D, (k + 1) * D))
            for dev in range(N_DEV):
                dada_ref[dev:dev + 1, k * D:(k + 1) * D] = ref[8 * dev + r:8 * dev + r + 1, :]
        update('norm1_w', dev_sum(acc1, 2, 1))
        update('norm2_w', dev_sum(acc2, 3, 1))
        update('final_norm_w', dev_sum(acc2, 4, 1))
        update('v_ln_w', dev_sum(dln, 0, 1))
        update('v_ln_b', dev_sum(dln, 1, 1))
        update('lower_bounds', dev_sum(dlb, 0, 2))
        update('gn_w', dev_sum(dgn, 0, 1))
        update('b_s', dev_sum(dbs, 0, NH))
        update('w_s', dev_sum(dws, 0, NH * BLK))
        loss_ref[...] = jnp.sum(dev_sum(acc2, 6, 1), axis=-1, keepdims=True)

    shapes = [SDS(params[n].shape, F32) for n in SMALL_NAMES]
    res = pl.pallas_call(
        body, out_shape=shapes * 4 + [SDS((1, 1), F32), SDS((N_DEV, 6 * D), F32)], name="small_finalize")(
            *gathered, *[d[n] for d in (params, moms, vels) for n in SMALL_NAMES])
    return [dict(zip(SMALL_NAMES, res[k * 10:(k + 1) * 10])) for k in range(4)], res[40], res[41]


def _position():
    x, y, c = lax.axis_index("x"), lax.axis_index("y"), lax.axis_index("c")
    return x, y, c


def _chip_at(x, y, r):
    return (x ^ (r >> 1), y ^ (r & 1))


class _RowGather:
    def __init__(self, ins, outs, send_sems, recv_sems, local_sems):
        self.ins, self.outs = ins, outs
        self.send_sems, self.recv_sems, self.local_sems = send_sems, recv_sems, local_sems
        self.x, self.y, self.c = _position()
        self.me, self.sibling = (self.x, self.y, self.c), (self.x, self.y, 1 - self.c)
        self.chips = [_chip_at(self.x, self.y, r) for r in (1, 2, 3)]

    def _rows(self, b, px, py, pc):
        m_per = self.ins[b].shape[0]
        return self.outs[b].at[pl.ds((4 * px + 2 * py + pc) * m_per, m_per), :]

    def _copy(self, b, k, blk, to, from_input=False):
        return pltpu.make_async_remote_copy(
            src_ref=self.ins[b] if from_input else self._rows(b, *blk), dst_ref=self._rows(b, *blk),
            send_sem=self.send_sems.at[7 * b + k], recv_sem=self.recv_sems.at[7 * b + k],
            device_id=to, device_id_type=MESH)

    def _local(self, b):
        return pltpu.make_async_copy(self.ins[b], self._rows(b, *self.me), self.local_sems.at[b])

    def _first(self, b):
        c = self.c
        return [self._copy(b, 0, self.me, self.sibling, from_input=True)] + [
            self._copy(b, 1 + j, self.me, (*chip, c), from_input=True) for j, chip in enumerate(self.chips)]

    def start(self):
        for b in range(len(self.ins)):
            self._local(b).start()
            for cp in self._first(b):
                cp.start()

    def forward(self):
        for b in range(len(self.ins)):
            for j, chip in enumerate(self.chips):
                self._copy(b, 1 + j, (*chip, self.c), self.me).wait_recv()
                self._copy(b, 4 + j, (*chip, self.c), self.sibling).start()

    def finish(self):
        for b in range(len(self.ins)):
            self._copy(b, 0, self.sibling, self.me).wait_recv()
            for j, chip in enumerate(self.chips):
                self._copy(b, 4 + j, (*chip, 1 - self.c), self.me).wait_recv()
        for b in range(len(self.ins)):
            for cp in self._first(b):
                cp.wait_send()
            for j, chip in enumerate(self.chips):
                self._copy(b, 4 + j, (*chip, self.c), self.sibling).wait_send()
            self._local(b).wait()


def _gather_rows(ins, outs, send_sems, recv_sems, local_sems, after_issue=None):
    g = _RowGather(ins, outs, send_sems, recv_sems, local_sems)
    g.start()
    if after_issue is not None:
        after_issue()
    g.forward()
    g.finish()


def _gather_rows_shapes(blocks):
    return [SDS((N_DEV * b.shape[0], b.shape[1]), b.dtype) for b in blocks]


def _gather_rows_sems(nb):
    return [pltpu.SemaphoreType.DMA((7 * nb,)), pltpu.SemaphoreType.DMA((7 * nb,)), pltpu.SemaphoreType.DMA((nb,))]


def _place_shard(w_shard, axis, chip_idx, name):
    R, C = w_shard.shape
    rb = _row_block(R)
    nb = R // rb
    full = (R * N_CHIPS, C) if axis == 0 else (R, C * N_CHIPS)
    omap = (lambda i, j: (j[0] * nb + i, 0)) if axis == 0 else (lambda i, j: (i, j[0]))

    def body(j_ref, w_ref, o_ref):
        o_ref[...] = w_ref[...].astype(BF16)

    return pl.pallas_call(
        body,
        grid_spec=pltpu.PrefetchScalarGridSpec(
            num_scalar_prefetch=1, grid=(nb,), in_specs=[pl.BlockSpec((rb, C), lambda i, j: (i, 0))],
            out_specs=pl.BlockSpec((rb, C), omap)),
        out_shape=SDS(full, BF16), compiler_params=_arb(), name=name)(chip_idx, w_shard)


class _WeightGather:
    def __init__(self, refs, axes, send_sems, recv_sems):
        self.refs, self.axes, self.send_sems, self.recv_sems = refs, axes, send_sems, recv_sems
        self.x, self.y, self.c = _position()
        self.j = 2 * self.x + self.y
        self.n = 3 * len(refs)

    def _half(self, w, chip_idx, half):
        ref, axis = self.refs[w], self.axes[w]
        if axis == 0:
            size = ref.shape[0] // N_CHIPS
            return ref.at[pl.ds(chip_idx * size + half * (size // 2), size // 2), :]
        size = ref.shape[1] // N_CHIPS
        rows = ref.shape[0] // 2
        return ref.at[pl.ds(half * rows, rows), pl.ds(chip_idx * size, size)]

    def _ici(self, w, r, chip_idx):
        k = 3 * w + r - 1
        piece = self._half(w, chip_idx, self.c)
        return pltpu.make_async_remote_copy(
            src_ref=piece, dst_ref=piece, send_sem=self.send_sems.at[k], recv_sem=self.recv_sems.at[k],
            device_id=(*_chip_at(self.x, self.y, r), self.c), device_id_type=MESH)

    def _d2d(self, w, r, half):
        k = self.n + 3 * w + r - 1
        piece = self._half(w, self.j ^ r, half)
        return pltpu.make_async_remote_copy(
            src_ref=piece, dst_ref=piece, send_sem=self.send_sems.at[k], recv_sem=self.recv_sems.at[k],
            device_id=(self.x, self.y, 1 - self.c), device_id_type=MESH)

    def _each(self):
        return [(w, r) for w in range(len(self.refs)) for r in (1, 2, 3)]

    def start(self):
        for w, r in self._each():
            self._ici(w, r, self.j).start()

    def forward(self):
        for w, r in self._each():
            self._ici(w, r, self.j ^ r).wait_recv()
            self._d2d(w, r, self.c).start()

    def finish(self):
        for w, r in self._each():
            self._ici(w, r, self.j).wait_send()
            self._d2d(w, r, self.c).wait_send()
            self._d2d(w, r, 1 - self.c).wait_recv()


def _gather_sems(n_weights):
    return [pltpu.SemaphoreType.DMA((6 * n_weights,)), pltpu.SemaphoreType.DMA((6 * n_weights,))]


def _gather_w_in_and_ada(placed, axis, c_block, w_ada):
    n = w_ada.shape[1]

    def body(w_any, c_ref, wada_ref, w_out, call_ref, cact_ref, pall_ref, p_scr, *sems):
        g = _WeightGather([w_out], [axis], *sems[:2])
        _gather_rows([c_ref], [call_ref], *sems[2:5], after_issue=g.start)
        pick = (lax.broadcasted_iota(jnp.int32, (N_DEV, N_DEV * 8), 1)
                == 8 * lax.broadcasted_iota(jnp.int32, (N_DEV, N_DEV * 8), 0)).astype(F32)
        cv = _dot(pick, call_ref[...], precision=HIGHEST)
        ca = cv * _sigmoid(cv)
        cact_ref[...] = ca
        p_scr[...] = _dot(ca, wada_ref[...], precision=HIGHEST)
        products = _RowGather([p_scr], [pall_ref], *sems[5:])
        products.start()
        g.forward()
        products.forward()
        products.finish()
        g.finish()

    anyspec = pl.BlockSpec(memory_space=pl.ANY)
    vmem = pl.BlockSpec(memory_space=pltpu.VMEM)
    rows = N_DEV * c_block.shape[0]
    res = pl.pallas_call(
        body, out_shape=[SDS(placed.shape, placed.dtype), SDS((rows, D), F32), SDS((N_DEV, D), F32), SDS((rows, n), F32)],
        in_specs=[anyspec, vmem, vmem], out_specs=[anyspec, vmem, vmem, vmem],
        scratch_shapes=[pltpu.VMEM((N_DEV, n), F32)] + _gather_sems(1) + _gather_rows_sems(1) + _gather_rows_sems(1),
        input_output_aliases={0: 0}, name="gather_w_in_and_ada")(placed, c_block, w_ada)
    return res[0], res[2], res[3]


class _ChipExchange:
    def __init__(self, ins, outs, send_sems, recv_sems):
        self.ins, self.outs, self.send_sems, self.recv_sems = ins, outs, send_sems, recv_sems
        self.x, self.y, self.c = _position()
        self.j = 2 * self.x + self.y

    def _copies(self):
        for w in range(len(self.ins)):
            for r in (1, 2, 3):
                k = 3 * w + r - 1
                yield pltpu.make_async_remote_copy(
                    src_ref=self.ins[w].at[self.j ^ r], dst_ref=self.outs[w].at[r - 1],
                    send_sem=self.send_sems.at[k], recv_sem=self.recv_sems.at[k],
                    device_id=(*_chip_at(self.x, self.y, r), self.c), device_id_type=MESH)

    def start(self):
        for cp in self._copies():
            cp.start()

    def finish(self):
        for cp in self._copies():
            cp.wait()


def _exchange_sems(n_weights):
    return [pltpu.SemaphoreType.DMA((3 * n_weights,)), pltpu.SemaphoreType.DMA((3 * n_weights,))]


class _CoreExchange:
    def __init__(self, ins, outs, send_sems, recv_sems):
        self.ins, self.outs, self.send_sems, self.recv_sems = ins, outs, send_sems, recv_sems
        self.x, self.y, self.c = _position()

    def _copies(self):
        for w in range(len(self.ins)):
            yield pltpu.make_async_remote_copy(
                src_ref=self.ins[w].at[:, 1 - self.c], dst_ref=self.outs[w],
                send_sem=self.send_sems.at[w], recv_sem=self.recv_sems.at[w],
                device_id=(self.x, self.y, 1 - self.c), device_id_type=MESH)

    def start(self):
        for cp in self._copies():
            cp.start()

    def finish(self):
        for cp in self._copies():
            cp.wait()


def _core_exchange_shapes(grads):
    return [SDS((g.shape[0], g.shape[2], g.shape[3]), g.dtype) for g in grads]


def _core_exchange_sems(n):
    return [pltpu.SemaphoreType.DMA((n,)), pltpu.SemaphoreType.DMA((n,))]


def _exchange_core_halves(grads, name):
    nw = len(grads)

    def body(*refs):
        ex = _CoreExchange(refs[:nw], refs[nw:2 * nw], *refs[2 * nw:])
        ex.start()
        ex.finish()

    anyspec = pl.BlockSpec(memory_space=pl.ANY)
    return pl.pallas_call(
        body, out_shape=_core_exchange_shapes(grads), in_specs=[anyspec] * nw, out_specs=[anyspec] * nw,
        scratch_shapes=_core_exchange_sems(nw), name=name)(*grads)


def _add_core_halves(g4, recv, c_idx, rb, name):
    ns, _, rh, C = g4.shape

    def body(c_ref, g_ref, r_ref, o_ref):
        o_ref[...] = (g_ref[0] + r_ref[...]).astype(BF16)

    return pl.pallas_call(
        body,
        grid_spec=pltpu.PrefetchScalarGridSpec(
            num_scalar_prefetch=1, grid=(ns, rh // rb),
            in_specs=[pl.BlockSpec((1, 1, rb, C), lambda s, i, cr: (s, cr[0], i, 0)),
                      pl.BlockSpec((1, rb, C), lambda s, i, cr: (s, i, 0))],
            out_specs=pl.BlockSpec((1, rb, C), lambda s, i, cr: (s, i, 0))),
        out_shape=SDS((ns, rh, C), BF16), compiler_params=_arb(2), name=name)(c_idx, g4, recv)


def _add_core_halves_in(g4, recv, c_idx, name):
    n_slabs, _, rh, C = g4.shape
    cb = 256
    per_slab, per_chip, n_blocks = C // cb, DIN // N_CHIPS // cb, DIN // cb

    def stored(s, k):
        sb = (per_chip * s + k + 4 * DH // cb) % n_blocks
        return sb // per_slab, sb % per_slab

    def body(c_ref, g_ref, r_ref, o_ref):
        o_ref[...] = (g_ref[0] + r_ref[...].astype(F32)).astype(BF16)

    return pl.pallas_call(
        body,
        grid_spec=pltpu.PrefetchScalarGridSpec(
            num_scalar_prefetch=1, grid=(N_CHIPS, per_chip),
            in_specs=[pl.BlockSpec((1, 1, rh, cb), lambda s, k, cr: (stored(s, k)[0], cr[0], 0, stored(s, k)[1])),
                      pl.BlockSpec((1, rh, cb), lambda s, k, cr: (stored(s, k)[0], 0, stored(s, k)[1]))],
            out_specs=pl.BlockSpec((1, rh, cb), lambda s, k, cr: (s, 0, k))),
        out_shape=SDS((N_CHIPS, rh, DIN // N_CHIPS), BF16), compiler_params=_arb(2), name=name)(c_idx, g4, recv)


def _slot_shapes(sums):
    return [SDS((3,) + s.shape[1:], s.dtype) for s in sums]


def _add_chips(own, slots, order, rb, name):
    _, rh, C = slots.shape

    def body(o_ref, own_ref, a_ref, b_ref, c_ref, d_ref, out_ref):
        mine = own_ref[0].astype(F32)
        t = [jnp.where(o_ref[i] == 0, mine, r[0].astype(F32)) for i, r in enumerate((a_ref, b_ref, c_ref, d_ref))]
        out_ref[...] = ((t[0] + t[1]) + t[2]) + t[3]

    def spec(i):
        return pl.BlockSpec((1, rb, C), lambda t, o: (jnp.maximum(o[i], 1) - 1, t, 0))

    return pl.pallas_call(
        body,
        grid_spec=pltpu.PrefetchScalarGridSpec(
            num_scalar_prefetch=1, grid=(rh // rb,),
            in_specs=[pl.BlockSpec((1, rb, C), lambda t, o: (o[4], t, 0)), spec(0), spec(1), spec(2), spec(3)],
            out_specs=pl.BlockSpec((rb, C), lambda t, o: (t, 0))),
        out_shape=SDS((rh, C), F32), compiler_params=_arb(), name=name)(order, own, slots, slots, slots, slots)


def _share_halves_and_gather(halves, row_blocks):
    nw, nr = len(halves), len(row_blocks)

    def body(*refs):
        ins, outs = refs[:nw], refs[nw + nr:2 * nw + nr]
        sems = refs[2 * (nw + nr):]
        send_sems, recv_sems = sems[:2]
        rows_gather = _RowGather(refs[nw:nw + nr], refs[2 * nw + nr:2 * (nw + nr)], *sems[2:])
        x, y, c = _position()
        rows_gather.start()
        started = []
        for w in range(nw):
            cp = pltpu.make_async_remote_copy(
                src_ref=ins[w], dst_ref=outs[w], send_sem=send_sems.at[w], recv_sem=recv_sems.at[w],
                device_id=(x, y, 1 - c), device_id_type=MESH)
            cp.start()
            started.append(cp)
        rows_gather.forward()
        rows_gather.finish()
        for cp in started:
            cp.wait()

    anyspec = pl.BlockSpec(memory_space=pl.ANY)
    vmem = pl.BlockSpec(memory_space=pltpu.VMEM)
    res = pl.pallas_call(
        body, out_shape=[SDS(h.shape, F32) for h in halves] + _gather_rows_shapes(row_blocks),
        in_specs=[anyspec] * nw + [vmem] * nr, out_specs=[anyspec] * nw + [vmem] * nr,
        scratch_shapes=[pltpu.SemaphoreType.DMA((nw,)), pltpu.SemaphoreType.DMA((nw,))] + _gather_rows_sems(nr),
        name="share_halves_and_gather")(*halves, *row_blocks)
    return res[:nw], res[nw:]


def _small_2d(b_ada, norm1_w, norm2_w, final_norm_w, v_ln_w, v_ln_b, lower_bounds, gn_w, b_s, w_s):
    return dict(zip(SMALL_NAMES, (b_ada, norm1_w, norm2_w, final_norm_w.reshape(1, D), v_ln_w, v_ln_b, lower_bounds, gn_w,
                                  b_s.reshape(NH, BLK), w_s.reshape(NH * BLK, BLK))))


def _small_original_shapes(d):
    out = dict(d)
    out['final_norm_w'] = d['final_norm_w'].reshape(D)
    out['b_s'] = d['b_s'].reshape(1, NH, BLK)
    out['w_s'] = d['w_s'].reshape(1, NH, BLK, BLK)
    return out


def _row_block(r):
    for cand in (256, 176, 128, 64, 32, 16, 8):
        if r % cand == 0:
            return cand
    return r


def kernel(x, c, w_ada, b_ada, norm1_w, w_in, w_s, b_s, v_ln_w, v_ln_b, lower_bounds, gn_w, w_out, norm2_w, w_ffn_in, w_ffn_out, final_norm_w, loss_target, m_w_ada, m_b_ada, m_norm1_w, m_w_in, m_w_s, m_b_s, m_v_ln_w, m_v_ln_b, m_lower_bounds, m_gn_w, m_w_out, m_norm2_w, m_w_ffn_in, m_w_ffn_out, m_final_norm_w, v_w_ada, v_b_ada, v_norm1_w, v_w_in, v_w_s, v_b_s, v_v_ln_w, v_v_ln_b, v_lower_bounds, v_gn_w, v_w_out, v_norm2_w, v_w_ffn_in, v_w_ffn_out, v_final_norm_w):
    T = x.shape[1]
    tm, tp = min(TOKEN_TILE, T), min(PROJ_TILE, T)
    px, py, pc = _position()
    chip = 2 * px + py
    me = 4 * px + 2 * py + pc
    x2d = x.reshape(T, D)
    tgt = loss_target.reshape(T, D)

    chip_idx = jnp.reshape(chip, (1,)).astype(jnp.int32)
    c_idx = jnp.reshape(pc, (1,)).astype(jnp.int32)
    w_in_b, cact, ada_all = _gather_w_in_and_ada(
        _place_shard(w_in[0], 1, chip_idx, "place_in"), 1, jnp.broadcast_to(c, (8, D)), w_ada[0])
    placed = [_place_shard(w_out[0], 0, chip_idx, "place_out"), _place_shard(w_ffn_in[0], 1, chip_idx, "place_ffn_in"),
              _place_shard(w_ffn_out[0], 0, chip_idx, "place_ffn_out")]

    n_ada = ada_all.shape[1]
    ada_all = ada_all.reshape(N_CHIPS, 2, N_DEV, n_ada)[:, 0]
    ada = lax.dynamic_index_in_dim(ada_all, me, axis=1, keepdims=False).reshape(1, 6 * D) + b_ada

    rr = lax.broadcasted_iota(jnp.int32, (BLK, BLK), 0) // CH
    cc = lax.broadcasted_iota(jnp.int32, (BLK, BLK), 1) // CH
    ws_b = jnp.where((rr >= cc)[None], w_s[0], 0.0).astype(BF16)
    bst = b_s[0].T
    lnw, lnb = v_ln_w, v_ln_b
    nw1, nw2, fw = norm1_w, norm2_w, final_norm_w.reshape(1, D)

    tables = _hgrn_tables()
    (h1, proj, ycat, o_pre, a_all, st_all), (w_out_b, w_fi_b, w_fo_b) = _proj_hgrn_fwd(
        x2d, nw1, ada, w_in_b, lower_bounds, gn_w, tables, placed, [0, 1, 0])

    dycat, dx1, h2, act, dff, dgu, dmix, acc2, ycat = _token_local(
        x2d, ycat, tgt, ada, nw2, ada, ada, ada, fw, w_out_b, w_fi_b, w_fo_b, proj, ws_b, bst, lnw, lnb, tm)

    tt, tt_sq = min(WGRAD_TOKENS, T), min(WGRAD_TOKENS_SQUARE, T)
    order = jnp.concatenate([chip ^ jnp.arange(N_CHIPS, dtype=jnp.int32), chip_idx]).astype(jnp.int32)

    def by_core_half(g):
        return g.reshape(g.shape[0], 2, g.shape[1] // 2, g.shape[2])

    def core_sums(g4, recv, names):
        return [_add_core_halves(a, b, c_idx, _row_block(a.shape[2]), "add_core_" + n) for a, b, n in zip(g4, recv, names)]

    def chip_sums(sums, slots, names):
        return [_add_chips(o, s, order, _row_block(s.shape[1]), "add_chips_" + n) for o, s, n in zip(sums, slots, names)]

    g_out = _wgrad(ycat, dmix, D, D, tt, "wgrad_out").reshape(N_CHIPS, D // N_CHIPS, D)
    g_fi = _wgrad(h2, dgu, D, FFB, tt, "wgrad_ffn_in")
    g_fo = _wgrad(act, dff, FFB, D, tt, "wgrad_ffn_out").reshape(N_CHIPS, DFF // N_CHIPS, D)
    late_names = ["out", "ffn_in", "ffn_out"]
    late_g4 = [by_core_half(g) for g in (g_out, g_fi, g_fo)]

    (dproj, dws, dbs, dln), late_recv = _gmlp_bwd(proj, dycat, ws_b, bst, lnw, lnb, late_g4)
    late_sums = core_sums(late_g4, late_recv, late_names)
    (dproj, dlb, dgn), late_slots, (acc2_all, dln_all, dbs_all, dws_all) = _hgrn_bwd(
        proj, o_pre, a_all, st_all, dycat, lower_bounds, gn_w, dproj, tables, late_sums, [acc2, dln, dbs, dws])

    g_in, g_in_wire = _wgrad(h1, dproj, D, D, tt_sq, "wgrad_in", bf16_copy=True)
    (in_recv,) = _exchange_core_halves([by_core_half(g_in_wire)], "exchange_core_halves_in")
    in_sums = [_add_core_halves_in(by_core_half(g_in), in_recv, c_idx, "add_core_in")]
    (grad_x, acc1), in_slots = _proj_in_bwd(dproj, x2d, dx1, nw1, ada, w_in_b, tp, in_sums)
    names = ["in"] + late_names
    halves = chip_sums(in_sums, in_slots, ["in"]) + chip_sums(late_sums, late_slots, late_names)
    sibling_halves, (acc1_all, dlb_all, dgn_all) = _share_halves_and_gather(halves, [acc1, dlb, dgn])

    big_w = [(w_in, m_w_in, v_w_in), (w_out, m_w_out, v_w_out), (w_ffn_in, m_w_ffn_in, v_w_ffn_in),
             (w_ffn_out, m_w_ffn_out, v_w_ffn_out)]
    big_out = []
    for mine, sib, (w, m, v), n in zip(halves, sibling_halves, big_w, names):
        res = _adamw_halves(w[0], mine, sib, m[0], v[0], c_idx, _row_block(mine.shape[0]), "adamw_" + n)
        big_out.append([r[None] for r in res])

    gathered = [acc1_all, acc2_all, dln_all, dlb_all, dgn_all, dbs_all, dws_all]
    small, loss, dada_all = _small_finalize(
        gathered,
        _small_2d(b_ada, norm1_w, norm2_w, final_norm_w, v_ln_w, v_ln_b, lower_bounds, gn_w, b_s, w_s),
        _small_2d(m_b_ada, m_norm1_w, m_norm2_w, m_final_norm_w, m_v_ln_w, m_v_ln_b, m_lower_bounds, m_gn_w, m_b_s, m_w_s),
        _small_2d(v_b_ada, v_norm1_w, v_norm2_w, v_final_norm_w, v_v_ln_w, v_v_ln_b, v_lower_bounds, v_gn_w, v_b_s, v_w_s))
    small = [_small_original_shapes(d) for d in small]
    loss = loss.reshape(())

    ada_out = [o[None] for o in _ada_wgrad_adam(cact.T, dada_all, w_ada[0], m_w_ada[0], v_w_ada[0], chip_idx)]

    order_names = ['w_ada', 'b_ada', 'norm1_w', 'w_in', 'w_s', 'b_s', 'v_ln_w', 'v_ln_b', 'lower_bounds', 'gn_w',
                   'w_out', 'norm2_w', 'w_ffn_in', 'w_ffn_out', 'final_norm_w']
    big_idx = {'w_in': 0, 'w_out': 1, 'w_ffn_in': 2, 'w_ffn_out': 3}
    outs = [loss, grad_x.reshape(1, T, D)]
    for kind in range(4):
        for n in order_names:
            if n == 'w_ada':
                outs.append(ada_out[kind])
            elif n in big_idx:
                outs.append(big_out[big_idx[n]][kind])
            else:
                outs.append(small[kind][n])
    return tuple(outs)
```
